```python
import math
import jax, jax.numpy as jnp
from jax import lax
import numpy as np

D_MODEL = 2048
BATCH = 8
SEQ = 4096
DEPTH = 1

D_INNER_A = 3 * D_MODEL // 2
HEAD_DIM_A = 64
N_HEADS_A = D_INNER_A // HEAD_DIM_A
N_GROUPS_A = 8
HEADS_PER_GROUP = N_HEADS_A // N_GROUPS_A
D_STATE_A = 128
CONV_A = 4
CHUNK = 256
CONV_DIM_A = D_INNER_A + 2 * N_GROUPS_A * D_STATE_A
D_S5 = D_MODEL // 2
S5_GROUP = 16
N_GROUPS_S5 = D_S5 // S5_GROUP
S5_STATE = 64
D_FF = 256 * ((8 * D_MODEL // 3 + 255) // 256)
CONV_FFN = 3
EPS = 1e-6
DT_MIN = 1e-3
DT_MAX = 1e-1
EIG_MAX = -1e-4
IN_COLS = D_INNER_A + CONV_DIM_A + N_HEADS_A + D_S5 + 2 * D_MODEL

kernel_name = "hybrid_ssd_s5_gated_convffn"


def rmsnorm(x, w):
    xf = x.astype(jnp.float32)
    xf = xf * lax.rsqrt(jnp.mean(xf * xf, axis=-1, keepdims=True) + EPS)
    return xf.astype(x.dtype) * w


def causal_dwconv(x, w, b):
    k = w.shape[0]
    y = lax.conv_general_dilated(
        x, w[:, None, :].astype(x.dtype), window_strides=(1,), padding=[(k - 1, 0)],
        dimension_numbers=("NWC", "WIO", "NWC"), feature_group_count=x.shape[-1])
    return y + b


def segsum(a):
    t = a.shape[-1]
    ar = jnp.broadcast_to(a[..., :, None], a.shape + (t,))
    strict = jnp.tril(jnp.ones((t, t), dtype=bool), k=-1)
    cs = jnp.cumsum(jnp.where(strict, ar, 0.0), axis=-2)
    incl = jnp.tril(jnp.ones((t, t), dtype=bool), k=0)
    return jnp.where(incl, cs, -jnp.inf)


def ssd_chunked(xh, da, bm, cm):
    b, seqlen = xh.shape[:2]
    nc = -(-seqlen // CHUNK)
    pad = nc * CHUNK - seqlen
    if pad:
        xh = jnp.pad(xh, ((0, 0), (0, pad), (0, 0), (0, 0), (0, 0)))
        da = jnp.pad(da, ((0, 0), (0, pad), (0, 0), (0, 0)))
        bm = jnp.pad(bm, ((0, 0), (0, pad), (0, 0), (0, 0)))
        cm = jnp.pad(cm, ((0, 0), (0, pad), (0, 0), (0, 0)))
    X = xh.reshape(b, nc, CHUNK, N_GROUPS_A, HEADS_PER_GROUP, HEAD_DIM_A)
    A = da.reshape(b, nc, CHUNK, N_GROUPS_A, HEADS_PER_GROUP).transpose(0, 3, 4, 1, 2)
    Bc = bm.reshape(b, nc, CHUNK, N_GROUPS_A, D_STATE_A)
    Cc = cm.reshape(b, nc, CHUNK, N_GROUPS_A, D_STATE_A)
    a_cs = jnp.cumsum(A, axis=-1)
    decay_in = jnp.exp(segsum(A))
    cb = jnp.einsum("bclgn,bcsgn->bgcls", Cc, Bc)
    y_diag = jnp.einsum("bgcls,bgrcls,bcsgrp->bclgrp", cb, decay_in, X)
    decay_states = jnp.exp(a_cs[..., -1:] - a_cs)
    states = jnp.einsum("bclgn,bgrcl,bclgrp->bcgrpn", Bc, decay_states, X)
    states = jnp.concatenate([jnp.zeros_like(states[:, :1]), states], axis=1)
    chunk_decay = jnp.exp(segsum(jnp.pad(a_cs[..., -1], ((0, 0), (0, 0), (0, 0), (1, 0)))))
    new_states = jnp.einsum("bgrzc,bcgrpn->bzgrpn", chunk_decay, states)
    states = new_states[:, :-1]
    y_off = jnp.einsum("bclgn,bcgrpn,bgrcl->bclgrp", Cc, states, jnp.exp(a_cs))
    y = (y_diag + y_off).reshape(b, nc * CHUNK, N_GROUPS_A, HEADS_PER_GROUP, HEAD_DIM_A)
    return y[:, :seqlen]


def mamba2_mixer(z, xbc, dt_raw, conv_w, conv_b, dt_bias, a_log, d_skip, norm_w, w_proj):
    b, seqlen, _ = z.shape
    f32 = jnp.float32
    xbc = jax.nn.silu(causal_dwconv(xbc, conv_w, conv_b))
    gn = N_GROUPS_A * D_STATE_A
    xs, bm, cm = jnp.split(xbc, [D_INNER_A, D_INNER_A + gn], axis=-1)
    dt = jax.nn.softplus(dt_raw.astype(f32) + dt_bias.astype(f32))
    a = -jnp.exp(a_log.astype(f32))
    dt_g = dt.reshape(b, seqlen, N_GROUPS_A, HEADS_PER_GROUP)
    xh = xs.astype(f32).reshape(b, seqlen, N_GROUPS_A, HEADS_PER_GROUP, HEAD_DIM_A)
    y = ssd_chunked(xh * dt_g[..., None],
                    dt_g * a.reshape(N_GROUPS_A, HEADS_PER_GROUP),
                    bm.astype(f32).reshape(b, seqlen, N_GROUPS_A, D_STATE_A),
                    cm.astype(f32).reshape(b, seqlen, N_GROUPS_A, D_STATE_A))
    y = y + d_skip.astype(f32).reshape(N_GROUPS_A, HEADS_PER_GROUP)[:, :, None] * xh
    y = y.reshape(b, seqlen, D_INNER_A) * jax.nn.silu(z.astype(f32))
    yg = y.reshape(b, seqlen, N_GROUPS_A, D_INNER_A // N_GROUPS_A)
    yg = yg * lax.rsqrt(jnp.mean(yg * yg, axis=-1, keepdims=True) + EPS)
    y = yg.reshape(b, seqlen, D_INNER_A).astype(z.dtype) * norm_w
    return y @ w_proj


def _lin_rec(e1, e2):
    a1, b1 = e1
    a2, b2 = e2
    return a1 * a2, a2 * b1 + b2


def s5_mixer(u, lam_re, lam_im, log_dt, b_re, b_im, c_re, c_im, d_skip, w_glu):
    b, seqlen, _ = u.shape
    f32 = jnp.float32
    ug = u.astype(f32).reshape(b, seqlen, N_GROUPS_S5, S5_GROUP)
    lam = lax.complex(jnp.minimum(lam_re.astype(f32), EIG_MAX), lam_im.astype(f32))
    dt = jnp.exp(log_dt.astype(f32))[:, None]
    lam_bar = jnp.exp(lam * dt)
    bmat = lax.complex(b_re.astype(f32), b_im.astype(f32))
    cmat = lax.complex(c_re.astype(f32), c_im.astype(f32))
    b_bar = ((lam_bar - 1.0) / lam)[..., None] * bmat
    bu = jnp.einsum("gpc,blgc->blgp", b_bar, ug.astype(jnp.complex64))
    a_el = jnp.broadcast_to(lam_bar, bu.shape)
    _, states = lax.associative_scan(_lin_rec, (a_el, bu), axis=1)
    y = jnp.einsum("gcp,blgp->blgc", cmat, states).real + d_skip.astype(f32).reshape(N_GROUPS_S5, S5_GROUP) * ug
    y = jax.nn.gelu(y.reshape(b, seqlen, D_S5)).astype(u.dtype)
    val, gate = jnp.split(y @ w_glu, 2, axis=-1)
    return val * jax.nn.sigmoid(gate)


def conv_glu_ffn(hn, w_up, conv_w, conv_b, w_down):
    up = causal_dwconv(hn @ w_up, conv_w, conv_b)
    gate, val = jnp.split(up, 2, axis=-1)
    return (jax.nn.silu(gate) * val) @ w_down


def _fwd_setup_inputs(seed: int = 0) -> dict:
    key = jax.random.key(seed)
    ks = jax.random.split(key, 28)
    f32 = jnp.float32
    L = DEPTH

    def nrm(k, shape, scale):
        return jax.random.normal(k, shape, f32) * scale

    log_lo, log_hi = math.log(DT_MIN), math.log(DT_MAX)
    dt0 = jnp.exp(jax.random.uniform(ks[6], (L, N_HEADS_A), f32, log_lo, log_hi))
    dt_bias = dt0 + jnp.log(-jnp.expm1(-dt0))
    lam_im0 = math.pi * jnp.arange(S5_STATE, dtype=f32)
    return {
        "x": nrm(ks[0], (BATCH, SEQ, D_MODEL), 1.0),
        "norm_mix_w": 1.0 + nrm(ks[1], (L, D_MODEL), 0.02),
        "w_in": nrm(ks[2], (L, D_MODEL, IN_COLS), D_MODEL ** -0.5),
        "conv_a_w": nrm(ks[3], (L, CONV_A, CONV_DIM_A), CONV_A ** -0.5),
        "conv_a_b": nrm(ks[4], (L, CONV_DIM_A), 0.01),
        "dt_bias": dt_bias,
        "a_log": jnp.log(jax.random.uniform(ks[7], (L, N_HEADS_A), f32, 1.0, 16.0)),
        "d_a": 1.0 + nrm(ks[8], (L, N_HEADS_A), 0.02),
        "norm_a_w": 1.0 + nrm(ks[9], (L, D_INNER_A), 0.02),
        "w_proj_a": nrm(ks[10], (L, D_INNER_A, D_MODEL), D_INNER_A ** -0.5),
        "s5_lam_re": -0.5 + nrm(ks[11], (L, N_GROUPS_S5, S5_STATE), 0.01),
        "s5_lam_im": lam_im0 + nrm(ks[12], (L, N_GROUPS_S5, S5_STATE), 0.01),
        "s5_log_dt": jax.random.uniform(ks[13], (L, N_GROUPS_S5), f32, log_lo, log_hi),
        "s5_b_re": nrm(ks[14], (L, N_GROUPS_S5, S5_STATE, S5_GROUP), (2 * S5_GROUP) ** -0.5),
        "s5_b_im": nrm(ks[15], (L, N_GROUPS_S5, S5_STATE, S5_GROUP), (2 * S5_GROUP) ** -0.5),
        "s5_c_re": nrm(ks[16], (L, N_GROUPS_S5, S5_GROUP, S5_STATE), S5_STATE ** -0.5),
        "s5_c_im": nrm(ks[17], (L, N_GROUPS_S5, S5_GROUP, S5_STATE), S5_STATE ** -0.5),
        "s5_d": nrm(ks[18], (L, D_S5), 1.0),
        "w_s5_glu": nrm(ks[19], (L, D_S5, 2 * D_MODEL), D_S5 ** -0.5),
        "w_out": nrm(ks[20], (L, D_MODEL, D_MODEL), D_MODEL ** -0.5),
        "norm_ffn_w": 1.0 + nrm(ks[21], (L, D_MODEL), 0.02),
        "w_up": nrm(ks[22], (L, D_MODEL, 2 * D_FF), D_MODEL ** -0.5),
        "conv_ffn_w": nrm(ks[23], (L, CONV_FFN, 2 * D_FF), CONV_FFN ** -0.5),
        "conv_ffn_b": nrm(ks[24], (L, 2 * D_FF), 0.01),
        "w_down": nrm(ks[25], (L, D_FF, D_MODEL), D_FF ** -0.5),
        "norm_final_w": 1.0 + nrm(ks[26], (D_MODEL,), 0.02),
    }


def _fwd_reference(x, norm_mix_w, w_in, conv_a_w, conv_a_b, dt_bias, a_log, d_a, norm_a_w, w_proj_a,
              s5_lam_re, s5_lam_im, s5_log_dt, s5_b_re, s5_b_im, s5_c_re, s5_c_im, s5_d, w_s5_glu,
              w_out, norm_ffn_w, w_up, conv_ffn_w, conv_ffn_b, w_down, norm_final_w):
    sizes = [D_INNER_A, CONV_DIM_A, N_HEADS_A, D_S5, D_MODEL, D_MODEL]
    splits = [int(s) for s in np.cumsum(sizes)[:-1]]
    h = x
    for i in range(DEPTH):
        hn = rmsnorm(h, norm_mix_w[i])
        z, xbc, dt_raw, u, g_a, g_b = jnp.split(hn @ w_in[i], splits, axis=-1)
        y_a = mamba2_mixer(z, xbc, dt_raw, conv_a_w[i], conv_a_b[i], dt_bias[i], a_log[i], d_a[i],
                           norm_a_w[i], w_proj_a[i])
        y_b = s5_mixer(u, s5_lam_re[i], s5_lam_im[i], s5_log_dt[i], s5_b_re[i], s5_b_im[i],
                       s5_c_re[i], s5_c_im[i], s5_d[i], w_s5_glu[i])
        merged = jax.nn.sigmoid(g_a) * y_a + jax.nn.sigmoid(g_b) * y_b
        h = h + merged @ w_out[i]
        hn = rmsnorm(h, norm_ffn_w[i])
        h = h + conv_glu_ffn(hn, w_up[i], conv_ffn_w[i], conv_ffn_b[i], w_down[i])
    return rmsnorm(h, norm_final_w)


import jax as _jax
import jax.numpy as _jnp

TWIN_FORMAT = 'train_step'
FWD_PARAMS = ['x', 'norm_mix_w', 'w_in', 'conv_a_w', 'conv_a_b', 'dt_bias', 'a_log', 'd_a', 'norm_a_w', 'w_proj_a', 's5_lam_re', 's5_lam_im', 's5_log_dt', 's5_b_re', 's5_b_im', 's5_c_re', 's5_c_im', 's5_d', 'w_s5_glu', 'w_out', 'norm_ffn_w', 'w_up', 'conv_ffn_w', 'conv_ffn_b', 'w_down', 'norm_final_w']
TWIN_WEIGHTS = ['norm_mix_w', 'w_in', 'conv_a_w', 'conv_a_b', 'dt_bias', 'a_log', 'd_a', 'norm_a_w', 'w_proj_a', 's5_lam_re', 's5_lam_im', 's5_log_dt', 's5_b_re', 's5_b_im', 's5_c_re', 's5_c_im', 's5_d', 'w_s5_glu', 'w_out', 'norm_ffn_w', 'w_up', 'conv_ffn_w', 'conv_ffn_b', 'w_down', 'norm_final_w']
TWIN_DIFF_INPUT = 'x'
TWIN_INPUTS = ['x', 'norm_mix_w', 'w_in', 'conv_a_w', 'conv_a_b', 'dt_bias', 'a_log', 'd_a', 'norm_a_w', 'w_proj_a', 's5_lam_re', 's5_lam_im', 's5_log_dt', 's5_b_re', 's5_b_im', 's5_c_re', 's5_c_im', 's5_d', 'w_s5_glu', 'w_out', 'norm_ffn_w', 'w_up', 'conv_ffn_w', 'conv_ffn_b', 'w_down', 'norm_final_w', 'loss_target', 'm_norm_mix_w', 'm_w_in', 'm_conv_a_w', 'm_conv_a_b', 'm_dt_bias', 'm_a_log', 'm_d_a', 'm_norm_a_w', 'm_w_proj_a', 'm_s5_lam_re', 'm_s5_lam_im', 'm_s5_log_dt', 'm_s5_b_re', 'm_s5_b_im', 'm_s5_c_re', 'm_s5_c_im', 'm_s5_d', 'm_w_s5_glu', 'm_w_out', 'm_norm_ffn_w', 'm_w_up', 'm_conv_ffn_w', 'm_conv_ffn_b', 'm_w_down', 'm_norm_final_w', 'v_norm_mix_w', 'v_w_in', 'v_conv_a_w', 'v_conv_a_b', 'v_dt_bias', 'v_a_log', 'v_d_a', 'v_norm_a_w', 'v_w_proj_a', 'v_s5_lam_re', 'v_s5_lam_im', 'v_s5_log_dt', 'v_s5_b_re', 'v_s5_b_im', 'v_s5_c_re', 'v_s5_c_im', 'v_s5_d', 'v_w_s5_glu', 'v_w_out', 'v_norm_ffn_w', 'v_w_up', 'v_conv_ffn_w', 'v_conv_ffn_b', 'v_w_down', 'v_norm_final_w']
TWIN_OUTPUTS = ['loss', 'grad_x', 'grad_norm_mix_w', 'grad_w_in', 'grad_conv_a_w', 'grad_conv_a_b', 'grad_dt_bias', 'grad_a_log', 'grad_d_a', 'grad_norm_a_w', 'grad_w_proj_a', 'grad_s5_lam_re', 'grad_s5_lam_im', 'grad_s5_log_dt', 'grad_s5_b_re', 'grad_s5_b_im', 'grad_s5_c_re', 'grad_s5_c_im', 'grad_s5_d', 'grad_w_s5_glu', 'grad_w_out', 'grad_norm_ffn_w', 'grad_w_up', 'grad_conv_ffn_w', 'grad_conv_ffn_b', 'grad_w_down', 'grad_norm_final_w', 'delta_norm_mix_w', 'delta_w_in', 'delta_conv_a_w', 'delta_conv_a_b', 'delta_dt_bias', 'delta_a_log', 'delta_d_a', 'delta_norm_a_w', 'delta_w_proj_a', 'delta_s5_lam_re', 'delta_s5_lam_im', 'delta_s5_log_dt', 'delta_s5_b_re', 'delta_s5_b_im', 'delta_s5_c_re', 'delta_s5_c_im', 'delta_s5_d', 'delta_w_s5_glu', 'delta_w_out', 'delta_norm_ffn_w', 'delta_w_up', 'delta_conv_ffn_w', 'delta_conv_ffn_b', 'delta_w_down', 'delta_norm_final_w', 'new_m_norm_mix_w', 'new_m_w_in', 'new_m_conv_a_w', 'new_m_conv_a_b', 'new_m_dt_bias', 'new_m_a_log', 'new_m_d_a', 'new_m_norm_a_w', 'new_m_w_proj_a', 'new_m_s5_lam_re', 'new_m_s5_lam_im', 'new_m_s5_log_dt', 'new_m_s5_b_re', 'new_m_s5_b_im', 'new_m_s5_c_re', 'new_m_s5_c_im', 'new_m_s5_d', 'new_m_w_s5_glu', 'new_m_w_out', 'new_m_norm_ffn_w', 'new_m_w_up', 'new_m_conv_ffn_w', 'new_m_conv_ffn_b', 'new_m_w_down', 'new_m_norm_final_w', 'new_v_norm_mix_w', 'new_v_w_in', 'new_v_conv_a_w', 'new_v_conv_a_b', 'new_v_dt_bias', 'new_v_a_log', 'new_v_d_a', 'new_v_norm_a_w', 'new_v_w_proj_a', 'new_v_s5_lam_re', 'new_v_s5_lam_im', 'new_v_s5_log_dt', 'new_v_s5_b_re', 'new_v_s5_b_im', 'new_v_s5_c_re', 'new_v_s5_c_im', 'new_v_s5_d', 'new_v_w_s5_glu', 'new_v_w_out', 'new_v_norm_ffn_w', 'new_v_w_up', 'new_v_conv_ffn_w', 'new_v_conv_ffn_b', 'new_v_w_down', 'new_v_norm_final_w']
TWIN_LEAF_KINDS = {'loss': 'loss', 'grad_x': 'grad_x', 'grad_norm_mix_w': 'grad_w', 'grad_w_in': 'grad_w', 'grad_conv_a_w': 'grad_w', 'grad_conv_a_b': 'grad_w', 'grad_dt_bias': 'grad_w', 'grad_a_log': 'grad_w', 'grad_d_a': 'grad_w', 'grad_norm_a_w': 'grad_w', 'grad_w_proj_a': 'grad_w', 'grad_s5_lam_re': 'grad_w', 'grad_s5_lam_im': 'grad_w', 'grad_s5_log_dt': 'grad_w', 'grad_s5_b_re': 'grad_w', 'grad_s5_b_im': 'grad_w', 'grad_s5_c_re': 'grad_w', 'grad_s5_c_im': 'grad_w', 'grad_s5_d': 'grad_w', 'grad_w_s5_glu': 'grad_w', 'grad_w_out': 'grad_w', 'grad_norm_ffn_w': 'grad_w', 'grad_w_up': 'grad_w', 'grad_conv_ffn_w': 'grad_w', 'grad_conv_ffn_b': 'grad_w', 'grad_w_down': 'grad_w', 'grad_norm_final_w': 'grad_w', 'delta_norm_mix_w': 'delta_w', 'delta_w_in': 'delta_w', 'delta_conv_a_w': 'delta_w', 'delta_conv_a_b': 'delta_w', 'delta_dt_bias': 'delta_w', 'delta_a_log': 'delta_w', 'delta_d_a': 'delta_w', 'delta_norm_a_w': 'delta_w', 'delta_w_proj_a': 'delta_w', 'delta_s5_lam_re': 'delta_w', 'delta_s5_lam_im': 'delta_w', 'delta_s5_log_dt': 'delta_w', 'delta_s5_b_re': 'delta_w', 'delta_s5_b_im': 'delta_w', 'delta_s5_c_re': 'delta_w', 'delta_s5_c_im': 'delta_w', 'delta_s5_d': 'delta_w', 'delta_w_s5_glu': 'delta_w', 'delta_w_out': 'delta_w', 'delta_norm_ffn_w': 'delta_w', 'delta_w_up': 'delta_w', 'delta_conv_ffn_w': 'delta_w', 'delta_conv_ffn_b': 'delta_w', 'delta_w_down': 'delta_w', 'delta_norm_final_w': 'delta_w', 'new_m_norm_mix_w': 'new_m', 'new_m_w_in': 'new_m', 'new_m_conv_a_w': 'new_m', 'new_m_conv_a_b': 'new_m', 'new_m_dt_bias': 'new_m', 'new_m_a_log': 'new_m', 'new_m_d_a': 'new_m', 'new_m_norm_a_w': 'new_m', 'new_m_w_proj_a': 'new_m', 'new_m_s5_lam_re': 'new_m', 'new_m_s5_lam_im': 'new_m', 'new_m_s5_log_dt': 'new_m', 'new_m_s5_b_re': 'new_m', 'new_m_s5_b_im': 'new_m', 'new_m_s5_c_re': 'new_m', 'new_m_s5_c_im': 'new_m', 'new_m_s5_d': 'new_m', 'new_m_w_s5_glu': 'new_m', 'new_m_w_out': 'new_m', 'new_m_norm_ffn_w': 'new_m', 'new_m_w_up': 'new_m', 'new_m_conv_ffn_w': 'new_m', 'new_m_conv_ffn_b': 'new_m', 'new_m_w_down': 'new_m', 'new_m_norm_final_w': 'new_m', 'new_v_norm_mix_w': 'new_v', 'new_v_w_in': 'new_v', 'new_v_conv_a_w': 'new_v', 'new_v_conv_a_b': 'new_v', 'new_v_dt_bias': 'new_v', 'new_v_a_log': 'new_v', 'new_v_d_a': 'new_v', 'new_v_norm_a_w': 'new_v', 'new_v_w_proj_a': 'new_v', 'new_v_s5_lam_re': 'new_v', 'new_v_s5_lam_im': 'new_v', 'new_v_s5_log_dt': 'new_v', 'new_v_s5_b_re': 'new_v', 'new_v_s5_b_im': 'new_v', 'new_v_s5_c_re': 'new_v', 'new_v_s5_c_im': 'new_v', 'new_v_s5_d': 'new_v', 'new_v_w_s5_glu': 'new_v', 'new_v_w_out': 'new_v', 'new_v_norm_ffn_w': 'new_v', 'new_v_w_up': 'new_v', 'new_v_conv_ffn_w': 'new_v', 'new_v_conv_ffn_b': 'new_v', 'new_v_w_down': 'new_v', 'new_v_norm_final_w': 'new_v'}


def _forward(args):
    return _fwd_reference(*[args[k] for k in FWD_PARAMS])


def _output_shape():
    def fwd():
        inp = _fwd_setup_inputs(0)
        return _fwd_reference(*[inp[k] for k in FWD_PARAMS])
    out = _jax.eval_shape(fwd)
    return out.shape, out.dtype

N_MICROBATCH = 1
ADAM_LR = 0.001
ADAM_B1 = 0.9
ADAM_B2 = 0.999
ADAM_EPS = 1e-08
ADAM_WD = 0.01
ADAM_STEP = 10
PER_EXAMPLE_BATCH_AXIS = {'x': 0, 'loss_target': 0}
SHARED_INPUTS = []
_WEIGHT_DTYPES = {'norm_mix_w': _jnp.float32, 'w_in': _jnp.float32, 'conv_a_w': _jnp.float32, 'conv_a_b': _jnp.float32, 'dt_bias': _jnp.float32, 'a_log': _jnp.float32, 'd_a': _jnp.float32, 'norm_a_w': _jnp.float32, 'w_proj_a': _jnp.float32, 's5_lam_re': _jnp.float32, 's5_lam_im': _jnp.float32, 's5_log_dt': _jnp.float32, 's5_b_re': _jnp.float32, 's5_b_im': _jnp.float32, 's5_c_re': _jnp.float32, 's5_c_im': _jnp.float32, 's5_d': _jnp.float32, 'w_s5_glu': _jnp.float32, 'w_out': _jnp.float32, 'norm_ffn_w': _jnp.float32, 'w_up': _jnp.float32, 'conv_ffn_w': _jnp.float32, 'conv_ffn_b': _jnp.float32, 'w_down': _jnp.float32, 'norm_final_w': _jnp.float32}
MOMENT_SCALE = {'norm_mix_w': 7.537329e-02, 'w_in': 2.906691e-02, 'conv_a_w': 3.174459e-02, 'conv_a_b': 4.359518e-02, 'dt_bias': 8.132297e-02, 'a_log': 9.780848e-02, 'd_a': 1.939308e-01, 'norm_a_w': 3.859062e-02, 'w_proj_a': 4.644204e-02, 's5_lam_re': 1.769303e-03, 's5_lam_im': 1.805441e-03, 's5_log_dt': 1.218291e+00, 's5_b_re': 1.166789e-03, 's5_b_im': 1.142427e-03, 's5_c_re': 1.644470e-03, 's5_c_im': 1.611235e-03, 's5_d': 2.409343e-02, 'w_s5_glu': 1.214887e-02, 'w_out': 4.928101e-02, 'norm_ffn_w': 6.109304e-02, 'w_up': 2.560116e-02, 'conv_ffn_w': 2.561757e-02, 'conv_ffn_b': 2.573333e-02, 'w_down': 4.178833e-02, 'norm_final_w': 1.600315e+01}


def _to_microbatches(a, axis):
    t = _jnp.moveaxis(a, axis, 0)
    t = t.reshape((N_MICROBATCH, t.shape[0] // N_MICROBATCH) + t.shape[1:])
    return _jnp.moveaxis(t, 1, axis + 1)


def setup_inputs(seed: int = 0) -> dict:
    inp = _fwd_setup_inputs(seed)
    key = _jax.random.fold_in(_jax.random.key(seed), 7919)
    shape, _ = _output_shape()
    out = dict(inp)
    out["loss_target"] = _jax.random.normal(_jax.random.fold_in(key, 0), shape, _jnp.float32)
    for i, name in enumerate(TWIN_WEIGHTS):
        w = inp[name].astype(_jnp.float32)
        if MOMENT_SCALE is None:
            s = _jnp.sqrt(_jnp.mean(_jnp.square(w)) + 1e-30)
        else:
            s = MOMENT_SCALE[name]
        km, kv = _jax.random.split(_jax.random.fold_in(key, i + 1))
        out[name] = w
        out["m_" + name] = s * _jax.random.normal(km, w.shape, _jnp.float32)
        out["v_" + name] = (s * s) * _jax.random.uniform(kv, w.shape, _jnp.float32, 0.5, 1.5)
    if N_MICROBATCH > 1:
        for name, axis in PER_EXAMPLE_BATCH_AXIS.items():
            out[name] = _to_microbatches(out[name], axis)
    return {'x': out['x'], 'norm_mix_w': out['norm_mix_w'], 'w_in': out['w_in'], 'conv_a_w': out['conv_a_w'], 'conv_a_b': out['conv_a_b'], 'dt_bias': out['dt_bias'], 'a_log': out['a_log'], 'd_a': out['d_a'], 'norm_a_w': out['norm_a_w'], 'w_proj_a': out['w_proj_a'], 's5_lam_re': out['s5_lam_re'], 's5_lam_im': out['s5_lam_im'], 's5_log_dt': out['s5_log_dt'], 's5_b_re': out['s5_b_re'], 's5_b_im': out['s5_b_im'], 's5_c_re': out['s5_c_re'], 's5_c_im': out['s5_c_im'], 's5_d': out['s5_d'], 'w_s5_glu': out['w_s5_glu'], 'w_out': out['w_out'], 'norm_ffn_w': out['norm_ffn_w'], 'w_up': out['w_up'], 'conv_ffn_w': out['conv_ffn_w'], 'conv_ffn_b': out['conv_ffn_b'], 'w_down': out['w_down'], 'norm_final_w': out['norm_final_w'], 'loss_target': out['loss_target'], 'm_norm_mix_w': out['m_norm_mix_w'], 'm_w_in': out['m_w_in'], 'm_conv_a_w': out['m_conv_a_w'], 'm_conv_a_b': out['m_conv_a_b'], 'm_dt_bias': out['m_dt_bias'], 'm_a_log': out['m_a_log'], 'm_d_a': out['m_d_a'], 'm_norm_a_w': out['m_norm_a_w'], 'm_w_proj_a': out['m_w_proj_a'], 'm_s5_lam_re': out['m_s5_lam_re'], 'm_s5_lam_im': out['m_s5_lam_im'], 'm_s5_log_dt': out['m_s5_log_dt'], 'm_s5_b_re': out['m_s5_b_re'], 'm_s5_b_im': out['m_s5_b_im'], 'm_s5_c_re': out['m_s5_c_re'], 'm_s5_c_im': out['m_s5_c_im'], 'm_s5_d': out['m_s5_d'], 'm_w_s5_glu': out['m_w_s5_glu'], 'm_w_out': out['m_w_out'], 'm_norm_ffn_w': out['m_norm_ffn_w'], 'm_w_up': out['m_w_up'], 'm_conv_ffn_w': out['m_conv_ffn_w'], 'm_conv_ffn_b': out['m_conv_ffn_b'], 'm_w_down': out['m_w_down'], 'm_norm_final_w': out['m_norm_final_w'], 'v_norm_mix_w': out['v_norm_mix_w'], 'v_w_in': out['v_w_in'], 'v_conv_a_w': out['v_conv_a_w'], 'v_conv_a_b': out['v_conv_a_b'], 'v_dt_bias': out['v_dt_bias'], 'v_a_log': out['v_a_log'], 'v_d_a': out['v_d_a'], 'v_norm_a_w': out['v_norm_a_w'], 'v_w_proj_a': out['v_w_proj_a'], 'v_s5_lam_re': out['v_s5_lam_re'], 'v_s5_lam_im': out['v_s5_lam_im'], 'v_s5_log_dt': out['v_s5_log_dt'], 'v_s5_b_re': out['v_s5_b_re'], 'v_s5_b_im': out['v_s5_b_im'], 'v_s5_c_re': out['v_s5_c_re'], 'v_s5_c_im': out['v_s5_c_im'], 'v_s5_d': out['v_s5_d'], 'v_w_s5_glu': out['v_w_s5_glu'], 'v_w_out': out['v_w_out'], 'v_norm_ffn_w': out['v_norm_ffn_w'], 'v_w_up': out['v_w_up'], 'v_conv_ffn_w': out['v_conv_ffn_w'], 'v_conv_ffn_b': out['v_conv_ffn_b'], 'v_w_down': out['v_w_down'], 'v_norm_final_w': out['v_norm_final_w']}


def _loss(weights, diff, rest, loss_target):
    with _jax.named_scope("forward"):
        args = {**rest, TWIN_DIFF_INPUT: diff, **{k: w.astype(_WEIGHT_DTYPES[k]) for k, w in weights.items()}}
        y = _forward(args)
    with _jax.named_scope("loss_head"):
        err = _jnp.square(y.astype(_jnp.float32) - loss_target)
        return 0.5 * _jnp.sum(_jnp.mean(err, axis=-1)) if err.ndim else 0.5 * err


def _adamw(w, g, m, v):
    m = ADAM_B1 * m + (1.0 - ADAM_B1) * g
    v = ADAM_B2 * v + (1.0 - ADAM_B2) * _jnp.square(g)
    m_hat = m / (1.0 - ADAM_B1 ** ADAM_STEP)
    v_hat = v / (1.0 - ADAM_B2 ** ADAM_STEP)
    delta = -ADAM_LR * (m_hat / (_jnp.sqrt(v_hat) + ADAM_EPS) + ADAM_WD * w)
    return delta, m, v


def reference(x, norm_mix_w, w_in, conv_a_w, conv_a_b, dt_bias, a_log, d_a, norm_a_w, w_proj_a, s5_lam_re, s5_lam_im, s5_log_dt, s5_b_re, s5_b_im, s5_c_re, s5_c_im, s5_d, w_s5_glu, w_out, norm_ffn_w, w_up, conv_ffn_w, conv_ffn_b, w_down, norm_final_w, loss_target, m_norm_mix_w, m_w_in, m_conv_a_w, m_conv_a_b, m_dt_bias, m_a_log, m_d_a, m_norm_a_w, m_w_proj_a, m_s5_lam_re, m_s5_lam_im, m_s5_log_dt, m_s5_b_re, m_s5_b_im, m_s5_c_re, m_s5_c_im, m_s5_d, m_w_s5_glu, m_w_out, m_norm_ffn_w, m_w_up, m_conv_ffn_w, m_conv_ffn_b, m_w_down, m_norm_final_w, v_norm_mix_w, v_w_in, v_conv_a_w, v_conv_a_b, v_dt_bias, v_a_log, v_d_a, v_norm_a_w, v_w_proj_a, v_s5_lam_re, v_s5_lam_im, v_s5_log_dt, v_s5_b_re, v_s5_b_im, v_s5_c_re, v_s5_c_im, v_s5_d, v_w_s5_glu, v_w_out, v_norm_ffn_w, v_w_up, v_conv_ffn_w, v_conv_ffn_b, v_w_down, v_norm_final_w):
    given = dict(x=x, norm_mix_w=norm_mix_w, w_in=w_in, conv_a_w=conv_a_w, conv_a_b=conv_a_b, dt_bias=dt_bias, a_log=a_log, d_a=d_a, norm_a_w=norm_a_w, w_proj_a=w_proj_a, s5_lam_re=s5_lam_re, s5_lam_im=s5_lam_im, s5_log_dt=s5_log_dt, s5_b_re=s5_b_re, s5_b_im=s5_b_im, s5_c_re=s5_c_re, s5_c_im=s5_c_im, s5_d=s5_d, w_s5_glu=w_s5_glu, w_out=w_out, norm_ffn_w=norm_ffn_w, w_up=w_up, conv_ffn_w=conv_ffn_w, conv_ffn_b=conv_ffn_b, w_down=w_down, norm_final_w=norm_final_w, loss_target=loss_target, m_norm_mix_w=m_norm_mix_w, m_w_in=m_w_in, m_conv_a_w=m_conv_a_w, m_conv_a_b=m_conv_a_b, m_dt_bias=m_dt_bias, m_a_log=m_a_log, m_d_a=m_d_a, m_norm_a_w=m_norm_a_w, m_w_proj_a=m_w_proj_a, m_s5_lam_re=m_s5_lam_re, m_s5_lam_im=m_s5_lam_im, m_s5_log_dt=m_s5_log_dt, m_s5_b_re=m_s5_b_re, m_s5_b_im=m_s5_b_im, m_s5_c_re=m_s5_c_re, m_s5_c_im=m_s5_c_im, m_s5_d=m_s5_d, m_w_s5_glu=m_w_s5_glu, m_w_out=m_w_out, m_norm_ffn_w=m_norm_ffn_w, m_w_up=m_w_up, m_conv_ffn_w=m_conv_ffn_w, m_conv_ffn_b=m_conv_ffn_b, m_w_down=m_w_down, m_norm_final_w=m_norm_final_w, v_norm_mix_w=v_norm_mix_w, v_w_in=v_w_in, v_conv_a_w=v_conv_a_w, v_conv_a_b=v_conv_a_b, v_dt_bias=v_dt_bias, v_a_log=v_a_log, v_d_a=v_d_a, v_norm_a_w=v_norm_a_w, v_w_proj_a=v_w_proj_a, v_s5_lam_re=v_s5_lam_re, v_s5_lam_im=v_s5_lam_im, v_s5_log_dt=v_s5_log_dt, v_s5_b_re=v_s5_b_re, v_s5_b_im=v_s5_b_im, v_s5_c_re=v_s5_c_re, v_s5_c_im=v_s5_c_im, v_s5_d=v_s5_d, v_w_s5_glu=v_w_s5_glu, v_w_out=v_w_out, v_norm_ffn_w=v_norm_ffn_w, v_w_up=v_w_up, v_conv_ffn_w=v_conv_ffn_w, v_conv_ffn_b=v_conv_ffn_b, v_w_down=v_w_down, v_norm_final_w=v_norm_final_w)
    weights = {n: given[n] for n in TWIN_WEIGHTS}
    shared = {n: given[n] for n in SHARED_INPUTS}
    per_example = {n: given[n] for n in ['x']}
    grad_fn = _jax.value_and_grad(_loss, argnums=(0, 1))

    def one_microbatch(ex, loss_target):
        ex = dict(ex)
        diff = ex.pop(TWIN_DIFF_INPUT)
        return grad_fn(weights, diff, {**shared, **ex}, loss_target)

    if N_MICROBATCH == 1:
        loss, (grad_w, grad_x) = one_microbatch(per_example, given["loss_target"])
    else:
        def body(carry, xs):
            loss_sum, grad_sum = carry
            l_k, (gw_k, gx_k) = one_microbatch(xs[0], xs[1])
            with _jax.named_scope("update"):
                return (loss_sum + l_k, _jax.tree.map(_jnp.add, grad_sum, gw_k)), gx_k

        init = (_jnp.zeros((), _jnp.float32), _jax.tree.map(_jnp.zeros_like, weights))
        (loss, grad_w), grad_x = _jax.lax.scan(body, init, (per_example, given["loss_target"]))
    with _jax.named_scope("update"):
        delta_w, new_m, new_v = {}, {}, {}
        for n in TWIN_WEIGHTS:
            delta_w[n], new_m[n], new_v[n] = _adamw(weights[n], grad_w[n], given["m_" + n], given["v_" + n])
    return (loss, grad_x, *[grad_w[n] for n in TWIN_WEIGHTS], *[delta_w[n] for n in TWIN_WEIGHTS],
            *[new_m[n] for n in TWIN_WEIGHTS], *[new_v[n] for n in TWIN_WEIGHTS])
```

```python
import functools

import jax
import jax.numpy as jnp
from jax import lax
from jax.experimental import pallas as pl
from jax.experimental.pallas import tpu as pltpu

F32 = jnp.float32
BF16 = jnp.bfloat16
HIGHEST = lax.Precision.HIGHEST
MESH = pl.DeviceIdType.MESH

EPS = 1e-6
EIG_MAX = -1e-4
D_STATE = 128
CHUNK = 256
ADAM_LR = 0.001
ADAM_B1 = 0.9
ADAM_B2 = 0.999
ADAM_EPS = 1e-08
ADAM_WD = 0.01
ADAM_STEP = 10
N_DEV = 8
LANES = 128
SUBLANES = 8
VMEM_LIMIT = 56 * 1024 * 1024


def _cp(*sem):
    return pltpu.CompilerParams(dimension_semantics=sem, vmem_limit_bytes=VMEM_LIMIT)


def _tile(dim, pref, unit=LANES):
    if dim <= unit:
        return dim
    t = (min(pref, dim) // unit) * unit
    while dim % t:
        t -= unit
    return t


_DIMS = {"nn": (((1,), (0,)), ((), ())), "nt": (((1,), (1,)), ((), ())), "tn": (((0,), (0,)), ((), ()))}


def _dot(a, b, kind):
    return lax.dot_general(a.astype(BF16), b.astype(BF16), _DIMS[kind], preferred_element_type=F32)


@functools.partial(jax.custom_vjp, nondiff_argnums=(2,))
def _bdot(a, b, kind):
    return _dot(a, b, kind)


def _bdot_fwd(a, b, kind):
    return _dot(a, b, kind), (a, b)


def _bdot_bwd(kind, res, g):
    a, b = res
    if kind == "nn":
        return _dot(g, b, "nt"), _dot(a, g, "tn")
    if kind == "nt":
        return _dot(g, b, "nn"), _dot(g, a, "tn")
    return _dot(b, g, "nt"), _dot(a, g, "nn")


_bdot.defvjp(_bdot_fwd, _bdot_bwd)


def _mm(a, b, *, ta=False, tb=False, acc=None, out_dtype=F32, name):
    m, k = (a.shape[1], a.shape[0]) if ta else a.shape
    n = b.shape[0] if tb else b.shape[1]
    assert (b.shape[1] if tb else b.shape[0]) == k, (a.shape, b.shape, ta, tb)
    tm, tn, tk = _tile(m, 512), _tile(n, 1024), _tile(k, 512)
    nk = k // tk
    kind = "tn" if ta else ("nt" if tb else "nn")
    assert not (ta and tb)
    a_spec = pl.BlockSpec((tk, tm), lambda i, j, l: (l, i)) if ta else pl.BlockSpec((tm, tk), lambda i, j, l: (i, l))
    b_spec = pl.BlockSpec((tn, tk), lambda i, j, l: (j, l)) if tb else pl.BlockSpec((tk, tn), lambda i, j, l: (l, j))
    o_spec = pl.BlockSpec((tm, tn), lambda i, j, l: (i, j))
    has_acc = acc is not None

    def body(*refs):
        if has_acc:
            a_ref, b_ref, c_ref, o_ref, acc_ref = refs
        else:
            a_ref, b_ref, o_ref, acc_ref = refs
        l = pl.program_id(2)

        @pl.when(l == 0)
        def _():
            if has_acc:
                acc_ref[...] = c_ref[...].astype(F32)
            else:
                acc_ref[...] = jnp.zeros_like(acc_ref)

        acc_ref[...] += _dot(a_ref[...], b_ref[...], kind)

        @pl.when(l == nk - 1)
        def _():
            o_ref[...] = acc_ref[...].astype(o_ref.dtype)

    ins = [a, b] + ([acc] if has_acc else [])
    in_specs = [a_spec, b_spec] + ([o_spec] if has_acc else [])
    return pl.pallas_call(
        body, name=name, grid=(m // tm, n // tn, nk), in_specs=in_specs, out_specs=o_spec,
        out_shape=jax.ShapeDtypeStruct((m, n), out_dtype),
        scratch_shapes=[pltpu.VMEM((tm, tn), F32)],
        compiler_params=_cp("parallel", "parallel", "arbitrary"),
    )(*ins)


def _row_spec(arr, tb, nj):
    return pl.BlockSpec((tb, arr.shape[1] // nj), lambda j, i: (i, j))


def _par_spec(arr):
    return pl.BlockSpec((1,) + arr.shape[1:], lambda j, i: (j, 0, 0))


def _blocked_fwd(fn, rows, params, outs, *, nj=1, tb, name):
    t = rows[0].shape[0]
    nr, npar = len(rows), len(params)

    def body(*refs):
        res = fn(*[r[...] for r in refs[:nr]], *[p[0] for p in refs[nr:nr + npar]])
        for o_ref, val in zip(refs[nr + npar:], res):
            o_ref[...] = val.astype(o_ref.dtype)

    return pl.pallas_call(
        body, name=name, grid=(nj, t // tb),
        in_specs=[_row_spec(a, tb, nj) for a in rows] + [_par_spec(p) for p in params],
        out_specs=[pl.BlockSpec((tb, c // nj), lambda j, i: (i, j)) for c, _ in outs],
        out_shape=[jax.ShapeDtypeStruct((t, c), dt) for c, dt in outs],
        compiler_params=_cp("parallel", "arbitrary"),
    )(*rows, *params)


def _blocked_bwd(fn, rows, params, cts, row_grad_dtypes, *, adds=None, nj=1, tb, name):
    t = rows[0].shape[0]
    nr, npar, nct = len(rows), len(params), len(cts)
    adds = adds or {}
    add_keys = sorted(adds)
    want = [k for k, dt in enumerate(row_grad_dtypes) if dt is not None]

    def body(*refs):
        row_refs = refs[:nr]
        par_refs = refs[nr:nr + npar]
        ct_refs = refs[nr + npar:nr + npar + nct]
        add_refs = dict(zip(add_keys, refs[nr + npar + nct:nr + npar + nct + len(add_keys)]))
        out_refs = refs[nr + npar + nct + len(add_keys):]
        _, vjp = jax.vjp(fn, *[r[...] for r in row_refs], *[p[0] for p in par_refs])
        grads = vjp(tuple(c[...].astype(F32) for c in ct_refs))
        for o_ref, k in zip(out_refs, want):
            g = grads[k]
            if k in add_refs:
                g = g + add_refs[k][...].astype(F32)
            o_ref[...] = g.astype(o_ref.dtype)
        first = pl.program_id(1) == 0
        for o_ref, g in zip(out_refs[len(want):], grads[nr:]):
            @pl.when(first)
            def _(o_ref=o_ref):
                o_ref[...] = jnp.zeros_like(o_ref)
            o_ref[0] += g

    add_arrs = [adds[k] for k in add_keys]
    return pl.pallas_call(
        body, name=name, grid=(nj, t // tb),
        in_specs=[_row_spec(a, tb, nj) for a in rows] + [_par_spec(p) for p in params]
        + [_row_spec(c, tb, nj) for c in cts] + [_row_spec(a, tb, nj) for a in add_arrs],
        out_specs=[_row_spec(rows[k], tb, nj) for k in want] + [_par_spec(p) for p in params],
        out_shape=[jax.ShapeDtypeStruct(rows[k].shape, row_grad_dtypes[k]) for k in want]
        + [jax.ShapeDtypeStruct(p.shape, F32) for p in params],
        compiler_params=_cp("parallel", "arbitrary"),
    )(*rows, *params, *cts, *add_arrs)


def _rms_fn(x, w):
    return (x * lax.rsqrt(jnp.mean(x * x, axis=-1, keepdims=True) + EPS) * w,)


def _silu(x):
    return x * jax.nn.sigmoid(x)


def _merge_fn(glu_v, glu_g, g_a, g_b, y_a):
    y_b = glu_v * jax.nn.sigmoid(glu_g)
    return (jax.nn.sigmoid(g_a) * y_a + jax.nn.sigmoid(g_b) * y_b,)


def _s5_bu_fn(u, b_re, b_im):
    return _bdot(u, b_re, "nn"), _bdot(u, b_im, "nn")


def _s5_out_fn(s_re, s_im, u, c_re, c_im_neg, d):
    return (jax.nn.gelu(_bdot(s_re, c_re, "nn") + _bdot(s_im, c_im_neg, "nn") + d * u),)


HALO = SUBLANES


def _conv_fn(comb, kw, ns):
    def fn(*args):
        cs = []
        for s in range(ns):
            xp, xm, w, b = args[4 * s:4 * s + 4]
            xe = jnp.concatenate([xp, xm], axis=0)
            tb = xm.shape[0]
            y = b
            for k in range(kw):
                off = HALO - kw + 1 + k
                y = y + w[k:k + 1, :] * xe[off:off + tb, :]
            cs.append(y)
        return comb(*cs)
    return fn


def _conv_specs(xs, ws, bs, tb, cb, time_of):
    specs = []
    for x, w, b in zip(xs, ws, bs):
        specs += [
            pl.BlockSpec((HALO, cb), lambda j, i: (jnp.maximum(time_of(i) * (tb // HALO) - 1, 0), j)),
            pl.BlockSpec((tb, cb), lambda j, i: (time_of(i), j)),
            pl.BlockSpec((w.shape[0], cb), lambda j, i: (0, j)),
            pl.BlockSpec((1, cb), lambda j, i: (0, j)),
        ]
    return specs


def _conv_fwd(comb, xs, ws, bs, *, out_dtype, name, tb=512):
    t, c = xs[0].shape
    cb = _tile(c, 512)
    ns = len(xs)
    fn = _conv_fn(comb, ws[0].shape[0], ns)

    def body(*refs):
        i = pl.program_id(1)
        args = []
        for s in range(ns):
            xp_ref, xm_ref, w_ref, b_ref = refs[4 * s:4 * s + 4]
            xp = jnp.where(i == 0, 0.0, xp_ref[...])
            args += [xp, xm_ref[...], w_ref[...], b_ref[...]]
        refs[4 * ns][...] = fn(*args).astype(out_dtype)

    flat = [a for x, w, b in zip(xs, ws, bs) for a in (x, x, w, b)]
    return pl.pallas_call(
        body, name=name, grid=(c // cb, t // tb),
        in_specs=_conv_specs(xs, ws, bs, tb, cb, lambda i: i),
        out_specs=pl.BlockSpec((tb, cb), lambda j, i: (i, j)),
        out_shape=jax.ShapeDtypeStruct((t, c), out_dtype),
        compiler_params=_cp("parallel", "arbitrary"),
    )(*flat)


def _conv_bwd(comb, xs, ws, bs, dy, *, dx_dtype, name, tb=512):
    t, c = xs[0].shape
    cb = _tile(c, 512)
    ns = len(xs)
    nt = t // tb
    fn = _conv_fn(comb, ws[0].shape[0], ns)

    def body(*refs):
        step = pl.program_id(1)
        in_refs = refs[:4 * ns]
        dy_ref = refs[4 * ns]
        out_refs = refs[4 * ns + 1:4 * ns + 1 + 3 * ns]
        carry_refs = refs[4 * ns + 1 + 3 * ns:]
        args = []
        for s in range(ns):
            xp_ref, xm_ref, w_ref, b_ref = in_refs[4 * s:4 * s + 4]
            xp = jnp.where(step == nt - 1, 0.0, xp_ref[...])
            args += [xp, xm_ref[...], w_ref[...], b_ref[...]]
        _, vjp = jax.vjp(fn, *args)
        grads = vjp(dy_ref[...].astype(F32))
        for s in range(ns):
            dxp, dxm, dw, db = grads[4 * s:4 * s + 4]
            dx_ref, dw_ref, db_ref = out_refs[3 * s:3 * s + 3]
            carry = carry_refs[s]

            @pl.when(step == 0)
            def _(carry=carry, dw_ref=dw_ref, db_ref=db_ref):
                carry[...] = jnp.zeros_like(carry)
                dw_ref[...] = jnp.zeros_like(dw_ref)
                db_ref[...] = jnp.zeros_like(db_ref)

            tail = jnp.concatenate([jnp.zeros((tb - HALO, cb), F32), carry[...]], axis=0)
            dx_ref[...] = (dxm + tail).astype(dx_dtype)
            carry[...] = dxp
            dw_ref[...] += dw
            db_ref[...] += db

    flat = [a for x, w, b in zip(xs, ws, bs) for a in (x, x, w, b)]
    rev = lambda i: nt - 1 - i
    out_specs, out_shape = [], []
    for x, w, b in zip(xs, ws, bs):
        out_specs += [pl.BlockSpec((tb, cb), lambda j, i: (rev(i), j)),
                      pl.BlockSpec((w.shape[0], cb), lambda j, i: (0, j)),
                      pl.BlockSpec((1, cb), lambda j, i: (0, j))]
        out_shape += [jax.ShapeDtypeStruct((t, c), dx_dtype), jax.ShapeDtypeStruct(w.shape, F32),
                      jax.ShapeDtypeStruct(b.shape, F32)]
    res = pl.pallas_call(
        body, name=name, grid=(c // cb, nt),
        in_specs=_conv_specs(xs, ws, bs, tb, cb, rev) + [pl.BlockSpec((tb, cb), lambda j, i: (rev(i), j))],
        out_specs=out_specs, out_shape=out_shape,
        scratch_shapes=[pltpu.VMEM((HALO, cb), F32) for _ in range(ns)],
        compiler_params=_cp("parallel", "arbitrary"),
    )(*flat, dy)
    return [tuple(res[3 * s:3 * s + 3]) for s in range(ns)]


def _comb_silu(c):
    return _silu(c)


def _comb_glu(cg, cv):
    return _silu(cg) * cv


def _ssd_fn(nheads, hdim):
    def fn(x, bm, cm, z, dtr, hin, dtb, alog, dsk, nw):
        q = x.shape[0]
        dt = jax.nn.softplus(dtr + dtb)
        da = dt * (-jnp.exp(alog))
        li = lax.broadcasted_iota(jnp.int32, (q, q), 0)
        si = lax.broadcasted_iota(jnp.int32, (q, q), 1)
        causal = li >= si
        tri = causal.astype(F32)
        acs = jnp.dot(tri, da, precision=HIGHEST, preferred_element_type=F32)
        acs_row = lax.dot_general(da, tri, (((0,), (1,)), ((), ())), precision=HIGHEST,
                                  preferred_element_type=F32)
        cb = _bdot(cm, bm, "nt")
        ch = _bdot(cm, hin, "nn")
        ys, hs = [], []
        for r in range(nheads):
            cols = slice(r * hdim, (r + 1) * hdim)
            xr = x[:, cols]
            a_col = acs[:, r:r + 1]
            decay = jnp.exp(jnp.where(causal, a_col - acs_row[r:r + 1, :], -1e30))
            xd = xr * dt[:, r:r + 1]
            y_diag = _bdot(cb * decay, xd, "nn")
            y_off = ch[:, cols] * jnp.exp(a_col)
            last = acs[q - 1:q, r:r + 1]
            st = _bdot(bm * jnp.exp(last - a_col), xd, "tn")
            hs.append(jnp.exp(last) * hin[:, cols] + st)
            ys.append(y_diag + y_off + dsk[:, r:r + 1] * xr)
        y = jnp.concatenate(ys, axis=1) * _silu(z)
        yn = y * lax.rsqrt(jnp.mean(y * y, axis=-1, keepdims=True) + EPS) * nw
        return yn, jnp.concatenate(hs, axis=1)
    return fn


def _ssd_specs(rp, nr, time_of):
    row = lambda w: pl.BlockSpec((CHUNK, w), lambda g, c: (time_of(c), g))
    par = lambda w: pl.BlockSpec((1, 1, w), lambda g, c: (g, 0, 0))
    return dict(
        x=row(rp), bc=row(D_STATE), dtr=pl.BlockSpec((1, CHUNK, nr), lambda g, c: (g, time_of(c), 0)),
        h=pl.BlockSpec((1, 1, D_STATE, rp), lambda g, c: (g, time_of(c), 0, 0)), pr=par(nr), pw=par(rp))


def _ssd_fwd(xs, bm, cm, z, dtr, dtb, alog, dsk, nw, *, name):
    t = xs.shape[0]
    g, _, nr = dtr.shape
    rp = xs.shape[1] // g
    nc = t // CHUNK
    fn = _ssd_fn(nr, rp // nr)
    sp = _ssd_specs(rp, nr, lambda c: c)

    def body(x_ref, b_ref, c_ref, z_ref, dtr_ref, dtb_ref, al_ref, dsk_ref, nw_ref, yn_ref, hs_ref, h_ref):
        @pl.when(pl.program_id(1) == 0)
        def _():
            h_ref[...] = jnp.zeros_like(h_ref)
        hin = h_ref[...]
        hs_ref[0, 0] = hin
        yn, hout = fn(x_ref[...], b_ref[...], c_ref[...], z_ref[...], dtr_ref[0], hin,
                      dtb_ref[0], al_ref[0], dsk_ref[0], nw_ref[0])
        yn_ref[...] = yn.astype(yn_ref.dtype)
        h_ref[...] = hout

    return pl.pallas_call(
        body, name=name, grid=(g, nc),
        in_specs=[sp["x"], sp["bc"], sp["bc"], sp["x"], sp["dtr"], sp["pr"], sp["pr"], sp["pr"], sp["pw"]],
        out_specs=[sp["x"], sp["h"]],
        out_shape=[jax.ShapeDtypeStruct(xs.shape, BF16), jax.ShapeDtypeStruct((g, nc, D_STATE, rp), F32)],
        scratch_shapes=[pltpu.VMEM((D_STATE, rp), F32)],
        compiler_params=_cp("parallel", "arbitrary"),
    )(xs, bm, cm, z, dtr, dtb, alog, dsk, nw)


def _ssd_bwd(xs, bm, cm, z, dtr, hsave, dtb, alog, dsk, nw, dyn, *, name):
    t = xs.shape[0]
    g, _, nr = dtr.shape
    rp = xs.shape[1] // g
    nc = t // CHUNK
    fn = _ssd_fn(nr, rp // nr)
    sp = _ssd_specs(rp, nr, lambda c: nc - 1 - c)

    def body(x_ref, b_ref, c_ref, z_ref, dtr_ref, hs_ref, dtb_ref, al_ref, dsk_ref, nw_ref, dyn_ref,
             dx_ref, db_ref, dc_ref, dz_ref, ddtr_ref, ddtb_ref, dal_ref, ddsk_ref, dnw_ref, dh_ref):
        first = pl.program_id(1) == 0

        @pl.when(first)
        def _():
            dh_ref[...] = jnp.zeros_like(dh_ref)
            for r in (ddtb_ref, dal_ref, ddsk_ref, dnw_ref):
                r[...] = jnp.zeros_like(r)

        _, vjp = jax.vjp(fn, x_ref[...], b_ref[...], c_ref[...], z_ref[...], dtr_ref[0], hs_ref[0, 0],
                         dtb_ref[0], al_ref[0], dsk_ref[0], nw_ref[0])
        dx, db, dc, dz, ddtr, dhin, ddtb, dal, ddsk, dnw = vjp((dyn_ref[...].astype(F32), dh_ref[...]))
        dx_ref[...] = dx
        db_ref[...] = db
        dc_ref[...] = dc
        dz_ref[...] = dz.astype(dz_ref.dtype)
        ddtr_ref[0] = ddtr
        dh_ref[...] = dhin
        ddtb_ref[0] += ddtb
        dal_ref[0] += dal
        ddsk_ref[0] += ddsk
        dnw_ref[0] += dnw

    sd = jax.ShapeDtypeStruct
    return pl.pallas_call(
        body, name=name, grid=(g, nc),
        in_specs=[sp["x"], sp["bc"], sp["bc"], sp["x"], sp["dtr"], sp["h"], sp["pr"], sp["pr"], sp["pr"], sp["pw"],
                  sp["x"]],
        out_specs=[sp["x"], sp["bc"], sp["bc"], sp["x"], sp["dtr"], sp["pr"], sp["pr"], sp["pr"], sp["pw"]],
        out_shape=[sd(xs.shape, F32), sd(bm.shape, F32), sd(cm.shape, F32), sd(z.shape, BF16), sd(dtr.shape, F32),
                   sd(dtb.shape, F32), sd(alog.shape, F32), sd(dsk.shape, F32), sd(nw.shape, F32)],
        scratch_shapes=[pltpu.VMEM((D_STATE, rp), F32)],
        compiler_params=_cp("parallel", "arbitrary"),
    )(xs, bm, cm, z, dtr, hsave, dtb, alog, dsk, nw, dyn)


def _s5_param_fn(lam_re, lam_im, log_dt, bt_re, bt_im):
    lr = jnp.minimum(lam_re, EIG_MAX)
    dt = jnp.exp(log_dt)
    mag = jnp.exp(lr * dt)
    lb_re = mag * jnp.cos(lam_im * dt)
    lb_im = mag * jnp.sin(lam_im * dt)
    n_re = lb_re - 1.0
    den = lr * lr + lam_im * lam_im
    k_re = (n_re * lr + lb_im * lam_im) / den
    k_im = (lb_im * lr - n_re * lam_im) / den
    return lb_re, lb_im, k_re * bt_re - k_im * bt_im, k_re * bt_im + k_im * bt_re


def _s5_params(lam_re, lam_im, log_dt, bt_re, bt_im, cts=None, *, name):
    args = (lam_re, lam_im, log_dt, bt_re, bt_im)
    n = len(args)

    def body(*refs):
        vals = [r[...] for r in refs[:n]]
        if cts is None:
            res = _s5_param_fn(*vals)
        else:
            _, vjp = jax.vjp(_s5_param_fn, *vals)
            res = vjp(tuple(r[...] for r in refs[n:n + 4]))
        for o_ref, v in zip(refs[-len(res):], res):
            o_ref[...] = v

    if cts is None:
        out = [lam_re, lam_im, bt_re, bt_im]
        ins = args
    else:
        out = list(args)
        ins = args + tuple(cts)
    return pl.pallas_call(
        body, name=name, out_shape=[jax.ShapeDtypeStruct(a.shape, F32) for a in out],
        compiler_params=pltpu.CompilerParams(vmem_limit_bytes=VMEM_LIMIT),
    )(*ins)


SCAN_COLS = 512


def _scan_specs(tb, time_of):
    row = pl.BlockSpec((tb, SCAN_COLS), lambda j, i: (time_of(i), j))
    par = pl.BlockSpec((1, SCAN_COLS), lambda j, i: (0, j))
    return row, par


def _s5_scan_fwd(bu_re, bu_im, lb_re, lb_im, *, name, tb=256):
    t, c = bu_re.shape
    nj = c // SCAN_COLS
    row, par = _scan_specs(tb, lambda i: i)

    def body(bre_ref, bim_ref, lre_ref, lim_ref, sre_ref, sim_ref, cre_ref, cim_ref):
        @pl.when(pl.program_id(1) == 0)
        def _():
            cre_ref[...] = jnp.zeros_like(cre_ref)
            cim_ref[...] = jnp.zeros_like(cim_ref)
        ar, ai = lre_ref[...], lim_ref[...]

        def step(k, s):
            sr, si = s
            nr = ar * sr - ai * si + bre_ref[pl.ds(k, 1), :]
            ni = ar * si + ai * sr + bim_ref[pl.ds(k, 1), :]
            sre_ref[pl.ds(k, 1), :] = nr
            sim_ref[pl.ds(k, 1), :] = ni
            return nr, ni

        sr, si = lax.fori_loop(0, tb, step, (cre_ref[...], cim_ref[...]), unroll=8)
        cre_ref[...] = sr
        cim_ref[...] = si

    return pl.pallas_call(
        body, name=name, grid=(nj, t // tb), in_specs=[row, row, par, par], out_specs=[row, row],
        out_shape=[jax.ShapeDtypeStruct((t, c), F32)] * 2,
        scratch_shapes=[pltpu.VMEM((1, SCAN_COLS), F32)] * 2,
        compiler_params=_cp("parallel", "arbitrary"),
    )(bu_re, bu_im, lb_re, lb_im)


def _s5_scan_bwd(s_re, s_im, ds_re, ds_im, lb_re, lb_im, *, name, tb=256):
    t, c = s_re.shape
    nj = c // SCAN_COLS
    nt = t // tb
    rev = lambda i: nt - 1 - i
    row, par = _scan_specs(tb, rev)
    prev = pl.BlockSpec((HALO, SCAN_COLS), lambda j, i: (jnp.maximum(rev(i) * (tb // HALO) - 1, 0), j))

    def body(sre_ref, sim_ref, pre_ref, pim_ref, dre_ref, dim_ref, lre_ref, lim_ref,
             gre_ref, gim_ref, dlre_ref, dlim_ref, cre_ref, cim_ref, ext_re, ext_im):
        step_id = pl.program_id(1)

        @pl.when(step_id == 0)
        def _():
            cre_ref[...] = jnp.zeros_like(cre_ref)
            cim_ref[...] = jnp.zeros_like(cim_ref)
            dlre_ref[...] = jnp.zeros_like(dlre_ref)
            dlim_ref[...] = jnp.zeros_like(dlim_ref)
        ar, ai = lre_ref[...], lim_ref[...]

        def step(k, g):
            gr, gi = g
            row = tb - 1 - k
            nr = dre_ref[pl.ds(row, 1), :] + ar * gr + ai * gi
            ni = dim_ref[pl.ds(row, 1), :] + ar * gi - ai * gr
            gre_ref[pl.ds(row, 1), :] = nr
            gim_ref[pl.ds(row, 1), :] = ni
            return nr, ni

        gr, gi = lax.fori_loop(0, tb, step, (cre_ref[...], cim_ref[...]), unroll=8)
        cre_ref[...] = gr
        cim_ref[...] = gi
        has_past = step_id != nt - 1
        ext_re[pl.ds(0, HALO), :] = jnp.where(has_past, pre_ref[...], 0.0)
        ext_im[pl.ds(0, HALO), :] = jnp.where(has_past, pim_ref[...], 0.0)
        ext_re[pl.ds(HALO, tb), :] = sre_ref[...]
        ext_im[pl.ds(HALO, tb), :] = sim_ref[...]
        pr, pi = ext_re[pl.ds(HALO - 1, tb), :], ext_im[pl.ds(HALO - 1, tb), :]
        g_re, g_im = gre_ref[...], gim_ref[...]
        dlre_ref[...] += jnp.sum(pr * g_re + pi * g_im, axis=0, keepdims=True)
        dlim_ref[...] += jnp.sum(pr * g_im - pi * g_re, axis=0, keepdims=True)

    return pl.pallas_call(
        body, name=name, grid=(nj, nt),
        in_specs=[row, row, prev, prev, row, row, par, par], out_specs=[row, row, par, par],
        out_shape=[jax.ShapeDtypeStruct((t, c), F32)] * 2 + [jax.ShapeDtypeStruct((1, c), F32)] * 2,
        scratch_shapes=[pltpu.VMEM((1, SCAN_COLS), F32)] * 2 + [pltpu.VMEM((HALO + tb, SCAN_COLS), F32)] * 2,
        compiler_params=_cp("parallel", "arbitrary"),
    )(s_re, s_im, s_re, s_im, ds_re, ds_im, lb_re, lb_im)


def _loss_fn(h, w, tgt):
    err = _rms_fn(h, w)[0] - tgt
    return 0.5 * jnp.sum(jnp.mean(err * err, axis=-1, keepdims=True), axis=0, keepdims=True)


def _loss_head(h, w, tgt, *, name, tb=256):
    t, d = h.shape

    def body(h_ref, w_ref, t_ref, loss_ref, dh_ref, dw_ref):
        @pl.when(pl.program_id(0) == 0)
        def _():
            loss_ref[...] = jnp.zeros_like(loss_ref)
            dw_ref[...] = jnp.zeros_like(dw_ref)
        part, vjp = jax.vjp(_loss_fn, h_ref[...], w_ref[...], t_ref[...])
        dh, dw, _ = vjp(jnp.ones((1, 1), F32))
        loss_ref[...] += jnp.broadcast_to(part, loss_ref.shape)
        dh_ref[...] = dh
        dw_ref[...] += dw

    row = pl.BlockSpec((tb, d), lambda i: (i, 0))
    par = pl.BlockSpec((1, d), lambda i: (0, 0))
    return pl.pallas_call(
        body, name=name, grid=(t // tb,), in_specs=[row, par, row],
        out_specs=[pl.BlockSpec((SUBLANES, LANES), lambda i: (0, 0)), row, par],
        out_shape=[jax.ShapeDtypeStruct((SUBLANES, LANES), F32), jax.ShapeDtypeStruct((t, d), F32),
                   jax.ShapeDtypeStruct((1, d), F32)],
        compiler_params=_cp("arbitrary"),
    )(h, w, tgt)


def _adamw(w, g, m, v, *, name):
    r, c = w.shape
    tr = _tile(r, 256, SUBLANES)

    def body(w_ref, g_ref, m_ref, v_ref, d_ref, nm_ref, nv_ref):
        g = g_ref[...]
        nm = ADAM_B1 * m_ref[...] + (1.0 - ADAM_B1) * g
        nv = ADAM_B2 * v_ref[...] + (1.0 - ADAM_B2) * (g * g)
        m_hat = nm / (1.0 - ADAM_B1 ** ADAM_STEP)
        v_hat = nv / (1.0 - ADAM_B2 ** ADAM_STEP)
        d_ref[...] = -ADAM_LR * (m_hat / (jnp.sqrt(v_hat) + ADAM_EPS) + ADAM_WD * w_ref[...])
        nm_ref[...] = nm
        nv_ref[...] = nv

    spec = pl.BlockSpec((tr, c), lambda i: (i, 0))
    return pl.pallas_call(
        body, name=name, grid=(r // tr,), in_specs=[spec] * 4, out_specs=[spec] * 3,
        out_shape=[jax.ShapeDtypeStruct((r, c), F32)] * 3, compiler_params=_cp("parallel"),
    )(w, g, m, v)


def _sum_parts(parts, *, name):
    _, r, c = parts.shape
    tr = _tile(r, 128, SUBLANES)

    def body(p_ref, o_ref):
        acc = p_ref[0].astype(F32)
        for k in range(1, N_DEV):
            acc = acc + p_ref[k].astype(F32)
        o_ref[...] = acc

    return pl.pallas_call(
        body, name=name, grid=(r // tr,), in_specs=[pl.BlockSpec((N_DEV, tr, c), lambda i: (0, i, 0))],
        out_specs=pl.BlockSpec((tr, c), lambda i: (i, 0)), out_shape=jax.ShapeDtypeStruct((r, c), F32),
        compiler_params=_cp("parallel"),
    )(parts)


def _position():
    return lax.axis_index("x"), lax.axis_index("y"), lax.axis_index("c")


def _flat(px, py, pc):
    return 4 * px + 2 * py + pc


def _all_gather(shard, *, name):
    def body(x_ref, out_ref, send_sems, recv_sems, local_sem):
        x, y, c = _position()
        me, sibling = (x, y, c), (x, y, 1 - c)
        chips = [(1 - x, y), (x, 1 - y), (1 - x, 1 - y)]

        def copy(k, block, to, src=None):
            slot = out_ref.at[_flat(*block)]
            return pltpu.make_async_remote_copy(
                src_ref=slot if src is None else src, dst_ref=slot, send_sem=send_sems.at[k],
                recv_sem=recv_sems.at[k], device_id=to, device_id_type=MESH)

        mine = pltpu.make_async_copy(x_ref, out_ref.at[_flat(*me)], local_sem)
        mine.start()
        first = [copy(0, me, sibling, src=x_ref)]
        first += [copy(1 + j, me, (*chip, c), src=x_ref) for j, chip in enumerate(chips)]
        for cp in first:
            cp.start()
        passed = [copy(4 + j, (*chip, c), sibling) for j, chip in enumerate(chips)]
        for j, chip in enumerate(chips):
            copy(1 + j, (*chip, c), me).wait_recv()
            passed[j].start()
        copy(0, sibling, me).wait_recv()
        for j, chip in enumerate(chips):
            copy(4 + j, (*chip, 1 - c), me).wait_recv()
        for cp in first + passed:
            cp.wait_send()
        mine.wait()

    return pl.pallas_call(
        body, name=name, out_shape=jax.ShapeDtypeStruct((N_DEV,) + shard.shape, shard.dtype),
        in_specs=[pl.BlockSpec(memory_space=pl.ANY)], out_specs=pl.BlockSpec(memory_space=pl.ANY),
        scratch_shapes=[pltpu.SemaphoreType.DMA((7,)), pltpu.SemaphoreType.DMA((7,)), pltpu.SemaphoreType.DMA(())],
    )(shard)


def _exchange_parts(parts, *, name):
    def body(p_ref, land_ref, send_sems, recv_sems, local_sem):
        x, y, c = _position()
        me = _flat(x, y, c)
        local = pltpu.make_async_copy(p_ref.at[me], land_ref.at[me], local_sem)
        local.start()
        copies = []
        for k in range(1, N_DEV):
            peer = (x ^ ((k >> 2) & 1), y ^ ((k >> 1) & 1), c ^ (k & 1))
            cp = pltpu.make_async_remote_copy(
                src_ref=p_ref.at[_flat(*peer)], dst_ref=land_ref.at[me], send_sem=send_sems.at[k - 1],
                recv_sem=recv_sems.at[k - 1], device_id=peer, device_id_type=MESH)
            cp.start()
            copies.append((cp, peer))
        for k, (cp, peer) in enumerate(copies):
            pltpu.make_async_remote_copy(
                src_ref=p_ref.at[me], dst_ref=land_ref.at[_flat(*peer)], send_sem=send_sems.at[k],
                recv_sem=recv_sems.at[k], device_id=peer, device_id_type=MESH).wait_recv()
        for cp, _ in copies:
            cp.wait_send()
        local.wait()

    return pl.pallas_call(
        body, name=name, out_shape=jax.ShapeDtypeStruct(parts.shape, parts.dtype),
        in_specs=[pl.BlockSpec(memory_space=pl.ANY)], out_specs=pl.BlockSpec(memory_space=pl.ANY),
        scratch_shapes=[pltpu.SemaphoreType.DMA((7,)), pltpu.SemaphoreType.DMA((7,)), pltpu.SemaphoreType.DMA(())],
    )(parts)


def _reduce_scatter(parts, *, name):
    return _sum_parts(_exchange_parts(parts, name=name + "_x"), name=name + "_sum")


def _pad_cols(a, mult):
    pad = -a.shape[1] % mult
    return jnp.pad(a, ((0, 0), (0, pad))) if pad else a


def _pack(arrs, cols):
    flat = jnp.concatenate([a.reshape(-1).astype(F32) for a in arrs])
    sizes = [int(a.size) for a in arrs]
    flat = jnp.pad(flat, (0, -flat.shape[0] % (SUBLANES * cols)))
    return flat.reshape(-1, cols), sizes


def _unpack(flat2d, sizes, shapes):
    flat = flat2d.reshape(-1)
    out, o = [], 0
    for n, s in zip(sizes, shapes):
        out.append(flat[o:o + n].reshape(s))
        o += n
    return out


PACK_COLS = SUBLANES * LANES


def kernel(x, norm_mix_w, w_in, conv_a_w, conv_a_b, dt_bias, a_log, d_a, norm_a_w, w_proj_a, s5_lam_re, s5_lam_im, s5_log_dt, s5_b_re, s5_b_im, s5_c_re, s5_c_im, s5_d, w_s5_glu, w_out, norm_ffn_w, w_up, conv_ffn_w, conv_ffn_b, w_down, norm_final_w, loss_target, m_norm_mix_w, m_w_in, m_conv_a_w, m_conv_a_b, m_dt_bias, m_a_log, m_d_a, m_norm_a_w, m_w_proj_a, m_s5_lam_re, m_s5_lam_im, m_s5_log_dt, m_s5_b_re, m_s5_b_im, m_s5_c_re, m_s5_c_im, m_s5_d, m_w_s5_glu, m_w_out, m_norm_ffn_w, m_w_up, m_conv_ffn_w, m_conv_ffn_b, m_w_down, m_norm_final_w, v_norm_mix_w, v_w_in, v_conv_a_w, v_conv_a_b, v_dt_bias, v_a_log, v_d_a, v_norm_a_w, v_w_proj_a, v_s5_lam_re, v_s5_lam_im, v_s5_log_dt, v_s5_b_re, v_s5_b_im, v_s5_c_re, v_s5_c_im, v_s5_d, v_w_s5_glu, v_w_out, v_norm_ffn_w, v_w_up, v_conv_ffn_w, v_conv_ffn_b, v_w_down, v_norm_final_w):
    weights = dict(norm_mix_w=norm_mix_w, w_in=w_in, conv_a_w=conv_a_w, conv_a_b=conv_a_b, dt_bias=dt_bias, a_log=a_log, d_a=d_a, norm_a_w=norm_a_w, w_proj_a=w_proj_a, s5_lam_re=s5_lam_re, s5_lam_im=s5_lam_im, s5_log_dt=s5_log_dt, s5_b_re=s5_b_re, s5_b_im=s5_b_im, s5_c_re=s5_c_re, s5_c_im=s5_c_im, s5_d=s5_d, w_s5_glu=w_s5_glu, w_out=w_out, norm_ffn_w=norm_ffn_w, w_up=w_up, conv_ffn_w=conv_ffn_w, conv_ffn_b=conv_ffn_b, w_down=w_down, norm_final_w=norm_final_w)
    moms = dict(norm_mix_w=m_norm_mix_w, w_in=m_w_in, conv_a_w=m_conv_a_w, conv_a_b=m_conv_a_b, dt_bias=m_dt_bias, a_log=m_a_log, d_a=m_d_a, norm_a_w=m_norm_a_w, w_proj_a=m_w_proj_a, s5_lam_re=m_s5_lam_re, s5_lam_im=m_s5_lam_im, s5_log_dt=m_s5_log_dt, s5_b_re=m_s5_b_re, s5_b_im=m_s5_b_im, s5_c_re=m_s5_c_re, s5_c_im=m_s5_c_im, s5_d=m_s5_d, w_s5_glu=m_w_s5_glu, w_out=m_w_out, norm_ffn_w=m_norm_ffn_w, w_up=m_w_up, conv_ffn_w=m_conv_ffn_w, conv_ffn_b=m_conv_ffn_b, w_down=m_w_down, norm_final_w=m_norm_final_w)
    vars_ = dict(norm_mix_w=v_norm_mix_w, w_in=v_w_in, conv_a_w=v_conv_a_w, conv_a_b=v_conv_a_b, dt_bias=v_dt_bias, a_log=v_a_log, d_a=v_d_a, norm_a_w=v_norm_a_w, w_proj_a=v_w_proj_a, s5_lam_re=v_s5_lam_re, s5_lam_im=v_s5_lam_im, s5_log_dt=v_s5_log_dt, s5_b_re=v_s5_b_re, s5_b_im=v_s5_b_im, s5_c_re=v_s5_c_re, s5_c_im=v_s5_c_im, s5_d=v_s5_d, w_s5_glu=v_w_s5_glu, w_out=v_w_out, norm_ffn_w=v_norm_ffn_w, w_up=v_w_up, conv_ffn_w=v_conv_ffn_w, conv_ffn_b=v_conv_ffn_b, w_down=v_w_down, norm_final_w=v_norm_final_w)
    names = list(weights)
    col_sharded = ("w_in", "w_s5_glu", "w_up")
    row_sharded = ("w_proj_a", "w_out", "w_down")
    conv_sharded = ("conv_a_w", "conv_ffn_w")
    replicated = [n for n in names if n not in col_sharded + row_sharded + conv_sharded]

    x2, tgt = x[0], loss_target[0]
    t, d = x2.shape
    nh = dt_bias.shape[-1]
    d_inner = norm_a_w.shape[-1]
    conv_dim = conv_a_b.shape[-1]
    gn = (conv_dim - d_inner) // 2
    ng = gn // D_STATE
    nr = nh // ng
    rp = d_inner // ng
    d_s5 = s5_d.shape[-1]
    gs, ps = s5_lam_re.shape[1:]
    cs = d_s5 // gs
    n_oct = gs // 8
    assert 8 * ps == SCAN_COLS and 8 * cs == LANES and gs % 8 == 0
    d_ff = w_down.shape[1] * N_DEV
    dev = _flat(*_position())

    def gather_cols(w, name):
        g = _all_gather(w.astype(BF16), name=name)
        return jnp.transpose(g, (1, 0, 2)).reshape(w.shape[0], -1)

    def gather_rows(w, name):
        return _all_gather(w.astype(BF16), name=name).reshape(-1, w.shape[1])

    w_in_f = gather_cols(w_in[0], "ag_w_in")
    seg_names = ("z", "xs", "bm", "cm", "dt", "u", "ga", "gb")
    seg_sizes = (d_inner, d_inner, gn, gn, nh, d_s5, d, d)
    w_seg, o = {}, 0
    for sn, sz in zip(seg_names, seg_sizes):
        w_seg[sn] = _pad_cols(w_in_f[:, o:o + sz], LANES)
        o += sz
    w_proj = gather_rows(w_proj_a[0], "ag_w_proj_a")
    w_glu = gather_cols(w_s5_glu[0], "ag_w_s5_glu")
    w_glu_v, w_glu_g = w_glu[:, :d], w_glu[:, d:]
    w_o = gather_rows(w_out[0], "ag_w_out")
    w_u = gather_cols(w_up[0], "ag_w_up")
    w_up_g, w_up_v = w_u[:, :d_ff], w_u[:, d_ff:]
    w_dn = gather_rows(w_down[0], "ag_w_down")
    ka, kf = conv_a_w.shape[1], conv_ffn_w.shape[1]
    taps = jnp.concatenate([conv_a_w[0].reshape(1, -1), conv_ffn_w[0].reshape(1, -1)], axis=1)
    taps = _all_gather(taps, name="ag_conv_taps")[:, 0]
    na = ka * conv_a_w.shape[2]
    cw_a = jnp.transpose(taps[:, :na].reshape(N_DEV, ka, -1), (1, 0, 2)).reshape(ka, conv_dim)
    cw_f = jnp.transpose(taps[:, na:].reshape(N_DEV, kf, -1), (1, 0, 2)).reshape(kf, 2 * d_ff)
    cb_a, cb_f = conv_a_b, conv_ffn_b
    a_cols = {"xs": slice(0, d_inner), "bm": slice(d_inner, d_inner + gn), "cm": slice(d_inner + gn, conv_dim)}

    w1 = norm_mix_w.reshape(1, 1, d)
    hn1, = _blocked_fwd(_rms_fn, [x2], [w1], [(d, BF16)], tb=256, name="rms1")
    pre = {sn: _mm(hn1, w_seg[sn], name="in_" + sn) for sn in seg_names}
    act_a = {sn: _conv_fwd(_comb_silu, [pre[sn]], [cw_a[:, a_cols[sn]]], [cb_a[:, a_cols[sn]]], out_dtype=F32,
                           name="conv_a_" + sn) for sn in a_cols}
    dtr3 = jnp.transpose(pre["dt"][:, :nh].reshape(t, ng, nr), (1, 0, 2))
    dtb3, alog3, dsk3 = (p.reshape(ng, 1, nr) for p in (dt_bias, a_log, d_a))
    nw3 = norm_a_w.reshape(ng, 1, rp)
    yn, hsave = _ssd_fwd(act_a["xs"], act_a["bm"], act_a["cm"], pre["z"], dtr3, dtb3, alog3, dsk3, nw3, name="ssd")
    y_a = _mm(yn, w_proj, name="proj_a")

    lam_re3, lam_im3 = s5_lam_re[0][:, None, :], s5_lam_im[0][:, None, :]
    logdt3 = s5_log_dt[0][:, None, None]
    bt_re, bt_im = jnp.transpose(s5_b_re[0], (0, 2, 1)), jnp.transpose(s5_b_im[0], (0, 2, 1))
    lb_re3, lb_im3, bb_re, bb_im = _s5_params(lam_re3, lam_im3, logdt3, bt_re, bt_im, name="s5_params")
    eye = jnp.eye(8, dtype=F32)

    def diag_b(bt):
        return (bt.reshape(n_oct, 8, cs, 1, ps) * eye[None, :, None, :, None]).reshape(n_oct, 8 * cs, 8 * ps)

    def undiag_b(blk):
        return (blk.reshape(n_oct, 8, cs, 8, ps) * eye[None, :, None, :, None]).sum(axis=3).reshape(gs, cs, ps)

    def diag_c(cm):
        ct = jnp.transpose(cm.reshape(n_oct, 8, cs, ps), (0, 1, 3, 2))
        return (ct[:, :, :, None, :] * eye[None, :, None, :, None]).reshape(n_oct, 8 * ps, 8 * cs)

    def undiag_c(blk):
        ct = (blk.reshape(n_oct, 8, ps, 8, cs) * eye[None, :, None, :, None]).sum(axis=3)
        return jnp.transpose(ct, (0, 1, 3, 2)).reshape(gs, cs, ps)

    b_blk_re, b_blk_im = diag_b(bb_re), diag_b(bb_im)
    c_blk_re, c_blk_imn = diag_c(s5_c_re[0]), diag_c(-s5_c_im[0])
    d3 = s5_d.reshape(n_oct, 1, LANES)
    lb_re, lb_im = lb_re3.reshape(1, gs * ps), lb_im3.reshape(1, gs * ps)
    u = pre["u"]
    bu_re, bu_im = _blocked_fwd(_s5_bu_fn, [u], [b_blk_re, b_blk_im], [(gs * ps, F32)] * 2, nj=n_oct, tb=512,
                                name="s5_bu")
    s_re, s_im = _s5_scan_fwd(bu_re, bu_im, lb_re, lb_im, name="s5_scan")
    yb, = _blocked_fwd(_s5_out_fn, [s_re, s_im, u], [c_blk_re, c_blk_imn, d3], [(d_s5, BF16)], nj=n_oct, tb=512,
                       name="s5_out")
    glu_v = _mm(yb, w_glu_v, name="glu_v")
    glu_g = _mm(yb, w_glu_g, name="glu_g")
    merged, = _blocked_fwd(_merge_fn, [glu_v, glu_g, pre["ga"], pre["gb"], y_a], [], [(d, BF16)], tb=256,
                           name="merge")
    h1 = _mm(merged, w_o, acc=x2, name="out_proj")
    w2 = norm_ffn_w.reshape(1, 1, d)
    hn2, = _blocked_fwd(_rms_fn, [h1], [w2], [(d, BF16)], tb=256, name="rms2")
    up_g = _mm(hn2, w_up_g, name="up_g")
    up_v = _mm(hn2, w_up_v, name="up_v")
    f_w = [cw_f[:, :d_ff], cw_f[:, d_ff:]]
    f_b = [cb_f[:, :d_ff], cb_f[:, d_ff:]]
    act = _conv_fwd(_comb_glu, [up_g, up_v], f_w, f_b, out_dtype=BF16, name="conv_ffn")
    h2 = _mm(act, w_dn, acc=h1, name="down")
    loss_tile, dh2, g_final = _loss_head(h2, norm_final_w.reshape(1, d), tgt, name="loss_head")

    grads = {}
    d_act = _mm(dh2, w_dn, tb=True, name="d_act")
    g_down = _mm(act, dh2, ta=True, name="g_w_down")
    (dup_g, dwf_g, dbf_g), (dup_v, dwf_v, dbf_v) = _conv_bwd(
        _comb_glu, [up_g, up_v], f_w, f_b, d_act, dx_dtype=BF16, name="conv_ffn_bwd")
    dhn2 = _mm(dup_g, w_up_g, tb=True, name="d_hn2_g")
    dhn2 = _mm(dup_v, w_up_v, tb=True, acc=dhn2, name="d_hn2_v")
    g_up = jnp.concatenate([_mm(hn2, dup_g, ta=True, name="g_w_up_g"), _mm(hn2, dup_v, ta=True, name="g_w_up_v")],
                           axis=1)
    dh1, g_w2 = _blocked_bwd(_rms_fn, [h1], [w2], [dhn2], [F32], adds={0: dh2}, tb=256, name="rms2_bwd")
    d_merged = _mm(dh1, w_o, tb=True, name="d_merged")
    g_out = _mm(merged, dh1, ta=True, name="g_w_out")
    dglu_v, dglu_g, dga, dgb, dy_a = _blocked_bwd(
        _merge_fn, [glu_v, glu_g, pre["ga"], pre["gb"], y_a], [], [d_merged], [BF16] * 5, tb=128, name="merge_bwd")
    dyb = _mm(dglu_v, w_glu_v, tb=True, name="d_yb_v")
    dyb = _mm(dglu_g, w_glu_g, tb=True, acc=dyb, name="d_yb_g")
    g_glu = jnp.concatenate([_mm(yb, dglu_v, ta=True, name="g_w_glu_v"), _mm(yb, dglu_g, ta=True, name="g_w_glu_g")],
                            axis=1)
    ds_re, ds_im, du_skip, dc_blk_re, dc_blk_imn, dd3 = _blocked_bwd(
        _s5_out_fn, [s_re, s_im, u], [c_blk_re, c_blk_imn, d3], [dyb], [F32, F32, F32], nj=n_oct, tb=512,
        name="s5_out_bwd")
    dbu_re, dbu_im, dlb_re, dlb_im = _s5_scan_bwd(s_re, s_im, ds_re, ds_im, lb_re, lb_im, name="s5_scan_bwd")
    du, db_blk_re, db_blk_im = _blocked_bwd(
        _s5_bu_fn, [u], [b_blk_re, b_blk_im], [dbu_re, dbu_im], [BF16], adds={0: du_skip}, nj=n_oct, tb=512,
        name="s5_bu_bwd")
    g_lre, g_lim, g_ldt, g_bt_re, g_bt_im = _s5_params(
        lam_re3, lam_im3, logdt3, bt_re, bt_im,
        cts=(dlb_re.reshape(gs, 1, ps), dlb_im.reshape(gs, 1, ps), undiag_b(db_blk_re), undiag_b(db_blk_im)),
        name="s5_params_bwd")
    grads["s5_lam_re"], grads["s5_lam_im"] = g_lre.reshape(s5_lam_re.shape), g_lim.reshape(s5_lam_im.shape)
    grads["s5_log_dt"] = g_ldt.reshape(s5_log_dt.shape)
    grads["s5_b_re"] = jnp.transpose(g_bt_re, (0, 2, 1)).reshape(s5_b_re.shape)
    grads["s5_b_im"] = jnp.transpose(g_bt_im, (0, 2, 1)).reshape(s5_b_im.shape)
    grads["s5_c_re"] = undiag_c(dc_blk_re).reshape(s5_c_re.shape)
    grads["s5_c_im"] = -undiag_c(dc_blk_imn).reshape(s5_c_im.shape)
    grads["s5_d"] = dd3.reshape(s5_d.shape)

    dyn = _mm(dy_a, w_proj, tb=True, name="d_yn")
    g_proj = _mm(yn, dy_a, ta=True, name="g_w_proj_a")
    dxs, dbm, dcm, dz, ddtr3, g_dtb, g_alog, g_dsk, g_nw = _ssd_bwd(
        act_a["xs"], act_a["bm"], act_a["cm"], pre["z"], dtr3, hsave, dtb3, alog3, dsk3, nw3, dyn, name="ssd_bwd")
    grads["dt_bias"], grads["a_log"], grads["d_a"] = (g.reshape(1, nh) for g in (g_dtb, g_alog, g_dsk))
    grads["norm_a_w"] = g_nw.reshape(1, d_inner)
    dpre = {"z": dz, "u": du, "ga": dga, "gb": dgb}
    dcw, dcb = {}, {}
    for sn, dact in (("xs", dxs), ("bm", dbm), ("cm", dcm)):
        (dpre[sn], dcw[sn], dcb[sn]), = _conv_bwd(
            _comb_silu, [pre[sn]], [cw_a[:, a_cols[sn]]], [cb_a[:, a_cols[sn]]], dact, dx_dtype=BF16,
            name="conv_a_bwd_" + sn)
    dpre["dt"] = _pad_cols(jnp.transpose(ddtr3, (1, 0, 2)).reshape(t, nh), LANES).astype(BF16)
    dhn1 = None
    g_segs = []
    for sn, sz in zip(seg_names, seg_sizes):
        dhn1 = _mm(dpre[sn], w_seg[sn], tb=True, acc=dhn1, name="d_hn1_" + sn)
        g_segs.append(_mm(hn1, dpre[sn], ta=True, name="g_w_in_" + sn)[:, :sz])
    g_in = jnp.concatenate(g_segs, axis=1)
    dx, g_w1 = _blocked_bwd(_rms_fn, [x2], [w1], [dhn1], [F32], adds={0: dh1}, tb=256, name="rms1_bwd")

    grads["norm_mix_w"], grads["norm_ffn_w"] = g_w1.reshape(1, d), g_w2.reshape(1, d)
    grads["norm_final_w"] = g_final.reshape(d)
    grads["conv_a_b"] = jnp.concatenate([dcb["xs"], dcb["bm"], dcb["cm"]], axis=1)
    grads["conv_ffn_b"] = jnp.concatenate([dbf_g, dbf_v], axis=1)
    g_cw_a = jnp.concatenate([dcw["xs"], dcw["bm"], dcw["cm"]], axis=1)
    g_cw_f = jnp.concatenate([dwf_g, dwf_v], axis=1)

    def scatter_cols(g, name):
        parts = jnp.transpose(g.reshape(g.shape[0], N_DEV, -1), (1, 0, 2)).astype(BF16)
        return _reduce_scatter(parts, name=name)[None]

    def scatter_rows(g, name):
        return _reduce_scatter(g.reshape(N_DEV, -1, g.shape[1]).astype(BF16), name=name)[None]

    grads["w_in"] = scatter_cols(g_in, "rs_w_in")
    grads["w_s5_glu"] = scatter_cols(g_glu, "rs_w_s5_glu")
    grads["w_up"] = scatter_cols(g_up, "rs_w_up")
    grads["w_proj_a"] = scatter_rows(g_proj, "rs_w_proj_a")
    grads["w_out"] = scatter_rows(g_out, "rs_w_out")
    grads["w_down"] = scatter_rows(g_down, "rs_w_down")

    small = [grads[n] for n in replicated] + [g_cw_a, g_cw_f, loss_tile[:1, :1]]
    packed, sizes = _pack(small, PACK_COLS)
    summed = _sum_parts(_all_gather(packed, name="ag_small_grads"), name="sum_small_grads")
    *rep_sums, s_cw_a, s_cw_f, loss = _unpack(summed, sizes, [a.shape for a in small])
    for n, g in zip(replicated, rep_sums):
        grads[n] = g
    wa, wf = conv_a_w.shape[2], conv_ffn_w.shape[2]
    grads["conv_a_w"] = lax.dynamic_slice_in_dim(s_cw_a, dev * wa, wa, axis=1)[None]
    grads["conv_ffn_w"] = lax.dynamic_slice_in_dim(s_cw_f, dev * wf, wf, axis=1)[None]

    delta, new_m, new_v = {}, {}, {}
    for n in col_sharded + row_sharded:
        shape = weights[n].shape
        two_d = lambda a: a.reshape(shape[-2], shape[-1])
        dl, nm, nv = _adamw(two_d(weights[n]), two_d(grads[n]), two_d(moms[n]), two_d(vars_[n]), name="adamw_" + n)
        delta[n], new_m[n], new_v[n] = dl.reshape(shape), nm.reshape(shape), nv.reshape(shape)
    small_names = replicated + list(conv_sharded)
    shapes = [weights[n].shape for n in small_names]
    pw, sizes = _pack([weights[n] for n in small_names], PACK_COLS)
    pg, _ = _pack([grads[n] for n in small_names], PACK_COLS)
    pm, _ = _pack([moms[n] for n in small_names], PACK_COLS)
    pv, _ = _pack([vars_[n] for n in small_names], PACK_COLS)
    dl, nm, nv = _adamw(pw, pg, pm, pv, name="adamw_small")
    for n, a, b, c in zip(small_names, _unpack(dl, sizes, shapes), _unpack(nm, sizes, shapes),
                          _unpack(nv, sizes, shapes)):
        delta[n], new_m[n], new_v[n] = a, b, c

    return (loss.reshape(()), dx[None], *[grads[n] for n in names], *[delta[n] for n in names],
            *[new_m[n] for n in names], *[new_v[n] for n in names])
```

```python
import functools

import jax
import jax.numpy as jnp
from jax import lax
from jax.experimental import pallas as pl
from jax.experimental.pallas import tpu as pltpu

F32 = jnp.float32
BF16 = jnp.bfloat16
HIGHEST = lax.Precision.HIGHEST
MESH = pl.DeviceIdType.MESH

EPS = 1e-6
EIG_MAX = -1e-4
D_STATE = 128
CHUNK = 256
ADAM_LR = 0.001
ADAM_B1 = 0.9
ADAM_B2 = 0.999
ADAM_EPS = 1e-08
ADAM_WD = 0.01
ADAM_STEP = 10
N_DEV = 8
LANES = 128
SUBLANES = 8
VMEM_LIMIT = 56 * 1024 * 1024
MM_MAX_K = 4096


def _cp(*sem):
    return pltpu.CompilerParams(dimension_semantics=sem, vmem_limit_bytes=VMEM_LIMIT)


def _tile(dim, pref, unit=LANES):
    if dim <= unit:
        return dim
    t = (min(pref, dim) // unit) * unit
    while dim % t:
        t -= unit
    return t


_DIMS = {"nn": (((1,), (0,)), ((), ())), "nt": (((1,), (1,)), ((), ())), "tn": (((0,), (0,)), ((), ()))}


def _dot(a, b, kind):
    return lax.dot_general(a.astype(BF16), b.astype(BF16), _DIMS[kind], preferred_element_type=F32)


@functools.partial(jax.custom_vjp, nondiff_argnums=(2,))
def _bdot(a, b, kind):
    return _dot(a, b, kind)


def _bdot_fwd(a, b, kind):
    return _dot(a, b, kind), (a, b)


def _bdot_bwd(kind, res, g):
    a, b = res
    if kind == "nn":
        return _dot(g, b, "nt"), _dot(a, g, "tn")
    if kind == "nt":
        return _dot(g, b, "nn"), _dot(g, a, "tn")
    return _dot(b, g, "nt"), _dot(a, g, "nn")


_bdot.defvjp(_bdot_fwd, _bdot_bwd)


def _mm(a, b, *, ta=False, tb=False, acc=None, out_dtype=F32, name, b_win=None, into=None):
    assert not (ta and tb)
    m, k = (a.shape[1], a.shape[0]) if ta else a.shape
    b_off, b_size = b_win or (0, b.shape[1])
    n = b.shape[0] if tb else b_size
    assert (b_size if tb else b.shape[0]) == k, (a.shape, b.shape, ta, tb, b_win)
    o_off = into[1] if into else 0
    nk = -(-k // MM_MAX_K)
    while k % nk or (k // nk) % LANES or (tb and b_off % (k // nk)):
        nk += 1
    tk = k // nk
    tm, tn = _tile(m, 1024 if tk <= 2048 else 512), _tile(n, 1024)
    while o_off % tn or (not tb and b_off % tn):
        tn = _tile(n, tn - LANES)
    kind = "tn" if ta else ("nt" if tb else "nn")
    a_spec = pl.BlockSpec((tk, tm), lambda i, j, l: (l, i)) if ta else pl.BlockSpec((tm, tk), lambda i, j, l: (i, l))
    if tb:
        b_spec = pl.BlockSpec((tn, tk), lambda i, j, l: (j, l + b_off // tk))
    else:
        b_spec = pl.BlockSpec((tk, tn), lambda i, j, l: (l, j + b_off // tn))
    c_spec = pl.BlockSpec((tm, tn), lambda i, j, l: (i, j))
    o_spec = pl.BlockSpec((tm, tn), lambda i, j, l: (i, j + o_off // tn))
    has_acc = acc is not None

    def body(*refs):
        a_ref, b_ref = refs[:2]
        c_ref = refs[2] if has_acc else None
        o_ref = refs[2 + has_acc + (into is not None)]
        if nk == 1:
            res = _dot(a_ref[...], b_ref[...], kind)
            if has_acc:
                res = res + c_ref[...].astype(F32)
            o_ref[...] = res.astype(o_ref.dtype)
            return
        acc_ref = refs[-1]
        l = pl.program_id(2)

        @pl.when(l == 0)
        def _():
            if has_acc:
                acc_ref[...] = c_ref[...].astype(F32)
            else:
                acc_ref[...] = jnp.zeros_like(acc_ref)

        acc_ref[...] += _dot(a_ref[...], b_ref[...], kind)

        @pl.when(l == nk - 1)
        def _():
            o_ref[...] = acc_ref[...].astype(o_ref.dtype)

    ins = [a, b] + ([acc] if has_acc else []) + ([into[0]] if into else [])
    in_specs = [a_spec, b_spec] + ([c_spec] if has_acc else []) + ([pl.BlockSpec(memory_space=pl.ANY)] if into else [])
    out_shape = jax.ShapeDtypeStruct(into[0].shape, into[0].dtype) if into else jax.ShapeDtypeStruct((m, n), out_dtype)
    return pl.pallas_call(
        body, name=name, grid=(m // tm, n // tn, nk), in_specs=in_specs, out_specs=o_spec, out_shape=out_shape,
        input_output_aliases={len(ins) - 1: 0} if into else {},
        scratch_shapes=[pltpu.VMEM((tm, tn), F32)] if nk > 1 else [],
        compiler_params=_cp("parallel", "parallel", "arbitrary"),
    )(*ins)


def _w_in_pieces(seg_sizes, seg_order, n_blk):
    layout, o = {}, 0
    for sn in seg_order:
        width = -(-seg_sizes[sn] // LANES) * LANES
        layout[sn] = (o, width)
        o += width
    pieces, start = [], 0
    for sn, sz in seg_sizes.items():
        lo = start
        while lo < start + sz:
            blk = lo // n_blk
            hi = min(start + sz, (blk + 1) * n_blk)
            pieces.append((blk, lo - blk * n_blk, sn, lo - start, layout[sn][0] + lo - start, hi - lo))
            lo = hi
        start += sz
    return pieces, layout


def _w_in_pack(gathered, pieces, layout, seg_sizes, *, name, tr=256):
    _, k, n_blk = gathered.shape
    n_pad = sum(w for _, w in layout.values())

    def body(g_ref, o_ref):
        for sn, (off, width) in layout.items():
            if width != seg_sizes[sn]:
                o_ref[:, pl.ds(off + seg_sizes[sn], width - seg_sizes[sn])] = jnp.zeros(
                    (tr, width - seg_sizes[sn]), o_ref.dtype)
        for blk, src, _, _, dst, width in pieces:
            o_ref[:, pl.ds(dst, width)] = g_ref[blk, :, pl.ds(src, width)]

    return pl.pallas_call(
        body, name=name, grid=(k // tr,), in_specs=[pl.BlockSpec((N_DEV, tr, n_blk), lambda i: (0, i, 0))],
        out_specs=pl.BlockSpec((tr, n_pad), lambda i: (i, 0)), out_shape=jax.ShapeDtypeStruct((k, n_pad), gathered.dtype),
        compiler_params=_cp("parallel"),
    )(gathered)


def _w_in_unpack(seg_grads, pieces, n_blk, *, name, tr=128):
    names = list(seg_grads)
    k = seg_grads[names[0]].shape[0]

    def body(*refs):
        o_ref = refs[-1]
        seg_ref = dict(zip(names, refs))
        for blk, dst, sn, src, _, width in pieces:
            o_ref[blk, :, pl.ds(dst, width)] = seg_ref[sn][:, pl.ds(src, width)].astype(o_ref.dtype)

    return pl.pallas_call(
        body, name=name, grid=(k // tr,),
        in_specs=[pl.BlockSpec((tr, seg_grads[sn].shape[1]), lambda i: (i, 0)) for sn in names],
        out_specs=pl.BlockSpec((N_DEV, tr, n_blk), lambda i: (0, i, 0)),
        out_shape=jax.ShapeDtypeStruct((N_DEV, k, n_blk), BF16), compiler_params=_cp("parallel"),
    )(*[seg_grads[sn] for sn in names])


def _row_spec(arr, tb, nj):
    return pl.BlockSpec((tb, arr.shape[1] // nj), lambda j, i: (i, j))


def _par_spec(arr):
    return pl.BlockSpec((1,) + arr.shape[1:], lambda j, i: (j, 0, 0))


def _blocked_fwd(fn, rows, params, outs, *, nj=1, tb, name):
    t = rows[0].shape[0]
    nr, npar = len(rows), len(params)

    def body(*refs):
        res = fn(*[r[...] for r in refs[:nr]], *[p[0] for p in refs[nr:nr + npar]])
        for o_ref, val in zip(refs[nr + npar:], res):
            o_ref[...] = val.astype(o_ref.dtype)

    return pl.pallas_call(
        body, name=name, grid=(nj, t // tb),
        in_specs=[_row_spec(a, tb, nj) for a in rows] + [_par_spec(p) for p in params],
        out_specs=[pl.BlockSpec((tb, c // nj), lambda j, i: (i, j)) for c, _ in outs],
        out_shape=[jax.ShapeDtypeStruct((t, c), dt) for c, dt in outs],
        compiler_params=_cp("parallel", "arbitrary"),
    )(*rows, *params)


def _blocked_bwd(fn, rows, params, cts, row_grad_dtypes, *, adds=None, nj=1, tb, name):
    t = rows[0].shape[0]
    nr, npar, nct = len(rows), len(params), len(cts)
    adds = adds or {}
    add_keys = sorted(adds)
    want, want_dtypes = [], []
    for k, dts in enumerate(row_grad_dtypes):
        for dt in (dts if isinstance(dts, tuple) else (dts,)):
            if dt is not None:
                want.append(k)
                want_dtypes.append(dt)

    def body(*refs):
        row_refs = refs[:nr]
        par_refs = refs[nr:nr + npar]
        ct_refs = refs[nr + npar:nr + npar + nct]
        add_refs = dict(zip(add_keys, refs[nr + npar + nct:nr + npar + nct + len(add_keys)]))
        out_refs = refs[nr + npar + nct + len(add_keys):]
        _, vjp = jax.vjp(fn, *[r[...] for r in row_refs], *[p[0] for p in par_refs])
        grads = vjp(tuple(c[...].astype(F32) for c in ct_refs))
        for o_ref, k in zip(out_refs, want):
            g = grads[k]
            if k in add_refs:
                g = g + add_refs[k][...].astype(F32)
            o_ref[...] = g.astype(o_ref.dtype)
        first = pl.program_id(1) == 0
        for o_ref, g in zip(out_refs[len(want):], grads[nr:]):
            @pl.when(first)
            def _(o_ref=o_ref):
                o_ref[...] = jnp.zeros_like(o_ref)
            o_ref[0] += g

    add_arrs = [adds[k] for k in add_keys]
    return pl.pallas_call(
        body, name=name, grid=(nj, t // tb),
        in_specs=[_row_spec(a, tb, nj) for a in rows] + [_par_spec(p) for p in params]
        + [_row_spec(c, tb, nj) for c in cts] + [_row_spec(a, tb, nj) for a in add_arrs],
        out_specs=[_row_spec(rows[k], tb, nj) for k in want] + [_par_spec(p) for p in params],
        out_shape=[jax.ShapeDtypeStruct(rows[k].shape, dt) for k, dt in zip(want, want_dtypes)]
        + [jax.ShapeDtypeStruct(p.shape, F32) for p in params],
        compiler_params=_cp("parallel", "arbitrary"),
    )(*rows, *params, *cts, *add_arrs)


def _rms_fn(x, w):
    return (x * lax.rsqrt(jnp.mean(x * x, axis=-1, keepdims=True) + EPS) * w,)


def _silu(x):
    return x * jax.nn.sigmoid(x)


def _merge_fn(glu_v, glu_g, g_a, g_b, y_a):
    y_b = glu_v * jax.nn.sigmoid(glu_g)
    return (jax.nn.sigmoid(g_a) * y_a + jax.nn.sigmoid(g_b) * y_b,)


def _s5_bu_fn(u, b_re, b_im):
    return _bdot(u, b_re, "nn"), _bdot(u, b_im, "nn")


def _s5_out_fn(s_re, s_im, u, c_re, c_im_neg, d):
    return (jax.nn.gelu(_bdot(s_re, c_re, "nn") + _bdot(s_im, c_im_neg, "nn") + d * u),)


HALO = SUBLANES


def _conv_fn(comb, kw, ns):
    def fn(*args):
        cs = []
        for s in range(ns):
            xp, xm, w, b = args[4 * s:4 * s + 4]
            xe = jnp.concatenate([xp, xm], axis=0)
            tb = xm.shape[0]
            y = b
            for k in range(kw):
                off = HALO - kw + 1 + k
                y = y + w[k:k + 1, :] * xe[off:off + tb, :]
            cs.append(y)
        return comb(*cs)
    return fn


def _conv_specs(xs, ws, bs, tb, cb, time_of):
    specs = []
    for x, w, b in zip(xs, ws, bs):
        specs += [
            pl.BlockSpec((HALO, cb), lambda j, i: (jnp.maximum(time_of(i) * (tb // HALO) - 1, 0), j)),
            pl.BlockSpec((tb, cb), lambda j, i: (time_of(i), j)),
            pl.BlockSpec((w.shape[0], cb), lambda j, i: (0, j)),
            pl.BlockSpec((1, cb), lambda j, i: (0, j)),
        ]
    return specs


def _conv_fwd(comb, xs, ws, bs, *, out_dtype, name, tb=512):
    t, c = xs[0].shape
    cb = _tile(c, 512)
    ns = len(xs)
    fn = _conv_fn(comb, ws[0].shape[0], ns)

    def body(*refs):
        i = pl.program_id(1)
        args = []
        for s in range(ns):
            xp_ref, xm_ref, w_ref, b_ref = refs[4 * s:4 * s + 4]
            xp = jnp.where(i == 0, 0.0, xp_ref[...])
            args += [xp, xm_ref[...], w_ref[...], b_ref[...]]
        refs[4 * ns][...] = fn(*args).astype(out_dtype)

    flat = [a for x, w, b in zip(xs, ws, bs) for a in (x, x, w, b)]
    return pl.pallas_call(
        body, name=name, grid=(c // cb, t // tb),
        in_specs=_conv_specs(xs, ws, bs, tb, cb, lambda i: i),
        out_specs=pl.BlockSpec((tb, cb), lambda j, i: (i, j)),
        out_shape=jax.ShapeDtypeStruct((t, c), out_dtype),
        compiler_params=_cp("parallel", "arbitrary"),
    )(*flat)


def _conv_bwd(comb, xs, ws, bs, dy, *, dx_dtype, name, tb=512):
    t, c = xs[0].shape
    cb = _tile(c, 512)
    ns = len(xs)
    nt = t // tb
    fn = _conv_fn(comb, ws[0].shape[0], ns)

    def body(*refs):
        step = pl.program_id(1)
        in_refs = refs[:4 * ns]
        dy_ref = refs[4 * ns]
        out_refs = refs[4 * ns + 1:4 * ns + 1 + 3 * ns]
        carry_refs = refs[4 * ns + 1 + 3 * ns:]
        args = []
        for s in range(ns):
            xp_ref, xm_ref, w_ref, b_ref = in_refs[4 * s:4 * s + 4]
            xp = jnp.where(step == nt - 1, 0.0, xp_ref[...])
            args += [xp, xm_ref[...], w_ref[...], b_ref[...]]
        _, vjp = jax.vjp(fn, *args)
        grads = vjp(dy_ref[...].astype(F32))
        for s in range(ns):
            dxp, dxm, dw, db = grads[4 * s:4 * s + 4]
            dx_ref, dw_ref, db_ref = out_refs[3 * s:3 * s + 3]
            carry = carry_refs[s]

            @pl.when(step == 0)
            def _(carry=carry, dw_ref=dw_ref, db_ref=db_ref):
                carry[...] = jnp.zeros_like(carry)
                dw_ref[...] = jnp.zeros_like(dw_ref)
                db_ref[...] = jnp.zeros_like(db_ref)

            tail = jnp.concatenate([jnp.zeros((tb - HALO, cb), F32), carry[...]], axis=0)
            dx_ref[...] = (dxm + tail).astype(dx_dtype)
            carry[...] = dxp
            dw_ref[...] += dw
            db_ref[...] += db

    flat = [a for x, w, b in zip(xs, ws, bs) for a in (x, x, w, b)]
    rev = lambda i: nt - 1 - i
    out_specs, out_shape = [], []
    for x, w, b in zip(xs, ws, bs):
        out_specs += [pl.BlockSpec((tb, cb), lambda j, i: (rev(i), j)),
                      pl.BlockSpec((w.shape[0], cb), lambda j, i: (0, j)),
                      pl.BlockSpec((1, cb), lambda j, i: (0, j))]
        out_shape += [jax.ShapeDtypeStruct((t, c), dx_dtype), jax.ShapeDtypeStruct(w.shape, F32),
                      jax.ShapeDtypeStruct(b.shape, F32)]
    res = pl.pallas_call(
        body, name=name, grid=(c // cb, nt),
        in_specs=_conv_specs(xs, ws, bs, tb, cb, rev) + [pl.BlockSpec((tb, cb), lambda j, i: (rev(i), j))],
        out_specs=out_specs, out_shape=out_shape,
        scratch_shapes=[pltpu.VMEM((HALO, cb), F32) for _ in range(ns)],
        compiler_params=_cp("parallel", "arbitrary"),
    )(*flat, dy)
    return [tuple(res[3 * s:3 * s + 3]) for s in range(ns)]


def _comb_silu(c):
    return _silu(c)


def _comb_glu(cg, cv):
    return _silu(cg) * cv


def _ssd_fn(nheads, hdim):
    def fn(x, bm, cm, z, dtr, hin, dtb, alog, dsk, nw):
        q = x.shape[0]
        dt = jax.nn.softplus(dtr + dtb)
        da = dt * (-jnp.exp(alog))
        li = lax.broadcasted_iota(jnp.int32, (q, q), 0)
        si = lax.broadcasted_iota(jnp.int32, (q, q), 1)
        causal = li >= si
        tri = causal.astype(F32)
        acs = jnp.dot(tri, da, precision=HIGHEST, preferred_element_type=F32)
        acs_row = lax.dot_general(da, tri, (((0,), (1,)), ((), ())), precision=HIGHEST,
                                  preferred_element_type=F32)
        cb = _bdot(cm, bm, "nt")
        ch = _bdot(cm, hin, "nn")
        ys, hs = [], []
        for r in range(nheads):
            cols = slice(r * hdim, (r + 1) * hdim)
            xr = x[:, cols]
            a_col = acs[:, r:r + 1]
            decay = jnp.exp(jnp.where(causal, a_col - acs_row[r:r + 1, :], -1e30))
            xd = xr * dt[:, r:r + 1]
            y_diag = _bdot(cb * decay, xd, "nn")
            y_off = ch[:, cols] * jnp.exp(a_col)
            last = acs[q - 1:q, r:r + 1]
            st = _bdot(bm * jnp.exp(last - a_col), xd, "tn")
            hs.append(jnp.exp(last) * hin[:, cols] + st)
            ys.append(y_diag + y_off + dsk[:, r:r + 1] * xr)
        y = jnp.concatenate(ys, axis=1) * _silu(z)
        yn = y * lax.rsqrt(jnp.mean(y * y, axis=-1, keepdims=True) + EPS) * nw
        return yn, jnp.concatenate(hs, axis=1)
    return fn


def _ssd_specs(rp, nr, time_of):
    row = lambda w: pl.BlockSpec((CHUNK, w), lambda g, c: (time_of(c), g))
    par = lambda w: pl.BlockSpec((1, 1, w), lambda g, c: (g, 0, 0))
    return dict(
        x=row(rp), bc=row(D_STATE), dtr=pl.BlockSpec((1, CHUNK, nr), lambda g, c: (g, time_of(c), 0)),
        h=pl.BlockSpec((1, 1, D_STATE, rp), lambda g, c: (g, time_of(c), 0, 0)), pr=par(nr), pw=par(rp))


def _ssd_fwd(xs, bm, cm, z, dtr, dtb, alog, dsk, nw, *, name):
    t = xs.shape[0]
    g, _, nr = dtr.shape
    rp = xs.shape[1] // g
    nc = t // CHUNK
    fn = _ssd_fn(nr, rp // nr)
    sp = _ssd_specs(rp, nr, lambda c: c)

    def body(x_ref, b_ref, c_ref, z_ref, dtr_ref, dtb_ref, al_ref, dsk_ref, nw_ref, yn_ref, hs_ref, h_ref):
        @pl.when(pl.program_id(1) == 0)
        def _():
            h_ref[...] = jnp.zeros_like(h_ref)
        hin = h_ref[...]
        hs_ref[0, 0] = hin
        yn, hout = fn(x_ref[...], b_ref[...], c_ref[...], z_ref[...], dtr_ref[0], hin,
                      dtb_ref[0], al_ref[0], dsk_ref[0], nw_ref[0])
        yn_ref[...] = yn.astype(yn_ref.dtype)
        h_ref[...] = hout

    return pl.pallas_call(
        body, name=name, grid=(g, nc),
        in_specs=[sp["x"], sp["bc"], sp["bc"], sp["x"], sp["dtr"], sp["pr"], sp["pr"], sp["pr"], sp["pw"]],
        out_specs=[sp["x"], sp["h"]],
        out_shape=[jax.ShapeDtypeStruct(xs.shape, BF16), jax.ShapeDtypeStruct((g, nc, D_STATE, rp), F32)],
        scratch_shapes=[pltpu.VMEM((D_STATE, rp), F32)],
        compiler_params=_cp("parallel", "arbitrary"),
    )(xs, bm, cm, z, dtr, dtb, alog, dsk, nw)


def _ssd_bwd(xs, bm, cm, z, dtr, hsave, dtb, alog, dsk, nw, dyn, *, name):
    t = xs.shape[0]
    g, _, nr = dtr.shape
    rp = xs.shape[1] // g
    nc = t // CHUNK
    fn = _ssd_fn(nr, rp // nr)
    sp = _ssd_specs(rp, nr, lambda c: nc - 1 - c)

    def body(x_ref, b_ref, c_ref, z_ref, dtr_ref, hs_ref, dtb_ref, al_ref, dsk_ref, nw_ref, dyn_ref,
             dx_ref, db_ref, dc_ref, dz_ref, ddtr_ref, ddtb_ref, dal_ref, ddsk_ref, dnw_ref, dh_ref):
        first = pl.program_id(1) == 0

        @pl.when(first)
        def _():
            dh_ref[...] = jnp.zeros_like(dh_ref)
            for r in (ddtb_ref, dal_ref, ddsk_ref, dnw_ref):
                r[...] = jnp.zeros_like(r)

        _, vjp = jax.vjp(fn, x_ref[...], b_ref[...], c_ref[...], z_ref[...], dtr_ref[0], hs_ref[0, 0],
                         dtb_ref[0], al_ref[0], dsk_ref[0], nw_ref[0])
        dx, db, dc, dz, ddtr, dhin, ddtb, dal, ddsk, dnw = vjp((dyn_ref[...].astype(F32), dh_ref[...]))
        dx_ref[...] = dx
        db_ref[...] = db
        dc_ref[...] = dc
        dz_ref[...] = dz.astype(dz_ref.dtype)
        ddtr_ref[0] = ddtr
        dh_ref[...] = dhin
        ddtb_ref[0] += ddtb
        dal_ref[0] += dal
        ddsk_ref[0] += ddsk
        dnw_ref[0] += dnw

    sd = jax.ShapeDtypeStruct
    return pl.pallas_call(
        body, name=name, grid=(g, nc),
        in_specs=[sp["x"], sp["bc"], sp["bc"], sp["x"], sp["dtr"], sp["h"], sp["pr"], sp["pr"], sp["pr"], sp["pw"],
                  sp["x"]],
        out_specs=[sp["x"], sp["bc"], sp["bc"], sp["x"], sp["dtr"], sp["pr"], sp["pr"], sp["pr"], sp["pw"]],
        out_shape=[sd(xs.shape, F32), sd(bm.shape, F32), sd(cm.shape, F32), sd(z.shape, BF16), sd(dtr.shape, F32),
                   sd(dtb.shape, F32), sd(alog.shape, F32), sd(dsk.shape, F32), sd(nw.shape, F32)],
        scratch_shapes=[pltpu.VMEM((D_STATE, rp), F32)],
        compiler_params=_cp("parallel", "arbitrary"),
    )(xs, bm, cm, z, dtr, hsave, dtb, alog, dsk, nw, dyn)


def _s5_param_fn(lam_re, lam_im, log_dt, bt_re, bt_im):
    lr = jnp.minimum(lam_re, EIG_MAX)
    dt = jnp.exp(log_dt)
    mag = jnp.exp(lr * dt)
    lb_re = mag * jnp.cos(lam_im * dt)
    lb_im = mag * jnp.sin(lam_im * dt)
    n_re = lb_re - 1.0
    den = lr * lr + lam_im * lam_im
    k_re = (n_re * lr + lb_im * lam_im) / den
    k_im = (lb_im * lr - n_re * lam_im) / den
    return lb_re, lb_im, k_re * bt_re - k_im * bt_im, k_re * bt_im + k_im * bt_re


def _s5_params(lam_re, lam_im, log_dt, bt_re, bt_im, cts=None, *, name):
    args = (lam_re, lam_im, log_dt, bt_re, bt_im)
    n = len(args)

    def body(*refs):
        vals = [r[...] for r in refs[:n]]
        if cts is None:
            res = _s5_param_fn(*vals)
        else:
            _, vjp = jax.vjp(_s5_param_fn, *vals)
            res = vjp(tuple(r[...] for r in refs[n:n + 4]))
        for o_ref, v in zip(refs[-len(res):], res):
            o_ref[...] = v

    if cts is None:
        out = [lam_re, lam_im, bt_re, bt_im]
        ins = args
    else:
        out = list(args)
        ins = args + tuple(cts)
    return pl.pallas_call(
        body, name=name, out_shape=[jax.ShapeDtypeStruct(a.shape, F32) for a in out],
        compiler_params=pltpu.CompilerParams(vmem_limit_bytes=VMEM_LIMIT),
    )(*ins)


SCAN_COLS = 512


def _scan_specs(tb, time_of):
    row = pl.BlockSpec((tb, SCAN_COLS), lambda j, i: (time_of(i), j))
    par = pl.BlockSpec((1, SCAN_COLS), lambda j, i: (0, j))
    return row, par


def _s5_scan_fwd(bu_re, bu_im, lb_re, lb_im, *, name, tb=256):
    t, c = bu_re.shape
    nj = c // SCAN_COLS
    row, par = _scan_specs(tb, lambda i: i)

    def body(bre_ref, bim_ref, lre_ref, lim_ref, sre_ref, sim_ref, cre_ref, cim_ref):
        @pl.when(pl.program_id(1) == 0)
        def _():
            cre_ref[...] = jnp.zeros_like(cre_ref)
            cim_ref[...] = jnp.zeros_like(cim_ref)
        ar, ai = lre_ref[...], lim_ref[...]

        def step(k, s):
            sr, si = s
            nr = ar * sr - ai * si + bre_ref[pl.ds(k, 1), :]
            ni = ar * si + ai * sr + bim_ref[pl.ds(k, 1), :]
            sre_ref[pl.ds(k, 1), :] = nr
            sim_ref[pl.ds(k, 1), :] = ni
            return nr, ni

        sr, si = lax.fori_loop(0, tb, step, (cre_ref[...], cim_ref[...]), unroll=8)
        cre_ref[...] = sr
        cim_ref[...] = si

    return pl.pallas_call(
        body, name=name, grid=(nj, t // tb), in_specs=[row, row, par, par], out_specs=[row, row],
        out_shape=[jax.ShapeDtypeStruct((t, c), F32)] * 2,
        scratch_shapes=[pltpu.VMEM((1, SCAN_COLS), F32)] * 2,
        compiler_params=_cp("parallel", "arbitrary"),
    )(bu_re, bu_im, lb_re, lb_im)


def _s5_scan_bwd(s_re, s_im, ds_re, ds_im, lb_re, lb_im, *, name, tb=256):
    t, c = s_re.shape
    nj = c // SCAN_COLS
    nt = t // tb
    rev = lambda i: nt - 1 - i
    row, par = _scan_specs(tb, rev)
    prev = pl.BlockSpec((HALO, SCAN_COLS), lambda j, i: (jnp.maximum(rev(i) * (tb // HALO) - 1, 0), j))

    def body(sre_ref, sim_ref, pre_ref, pim_ref, dre_ref, dim_ref, lre_ref, lim_ref,
             gre_ref, gim_ref, dlre_ref, dlim_ref, cre_ref, cim_ref, ext_re, ext_im):
        step_id = pl.program_id(1)

        @pl.when(step_id == 0)
        def _():
            cre_ref[...] = jnp.zeros_like(cre_ref)
            cim_ref[...] = jnp.zeros_like(cim_ref)
            dlre_ref[...] = jnp.zeros_like(dlre_ref)
            dlim_ref[...] = jnp.zeros_like(dlim_ref)
        ar, ai = lre_ref[...], lim_ref[...]

        def step(k, g):
            gr, gi = g
            row = tb - 1 - k
            nr = dre_ref[pl.ds(row, 1), :] + ar * gr + ai * gi
            ni = dim_ref[pl.ds(row, 1), :] + ar * gi - ai * gr
            gre_ref[pl.ds(row, 1), :] = nr
            gim_ref[pl.ds(row, 1), :] = ni
            return nr, ni

        gr, gi = lax.fori_loop(0, tb, step, (cre_ref[...], cim_ref[...]), unroll=8)
        cre_ref[...] = gr
        cim_ref[...] = gi
        has_past = step_id != nt - 1
        ext_re[pl.ds(0, HALO), :] = jnp.where(has_past, pre_ref[...], 0.0)
        ext_im[pl.ds(0, HALO), :] = jnp.where(has_past, pim_ref[...], 0.0)
        ext_re[pl.ds(HALO, tb), :] = sre_ref[...]
        ext_im[pl.ds(HALO, tb), :] = sim_ref[...]
        pr, pi = ext_re[pl.ds(HALO - 1, tb), :], ext_im[pl.ds(HALO - 1, tb), :]
        g_re, g_im = gre_ref[...], gim_ref[...]
        dlre_ref[...] += jnp.sum(pr * g_re + pi * g_im, axis=0, keepdims=True)
        dlim_ref[...] += jnp.sum(pr * g_im - pi * g_re, axis=0, keepdims=True)

    return pl.pallas_call(
        body, name=name, grid=(nj, nt),
        in_specs=[row, row, prev, prev, row, row, par, par], out_specs=[row, row, par, par],
        out_shape=[jax.ShapeDtypeStruct((t, c), F32)] * 2 + [jax.ShapeDtypeStruct((1, c), F32)] * 2,
        scratch_shapes=[pltpu.VMEM((1, SCAN_COLS), F32)] * 2 + [pltpu.VMEM((HALO + tb, SCAN_COLS), F32)] * 2,
        compiler_params=_cp("parallel", "arbitrary"),
    )(s_re, s_im, s_re, s_im, ds_re, ds_im, lb_re, lb_im)


def _loss_fn(h, w, tgt):
    err = _rms_fn(h, w)[0] - tgt
    return 0.5 * jnp.sum(jnp.mean(err * err, axis=-1, keepdims=True), axis=0, keepdims=True)


def _loss_head(h, w, tgt, *, name, tb=256):
    t, d = h.shape

    def body(h_ref, w_ref, t_ref, loss_ref, dh_ref, dhb_ref, dw_ref):
        @pl.when(pl.program_id(0) == 0)
        def _():
            loss_ref[...] = jnp.zeros_like(loss_ref)
            dw_ref[...] = jnp.zeros_like(dw_ref)
        part, vjp = jax.vjp(_loss_fn, h_ref[...], w_ref[...], t_ref[...])
        dh, dw, _ = vjp(jnp.ones((1, 1), F32))
        loss_ref[...] += jnp.broadcast_to(part, loss_ref.shape)
        dh_ref[...] = dh
        dhb_ref[...] = dh.astype(BF16)
        dw_ref[...] += dw

    row = pl.BlockSpec((tb, d), lambda i: (i, 0))
    par = pl.BlockSpec((1, d), lambda i: (0, 0))
    return pl.pallas_call(
        body, name=name, grid=(t // tb,), in_specs=[row, par, row],
        out_specs=[pl.BlockSpec((SUBLANES, LANES), lambda i: (0, 0)), row, row, par],
        out_shape=[jax.ShapeDtypeStruct((SUBLANES, LANES), F32), jax.ShapeDtypeStruct((t, d), F32),
                   jax.ShapeDtypeStruct((t, d), BF16), jax.ShapeDtypeStruct((1, d), F32)],
        compiler_params=_cp("arbitrary"),
    )(h, w, tgt)


def _adamw(w, g, m, v, *, name):
    r, c = w.shape
    tr = _tile(r, 256, SUBLANES)

    def body(w_ref, g_ref, m_ref, v_ref, d_ref, nm_ref, nv_ref):
        g = g_ref[...]
        nm = ADAM_B1 * m_ref[...] + (1.0 - ADAM_B1) * g
        nv = ADAM_B2 * v_ref[...] + (1.0 - ADAM_B2) * (g * g)
        m_hat = nm / (1.0 - ADAM_B1 ** ADAM_STEP)
        v_hat = nv / (1.0 - ADAM_B2 ** ADAM_STEP)
        d_ref[...] = -ADAM_LR * (m_hat / (jnp.sqrt(v_hat) + ADAM_EPS) + ADAM_WD * w_ref[...])
        nm_ref[...] = nm
        nv_ref[...] = nv

    spec = pl.BlockSpec((tr, c), lambda i: (i, 0))
    return pl.pallas_call(
        body, name=name, grid=(r // tr,), in_specs=[spec] * 4, out_specs=[spec] * 3,
        out_shape=[jax.ShapeDtypeStruct((r, c), F32)] * 3, compiler_params=_cp("parallel"),
    )(w, g, m, v)


def _sum_parts(parts, *, name):
    _, r, c = parts.shape
    tr = _tile(r, 128, SUBLANES)

    def body(p_ref, o_ref):
        acc = p_ref[0].astype(F32)
        for k in range(1, N_DEV):
            acc = acc + p_ref[k].astype(F32)
        o_ref[...] = acc

    return pl.pallas_call(
        body, name=name, grid=(r // tr,), in_specs=[pl.BlockSpec((N_DEV, tr, c), lambda i: (0, i, 0))],
        out_specs=pl.BlockSpec((tr, c), lambda i: (i, 0)), out_shape=jax.ShapeDtypeStruct((r, c), F32),
        compiler_params=_cp("parallel"),
    )(parts)


def _position():
    return lax.axis_index("x"), lax.axis_index("y"), lax.axis_index("c")


def _flat(px, py, pc):
    return 4 * px + 2 * py + pc


def _all_gather(shard, *, name):
    def body(x_ref, out_ref, send_sems, recv_sems, local_sem):
        x, y, c = _position()
        me, sibling = (x, y, c), (x, y, 1 - c)
        chips = [(1 - x, y), (x, 1 - y), (1 - x, 1 - y)]

        def copy(k, block, to, src=None):
            slot = out_ref.at[_flat(*block)]
            return pltpu.make_async_remote_copy(
                src_ref=slot if src is None else src, dst_ref=slot, send_sem=send_sems.at[k],
                recv_sem=recv_sems.at[k], device_id=to, device_id_type=MESH)

        mine = pltpu.make_async_copy(x_ref, out_ref.at[_flat(*me)], local_sem)
        mine.start()
        first = [copy(0, me, sibling, src=x_ref)]
        first += [copy(1 + j, me, (*chip, c), src=x_ref) for j, chip in enumerate(chips)]
        for cp in first:
            cp.start()
        passed = [copy(4 + j, (*chip, c), sibling) for j, chip in enumerate(chips)]
        for j, chip in enumerate(chips):
            copy(1 + j, (*chip, c), me).wait_recv()
            passed[j].start()
        copy(0, sibling, me).wait_recv()
        for j, chip in enumerate(chips):
            copy(4 + j, (*chip, 1 - c), me).wait_recv()
        for cp in first + passed:
            cp.wait_send()
        mine.wait()

    return pl.pallas_call(
        body, name=name, out_shape=jax.ShapeDtypeStruct((N_DEV,) + shard.shape, shard.dtype),
        in_specs=[pl.BlockSpec(memory_space=pl.ANY)], out_specs=pl.BlockSpec(memory_space=pl.ANY),
        scratch_shapes=[pltpu.SemaphoreType.DMA((7,)), pltpu.SemaphoreType.DMA((7,)), pltpu.SemaphoreType.DMA(())],
    )(shard)


_HBM = pl.BlockSpec(memory_space=pltpu.HBM)
_SEM = pl.BlockSpec(memory_space=pltpu.SEMAPHORE)
_EFFECT = pltpu.SideEffectType.DATAFLOW_SIDE_EFFECTING


def _copy_ends(src_ref, land_ref, mode, me, to):
    if mode == "gather_slot":
        return src_ref, land_ref.at[me]
    if mode == "gather_cols":
        w = src_ref.shape[1]
        return src_ref, land_ref.at[:, pl.ds(pl.multiple_of(me * w, LANES), w)]
    if mode == "scatter_slot":
        return src_ref.at[to], land_ref.at[me]
    w = land_ref.shape[2]
    return src_ref.at[:, pl.ds(pl.multiple_of(to * w, LANES), w)], land_ref.at[me]


def _land_shape(src, mode):
    if mode == "gather_slot":
        return (N_DEV,) + src.shape
    if mode == "gather_cols":
        return (src.shape[0], N_DEV * src.shape[1])
    if mode == "scatter_slot":
        return src.shape
    return (N_DEV, src.shape[0], src.shape[1] // N_DEV)


def _exchange_copies(src_ref, land_ref, send_sems, recv_sems, mode):
    x, y, c = _position()
    me = _flat(x, y, c)
    copies = []
    for k in range(1, N_DEV):
        peer = (x ^ ((k >> 2) & 1), y ^ ((k >> 1) & 1), c ^ (k & 1))
        src, dst = _copy_ends(src_ref, land_ref, mode, me, _flat(*peer))
        copies.append(pltpu.make_async_remote_copy(
            src_ref=src, dst_ref=dst, send_sem=send_sems.at[k - 1], recv_sem=recv_sems.at[k - 1],
            device_id=peer, device_id_type=MESH))
    return copies


def _place_own(src, mode, *, name):
    def body(src_ref, land_ref, sem):
        me = _flat(*_position())
        own_src, own_dst = _copy_ends(src_ref, land_ref, mode, me, me)
        cp = pltpu.make_async_copy(own_src, own_dst, sem)
        cp.start()
        cp.wait()

    return pl.pallas_call(
        body, name=name, out_shape=jax.ShapeDtypeStruct(_land_shape(src, mode), src.dtype),
        in_specs=[pl.BlockSpec(memory_space=pl.ANY)], out_specs=pl.BlockSpec(memory_space=pl.ANY),
        scratch_shapes=[pltpu.SemaphoreType.DMA(())],
    )(src)


def _exchange_start(src, mode, *, name):
    land = _place_own(src, mode, name=name + "_own")

    def body(src_ref, land_ref, send_sems, recv_sems, src_thru, land_thru, token):
        for cp in _exchange_copies(src_ref, land_ref, send_sems, recv_sems, mode):
            cp.start()
        token[...] = jnp.zeros_like(token)

    hbm = pltpu.with_memory_space_constraint
    *handle, token = pl.pallas_call(
        body, name=name,
        out_shape=(pltpu.SemaphoreType.DMA((N_DEV - 1,)), pltpu.SemaphoreType.DMA((N_DEV - 1,)),
                   pltpu.HBM(src.shape, src.dtype), pltpu.HBM(land.shape, land.dtype),
                   jax.ShapeDtypeStruct((SUBLANES, LANES), F32)),
        in_specs=(_HBM, _HBM), out_specs=(_SEM, _SEM, _HBM, _HBM, pl.BlockSpec(memory_space=pltpu.VMEM)),
        input_output_aliases={0: 2, 1: 3}, compiler_params=pltpu.CompilerParams(has_side_effects=_EFFECT),
    )(hbm(src, pltpu.HBM), hbm(land, pltpu.HBM))
    return (tuple(handle), mode), token


def _exchange_wait(pending, after, *, name):
    (send_sems, recv_sems, src_thru, land_thru), mode = pending

    def body(src_ref, land_ref, send_sems, recv_sems, after_ref, src_dead, got_ref):
        for cp in _exchange_copies(src_ref, land_ref, send_sems, recv_sems, mode):
            cp.wait_send()
            cp.wait_recv()

    return pl.pallas_call(
        body, name=name, out_shape=(pltpu.HBM(src_thru.shape, src_thru.dtype), pltpu.HBM(land_thru.shape, land_thru.dtype)),
        in_specs=(_HBM, _HBM, _SEM, _SEM, pl.BlockSpec(memory_space=pl.ANY)), out_specs=(_HBM, _HBM),
        input_output_aliases={0: 0, 1: 1}, compiler_params=pltpu.CompilerParams(has_side_effects=_EFFECT),
    )(src_thru, land_thru, send_sems, recv_sems, after)[1]


def _after(x, token):
    return x + token[0, 0].astype(x.dtype)


def _pad_cols(a, mult):
    pad = -a.shape[1] % mult
    return jnp.pad(a, ((0, 0), (0, pad))) if pad else a


def _pack(arrs, cols):
    flat = jnp.concatenate([a.reshape(-1).astype(F32) for a in arrs])
    sizes = [int(a.size) for a in arrs]
    flat = jnp.pad(flat, (0, -flat.shape[0] % (SUBLANES * cols)))
    return flat.reshape(-1, cols), sizes


def _unpack(flat2d, sizes, shapes):
    flat = flat2d.reshape(-1)
    out, o = [], 0
    for n, s in zip(sizes, shapes):
        out.append(flat[o:o + n].reshape(s))
        o += n
    return out


PACK_COLS = SUBLANES * LANES


def kernel(x, norm_mix_w, w_in, conv_a_w, conv_a_b, dt_bias, a_log, d_a, norm_a_w, w_proj_a, s5_lam_re, s5_lam_im, s5_log_dt, s5_b_re, s5_b_im, s5_c_re, s5_c_im, s5_d, w_s5_glu, w_out, norm_ffn_w, w_up, conv_ffn_w, conv_ffn_b, w_down, norm_final_w, loss_target, m_norm_mix_w, m_w_in, m_conv_a_w, m_conv_a_b, m_dt_bias, m_a_log, m_d_a, m_norm_a_w, m_w_proj_a, m_s5_lam_re, m_s5_lam_im, m_s5_log_dt, m_s5_b_re, m_s5_b_im, m_s5_c_re, m_s5_c_im, m_s5_d, m_w_s5_glu, m_w_out, m_norm_ffn_w, m_w_up, m_conv_ffn_w, m_conv_ffn_b, m_w_down, m_norm_final_w, v_norm_mix_w, v_w_in, v_conv_a_w, v_conv_a_b, v_dt_bias, v_a_log, v_d_a, v_norm_a_w, v_w_proj_a, v_s5_lam_re, v_s5_lam_im, v_s5_log_dt, v_s5_b_re, v_s5_b_im, v_s5_c_re, v_s5_c_im, v_s5_d, v_w_s5_glu, v_w_out, v_norm_ffn_w, v_w_up, v_conv_ffn_w, v_conv_ffn_b, v_w_down, v_norm_final_w):
    weights = dict(norm_mix_w=norm_mix_w, w_in=w_in, conv_a_w=conv_a_w, conv_a_b=conv_a_b, dt_bias=dt_bias, a_log=a_log, d_a=d_a, norm_a_w=norm_a_w, w_proj_a=w_proj_a, s5_lam_re=s5_lam_re, s5_lam_im=s5_lam_im, s5_log_dt=s5_log_dt, s5_b_re=s5_b_re, s5_b_im=s5_b_im, s5_c_re=s5_c_re, s5_c_im=s5_c_im, s5_d=s5_d, w_s5_glu=w_s5_glu, w_out=w_out, norm_ffn_w=norm_ffn_w, w_up=w_up, conv_ffn_w=conv_ffn_w, conv_ffn_b=conv_ffn_b, w_down=w_down, norm_final_w=norm_final_w)
    moms = dict(norm_mix_w=m_norm_mix_w, w_in=m_w_in, conv_a_w=m_conv_a_w, conv_a_b=m_conv_a_b, dt_bias=m_dt_bias, a_log=m_a_log, d_a=m_d_a, norm_a_w=m_norm_a_w, w_proj_a=m_w_proj_a, s5_lam_re=m_s5_lam_re, s5_lam_im=m_s5_lam_im, s5_log_dt=m_s5_log_dt, s5_b_re=m_s5_b_re, s5_b_im=m_s5_b_im, s5_c_re=m_s5_c_re, s5_c_im=m_s5_c_im, s5_d=m_s5_d, w_s5_glu=m_w_s5_glu, w_out=m_w_out, norm_ffn_w=m_norm_ffn_w, w_up=m_w_up, conv_ffn_w=m_conv_ffn_w, conv_ffn_b=m_conv_ffn_b, w_down=m_w_down, norm_final_w=m_norm_final_w)
    vars_ = dict(norm_mix_w=v_norm_mix_w, w_in=v_w_in, conv_a_w=v_conv_a_w, conv_a_b=v_conv_a_b, dt_bias=v_dt_bias, a_log=v_a_log, d_a=v_d_a, norm_a_w=v_norm_a_w, w_proj_a=v_w_proj_a, s5_lam_re=v_s5_lam_re, s5_lam_im=v_s5_lam_im, s5_log_dt=v_s5_log_dt, s5_b_re=v_s5_b_re, s5_b_im=v_s5_b_im, s5_c_re=v_s5_c_re, s5_c_im=v_s5_c_im, s5_d=v_s5_d, w_s5_glu=v_w_s5_glu, w_out=v_w_out, norm_ffn_w=v_norm_ffn_w, w_up=v_w_up, conv_ffn_w=v_conv_ffn_w, conv_ffn_b=v_conv_ffn_b, w_down=v_w_down, norm_final_w=v_norm_final_w)
    names = list(weights)
    col_sharded = ("w_in", "w_s5_glu", "w_up")
    row_sharded = ("w_proj_a", "w_out", "w_down")
    conv_sharded = ("conv_a_w", "conv_ffn_w")
    replicated = [n for n in names if n not in col_sharded + row_sharded + conv_sharded]

    x2, tgt = x[0], loss_target[0]
    t, d = x2.shape
    nh = dt_bias.shape[-1]
    d_inner = norm_a_w.shape[-1]
    conv_dim = conv_a_b.shape[-1]
    gn = (conv_dim - d_inner) // 2
    ng = gn // D_STATE
    nr = nh // ng
    rp = d_inner // ng
    d_s5 = s5_d.shape[-1]
    gs, ps = s5_lam_re.shape[1:]
    cs = d_s5 // gs
    n_oct = gs // 8
    assert 8 * ps == SCAN_COLS and 8 * cs == LANES and gs % 8 == 0
    d_ff = w_down.shape[1] * N_DEV
    dev = _flat(*_position())

    ka, kf = conv_a_w.shape[1], conv_ffn_w.shape[1]
    taps = jnp.concatenate([conv_a_w[0].reshape(1, -1), conv_ffn_w[0].reshape(1, -1)], axis=1)
    taps = _all_gather(taps, name="ag_conv_taps")[:, 0]
    def by_cols(n):
        return n in ("w_s5_glu", "w_up") and weights[n].shape[2] % LANES == 0

    pending, started = {}, taps[:1, :1] * 0.0
    for n in ("w_in", "w_proj_a", "w_s5_glu", "w_out", "w_up", "w_down"):
        pending[n], token = _exchange_start(weights[n][0].astype(BF16), "gather_cols" if by_cols(n) else "gather_slot",
                                            name="ag_" + n)
        started = started + token[:1, :1]

    def gathered(n, after):
        g = _exchange_wait(pending[n], after, name="agw_" + n)
        if by_cols(n) or n == "w_in":
            return g
        if n in row_sharded:
            return g.reshape(-1, g.shape[2])
        return jnp.transpose(g, (1, 0, 2)).reshape(g.shape[1], -1)

    seg_sizes = dict(z=d_inner, xs=d_inner, bm=gn, cm=gn, dt=nh, u=d_s5, ga=d, gb=d)
    seg_names = tuple(seg_sizes)
    pieces, seg_at = _w_in_pieces(seg_sizes, ("z", "xs", "ga", "gb", "bm", "cm", "u", "dt"), w_in.shape[2])
    na = ka * conv_a_w.shape[2]
    cw_a = jnp.transpose(taps[:, :na].reshape(N_DEV, ka, -1), (1, 0, 2)).reshape(ka, conv_dim)
    cw_f = jnp.transpose(taps[:, na:].reshape(N_DEV, kf, -1), (1, 0, 2)).reshape(kf, 2 * d_ff)
    cb_a, cb_f = conv_a_b, conv_ffn_b
    a_cols = {"xs": slice(0, d_inner), "bm": slice(d_inner, d_inner + gn), "cm": slice(d_inner + gn, conv_dim)}

    w1 = norm_mix_w.reshape(1, 1, d) + started[0, 0]
    hn1, = _blocked_fwd(_rms_fn, [x2], [w1], [(d, BF16)], tb=256, name="rms1")
    w_in_p = _w_in_pack(gathered("w_in", hn1), pieces, seg_at, seg_sizes, name="w_in_pack")
    pre = {sn: _mm(hn1, w_in_p, b_win=seg_at[sn], name="in_" + sn) for sn in seg_names}
    act_a = {sn: _conv_fwd(_comb_silu, [pre[sn]], [cw_a[:, a_cols[sn]]], [cb_a[:, a_cols[sn]]], out_dtype=F32,
                           name="conv_a_" + sn) for sn in a_cols}
    dtr3 = jnp.transpose(pre["dt"][:, :nh].reshape(t, ng, nr), (1, 0, 2))
    dtb3, alog3, dsk3 = (p.reshape(ng, 1, nr) for p in (dt_bias, a_log, d_a))
    nw3 = norm_a_w.reshape(ng, 1, rp)
    yn, hsave = _ssd_fwd(act_a["xs"], act_a["bm"], act_a["cm"], pre["z"], dtr3, dtb3, alog3, dsk3, nw3, name="ssd")
    w_proj = gathered("w_proj_a", yn)
    y_a = _mm(yn, w_proj, name="proj_a")

    lam_re3, lam_im3 = s5_lam_re[0][:, None, :], s5_lam_im[0][:, None, :]
    logdt3 = s5_log_dt[0][:, None, None]
    bt_re, bt_im = jnp.transpose(s5_b_re[0], (0, 2, 1)), jnp.transpose(s5_b_im[0], (0, 2, 1))
    lb_re3, lb_im3, bb_re, bb_im = _s5_params(lam_re3, lam_im3, logdt3, bt_re, bt_im, name="s5_params")
    eye = jnp.eye(8, dtype=F32)

    def diag_b(bt):
        return (bt.reshape(n_oct, 8, cs, 1, ps) * eye[None, :, None, :, None]).reshape(n_oct, 8 * cs, 8 * ps)

    def undiag_b(blk):
        return (blk.reshape(n_oct, 8, cs, 8, ps) * eye[None, :, None, :, None]).sum(axis=3).reshape(gs, cs, ps)

    def diag_c(cm):
        ct = jnp.transpose(cm.reshape(n_oct, 8, cs, ps), (0, 1, 3, 2))
        return (ct[:, :, :, None, :] * eye[None, :, None, :, None]).reshape(n_oct, 8 * ps, 8 * cs)

    def undiag_c(blk):
        ct = (blk.reshape(n_oct, 8, ps, 8, cs) * eye[None, :, None, :, None]).sum(axis=3)
        return jnp.transpose(ct, (0, 1, 3, 2)).reshape(gs, cs, ps)

    b_blk_re, b_blk_im = diag_b(bb_re), diag_b(bb_im)
    c_blk_re, c_blk_imn = diag_c(s5_c_re[0]), diag_c(-s5_c_im[0])
    d3 = s5_d.reshape(n_oct, 1, LANES)
    lb_re, lb_im = lb_re3.reshape(1, gs * ps), lb_im3.reshape(1, gs * ps)
    u = pre["u"]
    bu_re, bu_im = _blocked_fwd(_s5_bu_fn, [u], [b_blk_re, b_blk_im], [(gs * ps, F32)] * 2, nj=n_oct, tb=512,
                                name="s5_bu")
    s_re, s_im = _s5_scan_fwd(bu_re, bu_im, lb_re, lb_im, name="s5_scan")
    yb, = _blocked_fwd(_s5_out_fn, [s_re, s_im, u], [c_blk_re, c_blk_imn, d3], [(d_s5, BF16)], nj=n_oct, tb=512,
                       name="s5_out")
    w_glu = gathered("w_s5_glu", yb)
    glu_v = _mm(yb, w_glu, b_win=(0, d), name="glu_v")
    glu_g = _mm(yb, w_glu, b_win=(d, d), name="glu_g")
    merged, = _blocked_fwd(_merge_fn, [glu_v, glu_g, pre["ga"], pre["gb"], y_a], [], [(d, BF16)], tb=256,
                           name="merge")
    w_o = gathered("w_out", merged)
    h1 = _mm(merged, w_o, acc=x2, name="out_proj")
    w2 = norm_ffn_w.reshape(1, 1, d)
    hn2, = _blocked_fwd(_rms_fn, [h1], [w2], [(d, BF16)], tb=256, name="rms2")
    w_u = gathered("w_up", hn2)
    up_g = _mm(hn2, w_u, b_win=(0, d_ff), name="up_g")
    up_v = _mm(hn2, w_u, b_win=(d_ff, d_ff), name="up_v")
    f_w = [cw_f[:, :d_ff], cw_f[:, d_ff:]]
    f_b = [cb_f[:, :d_ff], cb_f[:, d_ff:]]
    act = _conv_fwd(_comb_glu, [up_g, up_v], f_w, f_b, out_dtype=BF16, name="conv_ffn")
    w_dn = gathered("w_down", act)
    h2 = _mm(act, w_dn, acc=h1, name="down")
    loss_tile, dh2, dh2_b, g_final = _loss_head(h2, norm_final_w.reshape(1, d), tgt, name="loss_head")

    grads, scattering = {}, {}

    def scatter_start(n, g):
        if by_cols(n):
            src, mode = g, "scatter_cols"
        elif n in row_sharded:
            src, mode = g.reshape(N_DEV, -1, g.shape[1]), "scatter_slot"
        elif n == "w_in":
            src, mode = g, "scatter_slot"
        else:
            src, mode = jnp.transpose(g.reshape(g.shape[0], N_DEV, -1), (1, 0, 2)), "scatter_slot"
        scattering[n], token = _exchange_start(src, mode, name="rs_" + n)
        return token

    d_act = _mm(dh2_b, w_dn, tb=True, name="d_act")
    g_down = _mm(act, dh2_b, ta=True, out_dtype=BF16, name="g_w_down")
    tok = scatter_start("w_down", g_down)
    (dup_g, dwf_g, dbf_g), (dup_v, dwf_v, dbf_v) = _conv_bwd(
        _comb_glu, [up_g, up_v], f_w, [_after(f_b[0], tok), f_b[1]], d_act, dx_dtype=BF16, name="conv_ffn_bwd")
    dhn2 = _mm(dup_g, w_u, tb=True, b_win=(0, d_ff), name="d_hn2_g")
    dhn2 = _mm(dup_v, w_u, tb=True, b_win=(d_ff, d_ff), acc=dhn2, name="d_hn2_v")
    g_up = _mm(hn2, dup_g, ta=True, into=(lax.empty((d, 2 * d_ff), BF16), 0), name="g_w_up_g")
    g_up = _mm(hn2, dup_v, ta=True, into=(g_up, d_ff), name="g_w_up_v")
    tok = scatter_start("w_up", g_up)
    dh1, dh1_b, g_w2 = _blocked_bwd(_rms_fn, [h1], [_after(w2, tok)], [dhn2], [(F32, BF16)], adds={0: dh2}, tb=256,
                                    name="rms2_bwd")
    d_merged = _mm(dh1_b, w_o, tb=True, name="d_merged")
    g_out = _mm(merged, dh1_b, ta=True, out_dtype=BF16, name="g_w_out")
    tok = scatter_start("w_out", g_out)
    dglu_v, dglu_g, dga, dgb, dy_a = _blocked_bwd(
        _merge_fn, [glu_v, glu_g, pre["ga"], pre["gb"], y_a], [], [d_merged], [BF16] * 5, tb=128, name="merge_bwd")
    dyb = _mm(dglu_v, w_glu, tb=True, b_win=(0, d), name="d_yb_v")
    dyb = _mm(dglu_g, w_glu, tb=True, b_win=(d, d), acc=dyb, name="d_yb_g")
    g_glu = _mm(yb, dglu_v, ta=True, into=(lax.empty((d_s5, 2 * d), BF16), 0), name="g_w_glu_v")
    g_glu = _mm(yb, dglu_g, ta=True, into=(g_glu, d), name="g_w_glu_g")
    tok = tok + scatter_start("w_s5_glu", g_glu)
    ds_re, ds_im, du_skip, dc_blk_re, dc_blk_imn, dd3 = _blocked_bwd(
        _s5_out_fn, [s_re, s_im, u], [c_blk_re, c_blk_imn, _after(d3, tok)], [dyb], [F32, F32, F32], nj=n_oct, tb=512,
        name="s5_out_bwd")
    dbu_re, dbu_im, dlb_re, dlb_im = _s5_scan_bwd(s_re, s_im, ds_re, ds_im, lb_re, lb_im, name="s5_scan_bwd")
    du, db_blk_re, db_blk_im = _blocked_bwd(
        _s5_bu_fn, [u], [b_blk_re, b_blk_im], [dbu_re, dbu_im], [BF16], adds={0: du_skip}, nj=n_oct, tb=512,
        name="s5_bu_bwd")
    g_lre, g_lim, g_ldt, g_bt_re, g_bt_im = _s5_params(
        lam_re3, lam_im3, logdt3, bt_re, bt_im,
        cts=(dlb_re.reshape(gs, 1, ps), dlb_im.reshape(gs, 1, ps), undiag_b(db_blk_re), undiag_b(db_blk_im)),
        name="s5_params_bwd")
    grads["s5_lam_re"], grads["s5_lam_im"] = g_lre.reshape(s5_lam_re.shape), g_lim.reshape(s5_lam_im.shape)
    grads["s5_log_dt"] = g_ldt.reshape(s5_log_dt.shape)
    grads["s5_b_re"] = jnp.transpose(g_bt_re, (0, 2, 1)).reshape(s5_b_re.shape)
    grads["s5_b_im"] = jnp.transpose(g_bt_im, (0, 2, 1)).reshape(s5_b_im.shape)
    grads["s5_c_re"] = undiag_c(dc_blk_re).reshape(s5_c_re.shape)
    grads["s5_c_im"] = -undiag_c(dc_blk_imn).reshape(s5_c_im.shape)
    grads["s5_d"] = dd3.reshape(s5_d.shape)

    dyn = _mm(dy_a, w_proj, tb=True, name="d_yn")
    g_proj = _mm(yn, dy_a, ta=True, out_dtype=BF16, name="g_w_proj_a")
    tok = scatter_start("w_proj_a", g_proj)
    dxs, dbm, dcm, dz, ddtr3, g_dtb, g_alog, g_dsk, g_nw = _ssd_bwd(
        act_a["xs"], act_a["bm"], act_a["cm"], pre["z"], dtr3, hsave, dtb3, alog3, dsk3, _after(nw3, tok), dyn,
        name="ssd_bwd")
    grads["dt_bias"], grads["a_log"], grads["d_a"] = (g.reshape(1, nh) for g in (g_dtb, g_alog, g_dsk))
    grads["norm_a_w"] = g_nw.reshape(1, d_inner)
    dpre = {"z": dz, "u": du, "ga": dga, "gb": dgb}
    dcw, dcb = {}, {}
    for sn, dact in (("xs", dxs), ("bm", dbm), ("cm", dcm)):
        (dpre[sn], dcw[sn], dcb[sn]), = _conv_bwd(
            _comb_silu, [pre[sn]], [cw_a[:, a_cols[sn]]], [cb_a[:, a_cols[sn]]], dact, dx_dtype=BF16,
            name="conv_a_bwd_" + sn)
    dpre["dt"] = _pad_cols(jnp.transpose(ddtr3, (1, 0, 2)).reshape(t, nh), LANES).astype(BF16)
    g_in = _w_in_unpack({sn: _mm(hn1, dpre[sn], ta=True, name="g_w_in_" + sn) for sn in seg_names}, pieces,
                        w_in.shape[2], name="w_in_unpack")
    tok = scatter_start("w_in", g_in)
    dhn1 = _mm(_after(dpre["dt"], tok), w_in_p, tb=True, b_win=seg_at["dt"], name="d_hn1_dt")
    for sn in seg_names:
        if sn != "dt":
            dhn1 = _mm(dpre[sn], w_in_p, tb=True, b_win=seg_at[sn], acc=dhn1, name="d_hn1_" + sn)
    dx, g_w1 = _blocked_bwd(_rms_fn, [x2], [w1], [dhn1], [F32], adds={0: dh1}, tb=256, name="rms1_bwd")

    grads["norm_mix_w"], grads["norm_ffn_w"] = g_w1.reshape(1, d), g_w2.reshape(1, d)
    grads["norm_final_w"] = g_final.reshape(d)
    grads["conv_a_b"] = jnp.concatenate([dcb["xs"], dcb["bm"], dcb["cm"]], axis=1)
    grads["conv_ffn_b"] = jnp.concatenate([dbf_g, dbf_v], axis=1)
    g_cw_a = jnp.concatenate([dcw["xs"], dcw["bm"], dcw["cm"]], axis=1)
    g_cw_f = jnp.concatenate([dwf_g, dwf_v], axis=1)

    small = [grads[n] for n in replicated] + [g_cw_a, g_cw_f, loss_tile[:1, :1]]
    packed, sizes = _pack(small, PACK_COLS)
    summed = _sum_parts(_all_gather(packed, name="ag_small_grads"), name="sum_small_grads")
    *rep_sums, s_cw_a, s_cw_f, loss = _unpack(summed, sizes, [a.shape for a in small])
    for n, g in zip(replicated, rep_sums):
        grads[n] = g
    wa, wf = conv_a_w.shape[2], conv_ffn_w.shape[2]
    grads["conv_a_w"] = lax.dynamic_slice_in_dim(s_cw_a, dev * wa, wa, axis=1)[None]
    grads["conv_ffn_w"] = lax.dynamic_slice_in_dim(s_cw_f, dev * wf, wf, axis=1)[None]

    delta, new_m, new_v = {}, {}, {}
    done = dx
    for n in ("w_down", "w_up", "w_out", "w_s5_glu", "w_proj_a", "w_in"):
        land = _exchange_wait(scattering[n], done, name="rsw_" + n)
        shape = weights[n].shape
        two_d = lambda a: a.reshape(shape[-2], shape[-1])
        g = _sum_parts(land, name="rs_sum_" + n)
        grads[n] = g.reshape(shape)
        dl, nm, nv = _adamw(two_d(weights[n]), g, two_d(moms[n]), two_d(vars_[n]), name="adamw_" + n)
        delta[n], new_m[n], new_v[n] = dl.reshape(shape), nm.reshape(shape), nv.reshape(shape)
        done = dl
    small_names = replicated + list(conv_sharded)
    shapes = [weights[n].shape for n in small_names]
    pw, sizes = _pack([weights[n] for n in small_names], PACK_COLS)
    pg, _ = _pack([grads[n] for n in small_names], PACK_COLS)
    pm, _ = _pack([moms[n] for n in small_names], PACK_COLS)
    pv, _ = _pack([vars_[n] for n in small_names], PACK_COLS)
    dl, nm, nv = _adamw(pw, pg, pm, pv, name="adamw_small")
    for n, a, b, c in zip(small_names, _unpack(dl, sizes, shapes), _unpack(nm, sizes, shapes),
                          _unpack(nv, sizes, shapes)):
        delta[n], new_m[n], new_v[n] = a, b, c

    return (loss.reshape(()), dx[None], *[grads[n] for n in names], *[delta[n] for n in names],
            *[new_m[n] for n in names], *[new_v[n] for n in names])
```

```python
import functools

import jax
import jax.numpy as jnp
from jax import lax
from jax.experimental import pallas as pl
from jax.experimental.pallas import tpu as pltpu

F32 = jnp.float32
BF16 = jnp.bfloat16
HIGHEST = lax.Precision.HIGHEST
MESH = pl.DeviceIdType.MESH

EPS = 1e-6
EIG_MAX = -1e-4
D_STATE = 128
CHUNK = 128
ADAM_LR = 0.001
ADAM_B1 = 0.9
ADAM_B2 = 0.999
ADAM_EPS = 1e-08
ADAM_WD = 0.01
ADAM_STEP = 10
N_DEV = 8
LANES = 128
SUBLANES = 8
VMEM_LIMIT = 56 * 1024 * 1024
MM_MAX_K = 4096


def _cp(*sem):
    return pltpu.CompilerParams(dimension_semantics=sem, vmem_limit_bytes=VMEM_LIMIT)


def _tile(dim, pref, unit=LANES):
    if dim <= unit:
        return dim
    t = (min(pref, dim) // unit) * unit
    while dim % t:
        t -= unit
    return t


_DIMS = {"nn": (((1,), (0,)), ((), ())), "nt": (((1,), (1,)), ((), ())), "tn": (((0,), (0,)), ((), ()))}


def _dot(a, b, kind):
    return lax.dot_general(a.astype(BF16), b.astype(BF16), _DIMS[kind], preferred_element_type=F32)


@functools.partial(jax.custom_vjp, nondiff_argnums=(2,))
def _bdot(a, b, kind):
    return _dot(a, b, kind)


def _bdot_fwd(a, b, kind):
    return _dot(a, b, kind), (a, b)


def _bdot_bwd(kind, res, g):
    a, b = res
    if kind == "nn":
        return _dot(g, b, "nt"), _dot(a, g, "tn")
    if kind == "nt":
        return _dot(g, b, "nn"), _dot(g, a, "tn")
    return _dot(b, g, "nt"), _dot(a, g, "nn")


_bdot.defvjp(_bdot_fwd, _bdot_bwd)


def _mm(a, b, *, ta=False, tb=False, acc=None, out_dtype=F32, name, b_win=None, into=None):
    assert not (ta and tb)
    m, k = (a.shape[1], a.shape[0]) if ta else a.shape
    b_off, b_size = b_win or (0, b.shape[1])
    n = b.shape[0] if tb else b_size
    assert (b_size if tb else b.shape[0]) == k, (a.shape, b.shape, ta, tb, b_win)
    o_off = into[1] if into else 0
    nk = -(-k // MM_MAX_K)
    while k % nk or (k // nk) % LANES or (tb and b_off % (k // nk)):
        nk += 1
    tk = k // nk
    tm, tn = _tile(m, 1024 if tk <= 2048 else 512), _tile(n, 1024)
    while o_off % tn or (not tb and b_off % tn):
        tn = _tile(n, tn - LANES)
    kind = "tn" if ta else ("nt" if tb else "nn")
    a_spec = pl.BlockSpec((tk, tm), lambda i, j, l: (l, i)) if ta else pl.BlockSpec((tm, tk), lambda i, j, l: (i, l))
    if tb:
        b_spec = pl.BlockSpec((tn, tk), lambda i, j, l: (j, l + b_off // tk))
    else:
        b_spec = pl.BlockSpec((tk, tn), lambda i, j, l: (l, j + b_off // tn))
    c_spec = pl.BlockSpec((tm, tn), lambda i, j, l: (i, j))
    o_spec = pl.BlockSpec((tm, tn), lambda i, j, l: (i, j + o_off // tn))
    has_acc = acc is not None

    def body(*refs):
        a_ref, b_ref = refs[:2]
        c_ref = refs[2] if has_acc else None
        o_ref = refs[2 + has_acc + (into is not None)]
        if nk == 1:
            res = _dot(a_ref[...], b_ref[...], kind)
            if has_acc:
                res = res + c_ref[...].astype(F32)
            o_ref[...] = res.astype(o_ref.dtype)
            return
        acc_ref = refs[-1]
        l = pl.program_id(2)

        @pl.when(l == 0)
        def _():
            if has_acc:
                acc_ref[...] = c_ref[...].astype(F32)
            else:
                acc_ref[...] = jnp.zeros_like(acc_ref)

        acc_ref[...] += _dot(a_ref[...], b_ref[...], kind)

        @pl.when(l == nk - 1)
        def _():
            o_ref[...] = acc_ref[...].astype(o_ref.dtype)

    ins = [a, b] + ([acc] if has_acc else []) + ([into[0]] if into else [])
    in_specs = [a_spec, b_spec] + ([c_spec] if has_acc else []) + ([pl.BlockSpec(memory_space=pl.ANY)] if into else [])
    out_shape = jax.ShapeDtypeStruct(into[0].shape, into[0].dtype) if into else jax.ShapeDtypeStruct((m, n), out_dtype)
    return pl.pallas_call(
        body, name=name, grid=(m // tm, n // tn, nk), in_specs=in_specs, out_specs=o_spec, out_shape=out_shape,
        input_output_aliases={len(ins) - 1: 0} if into else {},
        scratch_shapes=[pltpu.VMEM((tm, tn), F32)] if nk > 1 else [],
        compiler_params=_cp("parallel", "parallel", "arbitrary"),
    )(*ins)


def _w_in_pieces(seg_sizes, seg_order, n_blk):
    layout, o = {}, 0
    for sn in seg_order:
        width = -(-seg_sizes[sn] // LANES) * LANES
        layout[sn] = (o, width)
        o += width
    pieces, start = [], 0
    for sn, sz in seg_sizes.items():
        lo = start
        while lo < start + sz:
            blk = lo // n_blk
            hi = min(start + sz, (blk + 1) * n_blk)
            pieces.append((blk, lo - blk * n_blk, sn, lo - start, layout[sn][0] + lo - start, hi - lo))
            lo = hi
        start += sz
    return pieces, layout


def _w_in_pack(gathered, pieces, layout, seg_sizes, *, name, tr=256):
    _, k, n_blk = gathered.shape
    n_pad = sum(w for _, w in layout.values())

    def body(g_ref, o_ref):
        for sn, (off, width) in layout.items():
            if width != seg_sizes[sn]:
                o_ref[:, pl.ds(off + seg_sizes[sn], width - seg_sizes[sn])] = jnp.zeros(
                    (tr, width - seg_sizes[sn]), o_ref.dtype)
        for blk, src, _, _, dst, width in pieces:
            o_ref[:, pl.ds(dst, width)] = g_ref[blk, :, pl.ds(src, width)]

    return pl.pallas_call(
        body, name=name, grid=(k // tr,), in_specs=[pl.BlockSpec((N_DEV, tr, n_blk), lambda i: (0, i, 0))],
        out_specs=pl.BlockSpec((tr, n_pad), lambda i: (i, 0)), out_shape=jax.ShapeDtypeStruct((k, n_pad), gathered.dtype),
        compiler_params=_cp("parallel"),
    )(gathered)


def _w_in_unpack(seg_grads, pieces, n_blk, *, name, tr=128):
    names = list(seg_grads)
    k = seg_grads[names[0]].shape[0]

    def body(*refs):
        o_ref = refs[-1]
        seg_ref = dict(zip(names, refs))
        for blk, dst, sn, src, _, width in pieces:
            o_ref[blk, :, pl.ds(dst, width)] = seg_ref[sn][:, pl.ds(src, width)].astype(o_ref.dtype)

    return pl.pallas_call(
        body, name=name, grid=(k // tr,),
        in_specs=[pl.BlockSpec((tr, seg_grads[sn].shape[1]), lambda i: (i, 0)) for sn in names],
        out_specs=pl.BlockSpec((N_DEV, tr, n_blk), lambda i: (0, i, 0)),
        out_shape=jax.ShapeDtypeStruct((N_DEV, k, n_blk), BF16), compiler_params=_cp("parallel"),
    )(*[seg_grads[sn] for sn in names])


def _row_spec(arr, tb, nj):
    return pl.BlockSpec((tb, arr.shape[1] // nj), lambda j, i: (i, j))


def _par_spec(arr):
    return pl.BlockSpec((1,) + arr.shape[1:], lambda j, i: (j, 0, 0))


def _blocked_fwd(fn, rows, params, outs, *, nj=1, tb, name):
    t = rows[0].shape[0]
    nr, npar = len(rows), len(params)

    def body(*refs):
        res = fn(*[r[...] for r in refs[:nr]], *[p[0] for p in refs[nr:nr + npar]])
        for o_ref, val in zip(refs[nr + npar:], res):
            o_ref[...] = val.astype(o_ref.dtype)

    return pl.pallas_call(
        body, name=name, grid=(nj, t // tb),
        in_specs=[_row_spec(a, tb, nj) for a in rows] + [_par_spec(p) for p in params],
        out_specs=[pl.BlockSpec((tb, c // nj), lambda j, i: (i, j)) for c, _ in outs],
        out_shape=[jax.ShapeDtypeStruct((t, c), dt) for c, dt in outs],
        compiler_params=_cp("parallel", "arbitrary"),
    )(*rows, *params)


def _blocked_bwd(fn, rows, params, cts, row_grad_dtypes, *, adds=None, nj=1, tb, name):
    t = rows[0].shape[0]
    nr, npar, nct = len(rows), len(params), len(cts)
    adds = adds or {}
    add_keys = sorted(adds)
    want, want_dtypes = [], []
    for k, dts in enumerate(row_grad_dtypes):
        for dt in (dts if isinstance(dts, tuple) else (dts,)):
            if dt is not None:
                want.append(k)
                want_dtypes.append(dt)

    def body(*refs):
        row_refs = refs[:nr]
        par_refs = refs[nr:nr + npar]
        ct_refs = refs[nr + npar:nr + npar + nct]
        add_refs = dict(zip(add_keys, refs[nr + npar + nct:nr + npar + nct + len(add_keys)]))
        out_refs = refs[nr + npar + nct + len(add_keys):]
        _, vjp = jax.vjp(fn, *[r[...] for r in row_refs], *[p[0] for p in par_refs])
        grads = vjp(tuple(c[...].astype(F32) for c in ct_refs))
        for o_ref, k in zip(out_refs, want):
            g = grads[k]
            if k in add_refs:
                g = g + add_refs[k][...].astype(F32)
            o_ref[...] = g.astype(o_ref.dtype)
        first = pl.program_id(1) == 0
        for o_ref, g in zip(out_refs[len(want):], grads[nr:]):
            @pl.when(first)
            def _(o_ref=o_ref):
                o_ref[...] = jnp.zeros_like(o_ref)
            o_ref[0] += g

    add_arrs = [adds[k] for k in add_keys]
    return pl.pallas_call(
        body, name=name, grid=(nj, t // tb),
        in_specs=[_row_spec(a, tb, nj) for a in rows] + [_par_spec(p) for p in params]
        + [_row_spec(c, tb, nj) for c in cts] + [_row_spec(a, tb, nj) for a in add_arrs],
        out_specs=[_row_spec(rows[k], tb, nj) for k in want] + [_par_spec(p) for p in params],
        out_shape=[jax.ShapeDtypeStruct(rows[k].shape, dt) for k, dt in zip(want, want_dtypes)]
        + [jax.ShapeDtypeStruct(p.shape, F32) for p in params],
        compiler_params=_cp("parallel", "arbitrary"),
    )(*rows, *params, *cts, *add_arrs)


def _rms_fn(x, w):
    return (x * lax.rsqrt(jnp.mean(x * x, axis=-1, keepdims=True) + EPS) * w,)


def _silu(x):
    return x * jax.nn.sigmoid(x)


def _merge_fn(glu_v, glu_g, g_a, g_b, y_a):
    y_b = glu_v * jax.nn.sigmoid(glu_g)
    return (jax.nn.sigmoid(g_a) * y_a + jax.nn.sigmoid(g_b) * y_b,)


def _s5_bu_fn(u, b_re, b_im):
    return _bdot(u, b_re, "nn"), _bdot(u, b_im, "nn")


def _s5_out_fn(s_re, s_im, u, c_re, c_im_neg, d):
    return (jax.nn.gelu(_bdot(s_re, c_re, "nn") + _bdot(s_im, c_im_neg, "nn") + d * u),)


HALO = SUBLANES


def _conv_fn(comb, kw, ns):
    def fn(*args):
        cs = []
        for s in range(ns):
            xp, xm, w, b = args[4 * s:4 * s + 4]
            xe = jnp.concatenate([xp, xm], axis=0)
            tb = xm.shape[0]
            y = b
            for k in range(kw):
                off = HALO - kw + 1 + k
                y = y + w[k:k + 1, :] * xe[off:off + tb, :]
            cs.append(y)
        return comb(*cs)
    return fn


def _conv_specs(xs, ws, bs, tb, cb, time_of):
    specs = []
    for x, w, b in zip(xs, ws, bs):
        specs += [
            pl.BlockSpec((HALO, cb), lambda j, i: (jnp.maximum(time_of(i) * (tb // HALO) - 1, 0), j)),
            pl.BlockSpec((tb, cb), lambda j, i: (time_of(i), j)),
            pl.BlockSpec((w.shape[0], cb), lambda j, i: (0, j)),
            pl.BlockSpec((1, cb), lambda j, i: (0, j)),
        ]
    return specs


def _conv_fwd(comb, xs, ws, bs, *, out_dtype, name, tb=512):
    t, c = xs[0].shape
    cb = _tile(c, 512)
    ns = len(xs)
    fn = _conv_fn(comb, ws[0].shape[0], ns)

    def body(*refs):
        i = pl.program_id(1)
        args = []
        for s in range(ns):
            xp_ref, xm_ref, w_ref, b_ref = refs[4 * s:4 * s + 4]
            xp = jnp.where(i == 0, 0.0, xp_ref[...])
            args += [xp, xm_ref[...], w_ref[...], b_ref[...]]
        refs[4 * ns][...] = fn(*args).astype(out_dtype)

    flat = [a for x, w, b in zip(xs, ws, bs) for a in (x, x, w, b)]
    return pl.pallas_call(
        body, name=name, grid=(c // cb, t // tb),
        in_specs=_conv_specs(xs, ws, bs, tb, cb, lambda i: i),
        out_specs=pl.BlockSpec((tb, cb), lambda j, i: (i, j)),
        out_shape=jax.ShapeDtypeStruct((t, c), out_dtype),
        compiler_params=_cp("parallel", "arbitrary"),
    )(*flat)


def _conv_bwd(comb, xs, ws, bs, dy, *, dx_dtype, name, tb=512):
    t, c = xs[0].shape
    cb = _tile(c, 512)
    ns = len(xs)
    nt = t // tb
    fn = _conv_fn(comb, ws[0].shape[0], ns)

    def body(*refs):
        step = pl.program_id(1)
        in_refs = refs[:4 * ns]
        dy_ref = refs[4 * ns]
        out_refs = refs[4 * ns + 1:4 * ns + 1 + 3 * ns]
        carry_refs = refs[4 * ns + 1 + 3 * ns:]
        args = []
        for s in range(ns):
            xp_ref, xm_ref, w_ref, b_ref = in_refs[4 * s:4 * s + 4]
            xp = jnp.where(step == nt - 1, 0.0, xp_ref[...])
            args += [xp, xm_ref[...], w_ref[...], b_ref[...]]
        _, vjp = jax.vjp(fn, *args)
        grads = vjp(dy_ref[...].astype(F32))
        for s in range(ns):
            dxp, dxm, dw, db = grads[4 * s:4 * s + 4]
            dx_ref, dw_ref, db_ref = out_refs[3 * s:3 * s + 3]
            carry = carry_refs[s]

            @pl.when(step == 0)
            def _(carry=carry, dw_ref=dw_ref, db_ref=db_ref):
                carry[...] = jnp.zeros_like(carry)
                dw_ref[...] = jnp.zeros_like(dw_ref)
                db_ref[...] = jnp.zeros_like(db_ref)

            tail = jnp.concatenate([jnp.zeros((tb - HALO, cb), F32), carry[...]], axis=0)
            dx_ref[...] = (dxm + tail).astype(dx_dtype)
            carry[...] = dxp
            dw_ref[...] += dw
            db_ref[...] += db

    flat = [a for x, w, b in zip(xs, ws, bs) for a in (x, x, w, b)]
    rev = lambda i: nt - 1 - i
    out_specs, out_shape = [], []
    for x, w, b in zip(xs, ws, bs):
        out_specs += [pl.BlockSpec((tb, cb), lambda j, i: (rev(i), j)),
                      pl.BlockSpec((w.shape[0], cb), lambda j, i: (0, j)),
                      pl.BlockSpec((1, cb), lambda j, i: (0, j))]
        out_shape += [jax.ShapeDtypeStruct((t, c), dx_dtype), jax.ShapeDtypeStruct(w.shape, F32),
                      jax.ShapeDtypeStruct(b.shape, F32)]
    res = pl.pallas_call(
        body, name=name, grid=(c // cb, nt),
        in_specs=_conv_specs(xs, ws, bs, tb, cb, rev) + [pl.BlockSpec((tb, cb), lambda j, i: (rev(i), j))],
        out_specs=out_specs, out_shape=out_shape,
        scratch_shapes=[pltpu.VMEM((HALO, cb), F32) for _ in range(ns)],
        compiler_params=_cp("parallel", "arbitrary"),
    )(*flat, dy)
    return [tuple(res[3 * s:3 * s + 3]) for s in range(ns)]


def _comb_silu(c):
    return _silu(c)


def _comb_glu(cg, cv):
    return _silu(cg) * cv


def _ssd_fn(nheads, hdim):
    def fn(x, bm, cm, z, dtr, hin, dtb, alog, dsk, nw):
        q = x.shape[0]
        dt = jax.nn.softplus(dtr + dtb)
        da = dt * (-jnp.exp(alog))
        li = lax.broadcasted_iota(jnp.int32, (q, q), 0)
        si = lax.broadcasted_iota(jnp.int32, (q, q), 1)
        causal = li >= si
        tri = causal.astype(F32)
        acs = jnp.dot(tri, da, precision=HIGHEST, preferred_element_type=F32)
        acs_row = lax.dot_general(da, tri, (((0,), (1,)), ((), ())), precision=HIGHEST,
                                  preferred_element_type=F32)
        cb = _bdot(cm, bm, "nt")
        ch = _bdot(cm, hin, "nn")
        ys, hs = [], []
        for r in range(nheads):
            cols = slice(r * hdim, (r + 1) * hdim)
            xr = x[:, cols]
            a_col = acs[:, r:r + 1]
            decay = jnp.exp(jnp.where(causal, a_col - acs_row[r:r + 1, :], -1e30))
            xd = xr * dt[:, r:r + 1]
            y_diag = _bdot(cb * decay, xd, "nn")
            y_off = ch[:, cols] * jnp.exp(a_col)
            last = acs[q - 1:q, r:r + 1]
            st = _bdot(bm * jnp.exp(last - a_col), xd, "tn")
            hs.append(jnp.exp(last) * hin[:, cols] + st)
            ys.append(y_diag + y_off + dsk[:, r:r + 1] * xr)
        y = jnp.concatenate(ys, axis=1) * _silu(z)
        yn = y * lax.rsqrt(jnp.mean(y * y, axis=-1, keepdims=True) + EPS) * nw
        return yn, jnp.concatenate(hs, axis=1)
    return fn


def _ssd_specs(rp, nr, time_of):
    row = lambda w: pl.BlockSpec((CHUNK, w), lambda g, c: (time_of(c), g))
    par = lambda w: pl.BlockSpec((1, 1, w), lambda g, c: (g, 0, 0))
    return dict(
        x=row(rp), bc=row(D_STATE), dtr=pl.BlockSpec((1, CHUNK, nr), lambda g, c: (g, time_of(c), 0)),
        h=pl.BlockSpec((1, 1, D_STATE, rp), lambda g, c: (g, time_of(c), 0, 0)), pr=par(nr), pw=par(rp))


def _ssd_fwd(xs, bm, cm, z, dtr, dtb, alog, dsk, nw, *, name):
    t = xs.shape[0]
    g, _, nr = dtr.shape
    rp = xs.shape[1] // g
    nc = t // CHUNK
    fn = _ssd_fn(nr, rp // nr)
    sp = _ssd_specs(rp, nr, lambda c: c)

    def body(x_ref, b_ref, c_ref, z_ref, dtr_ref, dtb_ref, al_ref, dsk_ref, nw_ref, yn_ref, hs_ref, h_ref):
        @pl.when(pl.program_id(1) == 0)
        def _():
            h_ref[...] = jnp.zeros_like(h_ref)
        hin = h_ref[...]
        hs_ref[0, 0] = hin
        yn, hout = fn(x_ref[...], b_ref[...], c_ref[...], z_ref[...], dtr_ref[0], hin,
                      dtb_ref[0], al_ref[0], dsk_ref[0], nw_ref[0])
        yn_ref[...] = yn.astype(yn_ref.dtype)
        h_ref[...] = hout

    return pl.pallas_call(
        body, name=name, grid=(g, nc),
        in_specs=[sp["x"], sp["bc"], sp["bc"], sp["x"], sp["dtr"], sp["pr"], sp["pr"], sp["pr"], sp["pw"]],
        out_specs=[sp["x"], sp["h"]],
        out_shape=[jax.ShapeDtypeStruct(xs.shape, BF16), jax.ShapeDtypeStruct((g, nc, D_STATE, rp), F32)],
        scratch_shapes=[pltpu.VMEM((D_STATE, rp), F32)],
        compiler_params=_cp("parallel", "arbitrary"),
    )(xs, bm, cm, z, dtr, dtb, alog, dsk, nw)


def _ssd_bwd(xs, bm, cm, z, dtr, hsave, dtb, alog, dsk, nw, dyn, *, name):
    t = xs.shape[0]
    g, _, nr = dtr.shape
    rp = xs.shape[1] // g
    nc = t // CHUNK
    fn = _ssd_fn(nr, rp // nr)
    sp = _ssd_specs(rp, nr, lambda c: nc - 1 - c)

    def body(x_ref, b_ref, c_ref, z_ref, dtr_ref, hs_ref, dtb_ref, al_ref, dsk_ref, nw_ref, dyn_ref,
             dx_ref, db_ref, dc_ref, dz_ref, ddtr_ref, ddtb_ref, dal_ref, ddsk_ref, dnw_ref, dh_ref):
        first = pl.program_id(1) == 0

        @pl.when(first)
        def _():
            dh_ref[...] = jnp.zeros_like(dh_ref)
            for r in (ddtb_ref, dal_ref, ddsk_ref, dnw_ref):
                r[...] = jnp.zeros_like(r)

        _, vjp = jax.vjp(fn, x_ref[...], b_ref[...], c_ref[...], z_ref[...], dtr_ref[0], hs_ref[0, 0],
                         dtb_ref[0], al_ref[0], dsk_ref[0], nw_ref[0])
        dx, db, dc, dz, ddtr, dhin, ddtb, dal, ddsk, dnw = vjp((dyn_ref[...].astype(F32), dh_ref[...]))
        dx_ref[...] = dx
        db_ref[...] = db
        dc_ref[...] = dc
        dz_ref[...] = dz.astype(dz_ref.dtype)
        ddtr_ref[0] = ddtr
        dh_ref[...] = dhin
        ddtb_ref[0] += ddtb
        dal_ref[0] += dal
        ddsk_ref[0] += ddsk
        dnw_ref[0] += dnw

    sd = jax.ShapeDtypeStruct
    return pl.pallas_call(
        body, name=name, grid=(g, nc),
        in_specs=[sp["x"], sp["bc"], sp["bc"], sp["x"], sp["dtr"], sp["h"], sp["pr"], sp["pr"], sp["pr"], sp["pw"],
                  sp["x"]],
        out_specs=[sp["x"], sp["bc"], sp["bc"], sp["x"], sp["dtr"], sp["pr"], sp["pr"], sp["pr"], sp["pw"]],
        out_shape=[sd(xs.shape, F32), sd(bm.shape, F32), sd(cm.shape, F32), sd(z.shape, BF16), sd(dtr.shape, F32),
                   sd(dtb.shape, F32), sd(alog.shape, F32), sd(dsk.shape, F32), sd(nw.shape, F32)],
        scratch_shapes=[pltpu.VMEM((D_STATE, rp), F32)],
        compiler_params=_cp("parallel", "arbitrary"),
    )(xs, bm, cm, z, dtr, hsave, dtb, alog, dsk, nw, dyn)


def _s5_param_fn(lam_re, lam_im, log_dt, bt_re, bt_im):
    lr = jnp.minimum(lam_re, EIG_MAX)
    dt = jnp.exp(log_dt)
    mag = jnp.exp(lr * dt)
    lb_re = mag * jnp.cos(lam_im * dt)
    lb_im = mag * jnp.sin(lam_im * dt)
    n_re = lb_re - 1.0
    den = lr * lr + lam_im * lam_im
    k_re = (n_re * lr + lb_im * lam_im) / den
    k_im = (lb_im * lr - n_re * lam_im) / den
    return lb_re, lb_im, k_re * bt_re - k_im * bt_im, k_re * bt_im + k_im * bt_re


def _s5_params(lam_re, lam_im, log_dt, bt_re, bt_im, cts=None, *, name):
    args = (lam_re, lam_im, log_dt, bt_re, bt_im)
    n = len(args)

    def body(*refs):
        vals = [r[...] for r in refs[:n]]
        if cts is None:
            res = _s5_param_fn(*vals)
        else:
            _, vjp = jax.vjp(_s5_param_fn, *vals)
            res = vjp(tuple(r[...] for r in refs[n:n + 4]))
        for o_ref, v in zip(refs[-len(res):], res):
            o_ref[...] = v

    if cts is None:
        out = [lam_re, lam_im, bt_re, bt_im]
        ins = args
    else:
        out = list(args)
        ins = args + tuple(cts)
    return pl.pallas_call(
        body, name=name, out_shape=[jax.ShapeDtypeStruct(a.shape, F32) for a in out],
        compiler_params=pltpu.CompilerParams(vmem_limit_bytes=VMEM_LIMIT),
    )(*ins)


SCAN_COLS = 512


def _scan_specs(tb, time_of):
    row = pl.BlockSpec((tb, SCAN_COLS), lambda j, i: (time_of(i), j))
    par = pl.BlockSpec((1, SCAN_COLS), lambda j, i: (0, j))
    return row, par


def _s5_scan_fwd(bu_re, bu_im, lb_re, lb_im, *, name, tb=256):
    t, c = bu_re.shape
    nj = c // SCAN_COLS
    row, par = _scan_specs(tb, lambda i: i)

    def body(bre_ref, bim_ref, lre_ref, lim_ref, sre_ref, sim_ref, cre_ref, cim_ref):
        @pl.when(pl.program_id(1) == 0)
        def _():
            cre_ref[...] = jnp.zeros_like(cre_ref)
            cim_ref[...] = jnp.zeros_like(cim_ref)
        ar, ai = lre_ref[...], lim_ref[...]

        def step(k, s):
            sr, si = s
            nr = ar * sr - ai * si + bre_ref[pl.ds(k, 1), :]
            ni = ar * si + ai * sr + bim_ref[pl.ds(k, 1), :]
            sre_ref[pl.ds(k, 1), :] = nr
            sim_ref[pl.ds(k, 1), :] = ni
            return nr, ni

        sr, si = lax.fori_loop(0, tb, step, (cre_ref[...], cim_ref[...]), unroll=8)
        cre_ref[...] = sr
        cim_ref[...] = si

    return pl.pallas_call(
        body, name=name, grid=(nj, t // tb), in_specs=[row, row, par, par], out_specs=[row, row],
        out_shape=[jax.ShapeDtypeStruct((t, c), F32)] * 2,
        scratch_shapes=[pltpu.VMEM((1, SCAN_COLS), F32)] * 2,
        compiler_params=_cp("parallel", "arbitrary"),
    )(bu_re, bu_im, lb_re, lb_im)


def _s5_scan_bwd(s_re, s_im, ds_re, ds_im, lb_re, lb_im, *, name, tb=256):
    t, c = s_re.shape
    nj = c // SCAN_COLS
    nt = t // tb
    rev = lambda i: nt - 1 - i
    row, par = _scan_specs(tb, rev)
    prev = pl.BlockSpec((HALO, SCAN_COLS), lambda j, i: (jnp.maximum(rev(i) * (tb // HALO) - 1, 0), j))

    def body(sre_ref, sim_ref, pre_ref, pim_ref, dre_ref, dim_ref, lre_ref, lim_ref,
             gre_ref, gim_ref, dlre_ref, dlim_ref, cre_ref, cim_ref, ext_re, ext_im):
        step_id = pl.program_id(1)

        @pl.when(step_id == 0)
        def _():
            cre_ref[...] = jnp.zeros_like(cre_ref)
            cim_ref[...] = jnp.zeros_like(cim_ref)
            dlre_ref[...] = jnp.zeros_like(dlre_ref)
            dlim_ref[...] = jnp.zeros_like(dlim_ref)
        ar, ai = lre_ref[...], lim_ref[...]

        def step(k, g):
            gr, gi = g
            row = tb - 1 - k
            nr = dre_ref[pl.ds(row, 1), :] + ar * gr + ai * gi
            ni = dim_ref[pl.ds(row, 1), :] + ar * gi - ai * gr
            gre_ref[pl.ds(row, 1), :] = nr
            gim_ref[pl.ds(row, 1), :] = ni
            return nr, ni

        gr, gi = lax.fori_loop(0, tb, step, (cre_ref[...], cim_ref[...]), unroll=8)
        cre_ref[...] = gr
        cim_ref[...] = gi
        has_past = step_id != nt - 1
        ext_re[pl.ds(0, HALO), :] = jnp.where(has_past, pre_ref[...], 0.0)
        ext_im[pl.ds(0, HALO), :] = jnp.where(has_past, pim_ref[...], 0.0)
        ext_re[pl.ds(HALO, tb), :] = sre_ref[...]
        ext_im[pl.ds(HALO, tb), :] = sim_ref[...]
        pr, pi = ext_re[pl.ds(HALO - 1, tb), :], ext_im[pl.ds(HALO - 1, tb), :]
        g_re, g_im = gre_ref[...], gim_ref[...]
        dlre_ref[...] += jnp.sum(pr * g_re + pi * g_im, axis=0, keepdims=True)
        dlim_ref[...] += jnp.sum(pr * g_im - pi * g_re, axis=0, keepdims=True)

    return pl.pallas_call(
        body, name=name, grid=(nj, nt),
        in_specs=[row, row, prev, prev, row, row, par, par], out_specs=[row, row, par, par],
        out_shape=[jax.ShapeDtypeStruct((t, c), F32)] * 2 + [jax.ShapeDtypeStruct((1, c), F32)] * 2,
        scratch_shapes=[pltpu.VMEM((1, SCAN_COLS), F32)] * 2 + [pltpu.VMEM((HALO + tb, SCAN_COLS), F32)] * 2,
        compiler_params=_cp("parallel", "arbitrary"),
    )(s_re, s_im, s_re, s_im, ds_re, ds_im, lb_re, lb_im)


def _loss_fn(h, w, tgt):
    err = _rms_fn(h, w)[0] - tgt
    return 0.5 * jnp.sum(jnp.mean(err * err, axis=-1, keepdims=True), axis=0, keepdims=True)


def _loss_head(h, w, tgt, *, name, tb=256):
    t, d = h.shape

    def body(h_ref, w_ref, t_ref, loss_ref, dh_ref, dhb_ref, dw_ref):
        @pl.when(pl.program_id(0) == 0)
        def _():
            loss_ref[...] = jnp.zeros_like(loss_ref)
            dw_ref[...] = jnp.zeros_like(dw_ref)
        part, vjp = jax.vjp(_loss_fn, h_ref[...], w_ref[...], t_ref[...])
        dh, dw, _ = vjp(jnp.ones((1, 1), F32))
        loss_ref[...] += jnp.broadcast_to(part, loss_ref.shape)
        dh_ref[...] = dh
        dhb_ref[...] = dh.astype(BF16)
        dw_ref[...] += dw

    row = pl.BlockSpec((tb, d), lambda i: (i, 0))
    par = pl.BlockSpec((1, d), lambda i: (0, 0))
    return pl.pallas_call(
        body, name=name, grid=(t // tb,), in_specs=[row, par, row],
        out_specs=[pl.BlockSpec((SUBLANES, LANES), lambda i: (0, 0)), row, row, par],
        out_shape=[jax.ShapeDtypeStruct((SUBLANES, LANES), F32), jax.ShapeDtypeStruct((t, d), F32),
                   jax.ShapeDtypeStruct((t, d), BF16), jax.ShapeDtypeStruct((1, d), F32)],
        compiler_params=_cp("arbitrary"),
    )(h, w, tgt)


def _adamw(w, g, m, v, *, name):
    r, c = w.shape
    tr = _tile(r, 256, SUBLANES)

    def body(w_ref, g_ref, m_ref, v_ref, d_ref, nm_ref, nv_ref):
        g = g_ref[...]
        nm = ADAM_B1 * m_ref[...] + (1.0 - ADAM_B1) * g
        nv = ADAM_B2 * v_ref[...] + (1.0 - ADAM_B2) * (g * g)
        m_hat = nm / (1.0 - ADAM_B1 ** ADAM_STEP)
        v_hat = nv / (1.0 - ADAM_B2 ** ADAM_STEP)
        d_ref[...] = -ADAM_LR * (m_hat / (jnp.sqrt(v_hat) + ADAM_EPS) + ADAM_WD * w_ref[...])
        nm_ref[...] = nm
        nv_ref[...] = nv

    spec = pl.BlockSpec((tr, c), lambda i: (i, 0))
    return pl.pallas_call(
        body, name=name, grid=(r // tr,), in_specs=[spec] * 4, out_specs=[spec] * 3,
        out_shape=[jax.ShapeDtypeStruct((r, c), F32)] * 3, compiler_params=_cp("parallel"),
    )(w, g, m, v)


def _sum_parts(parts, *, name):
    _, r, c = parts.shape
    tr = _tile(r, 128, SUBLANES)

    def body(p_ref, o_ref):
        acc = p_ref[0].astype(F32)
        for k in range(1, N_DEV):
            acc = acc + p_ref[k].astype(F32)
        o_ref[...] = acc

    return pl.pallas_call(
        body, name=name, grid=(r // tr,), in_specs=[pl.BlockSpec((N_DEV, tr, c), lambda i: (0, i, 0))],
        out_specs=pl.BlockSpec((tr, c), lambda i: (i, 0)), out_shape=jax.ShapeDtypeStruct((r, c), F32),
        compiler_params=_cp("parallel"),
    )(parts)


def _position():
    return lax.axis_index("x"), lax.axis_index("y"), lax.axis_index("c")


def _flat(px, py, pc):
    return 4 * px + 2 * py + pc


def _all_gather(shard, *, name):
    def body(x_ref, out_ref, send_sems, recv_sems, local_sem):
        x, y, c = _position()
        me, sibling = (x, y, c), (x, y, 1 - c)
        chips = [(1 - x, y), (x, 1 - y), (1 - x, 1 - y)]

        def copy(k, block, to, src=None):
            slot = out_ref.at[_flat(*block)]
            return pltpu.make_async_remote_copy(
                src_ref=slot if src is None else src, dst_ref=slot, send_sem=send_sems.at[k],
                recv_sem=recv_sems.at[k], device_id=to, device_id_type=MESH)

        mine = pltpu.make_async_copy(x_ref, out_ref.at[_flat(*me)], local_sem)
        mine.start()
        first = [copy(0, me, sibling, src=x_ref)]
        first += [copy(1 + j, me, (*chip, c), src=x_ref) for j, chip in enumerate(chips)]
        for cp in first:
            cp.start()
        passed = [copy(4 + j, (*chip, c), sibling) for j, chip in enumerate(chips)]
        for j, chip in enumerate(chips):
            copy(1 + j, (*chip, c), me).wait_recv()
            passed[j].start()
        copy(0, sibling, me).wait_recv()
        for j, chip in enumerate(chips):
            copy(4 + j, (*chip, 1 - c), me).wait_recv()
        for cp in first + passed:
            cp.wait_send()
        mine.wait()

    return pl.pallas_call(
        body, name=name, out_shape=jax.ShapeDtypeStruct((N_DEV,) + shard.shape, shard.dtype),
        in_specs=[pl.BlockSpec(memory_space=pl.ANY)], out_specs=pl.BlockSpec(memory_space=pl.ANY),
        scratch_shapes=[pltpu.SemaphoreType.DMA((7,)), pltpu.SemaphoreType.DMA((7,)), pltpu.SemaphoreType.DMA(())],
    )(shard)


_HBM = pl.BlockSpec(memory_space=pltpu.HBM)
_SEM = pl.BlockSpec(memory_space=pltpu.SEMAPHORE)
_EFFECT = pltpu.SideEffectType.DATAFLOW_SIDE_EFFECTING


def _copy_ends(src_ref, land_ref, mode, me, to, piece=(0, 1)):
    if mode == "gather_slot":
        src, dst = src_ref, land_ref.at[me]
    elif mode == "gather_cols":
        w = src_ref.shape[1]
        src, dst = src_ref, land_ref.at[:, pl.ds(pl.multiple_of(me * w, LANES), w)]
    elif mode == "scatter_slot":
        src, dst = src_ref.at[to], land_ref.at[me]
    else:
        w = land_ref.shape[2]
        src, dst = src_ref.at[:, pl.ds(pl.multiple_of(to * w, LANES), w)], land_ref.at[me]
    i, n = piece
    if n > 1:
        rows = src.shape[0] // n
        src, dst = src.at[pl.ds(i * rows, rows)], dst.at[pl.ds(i * rows, rows)]
    return src, dst


BF16_ROWS = 16
LOCAL_PIECES = 8


def _pieces(src, mode):
    rows = src.shape[1] if mode == "scatter_slot" else src.shape[0]
    n = LOCAL_PIECES
    while n > 1 and rows % (n * BF16_ROWS):
        n //= 2
    return n


def _land_shape(src, mode):
    if mode == "gather_slot":
        return (N_DEV,) + src.shape
    if mode == "gather_cols":
        return (src.shape[0], N_DEV * src.shape[1])
    if mode == "scatter_slot":
        return src.shape
    return (N_DEV, src.shape[0], src.shape[1] // N_DEV)


def _exchange_copies(src_ref, land_ref, send_sems, recv_sems, mode, n_pieces):
    x, y, c = _position()
    me = _flat(x, y, c)
    copies = []
    for k in range(1, N_DEV):
        peer = (x ^ ((k >> 2) & 1), y ^ ((k >> 1) & 1), c ^ (k & 1))
        n = n_pieces if k == 1 else 1
        for i in range(n):
            src, dst = _copy_ends(src_ref, land_ref, mode, me, _flat(*peer), (i, n))
            copies.append(pltpu.make_async_remote_copy(
                src_ref=src, dst_ref=dst, send_sem=send_sems.at[len(copies)], recv_sem=recv_sems.at[len(copies)],
                device_id=peer, device_id_type=MESH))
    return copies


def _place_own(src, mode, *, name):
    n = _pieces(src, mode)

    def body(src_ref, land_ref, sems):
        me = _flat(*_position())
        copies = [pltpu.make_async_copy(*_copy_ends(src_ref, land_ref, mode, me, me, (i, n)), sems.at[i])
                  for i in range(n)]
        for cp in copies:
            cp.start()
        for cp in copies:
            cp.wait()

    return pl.pallas_call(
        body, name=name, out_shape=jax.ShapeDtypeStruct(_land_shape(src, mode), src.dtype),
        in_specs=[pl.BlockSpec(memory_space=pl.ANY)], out_specs=pl.BlockSpec(memory_space=pl.ANY),
        scratch_shapes=[pltpu.SemaphoreType.DMA((n,))],
    )(src)


def _exchange_start(src, mode, *, name):
    land = _place_own(src, mode, name=name + "_own")
    n_pieces = _pieces(src, mode)
    n_copies = N_DEV - 2 + n_pieces

    def body(src_ref, land_ref, send_sems, recv_sems, src_thru, land_thru, token):
        for cp in _exchange_copies(src_ref, land_ref, send_sems, recv_sems, mode, n_pieces):
            cp.start()
        token[...] = jnp.zeros_like(token)

    hbm = pltpu.with_memory_space_constraint
    *handle, token = pl.pallas_call(
        body, name=name,
        out_shape=(pltpu.SemaphoreType.DMA((n_copies,)), pltpu.SemaphoreType.DMA((n_copies,)),
                   pltpu.HBM(src.shape, src.dtype), pltpu.HBM(land.shape, land.dtype),
                   jax.ShapeDtypeStruct((SUBLANES, LANES), F32)),
        in_specs=(_HBM, _HBM), out_specs=(_SEM, _SEM, _HBM, _HBM, pl.BlockSpec(memory_space=pltpu.VMEM)),
        input_output_aliases={0: 2, 1: 3}, compiler_params=pltpu.CompilerParams(has_side_effects=_EFFECT),
    )(hbm(src, pltpu.HBM), hbm(land, pltpu.HBM))
    return (tuple(handle), mode), token


def _exchange_wait(pending, after, *, name):
    (send_sems, recv_sems, src_thru, land_thru), mode = pending
    n_pieces = _pieces(src_thru, mode)

    def body(src_ref, land_ref, send_sems, recv_sems, after_ref, src_dead, got_ref):
        for cp in _exchange_copies(src_ref, land_ref, send_sems, recv_sems, mode, n_pieces):
            cp.wait_send()
            cp.wait_recv()

    return pl.pallas_call(
        body, name=name, out_shape=(pltpu.HBM(src_thru.shape, src_thru.dtype), pltpu.HBM(land_thru.shape, land_thru.dtype)),
        in_specs=(_HBM, _HBM, _SEM, _SEM, pl.BlockSpec(memory_space=pl.ANY)), out_specs=(_HBM, _HBM),
        input_output_aliases={0: 0, 1: 1}, compiler_params=pltpu.CompilerParams(has_side_effects=_EFFECT),
    )(src_thru, land_thru, send_sems, recv_sems, after)[1]


def _after(x, token):
    return x + token[0, 0].astype(x.dtype)


def _pad_cols(a, mult):
    pad = -a.shape[1] % mult
    return jnp.pad(a, ((0, 0), (0, pad))) if pad else a


def _pack(arrs, cols):
    flat = jnp.concatenate([a.reshape(-1).astype(F32) for a in arrs])
    sizes = [int(a.size) for a in arrs]
    flat = jnp.pad(flat, (0, -flat.shape[0] % (SUBLANES * cols)))
    return flat.reshape(-1, cols), sizes


def _unpack(flat2d, sizes, shapes):
    flat = flat2d.reshape(-1)
    out, o = [], 0
    for n, s in zip(sizes, shapes):
        out.append(flat[o:o + n].reshape(s))
        o += n
    return out


PACK_COLS = SUBLANES * LANES


def kernel(x, norm_mix_w, w_in, conv_a_w, conv_a_b, dt_bias, a_log, d_a, norm_a_w, w_proj_a, s5_lam_re, s5_lam_im, s5_log_dt, s5_b_re, s5_b_im, s5_c_re, s5_c_im, s5_d, w_s5_glu, w_out, norm_ffn_w, w_up, conv_ffn_w, conv_ffn_b, w_down, norm_final_w, loss_target, m_norm_mix_w, m_w_in, m_conv_a_w, m_conv_a_b, m_dt_bias, m_a_log, m_d_a, m_norm_a_w, m_w_proj_a, m_s5_lam_re, m_s5_lam_im, m_s5_log_dt, m_s5_b_re, m_s5_b_im, m_s5_c_re, m_s5_c_im, m_s5_d, m_w_s5_glu, m_w_out, m_norm_ffn_w, m_w_up, m_conv_ffn_w, m_conv_ffn_b, m_w_down, m_norm_final_w, v_norm_mix_w, v_w_in, v_conv_a_w, v_conv_a_b, v_dt_bias, v_a_log, v_d_a, v_norm_a_w, v_w_proj_a, v_s5_lam_re, v_s5_lam_im, v_s5_log_dt, v_s5_b_re, v_s5_b_im, v_s5_c_re, v_s5_c_im, v_s5_d, v_w_s5_glu, v_w_out, v_norm_ffn_w, v_w_up, v_conv_ffn_w, v_conv_ffn_b, v_w_down, v_norm_final_w):
    weights = dict(norm_mix_w=norm_mix_w, w_in=w_in, conv_a_w=conv_a_w, conv_a_b=conv_a_b, dt_bias=dt_bias, a_log=a_log, d_a=d_a, norm_a_w=norm_a_w, w_proj_a=w_proj_a, s5_lam_re=s5_lam_re, s5_lam_im=s5_lam_im, s5_log_dt=s5_log_dt, s5_b_re=s5_b_re, s5_b_im=s5_b_im, s5_c_re=s5_c_re, s5_c_im=s5_c_im, s5_d=s5_d, w_s5_glu=w_s5_glu, w_out=w_out, norm_ffn_w=norm_ffn_w, w_up=w_up, conv_ffn_w=conv_ffn_w, conv_ffn_b=conv_ffn_b, w_down=w_down, norm_final_w=norm_final_w)
    moms = dict(norm_mix_w=m_norm_mix_w, w_in=m_w_in, conv_a_w=m_conv_a_w, conv_a_b=m_conv_a_b, dt_bias=m_dt_bias, a_log=m_a_log, d_a=m_d_a, norm_a_w=m_norm_a_w, w_proj_a=m_w_proj_a, s5_lam_re=m_s5_lam_re, s5_lam_im=m_s5_lam_im, s5_log_dt=m_s5_log_dt, s5_b_re=m_s5_b_re, s5_b_im=m_s5_b_im, s5_c_re=m_s5_c_re, s5_c_im=m_s5_c_im, s5_d=m_s5_d, w_s5_glu=m_w_s5_glu, w_out=m_w_out, norm_ffn_w=m_norm_ffn_w, w_up=m_w_up, conv_ffn_w=m_conv_ffn_w, conv_ffn_b=m_conv_ffn_b, w_down=m_w_down, norm_final_w=m_norm_final_w)
    vars_ = dict(norm_mix_w=v_norm_mix_w, w_in=v_w_in, conv_a_w=v_conv_a_w, conv_a_b=v_conv_a_b, dt_bias=v_dt_bias, a_log=v_a_log, d_a=v_d_a, norm_a_w=v_norm_a_w, w_proj_a=v_w_proj_a, s5_lam_re=v_s5_lam_re, s5_lam_im=v_s5_lam_im, s5_log_dt=v_s5_log_dt, s5_b_re=v_s5_b_re, s5_b_im=v_s5_b_im, s5_c_re=v_s5_c_re, s5_c_im=v_s5_c_im, s5_d=v_s5_d, w_s5_glu=v_w_s5_glu, w_out=v_w_out, norm_ffn_w=v_norm_ffn_w, w_up=v_w_up, conv_ffn_w=v_conv_ffn_w, conv_ffn_b=v_conv_ffn_b, w_down=v_w_down, norm_final_w=v_norm_final_w)
    names = list(weights)
    col_sharded = ("w_in", "w_s5_glu", "w_up")
    row_sharded = ("w_proj_a", "w_out", "w_down")
    conv_sharded = ("conv_a_w", "conv_ffn_w")
    replicated = [n for n in names if n not in col_sharded + row_sharded + conv_sharded]

    x2, tgt = x[0], loss_target[0]
    t, d = x2.shape
    nh = dt_bias.shape[-1]
    d_inner = norm_a_w.shape[-1]
    conv_dim = conv_a_b.shape[-1]
    gn = (conv_dim - d_inner) // 2
    ng = gn // D_STATE
    nr = nh // ng
    rp = d_inner // ng
    d_s5 = s5_d.shape[-1]
    gs, ps = s5_lam_re.shape[1:]
    cs = d_s5 // gs
    n_oct = gs // 8
    assert 8 * ps == SCAN_COLS and 8 * cs == LANES and gs % 8 == 0
    d_ff = w_down.shape[1] * N_DEV
    dev = _flat(*_position())

    ka, kf = conv_a_w.shape[1], conv_ffn_w.shape[1]
    taps = jnp.concatenate([conv_a_w[0].reshape(1, -1), conv_ffn_w[0].reshape(1, -1)], axis=1)
    taps = _all_gather(taps, name="ag_conv_taps")[:, 0]
    def by_cols(n):
        return n in ("w_s5_glu", "w_up") and weights[n].shape[2] % LANES == 0

    pending, started = {}, taps[:1, :1] * 0.0
    for n in ("w_in", "w_proj_a", "w_s5_glu", "w_out", "w_up", "w_down"):
        pending[n], token = _exchange_start(weights[n][0].astype(BF16), "gather_cols" if by_cols(n) else "gather_slot",
                                            name="ag_" + n)
        started = started + token[:1, :1]

    def gathered(n, after):
        g = _exchange_wait(pending[n], after, name="agw_" + n)
        if by_cols(n) or n == "w_in":
            return g
        if n in row_sharded:
            return g.reshape(-1, g.shape[2])
        return jnp.transpose(g, (1, 0, 2)).reshape(g.shape[1], -1)

    seg_sizes = dict(z=d_inner, xs=d_inner, bm=gn, cm=gn, dt=nh, u=d_s5, ga=d, gb=d)
    seg_names = tuple(seg_sizes)
    pieces, seg_at = _w_in_pieces(seg_sizes, ("z", "xs", "ga", "gb", "bm", "cm", "u", "dt"), w_in.shape[2])
    na = ka * conv_a_w.shape[2]
    cw_a = jnp.transpose(taps[:, :na].reshape(N_DEV, ka, -1), (1, 0, 2)).reshape(ka, conv_dim)
    cw_f = jnp.transpose(taps[:, na:].reshape(N_DEV, kf, -1), (1, 0, 2)).reshape(kf, 2 * d_ff)
    cb_a, cb_f = conv_a_b, conv_ffn_b
    a_cols = {"xs": slice(0, d_inner), "bm": slice(d_inner, d_inner + gn), "cm": slice(d_inner + gn, conv_dim)}

    w1 = norm_mix_w.reshape(1, 1, d) + started[0, 0]
    hn1, = _blocked_fwd(_rms_fn, [x2], [w1], [(d, BF16)], tb=256, name="rms1")
    w_in_p = _w_in_pack(gathered("w_in", hn1), pieces, seg_at, seg_sizes, name="w_in_pack")
    pre = {sn: _mm(hn1, w_in_p, b_win=seg_at[sn], name="in_" + sn) for sn in seg_names}
    act_a = {sn: _conv_fwd(_comb_silu, [pre[sn]], [cw_a[:, a_cols[sn]]], [cb_a[:, a_cols[sn]]], out_dtype=F32,
                           name="conv_a_" + sn) for sn in a_cols}
    dtr3 = jnp.transpose(pre["dt"][:, :nh].reshape(t, ng, nr), (1, 0, 2))
    dtb3, alog3, dsk3 = (p.reshape(ng, 1, nr) for p in (dt_bias, a_log, d_a))
    nw3 = norm_a_w.reshape(ng, 1, rp)
    yn, hsave = _ssd_fwd(act_a["xs"], act_a["bm"], act_a["cm"], pre["z"], dtr3, dtb3, alog3, dsk3, nw3, name="ssd")
    w_proj = gathered("w_proj_a", yn)
    y_a = _mm(yn, w_proj, name="proj_a")

    lam_re3, lam_im3 = s5_lam_re[0][:, None, :], s5_lam_im[0][:, None, :]
    logdt3 = s5_log_dt[0][:, None, None]
    bt_re, bt_im = jnp.transpose(s5_b_re[0], (0, 2, 1)), jnp.transpose(s5_b_im[0], (0, 2, 1))
    lb_re3, lb_im3, bb_re, bb_im = _s5_params(lam_re3, lam_im3, logdt3, bt_re, bt_im, name="s5_params")
    eye = jnp.eye(8, dtype=F32)

    def diag_b(bt):
        return (bt.reshape(n_oct, 8, cs, 1, ps) * eye[None, :, None, :, None]).reshape(n_oct, 8 * cs, 8 * ps)

    def undiag_b(blk):
        return (blk.reshape(n_oct, 8, cs, 8, ps) * eye[None, :, None, :, None]).sum(axis=3).reshape(gs, cs, ps)

    def diag_c(cm):
        ct = jnp.transpose(cm.reshape(n_oct, 8, cs, ps), (0, 1, 3, 2))
        return (ct[:, :, :, None, :] * eye[None, :, None, :, None]).reshape(n_oct, 8 * ps, 8 * cs)

    def undiag_c(blk):
        ct = (blk.reshape(n_oct, 8, ps, 8, cs) * eye[None, :, None, :, None]).sum(axis=3)
        return jnp.transpose(ct, (0, 1, 3, 2)).reshape(gs, cs, ps)

    b_blk_re, b_blk_im = diag_b(bb_re), diag_b(bb_im)
    c_blk_re, c_blk_imn = diag_c(s5_c_re[0]), diag_c(-s5_c_im[0])
    d3 = s5_d.reshape(n_oct, 1, LANES)
    lb_re, lb_im = lb_re3.reshape(1, gs * ps), lb_im3.reshape(1, gs * ps)
    u = pre["u"]
    bu_re, bu_im = _blocked_fwd(_s5_bu_fn, [u], [b_blk_re, b_blk_im], [(gs * ps, F32)] * 2, nj=n_oct, tb=512,
                                name="s5_bu")
    s_re, s_im = _s5_scan_fwd(bu_re, bu_im, lb_re, lb_im, name="s5_scan")
    yb, = _blocked_fwd(_s5_out_fn, [s_re, s_im, u], [c_blk_re, c_blk_imn, d3], [(d_s5, BF16)], nj=n_oct, tb=512,
                       name="s5_out")
    w_glu = gathered("w_s5_glu", yb)
    glu_v = _mm(yb, w_glu, b_win=(0, d), name="glu_v")
    glu_g = _mm(yb, w_glu, b_win=(d, d), name="glu_g")
    merged, = _blocked_fwd(_merge_fn, [glu_v, glu_g, pre["ga"], pre["gb"], y_a], [], [(d, BF16)], tb=256,
                           name="merge")
    w_o = gathered("w_out", merged)
    h1 = _mm(merged, w_o, acc=x2, name="out_proj")
    w2 = norm_ffn_w.reshape(1, 1, d)
    hn2, = _blocked_fwd(_rms_fn, [h1], [w2], [(d, BF16)], tb=256, name="rms2")
    w_u = gathered("w_up", hn2)
    up_g = _mm(hn2, w_u, b_win=(0, d_ff), name="up_g")
    up_v = _mm(hn2, w_u, b_win=(d_ff, d_ff), name="up_v")
    f_w = [cw_f[:, :d_ff], cw_f[:, d_ff:]]
    f_b = [cb_f[:, :d_ff], cb_f[:, d_ff:]]
    act = _conv_fwd(_comb_glu, [up_g, up_v], f_w, f_b, out_dtype=BF16, name="conv_ffn")
    w_dn = gathered("w_down", act)
    h2 = _mm(act, w_dn, acc=h1, name="down")
    loss_tile, dh2, dh2_b, g_final = _loss_head(h2, norm_final_w.reshape(1, d), tgt, name="loss_head")

    grads, scattering = {}, {}

    def scatter_start(n, g):
        if by_cols(n):
            src, mode = g, "scatter_cols"
        elif n in row_sharded:
            src, mode = g.reshape(N_DEV, -1, g.shape[1]), "scatter_slot"
        elif n == "w_in":
            src, mode = g, "scatter_slot"
        else:
            src, mode = jnp.transpose(g.reshape(g.shape[0], N_DEV, -1), (1, 0, 2)), "scatter_slot"
        scattering[n], token = _exchange_start(src, mode, name="rs_" + n)
        return token

    d_act = _mm(dh2_b, w_dn, tb=True, name="d_act")
    g_down = _mm(act, dh2_b, ta=True, out_dtype=BF16, name="g_w_down")
    tok = scatter_start("w_down", g_down)
    (dup_g, dwf_g, dbf_g), (dup_v, dwf_v, dbf_v) = _conv_bwd(
        _comb_glu, [up_g, up_v], f_w, [_after(f_b[0], tok), f_b[1]], d_act, dx_dtype=BF16, name="conv_ffn_bwd")
    dhn2 = _mm(dup_g, w_u, tb=True, b_win=(0, d_ff), name="d_hn2_g")
    dhn2 = _mm(dup_v, w_u, tb=True, b_win=(d_ff, d_ff), acc=dhn2, name="d_hn2_v")
    g_up = _mm(hn2, dup_g, ta=True, into=(lax.empty((d, 2 * d_ff), BF16), 0), name="g_w_up_g")
    g_up = _mm(hn2, dup_v, ta=True, into=(g_up, d_ff), name="g_w_up_v")
    tok = scatter_start("w_up", g_up)
    dh1, dh1_b, g_w2 = _blocked_bwd(_rms_fn, [h1], [_after(w2, tok)], [dhn2], [(F32, BF16)], adds={0: dh2}, tb=256,
                                    name="rms2_bwd")
    d_merged = _mm(dh1_b, w_o, tb=True, name="d_merged")
    g_out = _mm(merged, dh1_b, ta=True, out_dtype=BF16, name="g_w_out")
    tok = scatter_start("w_out", g_out)
    dglu_v, dglu_g, dga, dgb, dy_a = _blocked_bwd(
        _merge_fn, [glu_v, glu_g, pre["ga"], pre["gb"], y_a], [], [d_merged], [BF16] * 5, tb=128, name="merge_bwd")
    dyb = _mm(dglu_v, w_glu, tb=True, b_win=(0, d), name="d_yb_v")
    dyb = _mm(dglu_g, w_glu, tb=True, b_win=(d, d), acc=dyb, name="d_yb_g")
    g_glu = _mm(yb, dglu_v, ta=True, into=(lax.empty((d_s5, 2 * d), BF16), 0), name="g_w_glu_v")
    g_glu = _mm(yb, dglu_g, ta=True, into=(g_glu, d), name="g_w_glu_g")
    tok = tok + scatter_start("w_s5_glu", g_glu)
    ds_re, ds_im, du_skip, dc_blk_re, dc_blk_imn, dd3 = _blocked_bwd(
        _s5_out_fn, [s_re, s_im, u], [c_blk_re, c_blk_imn, _after(d3, tok)], [dyb], [F32, F32, F32], nj=n_oct, tb=512,
        name="s5_out_bwd")
    dbu_re, dbu_im, dlb_re, dlb_im = _s5_scan_bwd(s_re, s_im, ds_re, ds_im, lb_re, lb_im, name="s5_scan_bwd")
    du, db_blk_re, db_blk_im = _blocked_bwd(
        _s5_bu_fn, [u], [b_blk_re, b_blk_im], [dbu_re, dbu_im], [BF16], adds={0: du_skip}, nj=n_oct, tb=512,
        name="s5_bu_bwd")
    g_lre, g_lim, g_ldt, g_bt_re, g_bt_im = _s5_params(
        lam_re3, lam_im3, logdt3, bt_re, bt_im,
        cts=(dlb_re.reshape(gs, 1, ps), dlb_im.reshape(gs, 1, ps), undiag_b(db_blk_re), undiag_b(db_blk_im)),
        name="s5_params_bwd")
    grads["s5_lam_re"], grads["s5_lam_im"] = g_lre.reshape(s5_lam_re.shape), g_lim.reshape(s5_lam_im.shape)
    grads["s5_log_dt"] = g_ldt.reshape(s5_log_dt.shape)
    grads["s5_b_re"] = jnp.transpose(g_bt_re, (0, 2, 1)).reshape(s5_b_re.shape)
    grads["s5_b_im"] = jnp.transpose(g_bt_im, (0, 2, 1)).reshape(s5_b_im.shape)
    grads["s5_c_re"] = undiag_c(dc_blk_re).reshape(s5_c_re.shape)
    grads["s5_c_im"] = -undiag_c(dc_blk_imn).reshape(s5_c_im.shape)
    grads["s5_d"] = dd3.reshape(s5_d.shape)

    dyn = _mm(dy_a, w_proj, tb=True, name="d_yn")
    g_proj = _mm(yn, dy_a, ta=True, out_dtype=BF16, name="g_w_proj_a")
    tok = scatter_start("w_proj_a", g_proj)
    dxs, dbm, dcm, dz, ddtr3, g_dtb, g_alog, g_dsk, g_nw = _ssd_bwd(
        act_a["xs"], act_a["bm"], act_a["cm"], pre["z"], dtr3, hsave, dtb3, alog3, dsk3, _after(nw3, tok), dyn,
        name="ssd_bwd")
    grads["dt_bias"], grads["a_log"], grads["d_a"] = (g.reshape(1, nh) for g in (g_dtb, g_alog, g_dsk))
    grads["norm_a_w"] = g_nw.reshape(1, d_inner)
    dpre = {"z": dz, "u": du, "ga": dga, "gb": dgb}
    dcw, dcb = {}, {}
    for sn, dact in (("xs", dxs), ("bm", dbm), ("cm", dcm)):
        (dpre[sn], dcw[sn], dcb[sn]), = _conv_bwd(
            _comb_silu, [pre[sn]], [cw_a[:, a_cols[sn]]], [cb_a[:, a_cols[sn]]], dact, dx_dtype=BF16,
            name="conv_a_bwd_" + sn)
    dpre["dt"] = _pad_cols(jnp.transpose(ddtr3, (1, 0, 2)).reshape(t, nh), LANES).astype(BF16)
    g_in = _w_in_unpack({sn: _mm(hn1, dpre[sn], ta=True, name="g_w_in_" + sn) for sn in seg_names}, pieces,
                        w_in.shape[2], name="w_in_unpack")
    tok = scatter_start("w_in", g_in)
    dhn1 = _mm(_after(dpre["dt"], tok), w_in_p, tb=True, b_win=seg_at["dt"], name="d_hn1_dt")
    for sn in seg_names:
        if sn != "dt":
            dhn1 = _mm(dpre[sn], w_in_p, tb=True, b_win=seg_at[sn], acc=dhn1, name="d_hn1_" + sn)
    dx, g_w1 = _blocked_bwd(_rms_fn, [x2], [w1], [dhn1], [F32], adds={0: dh1}, tb=256, name="rms1_bwd")

    grads["norm_mix_w"], grads["norm_ffn_w"] = g_w1.reshape(1, d), g_w2.reshape(1, d)
    grads["norm_final_w"] = g_final.reshape(d)
    grads["conv_a_b"] = jnp.concatenate([dcb["xs"], dcb["bm"], dcb["cm"]], axis=1)
    grads["conv_ffn_b"] = jnp.concatenate([dbf_g, dbf_v], axis=1)
    g_cw_a = jnp.concatenate([dcw["xs"], dcw["bm"], dcw["cm"]], axis=1)
    g_cw_f = jnp.concatenate([dwf_g, dwf_v], axis=1)

    small = [grads[n] for n in replicated] + [g_cw_a, g_cw_f, loss_tile[:1, :1]]
    packed, sizes = _pack(small, PACK_COLS)
    summed = _sum_parts(_all_gather(packed, name="ag_small_grads"), name="sum_small_grads")
    *rep_sums, s_cw_a, s_cw_f, loss = _unpack(summed, sizes, [a.shape for a in small])
    for n, g in zip(replicated, rep_sums):
        grads[n] = g
    wa, wf = conv_a_w.shape[2], conv_ffn_w.shape[2]
    grads["conv_a_w"] = lax.dynamic_slice_in_dim(s_cw_a, dev * wa, wa, axis=1)[None]
    grads["conv_ffn_w"] = lax.dynamic_slice_in_dim(s_cw_f, dev * wf, wf, axis=1)[None]

    delta, new_m, new_v = {}, {}, {}
    done = dx
    for n in ("w_down", "w_up", "w_out", "w_s5_glu", "w_proj_a", "w_in"):
        land = _exchange_wait(scattering[n], done, name="rsw_" + n)
        shape = weights[n].shape
        two_d = lambda a: a.reshape(shape[-2], shape[-1])
        g = _sum_parts(land, name="rs_sum_" + n)
        grads[n] = g.reshape(shape)
        dl, nm, nv = _adamw(two_d(weights[n]), g, two_d(moms[n]), two_d(vars_[n]), name="adamw_" + n)
        delta[n], new_m[n], new_v[n] = dl.reshape(shape), nm.reshape(shape), nv.reshape(shape)
        done = dl
    small_names = replicated + list(conv_sharded)
    shapes = [weights[n].shape for n in small_names]
    pw, sizes = _pack([weights[n] for n in small_names], PACK_COLS)
    pg, _ = _pack([grads[n] for n in small_names], PACK_COLS)
    pm, _ = _pack([moms[n] for n in small_names], PACK_COLS)
    pv, _ = _pack([vars_[n] for n in small_names], PACK_COLS)
    dl, nm, nv = _adamw(pw, pg, pm, pv, name="adamw_small")
    for n, a, b, c in zip(small_names, _unpack(dl, sizes, shapes), _unpack(nm, sizes, shapes),
                          _unpack(nv, sizes, shapes)):
        delta[n], new_m[n], new_v[n] = a, b, c

    return (loss.reshape(()), dx[None], *[grads[n] for n in names], *[delta[n] for n in names],
            *[new_m[n] for n in names], *[new_v[n] for n in names])
```

```python
import functools

import jax
import jax.numpy as jnp
from jax import lax
from jax.experimental import pallas as pl
from jax.experimental.pallas import tpu as pltpu

F32 = jnp.float32
BF16 = jnp.bfloat16
HIGHEST = lax.Precision.HIGHEST
MESH = pl.DeviceIdType.MESH

EPS = 1e-6
EIG_MAX = -1e-4
D_STATE = 128
CHUNK = 256
ADAM_LR = 0.001
ADAM_B1 = 0.9
ADAM_B2 = 0.999
ADAM_EPS = 1e-08
ADAM_WD = 0.01
ADAM_STEP = 10
N_DEV = 8
LANES = 128
SUBLANES = 8
VMEM_LIMIT = 56 * 1024 * 1024
MM_MAX_K = 4096


def _cp(*sem):
    return pltpu.CompilerParams(dimension_semantics=sem, vmem_limit_bytes=VMEM_LIMIT)


def _tile(dim, pref, unit=LANES):
    if dim <= unit:
        return dim
    t = (min(pref, dim) // unit) * unit
    while dim % t:
        t -= unit
    return t


_DIMS = {"nn": (((1,), (0,)), ((), ())), "nt": (((1,), (1,)), ((), ())), "tn": (((0,), (0,)), ((), ()))}


def _dot(a, b, kind):
    return lax.dot_general(a.astype(BF16), b.astype(BF16), _DIMS[kind], preferred_element_type=F32)


@functools.partial(jax.custom_vjp, nondiff_argnums=(2,))
def _bdot(a, b, kind):
    return _dot(a, b, kind)


def _bdot_fwd(a, b, kind):
    return _dot(a, b, kind), (a, b)


def _bdot_bwd(kind, res, g):
    a, b = res
    if kind == "nn":
        return _dot(g, b, "nt"), _dot(a, g, "tn")
    if kind == "nt":
        return _dot(g, b, "nn"), _dot(g, a, "tn")
    return _dot(b, g, "nt"), _dot(a, g, "nn")


_bdot.defvjp(_bdot_fwd, _bdot_bwd)


def _mm(a, b, *, ta=False, tb=False, acc=None, out_dtype=F32, name, b_win=None, into=None):
    assert not (ta and tb)
    m, k = (a.shape[1], a.shape[0]) if ta else a.shape
    b_off, b_size = b_win or (0, b.shape[1])
    n = b.shape[0] if tb else b_size
    assert (b_size if tb else b.shape[0]) == k, (a.shape, b.shape, ta, tb, b_win)
    o_off = into[1] if into else 0
    nk = -(-k // MM_MAX_K)
    while k % nk or (k // nk) % LANES or (tb and b_off % (k // nk)):
        nk += 1
    tk = k // nk
    tm, tn = _tile(m, 1024 if tk <= 2048 else 512), _tile(n, 1024)
    while o_off % tn or (not tb and b_off % tn):
        tn = _tile(n, tn - LANES)
    kind = "tn" if ta else ("nt" if tb else "nn")
    a_spec = pl.BlockSpec((tk, tm), lambda i, j, l: (l, i)) if ta else pl.BlockSpec((tm, tk), lambda i, j, l: (i, l))
    if tb:
        b_spec = pl.BlockSpec((tn, tk), lambda i, j, l: (j, l + b_off // tk))
    else:
        b_spec = pl.BlockSpec((tk, tn), lambda i, j, l: (l, j + b_off // tn))
    c_spec = pl.BlockSpec((tm, tn), lambda i, j, l: (i, j))
    o_spec = pl.BlockSpec((tm, tn), lambda i, j, l: (i, j + o_off // tn))
    has_acc = acc is not None

    def body(*refs):
        a_ref, b_ref = refs[:2]
        c_ref = refs[2] if has_acc else None
        o_ref = refs[2 + has_acc + (into is not None)]
        if nk == 1:
            res = _dot(a_ref[...], b_ref[...], kind)
            if has_acc:
                res = res + c_ref[...].astype(F32)
            o_ref[...] = res.astype(o_ref.dtype)
            return
        acc_ref = refs[-1]
        l = pl.program_id(2)

        @pl.when(l == 0)
        def _():
            if has_acc:
                acc_ref[...] = c_ref[...].astype(F32)
            else:
                acc_ref[...] = jnp.zeros_like(acc_ref)

        acc_ref[...] += _dot(a_ref[...], b_ref[...], kind)

        @pl.when(l == nk - 1)
        def _():
            o_ref[...] = acc_ref[...].astype(o_ref.dtype)

    ins = [a, b] + ([acc] if has_acc else []) + ([into[0]] if into else [])
    in_specs = [a_spec, b_spec] + ([c_spec] if has_acc else []) + ([pl.BlockSpec(memory_space=pl.ANY)] if into else [])
    out_shape = jax.ShapeDtypeStruct(into[0].shape, into[0].dtype) if into else jax.ShapeDtypeStruct((m, n), out_dtype)
    return pl.pallas_call(
        body, name=name, grid=(m // tm, n // tn, nk), in_specs=in_specs, out_specs=o_spec, out_shape=out_shape,
        input_output_aliases={len(ins) - 1: 0} if into else {},
        scratch_shapes=[pltpu.VMEM((tm, tn), F32)] if nk > 1 else [],
        compiler_params=_cp("parallel", "parallel", "arbitrary"),
    )(*ins)


def _w_in_pieces(seg_sizes, seg_order, n_blk):
    layout, o = {}, 0
    for sn in seg_order:
        width = -(-seg_sizes[sn] // LANES) * LANES
        layout[sn] = (o, width)
        o += width
    pieces, start = [], 0
    for sn, sz in seg_sizes.items():
        lo = start
        while lo < start + sz:
            blk = lo // n_blk
            hi = min(start + sz, (blk + 1) * n_blk)
            pieces.append((blk, lo - blk * n_blk, sn, lo - start, layout[sn][0] + lo - start, hi - lo))
            lo = hi
        start += sz
    return pieces, layout


def _w_in_pack(gathered, pieces, layout, seg_sizes, *, name, tr=256):
    _, k, n_blk = gathered.shape
    n_pad = sum(w for _, w in layout.values())

    def body(g_ref, o_ref):
        for sn, (off, width) in layout.items():
            if width != seg_sizes[sn]:
                o_ref[:, pl.ds(off + seg_sizes[sn], width - seg_sizes[sn])] = jnp.zeros(
                    (tr, width - seg_sizes[sn]), o_ref.dtype)
        for blk, src, _, _, dst, width in pieces:
            o_ref[:, pl.ds(dst, width)] = g_ref[blk, :, pl.ds(src, width)]

    return pl.pallas_call(
        body, name=name, grid=(k // tr,), in_specs=[pl.BlockSpec((N_DEV, tr, n_blk), lambda i: (0, i, 0))],
        out_specs=pl.BlockSpec((tr, n_pad), lambda i: (i, 0)), out_shape=jax.ShapeDtypeStruct((k, n_pad), gathered.dtype),
        compiler_params=_cp("parallel"),
    )(gathered)


def _w_in_unpack(seg_grads, pieces, n_blk, *, name, tr=128):
    names = list(seg_grads)
    k = seg_grads[names[0]].shape[0]

    def body(*refs):
        o_ref = refs[-1]
        seg_ref = dict(zip(names, refs))
        for blk, dst, sn, src, _, width in pieces:
            o_ref[blk, :, pl.ds(dst, width)] = seg_ref[sn][:, pl.ds(src, width)].astype(o_ref.dtype)

    return pl.pallas_call(
        body, name=name, grid=(k // tr,),
        in_specs=[pl.BlockSpec((tr, seg_grads[sn].shape[1]), lambda i: (i, 0)) for sn in names],
        out_specs=pl.BlockSpec((N_DEV, tr, n_blk), lambda i: (0, i, 0)),
        out_shape=jax.ShapeDtypeStruct((N_DEV, k, n_blk), BF16), compiler_params=_cp("parallel"),
    )(*[seg_grads[sn] for sn in names])


def _row_spec(arr, tb, nj):
    return pl.BlockSpec((tb, arr.shape[1] // nj), lambda j, i: (i, j))


def _par_spec(arr):
    return pl.BlockSpec((1,) + arr.shape[1:], lambda j, i: (j, 0, 0))


def _blocked_fwd(fn, rows, params, outs, *, nj=1, tb, name):
    t = rows[0].shape[0]
    nr, npar = len(rows), len(params)

    def body(*refs):
        res = fn(*[r[...] for r in refs[:nr]], *[p[0] for p in refs[nr:nr + npar]])
        for o_ref, val in zip(refs[nr + npar:], res):
            o_ref[...] = val.astype(o_ref.dtype)

    return pl.pallas_call(
        body, name=name, grid=(nj, t // tb),
        in_specs=[_row_spec(a, tb, nj) for a in rows] + [_par_spec(p) for p in params],
        out_specs=[pl.BlockSpec((tb, c // nj), lambda j, i: (i, j)) for c, _ in outs],
        out_shape=[jax.ShapeDtypeStruct((t, c), dt) for c, dt in outs],
        compiler_params=_cp("parallel", "arbitrary"),
    )(*rows, *params)


def _blocked_bwd(fn, rows, params, cts, row_grad_dtypes, *, adds=None, nj=1, tb, name):
    t = rows[0].shape[0]
    nr, npar, nct = len(rows), len(params), len(cts)
    adds = adds or {}
    add_keys = sorted(adds)
    want, want_dtypes = [], []
    for k, dts in enumerate(row_grad_dtypes):
        for dt in (dts if isinstance(dts, tuple) else (dts,)):
            if dt is not None:
                want.append(k)
                want_dtypes.append(dt)

    def body(*refs):
        row_refs = refs[:nr]
        par_refs = refs[nr:nr + npar]
        ct_refs = refs[nr + npar:nr + npar + nct]
        add_refs = dict(zip(add_keys, refs[nr + npar + nct:nr + npar + nct + len(add_keys)]))
        out_refs = refs[nr + npar + nct + len(add_keys):]
        _, vjp = jax.vjp(fn, *[r[...] for r in row_refs], *[p[0] for p in par_refs])
        grads = vjp(tuple(c[...].astype(F32) for c in ct_refs))
        for o_ref, k in zip(out_refs, want):
            g = grads[k]
            if k in add_refs:
                g = g + add_refs[k][...].astype(F32)
            o_ref[...] = g.astype(o_ref.dtype)
        first = pl.program_id(1) == 0
        for o_ref, g in zip(out_refs[len(want):], grads[nr:]):
            @pl.when(first)
            def _(o_ref=o_ref):
                o_ref[...] = jnp.zeros_like(o_ref)
            o_ref[0] += g

    add_arrs = [adds[k] for k in add_keys]
    return pl.pallas_call(
        body, name=name, grid=(nj, t // tb),
        in_specs=[_row_spec(a, tb, nj) for a in rows] + [_par_spec(p) for p in params]
        + [_row_spec(c, tb, nj) for c in cts] + [_row_spec(a, tb, nj) for a in add_arrs],
        out_specs=[_row_spec(rows[k], tb, nj) for k in want] + [_par_spec(p) for p in params],
        out_shape=[jax.ShapeDtypeStruct(rows[k].shape, dt) for k, dt in zip(want, want_dtypes)]
        + [jax.ShapeDtypeStruct(p.shape, F32) for p in params],
        compiler_params=_cp("parallel", "arbitrary"),
    )(*rows, *params, *cts, *add_arrs)


def _rms_fn(x, w):
    return (x * lax.rsqrt(jnp.mean(x * x, axis=-1, keepdims=True) + EPS) * w,)


def _silu(x):
    return x * jax.nn.sigmoid(x)


def _merge_fn(glu_v, glu_g, g_a, g_b, y_a):
    y_b = glu_v * jax.nn.sigmoid(glu_g)
    return (jax.nn.sigmoid(g_a) * y_a + jax.nn.sigmoid(g_b) * y_b,)


def _s5_bu_fn(u, b_re, b_im):
    return _bdot(u, b_re, "nn"), _bdot(u, b_im, "nn")


def _s5_out_fn(s_re, s_im, u, c_re, c_im_neg, d):
    return (jax.nn.gelu(_bdot(s_re, c_re, "nn") + _bdot(s_im, c_im_neg, "nn") + d * u),)


HALO = SUBLANES


def _conv_fn(comb, kw, ns):
    def fn(*args):
        cs = []
        for s in range(ns):
            xp, xm, w, b = args[4 * s:4 * s + 4]
            xe = jnp.concatenate([xp, xm], axis=0)
            tb = xm.shape[0]
            y = b
            for k in range(kw):
                off = HALO - kw + 1 + k
                y = y + w[k:k + 1, :] * xe[off:off + tb, :]
            cs.append(y)
        return comb(*cs)
    return fn


def _conv_specs(xs, ws, bs, tb, cb, time_of):
    specs = []
    for x, w, b in zip(xs, ws, bs):
        specs += [
            pl.BlockSpec((HALO, cb), lambda j, i: (jnp.maximum(time_of(i) * (tb // HALO) - 1, 0), j)),
            pl.BlockSpec((tb, cb), lambda j, i: (time_of(i), j)),
            pl.BlockSpec((w.shape[0], cb), lambda j, i: (0, j)),
            pl.BlockSpec((1, cb), lambda j, i: (0, j)),
        ]
    return specs


def _conv_fwd(comb, xs, ws, bs, *, out_dtype, name, tb=512):
    t, c = xs[0].shape
    cb = _tile(c, 512)
    ns = len(xs)
    fn = _conv_fn(comb, ws[0].shape[0], ns)

    def body(*refs):
        i = pl.program_id(1)
        args = []
        for s in range(ns):
            xp_ref, xm_ref, w_ref, b_ref = refs[4 * s:4 * s + 4]
            xp = jnp.where(i == 0, 0.0, xp_ref[...])
            args += [xp, xm_ref[...], w_ref[...], b_ref[...]]
        refs[4 * ns][...] = fn(*args).astype(out_dtype)

    flat = [a for x, w, b in zip(xs, ws, bs) for a in (x, x, w, b)]
    return pl.pallas_call(
        body, name=name, grid=(c // cb, t // tb),
        in_specs=_conv_specs(xs, ws, bs, tb, cb, lambda i: i),
        out_specs=pl.BlockSpec((tb, cb), lambda j, i: (i, j)),
        out_shape=jax.ShapeDtypeStruct((t, c), out_dtype),
        compiler_params=_cp("parallel", "arbitrary"),
    )(*flat)


def _conv_bwd(comb, xs, ws, bs, dy, *, dx_dtype, name, tb=512):
    t, c = xs[0].shape
    cb = _tile(c, 512)
    ns = len(xs)
    nt = t // tb
    fn = _conv_fn(comb, ws[0].shape[0], ns)

    def body(*refs):
        step = pl.program_id(1)
        in_refs = refs[:4 * ns]
        dy_ref = refs[4 * ns]
        out_refs = refs[4 * ns + 1:4 * ns + 1 + 3 * ns]
        carry_refs = refs[4 * ns + 1 + 3 * ns:]
        args = []
        for s in range(ns):
            xp_ref, xm_ref, w_ref, b_ref = in_refs[4 * s:4 * s + 4]
            xp = jnp.where(step == nt - 1, 0.0, xp_ref[...])
            args += [xp, xm_ref[...], w_ref[...], b_ref[...]]
        _, vjp = jax.vjp(fn, *args)
        grads = vjp(dy_ref[...].astype(F32))
        for s in range(ns):
            dxp, dxm, dw, db = grads[4 * s:4 * s + 4]
            dx_ref, dw_ref, db_ref = out_refs[3 * s:3 * s + 3]
            carry = carry_refs[s]

            @pl.when(step == 0)
            def _(carry=carry, dw_ref=dw_ref, db_ref=db_ref):
                carry[...] = jnp.zeros_like(carry)
                dw_ref[...] = jnp.zeros_like(dw_ref)
                db_ref[...] = jnp.zeros_like(db_ref)

            tail = jnp.concatenate([jnp.zeros((tb - HALO, cb), F32), carry[...]], axis=0)
            dx_ref[...] = (dxm + tail).astype(dx_dtype)
            carry[...] = dxp
            dw_ref[...] += dw
            db_ref[...] += db

    flat = [a for x, w, b in zip(xs, ws, bs) for a in (x, x, w, b)]
    rev = lambda i: nt - 1 - i
    out_specs, out_shape = [], []
    for x, w, b in zip(xs, ws, bs):
        out_specs += [pl.BlockSpec((tb, cb), lambda j, i: (rev(i), j)),
                      pl.BlockSpec((w.shape[0], cb), lambda j, i: (0, j)),
                      pl.BlockSpec((1, cb), lambda j, i: (0, j))]
        out_shape += [jax.ShapeDtypeStruct((t, c), dx_dtype), jax.ShapeDtypeStruct(w.shape, F32),
                      jax.ShapeDtypeStruct(b.shape, F32)]
    res = pl.pallas_call(
        body, name=name, grid=(c // cb, nt),
        in_specs=_conv_specs(xs, ws, bs, tb, cb, rev) + [pl.BlockSpec((tb, cb), lambda j, i: (rev(i), j))],
        out_specs=out_specs, out_shape=out_shape,
        scratch_shapes=[pltpu.VMEM((HALO, cb), F32) for _ in range(ns)],
        compiler_params=_cp("parallel", "arbitrary"),
    )(*flat, dy)
    return [tuple(res[3 * s:3 * s + 3]) for s in range(ns)]


def _comb_silu(c):
    return _silu(c)


def _comb_glu(cg, cv):
    return _silu(cg) * cv


def _ssd_fn(nheads, hdim):
    def fn(x, bm, cm, z, dtr, hin, dtb, alog, dsk, nw):
        q = x.shape[0]
        dt = jax.nn.softplus(dtr + dtb)
        da = dt * (-jnp.exp(alog))
        li = lax.broadcasted_iota(jnp.int32, (q, q), 0)
        si = lax.broadcasted_iota(jnp.int32, (q, q), 1)
        causal = li >= si
        tri = causal.astype(F32)
        acs = jnp.dot(tri, da, precision=HIGHEST, preferred_element_type=F32)
        acs_row = lax.dot_general(da, tri, (((0,), (1,)), ((), ())), precision=HIGHEST,
                                  preferred_element_type=F32)
        cb = _bdot(cm, bm, "nt")
        ch = _bdot(cm, hin, "nn")
        ys, hs = [], []
        for r in range(nheads):
            cols = slice(r * hdim, (r + 1) * hdim)
            xr = x[:, cols]
            a_col = acs[:, r:r + 1]
            decay = jnp.exp(jnp.where(causal, a_col - acs_row[r:r + 1, :], -1e30))
            xd = xr * dt[:, r:r + 1]
            y_diag = _bdot(cb * decay, xd, "nn")
            y_off = ch[:, cols] * jnp.exp(a_col)
            last = acs[q - 1:q, r:r + 1]
            st = _bdot(bm * jnp.exp(last - a_col), xd, "tn")
            hs.append(jnp.exp(last) * hin[:, cols] + st)
            ys.append(y_diag + y_off + dsk[:, r:r + 1] * xr)
        y = jnp.concatenate(ys, axis=1) * _silu(z)
        yn = y * lax.rsqrt(jnp.mean(y * y, axis=-1, keepdims=True) + EPS) * nw
        return yn, jnp.concatenate(hs, axis=1)
    return fn


def _ssd_specs(rp, nr, time_of):
    row = lambda w: pl.BlockSpec((CHUNK, w), lambda g, c: (time_of(c), g))
    par = lambda w: pl.BlockSpec((1, 1, w), lambda g, c: (g, 0, 0))
    return dict(
        x=row(rp), bc=row(D_STATE), dtr=pl.BlockSpec((1, CHUNK, nr), lambda g, c: (g, time_of(c), 0)),
        h=pl.BlockSpec((1, 1, D_STATE, rp), lambda g, c: (g, time_of(c), 0, 0)), pr=par(nr), pw=par(rp))


def _ssd_fwd(xs, bm, cm, z, dtr, dtb, alog, dsk, nw, *, name):
    t = xs.shape[0]
    g, _, nr = dtr.shape
    rp = xs.shape[1] // g
    nc = t // CHUNK
    fn = _ssd_fn(nr, rp // nr)
    sp = _ssd_specs(rp, nr, lambda c: c)

    def body(x_ref, b_ref, c_ref, z_ref, dtr_ref, dtb_ref, al_ref, dsk_ref, nw_ref, yn_ref, hs_ref, h_ref):
        @pl.when(pl.program_id(1) == 0)
        def _():
            h_ref[...] = jnp.zeros_like(h_ref)
        hin = h_ref[...]
        hs_ref[0, 0] = hin
        yn, hout = fn(x_ref[...], b_ref[...], c_ref[...], z_ref[...], dtr_ref[0], hin,
                      dtb_ref[0], al_ref[0], dsk_ref[0], nw_ref[0])
        yn_ref[...] = yn.astype(yn_ref.dtype)
        h_ref[...] = hout

    return pl.pallas_call(
        body, name=name, grid=(g, nc),
        in_specs=[sp["x"], sp["bc"], sp["bc"], sp["x"], sp["dtr"], sp["pr"], sp["pr"], sp["pr"], sp["pw"]],
        out_specs=[sp["x"], sp["h"]],
        out_shape=[jax.ShapeDtypeStruct(xs.shape, BF16), jax.ShapeDtypeStruct((g, nc, D_STATE, rp), F32)],
        scratch_shapes=[pltpu.VMEM((D_STATE, rp), F32)],
        compiler_params=_cp("parallel", "arbitrary"),
    )(xs, bm, cm, z, dtr, dtb, alog, dsk, nw)


def _ssd_bwd(xs, bm, cm, z, dtr, hsave, dtb, alog, dsk, nw, dyn, *, name):
    t = xs.shape[0]
    g, _, nr = dtr.shape
    rp = xs.shape[1] // g
    nc = t // CHUNK
    fn = _ssd_fn(nr, rp // nr)
    sp = _ssd_specs(rp, nr, lambda c: nc - 1 - c)

    def body(x_ref, b_ref, c_ref, z_ref, dtr_ref, hs_ref, dtb_ref, al_ref, dsk_ref, nw_ref, dyn_ref,
             dx_ref, db_ref, dc_ref, dz_ref, ddtr_ref, ddtb_ref, dal_ref, ddsk_ref, dnw_ref, dh_ref):
        first = pl.program_id(1) == 0

        @pl.when(first)
        def _():
            dh_ref[...] = jnp.zeros_like(dh_ref)
            for r in (ddtb_ref, dal_ref, ddsk_ref, dnw_ref):
                r[...] = jnp.zeros_like(r)

        _, vjp = jax.vjp(fn, x_ref[...], b_ref[...], c_ref[...], z_ref[...], dtr_ref[0], hs_ref[0, 0],
                         dtb_ref[0], al_ref[0], dsk_ref[0], nw_ref[0])
        dx, db, dc, dz, ddtr, dhin, ddtb, dal, ddsk, dnw = vjp((dyn_ref[...].astype(F32), dh_ref[...]))
        dx_ref[...] = dx
        db_ref[...] = db
        dc_ref[...] = dc
        dz_ref[...] = dz.astype(dz_ref.dtype)
        ddtr_ref[0] = ddtr
        dh_ref[...] = dhin
        ddtb_ref[0] += ddtb
        dal_ref[0] += dal
        ddsk_ref[0] += ddsk
        dnw_ref[0] += dnw

    sd = jax.ShapeDtypeStruct
    return pl.pallas_call(
        body, name=name, grid=(g, nc),
        in_specs=[sp["x"], sp["bc"], sp["bc"], sp["x"], sp["dtr"], sp["h"], sp["pr"], sp["pr"], sp["pr"], sp["pw"],
                  sp["x"]],
        out_specs=[sp["x"], sp["bc"], sp["bc"], sp["x"], sp["dtr"], sp["pr"], sp["pr"], sp["pr"], sp["pw"]],
        out_shape=[sd(xs.shape, F32), sd(bm.shape, F32), sd(cm.shape, F32), sd(z.shape, BF16), sd(dtr.shape, F32),
                   sd(dtb.shape, F32), sd(alog.shape, F32), sd(dsk.shape, F32), sd(nw.shape, F32)],
        scratch_shapes=[pltpu.VMEM((D_STATE, rp), F32)],
        compiler_params=_cp("parallel", "arbitrary"),
    )(xs, bm, cm, z, dtr, hsave, dtb, alog, dsk, nw, dyn)


def _s5_param_fn(lam_re, lam_im, log_dt, bt_re, bt_im):
    lr = jnp.minimum(lam_re, EIG_MAX)
    dt = jnp.exp(log_dt)
    mag = jnp.exp(lr * dt)
    lb_re = mag * jnp.cos(lam_im * dt)
    lb_im = mag * jnp.sin(lam_im * dt)
    n_re = lb_re - 1.0
    den = lr * lr + lam_im * lam_im
    k_re = (n_re * lr + lb_im * lam_im) / den
    k_im = (lb_im * lr - n_re * lam_im) / den
    return lb_re, lb_im, k_re * bt_re - k_im * bt_im, k_re * bt_im + k_im * bt_re


def _s5_params(lam_re, lam_im, log_dt, bt_re, bt_im, cts=None, *, name):
    args = (lam_re, lam_im, log_dt, bt_re, bt_im)
    n = len(args)

    def body(*refs):
        vals = [r[...] for r in refs[:n]]
        if cts is None:
            res = _s5_param_fn(*vals)
        else:
            _, vjp = jax.vjp(_s5_param_fn, *vals)
            res = vjp(tuple(r[...] for r in refs[n:n + 4]))
        for o_ref, v in zip(refs[-len(res):], res):
            o_ref[...] = v

    if cts is None:
        out = [lam_re, lam_im, bt_re, bt_im]
        ins = args
    else:
        out = list(args)
        ins = args + tuple(cts)
    return pl.pallas_call(
        body, name=name, out_shape=[jax.ShapeDtypeStruct(a.shape, F32) for a in out],
        compiler_params=pltpu.CompilerParams(vmem_limit_bytes=VMEM_LIMIT),
    )(*ins)


SCAN_COLS = 256


def _cmul(xr, xi, yr, yi):
    return xr * yr - xi * yi, xr * yi + xi * yr


def _scan_consts(a_re, a_im, cols, reverse):
    shape = (SUBLANES, cols)
    row = lax.broadcasted_iota(jnp.int32, shape, 0)
    dist = (SUBLANES - 1 - row) if reverse else row
    mr, mi = jnp.broadcast_to(a_re, shape), jnp.broadcast_to(a_im, shape)
    pr, pi = mr, mi
    mults = []
    for d in (1, 2, 4):
        mults.append((mr, mi))
        qr, qi = _cmul(pr, pi, mr, mi)
        has_bit = (dist & d) != 0
        pr, pi = jnp.where(has_bit, qr, pr), jnp.where(has_bit, qi, pi)
        mr, mi = _cmul(mr, mi, mr, mi)
    return mults, (pr, pi), dist


def _scan_group(xr, xi, consts, cr, ci, reverse):
    mults, (pr, pi), dist = consts
    for d, (mr, mi) in zip((1, 2, 4), mults):
        shift = (SUBLANES - d) if reverse else d
        sr = jnp.where(dist >= d, pltpu.roll(xr, shift, 0), 0.0)
        si = jnp.where(dist >= d, pltpu.roll(xi, shift, 0), 0.0)
        tr, ti = _cmul(mr, mi, sr, si)
        xr, xi = xr + tr, xi + ti
    last = slice(0, 1) if reverse else slice(SUBLANES - 1, SUBLANES)
    nr, ni = _cmul(pr[last], pi[last], cr, ci)
    tr, ti = _cmul(pr, pi, jnp.broadcast_to(cr, xr.shape), jnp.broadcast_to(ci, xr.shape))
    return xr + tr, xi + ti, xr[last] + nr, xi[last] + ni


def _scan_specs(tb, time_of):
    row = pl.BlockSpec((tb, SCAN_COLS), lambda j, i: (time_of(i), j))
    par = pl.BlockSpec((1, SCAN_COLS), lambda j, i: (0, j))
    return row, par


def _s5_scan_fwd(bu_re, bu_im, lb_re, lb_im, *, name, tb=512):
    t, c = bu_re.shape
    nj = c // SCAN_COLS
    row, par = _scan_specs(tb, lambda i: i)

    def body(bre_ref, bim_ref, lre_ref, lim_ref, sre_ref, sim_ref, cre_ref, cim_ref):
        @pl.when(pl.program_id(1) == 0)
        def _():
            cre_ref[...] = jnp.zeros_like(cre_ref)
            cim_ref[...] = jnp.zeros_like(cim_ref)
        consts = _scan_consts(lre_ref[...], lim_ref[...], SCAN_COLS, False)

        def group(k, carry):
            rows = pl.ds(pl.multiple_of(k * SUBLANES, SUBLANES), SUBLANES)
            sr, si, cr, ci = _scan_group(bre_ref[rows, :], bim_ref[rows, :], consts, *carry, False)
            sre_ref[rows, :] = sr
            sim_ref[rows, :] = si
            return cr, ci

        sr, si = lax.fori_loop(0, tb // SUBLANES, group, (cre_ref[...], cim_ref[...]), unroll=4)
        cre_ref[...] = sr
        cim_ref[...] = si

    return pl.pallas_call(
        body, name=name, grid=(nj, t // tb), in_specs=[row, row, par, par], out_specs=[row, row],
        out_shape=[jax.ShapeDtypeStruct((t, c), F32)] * 2,
        scratch_shapes=[pltpu.VMEM((1, SCAN_COLS), F32)] * 2,
        compiler_params=_cp("parallel", "arbitrary"),
    )(bu_re, bu_im, lb_re, lb_im)


def _s5_scan_bwd(s_re, s_im, ds_re, ds_im, lb_re, lb_im, *, name, tb=512):
    t, c = s_re.shape
    nj = c // SCAN_COLS
    nt = t // tb
    rev = lambda i: nt - 1 - i
    row, par = _scan_specs(tb, rev)
    prev = pl.BlockSpec((HALO, SCAN_COLS), lambda j, i: (jnp.maximum(rev(i) * (tb // HALO) - 1, 0), j))

    def body(sre_ref, sim_ref, pre_ref, pim_ref, dre_ref, dim_ref, lre_ref, lim_ref,
             gre_ref, gim_ref, dlre_ref, dlim_ref, cre_ref, cim_ref, ext_re, ext_im):
        step_id = pl.program_id(1)

        @pl.when(step_id == 0)
        def _():
            cre_ref[...] = jnp.zeros_like(cre_ref)
            cim_ref[...] = jnp.zeros_like(cim_ref)
            dlre_ref[...] = jnp.zeros_like(dlre_ref)
            dlim_ref[...] = jnp.zeros_like(dlim_ref)
        consts = _scan_consts(lre_ref[...], -lim_ref[...], SCAN_COLS, True)
        ngroups = tb // SUBLANES

        def group(k, carry):
            rows = pl.ds(pl.multiple_of((ngroups - 1 - k) * SUBLANES, SUBLANES), SUBLANES)
            gr, gi, cr, ci = _scan_group(dre_ref[rows, :], dim_ref[rows, :], consts, *carry, True)
            gre_ref[rows, :] = gr
            gim_ref[rows, :] = gi
            return cr, ci

        gr, gi = lax.fori_loop(0, ngroups, group, (cre_ref[...], cim_ref[...]), unroll=4)
        cre_ref[...] = gr
        cim_ref[...] = gi
        has_past = step_id != nt - 1
        ext_re[pl.ds(0, HALO), :] = jnp.where(has_past, pre_ref[...], 0.0)
        ext_im[pl.ds(0, HALO), :] = jnp.where(has_past, pim_ref[...], 0.0)
        ext_re[pl.ds(HALO, tb), :] = sre_ref[...]
        ext_im[pl.ds(HALO, tb), :] = sim_ref[...]
        pr, pi = ext_re[pl.ds(HALO - 1, tb), :], ext_im[pl.ds(HALO - 1, tb), :]
        g_re, g_im = gre_ref[...], gim_ref[...]
        dlre_ref[...] += jnp.sum(pr * g_re + pi * g_im, axis=0, keepdims=True)
        dlim_ref[...] += jnp.sum(pr * g_im - pi * g_re, axis=0, keepdims=True)

    return pl.pallas_call(
        body, name=name, grid=(nj, nt),
        in_specs=[row, row, prev, prev, row, row, par, par], out_specs=[row, row, par, par],
        out_shape=[jax.ShapeDtypeStruct((t, c), F32)] * 2 + [jax.ShapeDtypeStruct((1, c), F32)] * 2,
        scratch_shapes=[pltpu.VMEM((1, SCAN_COLS), F32)] * 2 + [pltpu.VMEM((HALO + tb, SCAN_COLS), F32)] * 2,
        compiler_params=_cp("parallel", "arbitrary"),
    )(s_re, s_im, s_re, s_im, ds_re, ds_im, lb_re, lb_im)


def _loss_fn(h, w, tgt):
    err = _rms_fn(h, w)[0] - tgt
    return 0.5 * jnp.sum(jnp.mean(err * err, axis=-1, keepdims=True), axis=0, keepdims=True)


def _loss_head(h, w, tgt, *, name, tb=256):
    t, d = h.shape

    def body(h_ref, w_ref, t_ref, loss_ref, dh_ref, dhb_ref, dw_ref):
        @pl.when(pl.program_id(0) == 0)
        def _():
            loss_ref[...] = jnp.zeros_like(loss_ref)
            dw_ref[...] = jnp.zeros_like(dw_ref)
        part, vjp = jax.vjp(_loss_fn, h_ref[...], w_ref[...], t_ref[...])
        dh, dw, _ = vjp(jnp.ones((1, 1), F32))
        loss_ref[...] += jnp.broadcast_to(part, loss_ref.shape)
        dh_ref[...] = dh
        dhb_ref[...] = dh.astype(BF16)
        dw_ref[...] += dw

    row = pl.BlockSpec((tb, d), lambda i: (i, 0))
    par = pl.BlockSpec((1, d), lambda i: (0, 0))
    return pl.pallas_call(
        body, name=name, grid=(t // tb,), in_specs=[row, par, row],
        out_specs=[pl.BlockSpec((SUBLANES, LANES), lambda i: (0, 0)), row, row, par],
        out_shape=[jax.ShapeDtypeStruct((SUBLANES, LANES), F32), jax.ShapeDtypeStruct((t, d), F32),
                   jax.ShapeDtypeStruct((t, d), BF16), jax.ShapeDtypeStruct((1, d), F32)],
        compiler_params=_cp("arbitrary"),
    )(h, w, tgt)


def _adamw(w, g, m, v, *, name):
    r, c = w.shape
    tr = _tile(r, 256, SUBLANES)

    def body(w_ref, g_ref, m_ref, v_ref, d_ref, nm_ref, nv_ref):
        g = g_ref[...]
        nm = ADAM_B1 * m_ref[...] + (1.0 - ADAM_B1) * g
        nv = ADAM_B2 * v_ref[...] + (1.0 - ADAM_B2) * (g * g)
        m_hat = nm / (1.0 - ADAM_B1 ** ADAM_STEP)
        v_hat = nv / (1.0 - ADAM_B2 ** ADAM_STEP)
        d_ref[...] = -ADAM_LR * (m_hat / (jnp.sqrt(v_hat) + ADAM_EPS) + ADAM_WD * w_ref[...])
        nm_ref[...] = nm
        nv_ref[...] = nv

    spec = pl.BlockSpec((tr, c), lambda i: (i, 0))
    return pl.pallas_call(
        body, name=name, grid=(r // tr,), in_specs=[spec] * 4, out_specs=[spec] * 3,
        out_shape=[jax.ShapeDtypeStruct((r, c), F32)] * 3, compiler_params=_cp("parallel"),
    )(w, g, m, v)


def _sum_parts(parts, *, name):
    _, r, c = parts.shape
    tr = _tile(r, 128, SUBLANES)

    def body(p_ref, o_ref):
        acc = p_ref[0].astype(F32)
        for k in range(1, N_DEV):
            acc = acc + p_ref[k].astype(F32)
        o_ref[...] = acc

    return pl.pallas_call(
        body, name=name, grid=(r // tr,), in_specs=[pl.BlockSpec((N_DEV, tr, c), lambda i: (0, i, 0))],
        out_specs=pl.BlockSpec((tr, c), lambda i: (i, 0)), out_shape=jax.ShapeDtypeStruct((r, c), F32),
        compiler_params=_cp("parallel"),
    )(parts)


def _position():
    return lax.axis_index("x"), lax.axis_index("y"), lax.axis_index("c")


def _flat(px, py, pc):
    return 4 * px + 2 * py + pc


def _all_gather(shard, *, name):
    def body(x_ref, out_ref, send_sems, recv_sems, local_sem):
        x, y, c = _position()
        me, sibling = (x, y, c), (x, y, 1 - c)
        chips = [(1 - x, y), (x, 1 - y), (1 - x, 1 - y)]

        def copy(k, block, to, src=None):
            slot = out_ref.at[_flat(*block)]
            return pltpu.make_async_remote_copy(
                src_ref=slot if src is None else src, dst_ref=slot, send_sem=send_sems.at[k],
                recv_sem=recv_sems.at[k], device_id=to, device_id_type=MESH)

        mine = pltpu.make_async_copy(x_ref, out_ref.at[_flat(*me)], local_sem)
        mine.start()
        first = [copy(0, me, sibling, src=x_ref)]
        first += [copy(1 + j, me, (*chip, c), src=x_ref) for j, chip in enumerate(chips)]
        for cp in first:
            cp.start()
        passed = [copy(4 + j, (*chip, c), sibling) for j, chip in enumerate(chips)]
        for j, chip in enumerate(chips):
            copy(1 + j, (*chip, c), me).wait_recv()
            passed[j].start()
        copy(0, sibling, me).wait_recv()
        for j, chip in enumerate(chips):
            copy(4 + j, (*chip, 1 - c), me).wait_recv()
        for cp in first + passed:
            cp.wait_send()
        mine.wait()

    return pl.pallas_call(
        body, name=name, out_shape=jax.ShapeDtypeStruct((N_DEV,) + shard.shape, shard.dtype),
        in_specs=[pl.BlockSpec(memory_space=pl.ANY)], out_specs=pl.BlockSpec(memory_space=pl.ANY),
        scratch_shapes=[pltpu.SemaphoreType.DMA((7,)), pltpu.SemaphoreType.DMA((7,)), pltpu.SemaphoreType.DMA(())],
    )(shard)


_HBM = pl.BlockSpec(memory_space=pltpu.HBM)
_SEM = pl.BlockSpec(memory_space=pltpu.SEMAPHORE)
_EFFECT = pltpu.SideEffectType.DATAFLOW_SIDE_EFFECTING


def _copy_ends(src_ref, land_ref, mode, me, to):
    if mode == "gather_slot":
        return src_ref, land_ref.at[me]
    if mode == "gather_cols":
        w = src_ref.shape[1]
        return src_ref, land_ref.at[:, pl.ds(pl.multiple_of(me * w, LANES), w)]
    if mode == "scatter_slot":
        return src_ref.at[to], land_ref.at[me]
    w = land_ref.shape[2]
    return src_ref.at[:, pl.ds(pl.multiple_of(to * w, LANES), w)], land_ref.at[me]


BF16_ROWS = 16


def _land_shape(src, mode):
    if mode == "gather_slot":
        return (N_DEV,) + src.shape
    if mode == "gather_cols":
        return (src.shape[0], N_DEV * src.shape[1])
    if mode == "scatter_slot":
        return src.shape
    return (N_DEV, src.shape[0], src.shape[1] // N_DEV)


def _exchange_copies(src_ref, land_ref, send_sems, recv_sems, mode):
    x, y, c = _position()
    me = _flat(x, y, c)
    copies = []
    for k in range(1, N_DEV):
        peer = (x ^ ((k >> 2) & 1), y ^ ((k >> 1) & 1), c ^ (k & 1))
        src, dst = _copy_ends(src_ref, land_ref, mode, me, _flat(*peer))
        copies.append(pltpu.make_async_remote_copy(
            src_ref=src, dst_ref=dst, send_sem=send_sems.at[k - 1], recv_sem=recv_sems.at[k - 1],
            device_id=peer, device_id_type=MESH))
    return copies


def _place_own(src, mode, dev, *, name):
    rows = src.shape[1] if mode == "scatter_slot" else src.shape[0]
    tr = _tile(rows, 512, BF16_ROWS)
    land = _land_shape(src, mode)
    width = land[-1] if mode.startswith("scatter") else src.shape[1]
    slot = pl.BlockSpec((1, tr, width), lambda i, d: (d[0], i, 0))
    cols = pl.BlockSpec((tr, width), lambda i, d: (i, d[0]))
    whole = pl.BlockSpec((tr, width), lambda i, d: (i, 0))
    in_spec, out_spec = {"gather_slot": (whole, slot), "gather_cols": (whole, cols), "scatter_slot": (slot, slot),
                         "scatter_cols": (cols, slot)}[mode]

    def body(dev_ref, src_ref, land_ref):
        land_ref[...] = src_ref[...].reshape(land_ref.shape)

    return pl.pallas_call(
        body, name=name, out_shape=jax.ShapeDtypeStruct(land, src.dtype),
        grid_spec=pltpu.PrefetchScalarGridSpec(num_scalar_prefetch=1, grid=(rows // tr,), in_specs=[in_spec],
                                               out_specs=out_spec),
        compiler_params=_cp("parallel"),
    )(dev, src)


def _exchange_start(src, mode, dev, *, name):
    land = _place_own(src, mode, dev, name=name + "_own")
    n_copies = N_DEV - 1

    def body(src_ref, land_ref, send_sems, recv_sems, src_thru, land_thru, token):
        for cp in _exchange_copies(src_ref, land_ref, send_sems, recv_sems, mode):
            cp.start()
        token[...] = jnp.zeros_like(token)

    hbm = pltpu.with_memory_space_constraint
    *handle, token = pl.pallas_call(
        body, name=name,
        out_shape=(pltpu.SemaphoreType.DMA((n_copies,)), pltpu.SemaphoreType.DMA((n_copies,)),
                   pltpu.HBM(src.shape, src.dtype), pltpu.HBM(land.shape, land.dtype),
                   jax.ShapeDtypeStruct((SUBLANES, LANES), F32)),
        in_specs=(_HBM, _HBM), out_specs=(_SEM, _SEM, _HBM, _HBM, pl.BlockSpec(memory_space=pltpu.VMEM)),
        input_output_aliases={0: 2, 1: 3}, compiler_params=pltpu.CompilerParams(has_side_effects=_EFFECT),
    )(hbm(src, pltpu.HBM), hbm(land, pltpu.HBM))
    return (tuple(handle), mode), token


def _exchange_wait(pending, after, *, name):
    (send_sems, recv_sems, src_thru, land_thru), mode = pending

    def body(src_ref, land_ref, send_sems, recv_sems, after_ref, src_dead, got_ref):
        for cp in _exchange_copies(src_ref, land_ref, send_sems, recv_sems, mode):
            cp.wait_send()
            cp.wait_recv()

    return pl.pallas_call(
        body, name=name, out_shape=(pltpu.HBM(src_thru.shape, src_thru.dtype), pltpu.HBM(land_thru.shape, land_thru.dtype)),
        in_specs=(_HBM, _HBM, _SEM, _SEM, pl.BlockSpec(memory_space=pl.ANY)), out_specs=(_HBM, _HBM),
        input_output_aliases={0: 0, 1: 1}, compiler_params=pltpu.CompilerParams(has_side_effects=_EFFECT),
    )(src_thru, land_thru, send_sems, recv_sems, after)[1]


def _after(x, token):
    return x + token[0, 0].astype(x.dtype)


def _pad_cols(a, mult):
    pad = -a.shape[1] % mult
    return jnp.pad(a, ((0, 0), (0, pad))) if pad else a


def _pack(arrs, cols):
    flat = jnp.concatenate([a.reshape(-1).astype(F32) for a in arrs])
    sizes = [int(a.size) for a in arrs]
    flat = jnp.pad(flat, (0, -flat.shape[0] % (SUBLANES * cols)))
    return flat.reshape(-1, cols), sizes


def _unpack(flat2d, sizes, shapes):
    flat = flat2d.reshape(-1)
    out, o = [], 0
    for n, s in zip(sizes, shapes):
        out.append(flat[o:o + n].reshape(s))
        o += n
    return out


PACK_COLS = SUBLANES * LANES


def kernel(x, norm_mix_w, w_in, conv_a_w, conv_a_b, dt_bias, a_log, d_a, norm_a_w, w_proj_a, s5_lam_re, s5_lam_im, s5_log_dt, s5_b_re, s5_b_im, s5_c_re, s5_c_im, s5_d, w_s5_glu, w_out, norm_ffn_w, w_up, conv_ffn_w, conv_ffn_b, w_down, norm_final_w, loss_target, m_norm_mix_w, m_w_in, m_conv_a_w, m_conv_a_b, m_dt_bias, m_a_log, m_d_a, m_norm_a_w, m_w_proj_a, m_s5_lam_re, m_s5_lam_im, m_s5_log_dt, m_s5_b_re, m_s5_b_im, m_s5_c_re, m_s5_c_im, m_s5_d, m_w_s5_glu, m_w_out, m_norm_ffn_w, m_w_up, m_conv_ffn_w, m_conv_ffn_b, m_w_down, m_norm_final_w, v_norm_mix_w, v_w_in, v_conv_a_w, v_conv_a_b, v_dt_bias, v_a_log, v_d_a, v_norm_a_w, v_w_proj_a, v_s5_lam_re, v_s5_lam_im, v_s5_log_dt, v_s5_b_re, v_s5_b_im, v_s5_c_re, v_s5_c_im, v_s5_d, v_w_s5_glu, v_w_out, v_norm_ffn_w, v_w_up, v_conv_ffn_w, v_conv_ffn_b, v_w_down, v_norm_final_w):
    weights = dict(norm_mix_w=norm_mix_w, w_in=w_in, conv_a_w=conv_a_w, conv_a_b=conv_a_b, dt_bias=dt_bias, a_log=a_log, d_a=d_a, norm_a_w=norm_a_w, w_proj_a=w_proj_a, s5_lam_re=s5_lam_re, s5_lam_im=s5_lam_im, s5_log_dt=s5_log_dt, s5_b_re=s5_b_re, s5_b_im=s5_b_im, s5_c_re=s5_c_re, s5_c_im=s5_c_im, s5_d=s5_d, w_s5_glu=w_s5_glu, w_out=w_out, norm_ffn_w=norm_ffn_w, w_up=w_up, conv_ffn_w=conv_ffn_w, conv_ffn_b=conv_ffn_b, w_down=w_down, norm_final_w=norm_final_w)
    moms = dict(norm_mix_w=m_norm_mix_w, w_in=m_w_in, conv_a_w=m_conv_a_w, conv_a_b=m_conv_a_b, dt_bias=m_dt_bias, a_log=m_a_log, d_a=m_d_a, norm_a_w=m_norm_a_w, w_proj_a=m_w_proj_a, s5_lam_re=m_s5_lam_re, s5_lam_im=m_s5_lam_im, s5_log_dt=m_s5_log_dt, s5_b_re=m_s5_b_re, s5_b_im=m_s5_b_im, s5_c_re=m_s5_c_re, s5_c_im=m_s5_c_im, s5_d=m_s5_d, w_s5_glu=m_w_s5_glu, w_out=m_w_out, norm_ffn_w=m_norm_ffn_w, w_up=m_w_up, conv_ffn_w=m_conv_ffn_w, conv_ffn_b=m_conv_ffn_b, w_down=m_w_down, norm_final_w=m_norm_final_w)
    vars_ = dict(norm_mix_w=v_norm_mix_w, w_in=v_w_in, conv_a_w=v_conv_a_w, conv_a_b=v_conv_a_b, dt_bias=v_dt_bias, a_log=v_a_log, d_a=v_d_a, norm_a_w=v_norm_a_w, w_proj_a=v_w_proj_a, s5_lam_re=v_s5_lam_re, s5_lam_im=v_s5_lam_im, s5_log_dt=v_s5_log_dt, s5_b_re=v_s5_b_re, s5_b_im=v_s5_b_im, s5_c_re=v_s5_c_re, s5_c_im=v_s5_c_im, s5_d=v_s5_d, w_s5_glu=v_w_s5_glu, w_out=v_w_out, norm_ffn_w=v_norm_ffn_w, w_up=v_w_up, conv_ffn_w=v_conv_ffn_w, conv_ffn_b=v_conv_ffn_b, w_down=v_w_down, norm_final_w=v_norm_final_w)
    names = list(weights)
    col_sharded = ("w_in", "w_s5_glu", "w_up")
    row_sharded = ("w_proj_a", "w_out", "w_down")
    conv_sharded = ("conv_a_w", "conv_ffn_w")
    replicated = [n for n in names if n not in col_sharded + row_sharded + conv_sharded]

    x2, tgt = x[0], loss_target[0]
    t, d = x2.shape
    nh = dt_bias.shape[-1]
    d_inner = norm_a_w.shape[-1]
    conv_dim = conv_a_b.shape[-1]
    gn = (conv_dim - d_inner) // 2
    ng = gn // D_STATE
    nr = nh // ng
    rp = d_inner // ng
    d_s5 = s5_d.shape[-1]
    gs, ps = s5_lam_re.shape[1:]
    cs = d_s5 // gs
    n_oct = gs // 8
    assert (gs * ps) % SCAN_COLS == 0 and 8 * cs == LANES and gs % 8 == 0
    d_ff = w_down.shape[1] * N_DEV
    dev = _flat(*_position())
    dev1 = dev.reshape(1).astype(jnp.int32)

    ka, kf = conv_a_w.shape[1], conv_ffn_w.shape[1]
    taps = jnp.concatenate([conv_a_w[0].reshape(1, -1), conv_ffn_w[0].reshape(1, -1)], axis=1)
    taps = _all_gather(taps, name="ag_conv_taps")[:, 0]
    def by_cols(n):
        return n in ("w_s5_glu", "w_up") and weights[n].shape[2] % LANES == 0

    pending, started = {}, taps[:1, :1] * 0.0
    for n in ("w_in", "w_proj_a", "w_s5_glu", "w_out", "w_up", "w_down"):
        pending[n], token = _exchange_start(weights[n][0].astype(BF16), "gather_cols" if by_cols(n) else "gather_slot",
                                            dev1, name="ag_" + n)
        started = started + token[:1, :1]

    def gathered(n, after):
        g = _exchange_wait(pending[n], after, name="agw_" + n)
        if by_cols(n) or n == "w_in":
            return g
        if n in row_sharded:
            return g.reshape(-1, g.shape[2])
        return jnp.transpose(g, (1, 0, 2)).reshape(g.shape[1], -1)

    seg_sizes = dict(z=d_inner, xs=d_inner, bm=gn, cm=gn, dt=nh, u=d_s5, ga=d, gb=d)
    seg_names = tuple(seg_sizes)
    pieces, seg_at = _w_in_pieces(seg_sizes, ("z", "xs", "ga", "gb", "bm", "cm", "u", "dt"), w_in.shape[2])
    na = ka * conv_a_w.shape[2]
    cw_a = jnp.transpose(taps[:, :na].reshape(N_DEV, ka, -1), (1, 0, 2)).reshape(ka, conv_dim)
    cw_f = jnp.transpose(taps[:, na:].reshape(N_DEV, kf, -1), (1, 0, 2)).reshape(kf, 2 * d_ff)
    cb_a, cb_f = conv_a_b, conv_ffn_b
    a_cols = {"xs": slice(0, d_inner), "bm": slice(d_inner, d_inner + gn), "cm": slice(d_inner + gn, conv_dim)}

    w1 = norm_mix_w.reshape(1, 1, d) + started[0, 0]
    hn1, = _blocked_fwd(_rms_fn, [x2], [w1], [(d, BF16)], tb=256, name="rms1")
    w_in_p = _w_in_pack(gathered("w_in", hn1), pieces, seg_at, seg_sizes, name="w_in_pack")
    pre = {sn: _mm(hn1, w_in_p, b_win=seg_at[sn], name="in_" + sn) for sn in seg_names}
    act_a = {sn: _conv_fwd(_comb_silu, [pre[sn]], [cw_a[:, a_cols[sn]]], [cb_a[:, a_cols[sn]]], out_dtype=F32,
                           name="conv_a_" + sn) for sn in a_cols}
    dtr3 = jnp.transpose(pre["dt"][:, :nh].reshape(t, ng, nr), (1, 0, 2))
    dtb3, alog3, dsk3 = (p.reshape(ng, 1, nr) for p in (dt_bias, a_log, d_a))
    nw3 = norm_a_w.reshape(ng, 1, rp)
    yn, hsave = _ssd_fwd(act_a["xs"], act_a["bm"], act_a["cm"], pre["z"], dtr3, dtb3, alog3, dsk3, nw3, name="ssd")
    w_proj = gathered("w_proj_a", yn)
    y_a = _mm(yn, w_proj, name="proj_a")

    lam_re3, lam_im3 = s5_lam_re[0][:, None, :], s5_lam_im[0][:, None, :]
    logdt3 = s5_log_dt[0][:, None, None]
    bt_re, bt_im = jnp.transpose(s5_b_re[0], (0, 2, 1)), jnp.transpose(s5_b_im[0], (0, 2, 1))
    lb_re3, lb_im3, bb_re, bb_im = _s5_params(lam_re3, lam_im3, logdt3, bt_re, bt_im, name="s5_params")
    eye = jnp.eye(8, dtype=F32)

    def diag_b(bt):
        return (bt.reshape(n_oct, 8, cs, 1, ps) * eye[None, :, None, :, None]).reshape(n_oct, 8 * cs, 8 * ps)

    def undiag_b(blk):
        return (blk.reshape(n_oct, 8, cs, 8, ps) * eye[None, :, None, :, None]).sum(axis=3).reshape(gs, cs, ps)

    def diag_c(cm):
        ct = jnp.transpose(cm.reshape(n_oct, 8, cs, ps), (0, 1, 3, 2))
        return (ct[:, :, :, None, :] * eye[None, :, None, :, None]).reshape(n_oct, 8 * ps, 8 * cs)

    def undiag_c(blk):
        ct = (blk.reshape(n_oct, 8, ps, 8, cs) * eye[None, :, None, :, None]).sum(axis=3)
        return jnp.transpose(ct, (0, 1, 3, 2)).reshape(gs, cs, ps)

    b_blk_re, b_blk_im = diag_b(bb_re), diag_b(bb_im)
    c_blk_re, c_blk_imn = diag_c(s5_c_re[0]), diag_c(-s5_c_im[0])
    d3 = s5_d.reshape(n_oct, 1, LANES)
    lb_re, lb_im = lb_re3.reshape(1, gs * ps), lb_im3.reshape(1, gs * ps)
    u = pre["u"]
    bu_re, bu_im = _blocked_fwd(_s5_bu_fn, [u], [b_blk_re, b_blk_im], [(gs * ps, F32)] * 2, nj=n_oct, tb=512,
                                name="s5_bu")
    s_re, s_im = _s5_scan_fwd(bu_re, bu_im, lb_re, lb_im, name="s5_scan")
    yb, = _blocked_fwd(_s5_out_fn, [s_re, s_im, u], [c_blk_re, c_blk_imn, d3], [(d_s5, BF16)], nj=n_oct, tb=512,
                       name="s5_out")
    w_glu = gathered("w_s5_glu", yb)
    glu_v = _mm(yb, w_glu, b_win=(0, d), name="glu_v")
    glu_g = _mm(yb, w_glu, b_win=(d, d), name="glu_g")
    merged, = _blocked_fwd(_merge_fn, [glu_v, glu_g, pre["ga"], pre["gb"], y_a], [], [(d, BF16)], tb=256,
                           name="merge")
    w_o = gathered("w_out", merged)
    h1 = _mm(merged, w_o, acc=x2, name="out_proj")
    w2 = norm_ffn_w.reshape(1, 1, d)
    hn2, = _blocked_fwd(_rms_fn, [h1], [w2], [(d, BF16)], tb=256, name="rms2")
    w_u = gathered("w_up", hn2)
    up_g = _mm(hn2, w_u, b_win=(0, d_ff), name="up_g")
    up_v = _mm(hn2, w_u, b_win=(d_ff, d_ff), name="up_v")
    f_w = [cw_f[:, :d_ff], cw_f[:, d_ff:]]
    f_b = [cb_f[:, :d_ff], cb_f[:, d_ff:]]
    act = _conv_fwd(_comb_glu, [up_g, up_v], f_w, f_b, out_dtype=BF16, name="conv_ffn")
    w_dn = gathered("w_down", act)
    h2 = _mm(act, w_dn, acc=h1, name="down")
    loss_tile, dh2, dh2_b, g_final = _loss_head(h2, norm_final_w.reshape(1, d), tgt, name="loss_head")

    grads, scattering = {}, {}

    def scatter_start(n, g):
        if by_cols(n):
            src, mode = g, "scatter_cols"
        elif n in row_sharded:
            src, mode = g.reshape(N_DEV, -1, g.shape[1]), "scatter_slot"
        elif n == "w_in":
            src, mode = g, "scatter_slot"
        else:
            src, mode = jnp.transpose(g.reshape(g.shape[0], N_DEV, -1), (1, 0, 2)), "scatter_slot"
        scattering[n], token = _exchange_start(src, mode, dev1, name="rs_" + n)
        return token

    d_act = _mm(dh2_b, w_dn, tb=True, name="d_act")
    g_down = _mm(act, dh2_b, ta=True, out_dtype=BF16, name="g_w_down")
    tok = scatter_start("w_down", g_down)
    (dup_g, dwf_g, dbf_g), (dup_v, dwf_v, dbf_v) = _conv_bwd(
        _comb_glu, [up_g, up_v], f_w, [_after(f_b[0], tok), f_b[1]], d_act, dx_dtype=BF16, name="conv_ffn_bwd")
    dhn2 = _mm(dup_g, w_u, tb=True, b_win=(0, d_ff), name="d_hn2_g")
    dhn2 = _mm(dup_v, w_u, tb=True, b_win=(d_ff, d_ff), acc=dhn2, name="d_hn2_v")
    g_up = _mm(hn2, dup_g, ta=True, into=(lax.empty((d, 2 * d_ff), BF16), 0), name="g_w_up_g")
    g_up = _mm(hn2, dup_v, ta=True, into=(g_up, d_ff), name="g_w_up_v")
    tok = scatter_start("w_up", g_up)
    dh1, dh1_b, g_w2 = _blocked_bwd(_rms_fn, [h1], [_after(w2, tok)], [dhn2], [(F32, BF16)], adds={0: dh2}, tb=256,
                                    name="rms2_bwd")
    d_merged = _mm(dh1_b, w_o, tb=True, name="d_merged")
    g_out = _mm(merged, dh1_b, ta=True, out_dtype=BF16, name="g_w_out")
    tok = scatter_start("w_out", g_out)
    dglu_v, dglu_g, dga, dgb, dy_a = _blocked_bwd(
        _merge_fn, [glu_v, glu_g, pre["ga"], pre["gb"], y_a], [], [d_merged], [BF16] * 5, tb=128, name="merge_bwd")
    dyb = _mm(dglu_v, w_glu, tb=True, b_win=(0, d), name="d_yb_v")
    dyb = _mm(dglu_g, w_glu, tb=True, b_win=(d, d), acc=dyb, name="d_yb_g")
    g_glu = _mm(yb, dglu_v, ta=True, into=(lax.empty((d_s5, 2 * d), BF16), 0), name="g_w_glu_v")
    g_glu = _mm(yb, dglu_g, ta=True, into=(g_glu, d), name="g_w_glu_g")
    tok = tok + scatter_start("w_s5_glu", g_glu)
    ds_re, ds_im, du_skip, dc_blk_re, dc_blk_imn, dd3 = _blocked_bwd(
        _s5_out_fn, [s_re, s_im, u], [c_blk_re, c_blk_imn, _after(d3, tok)], [dyb], [F32, F32, F32], nj=n_oct, tb=512,
        name="s5_out_bwd")
    dbu_re, dbu_im, dlb_re, dlb_im = _s5_scan_bwd(s_re, s_im, ds_re, ds_im, lb_re, lb_im, name="s5_scan_bwd")
    du, db_blk_re, db_blk_im = _blocked_bwd(
        _s5_bu_fn, [u], [b_blk_re, b_blk_im], [dbu_re, dbu_im], [BF16], adds={0: du_skip}, nj=n_oct, tb=512,
        name="s5_bu_bwd")
    g_lre, g_lim, g_ldt, g_bt_re, g_bt_im = _s5_params(
        lam_re3, lam_im3, logdt3, bt_re, bt_im,
        cts=(dlb_re.reshape(gs, 1, ps), dlb_im.reshape(gs, 1, ps), undiag_b(db_blk_re), undiag_b(db_blk_im)),
        name="s5_params_bwd")
    grads["s5_lam_re"], grads["s5_lam_im"] = g_lre.reshape(s5_lam_re.shape), g_lim.reshape(s5_lam_im.shape)
    grads["s5_log_dt"] = g_ldt.reshape(s5_log_dt.shape)
    grads["s5_b_re"] = jnp.transpose(g_bt_re, (0, 2, 1)).reshape(s5_b_re.shape)
    grads["s5_b_im"] = jnp.transpose(g_bt_im, (0, 2, 1)).reshape(s5_b_im.shape)
    grads["s5_c_re"] = undiag_c(dc_blk_re).reshape(s5_c_re.shape)
    grads["s5_c_im"] = -undiag_c(dc_blk_imn).reshape(s5_c_im.shape)
    grads["s5_d"] = dd3.reshape(s5_d.shape)

    dyn = _mm(dy_a, w_proj, tb=True, name="d_yn")
    g_proj = _mm(yn, dy_a, ta=True, out_dtype=BF16, name="g_w_proj_a")
    tok = scatter_start("w_proj_a", g_proj)
    dxs, dbm, dcm, dz, ddtr3, g_dtb, g_alog, g_dsk, g_nw = _ssd_bwd(
        act_a["xs"], act_a["bm"], act_a["cm"], pre["z"], dtr3, hsave, dtb3, alog3, dsk3, _after(nw3, tok), dyn,
        name="ssd_bwd")
    grads["dt_bias"], grads["a_log"], grads["d_a"] = (g.reshape(1, nh) for g in (g_dtb, g_alog, g_dsk))
    grads["norm_a_w"] = g_nw.reshape(1, d_inner)
    dpre = {"z": dz, "u": du, "ga": dga, "gb": dgb}
    dcw, dcb = {}, {}
    for sn, dact in (("xs", dxs), ("bm", dbm), ("cm", dcm)):
        (dpre[sn], dcw[sn], dcb[sn]), = _conv_bwd(
            _comb_silu, [pre[sn]], [cw_a[:, a_cols[sn]]], [cb_a[:, a_cols[sn]]], dact, dx_dtype=BF16,
            name="conv_a_bwd_" + sn)
    dpre["dt"] = _pad_cols(jnp.transpose(ddtr3, (1, 0, 2)).reshape(t, nh), LANES).astype(BF16)
    g_in = _w_in_unpack({sn: _mm(hn1, dpre[sn], ta=True, name="g_w_in_" + sn) for sn in seg_names}, pieces,
                        w_in.shape[2], name="w_in_unpack")
    tok = scatter_start("w_in", g_in)
    dhn1 = _mm(_after(dpre["dt"], tok), w_in_p, tb=True, b_win=seg_at["dt"], name="d_hn1_dt")
    for sn in seg_names:
        if sn != "dt":
            dhn1 = _mm(dpre[sn], w_in_p, tb=True, b_win=seg_at[sn], acc=dhn1, name="d_hn1_" + sn)
    dx, g_w1 = _blocked_bwd(_rms_fn, [x2], [w1], [dhn1], [F32], adds={0: dh1}, tb=256, name="rms1_bwd")

    grads["norm_mix_w"], grads["norm_ffn_w"] = g_w1.reshape(1, d), g_w2.reshape(1, d)
    grads["norm_final_w"] = g_final.reshape(d)
    grads["conv_a_b"] = jnp.concatenate([dcb["xs"], dcb["bm"], dcb["cm"]], axis=1)
    grads["conv_ffn_b"] = jnp.concatenate([dbf_g, dbf_v], axis=1)
    g_cw_a = jnp.concatenate([dcw["xs"], dcw["bm"], dcw["cm"]], axis=1)
    g_cw_f = jnp.concatenate([dwf_g, dwf_v], axis=1)

    small = [grads[n] for n in replicated] + [g_cw_a, g_cw_f, loss_tile[:1, :1]]
    packed, sizes = _pack(small, PACK_COLS)
    summed = _sum_parts(_all_gather(packed, name="ag_small_grads"), name="sum_small_grads")
    *rep_sums, s_cw_a, s_cw_f, loss = _unpack(summed, sizes, [a.shape for a in small])
    for n, g in zip(replicated, rep_sums):
        grads[n] = g
    wa, wf = conv_a_w.shape[2], conv_ffn_w.shape[2]
    grads["conv_a_w"] = lax.dynamic_slice_in_dim(s_cw_a, dev * wa, wa, axis=1)[None]
    grads["conv_ffn_w"] = lax.dynamic_slice_in_dim(s_cw_f, dev * wf, wf, axis=1)[None]

    delta, new_m, new_v = {}, {}, {}
    done = dx
    for n in ("w_down", "w_up", "w_out", "w_s5_glu", "w_proj_a", "w_in"):
        land = _exchange_wait(scattering[n], done, name="rsw_" + n)
        shape = weights[n].shape
        two_d = lambda a: a.reshape(shape[-2], shape[-1])
        g = _sum_parts(land, name="rs_sum_" + n)
        grads[n] = g.reshape(shape)
        dl, nm, nv = _adamw(two_d(weights[n]), g, two_d(moms[n]), two_d(vars_[n]), name="adamw_" + n)
        delta[n], new_m[n], new_v[n] = dl.reshape(shape), nm.reshape(shape), nv.reshape(shape)
        done = dl
    small_names = replicated + list(conv_sharded)
    shapes = [weights[n].shape for n in small_names]
    pw, sizes = _pack([weights[n] for n in small_names], PACK_COLS)
    pg, _ = _pack([grads[n] for n in small_names], PACK_COLS)
    pm, _ = _pack([moms[n] for n in small_names], PACK_COLS)
    pv, _ = _pack([vars_[n] for n in small_names], PACK_COLS)
    dl, nm, nv = _adamw(pw, pg, pm, pv, name="adamw_small")
    for n, a, b, c in zip(small_names, _unpack(dl, sizes, shapes), _unpack(nm, sizes, shapes),
                          _unpack(nv, sizes, shapes)):
        delta[n], new_m[n], new_v[n] = a, b, c

    return (loss.reshape(()), dx[None], *[grads[n] for n in names], *[delta[n] for n in names],
            *[new_m[n] for n in names], *[new_v[n] for n in names])
```

```python
import functools

import jax
import jax.numpy as jnp
from jax import lax
from jax.experimental import pallas as pl
from jax.experimental.pallas import tpu as pltpu

F32 = jnp.float32
BF16 = jnp.bfloat16
HIGHEST = lax.Precision.HIGHEST
MESH = pl.DeviceIdType.MESH

EPS = 1e-6
EIG_MAX = -1e-4
D_STATE = 128
CHUNK = 256
ADAM_LR = 0.001
ADAM_B1 = 0.9
ADAM_B2 = 0.999
ADAM_EPS = 1e-08
ADAM_WD = 0.01
ADAM_STEP = 10
N_DEV = 8
LANES = 128
SUBLANES = 8
VMEM_LIMIT = 56 * 1024 * 1024
MM_MAX_K = 4096


def _cp(*sem):
    return pltpu.CompilerParams(dimension_semantics=sem, vmem_limit_bytes=VMEM_LIMIT)


def _tile(dim, pref, unit=LANES):
    if dim <= unit:
        return dim
    t = (min(pref, dim) // unit) * unit
    while dim % t:
        t -= unit
    return t


_DIMS = {"nn": (((1,), (0,)), ((), ())), "nt": (((1,), (1,)), ((), ())), "tn": (((0,), (0,)), ((), ()))}


def _dot(a, b, kind):
    return lax.dot_general(a.astype(BF16), b.astype(BF16), _DIMS[kind], preferred_element_type=F32)


@functools.partial(jax.custom_vjp, nondiff_argnums=(2,))
def _bdot(a, b, kind):
    return _dot(a, b, kind)


def _bdot_fwd(a, b, kind):
    return _dot(a, b, kind), (a, b)


def _bdot_bwd(kind, res, g):
    a, b = res
    if kind == "nn":
        return _dot(g, b, "nt"), _dot(a, g, "tn")
    if kind == "nt":
        return _dot(g, b, "nn"), _dot(g, a, "tn")
    return _dot(b, g, "nt"), _dot(a, g, "nn")


_bdot.defvjp(_bdot_fwd, _bdot_bwd)


def _mm(a, b, *, ta=False, tb=False, acc=None, out_dtype=F32, name, b_win=None, into=None):
    assert not (ta and tb)
    m, k = (a.shape[1], a.shape[0]) if ta else a.shape
    b_off, b_size = b_win or (0, b.shape[1])
    n = b.shape[0] if tb else b_size
    assert (b_size if tb else b.shape[0]) == k, (a.shape, b.shape, ta, tb, b_win)
    o_off = into[1] if into else 0
    nk = -(-k // MM_MAX_K)
    while k % nk or (k // nk) % LANES or (tb and b_off % (k // nk)):
        nk += 1
    tk = k // nk
    tm, tn = _tile(m, 1024 if tk <= 2048 else 512), _tile(n, 1024)
    while o_off % tn or (not tb and b_off % tn):
        tn = _tile(n, tn - LANES)
    kind = "tn" if ta else ("nt" if tb else "nn")
    a_spec = pl.BlockSpec((tk, tm), lambda i, j, l: (l, i)) if ta else pl.BlockSpec((tm, tk), lambda i, j, l: (i, l))
    if tb:
        b_spec = pl.BlockSpec((tn, tk), lambda i, j, l: (j, l + b_off // tk))
    else:
        b_spec = pl.BlockSpec((tk, tn), lambda i, j, l: (l, j + b_off // tn))
    c_spec = pl.BlockSpec((tm, tn), lambda i, j, l: (i, j))
    o_spec = pl.BlockSpec((tm, tn), lambda i, j, l: (i, j + o_off // tn))
    has_acc = acc is not None

    def body(*refs):
        a_ref, b_ref = refs[:2]
        c_ref = refs[2] if has_acc else None
        o_ref = refs[2 + has_acc + (into is not None)]
        if nk == 1:
            res = _dot(a_ref[...], b_ref[...], kind)
            if has_acc:
                res = res + c_ref[...].astype(F32)
            o_ref[...] = res.astype(o_ref.dtype)
            return
        acc_ref = refs[-1]
        l = pl.program_id(2)

        @pl.when(l == 0)
        def _():
            if has_acc:
                acc_ref[...] = c_ref[...].astype(F32)
            else:
                acc_ref[...] = jnp.zeros_like(acc_ref)

        acc_ref[...] += _dot(a_ref[...], b_ref[...], kind)

        @pl.when(l == nk - 1)
        def _():
            o_ref[...] = acc_ref[...].astype(o_ref.dtype)

    ins = [a, b] + ([acc] if has_acc else []) + ([into[0]] if into else [])
    in_specs = [a_spec, b_spec] + ([c_spec] if has_acc else []) + ([pl.BlockSpec(memory_space=pl.ANY)] if into else [])
    out_shape = jax.ShapeDtypeStruct(into[0].shape, into[0].dtype) if into else jax.ShapeDtypeStruct((m, n), out_dtype)
    return pl.pallas_call(
        body, name=name, grid=(m // tm, n // tn, nk), in_specs=in_specs, out_specs=o_spec, out_shape=out_shape,
        input_output_aliases={len(ins) - 1: 0} if into else {},
        scratch_shapes=[pltpu.VMEM((tm, tn), F32)] if nk > 1 else [],
        compiler_params=_cp("parallel", "parallel", "arbitrary"),
    )(*ins)


def _w_in_pieces(seg_sizes, seg_order, n_blk):
    layout, o = {}, 0
    for sn in seg_order:
        width = -(-seg_sizes[sn] // LANES) * LANES
        layout[sn] = (o, width)
        o += width
    pieces, start = [], 0
    for sn, sz in seg_sizes.items():
        lo = start
        while lo < start + sz:
            blk = lo // n_blk
            hi = min(start + sz, (blk + 1) * n_blk)
            pieces.append((blk, lo - blk * n_blk, sn, lo - start, layout[sn][0] + lo - start, hi - lo))
            lo = hi
        start += sz
    return pieces, layout


def _w_in_pack(gathered, pieces, layout, seg_sizes, *, name, tr=256):
    _, k, n_blk = gathered.shape
    n_pad = sum(w for _, w in layout.values())

    def body(g_ref, o_ref):
        for sn, (off, width) in layout.items():
            if width != seg_sizes[sn]:
                o_ref[:, pl.ds(off + seg_sizes[sn], width - seg_sizes[sn])] = jnp.zeros(
                    (tr, width - seg_sizes[sn]), o_ref.dtype)
        for blk, src, _, _, dst, width in pieces:
            o_ref[:, pl.ds(dst, width)] = g_ref[blk, :, pl.ds(src, width)]

    return pl.pallas_call(
        body, name=name, grid=(k // tr,), in_specs=[pl.BlockSpec((N_DEV, tr, n_blk), lambda i: (0, i, 0))],
        out_specs=pl.BlockSpec((tr, n_pad), lambda i: (i, 0)), out_shape=jax.ShapeDtypeStruct((k, n_pad), gathered.dtype),
        compiler_params=_cp("parallel"),
    )(gathered)


def _w_in_unpack(seg_grads, pieces, n_blk, *, name, tr=128):
    names = list(seg_grads)
    k = seg_grads[names[0]].shape[0]

    def body(*refs):
        o_ref = refs[-1]
        seg_ref = dict(zip(names, refs))
        for blk, dst, sn, src, _, width in pieces:
            o_ref[blk, :, pl.ds(dst, width)] = seg_ref[sn][:, pl.ds(src, width)].astype(o_ref.dtype)

    return pl.pallas_call(
        body, name=name, grid=(k // tr,),
        in_specs=[pl.BlockSpec((tr, seg_grads[sn].shape[1]), lambda i: (i, 0)) for sn in names],
        out_specs=pl.BlockSpec((N_DEV, tr, n_blk), lambda i: (0, i, 0)),
        out_shape=jax.ShapeDtypeStruct((N_DEV, k, n_blk), BF16), compiler_params=_cp("parallel"),
    )(*[seg_grads[sn] for sn in names])


def _row_spec(arr, tb, nj):
    return pl.BlockSpec((tb, arr.shape[1] // nj), lambda j, i: (i, j))


def _par_spec(arr):
    return pl.BlockSpec((1,) + arr.shape[1:], lambda j, i: (j, 0, 0))


def _blocked_fwd(fn, rows, params, outs, *, nj=1, tb, name):
    t = rows[0].shape[0]
    nr, npar = len(rows), len(params)

    def body(*refs):
        res = fn(*[r[...] for r in refs[:nr]], *[p[0] for p in refs[nr:nr + npar]])
        for o_ref, val in zip(refs[nr + npar:], res):
            o_ref[...] = val.astype(o_ref.dtype)

    return pl.pallas_call(
        body, name=name, grid=(nj, t // tb),
        in_specs=[_row_spec(a, tb, nj) for a in rows] + [_par_spec(p) for p in params],
        out_specs=[pl.BlockSpec((tb, c // nj), lambda j, i: (i, j)) for c, _ in outs],
        out_shape=[jax.ShapeDtypeStruct((t, c), dt) for c, dt in outs],
        compiler_params=_cp("parallel", "arbitrary"),
    )(*rows, *params)


def _blocked_bwd(fn, rows, params, cts, row_grad_dtypes, *, adds=None, nj=1, tb, name):
    t = rows[0].shape[0]
    nr, npar, nct = len(rows), len(params), len(cts)
    adds = adds or {}
    add_keys = sorted(adds)
    want, want_dtypes = [], []
    for k, dts in enumerate(row_grad_dtypes):
        for dt in (dts if isinstance(dts, tuple) else (dts,)):
            if dt is not None:
                want.append(k)
                want_dtypes.append(dt)

    def body(*refs):
        row_refs = refs[:nr]
        par_refs = refs[nr:nr + npar]
        ct_refs = refs[nr + npar:nr + npar + nct]
        add_refs = dict(zip(add_keys, refs[nr + npar + nct:nr + npar + nct + len(add_keys)]))
        out_refs = refs[nr + npar + nct + len(add_keys):]
        _, vjp = jax.vjp(fn, *[r[...] for r in row_refs], *[p[0] for p in par_refs])
        grads = vjp(tuple(c[...].astype(F32) for c in ct_refs))
        for o_ref, k in zip(out_refs, want):
            g = grads[k]
            if k in add_refs:
                g = g + add_refs[k][...].astype(F32)
            o_ref[...] = g.astype(o_ref.dtype)
        first = pl.program_id(1) == 0
        for o_ref, g in zip(out_refs[len(want):], grads[nr:]):
            @pl.when(first)
            def _(o_ref=o_ref):
                o_ref[...] = jnp.zeros_like(o_ref)
            o_ref[0] += g

    add_arrs = [adds[k] for k in add_keys]
    return pl.pallas_call(
        body, name=name, grid=(nj, t // tb),
        in_specs=[_row_spec(a, tb, nj) for a in rows] + [_par_spec(p) for p in params]
        + [_row_spec(c, tb, nj) for c in cts] + [_row_spec(a, tb, nj) for a in add_arrs],
        out_specs=[_row_spec(rows[k], tb, nj) for k in want] + [_par_spec(p) for p in params],
        out_shape=[jax.ShapeDtypeStruct(rows[k].shape, dt) for k, dt in zip(want, want_dtypes)]
        + [jax.ShapeDtypeStruct(p.shape, F32) for p in params],
        compiler_params=_cp("parallel", "arbitrary"),
    )(*rows, *params, *cts, *add_arrs)


def _rms_fn(x, w):
    return (x * lax.rsqrt(jnp.mean(x * x, axis=-1, keepdims=True) + EPS) * w,)


def _silu(x):
    return x * jax.nn.sigmoid(x)


def _merge_fn(glu_v, glu_g, g_a, g_b, y_a):
    y_b = glu_v * jax.nn.sigmoid(glu_g)
    return (jax.nn.sigmoid(g_a) * y_a + jax.nn.sigmoid(g_b) * y_b,)


def _s5_bu_fn(u, b_re, b_im):
    return _bdot(u, b_re, "nn"), _bdot(u, b_im, "nn")


def _s5_out_fn(s_re, s_im, u, c_re, c_im_neg, d):
    return (jax.nn.gelu(_bdot(s_re, c_re, "nn") + _bdot(s_im, c_im_neg, "nn") + d * u),)


HALO = SUBLANES


def _conv_fn(comb, kw, ns):
    def fn(*args):
        cs = []
        for s in range(ns):
            xp, xm, w, b = args[4 * s:4 * s + 4]
            xe = jnp.concatenate([xp, xm], axis=0)
            tb = xm.shape[0]
            y = b
            for k in range(kw):
                off = HALO - kw + 1 + k
                y = y + w[k:k + 1, :] * xe[off:off + tb, :]
            cs.append(y)
        return comb(*cs)
    return fn


def _conv_specs(xs, ws, bs, tb, cb, time_of):
    specs = []
    for x, w, b in zip(xs, ws, bs):
        specs += [
            pl.BlockSpec((HALO, cb), lambda j, i: (jnp.maximum(time_of(i) * (tb // HALO) - 1, 0), j)),
            pl.BlockSpec((tb, cb), lambda j, i: (time_of(i), j)),
            pl.BlockSpec((w.shape[0], cb), lambda j, i: (0, j)),
            pl.BlockSpec((1, cb), lambda j, i: (0, j)),
        ]
    return specs


def _conv_fwd(comb, xs, ws, bs, *, out_dtype, name, tb=512):
    t, c = xs[0].shape
    cb = _tile(c, 512)
    ns = len(xs)
    fn = _conv_fn(comb, ws[0].shape[0], ns)

    def body(*refs):
        i = pl.program_id(1)
        args = []
        for s in range(ns):
            xp_ref, xm_ref, w_ref, b_ref = refs[4 * s:4 * s + 4]
            xp = jnp.where(i == 0, 0.0, xp_ref[...])
            args += [xp, xm_ref[...], w_ref[...], b_ref[...]]
        refs[4 * ns][...] = fn(*args).astype(out_dtype)

    flat = [a for x, w, b in zip(xs, ws, bs) for a in (x, x, w, b)]
    return pl.pallas_call(
        body, name=name, grid=(c // cb, t // tb),
        in_specs=_conv_specs(xs, ws, bs, tb, cb, lambda i: i),
        out_specs=pl.BlockSpec((tb, cb), lambda j, i: (i, j)),
        out_shape=jax.ShapeDtypeStruct((t, c), out_dtype),
        compiler_params=_cp("parallel", "arbitrary"),
    )(*flat)


def _conv_bwd(comb, xs, ws, bs, dy, *, dx_dtype, name, tb=512):
    t, c = xs[0].shape
    cb = _tile(c, 512)
    ns = len(xs)
    nt = t // tb
    fn = _conv_fn(comb, ws[0].shape[0], ns)

    def body(*refs):
        step = pl.program_id(1)
        in_refs = refs[:4 * ns]
        dy_ref = refs[4 * ns]
        out_refs = refs[4 * ns + 1:4 * ns + 1 + 3 * ns]
        carry_refs = refs[4 * ns + 1 + 3 * ns:]
        args = []
        for s in range(ns):
            xp_ref, xm_ref, w_ref, b_ref = in_refs[4 * s:4 * s + 4]
            xp = jnp.where(step == nt - 1, 0.0, xp_ref[...])
            args += [xp, xm_ref[...], w_ref[...], b_ref[...]]
        _, vjp = jax.vjp(fn, *args)
        grads = vjp(dy_ref[...].astype(F32))
        for s in range(ns):
            dxp, dxm, dw, db = grads[4 * s:4 * s + 4]
            dx_ref, dw_ref, db_ref = out_refs[3 * s:3 * s + 3]
            carry = carry_refs[s]

            @pl.when(step == 0)
            def _(carry=carry, dw_ref=dw_ref, db_ref=db_ref):
                carry[...] = jnp.zeros_like(carry)
                dw_ref[...] = jnp.zeros_like(dw_ref)
                db_ref[...] = jnp.zeros_like(db_ref)

            tail = jnp.concatenate([jnp.zeros((tb - HALO, cb), F32), carry[...]], axis=0)
            dx_ref[...] = (dxm + tail).astype(dx_dtype)
            carry[...] = dxp
            dw_ref[...] += dw
            db_ref[...] += db

    flat = [a for x, w, b in zip(xs, ws, bs) for a in (x, x, w, b)]
    rev = lambda i: nt - 1 - i
    out_specs, out_shape = [], []
    for x, w, b in zip(xs, ws, bs):
        out_specs += [pl.BlockSpec((tb, cb), lambda j, i: (rev(i), j)),
                      pl.BlockSpec((w.shape[0], cb), lambda j, i: (0, j)),
                      pl.BlockSpec((1, cb), lambda j, i: (0, j))]
        out_shape += [jax.ShapeDtypeStruct((t, c), dx_dtype), jax.ShapeDtypeStruct(w.shape, F32),
                      jax.ShapeDtypeStruct(b.shape, F32)]
    res = pl.pallas_call(
        body, name=name, grid=(c // cb, nt),
        in_specs=_conv_specs(xs, ws, bs, tb, cb, rev) + [pl.BlockSpec((tb, cb), lambda j, i: (rev(i), j))],
        out_specs=out_specs, out_shape=out_shape,
        scratch_shapes=[pltpu.VMEM((HALO, cb), F32) for _ in range(ns)],
        compiler_params=_cp("parallel", "arbitrary"),
    )(*flat, dy)
    return [tuple(res[3 * s:3 * s + 3]) for s in range(ns)]


def _comb_silu(c):
    return _silu(c)


def _comb_glu(cg, cv):
    return _silu(cg) * cv


def _ssd_fn(nheads, hdim):
    def fn(x, bm, cm, z, dtr, hin, dtb, alog, dsk, nw):
        q = x.shape[0]
        dt = jax.nn.softplus(dtr + dtb)
        da = dt * (-jnp.exp(alog))
        li = lax.broadcasted_iota(jnp.int32, (q, q), 0)
        si = lax.broadcasted_iota(jnp.int32, (q, q), 1)
        causal = li >= si
        tri = causal.astype(F32)
        acs = jnp.dot(tri, da, precision=HIGHEST, preferred_element_type=F32)
        acs_row = lax.dot_general(da, tri, (((0,), (1,)), ((), ())), precision=HIGHEST,
                                  preferred_element_type=F32)
        cb = _bdot(cm, bm, "nt")
        ch = _bdot(cm, hin, "nn")
        ys, hs = [], []
        for r in range(nheads):
            cols = slice(r * hdim, (r + 1) * hdim)
            xr = x[:, cols]
            a_col = acs[:, r:r + 1]
            decay = jnp.exp(jnp.where(causal, a_col - acs_row[r:r + 1, :], -1e30))
            xd = xr * dt[:, r:r + 1]
            y_diag = _bdot(cb * decay, xd, "nn")
            y_off = ch[:, cols] * jnp.exp(a_col)
            last = acs[q - 1:q, r:r + 1]
            st = _bdot(bm * jnp.exp(last - a_col), xd, "tn")
            hs.append(jnp.exp(last) * hin[:, cols] + st)
            ys.append(y_diag + y_off + dsk[:, r:r + 1] * xr)
        y = jnp.concatenate(ys, axis=1) * _silu(z)
        yn = y * lax.rsqrt(jnp.mean(y * y, axis=-1, keepdims=True) + EPS) * nw
        return yn, jnp.concatenate(hs, axis=1)
    return fn


def _ssd_specs(rp, nr, time_of):
    row = lambda w: pl.BlockSpec((CHUNK, w), lambda g, c: (time_of(c), g))
    par = lambda w: pl.BlockSpec((1, 1, w), lambda g, c: (g, 0, 0))
    return dict(
        x=row(rp), bc=row(D_STATE), dtr=pl.BlockSpec((1, CHUNK, nr), lambda g, c: (g, time_of(c), 0)),
        h=pl.BlockSpec((1, 1, D_STATE, rp), lambda g, c: (g, time_of(c), 0, 0)), pr=par(nr), pw=par(rp))


def _ssd_fwd(xs, bm, cm, z, dtr, dtb, alog, dsk, nw, *, name):
    t = xs.shape[0]
    g, _, nr = dtr.shape
    rp = xs.shape[1] // g
    nc = t // CHUNK
    fn = _ssd_fn(nr, rp // nr)
    sp = _ssd_specs(rp, nr, lambda c: c)

    def body(x_ref, b_ref, c_ref, z_ref, dtr_ref, dtb_ref, al_ref, dsk_ref, nw_ref, yn_ref, hs_ref, h_ref):
        @pl.when(pl.program_id(1) == 0)
        def _():
            h_ref[...] = jnp.zeros_like(h_ref)
        hin = h_ref[...]
        hs_ref[0, 0] = hin
        yn, hout = fn(x_ref[...], b_ref[...], c_ref[...], z_ref[...], dtr_ref[0], hin,
                      dtb_ref[0], al_ref[0], dsk_ref[0], nw_ref[0])
        yn_ref[...] = yn.astype(yn_ref.dtype)
        h_ref[...] = hout

    return pl.pallas_call(
        body, name=name, grid=(g, nc),
        in_specs=[sp["x"], sp["bc"], sp["bc"], sp["x"], sp["dtr"], sp["pr"], sp["pr"], sp["pr"], sp["pw"]],
        out_specs=[sp["x"], sp["h"]],
        out_shape=[jax.ShapeDtypeStruct(xs.shape, BF16), jax.ShapeDtypeStruct((g, nc, D_STATE, rp), F32)],
        scratch_shapes=[pltpu.VMEM((D_STATE, rp), F32)],
        compiler_params=_cp("parallel", "arbitrary"),
    )(xs, bm, cm, z, dtr, dtb, alog, dsk, nw)


def _ssd_bwd(xs, bm, cm, z, dtr, hsave, dtb, alog, dsk, nw, dyn, *, name):
    t = xs.shape[0]
    g, _, nr = dtr.shape
    rp = xs.shape[1] // g
    nc = t // CHUNK
    fn = _ssd_fn(nr, rp // nr)
    sp = _ssd_specs(rp, nr, lambda c: nc - 1 - c)

    def body(x_ref, b_ref, c_ref, z_ref, dtr_ref, hs_ref, dtb_ref, al_ref, dsk_ref, nw_ref, dyn_ref,
             dx_ref, db_ref, dc_ref, dz_ref, ddtr_ref, ddtb_ref, dal_ref, ddsk_ref, dnw_ref, dh_ref):
        first = pl.program_id(1) == 0

        @pl.when(first)
        def _():
            dh_ref[...] = jnp.zeros_like(dh_ref)
            for r in (ddtb_ref, dal_ref, ddsk_ref, dnw_ref):
                r[...] = jnp.zeros_like(r)

        _, vjp = jax.vjp(fn, x_ref[...], b_ref[...], c_ref[...], z_ref[...], dtr_ref[0], hs_ref[0, 0],
                         dtb_ref[0], al_ref[0], dsk_ref[0], nw_ref[0])
        dx, db, dc, dz, ddtr, dhin, ddtb, dal, ddsk, dnw = vjp((dyn_ref[...].astype(F32), dh_ref[...]))
        dx_ref[...] = dx
        db_ref[...] = db
        dc_ref[...] = dc
        dz_ref[...] = dz.astype(dz_ref.dtype)
        ddtr_ref[0] = ddtr
        dh_ref[...] = dhin
        ddtb_ref[0] += ddtb
        dal_ref[0] += dal
        ddsk_ref[0] += ddsk
        dnw_ref[0] += dnw

    sd = jax.ShapeDtypeStruct
    return pl.pallas_call(
        body, name=name, grid=(g, nc),
        in_specs=[sp["x"], sp["bc"], sp["bc"], sp["x"], sp["dtr"], sp["h"], sp["pr"], sp["pr"], sp["pr"], sp["pw"],
                  sp["x"]],
        out_specs=[sp["x"], sp["bc"], sp["bc"], sp["x"], sp["dtr"], sp["pr"], sp["pr"], sp["pr"], sp["pw"]],
        out_shape=[sd(xs.shape, F32), sd(bm.shape, F32), sd(cm.shape, F32), sd(z.shape, BF16), sd(dtr.shape, F32),
                   sd(dtb.shape, F32), sd(alog.shape, F32), sd(dsk.shape, F32), sd(nw.shape, F32)],
        scratch_shapes=[pltpu.VMEM((D_STATE, rp), F32)],
        compiler_params=_cp("parallel", "arbitrary"),
    )(xs, bm, cm, z, dtr, hsave, dtb, alog, dsk, nw, dyn)


def _s5_param_fn(lam_re, lam_im, log_dt, bt_re, bt_im):
    lr = jnp.minimum(lam_re, EIG_MAX)
    dt = jnp.exp(log_dt)
    mag = jnp.exp(lr * dt)
    lb_re = mag * jnp.cos(lam_im * dt)
    lb_im = mag * jnp.sin(lam_im * dt)
    n_re = lb_re - 1.0
    den = lr * lr + lam_im * lam_im
    k_re = (n_re * lr + lb_im * lam_im) / den
    k_im = (lb_im * lr - n_re * lam_im) / den
    return lb_re, lb_im, k_re * bt_re - k_im * bt_im, k_re * bt_im + k_im * bt_re


def _s5_params(lam_re, lam_im, log_dt, bt_re, bt_im, cts=None, *, name):
    args = (lam_re, lam_im, log_dt, bt_re, bt_im)
    n = len(args)

    def body(*refs):
        vals = [r[...] for r in refs[:n]]
        if cts is None:
            res = _s5_param_fn(*vals)
        else:
            _, vjp = jax.vjp(_s5_param_fn, *vals)
            res = vjp(tuple(r[...] for r in refs[n:n + 4]))
        for o_ref, v in zip(refs[-len(res):], res):
            o_ref[...] = v

    if cts is None:
        out = [lam_re, lam_im, bt_re, bt_im]
        ins = args
    else:
        out = list(args)
        ins = args + tuple(cts)
    return pl.pallas_call(
        body, name=name, out_shape=[jax.ShapeDtypeStruct(a.shape, F32) for a in out],
        compiler_params=pltpu.CompilerParams(vmem_limit_bytes=VMEM_LIMIT),
    )(*ins)


SCAN_COLS = 512


def _cmul(xr, xi, yr, yi):
    return xr * yr - xi * yi, xr * yi + xi * yr


def _scan_consts(a_re, a_im, cols, reverse):
    shape = (SUBLANES, cols)
    row = lax.broadcasted_iota(jnp.int32, shape, 0)
    dist = (SUBLANES - 1 - row) if reverse else row
    mr, mi = jnp.broadcast_to(a_re, shape), jnp.broadcast_to(a_im, shape)
    pr, pi = mr, mi
    mults = []
    for d in (1, 2, 4):
        mults.append((mr, mi))
        qr, qi = _cmul(pr, pi, mr, mi)
        has_bit = (dist & d) != 0
        pr, pi = jnp.where(has_bit, qr, pr), jnp.where(has_bit, qi, pi)
        mr, mi = _cmul(mr, mi, mr, mi)
    return mults, (pr, pi), dist


def _scan_group(xr, xi, consts, cr, ci, reverse):
    mults, (pr, pi), dist = consts
    for d, (mr, mi) in zip((1, 2, 4), mults):
        shift = (SUBLANES - d) if reverse else d
        sr = jnp.where(dist >= d, pltpu.roll(xr, shift, 0), 0.0)
        si = jnp.where(dist >= d, pltpu.roll(xi, shift, 0), 0.0)
        tr, ti = _cmul(mr, mi, sr, si)
        xr, xi = xr + tr, xi + ti
    last = slice(0, 1) if reverse else slice(SUBLANES - 1, SUBLANES)
    nr, ni = _cmul(pr[last], pi[last], cr, ci)
    tr, ti = _cmul(pr, pi, jnp.broadcast_to(cr, xr.shape), jnp.broadcast_to(ci, xr.shape))
    return xr + tr, xi + ti, xr[last] + nr, xi[last] + ni


def _scan_specs(tb, time_of):
    row = pl.BlockSpec((tb, SCAN_COLS), lambda j, i: (time_of(i), j))
    par = pl.BlockSpec((1, SCAN_COLS), lambda j, i: (0, j))
    return row, par


def _s5_scan_fwd(bu_re, bu_im, lb_re, lb_im, *, name, tb=512):
    t, c = bu_re.shape
    nj = c // SCAN_COLS
    row, par = _scan_specs(tb, lambda i: i)

    def body(bre_ref, bim_ref, lre_ref, lim_ref, sre_ref, sim_ref, cre_ref, cim_ref):
        @pl.when(pl.program_id(1) == 0)
        def _():
            cre_ref[...] = jnp.zeros_like(cre_ref)
            cim_ref[...] = jnp.zeros_like(cim_ref)
        consts = _scan_consts(lre_ref[...], lim_ref[...], SCAN_COLS, False)

        def group(k, carry):
            rows = pl.ds(pl.multiple_of(k * SUBLANES, SUBLANES), SUBLANES)
            sr, si, cr, ci = _scan_group(bre_ref[rows, :], bim_ref[rows, :], consts, *carry, False)
            sre_ref[rows, :] = sr
            sim_ref[rows, :] = si
            return cr, ci

        sr, si = lax.fori_loop(0, tb // SUBLANES, group, (cre_ref[...], cim_ref[...]), unroll=4)
        cre_ref[...] = sr
        cim_ref[...] = si

    return pl.pallas_call(
        body, name=name, grid=(nj, t // tb), in_specs=[row, row, par, par], out_specs=[row, row],
        out_shape=[jax.ShapeDtypeStruct((t, c), F32)] * 2,
        scratch_shapes=[pltpu.VMEM((1, SCAN_COLS), F32)] * 2,
        compiler_params=_cp("parallel", "arbitrary"),
    )(bu_re, bu_im, lb_re, lb_im)


def _s5_scan_bwd(s_re, s_im, ds_re, ds_im, lb_re, lb_im, *, name, tb=512):
    t, c = s_re.shape
    nj = c // SCAN_COLS
    nt = t // tb
    rev = lambda i: nt - 1 - i
    row, par = _scan_specs(tb, rev)
    prev = pl.BlockSpec((HALO, SCAN_COLS), lambda j, i: (jnp.maximum(rev(i) * (tb // HALO) - 1, 0), j))

    def body(sre_ref, sim_ref, pre_ref, pim_ref, dre_ref, dim_ref, lre_ref, lim_ref,
             gre_ref, gim_ref, dlre_ref, dlim_ref, cre_ref, cim_ref, ext_re, ext_im):
        step_id = pl.program_id(1)

        @pl.when(step_id == 0)
        def _():
            cre_ref[...] = jnp.zeros_like(cre_ref)
            cim_ref[...] = jnp.zeros_like(cim_ref)
            dlre_ref[...] = jnp.zeros_like(dlre_ref)
            dlim_ref[...] = jnp.zeros_like(dlim_ref)
        consts = _scan_consts(lre_ref[...], -lim_ref[...], SCAN_COLS, True)
        ngroups = tb // SUBLANES

        def group(k, carry):
            rows = pl.ds(pl.multiple_of((ngroups - 1 - k) * SUBLANES, SUBLANES), SUBLANES)
            gr, gi, cr, ci = _scan_group(dre_ref[rows, :], dim_ref[rows, :], consts, *carry, True)
            gre_ref[rows, :] = gr
            gim_ref[rows, :] = gi
            return cr, ci

        gr, gi = lax.fori_loop(0, ngroups, group, (cre_ref[...], cim_ref[...]), unroll=4)
        cre_ref[...] = gr
        cim_ref[...] = gi
        has_past = step_id != nt - 1
        ext_re[pl.ds(0, HALO), :] = jnp.where(has_past, pre_ref[...], 0.0)
        ext_im[pl.ds(0, HALO), :] = jnp.where(has_past, pim_ref[...], 0.0)
        ext_re[pl.ds(HALO, tb), :] = sre_ref[...]
        ext_im[pl.ds(HALO, tb), :] = sim_ref[...]
        pr, pi = ext_re[pl.ds(HALO - 1, tb), :], ext_im[pl.ds(HALO - 1, tb), :]
        g_re, g_im = gre_ref[...], gim_ref[...]
        dlre_ref[...] += jnp.sum(pr * g_re + pi * g_im, axis=0, keepdims=True)
        dlim_ref[...] += jnp.sum(pr * g_im - pi * g_re, axis=0, keepdims=True)

    return pl.pallas_call(
        body, name=name, grid=(nj, nt),
        in_specs=[row, row, prev, prev, row, row, par, par], out_specs=[row, row, par, par],
        out_shape=[jax.ShapeDtypeStruct((t, c), F32)] * 2 + [jax.ShapeDtypeStruct((1, c), F32)] * 2,
        scratch_shapes=[pltpu.VMEM((1, SCAN_COLS), F32)] * 2 + [pltpu.VMEM((HALO + tb, SCAN_COLS), F32)] * 2,
        compiler_params=_cp("parallel", "arbitrary"),
    )(s_re, s_im, s_re, s_im, ds_re, ds_im, lb_re, lb_im)


def _loss_fn(h, w, tgt):
    err = _rms_fn(h, w)[0] - tgt
    return 0.5 * jnp.sum(jnp.mean(err * err, axis=-1, keepdims=True), axis=0, keepdims=True)


def _loss_head(h, w, tgt, *, name, tb=256):
    t, d = h.shape

    def body(h_ref, w_ref, t_ref, loss_ref, dh_ref, dhb_ref, dw_ref):
        @pl.when(pl.program_id(0) == 0)
        def _():
            loss_ref[...] = jnp.zeros_like(loss_ref)
            dw_ref[...] = jnp.zeros_like(dw_ref)
        part, vjp = jax.vjp(_loss_fn, h_ref[...], w_ref[...], t_ref[...])
        dh, dw, _ = vjp(jnp.ones((1, 1), F32))
        loss_ref[...] += jnp.broadcast_to(part, loss_ref.shape)
        dh_ref[...] = dh
        dhb_ref[...] = dh.astype(BF16)
        dw_ref[...] += dw

    row = pl.BlockSpec((tb, d), lambda i: (i, 0))
    par = pl.BlockSpec((1, d), lambda i: (0, 0))
    return pl.pallas_call(
        body, name=name, grid=(t // tb,), in_specs=[row, par, row],
        out_specs=[pl.BlockSpec((SUBLANES, LANES), lambda i: (0, 0)), row, row, par],
        out_shape=[jax.ShapeDtypeStruct((SUBLANES, LANES), F32), jax.ShapeDtypeStruct((t, d), F32),
                   jax.ShapeDtypeStruct((t, d), BF16), jax.ShapeDtypeStruct((1, d), F32)],
        compiler_params=_cp("arbitrary"),
    )(h, w, tgt)


def _adamw(w, g, m, v, *, name):
    r, c = w.shape
    tr = _tile(r, 256, SUBLANES)

    def body(w_ref, g_ref, m_ref, v_ref, d_ref, nm_ref, nv_ref):
        g = g_ref[...]
        nm = ADAM_B1 * m_ref[...] + (1.0 - ADAM_B1) * g
        nv = ADAM_B2 * v_ref[...] + (1.0 - ADAM_B2) * (g * g)
        m_hat = nm / (1.0 - ADAM_B1 ** ADAM_STEP)
        v_hat = nv / (1.0 - ADAM_B2 ** ADAM_STEP)
        d_ref[...] = -ADAM_LR * (m_hat / (jnp.sqrt(v_hat) + ADAM_EPS) + ADAM_WD * w_ref[...])
        nm_ref[...] = nm
        nv_ref[...] = nv

    spec = pl.BlockSpec((tr, c), lambda i: (i, 0))
    return pl.pallas_call(
        body, name=name, grid=(r // tr,), in_specs=[spec] * 4, out_specs=[spec] * 3,
        out_shape=[jax.ShapeDtypeStruct((r, c), F32)] * 3, compiler_params=_cp("parallel"),
    )(w, g, m, v)


def _sum_parts(parts, *, name):
    _, r, c = parts.shape
    tr = _tile(r, 128, SUBLANES)

    def body(p_ref, o_ref):
        acc = p_ref[0].astype(F32)
        for k in range(1, N_DEV):
            acc = acc + p_ref[k].astype(F32)
        o_ref[...] = acc

    return pl.pallas_call(
        body, name=name, grid=(r // tr,), in_specs=[pl.BlockSpec((N_DEV, tr, c), lambda i: (0, i, 0))],
        out_specs=pl.BlockSpec((tr, c), lambda i: (i, 0)), out_shape=jax.ShapeDtypeStruct((r, c), F32),
        compiler_params=_cp("parallel"),
    )(parts)


def _position():
    return lax.axis_index("x"), lax.axis_index("y"), lax.axis_index("c")


def _flat(px, py, pc):
    return 4 * px + 2 * py + pc


def _all_gather(shard, *, name):
    def body(x_ref, out_ref, token, send_sems, recv_sems, local_sem):
        token[...] = jnp.zeros_like(token)
        x, y, c = _position()
        me, sibling = (x, y, c), (x, y, 1 - c)
        chips = [(1 - x, y), (x, 1 - y), (1 - x, 1 - y)]

        def copy(k, block, to, src=None):
            slot = out_ref.at[_flat(*block)]
            return pltpu.make_async_remote_copy(
                src_ref=slot if src is None else src, dst_ref=slot, send_sem=send_sems.at[k],
                recv_sem=recv_sems.at[k], device_id=to, device_id_type=MESH)

        mine = pltpu.make_async_copy(x_ref, out_ref.at[_flat(*me)], local_sem)
        mine.start()
        first = [copy(0, me, sibling, src=x_ref)]
        first += [copy(1 + j, me, (*chip, c), src=x_ref) for j, chip in enumerate(chips)]
        for cp in first:
            cp.start()
        passed = [copy(4 + j, (*chip, c), sibling) for j, chip in enumerate(chips)]
        for j, chip in enumerate(chips):
            copy(1 + j, (*chip, c), me).wait_recv()
            passed[j].start()
        copy(0, sibling, me).wait_recv()
        for j, chip in enumerate(chips):
            copy(4 + j, (*chip, 1 - c), me).wait_recv()
        for cp in first + passed:
            cp.wait_send()
        mine.wait()

    return pl.pallas_call(
        body, name=name,
        out_shape=(jax.ShapeDtypeStruct((N_DEV,) + shard.shape, shard.dtype),
                   jax.ShapeDtypeStruct((SUBLANES, LANES), F32)),
        in_specs=[pl.BlockSpec(memory_space=pl.ANY)],
        out_specs=(pl.BlockSpec(memory_space=pl.ANY), pl.BlockSpec(memory_space=pltpu.VMEM)),
        scratch_shapes=[pltpu.SemaphoreType.DMA((7,)), pltpu.SemaphoreType.DMA((7,)), pltpu.SemaphoreType.DMA(())],
    )(shard)


_HBM = pl.BlockSpec(memory_space=pltpu.HBM)
_SEM = pl.BlockSpec(memory_space=pltpu.SEMAPHORE)
_EFFECT = pltpu.SideEffectType.DATAFLOW_SIDE_EFFECTING


def _copy_ends(src_ref, land_ref, mode, me, to):
    if mode == "gather_slot":
        return src_ref, land_ref.at[me]
    if mode == "gather_cols":
        w = src_ref.shape[1]
        return src_ref, land_ref.at[:, pl.ds(pl.multiple_of(me * w, LANES), w)]
    if mode == "scatter_slot":
        return src_ref.at[to], land_ref.at[me]
    w = land_ref.shape[2]
    return src_ref.at[:, pl.ds(pl.multiple_of(to * w, LANES), w)], land_ref.at[me]


BF16_ROWS = 16


def _land_shape(src, mode):
    if mode == "gather_slot":
        return (N_DEV,) + src.shape
    if mode == "gather_cols":
        return (src.shape[0], N_DEV * src.shape[1])
    if mode == "scatter_slot":
        return src.shape
    return (N_DEV, src.shape[0], src.shape[1] // N_DEV)


def _exchange_copies(src_ref, land_ref, send_sems, recv_sems, mode):
    x, y, c = _position()
    me = _flat(x, y, c)
    copies = []
    for k in range(1, N_DEV):
        peer = (x ^ ((k >> 2) & 1), y ^ ((k >> 1) & 1), c ^ (k & 1))
        src, dst = _copy_ends(src_ref, land_ref, mode, me, _flat(*peer))
        copies.append(pltpu.make_async_remote_copy(
            src_ref=src, dst_ref=dst, send_sem=send_sems.at[k - 1], recv_sem=recv_sems.at[k - 1],
            device_id=peer, device_id_type=MESH))
    return copies


def _place_own(src, mode, dev, *, name):
    rows = src.shape[1] if mode == "scatter_slot" else src.shape[0]
    tr = _tile(rows, 512, BF16_ROWS)
    land = _land_shape(src, mode)
    width = land[-1] if mode.startswith("scatter") else src.shape[1]
    slot = pl.BlockSpec((1, tr, width), lambda i, d: (d[0], i, 0))
    cols = pl.BlockSpec((tr, width), lambda i, d: (i, d[0]))
    whole = pl.BlockSpec((tr, width), lambda i, d: (i, 0))
    in_spec, out_spec = {"gather_slot": (whole, slot), "gather_cols": (whole, cols), "scatter_slot": (slot, slot),
                         "scatter_cols": (cols, slot)}[mode]

    def body(dev_ref, src_ref, land_ref):
        land_ref[...] = src_ref[...].reshape(land_ref.shape)

    return pl.pallas_call(
        body, name=name, out_shape=jax.ShapeDtypeStruct(land, src.dtype),
        grid_spec=pltpu.PrefetchScalarGridSpec(num_scalar_prefetch=1, grid=(rows // tr,), in_specs=[in_spec],
                                               out_specs=out_spec),
        compiler_params=_cp("parallel"),
    )(dev, src)


def _exchange_start(src, mode, dev, *, name):
    land = _place_own(src, mode, dev, name=name + "_own")
    n_copies = N_DEV - 1

    def body(src_ref, land_ref, send_sems, recv_sems, src_thru, land_thru, token):
        for cp in _exchange_copies(src_ref, land_ref, send_sems, recv_sems, mode):
            cp.start()
        token[...] = jnp.zeros_like(token)

    hbm = pltpu.with_memory_space_constraint
    *handle, token = pl.pallas_call(
        body, name=name,
        out_shape=(pltpu.SemaphoreType.DMA((n_copies,)), pltpu.SemaphoreType.DMA((n_copies,)),
                   pltpu.HBM(src.shape, src.dtype), pltpu.HBM(land.shape, land.dtype),
                   jax.ShapeDtypeStruct((SUBLANES, LANES), F32)),
        in_specs=(_HBM, _HBM), out_specs=(_SEM, _SEM, _HBM, _HBM, pl.BlockSpec(memory_space=pltpu.VMEM)),
        input_output_aliases={0: 2, 1: 3}, compiler_params=pltpu.CompilerParams(has_side_effects=_EFFECT),
    )(hbm(src, pltpu.HBM), hbm(land, pltpu.HBM))
    return (tuple(handle), mode), token


def _exchange_wait(pending, after, *, name):
    (send_sems, recv_sems, src_thru, land_thru), mode = pending

    def body(src_ref, land_ref, send_sems, recv_sems, after_ref, src_dead, got_ref):
        for cp in _exchange_copies(src_ref, land_ref, send_sems, recv_sems, mode):
            cp.wait_send()
            cp.wait_recv()

    return pl.pallas_call(
        body, name=name, out_shape=(pltpu.HBM(src_thru.shape, src_thru.dtype), pltpu.HBM(land_thru.shape, land_thru.dtype)),
        in_specs=(_HBM, _HBM, _SEM, _SEM, pl.BlockSpec(memory_space=pl.ANY)), out_specs=(_HBM, _HBM),
        input_output_aliases={0: 0, 1: 1}, compiler_params=pltpu.CompilerParams(has_side_effects=_EFFECT),
    )(src_thru, land_thru, send_sems, recv_sems, after)[1]


def _after(x, token):
    return x + token[0, 0].astype(x.dtype)


def _pad_cols(a, mult):
    pad = -a.shape[1] % mult
    return jnp.pad(a, ((0, 0), (0, pad))) if pad else a


def _pack(arrs, cols):
    flat = jnp.concatenate([a.reshape(-1).astype(F32) for a in arrs])
    sizes = [int(a.size) for a in arrs]
    flat = jnp.pad(flat, (0, -flat.shape[0] % (SUBLANES * cols)))
    return flat.reshape(-1, cols), sizes


def _unpack(flat2d, sizes, shapes):
    flat = flat2d.reshape(-1)
    out, o = [], 0
    for n, s in zip(sizes, shapes):
        out.append(flat[o:o + n].reshape(s))
        o += n
    return out


PACK_COLS = SUBLANES * LANES


def kernel(x, norm_mix_w, w_in, conv_a_w, conv_a_b, dt_bias, a_log, d_a, norm_a_w, w_proj_a, s5_lam_re, s5_lam_im, s5_log_dt, s5_b_re, s5_b_im, s5_c_re, s5_c_im, s5_d, w_s5_glu, w_out, norm_ffn_w, w_up, conv_ffn_w, conv_ffn_b, w_down, norm_final_w, loss_target, m_norm_mix_w, m_w_in, m_conv_a_w, m_conv_a_b, m_dt_bias, m_a_log, m_d_a, m_norm_a_w, m_w_proj_a, m_s5_lam_re, m_s5_lam_im, m_s5_log_dt, m_s5_b_re, m_s5_b_im, m_s5_c_re, m_s5_c_im, m_s5_d, m_w_s5_glu, m_w_out, m_norm_ffn_w, m_w_up, m_conv_ffn_w, m_conv_ffn_b, m_w_down, m_norm_final_w, v_norm_mix_w, v_w_in, v_conv_a_w, v_conv_a_b, v_dt_bias, v_a_log, v_d_a, v_norm_a_w, v_w_proj_a, v_s5_lam_re, v_s5_lam_im, v_s5_log_dt, v_s5_b_re, v_s5_b_im, v_s5_c_re, v_s5_c_im, v_s5_d, v_w_s5_glu, v_w_out, v_norm_ffn_w, v_w_up, v_conv_ffn_w, v_conv_ffn_b, v_w_down, v_norm_final_w):
    weights = dict(norm_mix_w=norm_mix_w, w_in=w_in, conv_a_w=conv_a_w, conv_a_b=conv_a_b, dt_bias=dt_bias, a_log=a_log, d_a=d_a, norm_a_w=norm_a_w, w_proj_a=w_proj_a, s5_lam_re=s5_lam_re, s5_lam_im=s5_lam_im, s5_log_dt=s5_log_dt, s5_b_re=s5_b_re, s5_b_im=s5_b_im, s5_c_re=s5_c_re, s5_c_im=s5_c_im, s5_d=s5_d, w_s5_glu=w_s5_glu, w_out=w_out, norm_ffn_w=norm_ffn_w, w_up=w_up, conv_ffn_w=conv_ffn_w, conv_ffn_b=conv_ffn_b, w_down=w_down, norm_final_w=norm_final_w)
    moms = dict(norm_mix_w=m_norm_mix_w, w_in=m_w_in, conv_a_w=m_conv_a_w, conv_a_b=m_conv_a_b, dt_bias=m_dt_bias, a_log=m_a_log, d_a=m_d_a, norm_a_w=m_norm_a_w, w_proj_a=m_w_proj_a, s5_lam_re=m_s5_lam_re, s5_lam_im=m_s5_lam_im, s5_log_dt=m_s5_log_dt, s5_b_re=m_s5_b_re, s5_b_im=m_s5_b_im, s5_c_re=m_s5_c_re, s5_c_im=m_s5_c_im, s5_d=m_s5_d, w_s5_glu=m_w_s5_glu, w_out=m_w_out, norm_ffn_w=m_norm_ffn_w, w_up=m_w_up, conv_ffn_w=m_conv_ffn_w, conv_ffn_b=m_conv_ffn_b, w_down=m_w_down, norm_final_w=m_norm_final_w)
    vars_ = dict(norm_mix_w=v_norm_mix_w, w_in=v_w_in, conv_a_w=v_conv_a_w, conv_a_b=v_conv_a_b, dt_bias=v_dt_bias, a_log=v_a_log, d_a=v_d_a, norm_a_w=v_norm_a_w, w_proj_a=v_w_proj_a, s5_lam_re=v_s5_lam_re, s5_lam_im=v_s5_lam_im, s5_log_dt=v_s5_log_dt, s5_b_re=v_s5_b_re, s5_b_im=v_s5_b_im, s5_c_re=v_s5_c_re, s5_c_im=v_s5_c_im, s5_d=v_s5_d, w_s5_glu=v_w_s5_glu, w_out=v_w_out, norm_ffn_w=v_norm_ffn_w, w_up=v_w_up, conv_ffn_w=v_conv_ffn_w, conv_ffn_b=v_conv_ffn_b, w_down=v_w_down, norm_final_w=v_norm_final_w)
    names = list(weights)
    col_sharded = ("w_in", "w_s5_glu", "w_up")
    row_sharded = ("w_proj_a", "w_out", "w_down")
    conv_sharded = ("conv_a_w", "conv_ffn_w")
    replicated = [n for n in names if n not in col_sharded + row_sharded + conv_sharded]

    t, d = x.shape[1:]
    x2, tgt = x.reshape(t, d), loss_target.reshape(t, d)
    nh = dt_bias.shape[-1]
    d_inner = norm_a_w.shape[-1]
    conv_dim = conv_a_b.shape[-1]
    gn = (conv_dim - d_inner) // 2
    ng = gn // D_STATE
    nr = nh // ng
    rp = d_inner // ng
    d_s5 = s5_d.shape[-1]
    gs, ps = s5_lam_re.shape[1:]
    cs = d_s5 // gs
    n_oct = gs // 8
    assert (gs * ps) % SCAN_COLS == 0 and 8 * cs == LANES and gs % 8 == 0
    d_ff = w_down.shape[1] * N_DEV
    dev = _flat(*_position())
    dev1 = dev.reshape(1).astype(jnp.int32)

    ka, kf = conv_a_w.shape[1], conv_ffn_w.shape[1]
    taps = jnp.concatenate([conv_a_w[0].reshape(1, -1), conv_ffn_w[0].reshape(1, -1)], axis=1)
    taps = _all_gather(taps, name="ag_conv_taps")[0][:, 0]
    def by_cols(n):
        return n in ("w_s5_glu", "w_up") and weights[n].shape[2] % LANES == 0

    w_in_blocks, started = _all_gather(w_in[0].astype(BF16), name="ag_w_in")
    pending = {}
    for n in ("w_proj_a", "w_s5_glu", "w_out", "w_up", "w_down"):
        shard = _after(weights[n][0], started).astype(BF16)
        pending[n], token = _exchange_start(shard, "gather_cols" if by_cols(n) else "gather_slot", dev1,
                                            name="ag_" + n)
        started = started + token

    def gathered(n, after):
        g = _exchange_wait(pending[n], after, name="agw_" + n)
        if by_cols(n):
            return g
        if n in row_sharded:
            return g.reshape(-1, g.shape[2])
        return jnp.transpose(g, (1, 0, 2)).reshape(g.shape[1], -1)

    seg_sizes = dict(z=d_inner, xs=d_inner, bm=gn, cm=gn, dt=nh, u=d_s5, ga=d, gb=d)
    seg_names = tuple(seg_sizes)
    pieces, seg_at = _w_in_pieces(seg_sizes, ("z", "xs", "ga", "gb", "bm", "cm", "u", "dt"), w_in.shape[2])
    na = ka * conv_a_w.shape[2]
    cw_a = jnp.transpose(taps[:, :na].reshape(N_DEV, ka, -1), (1, 0, 2)).reshape(ka, conv_dim)
    cw_f = jnp.transpose(taps[:, na:].reshape(N_DEV, kf, -1), (1, 0, 2)).reshape(kf, 2 * d_ff)
    cb_a, cb_f = conv_a_b, conv_ffn_b
    a_cols = {"xs": slice(0, d_inner), "bm": slice(d_inner, d_inner + gn), "cm": slice(d_inner + gn, conv_dim)}

    w1 = norm_mix_w.reshape(1, 1, d) + started[0, 0]
    hn1, = _blocked_fwd(_rms_fn, [x2], [w1], [(d, BF16)], tb=256, name="rms1")
    w_in_p = _w_in_pack(w_in_blocks, pieces, seg_at, seg_sizes, name="w_in_pack")
    pre = {sn: _mm(hn1, w_in_p, b_win=seg_at[sn], name="in_" + sn) for sn in seg_names}
    act_a = {sn: _conv_fwd(_comb_silu, [pre[sn]], [cw_a[:, a_cols[sn]]], [cb_a[:, a_cols[sn]]], out_dtype=F32,
                           name="conv_a_" + sn) for sn in a_cols}
    dtr3 = jnp.transpose(pre["dt"][:, :nh].reshape(t, ng, nr), (1, 0, 2))
    dtb3, alog3, dsk3 = (p.reshape(ng, 1, nr) for p in (dt_bias, a_log, d_a))
    nw3 = norm_a_w.reshape(ng, 1, rp)
    yn, hsave = _ssd_fwd(act_a["xs"], act_a["bm"], act_a["cm"], pre["z"], dtr3, dtb3, alog3, dsk3, nw3, name="ssd")
    w_proj = gathered("w_proj_a", yn)
    y_a = _mm(yn, w_proj, name="proj_a")

    lam_re3, lam_im3 = s5_lam_re[0][:, None, :], s5_lam_im[0][:, None, :]
    logdt3 = s5_log_dt[0][:, None, None]
    bt_re, bt_im = jnp.transpose(s5_b_re[0], (0, 2, 1)), jnp.transpose(s5_b_im[0], (0, 2, 1))
    lb_re3, lb_im3, bb_re, bb_im = _s5_params(lam_re3, lam_im3, logdt3, bt_re, bt_im, name="s5_params")
    eye = jnp.eye(8, dtype=F32)

    def diag_b(bt):
        return (bt.reshape(n_oct, 8, cs, 1, ps) * eye[None, :, None, :, None]).reshape(n_oct, 8 * cs, 8 * ps)

    def undiag_b(blk):
        return (blk.reshape(n_oct, 8, cs, 8, ps) * eye[None, :, None, :, None]).sum(axis=3).reshape(gs, cs, ps)

    def diag_c(cm):
        ct = jnp.transpose(cm.reshape(n_oct, 8, cs, ps), (0, 1, 3, 2))
        return (ct[:, :, :, None, :] * eye[None, :, None, :, None]).reshape(n_oct, 8 * ps, 8 * cs)

    def undiag_c(blk):
        ct = (blk.reshape(n_oct, 8, ps, 8, cs) * eye[None, :, None, :, None]).sum(axis=3)
        return jnp.transpose(ct, (0, 1, 3, 2)).reshape(gs, cs, ps)

    b_blk_re, b_blk_im = diag_b(bb_re), diag_b(bb_im)
    c_blk_re, c_blk_imn = diag_c(s5_c_re[0]), diag_c(-s5_c_im[0])
    d3 = s5_d.reshape(n_oct, 1, LANES)
    lb_re, lb_im = lb_re3.reshape(1, gs * ps), lb_im3.reshape(1, gs * ps)
    u = pre["u"]
    bu_re, bu_im = _blocked_fwd(_s5_bu_fn, [u], [b_blk_re, b_blk_im], [(gs * ps, F32)] * 2, nj=n_oct, tb=512,
                                name="s5_bu")
    s_re, s_im = _s5_scan_fwd(bu_re, bu_im, lb_re, lb_im, name="s5_scan")
    yb, = _blocked_fwd(_s5_out_fn, [s_re, s_im, u], [c_blk_re, c_blk_imn, d3], [(d_s5, BF16)], nj=n_oct, tb=512,
                       name="s5_out")
    w_glu = gathered("w_s5_glu", yb)
    glu_v = _mm(yb, w_glu, b_win=(0, d), name="glu_v")
    glu_g = _mm(yb, w_glu, b_win=(d, d), name="glu_g")
    merged, = _blocked_fwd(_merge_fn, [glu_v, glu_g, pre["ga"], pre["gb"], y_a], [], [(d, BF16)], tb=256,
                           name="merge")
    w_o = gathered("w_out", merged)
    h1 = _mm(merged, w_o, acc=x2, name="out_proj")
    w2 = norm_ffn_w.reshape(1, 1, d)
    hn2, = _blocked_fwd(_rms_fn, [h1], [w2], [(d, BF16)], tb=256, name="rms2")
    w_u = gathered("w_up", hn2)
    up_g = _mm(hn2, w_u, b_win=(0, d_ff), name="up_g")
    up_v = _mm(hn2, w_u, b_win=(d_ff, d_ff), name="up_v")
    f_w = [cw_f[:, :d_ff], cw_f[:, d_ff:]]
    f_b = [cb_f[:, :d_ff], cb_f[:, d_ff:]]
    act = _conv_fwd(_comb_glu, [up_g, up_v], f_w, f_b, out_dtype=BF16, name="conv_ffn")
    w_dn = gathered("w_down", act)
    h2 = _mm(act, w_dn, acc=h1, name="down")
    loss_tile, dh2, dh2_b, g_final = _loss_head(h2, norm_final_w.reshape(1, d), tgt, name="loss_head")

    grads, scattering = {}, {}

    def scatter_start(n, g):
        if by_cols(n):
            src, mode = g, "scatter_cols"
        elif n in row_sharded:
            src, mode = g.reshape(N_DEV, -1, g.shape[1]), "scatter_slot"
        elif n == "w_in":
            src, mode = g, "scatter_slot"
        else:
            src, mode = jnp.transpose(g.reshape(g.shape[0], N_DEV, -1), (1, 0, 2)), "scatter_slot"
        scattering[n], token = _exchange_start(src, mode, dev1, name="rs_" + n)
        return token

    d_act = _mm(dh2_b, w_dn, tb=True, name="d_act")
    g_down = _mm(act, dh2_b, ta=True, out_dtype=BF16, name="g_w_down")
    tok = scatter_start("w_down", g_down)
    (dup_g, dwf_g, dbf_g), (dup_v, dwf_v, dbf_v) = _conv_bwd(
        _comb_glu, [up_g, up_v], f_w, [_after(f_b[0], tok), f_b[1]], d_act, dx_dtype=BF16, name="conv_ffn_bwd")
    dhn2 = _mm(dup_g, w_u, tb=True, b_win=(0, d_ff), name="d_hn2_g")
    dhn2 = _mm(dup_v, w_u, tb=True, b_win=(d_ff, d_ff), acc=dhn2, name="d_hn2_v")
    g_up = _mm(hn2, dup_g, ta=True, into=(lax.empty((d, 2 * d_ff), BF16), 0), name="g_w_up_g")
    g_up = _mm(hn2, dup_v, ta=True, into=(g_up, d_ff), name="g_w_up_v")
    tok = scatter_start("w_up", g_up)
    dh1, dh1_b, g_w2 = _blocked_bwd(_rms_fn, [h1], [_after(w2, tok)], [dhn2], [(F32, BF16)], adds={0: dh2}, tb=256,
                                    name="rms2_bwd")
    d_merged = _mm(dh1_b, w_o, tb=True, name="d_merged")
    g_out = _mm(merged, dh1_b, ta=True, out_dtype=BF16, name="g_w_out")
    tok = scatter_start("w_out", g_out)
    dglu_v, dglu_g, dga, dgb, dy_a = _blocked_bwd(
        _merge_fn, [glu_v, glu_g, pre["ga"], pre["gb"], y_a], [], [d_merged], [BF16] * 5, tb=128, name="merge_bwd")
    dyb = _mm(dglu_v, w_glu, tb=True, b_win=(0, d), name="d_yb_v")
    dyb = _mm(dglu_g, w_glu, tb=True, b_win=(d, d), acc=dyb, name="d_yb_g")
    g_glu = _mm(yb, dglu_v, ta=True, into=(lax.empty((d_s5, 2 * d), BF16), 0), name="g_w_glu_v")
    g_glu = _mm(yb, dglu_g, ta=True, into=(g_glu, d), name="g_w_glu_g")
    tok = tok + scatter_start("w_s5_glu", g_glu)
    ds_re, ds_im, du_skip, dc_blk_re, dc_blk_imn, dd3 = _blocked_bwd(
        _s5_out_fn, [s_re, s_im, u], [c_blk_re, c_blk_imn, _after(d3, tok)], [dyb], [F32, F32, F32], nj=n_oct, tb=512,
        name="s5_out_bwd")
    dbu_re, dbu_im, dlb_re, dlb_im = _s5_scan_bwd(s_re, s_im, ds_re, ds_im, lb_re, lb_im, name="s5_scan_bwd")
    du, db_blk_re, db_blk_im = _blocked_bwd(
        _s5_bu_fn, [u], [b_blk_re, b_blk_im], [dbu_re, dbu_im], [BF16], adds={0: du_skip}, nj=n_oct, tb=512,
        name="s5_bu_bwd")
    g_lre, g_lim, g_ldt, g_bt_re, g_bt_im = _s5_params(
        lam_re3, lam_im3, logdt3, bt_re, bt_im,
        cts=(dlb_re.reshape(gs, 1, ps), dlb_im.reshape(gs, 1, ps), undiag_b(db_blk_re), undiag_b(db_blk_im)),
        name="s5_params_bwd")
    grads["s5_lam_re"], grads["s5_lam_im"] = g_lre.reshape(s5_lam_re.shape), g_lim.reshape(s5_lam_im.shape)
    grads["s5_log_dt"] = g_ldt.reshape(s5_log_dt.shape)
    grads["s5_b_re"] = jnp.transpose(g_bt_re, (0, 2, 1)).reshape(s5_b_re.shape)
    grads["s5_b_im"] = jnp.transpose(g_bt_im, (0, 2, 1)).reshape(s5_b_im.shape)
    grads["s5_c_re"] = undiag_c(dc_blk_re).reshape(s5_c_re.shape)
    grads["s5_c_im"] = -undiag_c(dc_blk_imn).reshape(s5_c_im.shape)
    grads["s5_d"] = dd3.reshape(s5_d.shape)

    dyn = _mm(dy_a, w_proj, tb=True, name="d_yn")
    g_proj = _mm(yn, dy_a, ta=True, out_dtype=BF16, name="g_w_proj_a")
    tok = scatter_start("w_proj_a", g_proj)
    dxs, dbm, dcm, dz, ddtr3, g_dtb, g_alog, g_dsk, g_nw = _ssd_bwd(
        act_a["xs"], act_a["bm"], act_a["cm"], pre["z"], dtr3, hsave, dtb3, alog3, dsk3, _after(nw3, tok), dyn,
        name="ssd_bwd")
    grads["dt_bias"], grads["a_log"], grads["d_a"] = (g.reshape(1, nh) for g in (g_dtb, g_alog, g_dsk))
    grads["norm_a_w"] = g_nw.reshape(1, d_inner)
    dpre = {"z": dz, "u": du, "ga": dga, "gb": dgb}
    dcw, dcb = {}, {}
    for sn, dact in (("xs", dxs), ("bm", dbm), ("cm", dcm)):
        (dpre[sn], dcw[sn], dcb[sn]), = _conv_bwd(
            _comb_silu, [pre[sn]], [cw_a[:, a_cols[sn]]], [cb_a[:, a_cols[sn]]], dact, dx_dtype=BF16,
            name="conv_a_bwd_" + sn)
    dpre["dt"] = _pad_cols(jnp.transpose(ddtr3, (1, 0, 2)).reshape(t, nh), LANES).astype(BF16)
    g_in = _w_in_unpack({sn: _mm(hn1, dpre[sn], ta=True, name="g_w_in_" + sn) for sn in seg_names}, pieces,
                        w_in.shape[2], name="w_in_unpack")
    tok = scatter_start("w_in", g_in)
    dhn1 = _mm(_after(dpre["dt"], tok), w_in_p, tb=True, b_win=seg_at["dt"], name="d_hn1_dt")
    for sn in seg_names:
        if sn != "dt":
            dhn1 = _mm(dpre[sn], w_in_p, tb=True, b_win=seg_at[sn], acc=dhn1, name="d_hn1_" + sn)
    dx, g_w1 = _blocked_bwd(_rms_fn, [x2], [w1], [dhn1], [F32], adds={0: dh1}, tb=256, name="rms1_bwd")

    grads["norm_mix_w"], grads["norm_ffn_w"] = g_w1.reshape(1, d), g_w2.reshape(1, d)
    grads["norm_final_w"] = g_final.reshape(d)
    grads["conv_a_b"] = jnp.concatenate([dcb["xs"], dcb["bm"], dcb["cm"]], axis=1)
    grads["conv_ffn_b"] = jnp.concatenate([dbf_g, dbf_v], axis=1)
    g_cw_a = jnp.concatenate([dcw["xs"], dcw["bm"], dcw["cm"]], axis=1)
    g_cw_f = jnp.concatenate([dwf_g, dwf_v], axis=1)

    small = [grads[n] for n in replicated] + [g_cw_a, g_cw_f, loss_tile[:1, :1]]
    packed, sizes = _pack(small, PACK_COLS)
    summed = _sum_parts(_all_gather(packed, name="ag_small_grads")[0], name="sum_small_grads")
    *rep_sums, s_cw_a, s_cw_f, loss = _unpack(summed, sizes, [a.shape for a in small])
    for n, g in zip(replicated, rep_sums):
        grads[n] = g
    wa, wf = conv_a_w.shape[2], conv_ffn_w.shape[2]
    grads["conv_a_w"] = lax.dynamic_slice_in_dim(s_cw_a, dev * wa, wa, axis=1)[None]
    grads["conv_ffn_w"] = lax.dynamic_slice_in_dim(s_cw_f, dev * wf, wf, axis=1)[None]

    delta, new_m, new_v = {}, {}, {}
    done = dx
    for n in ("w_down", "w_up", "w_out", "w_s5_glu", "w_proj_a", "w_in"):
        land = _exchange_wait(scattering[n], done, name="rsw_" + n)
        shape = weights[n].shape
        two_d = lambda a: a.reshape(shape[-2], shape[-1])
        g = _sum_parts(land, name="rs_sum_" + n)
        grads[n] = g.reshape(shape)
        dl, nm, nv = _adamw(two_d(weights[n]), g, two_d(moms[n]), two_d(vars_[n]), name="adamw_" + n)
        delta[n], new_m[n], new_v[n] = dl.reshape(shape), nm.reshape(shape), nv.reshape(shape)
        done = dl
    for n in replicated + list(conv_sharded):
        shape = weights[n].shape
        two_d = lambda a: a.reshape(-1, shape[-1])
        dl, nm, nv = _adamw(two_d(weights[n]), two_d(grads[n]), two_d(moms[n]), two_d(vars_[n]), name="adamw_" + n)
        delta[n], new_m[n], new_v[n] = dl.reshape(shape), nm.reshape(shape), nv.reshape(shape)

    return (loss.reshape(()), dx.reshape(x.shape), *[grads[n] for n in names], *[delta[n] for n in names],
            *[new_m[n] for n in names], *[new_v[n] for n in names])
```

```python
import functools

import jax
import jax.numpy as jnp
from jax import lax
from jax.experimental import pallas as pl
from jax.experimental.pallas import tpu as pltpu

F32 = jnp.float32
BF16 = jnp.bfloat16
HIGHEST = lax.Precision.HIGHEST
MESH = pl.DeviceIdType.MESH

EPS = 1e-6
EIG_MAX = -1e-4
D_STATE = 128
CHUNK = 256
ADAM_LR = 0.001
ADAM_B1 = 0.9
ADAM_B2 = 0.999
ADAM_EPS = 1e-08
ADAM_WD = 0.01
ADAM_STEP = 10
N_DEV = 8
LANES = 128
SUBLANES = 8
VMEM_LIMIT = 56 * 1024 * 1024
MM_MAX_K = 4096


def _cp(*sem):
    return pltpu.CompilerParams(dimension_semantics=sem, vmem_limit_bytes=VMEM_LIMIT)


def _tile(dim, pref, unit=LANES):
    if dim <= unit:
        return dim
    t = (min(pref, dim) // unit) * unit
    while dim % t:
        t -= unit
    return t


_DIMS = {"nn": (((1,), (0,)), ((), ())), "nt": (((1,), (1,)), ((), ())), "tn": (((0,), (0,)), ((), ()))}


def _dot(a, b, kind):
    return lax.dot_general(a.astype(BF16), b.astype(BF16), _DIMS[kind], preferred_element_type=F32)


@functools.partial(jax.custom_vjp, nondiff_argnums=(2,))
def _bdot(a, b, kind):
    return _dot(a, b, kind)


def _bdot_fwd(a, b, kind):
    return _dot(a, b, kind), (a, b)


def _bdot_bwd(kind, res, g):
    a, b = res
    if kind == "nn":
        return _dot(g, b, "nt"), _dot(a, g, "tn")
    if kind == "nt":
        return _dot(g, b, "nn"), _dot(g, a, "tn")
    return _dot(b, g, "nt"), _dot(a, g, "nn")


_bdot.defvjp(_bdot_fwd, _bdot_bwd)


def _mm(a, b, *, ta=False, tb=False, acc=None, out_dtype=F32, name, b_win=None, into=None):
    assert not (ta and tb)
    m, k = (a.shape[1], a.shape[0]) if ta else a.shape
    b_off, b_size = b_win or (0, b.shape[1])
    n = b.shape[0] if tb else b_size
    assert (b_size if tb else b.shape[0]) == k, (a.shape, b.shape, ta, tb, b_win)
    o_off = into[1] if into else 0
    nk = -(-k // MM_MAX_K)
    while k % nk or (k // nk) % LANES or (tb and b_off % (k // nk)):
        nk += 1
    tk = k // nk
    tm, tn = _tile(m, 1024 if tk <= 2048 else 512), _tile(n, 1024)
    while o_off % tn or (not tb and b_off % tn):
        tn = _tile(n, tn - LANES)
    kind = "tn" if ta else ("nt" if tb else "nn")
    a_spec = pl.BlockSpec((tk, tm), lambda i, j, l: (l, i)) if ta else pl.BlockSpec((tm, tk), lambda i, j, l: (i, l))
    if tb:
        b_spec = pl.BlockSpec((tn, tk), lambda i, j, l: (j, l + b_off // tk))
    else:
        b_spec = pl.BlockSpec((tk, tn), lambda i, j, l: (l, j + b_off // tn))
    c_spec = pl.BlockSpec((tm, tn), lambda i, j, l: (i, j))
    o_spec = pl.BlockSpec((tm, tn), lambda i, j, l: (i, j + o_off // tn))
    has_acc = acc is not None

    def body(*refs):
        a_ref, b_ref = refs[:2]
        c_ref = refs[2] if has_acc else None
        o_ref = refs[2 + has_acc + (into is not None)]
        if nk == 1:
            res = _dot(a_ref[...], b_ref[...], kind)
            if has_acc:
                res = res + c_ref[...].astype(F32)
            o_ref[...] = res.astype(o_ref.dtype)
            return
        acc_ref = refs[-1]
        l = pl.program_id(2)

        @pl.when(l == 0)
        def _():
            if has_acc:
                acc_ref[...] = c_ref[...].astype(F32)
            else:
                acc_ref[...] = jnp.zeros_like(acc_ref)

        acc_ref[...] += _dot(a_ref[...], b_ref[...], kind)

        @pl.when(l == nk - 1)
        def _():
            o_ref[...] = acc_ref[...].astype(o_ref.dtype)

    ins = [a, b] + ([acc] if has_acc else []) + ([into[0]] if into else [])
    in_specs = [a_spec, b_spec] + ([c_spec] if has_acc else []) + ([pl.BlockSpec(memory_space=pl.ANY)] if into else [])
    out_shape = jax.ShapeDtypeStruct(into[0].shape, into[0].dtype) if into else jax.ShapeDtypeStruct((m, n), out_dtype)
    return pl.pallas_call(
        body, name=name, grid=(m // tm, n // tn, nk), in_specs=in_specs, out_specs=o_spec, out_shape=out_shape,
        input_output_aliases={len(ins) - 1: 0} if into else {},
        scratch_shapes=[pltpu.VMEM((tm, tn), F32)] if nk > 1 else [],
        compiler_params=_cp("parallel", "parallel", "arbitrary"),
    )(*ins)


def _w_in_pieces(seg_sizes, seg_order, n_blk):
    layout, o = {}, 0
    for sn in seg_order:
        width = -(-seg_sizes[sn] // LANES) * LANES
        layout[sn] = (o, width)
        o += width
    pieces, start = [], 0
    for sn, sz in seg_sizes.items():
        lo = start
        while lo < start + sz:
            blk = lo // n_blk
            hi = min(start + sz, (blk + 1) * n_blk)
            pieces.append((blk, lo - blk * n_blk, sn, lo - start, layout[sn][0] + lo - start, hi - lo))
            lo = hi
        start += sz
    return pieces, layout


def _w_in_pack(gathered, pieces, layout, seg_sizes, *, name, tr=256):
    _, k, n_blk = gathered.shape
    n_pad = sum(w for _, w in layout.values())

    def body(g_ref, o_ref):
        for sn, (off, width) in layout.items():
            if width != seg_sizes[sn]:
                o_ref[:, pl.ds(off + seg_sizes[sn], width - seg_sizes[sn])] = jnp.zeros(
                    (tr, width - seg_sizes[sn]), o_ref.dtype)
        for blk, src, _, _, dst, width in pieces:
            o_ref[:, pl.ds(dst, width)] = g_ref[blk, :, pl.ds(src, width)]

    return pl.pallas_call(
        body, name=name, grid=(k // tr,), in_specs=[pl.BlockSpec((N_DEV, tr, n_blk), lambda i: (0, i, 0))],
        out_specs=pl.BlockSpec((tr, n_pad), lambda i: (i, 0)), out_shape=jax.ShapeDtypeStruct((k, n_pad), gathered.dtype),
        compiler_params=_cp("parallel"),
    )(gathered)


def _w_in_unpack(seg_grads, pieces, n_blk, *, name, tr=128):
    names = list(seg_grads)
    k = seg_grads[names[0]].shape[0]

    def body(*refs):
        o_ref = refs[-1]
        seg_ref = dict(zip(names, refs))
        for blk, dst, sn, src, _, width in pieces:
            o_ref[blk, :, pl.ds(dst, width)] = seg_ref[sn][:, pl.ds(src, width)].astype(o_ref.dtype)

    return pl.pallas_call(
        body, name=name, grid=(k // tr,),
        in_specs=[pl.BlockSpec((tr, seg_grads[sn].shape[1]), lambda i: (i, 0)) for sn in names],
        out_specs=pl.BlockSpec((N_DEV, tr, n_blk), lambda i: (0, i, 0)),
        out_shape=jax.ShapeDtypeStruct((N_DEV, k, n_blk), BF16), compiler_params=_cp("parallel"),
    )(*[seg_grads[sn] for sn in names])


def _row_spec(arr, tb, nj):
    return pl.BlockSpec((tb, arr.shape[1] // nj), lambda j, i: (i, j))


def _par_spec(arr):
    return pl.BlockSpec((1,) + arr.shape[1:], lambda j, i: (j, 0, 0))


def _blocked_fwd(fn, rows, params, outs, *, nj=1, tb, name):
    t = rows[0].shape[0]
    nr, npar = len(rows), len(params)

    def body(*refs):
        res = fn(*[r[...] for r in refs[:nr]], *[p[0] for p in refs[nr:nr + npar]])
        for o_ref, val in zip(refs[nr + npar:], res):
            o_ref[...] = val.astype(o_ref.dtype)

    return pl.pallas_call(
        body, name=name, grid=(nj, t // tb),
        in_specs=[_row_spec(a, tb, nj) for a in rows] + [_par_spec(p) for p in params],
        out_specs=[pl.BlockSpec((tb, c // nj), lambda j, i: (i, j)) for c, _ in outs],
        out_shape=[jax.ShapeDtypeStruct((t, c), dt) for c, dt in outs],
        compiler_params=_cp("parallel", "arbitrary"),
    )(*rows, *params)


def _blocked_bwd(fn, rows, params, cts, row_grad_dtypes, *, adds=None, nj=1, tb, name):
    t = rows[0].shape[0]
    nr, npar, nct = len(rows), len(params), len(cts)
    adds = adds or {}
    add_keys = sorted(adds)
    want, want_dtypes = [], []
    for k, dts in enumerate(row_grad_dtypes):
        for dt in (dts if isinstance(dts, tuple) else (dts,)):
            if dt is not None:
                want.append(k)
                want_dtypes.append(dt)

    def body(*refs):
        row_refs = refs[:nr]
        par_refs = refs[nr:nr + npar]
        ct_refs = refs[nr + npar:nr + npar + nct]
        add_refs = dict(zip(add_keys, refs[nr + npar + nct:nr + npar + nct + len(add_keys)]))
        out_refs = refs[nr + npar + nct + len(add_keys):]
        _, vjp = jax.vjp(fn, *[r[...] for r in row_refs], *[p[0] for p in par_refs])
        grads = vjp(tuple(c[...].astype(F32) for c in ct_refs))
        for o_ref, k in zip(out_refs, want):
            g = grads[k]
            if k in add_refs:
                g = g + add_refs[k][...].astype(F32)
            o_ref[...] = g.astype(o_ref.dtype)
        first = pl.program_id(1) == 0
        for o_ref, g in zip(out_refs[len(want):], grads[nr:]):
            @pl.when(first)
            def _(o_ref=o_ref):
                o_ref[...] = jnp.zeros_like(o_ref)
            o_ref[0] += g

    add_arrs = [adds[k] for k in add_keys]
    return pl.pallas_call(
        body, name=name, grid=(nj, t // tb),
        in_specs=[_row_spec(a, tb, nj) for a in rows] + [_par_spec(p) for p in params]
        + [_row_spec(c, tb, nj) for c in cts] + [_row_spec(a, tb, nj) for a in add_arrs],
        out_specs=[_row_spec(rows[k], tb, nj) for k in want] + [_par_spec(p) for p in params],
        out_shape=[jax.ShapeDtypeStruct(rows[k].shape, dt) for k, dt in zip(want, want_dtypes)]
        + [jax.ShapeDtypeStruct(p.shape, F32) for p in params],
        compiler_params=_cp("parallel", "arbitrary"),
    )(*rows, *params, *cts, *add_arrs)


def _rms_fn(x, w):
    return (x * lax.rsqrt(jnp.mean(x * x, axis=-1, keepdims=True) + EPS) * w,)


def _silu(x):
    return x * jax.nn.sigmoid(x)


def _merge_fn(glu_v, glu_g, g_a, g_b, y_a):
    y_b = glu_v * jax.nn.sigmoid(glu_g)
    return (jax.nn.sigmoid(g_a) * y_a + jax.nn.sigmoid(g_b) * y_b,)


def _s5_bu_fn(u, b_re, b_im):
    return _bdot(u, b_re, "nn"), _bdot(u, b_im, "nn")


def _s5_out_fn(s_re, s_im, u, c_re, c_im_neg, d):
    return (jax.nn.gelu(_bdot(s_re, c_re, "nn") + _bdot(s_im, c_im_neg, "nn") + d * u),)


HALO = SUBLANES


def _conv_fn(comb, kw, ns):
    def fn(*args):
        cs = []
        for s in range(ns):
            xp, xm, w, b = args[4 * s:4 * s + 4]
            xe = jnp.concatenate([xp, xm], axis=0)
            tb = xm.shape[0]
            y = b
            for k in range(kw):
                off = HALO - kw + 1 + k
                y = y + w[k:k + 1, :] * xe[off:off + tb, :]
            cs.append(y)
        return comb(*cs)
    return fn


def _conv_specs(xs, ws, bs, tb, cb, time_of):
    specs = []
    for x, w, b in zip(xs, ws, bs):
        specs += [
            pl.BlockSpec((HALO, cb), lambda j, i: (jnp.maximum(time_of(i) * (tb // HALO) - 1, 0), j)),
            pl.BlockSpec((tb, cb), lambda j, i: (time_of(i), j)),
            pl.BlockSpec((w.shape[0], cb), lambda j, i: (0, j)),
            pl.BlockSpec((1, cb), lambda j, i: (0, j)),
        ]
    return specs


def _conv_fwd(comb, xs, ws, bs, *, out_dtype, name, tb=512):
    t, c = xs[0].shape
    cb = _tile(c, 512)
    ns = len(xs)
    fn = _conv_fn(comb, ws[0].shape[0], ns)

    def body(*refs):
        i = pl.program_id(1)
        args = []
        for s in range(ns):
            xp_ref, xm_ref, w_ref, b_ref = refs[4 * s:4 * s + 4]
            xp = jnp.where(i == 0, 0.0, xp_ref[...])
            args += [xp, xm_ref[...], w_ref[...], b_ref[...]]
        refs[4 * ns][...] = fn(*args).astype(out_dtype)

    flat = [a for x, w, b in zip(xs, ws, bs) for a in (x, x, w, b)]
    return pl.pallas_call(
        body, name=name, grid=(c // cb, t // tb),
        in_specs=_conv_specs(xs, ws, bs, tb, cb, lambda i: i),
        out_specs=pl.BlockSpec((tb, cb), lambda j, i: (i, j)),
        out_shape=jax.ShapeDtypeStruct((t, c), out_dtype),
        compiler_params=_cp("parallel", "arbitrary"),
    )(*flat)


def _conv_bwd(comb, xs, ws, bs, dy, *, dx_dtype, name, tb=512):
    t, c = xs[0].shape
    cb = _tile(c, 512)
    ns = len(xs)
    nt = t // tb
    fn = _conv_fn(comb, ws[0].shape[0], ns)

    def body(*refs):
        step = pl.program_id(1)
        in_refs = refs[:4 * ns]
        dy_ref = refs[4 * ns]
        out_refs = refs[4 * ns + 1:4 * ns + 1 + 3 * ns]
        carry_refs = refs[4 * ns + 1 + 3 * ns:]
        args = []
        for s in range(ns):
            xp_ref, xm_ref, w_ref, b_ref = in_refs[4 * s:4 * s + 4]
            xp = jnp.where(step == nt - 1, 0.0, xp_ref[...])
            args += [xp, xm_ref[...], w_ref[...], b_ref[...]]
        _, vjp = jax.vjp(fn, *args)
        grads = vjp(dy_ref[...].astype(F32))
        for s in range(ns):
            dxp, dxm, dw, db = grads[4 * s:4 * s + 4]
            dx_ref, dw_ref, db_ref = out_refs[3 * s:3 * s + 3]
            carry = carry_refs[s]

            @pl.when(step == 0)
            def _(carry=carry, dw_ref=dw_ref, db_ref=db_ref):
                carry[...] = jnp.zeros_like(carry)
                dw_ref[...] = jnp.zeros_like(dw_ref)
                db_ref[...] = jnp.zeros_like(db_ref)

            tail = jnp.concatenate([jnp.zeros((tb - HALO, cb), F32), carry[...]], axis=0)
            dx_ref[...] = (dxm + tail).astype(dx_dtype)
            carry[...] = dxp
            dw_ref[...] += dw
            db_ref[...] += db

    flat = [a for x, w, b in zip(xs, ws, bs) for a in (x, x, w, b)]
    rev = lambda i: nt - 1 - i
    out_specs, out_shape = [], []
    for x, w, b in zip(xs, ws, bs):
        out_specs += [pl.BlockSpec((tb, cb), lambda j, i: (rev(i), j)),
                      pl.BlockSpec((w.shape[0], cb), lambda j, i: (0, j)),
                      pl.BlockSpec((1, cb), lambda j, i: (0, j))]
        out_shape += [jax.ShapeDtypeStruct((t, c), dx_dtype), jax.ShapeDtypeStruct(w.shape, F32),
                      jax.ShapeDtypeStruct(b.shape, F32)]
    res = pl.pallas_call(
        body, name=name, grid=(c // cb, nt),
        in_specs=_conv_specs(xs, ws, bs, tb, cb, rev) + [pl.BlockSpec((tb, cb), lambda j, i: (rev(i), j))],
        out_specs=out_specs, out_shape=out_shape,
        scratch_shapes=[pltpu.VMEM((HALO, cb), F32) for _ in range(ns)],
        compiler_params=_cp("parallel", "arbitrary"),
    )(*flat, dy)
    return [tuple(res[3 * s:3 * s + 3]) for s in range(ns)]


def _comb_silu(c):
    return _silu(c)


def _comb_glu(cg, cv):
    return _silu(cg) * cv


def _ssd_fn(nheads, hdim):
    def fn(x, bm, cm, z, dtr, hin, dtb, alog, dsk, nw):
        q = x.shape[0]
        dt = jax.nn.softplus(dtr + dtb)
        da = dt * (-jnp.exp(alog))
        li = lax.broadcasted_iota(jnp.int32, (q, q), 0)
        si = lax.broadcasted_iota(jnp.int32, (q, q), 1)
        causal = li >= si
        tri = causal.astype(F32)
        acs = jnp.dot(tri, da, precision=HIGHEST, preferred_element_type=F32)
        acs_row = lax.dot_general(da, tri, (((0,), (1,)), ((), ())), precision=HIGHEST,
                                  preferred_element_type=F32)
        cb = _bdot(cm, bm, "nt")
        ch = _bdot(cm, hin, "nn")
        ys, hs = [], []
        for r in range(nheads):
            cols = slice(r * hdim, (r + 1) * hdim)
            xr = x[:, cols]
            a_col = acs[:, r:r + 1]
            decay = jnp.exp(jnp.where(causal, a_col - acs_row[r:r + 1, :], -1e30))
            xd = xr * dt[:, r:r + 1]
            y_diag = _bdot(cb * decay, xd, "nn")
            y_off = ch[:, cols] * jnp.exp(a_col)
            last = acs[q - 1:q, r:r + 1]
            st = _bdot(bm * jnp.exp(last - a_col), xd, "tn")
            hs.append(jnp.exp(last) * hin[:, cols] + st)
            ys.append(y_diag + y_off + dsk[:, r:r + 1] * xr)
        y = jnp.concatenate(ys, axis=1) * _silu(z)
        yn = y * lax.rsqrt(jnp.mean(y * y, axis=-1, keepdims=True) + EPS) * nw
        return yn, jnp.concatenate(hs, axis=1)
    return fn


def _ssd_specs(rp, nr, time_of):
    row = lambda w: pl.BlockSpec((CHUNK, w), lambda g, c: (time_of(c), g))
    par = lambda w: pl.BlockSpec((1, 1, w), lambda g, c: (g, 0, 0))
    return dict(
        x=row(rp), bc=row(D_STATE), dtr=pl.BlockSpec((1, CHUNK, nr), lambda g, c: (g, time_of(c), 0)),
        h=pl.BlockSpec((1, 1, D_STATE, rp), lambda g, c: (g, time_of(c), 0, 0)), pr=par(nr), pw=par(rp))


def _ssd_fwd(xs, bm, cm, z, dtr, dtb, alog, dsk, nw, *, name):
    t = xs.shape[0]
    g, _, nr = dtr.shape
    rp = xs.shape[1] // g
    nc = t // CHUNK
    fn = _ssd_fn(nr, rp // nr)
    sp = _ssd_specs(rp, nr, lambda c: c)

    def body(x_ref, b_ref, c_ref, z_ref, dtr_ref, dtb_ref, al_ref, dsk_ref, nw_ref, yn_ref, hs_ref, h_ref):
        @pl.when(pl.program_id(1) == 0)
        def _():
            h_ref[...] = jnp.zeros_like(h_ref)
        hin = h_ref[...]
        hs_ref[0, 0] = hin
        yn, hout = fn(x_ref[...], b_ref[...], c_ref[...], z_ref[...], dtr_ref[0], hin,
                      dtb_ref[0], al_ref[0], dsk_ref[0], nw_ref[0])
        yn_ref[...] = yn.astype(yn_ref.dtype)
        h_ref[...] = hout

    return pl.pallas_call(
        body, name=name, grid=(g, nc),
        in_specs=[sp["x"], sp["bc"], sp["bc"], sp["x"], sp["dtr"], sp["pr"], sp["pr"], sp["pr"], sp["pw"]],
        out_specs=[sp["x"], sp["h"]],
        out_shape=[jax.ShapeDtypeStruct(xs.shape, BF16), jax.ShapeDtypeStruct((g, nc, D_STATE, rp), F32)],
        scratch_shapes=[pltpu.VMEM((D_STATE, rp), F32)],
        compiler_params=_cp("parallel", "arbitrary"),
    )(xs, bm, cm, z, dtr, dtb, alog, dsk, nw)


def _ssd_bwd(xs, bm, cm, z, dtr, hsave, dtb, alog, dsk, nw, dyn, *, name):
    t = xs.shape[0]
    g, _, nr = dtr.shape
    rp = xs.shape[1] // g
    nc = t // CHUNK
    fn = _ssd_fn(nr, rp // nr)
    sp = _ssd_specs(rp, nr, lambda c: nc - 1 - c)

    def body(x_ref, b_ref, c_ref, z_ref, dtr_ref, hs_ref, dtb_ref, al_ref, dsk_ref, nw_ref, dyn_ref,
             dx_ref, db_ref, dc_ref, dz_ref, ddtr_ref, ddtb_ref, dal_ref, ddsk_ref, dnw_ref, dh_ref):
        first = pl.program_id(1) == 0

        @pl.when(first)
        def _():
            dh_ref[...] = jnp.zeros_like(dh_ref)
            for r in (ddtb_ref, dal_ref, ddsk_ref, dnw_ref):
                r[...] = jnp.zeros_like(r)

        _, vjp = jax.vjp(fn, x_ref[...], b_ref[...], c_ref[...], z_ref[...], dtr_ref[0], hs_ref[0, 0],
                         dtb_ref[0], al_ref[0], dsk_ref[0], nw_ref[0])
        dx, db, dc, dz, ddtr, dhin, ddtb, dal, ddsk, dnw = vjp((dyn_ref[...].astype(F32), dh_ref[...]))
        dx_ref[...] = dx
        db_ref[...] = db
        dc_ref[...] = dc
        dz_ref[...] = dz.astype(dz_ref.dtype)
        ddtr_ref[0] = ddtr
        dh_ref[...] = dhin
        ddtb_ref[0] += ddtb
        dal_ref[0] += dal
        ddsk_ref[0] += ddsk
        dnw_ref[0] += dnw

    sd = jax.ShapeDtypeStruct
    return pl.pallas_call(
        body, name=name, grid=(g, nc),
        in_specs=[sp["x"], sp["bc"], sp["bc"], sp["x"], sp["dtr"], sp["h"], sp["pr"], sp["pr"], sp["pr"], sp["pw"],
                  sp["x"]],
        out_specs=[sp["x"], sp["bc"], sp["bc"], sp["x"], sp["dtr"], sp["pr"], sp["pr"], sp["pr"], sp["pw"]],
        out_shape=[sd(xs.shape, F32), sd(bm.shape, F32), sd(cm.shape, F32), sd(z.shape, BF16), sd(dtr.shape, F32),
                   sd(dtb.shape, F32), sd(alog.shape, F32), sd(dsk.shape, F32), sd(nw.shape, F32)],
        scratch_shapes=[pltpu.VMEM((D_STATE, rp), F32)],
        compiler_params=_cp("parallel", "arbitrary"),
    )(xs, bm, cm, z, dtr, hsave, dtb, alog, dsk, nw, dyn)


def _s5_param_fn(lam_re, lam_im, log_dt, bt_re, bt_im):
    lr = jnp.minimum(lam_re, EIG_MAX)
    dt = jnp.exp(log_dt)
    mag = jnp.exp(lr * dt)
    lb_re = mag * jnp.cos(lam_im * dt)
    lb_im = mag * jnp.sin(lam_im * dt)
    n_re = lb_re - 1.0
    den = lr * lr + lam_im * lam_im
    k_re = (n_re * lr + lb_im * lam_im) / den
    k_im = (lb_im * lr - n_re * lam_im) / den
    return lb_re, lb_im, k_re * bt_re - k_im * bt_im, k_re * bt_im + k_im * bt_re


def _s5_params(lam_re, lam_im, log_dt, bt_re, bt_im, cts=None, *, name):
    args = (lam_re, lam_im, log_dt, bt_re, bt_im)
    n = len(args)

    def body(*refs):
        vals = [r[...] for r in refs[:n]]
        if cts is None:
            res = _s5_param_fn(*vals)
        else:
            _, vjp = jax.vjp(_s5_param_fn, *vals)
            res = vjp(tuple(r[...] for r in refs[n:n + 4]))
        for o_ref, v in zip(refs[-len(res):], res):
            o_ref[...] = v

    if cts is None:
        out = [lam_re, lam_im, bt_re, bt_im]
        ins = args
    else:
        out = list(args)
        ins = args + tuple(cts)
    return pl.pallas_call(
        body, name=name, out_shape=[jax.ShapeDtypeStruct(a.shape, F32) for a in out],
        compiler_params=pltpu.CompilerParams(vmem_limit_bytes=VMEM_LIMIT),
    )(*ins)


SCAN_COLS = 512


def _cmul(xr, xi, yr, yi):
    return xr * yr - xi * yi, xr * yi + xi * yr


def _scan_consts(a_re, a_im, cols, reverse):
    shape = (SUBLANES, cols)
    row = lax.broadcasted_iota(jnp.int32, shape, 0)
    dist = (SUBLANES - 1 - row) if reverse else row
    mr, mi = jnp.broadcast_to(a_re, shape), jnp.broadcast_to(a_im, shape)
    pr, pi = mr, mi
    mults = []
    for d in (1, 2, 4):
        mults.append((mr, mi))
        qr, qi = _cmul(pr, pi, mr, mi)
        has_bit = (dist & d) != 0
        pr, pi = jnp.where(has_bit, qr, pr), jnp.where(has_bit, qi, pi)
        mr, mi = _cmul(mr, mi, mr, mi)
    return mults, (pr, pi), dist


def _scan_group(xr, xi, consts, cr, ci, reverse):
    mults, (pr, pi), dist = consts
    for d, (mr, mi) in zip((1, 2, 4), mults):
        shift = (SUBLANES - d) if reverse else d
        sr = jnp.where(dist >= d, pltpu.roll(xr, shift, 0), 0.0)
        si = jnp.where(dist >= d, pltpu.roll(xi, shift, 0), 0.0)
        tr, ti = _cmul(mr, mi, sr, si)
        xr, xi = xr + tr, xi + ti
    last = slice(0, 1) if reverse else slice(SUBLANES - 1, SUBLANES)
    nr, ni = _cmul(pr[last], pi[last], cr, ci)
    tr, ti = _cmul(pr, pi, jnp.broadcast_to(cr, xr.shape), jnp.broadcast_to(ci, xr.shape))
    return xr + tr, xi + ti, xr[last] + nr, xi[last] + ni


def _scan_specs(tb, time_of):
    row = pl.BlockSpec((tb, SCAN_COLS), lambda j, i: (time_of(i), j))
    par = pl.BlockSpec((1, SCAN_COLS), lambda j, i: (0, j))
    return row, par


def _s5_scan_fwd(bu_re, bu_im, lb_re, lb_im, *, name, tb=512):
    t, c = bu_re.shape
    nj = c // SCAN_COLS
    row, par = _scan_specs(tb, lambda i: i)

    def body(bre_ref, bim_ref, lre_ref, lim_ref, sre_ref, sim_ref, cre_ref, cim_ref):
        @pl.when(pl.program_id(1) == 0)
        def _():
            cre_ref[...] = jnp.zeros_like(cre_ref)
            cim_ref[...] = jnp.zeros_like(cim_ref)
        consts = _scan_consts(lre_ref[...], lim_ref[...], SCAN_COLS, False)

        def group(k, carry):
            rows = pl.ds(pl.multiple_of(k * SUBLANES, SUBLANES), SUBLANES)
            sr, si, cr, ci = _scan_group(bre_ref[rows, :], bim_ref[rows, :], consts, *carry, False)
            sre_ref[rows, :] = sr
            sim_ref[rows, :] = si
            return cr, ci

        sr, si = lax.fori_loop(0, tb // SUBLANES, group, (cre_ref[...], cim_ref[...]), unroll=4)
        cre_ref[...] = sr
        cim_ref[...] = si

    return pl.pallas_call(
        body, name=name, grid=(nj, t // tb), in_specs=[row, row, par, par], out_specs=[row, row],
        out_shape=[jax.ShapeDtypeStruct((t, c), F32)] * 2,
        scratch_shapes=[pltpu.VMEM((1, SCAN_COLS), F32)] * 2,
        compiler_params=_cp("parallel", "arbitrary"),
    )(bu_re, bu_im, lb_re, lb_im)


def _s5_scan_bwd(s_re, s_im, ds_re, ds_im, lb_re, lb_im, *, name, tb=512):
    t, c = s_re.shape
    nj = c // SCAN_COLS
    nt = t // tb
    rev = lambda i: nt - 1 - i
    row, par = _scan_specs(tb, rev)
    prev = pl.BlockSpec((HALO, SCAN_COLS), lambda j, i: (jnp.maximum(rev(i) * (tb // HALO) - 1, 0), j))

    def body(sre_ref, sim_ref, pre_ref, pim_ref, dre_ref, dim_ref, lre_ref, lim_ref,
             gre_ref, gim_ref, dlre_ref, dlim_ref, cre_ref, cim_ref, ext_re, ext_im):
        step_id = pl.program_id(1)

        @pl.when(step_id == 0)
        def _():
            cre_ref[...] = jnp.zeros_like(cre_ref)
            cim_ref[...] = jnp.zeros_like(cim_ref)
            dlre_ref[...] = jnp.zeros_like(dlre_ref)
            dlim_ref[...] = jnp.zeros_like(dlim_ref)
        consts = _scan_consts(lre_ref[...], -lim_ref[...], SCAN_COLS, True)
        ngroups = tb // SUBLANES

        def group(k, carry):
            rows = pl.ds(pl.multiple_of((ngroups - 1 - k) * SUBLANES, SUBLANES), SUBLANES)
            gr, gi, cr, ci = _scan_group(dre_ref[rows, :], dim_ref[rows, :], consts, *carry, True)
            gre_ref[rows, :] = gr
            gim_ref[rows, :] = gi
            return cr, ci

        gr, gi = lax.fori_loop(0, ngroups, group, (cre_ref[...], cim_ref[...]), unroll=4)
        cre_ref[...] = gr
        cim_ref[...] = gi
        has_past = step_id != nt - 1
        ext_re[pl.ds(0, HALO), :] = jnp.where(has_past, pre_ref[...], 0.0)
        ext_im[pl.ds(0, HALO), :] = jnp.where(has_past, pim_ref[...], 0.0)
        ext_re[pl.ds(HALO, tb), :] = sre_ref[...]
        ext_im[pl.ds(HALO, tb), :] = sim_ref[...]
        pr, pi = ext_re[pl.ds(HALO - 1, tb), :], ext_im[pl.ds(HALO - 1, tb), :]
        g_re, g_im = gre_ref[...], gim_ref[...]
        dlre_ref[...] += jnp.sum(pr * g_re + pi * g_im, axis=0, keepdims=True)
        dlim_ref[...] += jnp.sum(pr * g_im - pi * g_re, axis=0, keepdims=True)

    return pl.pallas_call(
        body, name=name, grid=(nj, nt),
        in_specs=[row, row, prev, prev, row, row, par, par], out_specs=[row, row, par, par],
        out_shape=[jax.ShapeDtypeStruct((t, c), F32)] * 2 + [jax.ShapeDtypeStruct((1, c), F32)] * 2,
        scratch_shapes=[pltpu.VMEM((1, SCAN_COLS), F32)] * 2 + [pltpu.VMEM((HALO + tb, SCAN_COLS), F32)] * 2,
        compiler_params=_cp("parallel", "arbitrary"),
    )(s_re, s_im, s_re, s_im, ds_re, ds_im, lb_re, lb_im)


def _loss_fn(h, w, tgt):
    err = _rms_fn(h, w)[0] - tgt
    return 0.5 * jnp.sum(jnp.mean(err * err, axis=-1, keepdims=True), axis=0, keepdims=True)


def _loss_head(h, w, tgt, *, name, tb=256):
    t, d = h.shape

    def body(h_ref, w_ref, t_ref, loss_ref, dh_ref, dhb_ref, dw_ref):
        @pl.when(pl.program_id(0) == 0)
        def _():
            loss_ref[...] = jnp.zeros_like(loss_ref)
            dw_ref[...] = jnp.zeros_like(dw_ref)
        part, vjp = jax.vjp(_loss_fn, h_ref[...], w_ref[...], t_ref[...])
        dh, dw, _ = vjp(jnp.ones((1, 1), F32))
        loss_ref[...] += jnp.broadcast_to(part, loss_ref.shape)
        dh_ref[...] = dh
        dhb_ref[...] = dh.astype(BF16)
        dw_ref[...] += dw

    row = pl.BlockSpec((tb, d), lambda i: (i, 0))
    par = pl.BlockSpec((1, d), lambda i: (0, 0))
    return pl.pallas_call(
        body, name=name, grid=(t // tb,), in_specs=[row, par, row],
        out_specs=[pl.BlockSpec((SUBLANES, LANES), lambda i: (0, 0)), row, row, par],
        out_shape=[jax.ShapeDtypeStruct((SUBLANES, LANES), F32), jax.ShapeDtypeStruct((t, d), F32),
                   jax.ShapeDtypeStruct((t, d), BF16), jax.ShapeDtypeStruct((1, d), F32)],
        compiler_params=_cp("arbitrary"),
    )(h, w, tgt)


def _adamw(w, g, m, v, *, name):
    r, c = w.shape
    tr = _tile(r, 256, SUBLANES)

    def body(w_ref, g_ref, m_ref, v_ref, d_ref, nm_ref, nv_ref):
        g = g_ref[...]
        nm = ADAM_B1 * m_ref[...] + (1.0 - ADAM_B1) * g
        nv = ADAM_B2 * v_ref[...] + (1.0 - ADAM_B2) * (g * g)
        m_hat = nm / (1.0 - ADAM_B1 ** ADAM_STEP)
        v_hat = nv / (1.0 - ADAM_B2 ** ADAM_STEP)
        d_ref[...] = -ADAM_LR * (m_hat / (jnp.sqrt(v_hat) + ADAM_EPS) + ADAM_WD * w_ref[...])
        nm_ref[...] = nm
        nv_ref[...] = nv

    spec = pl.BlockSpec((tr, c), lambda i: (i, 0))
    return pl.pallas_call(
        body, name=name, grid=(r // tr,), in_specs=[spec] * 4, out_specs=[spec] * 3,
        out_shape=[jax.ShapeDtypeStruct((r, c), F32)] * 3, compiler_params=_cp("parallel"),
    )(w, g, m, v)


def _sum_parts(parts, *, name):
    _, r, c = parts.shape
    tr = _tile(r, 128, SUBLANES)

    def body(p_ref, o_ref):
        acc = p_ref[0].astype(F32)
        for k in range(1, N_DEV):
            acc = acc + p_ref[k].astype(F32)
        o_ref[...] = acc

    return pl.pallas_call(
        body, name=name, grid=(r // tr,), in_specs=[pl.BlockSpec((N_DEV, tr, c), lambda i: (0, i, 0))],
        out_specs=pl.BlockSpec((tr, c), lambda i: (i, 0)), out_shape=jax.ShapeDtypeStruct((r, c), F32),
        compiler_params=_cp("parallel"),
    )(parts)


def _position():
    return lax.axis_index("x"), lax.axis_index("y"), lax.axis_index("c")


def _flat(px, py, pc):
    return 4 * px + 2 * py + pc


def _all_gather(shard, *, name):
    def body(x_ref, out_ref, token, send_sems, recv_sems, local_sem):
        token[...] = jnp.zeros_like(token)
        x, y, c = _position()
        me, sibling = (x, y, c), (x, y, 1 - c)
        chips = [(1 - x, y), (x, 1 - y), (1 - x, 1 - y)]

        def copy(k, block, to, src=None):
            slot = out_ref.at[_flat(*block)]
            return pltpu.make_async_remote_copy(
                src_ref=slot if src is None else src, dst_ref=slot, send_sem=send_sems.at[k],
                recv_sem=recv_sems.at[k], device_id=to, device_id_type=MESH)

        mine = pltpu.make_async_copy(x_ref, out_ref.at[_flat(*me)], local_sem)
        mine.start()
        first = [copy(0, me, sibling, src=x_ref)]
        first += [copy(1 + j, me, (*chip, c), src=x_ref) for j, chip in enumerate(chips)]
        for cp in first:
            cp.start()
        passed = [copy(4 + j, (*chip, c), sibling) for j, chip in enumerate(chips)]
        for j, chip in enumerate(chips):
            copy(1 + j, (*chip, c), me).wait_recv()
            passed[j].start()
        copy(0, sibling, me).wait_recv()
        for j, chip in enumerate(chips):
            copy(4 + j, (*chip, 1 - c), me).wait_recv()
        for cp in first + passed:
            cp.wait_send()
        mine.wait()

    return pl.pallas_call(
        body, name=name,
        out_shape=(jax.ShapeDtypeStruct((N_DEV,) + shard.shape, shard.dtype),
                   jax.ShapeDtypeStruct((SUBLANES, LANES), F32)),
        in_specs=[pl.BlockSpec(memory_space=pl.ANY)],
        out_specs=(pl.BlockSpec(memory_space=pl.ANY), pl.BlockSpec(memory_space=pltpu.VMEM)),
        scratch_shapes=[pltpu.SemaphoreType.DMA((7,)), pltpu.SemaphoreType.DMA((7,)), pltpu.SemaphoreType.DMA(())],
    )(shard)


_HBM = pl.BlockSpec(memory_space=pltpu.HBM)
_SEM = pl.BlockSpec(memory_space=pltpu.SEMAPHORE)
_EFFECT = pltpu.SideEffectType.DATAFLOW_SIDE_EFFECTING


def _copy_ends(src_ref, land_ref, mode, me, to):
    if mode == "gather_slot":
        return src_ref, land_ref.at[me]
    if mode == "gather_cols":
        w = src_ref.shape[1]
        return src_ref, land_ref.at[:, pl.ds(pl.multiple_of(me * w, LANES), w)]
    if mode == "scatter_slot":
        return src_ref.at[to], land_ref.at[me]
    w = land_ref.shape[2]
    return src_ref.at[:, pl.ds(pl.multiple_of(to * w, LANES), w)], land_ref.at[me]


BF16_ROWS = 16


def _land_shape(src, mode):
    if mode == "gather_slot":
        return (N_DEV,) + src.shape
    if mode == "gather_cols":
        return (src.shape[0], N_DEV * src.shape[1])
    if mode == "scatter_slot":
        return src.shape
    return (N_DEV, src.shape[0], src.shape[1] // N_DEV)


def _exchange_copies(src_ref, land_ref, send_sems, recv_sems, mode):
    x, y, c = _position()
    me = _flat(x, y, c)
    copies = []
    for k in range(1, N_DEV):
        peer = (x ^ ((k >> 2) & 1), y ^ ((k >> 1) & 1), c ^ (k & 1))
        src, dst = _copy_ends(src_ref, land_ref, mode, me, _flat(*peer))
        copies.append(pltpu.make_async_remote_copy(
            src_ref=src, dst_ref=dst, send_sem=send_sems.at[k - 1], recv_sem=recv_sems.at[k - 1],
            device_id=peer, device_id_type=MESH))
    return copies


def _place_own(src, mode, dev, *, name):
    rows = src.shape[1] if mode == "scatter_slot" else src.shape[0]
    tr = _tile(rows, 512, BF16_ROWS)
    land = _land_shape(src, mode)
    width = land[-1] if mode.startswith("scatter") else src.shape[1]
    slot = pl.BlockSpec((1, tr, width), lambda i, d: (d[0], i, 0))
    cols = pl.BlockSpec((tr, width), lambda i, d: (i, d[0]))
    whole = pl.BlockSpec((tr, width), lambda i, d: (i, 0))
    in_spec, out_spec = {"gather_slot": (whole, slot), "gather_cols": (whole, cols), "scatter_slot": (slot, slot),
                         "scatter_cols": (cols, slot)}[mode]

    def body(dev_ref, src_ref, land_ref):
        land_ref[...] = src_ref[...].reshape(land_ref.shape)

    return pl.pallas_call(
        body, name=name, out_shape=jax.ShapeDtypeStruct(land, src.dtype),
        grid_spec=pltpu.PrefetchScalarGridSpec(num_scalar_prefetch=1, grid=(rows // tr,), in_specs=[in_spec],
                                               out_specs=out_spec),
        compiler_params=_cp("parallel"),
    )(dev, src)


def _exchange_start(src, mode, dev, *, name):
    land = _place_own(src, mode, dev, name=name + "_own")
    n_copies = N_DEV - 1

    def body(src_ref, land_ref, send_sems, recv_sems, src_thru, land_thru, token):
        for cp in _exchange_copies(src_ref, land_ref, send_sems, recv_sems, mode):
            cp.start()
        token[...] = jnp.zeros_like(token)

    hbm = pltpu.with_memory_space_constraint
    *handle, token = pl.pallas_call(
        body, name=name,
        out_shape=(pltpu.SemaphoreType.DMA((n_copies,)), pltpu.SemaphoreType.DMA((n_copies,)),
                   pltpu.HBM(src.shape, src.dtype), pltpu.HBM(land.shape, land.dtype),
                   jax.ShapeDtypeStruct((SUBLANES, LANES), F32)),
        in_specs=(_HBM, _HBM), out_specs=(_SEM, _SEM, _HBM, _HBM, pl.BlockSpec(memory_space=pltpu.VMEM)),
        input_output_aliases={0: 2, 1: 3}, compiler_params=pltpu.CompilerParams(has_side_effects=_EFFECT),
    )(hbm(src, pltpu.HBM), hbm(land, pltpu.HBM))
    return (tuple(handle), mode), token


def _exchange_wait(pending, after, *, name):
    (send_sems, recv_sems, src_thru, land_thru), mode = pending

    def body(src_ref, land_ref, send_sems, recv_sems, after_ref, src_dead, got_ref):
        for cp in _exchange_copies(src_ref, land_ref, send_sems, recv_sems, mode):
            cp.wait_send()
            cp.wait_recv()

    return pl.pallas_call(
        body, name=name, out_shape=(pltpu.HBM(src_thru.shape, src_thru.dtype), pltpu.HBM(land_thru.shape, land_thru.dtype)),
        in_specs=(_HBM, _HBM, _SEM, _SEM, pl.BlockSpec(memory_space=pl.ANY)), out_specs=(_HBM, _HBM),
        input_output_aliases={0: 0, 1: 1}, compiler_params=pltpu.CompilerParams(has_side_effects=_EFFECT),
    )(src_thru, land_thru, send_sems, recv_sems, after)[1]


def _after(x, token):
    return x + token[0, 0].astype(x.dtype)


def _pad_cols(a, mult):
    pad = -a.shape[1] % mult
    return jnp.pad(a, ((0, 0), (0, pad))) if pad else a


def _pack(arrs, cols):
    flat = jnp.concatenate([a.reshape(-1).astype(F32) for a in arrs])
    sizes = [int(a.size) for a in arrs]
    flat = jnp.pad(flat, (0, -flat.shape[0] % (SUBLANES * cols)))
    return flat.reshape(-1, cols), sizes


def _unpack(flat2d, sizes, shapes):
    flat = flat2d.reshape(-1)
    out, o = [], 0
    for n, s in zip(sizes, shapes):
        out.append(flat[o:o + n].reshape(s))
        o += n
    return out


PACK_COLS = SUBLANES * LANES


def kernel(x, norm_mix_w, w_in, conv_a_w, conv_a_b, dt_bias, a_log, d_a, norm_a_w, w_proj_a, s5_lam_re, s5_lam_im, s5_log_dt, s5_b_re, s5_b_im, s5_c_re, s5_c_im, s5_d, w_s5_glu, w_out, norm_ffn_w, w_up, conv_ffn_w, conv_ffn_b, w_down, norm_final_w, loss_target, m_norm_mix_w, m_w_in, m_conv_a_w, m_conv_a_b, m_dt_bias, m_a_log, m_d_a, m_norm_a_w, m_w_proj_a, m_s5_lam_re, m_s5_lam_im, m_s5_log_dt, m_s5_b_re, m_s5_b_im, m_s5_c_re, m_s5_c_im, m_s5_d, m_w_s5_glu, m_w_out, m_norm_ffn_w, m_w_up, m_conv_ffn_w, m_conv_ffn_b, m_w_down, m_norm_final_w, v_norm_mix_w, v_w_in, v_conv_a_w, v_conv_a_b, v_dt_bias, v_a_log, v_d_a, v_norm_a_w, v_w_proj_a, v_s5_lam_re, v_s5_lam_im, v_s5_log_dt, v_s5_b_re, v_s5_b_im, v_s5_c_re, v_s5_c_im, v_s5_d, v_w_s5_glu, v_w_out, v_norm_ffn_w, v_w_up, v_conv_ffn_w, v_conv_ffn_b, v_w_down, v_norm_final_w):
    weights = dict(norm_mix_w=norm_mix_w, w_in=w_in, conv_a_w=conv_a_w, conv_a_b=conv_a_b, dt_bias=dt_bias, a_log=a_log, d_a=d_a, norm_a_w=norm_a_w, w_proj_a=w_proj_a, s5_lam_re=s5_lam_re, s5_lam_im=s5_lam_im, s5_log_dt=s5_log_dt, s5_b_re=s5_b_re, s5_b_im=s5_b_im, s5_c_re=s5_c_re, s5_c_im=s5_c_im, s5_d=s5_d, w_s5_glu=w_s5_glu, w_out=w_out, norm_ffn_w=norm_ffn_w, w_up=w_up, conv_ffn_w=conv_ffn_w, conv_ffn_b=conv_ffn_b, w_down=w_down, norm_final_w=norm_final_w)
    moms = dict(norm_mix_w=m_norm_mix_w, w_in=m_w_in, conv_a_w=m_conv_a_w, conv_a_b=m_conv_a_b, dt_bias=m_dt_bias, a_log=m_a_log, d_a=m_d_a, norm_a_w=m_norm_a_w, w_proj_a=m_w_proj_a, s5_lam_re=m_s5_lam_re, s5_lam_im=m_s5_lam_im, s5_log_dt=m_s5_log_dt, s5_b_re=m_s5_b_re, s5_b_im=m_s5_b_im, s5_c_re=m_s5_c_re, s5_c_im=m_s5_c_im, s5_d=m_s5_d, w_s5_glu=m_w_s5_glu, w_out=m_w_out, norm_ffn_w=m_norm_ffn_w, w_up=m_w_up, conv_ffn_w=m_conv_ffn_w, conv_ffn_b=m_conv_ffn_b, w_down=m_w_down, norm_final_w=m_norm_final_w)
    vars_ = dict(norm_mix_w=v_norm_mix_w, w_in=v_w_in, conv_a_w=v_conv_a_w, conv_a_b=v_conv_a_b, dt_bias=v_dt_bias, a_log=v_a_log, d_a=v_d_a, norm_a_w=v_norm_a_w, w_proj_a=v_w_proj_a, s5_lam_re=v_s5_lam_re, s5_lam_im=v_s5_lam_im, s5_log_dt=v_s5_log_dt, s5_b_re=v_s5_b_re, s5_b_im=v_s5_b_im, s5_c_re=v_s5_c_re, s5_c_im=v_s5_c_im, s5_d=v_s5_d, w_s5_glu=v_w_s5_glu, w_out=v_w_out, norm_ffn_w=v_norm_ffn_w, w_up=v_w_up, conv_ffn_w=v_conv_ffn_w, conv_ffn_b=v_conv_ffn_b, w_down=v_w_down, norm_final_w=v_norm_final_w)
    names = list(weights)
    col_sharded = ("w_in", "w_s5_glu", "w_up")
    row_sharded = ("w_proj_a", "w_out", "w_down")
    conv_sharded = ("conv_a_w", "conv_ffn_w")
    replicated = [n for n in names if n not in col_sharded + row_sharded + conv_sharded]

    t, d = x.shape[1:]
    x2, tgt = x.reshape(t, d), loss_target.reshape(t, d)
    nh = dt_bias.shape[-1]
    d_inner = norm_a_w.shape[-1]
    conv_dim = conv_a_b.shape[-1]
    gn = (conv_dim - d_inner) // 2
    ng = gn // D_STATE
    nr = nh // ng
    rp = d_inner // ng
    d_s5 = s5_d.shape[-1]
    gs, ps = s5_lam_re.shape[1:]
    cs = d_s5 // gs
    n_oct = gs // 8
    assert (gs * ps) % SCAN_COLS == 0 and 8 * cs == LANES and gs % 8 == 0
    d_ff = w_down.shape[1] * N_DEV
    dev = _flat(*_position())
    dev1 = dev.reshape(1).astype(jnp.int32)

    ka, kf = conv_a_w.shape[1], conv_ffn_w.shape[1]
    taps = jnp.concatenate([conv_a_w[0].reshape(1, -1), conv_ffn_w[0].reshape(1, -1)], axis=1)
    taps, taps_done = _all_gather(taps, name="ag_conv_taps")
    taps = taps[:, 0]

    def by_cols(n):
        return n in ("w_s5_glu", "w_up") and weights[n].shape[2] % LANES == 0

    w_in_blocks, started = _all_gather(_after(w_in[0], taps_done).astype(BF16), name="ag_w_in")
    pending = {}
    for n in ("w_proj_a", "w_s5_glu", "w_out", "w_up", "w_down"):
        shard = _after(weights[n][0], started).astype(BF16)
        pending[n], token = _exchange_start(shard, "gather_cols" if by_cols(n) else "gather_slot", dev1,
                                            name="ag_" + n)
        started = started + token

    def gathered(n, after):
        g = _exchange_wait(pending[n], after, name="agw_" + n)
        if by_cols(n):
            return g
        if n in row_sharded:
            return g.reshape(-1, g.shape[2])
        return jnp.transpose(g, (1, 0, 2)).reshape(g.shape[1], -1)

    seg_sizes = dict(z=d_inner, xs=d_inner, bm=gn, cm=gn, dt=nh, u=d_s5, ga=d, gb=d)
    seg_names = tuple(seg_sizes)
    pieces, seg_at = _w_in_pieces(seg_sizes, ("z", "xs", "ga", "gb", "bm", "cm", "u", "dt"), w_in.shape[2])
    na = ka * conv_a_w.shape[2]
    cw_a = jnp.transpose(taps[:, :na].reshape(N_DEV, ka, -1), (1, 0, 2)).reshape(ka, conv_dim)
    cw_f = jnp.transpose(taps[:, na:].reshape(N_DEV, kf, -1), (1, 0, 2)).reshape(kf, 2 * d_ff)
    cb_a, cb_f = conv_a_b, conv_ffn_b
    a_cols = {"xs": slice(0, d_inner), "bm": slice(d_inner, d_inner + gn), "cm": slice(d_inner + gn, conv_dim)}

    w1 = norm_mix_w.reshape(1, 1, d) + started[0, 0]
    hn1, = _blocked_fwd(_rms_fn, [x2], [w1], [(d, BF16)], tb=256, name="rms1")
    w_in_p = _w_in_pack(w_in_blocks, pieces, seg_at, seg_sizes, name="w_in_pack")
    pre = {sn: _mm(hn1, w_in_p, b_win=seg_at[sn], name="in_" + sn) for sn in seg_names}
    act_a = {sn: _conv_fwd(_comb_silu, [pre[sn]], [cw_a[:, a_cols[sn]]], [cb_a[:, a_cols[sn]]], out_dtype=F32,
                           name="conv_a_" + sn) for sn in a_cols}
    dtr3 = jnp.transpose(pre["dt"][:, :nh].reshape(t, ng, nr), (1, 0, 2))
    dtb3, alog3, dsk3 = (p.reshape(ng, 1, nr) for p in (dt_bias, a_log, d_a))
    nw3 = norm_a_w.reshape(ng, 1, rp)
    yn, hsave = _ssd_fwd(act_a["xs"], act_a["bm"], act_a["cm"], pre["z"], dtr3, dtb3, alog3, dsk3, nw3, name="ssd")
    w_proj = gathered("w_proj_a", yn)
    y_a = _mm(yn, w_proj, name="proj_a")

    lam_re3, lam_im3 = s5_lam_re[0][:, None, :], s5_lam_im[0][:, None, :]
    logdt3 = s5_log_dt[0][:, None, None]
    bt_re, bt_im = jnp.transpose(s5_b_re[0], (0, 2, 1)), jnp.transpose(s5_b_im[0], (0, 2, 1))
    lb_re3, lb_im3, bb_re, bb_im = _s5_params(lam_re3, lam_im3, logdt3, bt_re, bt_im, name="s5_params")
    eye = jnp.eye(8, dtype=F32)

    def diag_b(bt):
        return (bt.reshape(n_oct, 8, cs, 1, ps) * eye[None, :, None, :, None]).reshape(n_oct, 8 * cs, 8 * ps)

    def undiag_b(blk):
        return (blk.reshape(n_oct, 8, cs, 8, ps) * eye[None, :, None, :, None]).sum(axis=3).reshape(gs, cs, ps)

    def diag_c(cm):
        ct = jnp.transpose(cm.reshape(n_oct, 8, cs, ps), (0, 1, 3, 2))
        return (ct[:, :, :, None, :] * eye[None, :, None, :, None]).reshape(n_oct, 8 * ps, 8 * cs)

    def undiag_c(blk):
        ct = (blk.reshape(n_oct, 8, ps, 8, cs) * eye[None, :, None, :, None]).sum(axis=3)
        return jnp.transpose(ct, (0, 1, 3, 2)).reshape(gs, cs, ps)

    b_blk_re, b_blk_im = diag_b(bb_re), diag_b(bb_im)
    c_blk_re, c_blk_imn = diag_c(s5_c_re[0]), diag_c(-s5_c_im[0])
    d3 = s5_d.reshape(n_oct, 1, LANES)
    lb_re, lb_im = lb_re3.reshape(1, gs * ps), lb_im3.reshape(1, gs * ps)
    u = pre["u"]
    bu_re, bu_im = _blocked_fwd(_s5_bu_fn, [u], [b_blk_re, b_blk_im], [(gs * ps, F32)] * 2, nj=n_oct, tb=512,
                                name="s5_bu")
    s_re, s_im = _s5_scan_fwd(bu_re, bu_im, lb_re, lb_im, name="s5_scan")
    yb, = _blocked_fwd(_s5_out_fn, [s_re, s_im, u], [c_blk_re, c_blk_imn, d3], [(d_s5, BF16)], nj=n_oct, tb=512,
                       name="s5_out")
    w_glu = gathered("w_s5_glu", yb)
    glu_v = _mm(yb, w_glu, b_win=(0, d), name="glu_v")
    glu_g = _mm(yb, w_glu, b_win=(d, d), name="glu_g")
    merged, = _blocked_fwd(_merge_fn, [glu_v, glu_g, pre["ga"], pre["gb"], y_a], [], [(d, BF16)], tb=256,
                           name="merge")
    w_o = gathered("w_out", merged)
    h1 = _mm(merged, w_o, acc=x2, name="out_proj")
    w2 = norm_ffn_w.reshape(1, 1, d)
    hn2, = _blocked_fwd(_rms_fn, [h1], [w2], [(d, BF16)], tb=256, name="rms2")
    w_u = gathered("w_up", hn2)
    up_g = _mm(hn2, w_u, b_win=(0, d_ff), name="up_g")
    up_v = _mm(hn2, w_u, b_win=(d_ff, d_ff), name="up_v")
    f_w = [cw_f[:, :d_ff], cw_f[:, d_ff:]]
    f_b = [cb_f[:, :d_ff], cb_f[:, d_ff:]]
    act = _conv_fwd(_comb_glu, [up_g, up_v], f_w, f_b, out_dtype=BF16, name="conv_ffn")
    w_dn = gathered("w_down", act)
    h2 = _mm(act, w_dn, acc=h1, name="down")
    loss_tile, dh2, dh2_b, g_final = _loss_head(h2, norm_final_w.reshape(1, d), tgt, name="loss_head")

    grads, scattering = {}, {}

    def scatter_start(n, g):
        if by_cols(n):
            src, mode = g, "scatter_cols"
        elif n in row_sharded:
            src, mode = g.reshape(N_DEV, -1, g.shape[1]), "scatter_slot"
        elif n == "w_in":
            src, mode = g, "scatter_slot"
        else:
            src, mode = jnp.transpose(g.reshape(g.shape[0], N_DEV, -1), (1, 0, 2)), "scatter_slot"
        scattering[n], token = _exchange_start(src, mode, dev1, name="rs_" + n)
        return token

    d_act = _mm(dh2_b, w_dn, tb=True, name="d_act")
    g_down = _mm(act, dh2_b, ta=True, out_dtype=BF16, name="g_w_down")
    tok = scatter_start("w_down", g_down)
    (dup_g, dwf_g, dbf_g), (dup_v, dwf_v, dbf_v) = _conv_bwd(
        _comb_glu, [up_g, up_v], f_w, [_after(f_b[0], tok), f_b[1]], d_act, dx_dtype=BF16, name="conv_ffn_bwd")
    dhn2 = _mm(dup_g, w_u, tb=True, b_win=(0, d_ff), name="d_hn2_g")
    dhn2 = _mm(dup_v, w_u, tb=True, b_win=(d_ff, d_ff), acc=dhn2, name="d_hn2_v")
    g_up = _mm(hn2, dup_g, ta=True, into=(lax.empty((d, 2 * d_ff), BF16), 0), name="g_w_up_g")
    g_up = _mm(hn2, dup_v, ta=True, into=(g_up, d_ff), name="g_w_up_v")
    tok = scatter_start("w_up", g_up)
    dh1, dh1_b, g_w2 = _blocked_bwd(_rms_fn, [h1], [_after(w2, tok)], [dhn2], [(F32, BF16)], adds={0: dh2}, tb=256,
                                    name="rms2_bwd")
    d_merged = _mm(dh1_b, w_o, tb=True, name="d_merged")
    g_out = _mm(merged, dh1_b, ta=True, out_dtype=BF16, name="g_w_out")
    tok = scatter_start("w_out", g_out)
    dglu_v, dglu_g, dga, dgb, dy_a = _blocked_bwd(
        _merge_fn, [glu_v, glu_g, pre["ga"], pre["gb"], y_a], [], [d_merged], [BF16] * 5, tb=128, name="merge_bwd")
    dyb = _mm(dglu_v, w_glu, tb=True, b_win=(0, d), name="d_yb_v")
    dyb = _mm(dglu_g, w_glu, tb=True, b_win=(d, d), acc=dyb, name="d_yb_g")
    g_glu = _mm(yb, dglu_v, ta=True, into=(lax.empty((d_s5, 2 * d), BF16), 0), name="g_w_glu_v")
    g_glu = _mm(yb, dglu_g, ta=True, into=(g_glu, d), name="g_w_glu_g")
    tok = tok + scatter_start("w_s5_glu", g_glu)
    ds_re, ds_im, du_skip, dc_blk_re, dc_blk_imn, dd3 = _blocked_bwd(
        _s5_out_fn, [s_re, s_im, u], [c_blk_re, c_blk_imn, _after(d3, tok)], [dyb], [F32, F32, F32], nj=n_oct, tb=512,
        name="s5_out_bwd")
    dbu_re, dbu_im, dlb_re, dlb_im = _s5_scan_bwd(s_re, s_im, ds_re, ds_im, lb_re, lb_im, name="s5_scan_bwd")
    du, db_blk_re, db_blk_im = _blocked_bwd(
        _s5_bu_fn, [u], [b_blk_re, b_blk_im], [dbu_re, dbu_im], [BF16], adds={0: du_skip}, nj=n_oct, tb=512,
        name="s5_bu_bwd")
    g_lre, g_lim, g_ldt, g_bt_re, g_bt_im = _s5_params(
        lam_re3, lam_im3, logdt3, bt_re, bt_im,
        cts=(dlb_re.reshape(gs, 1, ps), dlb_im.reshape(gs, 1, ps), undiag_b(db_blk_re), undiag_b(db_blk_im)),
        name="s5_params_bwd")
    grads["s5_lam_re"], grads["s5_lam_im"] = g_lre.reshape(s5_lam_re.shape), g_lim.reshape(s5_lam_im.shape)
    grads["s5_log_dt"] = g_ldt.reshape(s5_log_dt.shape)
    grads["s5_b_re"] = jnp.transpose(g_bt_re, (0, 2, 1)).reshape(s5_b_re.shape)
    grads["s5_b_im"] = jnp.transpose(g_bt_im, (0, 2, 1)).reshape(s5_b_im.shape)
    grads["s5_c_re"] = undiag_c(dc_blk_re).reshape(s5_c_re.shape)
    grads["s5_c_im"] = -undiag_c(dc_blk_imn).reshape(s5_c_im.shape)
    grads["s5_d"] = dd3.reshape(s5_d.shape)

    dyn = _mm(dy_a, w_proj, tb=True, name="d_yn")
    g_proj = _mm(yn, dy_a, ta=True, out_dtype=BF16, name="g_w_proj_a")
    tok = scatter_start("w_proj_a", g_proj)
    dxs, dbm, dcm, dz, ddtr3, g_dtb, g_alog, g_dsk, g_nw = _ssd_bwd(
        act_a["xs"], act_a["bm"], act_a["cm"], pre["z"], dtr3, hsave, dtb3, alog3, dsk3, _after(nw3, tok), dyn,
        name="ssd_bwd")
    grads["dt_bias"], grads["a_log"], grads["d_a"] = (g.reshape(1, nh) for g in (g_dtb, g_alog, g_dsk))
    grads["norm_a_w"] = g_nw.reshape(1, d_inner)
    dpre = {"z": dz, "u": du, "ga": dga, "gb": dgb}
    dcw, dcb = {}, {}
    for sn, dact in (("xs", dxs), ("bm", dbm), ("cm", dcm)):
        (dpre[sn], dcw[sn], dcb[sn]), = _conv_bwd(
            _comb_silu, [pre[sn]], [cw_a[:, a_cols[sn]]], [cb_a[:, a_cols[sn]]], dact, dx_dtype=BF16,
            name="conv_a_bwd_" + sn)
    dpre["dt"] = _pad_cols(jnp.transpose(ddtr3, (1, 0, 2)).reshape(t, nh), LANES).astype(BF16)
    g_in = _w_in_unpack({sn: _mm(hn1, dpre[sn], ta=True, name="g_w_in_" + sn) for sn in seg_names}, pieces,
                        w_in.shape[2], name="w_in_unpack")
    tok = scatter_start("w_in", g_in)
    dhn1 = _mm(_after(dpre["dt"], tok), w_in_p, tb=True, b_win=seg_at["dt"], name="d_hn1_dt")
    for sn in seg_names:
        if sn != "dt":
            dhn1 = _mm(dpre[sn], w_in_p, tb=True, b_win=seg_at[sn], acc=dhn1, name="d_hn1_" + sn)
    dx, g_w1 = _blocked_bwd(_rms_fn, [x2], [w1], [dhn1], [F32], adds={0: dh1}, tb=256, name="rms1_bwd")

    grads["norm_mix_w"], grads["norm_ffn_w"] = g_w1.reshape(1, d), g_w2.reshape(1, d)
    grads["norm_final_w"] = g_final.reshape(d)
    grads["conv_a_b"] = jnp.concatenate([dcb["xs"], dcb["bm"], dcb["cm"]], axis=1)
    grads["conv_ffn_b"] = jnp.concatenate([dbf_g, dbf_v], axis=1)
    g_cw_a = jnp.concatenate([dcw["xs"], dcw["bm"], dcw["cm"]], axis=1)
    g_cw_f = jnp.concatenate([dwf_g, dwf_v], axis=1)

    small = [grads[n] for n in replicated] + [g_cw_a, g_cw_f, loss_tile[:1, :1]]
    packed, sizes = _pack(small, PACK_COLS)
    summed = _sum_parts(_all_gather(packed, name="ag_small_grads")[0], name="sum_small_grads")
    *rep_sums, s_cw_a, s_cw_f, loss = _unpack(summed, sizes, [a.shape for a in small])
    for n, g in zip(replicated, rep_sums):
        grads[n] = g
    wa, wf = conv_a_w.shape[2], conv_ffn_w.shape[2]
    grads["conv_a_w"] = lax.dynamic_slice_in_dim(s_cw_a, dev * wa, wa, axis=1)[None]
    grads["conv_ffn_w"] = lax.dynamic_slice_in_dim(s_cw_f, dev * wf, wf, axis=1)[None]

    delta, new_m, new_v = {}, {}, {}
    done = dx
    for n in ("w_down", "w_up", "w_out", "w_s5_glu", "w_proj_a", "w_in"):
        land = _exchange_wait(scattering[n], done, name="rsw_" + n)
        shape = weights[n].shape
        two_d = lambda a: a.reshape(shape[-2], shape[-1])
        g = _sum_parts(land, name="rs_sum_" + n)
        grads[n] = g.reshape(shape)
        dl, nm, nv = _adamw(two_d(weights[n]), g, two_d(moms[n]), two_d(vars_[n]), name="adamw_" + n)
        delta[n], new_m[n], new_v[n] = dl.reshape(shape), nm.reshape(shape), nv.reshape(shape)
        done = dl
    for n in replicated + list(conv_sharded):
        shape = weights[n].shape
        two_d = lambda a: a.reshape(-1, shape[-1])
        dl, nm, nv = _adamw(two_d(weights[n]), two_d(grads[n]), two_d(moms[n]), two_d(vars_[n]), name="adamw_" + n)
        delta[n], new_m[n], new_v[n] = dl.reshape(shape), nm.reshape(shape), nv.reshape(shape)

    return (loss.reshape(()), dx.reshape(x.shape), *[grads[n] for n in names], *[delta[n] for n in names],
            *[new_m[n] for n in names], *[new_v[n] for n in names])
```

```python
import functools

import jax
import jax.numpy as jnp
from jax import lax
from jax.experimental import pallas as pl
from jax.experimental.pallas import tpu as pltpu

F32 = jnp.float32
BF16 = jnp.bfloat16
HIGHEST = lax.Precision.HIGHEST
MESH = pl.DeviceIdType.MESH

EPS = 1e-6
EIG_MAX = -1e-4
D_STATE = 128
CHUNK = 256
ADAM_LR = 0.001
ADAM_B1 = 0.9
ADAM_B2 = 0.999
ADAM_EPS = 1e-08
ADAM_WD = 0.01
ADAM_STEP = 10
N_DEV = 8
LANES = 128
SUBLANES = 8
VMEM_LIMIT = 56 * 1024 * 1024
MM_MAX_K = 4096


def _cp(*sem):
    return pltpu.CompilerParams(dimension_semantics=sem, vmem_limit_bytes=VMEM_LIMIT)


def _tile(dim, pref, unit=LANES):
    if dim <= unit:
        return dim
    t = (min(pref, dim) // unit) * unit
    while dim % t:
        t -= unit
    return t


_DIMS = {"nn": (((1,), (0,)), ((), ())), "nt": (((1,), (1,)), ((), ())), "tn": (((0,), (0,)), ((), ()))}


def _dot(a, b, kind):
    return lax.dot_general(a.astype(BF16), b.astype(BF16), _DIMS[kind], preferred_element_type=F32)


@functools.partial(jax.custom_vjp, nondiff_argnums=(2,))
def _bdot(a, b, kind):
    return _dot(a, b, kind)


def _bdot_fwd(a, b, kind):
    return _dot(a, b, kind), (a, b)


def _bdot_bwd(kind, res, g):
    a, b = res
    if kind == "nn":
        return _dot(g, b, "nt"), _dot(a, g, "tn")
    if kind == "nt":
        return _dot(g, b, "nn"), _dot(g, a, "tn")
    return _dot(b, g, "nt"), _dot(a, g, "nn")


_bdot.defvjp(_bdot_fwd, _bdot_bwd)


def _mm(a, b, *, ta=False, tb=False, acc=None, out_dtype=F32, name, b_win=None, into=None):
    assert not (ta and tb)
    m, k = (a.shape[1], a.shape[0]) if ta else a.shape
    b_off, b_size = b_win or (0, b.shape[1])
    n = b.shape[0] if tb else b_size
    assert (b_size if tb else b.shape[0]) == k, (a.shape, b.shape, ta, tb, b_win)
    o_off = into[1] if into else 0
    nk = -(-k // MM_MAX_K)
    while k % nk or (k // nk) % LANES or (tb and b_off % (k // nk)):
        nk += 1
    tk = k // nk
    tm, tn = _tile(m, 1024), _tile(n, 1024)
    while o_off % tn or (not tb and b_off % tn):
        tn = _tile(n, tn - LANES)
    kind = "tn" if ta else ("nt" if tb else "nn")
    a_spec = pl.BlockSpec((tk, tm), lambda i, j, l: (l, i)) if ta else pl.BlockSpec((tm, tk), lambda i, j, l: (i, l))
    if tb:
        b_spec = pl.BlockSpec((tn, tk), lambda i, j, l: (j, l + b_off // tk))
    else:
        b_spec = pl.BlockSpec((tk, tn), lambda i, j, l: (l, j + b_off // tn))
    c_spec = pl.BlockSpec((tm, tn), lambda i, j, l: (i, j))
    o_spec = pl.BlockSpec((tm, tn), lambda i, j, l: (i, j + o_off // tn))
    has_acc = acc is not None

    def body(*refs):
        a_ref, b_ref = refs[:2]
        c_ref = refs[2] if has_acc else None
        o_ref = refs[2 + has_acc + (into is not None)]
        if nk == 1:
            res = _dot(a_ref[...], b_ref[...], kind)
            if has_acc:
                res = res + c_ref[...].astype(F32)
            o_ref[...] = res.astype(o_ref.dtype)
            return
        acc_ref = refs[-1]
        l = pl.program_id(2)

        @pl.when(l == 0)
        def _():
            if has_acc:
                acc_ref[...] = c_ref[...].astype(F32)
            else:
                acc_ref[...] = jnp.zeros_like(acc_ref)

        acc_ref[...] += _dot(a_ref[...], b_ref[...], kind)

        @pl.when(l == nk - 1)
        def _():
            o_ref[...] = acc_ref[...].astype(o_ref.dtype)

    ins = [a, b] + ([acc] if has_acc else []) + ([into[0]] if into else [])
    in_specs = [a_spec, b_spec] + ([c_spec] if has_acc else []) + ([pl.BlockSpec(memory_space=pl.ANY)] if into else [])
    out_shape = jax.ShapeDtypeStruct(into[0].shape, into[0].dtype) if into else jax.ShapeDtypeStruct((m, n), out_dtype)
    return pl.pallas_call(
        body, name=name, grid=(m // tm, n // tn, nk), in_specs=in_specs, out_specs=o_spec, out_shape=out_shape,
        input_output_aliases={len(ins) - 1: 0} if into else {},
        scratch_shapes=[pltpu.VMEM((tm, tn), F32)] if nk > 1 else [],
        compiler_params=_cp("parallel", "parallel", "arbitrary"),
    )(*ins)


def _w_in_pieces(seg_sizes, seg_order, n_blk):
    layout, o = {}, 0
    for sn in seg_order:
        width = -(-seg_sizes[sn] // LANES) * LANES
        layout[sn] = (o, width)
        o += width
    pieces, start = [], 0
    for sn, sz in seg_sizes.items():
        lo = start
        while lo < start + sz:
            blk = lo // n_blk
            hi = min(start + sz, (blk + 1) * n_blk)
            pieces.append((blk, lo - blk * n_blk, sn, lo - start, layout[sn][0] + lo - start, hi - lo))
            lo = hi
        start += sz
    return pieces, layout


def _w_in_pack(gathered, pieces, layout, seg_sizes, *, name, tr=256):
    _, k, n_blk = gathered.shape
    n_pad = sum(w for _, w in layout.values())

    def body(g_ref, o_ref):
        for sn, (off, width) in layout.items():
            if width != seg_sizes[sn]:
                o_ref[:, pl.ds(off + seg_sizes[sn], width - seg_sizes[sn])] = jnp.zeros(
                    (tr, width - seg_sizes[sn]), o_ref.dtype)
        for blk, src, _, _, dst, width in pieces:
            o_ref[:, pl.ds(dst, width)] = g_ref[blk, :, pl.ds(src, width)]

    return pl.pallas_call(
        body, name=name, grid=(k // tr,), in_specs=[pl.BlockSpec((N_DEV, tr, n_blk), lambda i: (0, i, 0))],
        out_specs=pl.BlockSpec((tr, n_pad), lambda i: (i, 0)), out_shape=jax.ShapeDtypeStruct((k, n_pad), gathered.dtype),
        compiler_params=_cp("parallel"),
    )(gathered)


def _w_in_unpack(seg_grads, pieces, n_blk, *, name, tr=128):
    names = list(seg_grads)
    k = seg_grads[names[0]].shape[0]

    def body(*refs):
        o_ref = refs[-1]
        seg_ref = dict(zip(names, refs))
        for blk, dst, sn, src, _, width in pieces:
            o_ref[blk, :, pl.ds(dst, width)] = seg_ref[sn][:, pl.ds(src, width)].astype(o_ref.dtype)

    return pl.pallas_call(
        body, name=name, grid=(k // tr,),
        in_specs=[pl.BlockSpec((tr, seg_grads[sn].shape[1]), lambda i: (i, 0)) for sn in names],
        out_specs=pl.BlockSpec((N_DEV, tr, n_blk), lambda i: (0, i, 0)),
        out_shape=jax.ShapeDtypeStruct((N_DEV, k, n_blk), BF16), compiler_params=_cp("parallel"),
    )(*[seg_grads[sn] for sn in names])


def _row_spec(arr, tb, nj):
    return pl.BlockSpec((tb, arr.shape[1] // nj), lambda j, i: (i, j))


def _par_spec(arr):
    return pl.BlockSpec((1,) + arr.shape[1:], lambda j, i: (j, 0, 0))


def _blocked_fwd(fn, rows, params, outs, *, nj=1, tb, name):
    t = rows[0].shape[0]
    nr, npar = len(rows), len(params)

    def body(*refs):
        res = fn(*[r[...] for r in refs[:nr]], *[p[0] for p in refs[nr:nr + npar]])
        for o_ref, val in zip(refs[nr + npar:], res):
            o_ref[...] = val.astype(o_ref.dtype)

    return pl.pallas_call(
        body, name=name, grid=(nj, t // tb),
        in_specs=[_row_spec(a, tb, nj) for a in rows] + [_par_spec(p) for p in params],
        out_specs=[pl.BlockSpec((tb, c // nj), lambda j, i: (i, j)) for c, _ in outs],
        out_shape=[jax.ShapeDtypeStruct((t, c), dt) for c, dt in outs],
        compiler_params=_cp("parallel", "arbitrary"),
    )(*rows, *params)


def _blocked_bwd(fn, rows, params, cts, row_grad_dtypes, *, adds=None, nj=1, tb, name):
    t = rows[0].shape[0]
    nr, npar, nct = len(rows), len(params), len(cts)
    adds = adds or {}
    add_keys = sorted(adds)
    want, want_dtypes = [], []
    for k, dts in enumerate(row_grad_dtypes):
        for dt in (dts if isinstance(dts, tuple) else (dts,)):
            if dt is not None:
                want.append(k)
                want_dtypes.append(dt)

    def body(*refs):
        row_refs = refs[:nr]
        par_refs = refs[nr:nr + npar]
        ct_refs = refs[nr + npar:nr + npar + nct]
        add_refs = dict(zip(add_keys, refs[nr + npar + nct:nr + npar + nct + len(add_keys)]))
        out_refs = refs[nr + npar + nct + len(add_keys):]
        _, vjp = jax.vjp(fn, *[r[...] for r in row_refs], *[p[0] for p in par_refs])
        grads = vjp(tuple(c[...].astype(F32) for c in ct_refs))
        for o_ref, k in zip(out_refs, want):
            g = grads[k]
            if k in add_refs:
                g = g + add_refs[k][...].astype(F32)
            o_ref[...] = g.astype(o_ref.dtype)
        first = pl.program_id(1) == 0
        for o_ref, g in zip(out_refs[len(want):], grads[nr:]):
            @pl.when(first)
            def _(o_ref=o_ref):
                o_ref[...] = jnp.zeros_like(o_ref)
            o_ref[0] += g

    add_arrs = [adds[k] for k in add_keys]
    return pl.pallas_call(
        body, name=name, grid=(nj, t // tb),
        in_specs=[_row_spec(a, tb, nj) for a in rows] + [_par_spec(p) for p in params]
        + [_row_spec(c, tb, nj) for c in cts] + [_row_spec(a, tb, nj) for a in add_arrs],
        out_specs=[_row_spec(rows[k], tb, nj) for k in want] + [_par_spec(p) for p in params],
        out_shape=[jax.ShapeDtypeStruct(rows[k].shape, dt) for k, dt in zip(want, want_dtypes)]
        + [jax.ShapeDtypeStruct(p.shape, F32) for p in params],
        compiler_params=_cp("parallel", "arbitrary"),
    )(*rows, *params, *cts, *add_arrs)


def _rms_fn(x, w):
    return (x * lax.rsqrt(jnp.mean(x * x, axis=-1, keepdims=True) + EPS) * w,)


def _silu(x):
    return x * jax.nn.sigmoid(x)


def _merge_fn(glu_v, glu_g, g_a, g_b, y_a):
    y_b = glu_v * jax.nn.sigmoid(glu_g)
    return (jax.nn.sigmoid(g_a) * y_a + jax.nn.sigmoid(g_b) * y_b,)


def _s5_bu_fn(u, b_re, b_im):
    return _bdot(u, b_re, "nn"), _bdot(u, b_im, "nn")


def _s5_out_fn(s_re, s_im, u, c_re, c_im_neg, d):
    return (jax.nn.gelu(_bdot(s_re, c_re, "nn") + _bdot(s_im, c_im_neg, "nn") + d * u),)


HALO = SUBLANES


STRIP = 64


def _conv_strip(ext_ref, w_ref, b_ref, r0, cols):
    kw = w_ref.shape[0]
    xs = [ext_ref[pl.ds(r0 + HALO - kw + 1 + k, STRIP), cols] for k in range(kw)]
    c = b_ref[:, cols] + w_ref[0:1, cols] * xs[0]
    for k in range(1, kw):
        c = c + w_ref[k:k + 1, cols] * xs[k]
    return c, xs


def _fold(x):
    return x.reshape(STRIP // SUBLANES, SUBLANES, LANES).sum(axis=0)


def _conv_specs(xs, ws, bs, tb, cb, time_of):
    specs = []
    for x, w, b in zip(xs, ws, bs):
        specs += [
            pl.BlockSpec((HALO, cb), lambda j, i: (jnp.maximum(time_of(i) * (tb // HALO) - 1, 0), j)),
            pl.BlockSpec((tb, cb), lambda j, i: (time_of(i), j)),
            pl.BlockSpec((w.shape[0], cb), lambda j, i: (0, j)),
            pl.BlockSpec((1, cb), lambda j, i: (0, j)),
        ]
    return specs


def _conv_fwd(comb, xs, ws, bs, *, out_dtype, name, tb=512):
    t, c = xs[0].shape
    cb = _tile(c, 512)
    ns = len(xs)

    def body(*refs):
        i = pl.program_id(1)
        o_ref = refs[4 * ns]
        exts = refs[4 * ns + 1:]
        for s in range(ns):
            xp_ref, xm_ref = refs[4 * s:4 * s + 2]
            exts[s][pl.ds(0, HALO), :] = jnp.where(i == 0, 0.0, xp_ref[...])
            exts[s][pl.ds(HALO, tb), :] = xm_ref[...]
        for c0 in range(0, cb, LANES):
            cols = pl.ds(c0, LANES)
            for r0 in range(0, tb, STRIP):
                cs = [_conv_strip(exts[s], refs[4 * s + 2], refs[4 * s + 3], r0, cols)[0] for s in range(ns)]
                o_ref[pl.ds(r0, STRIP), cols] = comb(*cs).astype(out_dtype)

    flat = [a for x, w, b in zip(xs, ws, bs) for a in (x, x, w, b)]
    return pl.pallas_call(
        body, name=name, grid=(c // cb, t // tb),
        in_specs=_conv_specs(xs, ws, bs, tb, cb, lambda i: i),
        out_specs=pl.BlockSpec((tb, cb), lambda j, i: (i, j)),
        out_shape=jax.ShapeDtypeStruct((t, c), out_dtype),
        scratch_shapes=[pltpu.VMEM((HALO + tb, cb), F32) for _ in range(ns)],
        compiler_params=_cp("parallel", "arbitrary"),
    )(*flat)


def _conv_bwd(comb, xs, ws, bs, dy, *, dx_dtype, name, tb=512):
    t, c = xs[0].shape
    cb = _tile(c, 512)
    ns = len(xs)
    nt = t // tb
    kw = ws[0].shape[0]

    def body(*refs):
        step = pl.program_id(1)
        dy_ref = refs[4 * ns]
        out_refs = refs[4 * ns + 1:4 * ns + 1 + 3 * ns]
        scratch = refs[4 * ns + 1 + 3 * ns:]
        exts, dcs, carries = scratch[:ns], scratch[ns:2 * ns], scratch[2 * ns:]

        @pl.when(step == 0)
        def _():
            for s in range(ns):
                carries[s][...] = jnp.zeros_like(carries[s])
                out_refs[3 * s + 1][...] = jnp.zeros_like(out_refs[3 * s + 1])
                out_refs[3 * s + 2][...] = jnp.zeros_like(out_refs[3 * s + 2])

        for s in range(ns):
            xp_ref, xm_ref = refs[4 * s:4 * s + 2]
            exts[s][pl.ds(0, HALO), :] = jnp.where(step == nt - 1, 0.0, xp_ref[...])
            exts[s][pl.ds(HALO, tb), :] = xm_ref[...]
            dcs[s][pl.ds(tb, HALO), :] = carries[s][...]
        for c0 in range(0, cb, LANES):
            cols = pl.ds(c0, LANES)
            acc_w = [[jnp.zeros((SUBLANES, LANES), F32) for _ in range(kw)] for _ in range(ns)]
            acc_b = [jnp.zeros((SUBLANES, LANES), F32) for _ in range(ns)]
            for r0 in range(0, tb, STRIP):
                strips = [_conv_strip(exts[s], refs[4 * s + 2], refs[4 * s + 3], r0, cols) for s in range(ns)]
                _, vjp = jax.vjp(comb, *[cs for cs, _ in strips])
                grads = vjp(dy_ref[pl.ds(r0, STRIP), cols].astype(F32))
                for s in range(ns):
                    dcs[s][pl.ds(r0, STRIP), cols] = grads[s]
                    acc_b[s] = acc_b[s] + _fold(grads[s])
                    for k in range(kw):
                        acc_w[s][k] = acc_w[s][k] + _fold(grads[s] * strips[s][1][k])
            for s in range(ns):
                dw_ref, db_ref = out_refs[3 * s + 1], out_refs[3 * s + 2]
                db_ref[:, cols] += jnp.sum(acc_b[s], axis=0, keepdims=True)
                for k in range(kw):
                    dw_ref[k:k + 1, cols] += jnp.sum(acc_w[s][k], axis=0, keepdims=True)
        for s in range(ns):
            w_ref, dx_ref = refs[4 * s + 2], out_refs[3 * s]
            for c0 in range(0, cb, LANES):
                cols = pl.ds(c0, LANES)
                for r0 in range(0, tb, STRIP):
                    dx = w_ref[kw - 1:kw, cols] * dcs[s][pl.ds(r0, STRIP), cols]
                    for k in range(kw - 1):
                        dx = dx + w_ref[k:k + 1, cols] * dcs[s][pl.ds(r0 + kw - 1 - k, STRIP), cols]
                    dx_ref[pl.ds(r0, STRIP), cols] = dx.astype(dx_dtype)
            carries[s][...] = dcs[s][pl.ds(0, HALO), :]

    flat = [a for x, w, b in zip(xs, ws, bs) for a in (x, x, w, b)]
    rev = lambda i: nt - 1 - i
    out_specs, out_shape = [], []
    for x, w, b in zip(xs, ws, bs):
        out_specs += [pl.BlockSpec((tb, cb), lambda j, i: (rev(i), j)),
                      pl.BlockSpec((w.shape[0], cb), lambda j, i: (0, j)),
                      pl.BlockSpec((1, cb), lambda j, i: (0, j))]
        out_shape += [jax.ShapeDtypeStruct((t, c), dx_dtype), jax.ShapeDtypeStruct(w.shape, F32),
                      jax.ShapeDtypeStruct(b.shape, F32)]
    res = pl.pallas_call(
        body, name=name, grid=(c // cb, nt),
        in_specs=_conv_specs(xs, ws, bs, tb, cb, rev) + [pl.BlockSpec((tb, cb), lambda j, i: (rev(i), j))],
        out_specs=out_specs, out_shape=out_shape,
        scratch_shapes=[pltpu.VMEM((HALO + tb, cb), F32) for _ in range(2 * ns)]
        + [pltpu.VMEM((HALO, cb), F32) for _ in range(ns)],
        compiler_params=_cp("parallel", "arbitrary"),
    )(*flat, dy)
    return [tuple(res[3 * s:3 * s + 3]) for s in range(ns)]


def _comb_silu(c):
    return _silu(c)


def _comb_glu(cg, cv):
    return _silu(cg) * cv


def _ssd_fn(nheads, hdim):
    def fn(x, bm, cm, z, dtr, hin, dtb, alog, dsk, nw):
        q = x.shape[0]
        dt = jax.nn.softplus(dtr + dtb)
        da = dt * (-jnp.exp(alog))
        li = lax.broadcasted_iota(jnp.int32, (q, q), 0)
        si = lax.broadcasted_iota(jnp.int32, (q, q), 1)
        causal = li >= si
        tri = causal.astype(F32)
        acs = jnp.dot(tri, da, precision=HIGHEST, preferred_element_type=F32)
        acs_row = lax.dot_general(da, tri, (((0,), (1,)), ((), ())), precision=HIGHEST,
                                  preferred_element_type=F32)
        cb = _bdot(cm, bm, "nt")
        ch = _bdot(cm, hin, "nn")
        ys, hs = [], []
        for r in range(nheads):
            cols = slice(r * hdim, (r + 1) * hdim)
            xr = x[:, cols]
            a_col = acs[:, r:r + 1]
            decay = jnp.exp(jnp.where(causal, a_col - acs_row[r:r + 1, :], -1e30))
            xd = xr * dt[:, r:r + 1]
            y_diag = _bdot(cb * decay, xd, "nn")
            y_off = ch[:, cols] * jnp.exp(a_col)
            last = acs[q - 1:q, r:r + 1]
            st = _bdot(bm * jnp.exp(last - a_col), xd, "tn")
            hs.append(jnp.exp(last) * hin[:, cols] + st)
            ys.append(y_diag + y_off + dsk[:, r:r + 1] * xr)
        y = jnp.concatenate(ys, axis=1) * _silu(z)
        yn = y * lax.rsqrt(jnp.mean(y * y, axis=-1, keepdims=True) + EPS) * nw
        return yn, jnp.concatenate(hs, axis=1)
    return fn


def _ssd_specs(rp, nr, time_of):
    row = lambda w: pl.BlockSpec((CHUNK, w), lambda g, c: (time_of(c), g))
    par = lambda w: pl.BlockSpec((1, 1, w), lambda g, c: (g, 0, 0))
    return dict(
        x=row(rp), bc=row(D_STATE), dtr=pl.BlockSpec((1, CHUNK, nr), lambda g, c: (g, time_of(c), 0)),
        h=pl.BlockSpec((1, 1, D_STATE, rp), lambda g, c: (g, time_of(c), 0, 0)), pr=par(nr), pw=par(rp))


def _ssd_fwd(xs, bm, cm, z, dtr, dtb, alog, dsk, nw, *, name):
    t = xs.shape[0]
    g, _, nr = dtr.shape
    rp = xs.shape[1] // g
    nc = t // CHUNK
    fn = _ssd_fn(nr, rp // nr)
    sp = _ssd_specs(rp, nr, lambda c: c)

    def body(x_ref, b_ref, c_ref, z_ref, dtr_ref, dtb_ref, al_ref, dsk_ref, nw_ref, yn_ref, hs_ref, h_ref):
        @pl.when(pl.program_id(1) == 0)
        def _():
            h_ref[...] = jnp.zeros_like(h_ref)
        hin = h_ref[...]
        hs_ref[0, 0] = hin
        yn, hout = fn(x_ref[...], b_ref[...], c_ref[...], z_ref[...], dtr_ref[0], hin,
                      dtb_ref[0], al_ref[0], dsk_ref[0], nw_ref[0])
        yn_ref[...] = yn.astype(yn_ref.dtype)
        h_ref[...] = hout

    return pl.pallas_call(
        body, name=name, grid=(g, nc),
        in_specs=[sp["x"], sp["bc"], sp["bc"], sp["x"], sp["dtr"], sp["pr"], sp["pr"], sp["pr"], sp["pw"]],
        out_specs=[sp["x"], sp["h"]],
        out_shape=[jax.ShapeDtypeStruct(xs.shape, BF16), jax.ShapeDtypeStruct((g, nc, D_STATE, rp), F32)],
        scratch_shapes=[pltpu.VMEM((D_STATE, rp), F32)],
        compiler_params=_cp("parallel", "arbitrary"),
    )(xs, bm, cm, z, dtr, dtb, alog, dsk, nw)


def _ssd_bwd(xs, bm, cm, z, dtr, hsave, dtb, alog, dsk, nw, dyn, *, name):
    t = xs.shape[0]
    g, _, nr = dtr.shape
    rp = xs.shape[1] // g
    nc = t // CHUNK
    fn = _ssd_fn(nr, rp // nr)
    sp = _ssd_specs(rp, nr, lambda c: nc - 1 - c)

    def body(x_ref, b_ref, c_ref, z_ref, dtr_ref, hs_ref, dtb_ref, al_ref, dsk_ref, nw_ref, dyn_ref,
             dx_ref, db_ref, dc_ref, dz_ref, ddtr_ref, ddtb_ref, dal_ref, ddsk_ref, dnw_ref, dh_ref):
        first = pl.program_id(1) == 0

        @pl.when(first)
        def _():
            dh_ref[...] = jnp.zeros_like(dh_ref)
            for r in (ddtb_ref, dal_ref, ddsk_ref, dnw_ref):
                r[...] = jnp.zeros_like(r)

        _, vjp = jax.vjp(fn, x_ref[...], b_ref[...], c_ref[...], z_ref[...], dtr_ref[0], hs_ref[0, 0],
                         dtb_ref[0], al_ref[0], dsk_ref[0], nw_ref[0])
        dx, db, dc, dz, ddtr, dhin, ddtb, dal, ddsk, dnw = vjp((dyn_ref[...].astype(F32), dh_ref[...]))
        dx_ref[...] = dx
        db_ref[...] = db
        dc_ref[...] = dc
        dz_ref[...] = dz.astype(dz_ref.dtype)
        ddtr_ref[0] = ddtr
        dh_ref[...] = dhin
        ddtb_ref[0] += ddtb
        dal_ref[0] += dal
        ddsk_ref[0] += ddsk
        dnw_ref[0] += dnw

    sd = jax.ShapeDtypeStruct
    return pl.pallas_call(
        body, name=name, grid=(g, nc),
        in_specs=[sp["x"], sp["bc"], sp["bc"], sp["x"], sp["dtr"], sp["h"], sp["pr"], sp["pr"], sp["pr"], sp["pw"],
                  sp["x"]],
        out_specs=[sp["x"], sp["bc"], sp["bc"], sp["x"], sp["dtr"], sp["pr"], sp["pr"], sp["pr"], sp["pw"]],
        out_shape=[sd(xs.shape, F32), sd(bm.shape, F32), sd(cm.shape, F32), sd(z.shape, BF16), sd(dtr.shape, F32),
                   sd(dtb.shape, F32), sd(alog.shape, F32), sd(dsk.shape, F32), sd(nw.shape, F32)],
        scratch_shapes=[pltpu.VMEM((D_STATE, rp), F32)],
        compiler_params=_cp("parallel", "arbitrary"),
    )(xs, bm, cm, z, dtr, hsave, dtb, alog, dsk, nw, dyn)


def _s5_param_fn(lam_re, lam_im, log_dt, bt_re, bt_im):
    lr = jnp.minimum(lam_re, EIG_MAX)
    dt = jnp.exp(log_dt)
    mag = jnp.exp(lr * dt)
    lb_re = mag * jnp.cos(lam_im * dt)
    lb_im = mag * jnp.sin(lam_im * dt)
    n_re = lb_re - 1.0
    den = lr * lr + lam_im * lam_im
    k_re = (n_re * lr + lb_im * lam_im) / den
    k_im = (lb_im * lr - n_re * lam_im) / den
    return lb_re, lb_im, k_re * bt_re - k_im * bt_im, k_re * bt_im + k_im * bt_re


def _s5_params(lam_re, lam_im, log_dt, bt_re, bt_im, cts=None, *, name):
    args = (lam_re, lam_im, log_dt, bt_re, bt_im)
    n = len(args)

    def body(*refs):
        vals = [r[...] for r in refs[:n]]
        if cts is None:
            res = _s5_param_fn(*vals)
        else:
            _, vjp = jax.vjp(_s5_param_fn, *vals)
            res = vjp(tuple(r[...] for r in refs[n:n + 4]))
        for o_ref, v in zip(refs[-len(res):], res):
            o_ref[...] = v

    if cts is None:
        out = [lam_re, lam_im, bt_re, bt_im]
        ins = args
    else:
        out = list(args)
        ins = args + tuple(cts)
    return pl.pallas_call(
        body, name=name, out_shape=[jax.ShapeDtypeStruct(a.shape, F32) for a in out],
        compiler_params=pltpu.CompilerParams(vmem_limit_bytes=VMEM_LIMIT),
    )(*ins)


SCAN_COLS = 512


def _cmul(xr, xi, yr, yi):
    return xr * yr - xi * yi, xr * yi + xi * yr


def _scan_consts(a_re, a_im, cols, reverse):
    shape = (SUBLANES, cols)
    row = lax.broadcasted_iota(jnp.int32, shape, 0)
    dist = (SUBLANES - 1 - row) if reverse else row
    mr, mi = jnp.broadcast_to(a_re, shape), jnp.broadcast_to(a_im, shape)
    pr, pi = mr, mi
    mults = []
    for d in (1, 2, 4):
        mults.append((mr, mi))
        qr, qi = _cmul(pr, pi, mr, mi)
        has_bit = (dist & d) != 0
        pr, pi = jnp.where(has_bit, qr, pr), jnp.where(has_bit, qi, pi)
        mr, mi = _cmul(mr, mi, mr, mi)
    return mults, (pr, pi), dist


def _scan_group(xr, xi, consts, cr, ci, reverse):
    mults, (pr, pi), dist = consts
    for d, (mr, mi) in zip((1, 2, 4), mults):
        shift = (SUBLANES - d) if reverse else d
        sr = jnp.where(dist >= d, pltpu.roll(xr, shift, 0), 0.0)
        si = jnp.where(dist >= d, pltpu.roll(xi, shift, 0), 0.0)
        tr, ti = _cmul(mr, mi, sr, si)
        xr, xi = xr + tr, xi + ti
    last = slice(0, 1) if reverse else slice(SUBLANES - 1, SUBLANES)
    nr, ni = _cmul(pr[last], pi[last], cr, ci)
    tr, ti = _cmul(pr, pi, jnp.broadcast_to(cr, xr.shape), jnp.broadcast_to(ci, xr.shape))
    return xr + tr, xi + ti, xr[last] + nr, xi[last] + ni


def _scan_specs(tb, time_of):
    row = pl.BlockSpec((tb, SCAN_COLS), lambda j, i: (time_of(i), j))
    par = pl.BlockSpec((1, SCAN_COLS), lambda j, i: (0, j))
    return row, par


def _s5_scan_fwd(bu_re, bu_im, lb_re, lb_im, *, name, tb=512):
    t, c = bu_re.shape
    nj = c // SCAN_COLS
    row, par = _scan_specs(tb, lambda i: i)

    def body(bre_ref, bim_ref, lre_ref, lim_ref, sre_ref, sim_ref, cre_ref, cim_ref):
        @pl.when(pl.program_id(1) == 0)
        def _():
            cre_ref[...] = jnp.zeros_like(cre_ref)
            cim_ref[...] = jnp.zeros_like(cim_ref)
        consts = _scan_consts(lre_ref[...], lim_ref[...], SCAN_COLS, False)

        def group(k, carry):
            rows = pl.ds(pl.multiple_of(k * SUBLANES, SUBLANES), SUBLANES)
            sr, si, cr, ci = _scan_group(bre_ref[rows, :], bim_ref[rows, :], consts, *carry, False)
            sre_ref[rows, :] = sr
            sim_ref[rows, :] = si
            return cr, ci

        sr, si = lax.fori_loop(0, tb // SUBLANES, group, (cre_ref[...], cim_ref[...]), unroll=4)
        cre_ref[...] = sr
        cim_ref[...] = si

    return pl.pallas_call(
        body, name=name, grid=(nj, t // tb), in_specs=[row, row, par, par], out_specs=[row, row],
        out_shape=[jax.ShapeDtypeStruct((t, c), F32)] * 2,
        scratch_shapes=[pltpu.VMEM((1, SCAN_COLS), F32)] * 2,
        compiler_params=_cp("parallel", "arbitrary"),
    )(bu_re, bu_im, lb_re, lb_im)


def _s5_scan_bwd(s_re, s_im, ds_re, ds_im, lb_re, lb_im, *, name, tb=512):
    t, c = s_re.shape
    nj = c // SCAN_COLS
    nt = t // tb
    rev = lambda i: nt - 1 - i
    row, par = _scan_specs(tb, rev)
    prev = pl.BlockSpec((HALO, SCAN_COLS), lambda j, i: (jnp.maximum(rev(i) * (tb // HALO) - 1, 0), j))

    def body(sre_ref, sim_ref, pre_ref, pim_ref, dre_ref, dim_ref, lre_ref, lim_ref,
             gre_ref, gim_ref, dlre_ref, dlim_ref, cre_ref, cim_ref, ext_re, ext_im):
        step_id = pl.program_id(1)

        @pl.when(step_id == 0)
        def _():
            cre_ref[...] = jnp.zeros_like(cre_ref)
            cim_ref[...] = jnp.zeros_like(cim_ref)
            dlre_ref[...] = jnp.zeros_like(dlre_ref)
            dlim_ref[...] = jnp.zeros_like(dlim_ref)
        consts = _scan_consts(lre_ref[...], -lim_ref[...], SCAN_COLS, True)
        ngroups = tb // SUBLANES

        def group(k, carry):
            rows = pl.ds(pl.multiple_of((ngroups - 1 - k) * SUBLANES, SUBLANES), SUBLANES)
            gr, gi, cr, ci = _scan_group(dre_ref[rows, :], dim_ref[rows, :], consts, *carry, True)
            gre_ref[rows, :] = gr
            gim_ref[rows, :] = gi
            return cr, ci

        gr, gi = lax.fori_loop(0, ngroups, group, (cre_ref[...], cim_ref[...]), unroll=4)
        cre_ref[...] = gr
        cim_ref[...] = gi
        has_past = step_id != nt - 1
        ext_re[pl.ds(0, HALO), :] = jnp.where(has_past, pre_ref[...], 0.0)
        ext_im[pl.ds(0, HALO), :] = jnp.where(has_past, pim_ref[...], 0.0)
        ext_re[pl.ds(HALO, tb), :] = sre_ref[...]
        ext_im[pl.ds(HALO, tb), :] = sim_ref[...]
        pr, pi = ext_re[pl.ds(HALO - 1, tb), :], ext_im[pl.ds(HALO - 1, tb), :]
        g_re, g_im = gre_ref[...], gim_ref[...]
        dlre_ref[...] += jnp.sum(pr * g_re + pi * g_im, axis=0, keepdims=True)
        dlim_ref[...] += jnp.sum(pr * g_im - pi * g_re, axis=0, keepdims=True)

    return pl.pallas_call(
        body, name=name, grid=(nj, nt),
        in_specs=[row, row, prev, prev, row, row, par, par], out_specs=[row, row, par, par],
        out_shape=[jax.ShapeDtypeStruct((t, c), F32)] * 2 + [jax.ShapeDtypeStruct((1, c), F32)] * 2,
        scratch_shapes=[pltpu.VMEM((1, SCAN_COLS), F32)] * 2 + [pltpu.VMEM((HALO + tb, SCAN_COLS), F32)] * 2,
        compiler_params=_cp("parallel", "arbitrary"),
    )(s_re, s_im, s_re, s_im, ds_re, ds_im, lb_re, lb_im)


def _loss_fn(h, w, tgt):
    err = _rms_fn(h, w)[0] - tgt
    return 0.5 * jnp.sum(jnp.mean(err * err, axis=-1, keepdims=True), axis=0, keepdims=True)


def _loss_head(h, w, tgt, *, name, tb=256):
    t, d = h.shape

    def body(h_ref, w_ref, t_ref, loss_ref, dh_ref, dhb_ref, dw_ref):
        @pl.when(pl.program_id(0) == 0)
        def _():
            loss_ref[...] = jnp.zeros_like(loss_ref)
            dw_ref[...] = jnp.zeros_like(dw_ref)
        part, vjp = jax.vjp(_loss_fn, h_ref[...], w_ref[...], t_ref[...])
        dh, dw, _ = vjp(jnp.ones((1, 1), F32))
        loss_ref[...] += jnp.broadcast_to(part, loss_ref.shape)
        dh_ref[...] = dh
        dhb_ref[...] = dh.astype(BF16)
        dw_ref[...] += dw

    row = pl.BlockSpec((tb, d), lambda i: (i, 0))
    par = pl.BlockSpec((1, d), lambda i: (0, 0))
    return pl.pallas_call(
        body, name=name, grid=(t // tb,), in_specs=[row, par, row],
        out_specs=[pl.BlockSpec((SUBLANES, LANES), lambda i: (0, 0)), row, row, par],
        out_shape=[jax.ShapeDtypeStruct((SUBLANES, LANES), F32), jax.ShapeDtypeStruct((t, d), F32),
                   jax.ShapeDtypeStruct((t, d), BF16), jax.ShapeDtypeStruct((1, d), F32)],
        compiler_params=_cp("arbitrary"),
    )(h, w, tgt)


def _adamw(w, g, m, v, *, name):
    r, c = w.shape
    tr = _tile(r, 256, SUBLANES)

    def body(w_ref, g_ref, m_ref, v_ref, d_ref, nm_ref, nv_ref):
        g = g_ref[...]
        nm = ADAM_B1 * m_ref[...] + (1.0 - ADAM_B1) * g
        nv = ADAM_B2 * v_ref[...] + (1.0 - ADAM_B2) * (g * g)
        m_hat = nm / (1.0 - ADAM_B1 ** ADAM_STEP)
        v_hat = nv / (1.0 - ADAM_B2 ** ADAM_STEP)
        d_ref[...] = -ADAM_LR * (m_hat / (jnp.sqrt(v_hat) + ADAM_EPS) + ADAM_WD * w_ref[...])
        nm_ref[...] = nm
        nv_ref[...] = nv

    spec = pl.BlockSpec((tr, c), lambda i: (i, 0))
    return pl.pallas_call(
        body, name=name, grid=(r // tr,), in_specs=[spec] * 4, out_specs=[spec] * 3,
        out_shape=[jax.ShapeDtypeStruct((r, c), F32)] * 3, compiler_params=_cp("parallel"),
    )(w, g, m, v)


def _sum_parts(parts, *, name):
    _, r, c = parts.shape
    tr = _tile(r, 128, SUBLANES)

    def body(p_ref, o_ref):
        acc = p_ref[0].astype(F32)
        for k in range(1, N_DEV):
            acc = acc + p_ref[k].astype(F32)
        o_ref[...] = acc

    return pl.pallas_call(
        body, name=name, grid=(r // tr,), in_specs=[pl.BlockSpec((N_DEV, tr, c), lambda i: (0, i, 0))],
        out_specs=pl.BlockSpec((tr, c), lambda i: (i, 0)), out_shape=jax.ShapeDtypeStruct((r, c), F32),
        compiler_params=_cp("parallel"),
    )(parts)


def _position():
    return lax.axis_index("x"), lax.axis_index("y"), lax.axis_index("c")


def _flat(px, py, pc):
    return 4 * px + 2 * py + pc


def _all_gather(shard, *, name):
    def body(x_ref, out_ref, token, send_sems, recv_sems, local_sem):
        token[...] = jnp.zeros_like(token)
        x, y, c = _position()
        me, sibling = (x, y, c), (x, y, 1 - c)
        chips = [(1 - x, y), (x, 1 - y), (1 - x, 1 - y)]

        def copy(k, block, to, src=None):
            slot = out_ref.at[_flat(*block)]
            return pltpu.make_async_remote_copy(
                src_ref=slot if src is None else src, dst_ref=slot, send_sem=send_sems.at[k],
                recv_sem=recv_sems.at[k], device_id=to, device_id_type=MESH)

        mine = pltpu.make_async_copy(x_ref, out_ref.at[_flat(*me)], local_sem)
        mine.start()
        first = [copy(0, me, sibling, src=x_ref)]
        first += [copy(1 + j, me, (*chip, c), src=x_ref) for j, chip in enumerate(chips)]
        for cp in first:
            cp.start()
        passed = [copy(4 + j, (*chip, c), sibling) for j, chip in enumerate(chips)]
        for j, chip in enumerate(chips):
            copy(1 + j, (*chip, c), me).wait_recv()
            passed[j].start()
        copy(0, sibling, me).wait_recv()
        for j, chip in enumerate(chips):
            copy(4 + j, (*chip, 1 - c), me).wait_recv()
        for cp in first + passed:
            cp.wait_send()
        mine.wait()

    return pl.pallas_call(
        body, name=name,
        out_shape=(jax.ShapeDtypeStruct((N_DEV,) + shard.shape, shard.dtype),
                   jax.ShapeDtypeStruct((SUBLANES, LANES), F32)),
        in_specs=[pl.BlockSpec(memory_space=pl.ANY)],
        out_specs=(pl.BlockSpec(memory_space=pl.ANY), pl.BlockSpec(memory_space=pltpu.VMEM)),
        scratch_shapes=[pltpu.SemaphoreType.DMA((7,)), pltpu.SemaphoreType.DMA((7,)), pltpu.SemaphoreType.DMA(())],
    )(shard)


_HBM = pl.BlockSpec(memory_space=pltpu.HBM)
_SEM = pl.BlockSpec(memory_space=pltpu.SEMAPHORE)
_EFFECT = pltpu.SideEffectType.DATAFLOW_SIDE_EFFECTING


def _copy_ends(src_ref, land_ref, mode, me, to):
    if mode == "gather_slot":
        return src_ref, land_ref.at[me]
    if mode == "gather_cols":
        w = src_ref.shape[1]
        return src_ref, land_ref.at[:, pl.ds(pl.multiple_of(me * w, LANES), w)]
    if mode == "scatter_slot":
        return src_ref.at[to], land_ref.at[me]
    w = land_ref.shape[2]
    return src_ref.at[:, pl.ds(pl.multiple_of(to * w, LANES), w)], land_ref.at[me]


BF16_ROWS = 16


def _land_shape(src, mode):
    if mode == "gather_slot":
        return (N_DEV,) + src.shape
    if mode == "gather_cols":
        return (src.shape[0], N_DEV * src.shape[1])
    if mode == "scatter_slot":
        return src.shape
    return (N_DEV, src.shape[0], src.shape[1] // N_DEV)


def _exchange_copies(src_ref, land_ref, send_sems, recv_sems, mode):
    x, y, c = _position()
    me = _flat(x, y, c)
    copies = []
    for k in range(1, N_DEV):
        peer = (x ^ ((k >> 2) & 1), y ^ ((k >> 1) & 1), c ^ (k & 1))
        src, dst = _copy_ends(src_ref, land_ref, mode, me, _flat(*peer))
        copies.append(pltpu.make_async_remote_copy(
            src_ref=src, dst_ref=dst, send_sem=send_sems.at[k - 1], recv_sem=recv_sems.at[k - 1],
            device_id=peer, device_id_type=MESH))
    return copies


def _place_own(src, mode, dev, *, name):
    rows = src.shape[1] if mode == "scatter_slot" else src.shape[0]
    tr = _tile(rows, 512, BF16_ROWS)
    land = _land_shape(src, mode)
    width = land[-1] if mode.startswith("scatter") else src.shape[1]
    slot = pl.BlockSpec((1, tr, width), lambda i, d: (d[0], i, 0))
    cols = pl.BlockSpec((tr, width), lambda i, d: (i, d[0]))
    whole = pl.BlockSpec((tr, width), lambda i, d: (i, 0))
    in_spec, out_spec = {"gather_slot": (whole, slot), "gather_cols": (whole, cols), "scatter_slot": (slot, slot),
                         "scatter_cols": (cols, slot)}[mode]

    def body(dev_ref, src_ref, land_ref):
        land_ref[...] = src_ref[...].reshape(land_ref.shape)

    return pl.pallas_call(
        body, name=name, out_shape=jax.ShapeDtypeStruct(land, src.dtype),
        grid_spec=pltpu.PrefetchScalarGridSpec(num_scalar_prefetch=1, grid=(rows // tr,), in_specs=[in_spec],
                                               out_specs=out_spec),
        compiler_params=_cp("parallel"),
    )(dev, src)


def _exchange_start(src, mode, dev, *, name):
    land = _place_own(src, mode, dev, name=name + "_own")
    n_copies = N_DEV - 1

    def body(src_ref, land_ref, send_sems, recv_sems, src_thru, land_thru, token):
        for cp in _exchange_copies(src_ref, land_ref, send_sems, recv_sems, mode):
            cp.start()
        token[...] = jnp.zeros_like(token)

    hbm = pltpu.with_memory_space_constraint
    *handle, token = pl.pallas_call(
        body, name=name,
        out_shape=(pltpu.SemaphoreType.DMA((n_copies,)), pltpu.SemaphoreType.DMA((n_copies,)),
                   pltpu.HBM(src.shape, src.dtype), pltpu.HBM(land.shape, land.dtype),
                   jax.ShapeDtypeStruct((SUBLANES, LANES), F32)),
        in_specs=(_HBM, _HBM), out_specs=(_SEM, _SEM, _HBM, _HBM, pl.BlockSpec(memory_space=pltpu.VMEM)),
        input_output_aliases={0: 2, 1: 3}, compiler_params=pltpu.CompilerParams(has_side_effects=_EFFECT),
    )(hbm(src, pltpu.HBM), hbm(land, pltpu.HBM))
    return (tuple(handle), mode), token


def _exchange_wait(pending, after, *, name):
    (send_sems, recv_sems, src_thru, land_thru), mode = pending

    def body(src_ref, land_ref, send_sems, recv_sems, after_ref, src_dead, got_ref):
        for cp in _exchange_copies(src_ref, land_ref, send_sems, recv_sems, mode):
            cp.wait_send()
            cp.wait_recv()

    return pl.pallas_call(
        body, name=name, out_shape=(pltpu.HBM(src_thru.shape, src_thru.dtype), pltpu.HBM(land_thru.shape, land_thru.dtype)),
        in_specs=(_HBM, _HBM, _SEM, _SEM, pl.BlockSpec(memory_space=pl.ANY)), out_specs=(_HBM, _HBM),
        input_output_aliases={0: 0, 1: 1}, compiler_params=pltpu.CompilerParams(has_side_effects=_EFFECT),
    )(src_thru, land_thru, send_sems, recv_sems, after)[1]


def _after(x, token):
    return x + token[0, 0].astype(x.dtype)


def _pad_cols(a, mult):
    pad = -a.shape[1] % mult
    return jnp.pad(a, ((0, 0), (0, pad))) if pad else a


def _pack(arrs, cols):
    flat = jnp.concatenate([a.reshape(-1).astype(F32) for a in arrs])
    sizes = [int(a.size) for a in arrs]
    flat = jnp.pad(flat, (0, -flat.shape[0] % (SUBLANES * cols)))
    return flat.reshape(-1, cols), sizes


def _unpack(flat2d, sizes, shapes):
    flat = flat2d.reshape(-1)
    out, o = [], 0
    for n, s in zip(sizes, shapes):
        out.append(flat[o:o + n].reshape(s))
        o += n
    return out


PACK_COLS = SUBLANES * LANES


def kernel(x, norm_mix_w, w_in, conv_a_w, conv_a_b, dt_bias, a_log, d_a, norm_a_w, w_proj_a, s5_lam_re, s5_lam_im, s5_log_dt, s5_b_re, s5_b_im, s5_c_re, s5_c_im, s5_d, w_s5_glu, w_out, norm_ffn_w, w_up, conv_ffn_w, conv_ffn_b, w_down, norm_final_w, loss_target, m_norm_mix_w, m_w_in, m_conv_a_w, m_conv_a_b, m_dt_bias, m_a_log, m_d_a, m_norm_a_w, m_w_proj_a, m_s5_lam_re, m_s5_lam_im, m_s5_log_dt, m_s5_b_re, m_s5_b_im, m_s5_c_re, m_s5_c_im, m_s5_d, m_w_s5_glu, m_w_out, m_norm_ffn_w, m_w_up, m_conv_ffn_w, m_conv_ffn_b, m_w_down, m_norm_final_w, v_norm_mix_w, v_w_in, v_conv_a_w, v_conv_a_b, v_dt_bias, v_a_log, v_d_a, v_norm_a_w, v_w_proj_a, v_s5_lam_re, v_s5_lam_im, v_s5_log_dt, v_s5_b_re, v_s5_b_im, v_s5_c_re, v_s5_c_im, v_s5_d, v_w_s5_glu, v_w_out, v_norm_ffn_w, v_w_up, v_conv_ffn_w, v_conv_ffn_b, v_w_down, v_norm_final_w):
    weights = dict(norm_mix_w=norm_mix_w, w_in=w_in, conv_a_w=conv_a_w, conv_a_b=conv_a_b, dt_bias=dt_bias, a_log=a_log, d_a=d_a, norm_a_w=norm_a_w, w_proj_a=w_proj_a, s5_lam_re=s5_lam_re, s5_lam_im=s5_lam_im, s5_log_dt=s5_log_dt, s5_b_re=s5_b_re, s5_b_im=s5_b_im, s5_c_re=s5_c_re, s5_c_im=s5_c_im, s5_d=s5_d, w_s5_glu=w_s5_glu, w_out=w_out, norm_ffn_w=norm_ffn_w, w_up=w_up, conv_ffn_w=conv_ffn_w, conv_ffn_b=conv_ffn_b, w_down=w_down, norm_final_w=norm_final_w)
    moms = dict(norm_mix_w=m_norm_mix_w, w_in=m_w_in, conv_a_w=m_conv_a_w, conv_a_b=m_conv_a_b, dt_bias=m_dt_bias, a_log=m_a_log, d_a=m_d_a, norm_a_w=m_norm_a_w, w_proj_a=m_w_proj_a, s5_lam_re=m_s5_lam_re, s5_lam_im=m_s5_lam_im, s5_log_dt=m_s5_log_dt, s5_b_re=m_s5_b_re, s5_b_im=m_s5_b_im, s5_c_re=m_s5_c_re, s5_c_im=m_s5_c_im, s5_d=m_s5_d, w_s5_glu=m_w_s5_glu, w_out=m_w_out, norm_ffn_w=m_norm_ffn_w, w_up=m_w_up, conv_ffn_w=m_conv_ffn_w, conv_ffn_b=m_conv_ffn_b, w_down=m_w_down, norm_final_w=m_norm_final_w)
    vars_ = dict(norm_mix_w=v_norm_mix_w, w_in=v_w_in, conv_a_w=v_conv_a_w, conv_a_b=v_conv_a_b, dt_bias=v_dt_bias, a_log=v_a_log, d_a=v_d_a, norm_a_w=v_norm_a_w, w_proj_a=v_w_proj_a, s5_lam_re=v_s5_lam_re, s5_lam_im=v_s5_lam_im, s5_log_dt=v_s5_log_dt, s5_b_re=v_s5_b_re, s5_b_im=v_s5_b_im, s5_c_re=v_s5_c_re, s5_c_im=v_s5_c_im, s5_d=v_s5_d, w_s5_glu=v_w_s5_glu, w_out=v_w_out, norm_ffn_w=v_norm_ffn_w, w_up=v_w_up, conv_ffn_w=v_conv_ffn_w, conv_ffn_b=v_conv_ffn_b, w_down=v_w_down, norm_final_w=v_norm_final_w)
    names = list(weights)
    col_sharded = ("w_in", "w_s5_glu", "w_up")
    row_sharded = ("w_proj_a", "w_out", "w_down")
    conv_sharded = ("conv_a_w", "conv_ffn_w")
    replicated = [n for n in names if n not in col_sharded + row_sharded + conv_sharded]

    t, d = x.shape[1:]
    x2, tgt = x.reshape(t, d), loss_target.reshape(t, d)
    nh = dt_bias.shape[-1]
    d_inner = norm_a_w.shape[-1]
    conv_dim = conv_a_b.shape[-1]
    gn = (conv_dim - d_inner) // 2
    ng = gn // D_STATE
    nr = nh // ng
    rp = d_inner // ng
    d_s5 = s5_d.shape[-1]
    gs, ps = s5_lam_re.shape[1:]
    cs = d_s5 // gs
    n_oct = gs // 8
    assert (gs * ps) % SCAN_COLS == 0 and 8 * cs == LANES and gs % 8 == 0
    d_ff = w_down.shape[1] * N_DEV
    dev = _flat(*_position())
    dev1 = dev.reshape(1).astype(jnp.int32)

    ka, kf = conv_a_w.shape[1], conv_ffn_w.shape[1]
    taps = jnp.concatenate([conv_a_w[0].reshape(1, -1), conv_ffn_w[0].reshape(1, -1)], axis=1)
    taps, taps_done = _all_gather(taps, name="ag_conv_taps")
    taps = taps[:, 0]

    def by_cols(n):
        return n in ("w_s5_glu", "w_up") and weights[n].shape[2] % LANES == 0

    w_in_blocks, started = _all_gather(_after(w_in[0], taps_done).astype(BF16), name="ag_w_in")
    pending = {}
    for n in ("w_proj_a", "w_s5_glu", "w_out", "w_up", "w_down"):
        shard = _after(weights[n][0], started).astype(BF16)
        pending[n], token = _exchange_start(shard, "gather_cols" if by_cols(n) else "gather_slot", dev1,
                                            name="ag_" + n)
        started = started + token

    def gathered(n, after):
        g = _exchange_wait(pending[n], after, name="agw_" + n)
        if by_cols(n):
            return g
        if n in row_sharded:
            return g.reshape(-1, g.shape[2])
        return jnp.transpose(g, (1, 0, 2)).reshape(g.shape[1], -1)

    seg_sizes = dict(z=d_inner, xs=d_inner, bm=gn, cm=gn, dt=nh, u=d_s5, ga=d, gb=d)
    seg_names = tuple(seg_sizes)
    pieces, seg_at = _w_in_pieces(seg_sizes, ("z", "xs", "ga", "gb", "bm", "cm", "u", "dt"), w_in.shape[2])
    na = ka * conv_a_w.shape[2]
    cw_a = jnp.transpose(taps[:, :na].reshape(N_DEV, ka, -1), (1, 0, 2)).reshape(ka, conv_dim)
    cw_f = jnp.transpose(taps[:, na:].reshape(N_DEV, kf, -1), (1, 0, 2)).reshape(kf, 2 * d_ff)
    cb_a, cb_f = conv_a_b, conv_ffn_b
    a_cols = {"xs": slice(0, d_inner), "bm": slice(d_inner, d_inner + gn), "cm": slice(d_inner + gn, conv_dim)}

    w1 = norm_mix_w.reshape(1, 1, d) + started[0, 0]
    hn1, = _blocked_fwd(_rms_fn, [x2], [w1], [(d, BF16)], tb=256, name="rms1")
    w_in_p = _w_in_pack(w_in_blocks, pieces, seg_at, seg_sizes, name="w_in_pack")
    pre = {sn: _mm(hn1, w_in_p, b_win=seg_at[sn], name="in_" + sn) for sn in seg_names}
    act_a = {sn: _conv_fwd(_comb_silu, [pre[sn]], [cw_a[:, a_cols[sn]]], [cb_a[:, a_cols[sn]]], out_dtype=F32,
                           name="conv_a_" + sn) for sn in a_cols}
    dtr3 = jnp.transpose(pre["dt"][:, :nh].reshape(t, ng, nr), (1, 0, 2))
    dtb3, alog3, dsk3 = (p.reshape(ng, 1, nr) for p in (dt_bias, a_log, d_a))
    nw3 = norm_a_w.reshape(ng, 1, rp)
    yn, hsave = _ssd_fwd(act_a["xs"], act_a["bm"], act_a["cm"], pre["z"], dtr3, dtb3, alog3, dsk3, nw3, name="ssd")
    w_proj = gathered("w_proj_a", yn)
    y_a = _mm(yn, w_proj, name="proj_a")

    lam_re3, lam_im3 = s5_lam_re[0][:, None, :], s5_lam_im[0][:, None, :]
    logdt3 = s5_log_dt[0][:, None, None]
    bt_re, bt_im = jnp.transpose(s5_b_re[0], (0, 2, 1)), jnp.transpose(s5_b_im[0], (0, 2, 1))
    lb_re3, lb_im3, bb_re, bb_im = _s5_params(lam_re3, lam_im3, logdt3, bt_re, bt_im, name="s5_params")
    eye = jnp.eye(8, dtype=F32)

    def diag_b(bt):
        return (bt.reshape(n_oct, 8, cs, 1, ps) * eye[None, :, None, :, None]).reshape(n_oct, 8 * cs, 8 * ps)

    def undiag_b(blk):
        return (blk.reshape(n_oct, 8, cs, 8, ps) * eye[None, :, None, :, None]).sum(axis=3).reshape(gs, cs, ps)

    def diag_c(cm):
        ct = jnp.transpose(cm.reshape(n_oct, 8, cs, ps), (0, 1, 3, 2))
        return (ct[:, :, :, None, :] * eye[None, :, None, :, None]).reshape(n_oct, 8 * ps, 8 * cs)

    def undiag_c(blk):
        ct = (blk.reshape(n_oct, 8, ps, 8, cs) * eye[None, :, None, :, None]).sum(axis=3)
        return jnp.transpose(ct, (0, 1, 3, 2)).reshape(gs, cs, ps)

    b_blk_re, b_blk_im = diag_b(bb_re), diag_b(bb_im)
    c_blk_re, c_blk_imn = diag_c(s5_c_re[0]), diag_c(-s5_c_im[0])
    d3 = s5_d.reshape(n_oct, 1, LANES)
    lb_re, lb_im = lb_re3.reshape(1, gs * ps), lb_im3.reshape(1, gs * ps)
    u = pre["u"]
    bu_re, bu_im = _blocked_fwd(_s5_bu_fn, [u], [b_blk_re, b_blk_im], [(gs * ps, F32)] * 2, nj=n_oct, tb=512,
                                name="s5_bu")
    s_re, s_im = _s5_scan_fwd(bu_re, bu_im, lb_re, lb_im, name="s5_scan")
    yb, = _blocked_fwd(_s5_out_fn, [s_re, s_im, u], [c_blk_re, c_blk_imn, d3], [(d_s5, BF16)], nj=n_oct, tb=512,
                       name="s5_out")
    w_glu = gathered("w_s5_glu", yb)
    glu_v = _mm(yb, w_glu, b_win=(0, d), name="glu_v")
    glu_g = _mm(yb, w_glu, b_win=(d, d), name="glu_g")
    merged, = _blocked_fwd(_merge_fn, [glu_v, glu_g, pre["ga"], pre["gb"], y_a], [], [(d, BF16)], tb=256,
                           name="merge")
    w_o = gathered("w_out", merged)
    h1 = _mm(merged, w_o, acc=x2, name="out_proj")
    w2 = norm_ffn_w.reshape(1, 1, d)
    hn2, = _blocked_fwd(_rms_fn, [h1], [w2], [(d, BF16)], tb=256, name="rms2")
    w_u = gathered("w_up", hn2)
    up_g = _mm(hn2, w_u, b_win=(0, d_ff), name="up_g")
    up_v = _mm(hn2, w_u, b_win=(d_ff, d_ff), name="up_v")
    f_w = [cw_f[:, :d_ff], cw_f[:, d_ff:]]
    f_b = [cb_f[:, :d_ff], cb_f[:, d_ff:]]
    act = _conv_fwd(_comb_glu, [up_g, up_v], f_w, f_b, out_dtype=BF16, name="conv_ffn")
    w_dn = gathered("w_down", act)
    h2 = _mm(act, w_dn, acc=h1, name="down")
    loss_tile, dh2, dh2_b, g_final = _loss_head(h2, norm_final_w.reshape(1, d), tgt, name="loss_head")

    grads, scattering = {}, {}

    def scatter_start(n, g):
        if by_cols(n):
            src, mode = g, "scatter_cols"
        elif n in row_sharded:
            src, mode = g.reshape(N_DEV, -1, g.shape[1]), "scatter_slot"
        elif n == "w_in":
            src, mode = g, "scatter_slot"
        else:
            src, mode = jnp.transpose(g.reshape(g.shape[0], N_DEV, -1), (1, 0, 2)), "scatter_slot"
        scattering[n], token = _exchange_start(src, mode, dev1, name="rs_" + n)
        return token

    d_act = _mm(dh2_b, w_dn, tb=True, name="d_act")
    g_down = _mm(act, dh2_b, ta=True, out_dtype=BF16, name="g_w_down")
    tok = scatter_start("w_down", g_down)
    (dup_g, dwf_g, dbf_g), (dup_v, dwf_v, dbf_v) = _conv_bwd(
        _comb_glu, [up_g, up_v], f_w, [_after(f_b[0], tok), f_b[1]], d_act, dx_dtype=BF16, name="conv_ffn_bwd")
    dhn2 = _mm(dup_g, w_u, tb=True, b_win=(0, d_ff), name="d_hn2_g")
    dhn2 = _mm(dup_v, w_u, tb=True, b_win=(d_ff, d_ff), acc=dhn2, name="d_hn2_v")
    g_up = _mm(hn2, dup_g, ta=True, into=(lax.empty((d, 2 * d_ff), BF16), 0), name="g_w_up_g")
    g_up = _mm(hn2, dup_v, ta=True, into=(g_up, d_ff), name="g_w_up_v")
    tok = scatter_start("w_up", g_up)
    dh1, dh1_b, g_w2 = _blocked_bwd(_rms_fn, [h1], [_after(w2, tok)], [dhn2], [(F32, BF16)], adds={0: dh2}, tb=256,
                                    name="rms2_bwd")
    d_merged = _mm(dh1_b, w_o, tb=True, name="d_merged")
    g_out = _mm(merged, dh1_b, ta=True, out_dtype=BF16, name="g_w_out")
    tok = scatter_start("w_out", g_out)
    dglu_v, dglu_g, dga, dgb, dy_a = _blocked_bwd(
        _merge_fn, [glu_v, glu_g, pre["ga"], pre["gb"], y_a], [], [d_merged], [BF16] * 5, tb=128, name="merge_bwd")
    dyb = _mm(dglu_v, w_glu, tb=True, b_win=(0, d), name="d_yb_v")
    dyb = _mm(dglu_g, w_glu, tb=True, b_win=(d, d), acc=dyb, name="d_yb_g")
    g_glu = _mm(yb, dglu_v, ta=True, into=(lax.empty((d_s5, 2 * d), BF16), 0), name="g_w_glu_v")
    g_glu = _mm(yb, dglu_g, ta=True, into=(g_glu, d), name="g_w_glu_g")
    tok = tok + scatter_start("w_s5_glu", g_glu)
    ds_re, ds_im, du_skip, dc_blk_re, dc_blk_imn, dd3 = _blocked_bwd(
        _s5_out_fn, [s_re, s_im, u], [c_blk_re, c_blk_imn, _after(d3, tok)], [dyb], [F32, F32, F32], nj=n_oct, tb=512,
        name="s5_out_bwd")
    dbu_re, dbu_im, dlb_re, dlb_im = _s5_scan_bwd(s_re, s_im, ds_re, ds_im, lb_re, lb_im, name="s5_scan_bwd")
    du, db_blk_re, db_blk_im = _blocked_bwd(
        _s5_bu_fn, [u], [b_blk_re, b_blk_im], [dbu_re, dbu_im], [BF16], adds={0: du_skip}, nj=n_oct, tb=512,
        name="s5_bu_bwd")
    g_lre, g_lim, g_ldt, g_bt_re, g_bt_im = _s5_params(
        lam_re3, lam_im3, logdt3, bt_re, bt_im,
        cts=(dlb_re.reshape(gs, 1, ps), dlb_im.reshape(gs, 1, ps), undiag_b(db_blk_re), undiag_b(db_blk_im)),
        name="s5_params_bwd")
    grads["s5_lam_re"], grads["s5_lam_im"] = g_lre.reshape(s5_lam_re.shape), g_lim.reshape(s5_lam_im.shape)
    grads["s5_log_dt"] = g_ldt.reshape(s5_log_dt.shape)
    grads["s5_b_re"] = jnp.transpose(g_bt_re, (0, 2, 1)).reshape(s5_b_re.shape)
    grads["s5_b_im"] = jnp.transpose(g_bt_im, (0, 2, 1)).reshape(s5_b_im.shape)
    grads["s5_c_re"] = undiag_c(dc_blk_re).reshape(s5_c_re.shape)
    grads["s5_c_im"] = -undiag_c(dc_blk_imn).reshape(s5_c_im.shape)
    grads["s5_d"] = dd3.reshape(s5_d.shape)

    dyn = _mm(dy_a, w_proj, tb=True, name="d_yn")
    g_proj = _mm(yn, dy_a, ta=True, out_dtype=BF16, name="g_w_proj_a")
    tok = scatter_start("w_proj_a", g_proj)
    dxs, dbm, dcm, dz, ddtr3, g_dtb, g_alog, g_dsk, g_nw = _ssd_bwd(
        act_a["xs"], act_a["bm"], act_a["cm"], pre["z"], dtr3, hsave, dtb3, alog3, dsk3, _after(nw3, tok), dyn,
        name="ssd_bwd")
    grads["dt_bias"], grads["a_log"], grads["d_a"] = (g.reshape(1, nh) for g in (g_dtb, g_alog, g_dsk))
    grads["norm_a_w"] = g_nw.reshape(1, d_inner)
    dpre = {"z": dz, "u": du, "ga": dga, "gb": dgb}
    dcw, dcb = {}, {}
    for sn, dact in (("xs", dxs), ("bm", dbm), ("cm", dcm)):
        (dpre[sn], dcw[sn], dcb[sn]), = _conv_bwd(
            _comb_silu, [pre[sn]], [cw_a[:, a_cols[sn]]], [cb_a[:, a_cols[sn]]], dact, dx_dtype=BF16,
            name="conv_a_bwd_" + sn)
    dpre["dt"] = _pad_cols(jnp.transpose(ddtr3, (1, 0, 2)).reshape(t, nh), LANES).astype(BF16)
    g_in = _w_in_unpack({sn: _mm(hn1, dpre[sn], ta=True, name="g_w_in_" + sn) for sn in seg_names}, pieces,
                        w_in.shape[2], name="w_in_unpack")
    tok = scatter_start("w_in", g_in)
    dhn1 = _mm(_after(dpre["dt"], tok), w_in_p, tb=True, b_win=seg_at["dt"], name="d_hn1_dt")
    for sn in seg_names:
        if sn != "dt":
            dhn1 = _mm(dpre[sn], w_in_p, tb=True, b_win=seg_at[sn], acc=dhn1, name="d_hn1_" + sn)
    dx, g_w1 = _blocked_bwd(_rms_fn, [x2], [w1], [dhn1], [F32], adds={0: dh1}, tb=256, name="rms1_bwd")

    grads["norm_mix_w"], grads["norm_ffn_w"] = g_w1.reshape(1, d), g_w2.reshape(1, d)
    grads["norm_final_w"] = g_final.reshape(d)
    grads["conv_a_b"] = jnp.concatenate([dcb["xs"], dcb["bm"], dcb["cm"]], axis=1)
    grads["conv_ffn_b"] = jnp.concatenate([dbf_g, dbf_v], axis=1)
    g_cw_a = jnp.concatenate([dcw["xs"], dcw["bm"], dcw["cm"]], axis=1)
    g_cw_f = jnp.concatenate([dwf_g, dwf_v], axis=1)

    small = [grads[n] for n in replicated] + [g_cw_a, g_cw_f, loss_tile[:1, :1]]
    packed, sizes = _pack(small, PACK_COLS)
    summed = _sum_parts(_all_gather(packed, name="ag_small_grads")[0], name="sum_small_grads")
    *rep_sums, s_cw_a, s_cw_f, loss = _unpack(summed, sizes, [a.shape for a in small])
    for n, g in zip(replicated, rep_sums):
        grads[n] = g
    wa, wf = conv_a_w.shape[2], conv_ffn_w.shape[2]
    grads["conv_a_w"] = lax.dynamic_slice_in_dim(s_cw_a, dev * wa, wa, axis=1)[None]
    grads["conv_ffn_w"] = lax.dynamic_slice_in_dim(s_cw_f, dev * wf, wf, axis=1)[None]

    delta, new_m, new_v = {}, {}, {}
    done = dx
    for n in ("w_down", "w_up", "w_out", "w_s5_glu", "w_proj_a", "w_in"):
        land = _exchange_wait(scattering[n], done, name="rsw_" + n)
        shape = weights[n].shape
        two_d = lambda a: a.reshape(shape[-2], shape[-1])
        g = _sum_parts(land, name="rs_sum_" + n)
        grads[n] = g.reshape(shape)
        dl, nm, nv = _adamw(two_d(weights[n]), g, two_d(moms[n]), two_d(vars_[n]), name="adamw_" + n)
        delta[n], new_m[n], new_v[n] = dl.reshape(shape), nm.reshape(shape), nv.reshape(shape)
        done = dl
    for n in replicated + list(conv_sharded):
        shape = weights[n].shape
        two_d = lambda a: a.reshape(-1, shape[-1])
        dl, nm, nv = _adamw(two_d(weights[n]), two_d(grads[n]), two_d(moms[n]), two_d(vars_[n]), name="adamw_" + n)
        delta[n], new_m[n], new_v[n] = dl.reshape(shape), nm.reshape(shape), nv.reshape(shape)

    return (loss.reshape(()), dx.reshape(x.shape), *[grads[n] for n in names], *[delta[n] for n in names],
            *[new_m[n] for n in names], *[new_v[n] for n in names])
```

```python
import functools

import jax
import jax.numpy as jnp
from jax import lax
from jax.experimental import pallas as pl
from jax.experimental.pallas import tpu as pltpu

F32 = jnp.float32
BF16 = jnp.bfloat16
HIGHEST = lax.Precision.HIGHEST
MESH = pl.DeviceIdType.MESH

EPS = 1e-6
EIG_MAX = -1e-4
D_STATE = 128
CHUNK = 256
ADAM_LR = 0.001
ADAM_B1 = 0.9
ADAM_B2 = 0.999
ADAM_EPS = 1e-08
ADAM_WD = 0.01
ADAM_STEP = 10
N_DEV = 8
LANES = 128
SUBLANES = 8
VMEM_LIMIT = 56 * 1024 * 1024
MM_MAX_K = 4096


def _cp(*sem):
    return pltpu.CompilerParams(dimension_semantics=sem, vmem_limit_bytes=VMEM_LIMIT)


def _tile(dim, pref, unit=LANES):
    if dim <= unit:
        return dim
    t = (min(pref, dim) // unit) * unit
    while dim % t:
        t -= unit
    return t


_DIMS = {"nn": (((1,), (0,)), ((), ())), "nt": (((1,), (1,)), ((), ())), "tn": (((0,), (0,)), ((), ()))}


def _dot(a, b, kind):
    return lax.dot_general(a.astype(BF16), b.astype(BF16), _DIMS[kind], preferred_element_type=F32)


@functools.partial(jax.custom_vjp, nondiff_argnums=(2,))
def _bdot(a, b, kind):
    return _dot(a, b, kind)


def _bdot_fwd(a, b, kind):
    return _dot(a, b, kind), (a, b)


def _bdot_bwd(kind, res, g):
    a, b = res
    if kind == "nn":
        return _dot(g, b, "nt"), _dot(a, g, "tn")
    if kind == "nt":
        return _dot(g, b, "nn"), _dot(g, a, "tn")
    return _dot(b, g, "nt"), _dot(a, g, "nn")


_bdot.defvjp(_bdot_fwd, _bdot_bwd)


def _mm(a, b, *, ta=False, tb=False, acc=None, out_dtype=F32, name, b_win=None, into=None):
    assert not (ta and tb)
    m, k = (a.shape[1], a.shape[0]) if ta else a.shape
    b_off, b_size = b_win or (0, b.shape[1])
    n = b.shape[0] if tb else b_size
    assert (b_size if tb else b.shape[0]) == k, (a.shape, b.shape, ta, tb, b_win)
    o_off = into[1] if into else 0
    nk = -(-k // MM_MAX_K)
    while k % nk or (k // nk) % LANES or (tb and b_off % (k // nk)):
        nk += 1
    tk = k // nk
    tm, tn = _tile(m, 1024), _tile(n, 1024)
    while o_off % tn or (not tb and b_off % tn):
        tn = _tile(n, tn - LANES)
    kind = "tn" if ta else ("nt" if tb else "nn")
    a_spec = pl.BlockSpec((tk, tm), lambda i, j, l: (l, i)) if ta else pl.BlockSpec((tm, tk), lambda i, j, l: (i, l))
    if tb:
        b_spec = pl.BlockSpec((tn, tk), lambda i, j, l: (j, l + b_off // tk))
    else:
        b_spec = pl.BlockSpec((tk, tn), lambda i, j, l: (l, j + b_off // tn))
    c_spec = pl.BlockSpec((tm, tn), lambda i, j, l: (i, j))
    o_spec = pl.BlockSpec((tm, tn), lambda i, j, l: (i, j + o_off // tn))
    has_acc = acc is not None

    def body(*refs):
        a_ref, b_ref = refs[:2]
        c_ref = refs[2] if has_acc else None
        o_ref = refs[2 + has_acc + (into is not None)]
        if nk == 1:
            res = _dot(a_ref[...], b_ref[...], kind)
            if has_acc:
                res = res + c_ref[...].astype(F32)
            o_ref[...] = res.astype(o_ref.dtype)
            return
        acc_ref = refs[-1]
        l = pl.program_id(2)

        @pl.when(l == 0)
        def _():
            if has_acc:
                acc_ref[...] = c_ref[...].astype(F32)
            else:
                acc_ref[...] = jnp.zeros_like(acc_ref)

        acc_ref[...] += _dot(a_ref[...], b_ref[...], kind)

        @pl.when(l == nk - 1)
        def _():
            o_ref[...] = acc_ref[...].astype(o_ref.dtype)

    ins = [a, b] + ([acc] if has_acc else []) + ([into[0]] if into else [])
    in_specs = [a_spec, b_spec] + ([c_spec] if has_acc else []) + ([pl.BlockSpec(memory_space=pl.ANY)] if into else [])
    out_shape = jax.ShapeDtypeStruct(into[0].shape, into[0].dtype) if into else jax.ShapeDtypeStruct((m, n), out_dtype)
    return pl.pallas_call(
        body, name=name, grid=(m // tm, n // tn, nk), in_specs=in_specs, out_specs=o_spec, out_shape=out_shape,
        input_output_aliases={len(ins) - 1: 0} if into else {},
        scratch_shapes=[pltpu.VMEM((tm, tn), F32)] if nk > 1 else [],
        compiler_params=_cp("parallel", "parallel", "arbitrary"),
    )(*ins)


def _w_in_pieces(seg_sizes, seg_order, n_blk):
    layout, o = {}, 0
    for sn in seg_order:
        width = -(-seg_sizes[sn] // LANES) * LANES
        layout[sn] = (o, width)
        o += width
    pieces, start = [], 0
    for sn, sz in seg_sizes.items():
        lo = start
        while lo < start + sz:
            blk = lo // n_blk
            hi = min(start + sz, (blk + 1) * n_blk)
            pieces.append((blk, lo - blk * n_blk, sn, lo - start, layout[sn][0] + lo - start, hi - lo))
            lo = hi
        start += sz
    return pieces, layout


def _w_in_pack(gathered, pieces, layout, seg_sizes, *, name, tr=256):
    _, k, n_blk = gathered.shape
    n_pad = sum(w for _, w in layout.values())

    def body(g_ref, o_ref):
        for sn, (off, width) in layout.items():
            if width != seg_sizes[sn]:
                o_ref[:, pl.ds(off + seg_sizes[sn], width - seg_sizes[sn])] = jnp.zeros(
                    (tr, width - seg_sizes[sn]), o_ref.dtype)
        for blk, src, _, _, dst, width in pieces:
            o_ref[:, pl.ds(dst, width)] = g_ref[blk, :, pl.ds(src, width)]

    return pl.pallas_call(
        body, name=name, grid=(k // tr,), in_specs=[pl.BlockSpec((N_DEV, tr, n_blk), lambda i: (0, i, 0))],
        out_specs=pl.BlockSpec((tr, n_pad), lambda i: (i, 0)), out_shape=jax.ShapeDtypeStruct((k, n_pad), gathered.dtype),
        compiler_params=_cp("parallel"),
    )(gathered)


def _w_in_unpack(seg_grads, pieces, n_blk, *, name, tr=128):
    names = list(seg_grads)
    k = seg_grads[names[0]].shape[0]

    def body(*refs):
        o_ref = refs[-1]
        seg_ref = dict(zip(names, refs))
        for blk, dst, sn, src, _, width in pieces:
            o_ref[blk, :, pl.ds(dst, width)] = seg_ref[sn][:, pl.ds(src, width)].astype(o_ref.dtype)

    return pl.pallas_call(
        body, name=name, grid=(k // tr,),
        in_specs=[pl.BlockSpec((tr, seg_grads[sn].shape[1]), lambda i: (i, 0)) for sn in names],
        out_specs=pl.BlockSpec((N_DEV, tr, n_blk), lambda i: (0, i, 0)),
        out_shape=jax.ShapeDtypeStruct((N_DEV, k, n_blk), BF16), compiler_params=_cp("parallel"),
    )(*[seg_grads[sn] for sn in names])


def _row_spec(arr, tb, nj):
    return pl.BlockSpec((tb, arr.shape[1] // nj), lambda j, i: (i, j))


def _par_spec(arr):
    return pl.BlockSpec((1,) + arr.shape[1:], lambda j, i: (j, 0, 0))


def _blocked_fwd(fn, rows, params, outs, *, nj=1, tb, name):
    t = rows[0].shape[0]
    nr, npar = len(rows), len(params)

    def body(*refs):
        res = fn(*[r[...] for r in refs[:nr]], *[p[0] for p in refs[nr:nr + npar]])
        for o_ref, val in zip(refs[nr + npar:], res):
            o_ref[...] = val.astype(o_ref.dtype)

    return pl.pallas_call(
        body, name=name, grid=(nj, t // tb),
        in_specs=[_row_spec(a, tb, nj) for a in rows] + [_par_spec(p) for p in params],
        out_specs=[pl.BlockSpec((tb, c // nj), lambda j, i: (i, j)) for c, _ in outs],
        out_shape=[jax.ShapeDtypeStruct((t, c), dt) for c, dt in outs],
        compiler_params=_cp("parallel", "arbitrary"),
    )(*rows, *params)


def _blocked_bwd(fn, rows, params, cts, row_grad_dtypes, *, adds=None, nj=1, tb, name):
    t = rows[0].shape[0]
    nr, npar, nct = len(rows), len(params), len(cts)
    adds = adds or {}
    add_keys = sorted(adds)
    want, want_dtypes = [], []
    for k, dts in enumerate(row_grad_dtypes):
        for dt in (dts if isinstance(dts, tuple) else (dts,)):
            if dt is not None:
                want.append(k)
                want_dtypes.append(dt)

    def body(*refs):
        row_refs = refs[:nr]
        par_refs = refs[nr:nr + npar]
        ct_refs = refs[nr + npar:nr + npar + nct]
        add_refs = dict(zip(add_keys, refs[nr + npar + nct:nr + npar + nct + len(add_keys)]))
        out_refs = refs[nr + npar + nct + len(add_keys):]
        _, vjp = jax.vjp(fn, *[r[...] for r in row_refs], *[p[0] for p in par_refs])
        grads = vjp(tuple(c[...].astype(F32) for c in ct_refs))
        for o_ref, k in zip(out_refs, want):
            g = grads[k]
            if k in add_refs:
                g = g + add_refs[k][...].astype(F32)
            o_ref[...] = g.astype(o_ref.dtype)
        first = pl.program_id(1) == 0
        for o_ref, g in zip(out_refs[len(want):], grads[nr:]):
            @pl.when(first)
            def _(o_ref=o_ref):
                o_ref[...] = jnp.zeros_like(o_ref)
            o_ref[0] += g

    add_arrs = [adds[k] for k in add_keys]
    return pl.pallas_call(
        body, name=name, grid=(nj, t // tb),
        in_specs=[_row_spec(a, tb, nj) for a in rows] + [_par_spec(p) for p in params]
        + [_row_spec(c, tb, nj) for c in cts] + [_row_spec(a, tb, nj) for a in add_arrs],
        out_specs=[_row_spec(rows[k], tb, nj) for k in want] + [_par_spec(p) for p in params],
        out_shape=[jax.ShapeDtypeStruct(rows[k].shape, dt) for k, dt in zip(want, want_dtypes)]
        + [jax.ShapeDtypeStruct(p.shape, F32) for p in params],
        compiler_params=_cp("parallel", "arbitrary"),
    )(*rows, *params, *cts, *add_arrs)


def _rms_fn(x, w):
    return (x * lax.rsqrt(jnp.mean(x * x, axis=-1, keepdims=True) + EPS) * w,)


def _silu(x):
    return x * jax.nn.sigmoid(x)


def _merge_fn(glu_v, glu_g, g_a, g_b, y_a):
    y_b = glu_v * jax.nn.sigmoid(glu_g)
    return (jax.nn.sigmoid(g_a) * y_a + jax.nn.sigmoid(g_b) * y_b,)


def _s5_bu_fn(u, b_re, b_im):
    return _bdot(u, b_re, "nn"), _bdot(u, b_im, "nn")


def _s5_out_fn(s_re, s_im, u, c_re, c_im_neg, d):
    return (jax.nn.gelu(_bdot(s_re, c_re, "nn") + _bdot(s_im, c_im_neg, "nn") + d * u),)


HALO = SUBLANES


STRIP = 64


def _conv_strip(ext_ref, w_ref, b_ref, r0, cols):
    kw = w_ref.shape[0]
    xs = [ext_ref[pl.ds(r0 + HALO - kw + 1 + k, STRIP), cols] for k in range(kw)]
    c = b_ref[:, cols] + w_ref[0:1, cols] * xs[0]
    for k in range(1, kw):
        c = c + w_ref[k:k + 1, cols] * xs[k]
    return c, xs


def _fold(x):
    return x.reshape(STRIP // SUBLANES, SUBLANES, LANES).sum(axis=0)


def _conv_specs(xs, ws, bs, tb, cb, time_of):
    specs = []
    for x, w, b in zip(xs, ws, bs):
        specs += [
            pl.BlockSpec((HALO, cb), lambda j, i: (jnp.maximum(time_of(i) * (tb // HALO) - 1, 0), j)),
            pl.BlockSpec((tb, cb), lambda j, i: (time_of(i), j)),
            pl.BlockSpec((w.shape[0], cb), lambda j, i: (0, j)),
            pl.BlockSpec((1, cb), lambda j, i: (0, j)),
        ]
    return specs


def _conv_fwd(comb, xs, ws, bs, *, out_dtype, name, tb=512):
    t, c = xs[0].shape
    cb = _tile(c, 512)
    ns = len(xs)

    def body(*refs):
        i = pl.program_id(1)
        o_ref = refs[4 * ns]
        exts = refs[4 * ns + 1:]
        for s in range(ns):
            xp_ref, xm_ref = refs[4 * s:4 * s + 2]
            exts[s][pl.ds(0, HALO), :] = jnp.where(i == 0, 0.0, xp_ref[...])
            exts[s][pl.ds(HALO, tb), :] = xm_ref[...]
        for c0 in range(0, cb, LANES):
            cols = pl.ds(c0, LANES)
            for r0 in range(0, tb, STRIP):
                cs = [_conv_strip(exts[s], refs[4 * s + 2], refs[4 * s + 3], r0, cols)[0] for s in range(ns)]
                o_ref[pl.ds(r0, STRIP), cols] = comb(*cs).astype(out_dtype)

    flat = [a for x, w, b in zip(xs, ws, bs) for a in (x, x, w, b)]
    return pl.pallas_call(
        body, name=name, grid=(c // cb, t // tb),
        in_specs=_conv_specs(xs, ws, bs, tb, cb, lambda i: i),
        out_specs=pl.BlockSpec((tb, cb), lambda j, i: (i, j)),
        out_shape=jax.ShapeDtypeStruct((t, c), out_dtype),
        scratch_shapes=[pltpu.VMEM((HALO + tb, cb), F32) for _ in range(ns)],
        compiler_params=_cp("parallel", "arbitrary"),
    )(*flat)


def _conv_bwd(comb, xs, ws, bs, dy, *, dx_dtype, name, tb=512):
    t, c = xs[0].shape
    cb = _tile(c, 512)
    ns = len(xs)
    nt = t // tb
    kw = ws[0].shape[0]

    def body(*refs):
        step = pl.program_id(1)
        dy_ref = refs[4 * ns]
        out_refs = refs[4 * ns + 1:4 * ns + 1 + 3 * ns]
        scratch = refs[4 * ns + 1 + 3 * ns:]
        exts, dcs, carries = scratch[:ns], scratch[ns:2 * ns], scratch[2 * ns:]

        @pl.when(step == 0)
        def _():
            for s in range(ns):
                carries[s][...] = jnp.zeros_like(carries[s])
                out_refs[3 * s + 1][...] = jnp.zeros_like(out_refs[3 * s + 1])
                out_refs[3 * s + 2][...] = jnp.zeros_like(out_refs[3 * s + 2])

        for s in range(ns):
            xp_ref, xm_ref = refs[4 * s:4 * s + 2]
            exts[s][pl.ds(0, HALO), :] = jnp.where(step == nt - 1, 0.0, xp_ref[...])
            exts[s][pl.ds(HALO, tb), :] = xm_ref[...]
            dcs[s][pl.ds(tb, HALO), :] = carries[s][...]
        for c0 in range(0, cb, LANES):
            cols = pl.ds(c0, LANES)
            acc_w = [[jnp.zeros((SUBLANES, LANES), F32) for _ in range(kw)] for _ in range(ns)]
            acc_b = [jnp.zeros((SUBLANES, LANES), F32) for _ in range(ns)]
            for r0 in range(0, tb, STRIP):
                strips = [_conv_strip(exts[s], refs[4 * s + 2], refs[4 * s + 3], r0, cols) for s in range(ns)]
                _, vjp = jax.vjp(comb, *[cs for cs, _ in strips])
                grads = vjp(dy_ref[pl.ds(r0, STRIP), cols].astype(F32))
                for s in range(ns):
                    dcs[s][pl.ds(r0, STRIP), cols] = grads[s]
                    acc_b[s] = acc_b[s] + _fold(grads[s])
                    for k in range(kw):
                        acc_w[s][k] = acc_w[s][k] + _fold(grads[s] * strips[s][1][k])
            for s in range(ns):
                dw_ref, db_ref = out_refs[3 * s + 1], out_refs[3 * s + 2]
                db_ref[:, cols] += jnp.sum(acc_b[s], axis=0, keepdims=True)
                for k in range(kw):
                    dw_ref[k:k + 1, cols] += jnp.sum(acc_w[s][k], axis=0, keepdims=True)
        for s in range(ns):
            w_ref, dx_ref = refs[4 * s + 2], out_refs[3 * s]
            for c0 in range(0, cb, LANES):
                cols = pl.ds(c0, LANES)
                for r0 in range(0, tb, STRIP):
                    dx = w_ref[kw - 1:kw, cols] * dcs[s][pl.ds(r0, STRIP), cols]
                    for k in range(kw - 1):
                        dx = dx + w_ref[k:k + 1, cols] * dcs[s][pl.ds(r0 + kw - 1 - k, STRIP), cols]
                    dx_ref[pl.ds(r0, STRIP), cols] = dx.astype(dx_dtype)
            carries[s][...] = dcs[s][pl.ds(0, HALO), :]

    flat = [a for x, w, b in zip(xs, ws, bs) for a in (x, x, w, b)]
    rev = lambda i: nt - 1 - i
    out_specs, out_shape = [], []
    for x, w, b in zip(xs, ws, bs):
        out_specs += [pl.BlockSpec((tb, cb), lambda j, i: (rev(i), j)),
                      pl.BlockSpec((w.shape[0], cb), lambda j, i: (0, j)),
                      pl.BlockSpec((1, cb), lambda j, i: (0, j))]
        out_shape += [jax.ShapeDtypeStruct((t, c), dx_dtype), jax.ShapeDtypeStruct(w.shape, F32),
                      jax.ShapeDtypeStruct(b.shape, F32)]
    res = pl.pallas_call(
        body, name=name, grid=(c // cb, nt),
        in_specs=_conv_specs(xs, ws, bs, tb, cb, rev) + [pl.BlockSpec((tb, cb), lambda j, i: (rev(i), j))],
        out_specs=out_specs, out_shape=out_shape,
        scratch_shapes=[pltpu.VMEM((HALO + tb, cb), F32) for _ in range(2 * ns)]
        + [pltpu.VMEM((HALO, cb), F32) for _ in range(ns)],
        compiler_params=_cp("parallel", "arbitrary"),
    )(*flat, dy)
    return [tuple(res[3 * s:3 * s + 3]) for s in range(ns)]


def _comb_silu(c):
    return _silu(c)


def _comb_glu(cg, cv):
    return _silu(cg) * cv


def _ssd_fn(nheads, hdim):
    def fn(x, bm, cm, z, dtr, hin, dtb, alog, dsk, nw):
        q = x.shape[0]
        dt = jax.nn.softplus(dtr + dtb)
        da = dt * (-jnp.exp(alog))
        li = lax.broadcasted_iota(jnp.int32, (q, q), 0)
        si = lax.broadcasted_iota(jnp.int32, (q, q), 1)
        causal = li >= si
        tri = causal.astype(F32)
        acs = jnp.dot(tri, da, precision=HIGHEST, preferred_element_type=F32)
        acs_row = lax.dot_general(da, tri, (((0,), (1,)), ((), ())), precision=HIGHEST,
                                  preferred_element_type=F32)
        cb = _bdot(cm, bm, "nt")
        ch = _bdot(cm, hin, "nn")
        ys, hs = [], []
        for r in range(nheads):
            cols = slice(r * hdim, (r + 1) * hdim)
            xr = x[:, cols]
            a_col = acs[:, r:r + 1]
            decay = jnp.exp(jnp.where(causal, a_col - acs_row[r:r + 1, :], -1e30))
            xd = xr * dt[:, r:r + 1]
            y_diag = _bdot(cb * decay, xd, "nn")
            y_off = ch[:, cols] * jnp.exp(a_col)
            last = acs[q - 1:q, r:r + 1]
            st = _bdot(bm * jnp.exp(last - a_col), xd, "tn")
            hs.append(jnp.exp(last) * hin[:, cols] + st)
            ys.append(y_diag + y_off + dsk[:, r:r + 1] * xr)
        y = jnp.concatenate(ys, axis=1) * _silu(z)
        yn = y * lax.rsqrt(jnp.mean(y * y, axis=-1, keepdims=True) + EPS) * nw
        return yn, jnp.concatenate(hs, axis=1)
    return fn


def _ssd_specs(rp, nr, time_of):
    row = lambda w: pl.BlockSpec((CHUNK, w), lambda g, c: (time_of(c), g))
    par = lambda w: pl.BlockSpec((1, 1, w), lambda g, c: (g, 0, 0))
    return dict(
        x=row(rp), bc=row(D_STATE), dtr=pl.BlockSpec((1, CHUNK, nr), lambda g, c: (g, time_of(c), 0)),
        h=pl.BlockSpec((1, 1, D_STATE, rp), lambda g, c: (g, time_of(c), 0, 0)), pr=par(nr), pw=par(rp))


def _ssd_fwd(xs, bm, cm, z, dtr, dtb, alog, dsk, nw, *, name):
    t = xs.shape[0]
    g, _, nr = dtr.shape
    rp = xs.shape[1] // g
    nc = t // CHUNK
    fn = _ssd_fn(nr, rp // nr)
    sp = _ssd_specs(rp, nr, lambda c: c)

    def body(x_ref, b_ref, c_ref, z_ref, dtr_ref, dtb_ref, al_ref, dsk_ref, nw_ref, yn_ref, hs_ref, h_ref):
        @pl.when(pl.program_id(1) == 0)
        def _():
            h_ref[...] = jnp.zeros_like(h_ref)
        hin = h_ref[...]
        hs_ref[0, 0] = hin
        yn, hout = fn(x_ref[...], b_ref[...], c_ref[...], z_ref[...], dtr_ref[0], hin,
                      dtb_ref[0], al_ref[0], dsk_ref[0], nw_ref[0])
        yn_ref[...] = yn.astype(yn_ref.dtype)
        h_ref[...] = hout

    return pl.pallas_call(
        body, name=name, grid=(g, nc),
        in_specs=[sp["x"], sp["bc"], sp["bc"], sp["x"], sp["dtr"], sp["pr"], sp["pr"], sp["pr"], sp["pw"]],
        out_specs=[sp["x"], sp["h"]],
        out_shape=[jax.ShapeDtypeStruct(xs.shape, BF16), jax.ShapeDtypeStruct((g, nc, D_STATE, rp), F32)],
        scratch_shapes=[pltpu.VMEM((D_STATE, rp), F32)],
        compiler_params=_cp("parallel", "arbitrary"),
    )(xs, bm, cm, z, dtr, dtb, alog, dsk, nw)


def _ssd_bwd(xs, bm, cm, z, dtr, hsave, dtb, alog, dsk, nw, dyn, *, name):
    t = xs.shape[0]
    g, _, nr = dtr.shape
    rp = xs.shape[1] // g
    nc = t // CHUNK
    fn = _ssd_fn(nr, rp // nr)
    sp = _ssd_specs(rp, nr, lambda c: nc - 1 - c)

    def body(x_ref, b_ref, c_ref, z_ref, dtr_ref, hs_ref, dtb_ref, al_ref, dsk_ref, nw_ref, dyn_ref,
             dx_ref, db_ref, dc_ref, dz_ref, ddtr_ref, ddtb_ref, dal_ref, ddsk_ref, dnw_ref, dh_ref):
        first = pl.program_id(1) == 0

        @pl.when(first)
        def _():
            dh_ref[...] = jnp.zeros_like(dh_ref)
            for r in (ddtb_ref, dal_ref, ddsk_ref, dnw_ref):
                r[...] = jnp.zeros_like(r)

        _, vjp = jax.vjp(fn, x_ref[...], b_ref[...], c_ref[...], z_ref[...], dtr_ref[0], hs_ref[0, 0],
                         dtb_ref[0], al_ref[0], dsk_ref[0], nw_ref[0])
        dx, db, dc, dz, ddtr, dhin, ddtb, dal, ddsk, dnw = vjp((dyn_ref[...].astype(F32), dh_ref[...]))
        dx_ref[...] = dx
        db_ref[...] = db
        dc_ref[...] = dc
        dz_ref[...] = dz.astype(dz_ref.dtype)
        ddtr_ref[0] = ddtr
        dh_ref[...] = dhin
        ddtb_ref[0] += ddtb
        dal_ref[0] += dal
        ddsk_ref[0] += ddsk
        dnw_ref[0] += dnw

    sd = jax.ShapeDtypeStruct
    return pl.pallas_call(
        body, name=name, grid=(g, nc),
        in_specs=[sp["x"], sp["bc"], sp["bc"], sp["x"], sp["dtr"], sp["h"], sp["pr"], sp["pr"], sp["pr"], sp["pw"],
                  sp["x"]],
        out_specs=[sp["x"], sp["bc"], sp["bc"], sp["x"], sp["dtr"], sp["pr"], sp["pr"], sp["pr"], sp["pw"]],
        out_shape=[sd(xs.shape, F32), sd(bm.shape, F32), sd(cm.shape, F32), sd(z.shape, BF16), sd(dtr.shape, F32),
                   sd(dtb.shape, F32), sd(alog.shape, F32), sd(dsk.shape, F32), sd(nw.shape, F32)],
        scratch_shapes=[pltpu.VMEM((D_STATE, rp), F32)],
        compiler_params=_cp("parallel", "arbitrary"),
    )(xs, bm, cm, z, dtr, hsave, dtb, alog, dsk, nw, dyn)


def _s5_param_fn(lam_re, lam_im, log_dt, bt_re, bt_im):
    lr = jnp.minimum(lam_re, EIG_MAX)
    dt = jnp.exp(log_dt)
    mag = jnp.exp(lr * dt)
    lb_re = mag * jnp.cos(lam_im * dt)
    lb_im = mag * jnp.sin(lam_im * dt)
    n_re = lb_re - 1.0
    den = lr * lr + lam_im * lam_im
    k_re = (n_re * lr + lb_im * lam_im) / den
    k_im = (lb_im * lr - n_re * lam_im) / den
    return lb_re, lb_im, k_re * bt_re - k_im * bt_im, k_re * bt_im + k_im * bt_re


def _s5_params(lam_re, lam_im, log_dt, bt_re, bt_im, cts=None, *, name):
    args = (lam_re, lam_im, log_dt, bt_re, bt_im)
    n = len(args)

    def body(*refs):
        vals = [r[...] for r in refs[:n]]
        if cts is None:
            res = _s5_param_fn(*vals)
        else:
            _, vjp = jax.vjp(_s5_param_fn, *vals)
            res = vjp(tuple(r[...] for r in refs[n:n + 4]))
        for o_ref, v in zip(refs[-len(res):], res):
            o_ref[...] = v

    if cts is None:
        out = [lam_re, lam_im, bt_re, bt_im]
        ins = args
    else:
        out = list(args)
        ins = args + tuple(cts)
    return pl.pallas_call(
        body, name=name, out_shape=[jax.ShapeDtypeStruct(a.shape, F32) for a in out],
        compiler_params=pltpu.CompilerParams(vmem_limit_bytes=VMEM_LIMIT),
    )(*ins)


SCAN_COLS = 512


def _cmul(xr, xi, yr, yi):
    return xr * yr - xi * yi, xr * yi + xi * yr


def _scan_consts(a_re, a_im, cols, reverse):
    shape = (SUBLANES, cols)
    row = lax.broadcasted_iota(jnp.int32, shape, 0)
    dist = (SUBLANES - 1 - row) if reverse else row
    mr, mi = jnp.broadcast_to(a_re, shape), jnp.broadcast_to(a_im, shape)
    pr, pi = mr, mi
    mults = []
    for d in (1, 2, 4):
        mults.append((mr, mi))
        qr, qi = _cmul(pr, pi, mr, mi)
        has_bit = (dist & d) != 0
        pr, pi = jnp.where(has_bit, qr, pr), jnp.where(has_bit, qi, pi)
        mr, mi = _cmul(mr, mi, mr, mi)
    return mults, (pr, pi), dist


def _scan_group(xr, xi, consts, cr, ci, reverse):
    mults, (pr, pi), dist = consts
    for d, (mr, mi) in zip((1, 2, 4), mults):
        shift = (SUBLANES - d) if reverse else d
        sr = jnp.where(dist >= d, pltpu.roll(xr, shift, 0), 0.0)
        si = jnp.where(dist >= d, pltpu.roll(xi, shift, 0), 0.0)
        tr, ti = _cmul(mr, mi, sr, si)
        xr, xi = xr + tr, xi + ti
    last = slice(0, 1) if reverse else slice(SUBLANES - 1, SUBLANES)
    nr, ni = _cmul(pr[last], pi[last], cr, ci)
    tr, ti = _cmul(pr, pi, jnp.broadcast_to(cr, xr.shape), jnp.broadcast_to(ci, xr.shape))
    return xr + tr, xi + ti, xr[last] + nr, xi[last] + ni


def _scan_specs(tb, time_of):
    row = pl.BlockSpec((tb, SCAN_COLS), lambda j, i: (time_of(i), j))
    par = pl.BlockSpec((1, SCAN_COLS), lambda j, i: (0, j))
    return row, par


def _s5_scan_fwd(bu_re, bu_im, lb_re, lb_im, *, name, tb=512):
    t, c = bu_re.shape
    nj = c // SCAN_COLS
    row, par = _scan_specs(tb, lambda i: i)

    def body(bre_ref, bim_ref, lre_ref, lim_ref, sre_ref, sim_ref, cre_ref, cim_ref):
        @pl.when(pl.program_id(1) == 0)
        def _():
            cre_ref[...] = jnp.zeros_like(cre_ref)
            cim_ref[...] = jnp.zeros_like(cim_ref)
        consts = _scan_consts(lre_ref[...], lim_ref[...], SCAN_COLS, False)

        def group(k, carry):
            rows = pl.ds(pl.multiple_of(k * SUBLANES, SUBLANES), SUBLANES)
            sr, si, cr, ci = _scan_group(bre_ref[rows, :], bim_ref[rows, :], consts, *carry, False)
            sre_ref[rows, :] = sr
            sim_ref[rows, :] = si
            return cr, ci

        sr, si = lax.fori_loop(0, tb // SUBLANES, group, (cre_ref[...], cim_ref[...]), unroll=4)
        cre_ref[...] = sr
        cim_ref[...] = si

    return pl.pallas_call(
        body, name=name, grid=(nj, t // tb), in_specs=[row, row, par, par], out_specs=[row, row],
        out_shape=[jax.ShapeDtypeStruct((t, c), F32)] * 2,
        scratch_shapes=[pltpu.VMEM((1, SCAN_COLS), F32)] * 2,
        compiler_params=_cp("parallel", "arbitrary"),
    )(bu_re, bu_im, lb_re, lb_im)


def _s5_scan_bwd(s_re, s_im, ds_re, ds_im, lb_re, lb_im, *, name, tb=512):
    t, c = s_re.shape
    nj = c // SCAN_COLS
    nt = t // tb
    rev = lambda i: nt - 1 - i
    row, par = _scan_specs(tb, rev)
    prev = pl.BlockSpec((HALO, SCAN_COLS), lambda j, i: (jnp.maximum(rev(i) * (tb // HALO) - 1, 0), j))

    def body(sre_ref, sim_ref, pre_ref, pim_ref, dre_ref, dim_ref, lre_ref, lim_ref,
             gre_ref, gim_ref, dlre_ref, dlim_ref, cre_ref, cim_ref, ext_re, ext_im):
        step_id = pl.program_id(1)

        @pl.when(step_id == 0)
        def _():
            cre_ref[...] = jnp.zeros_like(cre_ref)
            cim_ref[...] = jnp.zeros_like(cim_ref)
            dlre_ref[...] = jnp.zeros_like(dlre_ref)
            dlim_ref[...] = jnp.zeros_like(dlim_ref)
        consts = _scan_consts(lre_ref[...], -lim_ref[...], SCAN_COLS, True)
        ngroups = tb // SUBLANES

        def group(k, carry):
            rows = pl.ds(pl.multiple_of((ngroups - 1 - k) * SUBLANES, SUBLANES), SUBLANES)
            gr, gi, cr, ci = _scan_group(dre_ref[rows, :], dim_ref[rows, :], consts, *carry, True)
            gre_ref[rows, :] = gr
            gim_ref[rows, :] = gi
            return cr, ci

        gr, gi = lax.fori_loop(0, ngroups, group, (cre_ref[...], cim_ref[...]), unroll=4)
        cre_ref[...] = gr
        cim_ref[...] = gi
        has_past = step_id != nt - 1
        ext_re[pl.ds(0, HALO), :] = jnp.where(has_past, pre_ref[...], 0.0)
        ext_im[pl.ds(0, HALO), :] = jnp.where(has_past, pim_ref[...], 0.0)
        ext_re[pl.ds(HALO, tb), :] = sre_ref[...]
        ext_im[pl.ds(HALO, tb), :] = sim_ref[...]
        pr, pi = ext_re[pl.ds(HALO - 1, tb), :], ext_im[pl.ds(HALO - 1, tb), :]
        g_re, g_im = gre_ref[...], gim_ref[...]
        dlre_ref[...] += jnp.sum(pr * g_re + pi * g_im, axis=0, keepdims=True)
        dlim_ref[...] += jnp.sum(pr * g_im - pi * g_re, axis=0, keepdims=True)

    return pl.pallas_call(
        body, name=name, grid=(nj, nt),
        in_specs=[row, row, prev, prev, row, row, par, par], out_specs=[row, row, par, par],
        out_shape=[jax.ShapeDtypeStruct((t, c), F32)] * 2 + [jax.ShapeDtypeStruct((1, c), F32)] * 2,
        scratch_shapes=[pltpu.VMEM((1, SCAN_COLS), F32)] * 2 + [pltpu.VMEM((HALO + tb, SCAN_COLS), F32)] * 2,
        compiler_params=_cp("parallel", "arbitrary"),
    )(s_re, s_im, s_re, s_im, ds_re, ds_im, lb_re, lb_im)


def _loss_fn(h, w, tgt):
    err = _rms_fn(h, w)[0] - tgt
    return 0.5 * jnp.sum(jnp.mean(err * err, axis=-1, keepdims=True), axis=0, keepdims=True)


def _loss_head(h, w, tgt, *, name, tb=256):
    t, d = h.shape

    def body(h_ref, w_ref, t_ref, loss_ref, dh_ref, dhb_ref, dw_ref):
        @pl.when(pl.program_id(0) == 0)
        def _():
            loss_ref[...] = jnp.zeros_like(loss_ref)
            dw_ref[...] = jnp.zeros_like(dw_ref)
        part, vjp = jax.vjp(_loss_fn, h_ref[...], w_ref[...], t_ref[...])
        dh, dw, _ = vjp(jnp.ones((1, 1), F32))
        loss_ref[...] += jnp.broadcast_to(part, loss_ref.shape)
        dh_ref[...] = dh
        dhb_ref[...] = dh.astype(BF16)
        dw_ref[...] += dw

    row = pl.BlockSpec((tb, d), lambda i: (i, 0))
    par = pl.BlockSpec((1, d), lambda i: (0, 0))
    return pl.pallas_call(
        body, name=name, grid=(t // tb,), in_specs=[row, par, row],
        out_specs=[pl.BlockSpec((SUBLANES, LANES), lambda i: (0, 0)), row, row, par],
        out_shape=[jax.ShapeDtypeStruct((SUBLANES, LANES), F32), jax.ShapeDtypeStruct((t, d), F32),
                   jax.ShapeDtypeStruct((t, d), BF16), jax.ShapeDtypeStruct((1, d), F32)],
        compiler_params=_cp("arbitrary"),
    )(h, w, tgt)


def _adamw(w, g, m, v, *, name):
    r, c = w.shape
    tr = _tile(r, 256, SUBLANES)

    def body(w_ref, g_ref, m_ref, v_ref, d_ref, nm_ref, nv_ref):
        g = g_ref[...]
        nm = ADAM_B1 * m_ref[...] + (1.0 - ADAM_B1) * g
        nv = ADAM_B2 * v_ref[...] + (1.0 - ADAM_B2) * (g * g)
        m_hat = nm / (1.0 - ADAM_B1 ** ADAM_STEP)
        v_hat = nv / (1.0 - ADAM_B2 ** ADAM_STEP)
        d_ref[...] = -ADAM_LR * (m_hat / (jnp.sqrt(v_hat) + ADAM_EPS) + ADAM_WD * w_ref[...])
        nm_ref[...] = nm
        nv_ref[...] = nv

    spec = pl.BlockSpec((tr, c), lambda i: (i, 0))
    return pl.pallas_call(
        body, name=name, grid=(r // tr,), in_specs=[spec] * 4, out_specs=[spec] * 3,
        out_shape=[jax.ShapeDtypeStruct((r, c), F32)] * 3, compiler_params=_cp("parallel"),
    )(w, g, m, v)


def _sum_parts(parts, *, name):
    _, r, c = parts.shape
    tr = _tile(r, 128, SUBLANES)

    def body(p_ref, o_ref):
        acc = p_ref[0].astype(F32)
        for k in range(1, N_DEV):
            acc = acc + p_ref[k].astype(F32)
        o_ref[...] = acc

    return pl.pallas_call(
        body, name=name, grid=(r // tr,), in_specs=[pl.BlockSpec((N_DEV, tr, c), lambda i: (0, i, 0))],
        out_specs=pl.BlockSpec((tr, c), lambda i: (i, 0)), out_shape=jax.ShapeDtypeStruct((r, c), F32),
        compiler_params=_cp("parallel"),
    )(parts)


def _position():
    return lax.axis_index("x"), lax.axis_index("y"), lax.axis_index("c")


def _flat(px, py, pc):
    return 4 * px + 2 * py + pc


def _all_gather(shard, *, name):
    def body(x_ref, out_ref, token, send_sems, recv_sems, local_sem):
        token[...] = jnp.zeros_like(token)
        x, y, c = _position()
        me, sibling = (x, y, c), (x, y, 1 - c)
        chips = [(1 - x, y), (x, 1 - y), (1 - x, 1 - y)]

        def copy(k, block, to, src=None):
            slot = out_ref.at[_flat(*block)]
            return pltpu.make_async_remote_copy(
                src_ref=slot if src is None else src, dst_ref=slot, send_sem=send_sems.at[k],
                recv_sem=recv_sems.at[k], device_id=to, device_id_type=MESH)

        mine = pltpu.make_async_copy(x_ref, out_ref.at[_flat(*me)], local_sem)
        mine.start()
        first = [copy(0, me, sibling, src=x_ref)]
        first += [copy(1 + j, me, (*chip, c), src=x_ref) for j, chip in enumerate(chips)]
        for cp in first:
            cp.start()
        passed = [copy(4 + j, (*chip, c), sibling) for j, chip in enumerate(chips)]
        for j, chip in enumerate(chips):
            copy(1 + j, (*chip, c), me).wait_recv()
            passed[j].start()
        copy(0, sibling, me).wait_recv()
        for j, chip in enumerate(chips):
            copy(4 + j, (*chip, 1 - c), me).wait_recv()
        for cp in first + passed:
            cp.wait_send()
        mine.wait()

    return pl.pallas_call(
        body, name=name,
        out_shape=(jax.ShapeDtypeStruct((N_DEV,) + shard.shape, shard.dtype),
                   jax.ShapeDtypeStruct((SUBLANES, LANES), F32)),
        in_specs=[pl.BlockSpec(memory_space=pl.ANY)],
        out_specs=(pl.BlockSpec(memory_space=pl.ANY), pl.BlockSpec(memory_space=pltpu.VMEM)),
        scratch_shapes=[pltpu.SemaphoreType.DMA((7,)), pltpu.SemaphoreType.DMA((7,)), pltpu.SemaphoreType.DMA(())],
    )(shard)


_HBM = pl.BlockSpec(memory_space=pltpu.HBM)
_SEM = pl.BlockSpec(memory_space=pltpu.SEMAPHORE)
_EFFECT = pltpu.SideEffectType.DATAFLOW_SIDE_EFFECTING


def _copy_ends(src_ref, land_ref, mode, me, to):
    if mode == "gather_slot":
        return src_ref, land_ref.at[me]
    if mode == "gather_cols":
        w = src_ref.shape[1]
        return src_ref, land_ref.at[:, pl.ds(pl.multiple_of(me * w, LANES), w)]
    if mode == "scatter_slot":
        return src_ref.at[to], land_ref.at[me]
    w = land_ref.shape[2]
    return src_ref.at[:, pl.ds(pl.multiple_of(to * w, LANES), w)], land_ref.at[me]


BF16_ROWS = 16


def _land_shape(src, mode):
    if mode == "gather_slot":
        return (N_DEV,) + src.shape
    if mode == "gather_cols":
        return (src.shape[0], N_DEV * src.shape[1])
    if mode == "scatter_slot":
        return src.shape
    return (N_DEV, src.shape[0], src.shape[1] // N_DEV)


def _exchange_copies(src_ref, land_ref, send_sems, recv_sems, mode):
    x, y, c = _position()
    me = _flat(x, y, c)
    copies = []
    for k in range(1, N_DEV):
        peer = (x ^ ((k >> 2) & 1), y ^ ((k >> 1) & 1), c ^ (k & 1))
        src, dst = _copy_ends(src_ref, land_ref, mode, me, _flat(*peer))
        copies.append(pltpu.make_async_remote_copy(
            src_ref=src, dst_ref=dst, send_sem=send_sems.at[k - 1], recv_sem=recv_sems.at[k - 1],
            device_id=peer, device_id_type=MESH))
    return copies


def _place_own(src, mode, dev, *, name):
    rows = src.shape[1] if mode == "scatter_slot" else src.shape[0]
    tr = _tile(rows, 512, BF16_ROWS)
    land = _land_shape(src, mode)
    width = land[-1] if mode.startswith("scatter") else src.shape[1]
    slot = pl.BlockSpec((1, tr, width), lambda i, d: (d[0], i, 0))
    cols = pl.BlockSpec((tr, width), lambda i, d: (i, d[0]))
    whole = pl.BlockSpec((tr, width), lambda i, d: (i, 0))
    in_spec, out_spec = {"gather_slot": (whole, slot), "gather_cols": (whole, cols), "scatter_slot": (slot, slot),
                         "scatter_cols": (cols, slot)}[mode]

    def body(dev_ref, src_ref, land_ref):
        land_ref[...] = src_ref[...].reshape(land_ref.shape)

    return pl.pallas_call(
        body, name=name, out_shape=jax.ShapeDtypeStruct(land, src.dtype),
        grid_spec=pltpu.PrefetchScalarGridSpec(num_scalar_prefetch=1, grid=(rows // tr,), in_specs=[in_spec],
                                               out_specs=out_spec),
        compiler_params=_cp("parallel"),
    )(dev, src)


def _exchange_start(src, mode, dev, *, name):
    land = _place_own(src, mode, dev, name=name + "_own")
    n_copies = N_DEV - 1

    def body(src_ref, land_ref, send_sems, recv_sems, src_thru, land_thru, token):
        for cp in _exchange_copies(src_ref, land_ref, send_sems, recv_sems, mode):
            cp.start()
        token[...] = jnp.zeros_like(token)

    hbm = pltpu.with_memory_space_constraint
    *handle, token = pl.pallas_call(
        body, name=name,
        out_shape=(pltpu.SemaphoreType.DMA((n_copies,)), pltpu.SemaphoreType.DMA((n_copies,)),
                   pltpu.HBM(src.shape, src.dtype), pltpu.HBM(land.shape, land.dtype),
                   jax.ShapeDtypeStruct((SUBLANES, LANES), F32)),
        in_specs=(_HBM, _HBM), out_specs=(_SEM, _SEM, _HBM, _HBM, pl.BlockSpec(memory_space=pltpu.VMEM)),
        input_output_aliases={0: 2, 1: 3}, compiler_params=pltpu.CompilerParams(has_side_effects=_EFFECT),
    )(hbm(src, pltpu.HBM), hbm(land, pltpu.HBM))
    return (tuple(handle), mode), token


def _exchange_wait(pending, after, *, name):
    (send_sems, recv_sems, src_thru, land_thru), mode = pending

    def body(src_ref, land_ref, send_sems, recv_sems, after_ref, src_dead, got_ref):
        for cp in _exchange_copies(src_ref, land_ref, send_sems, recv_sems, mode):
            cp.wait_send()
            cp.wait_recv()

    return pl.pallas_call(
        body, name=name, out_shape=(pltpu.HBM(src_thru.shape, src_thru.dtype), pltpu.HBM(land_thru.shape, land_thru.dtype)),
        in_specs=(_HBM, _HBM, _SEM, _SEM, pl.BlockSpec(memory_space=pl.ANY)), out_specs=(_HBM, _HBM),
        input_output_aliases={0: 0, 1: 1}, compiler_params=pltpu.CompilerParams(has_side_effects=_EFFECT),
    )(src_thru, land_thru, send_sems, recv_sems, after)[1]


def _after(x, token):
    return x + token[0, 0].astype(x.dtype)


def _touch(*arrays, name):
    def body(*refs):
        refs[-1][...] = jnp.zeros_like(refs[-1])

    return pl.pallas_call(
        body, name=name, out_shape=jax.ShapeDtypeStruct((SUBLANES, LANES), F32),
        in_specs=[pl.BlockSpec(memory_space=pl.ANY)] * len(arrays), out_specs=pl.BlockSpec(memory_space=pltpu.VMEM),
    )(*arrays)


def _pad_cols(a, mult):
    pad = -a.shape[1] % mult
    return jnp.pad(a, ((0, 0), (0, pad))) if pad else a


def _pack(arrs, cols):
    flat = jnp.concatenate([a.reshape(-1).astype(F32) for a in arrs])
    sizes = [int(a.size) for a in arrs]
    flat = jnp.pad(flat, (0, -flat.shape[0] % (SUBLANES * cols)))
    return flat.reshape(-1, cols), sizes


def _unpack(flat2d, sizes, shapes):
    flat = flat2d.reshape(-1)
    out, o = [], 0
    for n, s in zip(sizes, shapes):
        out.append(flat[o:o + n].reshape(s))
        o += n
    return out


PACK_COLS = SUBLANES * LANES


def kernel(x, norm_mix_w, w_in, conv_a_w, conv_a_b, dt_bias, a_log, d_a, norm_a_w, w_proj_a, s5_lam_re, s5_lam_im, s5_log_dt, s5_b_re, s5_b_im, s5_c_re, s5_c_im, s5_d, w_s5_glu, w_out, norm_ffn_w, w_up, conv_ffn_w, conv_ffn_b, w_down, norm_final_w, loss_target, m_norm_mix_w, m_w_in, m_conv_a_w, m_conv_a_b, m_dt_bias, m_a_log, m_d_a, m_norm_a_w, m_w_proj_a, m_s5_lam_re, m_s5_lam_im, m_s5_log_dt, m_s5_b_re, m_s5_b_im, m_s5_c_re, m_s5_c_im, m_s5_d, m_w_s5_glu, m_w_out, m_norm_ffn_w, m_w_up, m_conv_ffn_w, m_conv_ffn_b, m_w_down, m_norm_final_w, v_norm_mix_w, v_w_in, v_conv_a_w, v_conv_a_b, v_dt_bias, v_a_log, v_d_a, v_norm_a_w, v_w_proj_a, v_s5_lam_re, v_s5_lam_im, v_s5_log_dt, v_s5_b_re, v_s5_b_im, v_s5_c_re, v_s5_c_im, v_s5_d, v_w_s5_glu, v_w_out, v_norm_ffn_w, v_w_up, v_conv_ffn_w, v_conv_ffn_b, v_w_down, v_norm_final_w):
    weights = dict(norm_mix_w=norm_mix_w, w_in=w_in, conv_a_w=conv_a_w, conv_a_b=conv_a_b, dt_bias=dt_bias, a_log=a_log, d_a=d_a, norm_a_w=norm_a_w, w_proj_a=w_proj_a, s5_lam_re=s5_lam_re, s5_lam_im=s5_lam_im, s5_log_dt=s5_log_dt, s5_b_re=s5_b_re, s5_b_im=s5_b_im, s5_c_re=s5_c_re, s5_c_im=s5_c_im, s5_d=s5_d, w_s5_glu=w_s5_glu, w_out=w_out, norm_ffn_w=norm_ffn_w, w_up=w_up, conv_ffn_w=conv_ffn_w, conv_ffn_b=conv_ffn_b, w_down=w_down, norm_final_w=norm_final_w)
    moms = dict(norm_mix_w=m_norm_mix_w, w_in=m_w_in, conv_a_w=m_conv_a_w, conv_a_b=m_conv_a_b, dt_bias=m_dt_bias, a_log=m_a_log, d_a=m_d_a, norm_a_w=m_norm_a_w, w_proj_a=m_w_proj_a, s5_lam_re=m_s5_lam_re, s5_lam_im=m_s5_lam_im, s5_log_dt=m_s5_log_dt, s5_b_re=m_s5_b_re, s5_b_im=m_s5_b_im, s5_c_re=m_s5_c_re, s5_c_im=m_s5_c_im, s5_d=m_s5_d, w_s5_glu=m_w_s5_glu, w_out=m_w_out, norm_ffn_w=m_norm_ffn_w, w_up=m_w_up, conv_ffn_w=m_conv_ffn_w, conv_ffn_b=m_conv_ffn_b, w_down=m_w_down, norm_final_w=m_norm_final_w)
    vars_ = dict(norm_mix_w=v_norm_mix_w, w_in=v_w_in, conv_a_w=v_conv_a_w, conv_a_b=v_conv_a_b, dt_bias=v_dt_bias, a_log=v_a_log, d_a=v_d_a, norm_a_w=v_norm_a_w, w_proj_a=v_w_proj_a, s5_lam_re=v_s5_lam_re, s5_lam_im=v_s5_lam_im, s5_log_dt=v_s5_log_dt, s5_b_re=v_s5_b_re, s5_b_im=v_s5_b_im, s5_c_re=v_s5_c_re, s5_c_im=v_s5_c_im, s5_d=v_s5_d, w_s5_glu=v_w_s5_glu, w_out=v_w_out, norm_ffn_w=v_norm_ffn_w, w_up=v_w_up, conv_ffn_w=v_conv_ffn_w, conv_ffn_b=v_conv_ffn_b, w_down=v_w_down, norm_final_w=v_norm_final_w)
    names = list(weights)
    col_sharded = ("w_in", "w_s5_glu", "w_up")
    row_sharded = ("w_proj_a", "w_out", "w_down")
    conv_sharded = ("conv_a_w", "conv_ffn_w")
    replicated = [n for n in names if n not in col_sharded + row_sharded + conv_sharded]

    t, d = x.shape[1:]
    x2, tgt = x.reshape(t, d), loss_target.reshape(t, d)
    nh = dt_bias.shape[-1]
    d_inner = norm_a_w.shape[-1]
    conv_dim = conv_a_b.shape[-1]
    gn = (conv_dim - d_inner) // 2
    ng = gn // D_STATE
    nr = nh // ng
    rp = d_inner // ng
    d_s5 = s5_d.shape[-1]
    gs, ps = s5_lam_re.shape[1:]
    cs = d_s5 // gs
    n_oct = gs // 8
    assert (gs * ps) % SCAN_COLS == 0 and 8 * cs == LANES and gs % 8 == 0
    d_ff = w_down.shape[1] * N_DEV
    dev = _flat(*_position())
    dev1 = dev.reshape(1).astype(jnp.int32)

    ka, kf = conv_a_w.shape[1], conv_ffn_w.shape[1]
    taps = jnp.concatenate([conv_a_w[0].reshape(1, -1), conv_ffn_w[0].reshape(1, -1)], axis=1)
    taps, taps_done = _all_gather(taps, name="ag_conv_taps")
    taps = taps[:, 0]

    def by_cols(n):
        return n in ("w_s5_glu", "w_up") and weights[n].shape[2] % LANES == 0

    w_in_blocks, started = _all_gather(_after(w_in[0], taps_done).astype(BF16), name="ag_w_in")
    pending = {}
    for n in ("w_proj_a", "w_s5_glu", "w_out", "w_up", "w_down"):
        shard = _after(weights[n][0], started).astype(BF16)
        pending[n], token = _exchange_start(shard, "gather_cols" if by_cols(n) else "gather_slot", dev1,
                                            name="ag_" + n)
        started = started + token

    def gathered(n, after):
        g = _exchange_wait(pending[n], after, name="agw_" + n)
        if by_cols(n):
            return g
        if n in row_sharded:
            return g.reshape(-1, g.shape[2])
        return jnp.transpose(g, (1, 0, 2)).reshape(g.shape[1], -1)

    seg_sizes = dict(z=d_inner, xs=d_inner, bm=gn, cm=gn, dt=nh, u=d_s5, ga=d, gb=d)
    seg_names = tuple(seg_sizes)
    pieces, seg_at = _w_in_pieces(seg_sizes, ("z", "xs", "ga", "gb", "bm", "cm", "u", "dt"), w_in.shape[2])
    na = ka * conv_a_w.shape[2]
    cw_a = jnp.transpose(taps[:, :na].reshape(N_DEV, ka, -1), (1, 0, 2)).reshape(ka, conv_dim)
    cw_f = jnp.transpose(taps[:, na:].reshape(N_DEV, kf, -1), (1, 0, 2)).reshape(kf, 2 * d_ff)
    cb_a, cb_f = conv_a_b, conv_ffn_b
    a_cols = {"xs": slice(0, d_inner), "bm": slice(d_inner, d_inner + gn), "cm": slice(d_inner + gn, conv_dim)}

    w1 = norm_mix_w.reshape(1, 1, d) + started[0, 0]
    hn1, = _blocked_fwd(_rms_fn, [x2], [w1], [(d, BF16)], tb=256, name="rms1")
    w_in_p = _w_in_pack(w_in_blocks, pieces, seg_at, seg_sizes, name="w_in_pack")
    pre = {sn: _mm(hn1, w_in_p, b_win=seg_at[sn], name="in_" + sn) for sn in seg_names}
    act_a = {sn: _conv_fwd(_comb_silu, [pre[sn]], [cw_a[:, a_cols[sn]]], [cb_a[:, a_cols[sn]]], out_dtype=F32,
                           name="conv_a_" + sn) for sn in a_cols}
    dtr3 = jnp.transpose(pre["dt"][:, :nh].reshape(t, ng, nr), (1, 0, 2))
    dtb3, alog3, dsk3 = (p.reshape(ng, 1, nr) for p in (dt_bias, a_log, d_a))
    nw3 = norm_a_w.reshape(ng, 1, rp)
    yn, hsave = _ssd_fwd(act_a["xs"], act_a["bm"], act_a["cm"], pre["z"], dtr3, dtb3, alog3, dsk3, nw3, name="ssd")
    w_proj = gathered("w_proj_a", yn)
    y_a = _mm(yn, w_proj, name="proj_a")

    lam_re3, lam_im3 = s5_lam_re[0][:, None, :], s5_lam_im[0][:, None, :]
    logdt3 = s5_log_dt[0][:, None, None]
    bt_re, bt_im = jnp.transpose(s5_b_re[0], (0, 2, 1)), jnp.transpose(s5_b_im[0], (0, 2, 1))
    lb_re3, lb_im3, bb_re, bb_im = _s5_params(lam_re3, lam_im3, logdt3, bt_re, bt_im, name="s5_params")
    eye = jnp.eye(8, dtype=F32)

    def diag_b(bt):
        return (bt.reshape(n_oct, 8, cs, 1, ps) * eye[None, :, None, :, None]).reshape(n_oct, 8 * cs, 8 * ps)

    def undiag_b(blk):
        return (blk.reshape(n_oct, 8, cs, 8, ps) * eye[None, :, None, :, None]).sum(axis=3).reshape(gs, cs, ps)

    def diag_c(cm):
        ct = jnp.transpose(cm.reshape(n_oct, 8, cs, ps), (0, 1, 3, 2))
        return (ct[:, :, :, None, :] * eye[None, :, None, :, None]).reshape(n_oct, 8 * ps, 8 * cs)

    def undiag_c(blk):
        ct = (blk.reshape(n_oct, 8, ps, 8, cs) * eye[None, :, None, :, None]).sum(axis=3)
        return jnp.transpose(ct, (0, 1, 3, 2)).reshape(gs, cs, ps)

    b_blk_re, b_blk_im = diag_b(bb_re), diag_b(bb_im)
    c_blk_re, c_blk_imn = diag_c(s5_c_re[0]), diag_c(-s5_c_im[0])
    d3 = s5_d.reshape(n_oct, 1, LANES)
    lb_re, lb_im = lb_re3.reshape(1, gs * ps), lb_im3.reshape(1, gs * ps)
    u = pre["u"]
    bu_re, bu_im = _blocked_fwd(_s5_bu_fn, [u], [b_blk_re, b_blk_im], [(gs * ps, F32)] * 2, nj=n_oct, tb=512,
                                name="s5_bu")
    s_re, s_im = _s5_scan_fwd(bu_re, bu_im, lb_re, lb_im, name="s5_scan")
    yb, = _blocked_fwd(_s5_out_fn, [s_re, s_im, u], [c_blk_re, c_blk_imn, d3], [(d_s5, BF16)], nj=n_oct, tb=512,
                       name="s5_out")
    w_glu = gathered("w_s5_glu", yb)
    glu_v = _mm(yb, w_glu, b_win=(0, d), name="glu_v")
    glu_g = _mm(yb, w_glu, b_win=(d, d), name="glu_g")
    merged, = _blocked_fwd(_merge_fn, [glu_v, glu_g, pre["ga"], pre["gb"], y_a], [], [(d, BF16)], tb=256,
                           name="merge")
    w_o = gathered("w_out", merged)
    h1 = _mm(merged, w_o, acc=x2, name="out_proj")
    w2 = norm_ffn_w.reshape(1, 1, d)
    hn2, = _blocked_fwd(_rms_fn, [h1], [w2], [(d, BF16)], tb=256, name="rms2")
    w_u = gathered("w_up", hn2)
    up_g = _mm(hn2, w_u, b_win=(0, d_ff), name="up_g")
    up_v = _mm(hn2, w_u, b_win=(d_ff, d_ff), name="up_v")
    f_w = [cw_f[:, :d_ff], cw_f[:, d_ff:]]
    f_b = [cb_f[:, :d_ff], cb_f[:, d_ff:]]
    act = _conv_fwd(_comb_glu, [up_g, up_v], f_w, f_b, out_dtype=BF16, name="conv_ffn")
    w_dn = gathered("w_down", act)
    h2 = _mm(act, w_dn, acc=h1, name="down")
    loss_tile, dh2, dh2_b, g_final = _loss_head(h2, norm_final_w.reshape(1, d), tgt, name="loss_head")

    grads, scattering = {}, {}

    def scatter_start(n, g):
        if by_cols(n):
            src, mode = g, "scatter_cols"
        elif n in row_sharded:
            src, mode = g.reshape(N_DEV, -1, g.shape[1]), "scatter_slot"
        elif n == "w_in":
            src, mode = g, "scatter_slot"
        else:
            src, mode = jnp.transpose(g.reshape(g.shape[0], N_DEV, -1), (1, 0, 2)), "scatter_slot"
        scattering[n], token = _exchange_start(src, mode, dev1, name="rs_" + n)
        return token

    d_act = _mm(dh2_b, w_dn, tb=True, name="d_act")
    g_down = _mm(act, dh2_b, ta=True, out_dtype=BF16, name="g_w_down")
    tok = scatter_start("w_down", g_down)
    (dup_g, dwf_g, dbf_g), (dup_v, dwf_v, dbf_v) = _conv_bwd(
        _comb_glu, [up_g, up_v], f_w, [_after(f_b[0], tok), f_b[1]], d_act, dx_dtype=BF16, name="conv_ffn_bwd")
    dhn2 = _mm(dup_g, w_u, tb=True, b_win=(0, d_ff), name="d_hn2_g")
    dhn2 = _mm(dup_v, w_u, tb=True, b_win=(d_ff, d_ff), acc=dhn2, name="d_hn2_v")
    g_up = _mm(hn2, dup_g, ta=True, into=(lax.empty((d, 2 * d_ff), BF16), 0), name="g_w_up_g")
    g_up = _mm(hn2, dup_v, ta=True, into=(g_up, d_ff), name="g_w_up_v")
    tok = scatter_start("w_up", g_up)
    dh1, dh1_b, g_w2 = _blocked_bwd(_rms_fn, [h1], [_after(w2, tok)], [dhn2], [(F32, BF16)], adds={0: dh2}, tb=256,
                                    name="rms2_bwd")
    d_merged = _mm(dh1_b, w_o, tb=True, name="d_merged")
    g_out = _mm(merged, dh1_b, ta=True, out_dtype=BF16, name="g_w_out")
    tok = scatter_start("w_out", g_out)
    dglu_v, dglu_g, dga, dgb, dy_a = _blocked_bwd(
        _merge_fn, [glu_v, glu_g, pre["ga"], pre["gb"], y_a], [], [d_merged], [BF16] * 5, tb=128, name="merge_bwd")
    dyb = _mm(dglu_v, w_glu, tb=True, b_win=(0, d), name="d_yb_v")
    dyb = _mm(dglu_g, w_glu, tb=True, b_win=(d, d), acc=dyb, name="d_yb_g")
    g_glu = _mm(yb, dglu_v, ta=True, into=(lax.empty((d_s5, 2 * d), BF16), 0), name="g_w_glu_v")
    g_glu = _mm(yb, dglu_g, ta=True, into=(g_glu, d), name="g_w_glu_g")
    tok = tok + scatter_start("w_s5_glu", g_glu)
    ds_re, ds_im, du_skip, dc_blk_re, dc_blk_imn, dd3 = _blocked_bwd(
        _s5_out_fn, [s_re, s_im, u], [c_blk_re, c_blk_imn, _after(d3, tok)], [dyb], [F32, F32, F32], nj=n_oct, tb=512,
        name="s5_out_bwd")
    dbu_re, dbu_im, dlb_re, dlb_im = _s5_scan_bwd(s_re, s_im, ds_re, ds_im, lb_re, lb_im, name="s5_scan_bwd")
    du, db_blk_re, db_blk_im = _blocked_bwd(
        _s5_bu_fn, [u], [b_blk_re, b_blk_im], [dbu_re, dbu_im], [BF16], adds={0: du_skip}, nj=n_oct, tb=512,
        name="s5_bu_bwd")
    g_lre, g_lim, g_ldt, g_bt_re, g_bt_im = _s5_params(
        lam_re3, lam_im3, logdt3, bt_re, bt_im,
        cts=(dlb_re.reshape(gs, 1, ps), dlb_im.reshape(gs, 1, ps), undiag_b(db_blk_re), undiag_b(db_blk_im)),
        name="s5_params_bwd")
    grads["s5_lam_re"], grads["s5_lam_im"] = g_lre.reshape(s5_lam_re.shape), g_lim.reshape(s5_lam_im.shape)
    grads["s5_log_dt"] = g_ldt.reshape(s5_log_dt.shape)
    grads["s5_b_re"] = jnp.transpose(g_bt_re, (0, 2, 1)).reshape(s5_b_re.shape)
    grads["s5_b_im"] = jnp.transpose(g_bt_im, (0, 2, 1)).reshape(s5_b_im.shape)
    grads["s5_c_re"] = undiag_c(dc_blk_re).reshape(s5_c_re.shape)
    grads["s5_c_im"] = -undiag_c(dc_blk_imn).reshape(s5_c_im.shape)
    grads["s5_d"] = dd3.reshape(s5_d.shape)

    dyn = _mm(dy_a, w_proj, tb=True, name="d_yn")
    g_proj = _mm(yn, dy_a, ta=True, out_dtype=BF16, name="g_w_proj_a")
    tok = scatter_start("w_proj_a", g_proj)
    dxs, dbm, dcm, dz, ddtr3, g_dtb, g_alog, g_dsk, g_nw = _ssd_bwd(
        act_a["xs"], act_a["bm"], act_a["cm"], pre["z"], dtr3, hsave, dtb3, alog3, dsk3, _after(nw3, tok), dyn,
        name="ssd_bwd")
    grads["dt_bias"], grads["a_log"], grads["d_a"] = (g.reshape(1, nh) for g in (g_dtb, g_alog, g_dsk))
    grads["norm_a_w"] = g_nw.reshape(1, d_inner)
    dpre = {"z": dz, "u": du, "ga": dga, "gb": dgb}
    dcw, dcb = {}, {}
    for sn, dact in (("xs", dxs), ("bm", dbm), ("cm", dcm)):
        (dpre[sn], dcw[sn], dcb[sn]), = _conv_bwd(
            _comb_silu, [pre[sn]], [cw_a[:, a_cols[sn]]], [cb_a[:, a_cols[sn]]], dact, dx_dtype=BF16,
            name="conv_a_bwd_" + sn)
    dpre["dt"] = _pad_cols(jnp.transpose(ddtr3, (1, 0, 2)).reshape(t, nh), LANES).astype(BF16)
    g_in = _w_in_unpack({sn: _mm(hn1, dpre[sn], ta=True, name="g_w_in_" + sn) for sn in seg_names}, pieces,
                        w_in.shape[2], name="w_in_unpack")
    tok = scatter_start("w_in", g_in)
    dhn1 = _mm(_after(dpre["dt"], tok), w_in_p, tb=True, b_win=seg_at["dt"], name="d_hn1_dt")
    for sn in seg_names:
        if sn != "dt":
            dhn1 = _mm(dpre[sn], w_in_p, tb=True, b_win=seg_at[sn], acc=dhn1, name="d_hn1_" + sn)
    dx, g_w1 = _blocked_bwd(_rms_fn, [x2], [w1], [dhn1], [F32], adds={0: dh1}, tb=256, name="rms1_bwd")

    grads["norm_mix_w"], grads["norm_ffn_w"] = g_w1.reshape(1, d), g_w2.reshape(1, d)
    grads["norm_final_w"] = g_final.reshape(d)
    grads["conv_a_b"] = jnp.concatenate([dcb["xs"], dcb["bm"], dcb["cm"]], axis=1)
    grads["conv_ffn_b"] = jnp.concatenate([dbf_g, dbf_v], axis=1)
    g_cw_a = jnp.concatenate([dcw["xs"], dcw["bm"], dcw["cm"]], axis=1)
    g_cw_f = jnp.concatenate([dwf_g, dwf_v], axis=1)

    small = [grads[n] for n in replicated] + [g_cw_a, g_cw_f, loss_tile[:1, :1]]
    packed, sizes = _pack(small, PACK_COLS)
    summed = _sum_parts(_all_gather(packed, name="ag_small_grads")[0], name="sum_small_grads")
    *rep_sums, s_cw_a, s_cw_f, loss = _unpack(summed, sizes, [a.shape for a in small])
    for n, g in zip(replicated, rep_sums):
        grads[n] = g
    wa, wf = conv_a_w.shape[2], conv_ffn_w.shape[2]
    grads["conv_a_w"] = lax.dynamic_slice_in_dim(s_cw_a, dev * wa, wa, axis=1)[None]
    grads["conv_ffn_w"] = lax.dynamic_slice_in_dim(s_cw_f, dev * wf, wf, axis=1)[None]

    delta, new_m, new_v = {}, {}, {}
    done = dx
    for n in ("w_down", "w_up", "w_out", "w_s5_glu", "w_proj_a", "w_in"):
        shape = weights[n].shape
        two_d = lambda a: a.reshape(shape[-2], shape[-1])
        w2, m2, v2 = two_d(weights[n]), two_d(moms[n]), two_d(vars_[n])
        land = _exchange_wait(scattering[n], _touch(done, w2, m2, v2, name="ready_" + n), name="rsw_" + n)
        g = _sum_parts(land, name="rs_sum_" + n)
        grads[n] = g.reshape(shape)
        dl, nm, nv = _adamw(w2, g, m2, v2, name="adamw_" + n)
        delta[n], new_m[n], new_v[n] = dl.reshape(shape), nm.reshape(shape), nv.reshape(shape)
        done = dl
    for n in replicated + list(conv_sharded):
        shape = weights[n].shape
        two_d = lambda a: a.reshape(-1, shape[-1])
        dl, nm, nv = _adamw(two_d(weights[n]), two_d(grads[n]), two_d(moms[n]), two_d(vars_[n]), name="adamw_" + n)
        delta[n], new_m[n], new_v[n] = dl.reshape(shape), nm.reshape(shape), nv.reshape(shape)

    return (loss.reshape(()), dx.reshape(x.shape), *[grads[n] for n in names], *[delta[n] for n in names],
            *[new_m[n] for n in names], *[new_v[n] for n in names])
```

```python
import functools

import jax
import jax.numpy as jnp
from jax import lax
from jax.experimental import pallas as pl
from jax.experimental.pallas import tpu as pltpu

F32 = jnp.float32
BF16 = jnp.bfloat16
HIGHEST = lax.Precision.HIGHEST
MESH = pl.DeviceIdType.MESH

EPS = 1e-6
EIG_MAX = -1e-4
D_STATE = 128
CHUNK = 256
ADAM_LR = 0.001
ADAM_B1 = 0.9
ADAM_B2 = 0.999
ADAM_EPS = 1e-08
ADAM_WD = 0.01
ADAM_STEP = 10
N_DEV = 8
LANES = 128
SUBLANES = 8
VMEM_LIMIT = 56 * 1024 * 1024
MM_MAX_K = 4096


def _cp(*sem):
    return pltpu.CompilerParams(dimension_semantics=sem, vmem_limit_bytes=VMEM_LIMIT)


def _tile(dim, pref, unit=LANES):
    if dim <= unit:
        return dim
    t = (min(pref, dim) // unit) * unit
    while dim % t:
        t -= unit
    return t


_DIMS = {"nn": (((1,), (0,)), ((), ())), "nt": (((1,), (1,)), ((), ())), "tn": (((0,), (0,)), ((), ()))}


def _dot(a, b, kind):
    return lax.dot_general(a.astype(BF16), b.astype(BF16), _DIMS[kind], preferred_element_type=F32)


@functools.partial(jax.custom_vjp, nondiff_argnums=(2,))
def _bdot(a, b, kind):
    return _dot(a, b, kind)


def _bdot_fwd(a, b, kind):
    return _dot(a, b, kind), (a, b)


def _bdot_bwd(kind, res, g):
    a, b = res
    if kind == "nn":
        return _dot(g, b, "nt"), _dot(a, g, "tn")
    if kind == "nt":
        return _dot(g, b, "nn"), _dot(g, a, "tn")
    return _dot(b, g, "nt"), _dot(a, g, "nn")


_bdot.defvjp(_bdot_fwd, _bdot_bwd)


def _mm(a, b, *, ta=False, tb=False, acc=None, out_dtype=F32, name, b_win=None, into=None):
    assert not (ta and tb)
    m, k = (a.shape[1], a.shape[0]) if ta else a.shape
    b_off, b_size = b_win or (0, b.shape[1])
    n = b.shape[0] if tb else b_size
    assert (b_size if tb else b.shape[0]) == k, (a.shape, b.shape, ta, tb, b_win)
    o_off = into[1] if into else 0
    nk = -(-k // MM_MAX_K)
    while k % nk or (k // nk) % LANES or (tb and b_off % (k // nk)):
        nk += 1
    tk = k // nk
    tm, tn = _tile(m, 1024), _tile(n, 1024)
    while o_off % tn or (not tb and b_off % tn):
        tn = _tile(n, tn - LANES)
    kind = "tn" if ta else ("nt" if tb else "nn")
    a_spec = pl.BlockSpec((tk, tm), lambda i, j, l: (l, i)) if ta else pl.BlockSpec((tm, tk), lambda i, j, l: (i, l))
    if tb:
        b_spec = pl.BlockSpec((tn, tk), lambda i, j, l: (j, l + b_off // tk))
    else:
        b_spec = pl.BlockSpec((tk, tn), lambda i, j, l: (l, j + b_off // tn))
    c_spec = pl.BlockSpec((tm, tn), lambda i, j, l: (i, j))
    o_spec = pl.BlockSpec((tm, tn), lambda i, j, l: (i, j + o_off // tn))
    has_acc = acc is not None

    def body(*refs):
        a_ref, b_ref = refs[:2]
        c_ref = refs[2] if has_acc else None
        o_ref = refs[2 + has_acc + (into is not None)]
        if nk == 1:
            res = _dot(a_ref[...], b_ref[...], kind)
            if has_acc:
                res = res + c_ref[...].astype(F32)
            o_ref[...] = res.astype(o_ref.dtype)
            return
        acc_ref = refs[-1]
        l = pl.program_id(2)

        @pl.when(l == 0)
        def _():
            if has_acc:
                acc_ref[...] = c_ref[...].astype(F32)
            else:
                acc_ref[...] = jnp.zeros_like(acc_ref)

        acc_ref[...] += _dot(a_ref[...], b_ref[...], kind)

        @pl.when(l == nk - 1)
        def _():
            o_ref[...] = acc_ref[...].astype(o_ref.dtype)

    ins = [a, b] + ([acc] if has_acc else []) + ([into[0]] if into else [])
    in_specs = [a_spec, b_spec] + ([c_spec] if has_acc else []) + ([pl.BlockSpec(memory_space=pl.ANY)] if into else [])
    out_shape = jax.ShapeDtypeStruct(into[0].shape, into[0].dtype) if into else jax.ShapeDtypeStruct((m, n), out_dtype)
    return pl.pallas_call(
        body, name=name, grid=(m // tm, n // tn, nk), in_specs=in_specs, out_specs=o_spec, out_shape=out_shape,
        input_output_aliases={len(ins) - 1: 0} if into else {},
        scratch_shapes=[pltpu.VMEM((tm, tn), F32)] if nk > 1 else [],
        compiler_params=_cp("parallel", "parallel", "arbitrary"),
    )(*ins)


def _w_in_pieces(seg_sizes, seg_order, n_blk):
    layout, o = {}, 0
    for sn in seg_order:
        width = -(-seg_sizes[sn] // LANES) * LANES
        layout[sn] = (o, width)
        o += width
    pieces, start = [], 0
    for sn, sz in seg_sizes.items():
        lo = start
        while lo < start + sz:
            blk = lo // n_blk
            hi = min(start + sz, (blk + 1) * n_blk)
            pieces.append((blk, lo - blk * n_blk, sn, lo - start, layout[sn][0] + lo - start, hi - lo))
            lo = hi
        start += sz
    return pieces, layout


def _w_in_pack(gathered, pieces, layout, seg_sizes, *, name, tr=256):
    _, k, n_blk = gathered.shape
    n_pad = sum(w for _, w in layout.values())

    def body(g_ref, o_ref):
        for sn, (off, width) in layout.items():
            if width != seg_sizes[sn]:
                o_ref[:, pl.ds(off + seg_sizes[sn], width - seg_sizes[sn])] = jnp.zeros(
                    (tr, width - seg_sizes[sn]), o_ref.dtype)
        for blk, src, _, _, dst, width in pieces:
            o_ref[:, pl.ds(dst, width)] = g_ref[blk, :, pl.ds(src, width)]

    return pl.pallas_call(
        body, name=name, grid=(k // tr,), in_specs=[pl.BlockSpec((N_DEV, tr, n_blk), lambda i: (0, i, 0))],
        out_specs=pl.BlockSpec((tr, n_pad), lambda i: (i, 0)), out_shape=jax.ShapeDtypeStruct((k, n_pad), gathered.dtype),
        compiler_params=_cp("parallel"),
    )(gathered)


def _w_in_unpack(seg_grads, pieces, n_blk, *, name, tr=128):
    names = list(seg_grads)
    k = seg_grads[names[0]].shape[0]

    def body(*refs):
        o_ref = refs[-1]
        seg_ref = dict(zip(names, refs))
        for blk, dst, sn, src, _, width in pieces:
            o_ref[blk, :, pl.ds(dst, width)] = seg_ref[sn][:, pl.ds(src, width)].astype(o_ref.dtype)

    return pl.pallas_call(
        body, name=name, grid=(k // tr,),
        in_specs=[pl.BlockSpec((tr, seg_grads[sn].shape[1]), lambda i: (i, 0)) for sn in names],
        out_specs=pl.BlockSpec((N_DEV, tr, n_blk), lambda i: (0, i, 0)),
        out_shape=jax.ShapeDtypeStruct((N_DEV, k, n_blk), BF16), compiler_params=_cp("parallel"),
    )(*[seg_grads[sn] for sn in names])


def _row_spec(arr, tb, nj):
    return pl.BlockSpec((tb, arr.shape[1] // nj), lambda j, i: (i, j))


def _par_spec(arr):
    return pl.BlockSpec((1,) + arr.shape[1:], lambda j, i: (j, 0, 0))


def _blocked_fwd(fn, rows, params, outs, *, nj=1, tb, name):
    t = rows[0].shape[0]
    nr, npar = len(rows), len(params)

    def body(*refs):
        res = fn(*[r[...] for r in refs[:nr]], *[p[0] for p in refs[nr:nr + npar]])
        for o_ref, val in zip(refs[nr + npar:], res):
            o_ref[...] = val.astype(o_ref.dtype)

    return pl.pallas_call(
        body, name=name, grid=(nj, t // tb),
        in_specs=[_row_spec(a, tb, nj) for a in rows] + [_par_spec(p) for p in params],
        out_specs=[pl.BlockSpec((tb, c // nj), lambda j, i: (i, j)) for c, _ in outs],
        out_shape=[jax.ShapeDtypeStruct((t, c), dt) for c, dt in outs],
        compiler_params=_cp("parallel", "arbitrary"),
    )(*rows, *params)


def _blocked_bwd(fn, rows, params, cts, row_grad_dtypes, *, adds=None, nj=1, tb, name):
    t = rows[0].shape[0]
    nr, npar, nct = len(rows), len(params), len(cts)
    adds = adds or {}
    add_keys = sorted(adds)
    want, want_dtypes = [], []
    for k, dts in enumerate(row_grad_dtypes):
        for dt in (dts if isinstance(dts, tuple) else (dts,)):
            if dt is not None:
                want.append(k)
                want_dtypes.append(dt)

    def body(*refs):
        row_refs = refs[:nr]
        par_refs = refs[nr:nr + npar]
        ct_refs = refs[nr + npar:nr + npar + nct]
        add_refs = dict(zip(add_keys, refs[nr + npar + nct:nr + npar + nct + len(add_keys)]))
        out_refs = refs[nr + npar + nct + len(add_keys):]
        _, vjp = jax.vjp(fn, *[r[...] for r in row_refs], *[p[0] for p in par_refs])
        grads = vjp(tuple(c[...].astype(F32) for c in ct_refs))
        for o_ref, k in zip(out_refs, want):
            g = grads[k]
            if k in add_refs:
                g = g + add_refs[k][...].astype(F32)
            o_ref[...] = g.astype(o_ref.dtype)
        first = pl.program_id(1) == 0
        for o_ref, g in zip(out_refs[len(want):], grads[nr:]):
            @pl.when(first)
            def _(o_ref=o_ref):
                o_ref[...] = jnp.zeros_like(o_ref)
            o_ref[0] += g

    add_arrs = [adds[k] for k in add_keys]
    return pl.pallas_call(
        body, name=name, grid=(nj, t // tb),
        in_specs=[_row_spec(a, tb, nj) for a in rows] + [_par_spec(p) for p in params]
        + [_row_spec(c, tb, nj) for c in cts] + [_row_spec(a, tb, nj) for a in add_arrs],
        out_specs=[_row_spec(rows[k], tb, nj) for k in want] + [_par_spec(p) for p in params],
        out_shape=[jax.ShapeDtypeStruct(rows[k].shape, dt) for k, dt in zip(want, want_dtypes)]
        + [jax.ShapeDtypeStruct(p.shape, F32) for p in params],
        compiler_params=_cp("parallel", "arbitrary"),
    )(*rows, *params, *cts, *add_arrs)


def _rms_fn(x, w):
    return (x * lax.rsqrt(jnp.mean(x * x, axis=-1, keepdims=True) + EPS) * w,)


def _silu(x):
    return x * jax.nn.sigmoid(x)


def _merge_fn(glu_v, glu_g, g_a, g_b, y_a):
    y_b = glu_v * jax.nn.sigmoid(glu_g)
    return (jax.nn.sigmoid(g_a) * y_a + jax.nn.sigmoid(g_b) * y_b,)


def _s5_bu_fn(u, b_re, b_im):
    return _bdot(u, b_re, "nn"), _bdot(u, b_im, "nn")


def _s5_out_fn(s_re, s_im, u, c_re, c_im_neg, d):
    return (jax.nn.gelu(_bdot(s_re, c_re, "nn") + _bdot(s_im, c_im_neg, "nn") + d * u),)


HALO = SUBLANES


STRIP = 64


def _conv_strip(ext_ref, w_ref, b_ref, r0, cols):
    kw = w_ref.shape[0]
    xs = [ext_ref[pl.ds(r0 + HALO - kw + 1 + k, STRIP), cols] for k in range(kw)]
    c = b_ref[:, cols] + w_ref[0:1, cols] * xs[0]
    for k in range(1, kw):
        c = c + w_ref[k:k + 1, cols] * xs[k]
    return c, xs


def _fold(x):
    return x.reshape(STRIP // SUBLANES, SUBLANES, LANES).sum(axis=0)


def _conv_specs(xs, ws, bs, tb, cb, time_of):
    specs = []
    for x, w, b in zip(xs, ws, bs):
        specs += [
            pl.BlockSpec((HALO, cb), lambda j, i: (jnp.maximum(time_of(i) * (tb // HALO) - 1, 0), j)),
            pl.BlockSpec((tb, cb), lambda j, i: (time_of(i), j)),
            pl.BlockSpec((w.shape[0], cb), lambda j, i: (0, j)),
            pl.BlockSpec((1, cb), lambda j, i: (0, j)),
        ]
    return specs


def _conv_fwd(comb, xs, ws, bs, *, out_dtype, name, tb=512):
    t, c = xs[0].shape
    cb = _tile(c, 512)
    ns = len(xs)

    def body(*refs):
        i = pl.program_id(1)
        o_ref = refs[4 * ns]
        exts = refs[4 * ns + 1:]
        for s in range(ns):
            xp_ref, xm_ref = refs[4 * s:4 * s + 2]
            exts[s][pl.ds(0, HALO), :] = jnp.where(i == 0, 0.0, xp_ref[...])
            exts[s][pl.ds(HALO, tb), :] = xm_ref[...]
        for c0 in range(0, cb, LANES):
            cols = pl.ds(c0, LANES)
            for r0 in range(0, tb, STRIP):
                cs = [_conv_strip(exts[s], refs[4 * s + 2], refs[4 * s + 3], r0, cols)[0] for s in range(ns)]
                o_ref[pl.ds(r0, STRIP), cols] = comb(*cs).astype(out_dtype)

    flat = [a for x, w, b in zip(xs, ws, bs) for a in (x, x, w, b)]
    return pl.pallas_call(
        body, name=name, grid=(c // cb, t // tb),
        in_specs=_conv_specs(xs, ws, bs, tb, cb, lambda i: i),
        out_specs=pl.BlockSpec((tb, cb), lambda j, i: (i, j)),
        out_shape=jax.ShapeDtypeStruct((t, c), out_dtype),
        scratch_shapes=[pltpu.VMEM((HALO + tb, cb), F32) for _ in range(ns)],
        compiler_params=_cp("parallel", "arbitrary"),
    )(*flat)


def _conv_bwd(comb, xs, ws, bs, dy, *, dx_dtype, name, tb=512):
    t, c = xs[0].shape
    cb = _tile(c, 512)
    ns = len(xs)
    nt = t // tb
    kw = ws[0].shape[0]

    def body(*refs):
        step = pl.program_id(1)
        dy_ref = refs[4 * ns]
        out_refs = refs[4 * ns + 1:4 * ns + 1 + 3 * ns]
        scratch = refs[4 * ns + 1 + 3 * ns:]
        exts, dcs, carries = scratch[:ns], scratch[ns:2 * ns], scratch[2 * ns:]

        @pl.when(step == 0)
        def _():
            for s in range(ns):
                carries[s][...] = jnp.zeros_like(carries[s])
                out_refs[3 * s + 1][...] = jnp.zeros_like(out_refs[3 * s + 1])
                out_refs[3 * s + 2][...] = jnp.zeros_like(out_refs[3 * s + 2])

        for s in range(ns):
            xp_ref, xm_ref = refs[4 * s:4 * s + 2]
            exts[s][pl.ds(0, HALO), :] = jnp.where(step == nt - 1, 0.0, xp_ref[...])
            exts[s][pl.ds(HALO, tb), :] = xm_ref[...]
            dcs[s][pl.ds(tb, HALO), :] = carries[s][...]
        for c0 in range(0, cb, LANES):
            cols = pl.ds(c0, LANES)
            acc_w = [[jnp.zeros((SUBLANES, LANES), F32) for _ in range(kw)] for _ in range(ns)]
            acc_b = [jnp.zeros((SUBLANES, LANES), F32) for _ in range(ns)]
            for r0 in range(0, tb, STRIP):
                strips = [_conv_strip(exts[s], refs[4 * s + 2], refs[4 * s + 3], r0, cols) for s in range(ns)]
                _, vjp = jax.vjp(comb, *[cs for cs, _ in strips])
                grads = vjp(dy_ref[pl.ds(r0, STRIP), cols].astype(F32))
                for s in range(ns):
                    dcs[s][pl.ds(r0, STRIP), cols] = grads[s]
                    acc_b[s] = acc_b[s] + _fold(grads[s])
                    for k in range(kw):
                        acc_w[s][k] = acc_w[s][k] + _fold(grads[s] * strips[s][1][k])
            for s in range(ns):
                dw_ref, db_ref = out_refs[3 * s + 1], out_refs[3 * s + 2]
                db_ref[:, cols] += jnp.sum(acc_b[s], axis=0, keepdims=True)
                for k in range(kw):
                    dw_ref[k:k + 1, cols] += jnp.sum(acc_w[s][k], axis=0, keepdims=True)
        for s in range(ns):
            w_ref, dx_ref = refs[4 * s + 2], out_refs[3 * s]
            for c0 in range(0, cb, LANES):
                cols = pl.ds(c0, LANES)
                for r0 in range(0, tb, STRIP):
                    dx = w_ref[kw - 1:kw, cols] * dcs[s][pl.ds(r0, STRIP), cols]
                    for k in range(kw - 1):
                        dx = dx + w_ref[k:k + 1, cols] * dcs[s][pl.ds(r0 + kw - 1 - k, STRIP), cols]
                    dx_ref[pl.ds(r0, STRIP), cols] = dx.astype(dx_dtype)
            carries[s][...] = dcs[s][pl.ds(0, HALO), :]

    flat = [a for x, w, b in zip(xs, ws, bs) for a in (x, x, w, b)]
    rev = lambda i: nt - 1 - i
    out_specs, out_shape = [], []
    for x, w, b in zip(xs, ws, bs):
        out_specs += [pl.BlockSpec((tb, cb), lambda j, i: (rev(i), j)),
                      pl.BlockSpec((w.shape[0], cb), lambda j, i: (0, j)),
                      pl.BlockSpec((1, cb), lambda j, i: (0, j))]
        out_shape += [jax.ShapeDtypeStruct((t, c), dx_dtype), jax.ShapeDtypeStruct(w.shape, F32),
                      jax.ShapeDtypeStruct(b.shape, F32)]
    res = pl.pallas_call(
        body, name=name, grid=(c // cb, nt),
        in_specs=_conv_specs(xs, ws, bs, tb, cb, rev) + [pl.BlockSpec((tb, cb), lambda j, i: (rev(i), j))],
        out_specs=out_specs, out_shape=out_shape,
        scratch_shapes=[pltpu.VMEM((HALO + tb, cb), F32) for _ in range(2 * ns)]
        + [pltpu.VMEM((HALO, cb), F32) for _ in range(ns)],
        compiler_params=_cp("parallel", "arbitrary"),
    )(*flat, dy)
    return [tuple(res[3 * s:3 * s + 3]) for s in range(ns)]


def _comb_silu(c):
    return _silu(c)


def _comb_glu(cg, cv):
    return _silu(cg) * cv


def _ssd_fn(nheads, hdim):
    def fn(x, bm, cm, z, dtr, hin, dtb, alog, dsk, nw):
        q = x.shape[0]
        dt = jax.nn.softplus(dtr + dtb)
        da = dt * (-jnp.exp(alog))
        li = lax.broadcasted_iota(jnp.int32, (q, q), 0)
        si = lax.broadcasted_iota(jnp.int32, (q, q), 1)
        causal = li >= si
        tri = causal.astype(F32)
        acs = jnp.dot(tri, da, precision=HIGHEST, preferred_element_type=F32)
        acs_row = lax.dot_general(da, tri, (((0,), (1,)), ((), ())), precision=HIGHEST,
                                  preferred_element_type=F32)
        cb = _bdot(cm, bm, "nt")
        ch = _bdot(cm, hin, "nn")
        ys, hs = [], []
        for r in range(nheads):
            cols = slice(r * hdim, (r + 1) * hdim)
            xr = x[:, cols]
            a_col = acs[:, r:r + 1]
            decay = jnp.exp(jnp.where(causal, a_col - acs_row[r:r + 1, :], -1e30))
            xd = xr * dt[:, r:r + 1]
            y_diag = _bdot(cb * decay, xd, "nn")
            y_off = ch[:, cols] * jnp.exp(a_col)
            last = acs[q - 1:q, r:r + 1]
            st = _bdot(bm * jnp.exp(last - a_col), xd, "tn")
            hs.append(jnp.exp(last) * hin[:, cols] + st)
            ys.append(y_diag + y_off + dsk[:, r:r + 1] * xr)
        y = jnp.concatenate(ys, axis=1) * _silu(z)
        yn = y * lax.rsqrt(jnp.mean(y * y, axis=-1, keepdims=True) + EPS) * nw
        return yn, jnp.concatenate(hs, axis=1)
    return fn


def _ssd_specs(rp, nr, time_of):
    row = lambda w: pl.BlockSpec((CHUNK, w), lambda g, c: (time_of(c), g))
    par = lambda w: pl.BlockSpec((1, 1, w), lambda g, c: (g, 0, 0))
    return dict(
        x=row(rp), bc=row(D_STATE), dtr=pl.BlockSpec((1, CHUNK, nr), lambda g, c: (g, time_of(c), 0)),
        h=pl.BlockSpec((1, 1, D_STATE, rp), lambda g, c: (g, time_of(c), 0, 0)), pr=par(nr), pw=par(rp))


def _ssd_fwd(xs, bm, cm, z, dtr, dtb, alog, dsk, nw, *, name):
    t = xs.shape[0]
    g, _, nr = dtr.shape
    rp = xs.shape[1] // g
    nc = t // CHUNK
    fn = _ssd_fn(nr, rp // nr)
    sp = _ssd_specs(rp, nr, lambda c: c)

    def body(x_ref, b_ref, c_ref, z_ref, dtr_ref, dtb_ref, al_ref, dsk_ref, nw_ref, yn_ref, hs_ref, h_ref):
        @pl.when(pl.program_id(1) == 0)
        def _():
            h_ref[...] = jnp.zeros_like(h_ref)
        hin = h_ref[...]
        hs_ref[0, 0] = hin
        yn, hout = fn(x_ref[...], b_ref[...], c_ref[...], z_ref[...], dtr_ref[0], hin,
                      dtb_ref[0], al_ref[0], dsk_ref[0], nw_ref[0])
        yn_ref[...] = yn.astype(yn_ref.dtype)
        h_ref[...] = hout

    return pl.pallas_call(
        body, name=name, grid=(g, nc),
        in_specs=[sp["x"], sp["bc"], sp["bc"], sp["x"], sp["dtr"], sp["pr"], sp["pr"], sp["pr"], sp["pw"]],
        out_specs=[sp["x"], sp["h"]],
        out_shape=[jax.ShapeDtypeStruct(xs.shape, BF16), jax.ShapeDtypeStruct((g, nc, D_STATE, rp), F32)],
        scratch_shapes=[pltpu.VMEM((D_STATE, rp), F32)],
        compiler_params=_cp("parallel", "arbitrary"),
    )(xs, bm, cm, z, dtr, dtb, alog, dsk, nw)


def _ssd_bwd(xs, bm, cm, z, dtr, hsave, dtb, alog, dsk, nw, dyn, *, name):
    t = xs.shape[0]
    g, _, nr = dtr.shape
    rp = xs.shape[1] // g
    nc = t // CHUNK
    fn = _ssd_fn(nr, rp // nr)
    sp = _ssd_specs(rp, nr, lambda c: nc - 1 - c)

    def body(x_ref, b_ref, c_ref, z_ref, dtr_ref, hs_ref, dtb_ref, al_ref, dsk_ref, nw_ref, dyn_ref,
             dx_ref, db_ref, dc_ref, dz_ref, ddtr_ref, ddtb_ref, dal_ref, ddsk_ref, dnw_ref, dh_ref):
        first = pl.program_id(1) == 0

        @pl.when(first)
        def _():
            dh_ref[...] = jnp.zeros_like(dh_ref)
            for r in (ddtb_ref, dal_ref, ddsk_ref, dnw_ref):
                r[...] = jnp.zeros_like(r)

        _, vjp = jax.vjp(fn, x_ref[...], b_ref[...], c_ref[...], z_ref[...], dtr_ref[0], hs_ref[0, 0],
                         dtb_ref[0], al_ref[0], dsk_ref[0], nw_ref[0])
        dx, db, dc, dz, ddtr, dhin, ddtb, dal, ddsk, dnw = vjp((dyn_ref[...].astype(F32), dh_ref[...]))
        dx_ref[...] = dx
        db_ref[...] = db
        dc_ref[...] = dc
        dz_ref[...] = dz.astype(dz_ref.dtype)
        ddtr_ref[0] = ddtr
        dh_ref[...] = dhin
        ddtb_ref[0] += ddtb
        dal_ref[0] += dal
        ddsk_ref[0] += ddsk
        dnw_ref[0] += dnw

    sd = jax.ShapeDtypeStruct
    return pl.pallas_call(
        body, name=name, grid=(g, nc),
        in_specs=[sp["x"], sp["bc"], sp["bc"], sp["x"], sp["dtr"], sp["h"], sp["pr"], sp["pr"], sp["pr"], sp["pw"],
                  sp["x"]],
        out_specs=[sp["x"], sp["bc"], sp["bc"], sp["x"], sp["dtr"], sp["pr"], sp["pr"], sp["pr"], sp["pw"]],
        out_shape=[sd(xs.shape, F32), sd(bm.shape, F32), sd(cm.shape, F32), sd(z.shape, BF16), sd(dtr.shape, F32),
                   sd(dtb.shape, F32), sd(alog.shape, F32), sd(dsk.shape, F32), sd(nw.shape, F32)],
        scratch_shapes=[pltpu.VMEM((D_STATE, rp), F32)],
        compiler_params=_cp("parallel", "arbitrary"),
    )(xs, bm, cm, z, dtr, hsave, dtb, alog, dsk, nw, dyn)


def _s5_param_fn(lam_re, lam_im, log_dt, bt_re, bt_im):
    lr = jnp.minimum(lam_re, EIG_MAX)
    dt = jnp.exp(log_dt)
    mag = jnp.exp(lr * dt)
    lb_re = mag * jnp.cos(lam_im * dt)
    lb_im = mag * jnp.sin(lam_im * dt)
    n_re = lb_re - 1.0
    den = lr * lr + lam_im * lam_im
    k_re = (n_re * lr + lb_im * lam_im) / den
    k_im = (lb_im * lr - n_re * lam_im) / den
    return lb_re, lb_im, k_re * bt_re - k_im * bt_im, k_re * bt_im + k_im * bt_re


def _s5_params(lam_re, lam_im, log_dt, bt_re, bt_im, cts=None, *, name):
    args = (lam_re, lam_im, log_dt, bt_re, bt_im)
    n = len(args)

    def body(*refs):
        vals = [r[...] for r in refs[:n]]
        if cts is None:
            res = _s5_param_fn(*vals)
        else:
            _, vjp = jax.vjp(_s5_param_fn, *vals)
            res = vjp(tuple(r[...] for r in refs[n:n + 4]))
        for o_ref, v in zip(refs[-len(res):], res):
            o_ref[...] = v

    if cts is None:
        out = [lam_re, lam_im, bt_re, bt_im]
        ins = args
    else:
        out = list(args)
        ins = args + tuple(cts)
    return pl.pallas_call(
        body, name=name, out_shape=[jax.ShapeDtypeStruct(a.shape, F32) for a in out],
        compiler_params=pltpu.CompilerParams(vmem_limit_bytes=VMEM_LIMIT),
    )(*ins)


SCAN_COLS = 512


def _cmul(xr, xi, yr, yi):
    return xr * yr - xi * yi, xr * yi + xi * yr


def _scan_consts(a_re, a_im, cols, reverse):
    shape = (SUBLANES, cols)
    row = lax.broadcasted_iota(jnp.int32, shape, 0)
    dist = (SUBLANES - 1 - row) if reverse else row
    mr, mi = jnp.broadcast_to(a_re, shape), jnp.broadcast_to(a_im, shape)
    pr, pi = mr, mi
    mults = []
    for d in (1, 2, 4):
        mults.append((mr, mi))
        qr, qi = _cmul(pr, pi, mr, mi)
        has_bit = (dist & d) != 0
        pr, pi = jnp.where(has_bit, qr, pr), jnp.where(has_bit, qi, pi)
        mr, mi = _cmul(mr, mi, mr, mi)
    return mults, (pr, pi), dist


def _scan_group(xr, xi, consts, cr, ci, reverse):
    mults, (pr, pi), dist = consts
    for d, (mr, mi) in zip((1, 2, 4), mults):
        shift = (SUBLANES - d) if reverse else d
        sr = jnp.where(dist >= d, pltpu.roll(xr, shift, 0), 0.0)
        si = jnp.where(dist >= d, pltpu.roll(xi, shift, 0), 0.0)
        tr, ti = _cmul(mr, mi, sr, si)
        xr, xi = xr + tr, xi + ti
    last = slice(0, 1) if reverse else slice(SUBLANES - 1, SUBLANES)
    nr, ni = _cmul(pr[last], pi[last], cr, ci)
    tr, ti = _cmul(pr, pi, jnp.broadcast_to(cr, xr.shape), jnp.broadcast_to(ci, xr.shape))
    return xr + tr, xi + ti, xr[last] + nr, xi[last] + ni


def _scan_specs(tb, time_of):
    row = pl.BlockSpec((tb, SCAN_COLS), lambda j, i: (time_of(i), j))
    par = pl.BlockSpec((1, SCAN_COLS), lambda j, i: (0, j))
    return row, par


def _s5_scan_fwd(bu_re, bu_im, lb_re, lb_im, *, name, tb=512):
    t, c = bu_re.shape
    nj = c // SCAN_COLS
    row, par = _scan_specs(tb, lambda i: i)

    def body(bre_ref, bim_ref, lre_ref, lim_ref, sre_ref, sim_ref, cre_ref, cim_ref):
        @pl.when(pl.program_id(1) == 0)
        def _():
            cre_ref[...] = jnp.zeros_like(cre_ref)
            cim_ref[...] = jnp.zeros_like(cim_ref)
        consts = _scan_consts(lre_ref[...], lim_ref[...], SCAN_COLS, False)

        def group(k, carry):
            rows = pl.ds(pl.multiple_of(k * SUBLANES, SUBLANES), SUBLANES)
            sr, si, cr, ci = _scan_group(bre_ref[rows, :], bim_ref[rows, :], consts, *carry, False)
            sre_ref[rows, :] = sr
            sim_ref[rows, :] = si
            return cr, ci

        sr, si = lax.fori_loop(0, tb // SUBLANES, group, (cre_ref[...], cim_ref[...]), unroll=4)
        cre_ref[...] = sr
        cim_ref[...] = si

    return pl.pallas_call(
        body, name=name, grid=(nj, t // tb), in_specs=[row, row, par, par], out_specs=[row, row],
        out_shape=[jax.ShapeDtypeStruct((t, c), F32)] * 2,
        scratch_shapes=[pltpu.VMEM((1, SCAN_COLS), F32)] * 2,
        compiler_params=_cp("parallel", "arbitrary"),
    )(bu_re, bu_im, lb_re, lb_im)


def _s5_scan_bwd(s_re, s_im, ds_re, ds_im, lb_re, lb_im, *, name, tb=512):
    t, c = s_re.shape
    nj = c // SCAN_COLS
    nt = t // tb
    rev = lambda i: nt - 1 - i
    row, par = _scan_specs(tb, rev)
    prev = pl.BlockSpec((HALO, SCAN_COLS), lambda j, i: (jnp.maximum(rev(i) * (tb // HALO) - 1, 0), j))

    def body(sre_ref, sim_ref, pre_ref, pim_ref, dre_ref, dim_ref, lre_ref, lim_ref,
             gre_ref, gim_ref, dlre_ref, dlim_ref, cre_ref, cim_ref, ext_re, ext_im):
        step_id = pl.program_id(1)

        @pl.when(step_id == 0)
        def _():
            cre_ref[...] = jnp.zeros_like(cre_ref)
            cim_ref[...] = jnp.zeros_like(cim_ref)
            dlre_ref[...] = jnp.zeros_like(dlre_ref)
            dlim_ref[...] = jnp.zeros_like(dlim_ref)
        consts = _scan_consts(lre_ref[...], -lim_ref[...], SCAN_COLS, True)
        ngroups = tb // SUBLANES

        def group(k, carry):
            rows = pl.ds(pl.multiple_of((ngroups - 1 - k) * SUBLANES, SUBLANES), SUBLANES)
            gr, gi, cr, ci = _scan_group(dre_ref[rows, :], dim_ref[rows, :], consts, *carry, True)
            gre_ref[rows, :] = gr
            gim_ref[rows, :] = gi
            return cr, ci

        gr, gi = lax.fori_loop(0, ngroups, group, (cre_ref[...], cim_ref[...]), unroll=4)
        cre_ref[...] = gr
        cim_ref[...] = gi
        has_past = step_id != nt - 1
        ext_re[pl.ds(0, HALO), :] = jnp.where(has_past, pre_ref[...], 0.0)
        ext_im[pl.ds(0, HALO), :] = jnp.where(has_past, pim_ref[...], 0.0)
        ext_re[pl.ds(HALO, tb), :] = sre_ref[...]
        ext_im[pl.ds(HALO, tb), :] = sim_ref[...]
        pr, pi = ext_re[pl.ds(HALO - 1, tb), :], ext_im[pl.ds(HALO - 1, tb), :]
        g_re, g_im = gre_ref[...], gim_ref[...]
        dlre_ref[...] += jnp.sum(pr * g_re + pi * g_im, axis=0, keepdims=True)
        dlim_ref[...] += jnp.sum(pr * g_im - pi * g_re, axis=0, keepdims=True)

    return pl.pallas_call(
        body, name=name, grid=(nj, nt),
        in_specs=[row, row, prev, prev, row, row, par, par], out_specs=[row, row, par, par],
        out_shape=[jax.ShapeDtypeStruct((t, c), F32)] * 2 + [jax.ShapeDtypeStruct((1, c), F32)] * 2,
        scratch_shapes=[pltpu.VMEM((1, SCAN_COLS), F32)] * 2 + [pltpu.VMEM((HALO + tb, SCAN_COLS), F32)] * 2,
        compiler_params=_cp("parallel", "arbitrary"),
    )(s_re, s_im, s_re, s_im, ds_re, ds_im, lb_re, lb_im)


def _loss_fn(h, w, tgt):
    err = _rms_fn(h, w)[0] - tgt
    return 0.5 * jnp.sum(jnp.mean(err * err, axis=-1, keepdims=True), axis=0, keepdims=True)


def _loss_head(h, w, tgt, *, name, tb=256):
    t, d = h.shape

    def body(h_ref, w_ref, t_ref, loss_ref, dh_ref, dhb_ref, dw_ref):
        @pl.when(pl.program_id(0) == 0)
        def _():
            loss_ref[...] = jnp.zeros_like(loss_ref)
            dw_ref[...] = jnp.zeros_like(dw_ref)
        part, vjp = jax.vjp(_loss_fn, h_ref[...], w_ref[...], t_ref[...])
        dh, dw, _ = vjp(jnp.ones((1, 1), F32))
        loss_ref[...] += jnp.broadcast_to(part, loss_ref.shape)
        dh_ref[...] = dh
        dhb_ref[...] = dh.astype(BF16)
        dw_ref[...] += dw

    row = pl.BlockSpec((tb, d), lambda i: (i, 0))
    par = pl.BlockSpec((1, d), lambda i: (0, 0))
    return pl.pallas_call(
        body, name=name, grid=(t // tb,), in_specs=[row, par, row],
        out_specs=[pl.BlockSpec((SUBLANES, LANES), lambda i: (0, 0)), row, row, par],
        out_shape=[jax.ShapeDtypeStruct((SUBLANES, LANES), F32), jax.ShapeDtypeStruct((t, d), F32),
                   jax.ShapeDtypeStruct((t, d), BF16), jax.ShapeDtypeStruct((1, d), F32)],
        compiler_params=_cp("arbitrary"),
    )(h, w, tgt)


def _adamw(w, g, m, v, *, name):
    r, c = w.shape
    tr = _tile(r, 256, SUBLANES)

    def body(w_ref, g_ref, m_ref, v_ref, d_ref, nm_ref, nv_ref):
        g = g_ref[...]
        nm = ADAM_B1 * m_ref[...] + (1.0 - ADAM_B1) * g
        nv = ADAM_B2 * v_ref[...] + (1.0 - ADAM_B2) * (g * g)
        m_hat = nm / (1.0 - ADAM_B1 ** ADAM_STEP)
        v_hat = nv / (1.0 - ADAM_B2 ** ADAM_STEP)
        d_ref[...] = -ADAM_LR * (m_hat / (jnp.sqrt(v_hat) + ADAM_EPS) + ADAM_WD * w_ref[...])
        nm_ref[...] = nm
        nv_ref[...] = nv

    spec = pl.BlockSpec((tr, c), lambda i: (i, 0))
    return pl.pallas_call(
        body, name=name, grid=(r // tr,), in_specs=[spec] * 4, out_specs=[spec] * 3,
        out_shape=[jax.ShapeDtypeStruct((r, c), F32)] * 3, compiler_params=_cp("parallel"),
    )(w, g, m, v)


def _sum_parts(parts, *, name):
    _, r, c = parts.shape
    tr = _tile(r, 128, SUBLANES)

    def body(p_ref, o_ref):
        acc = p_ref[0].astype(F32)
        for k in range(1, N_DEV):
            acc = acc + p_ref[k].astype(F32)
        o_ref[...] = acc

    return pl.pallas_call(
        body, name=name, grid=(r // tr,), in_specs=[pl.BlockSpec((N_DEV, tr, c), lambda i: (0, i, 0))],
        out_specs=pl.BlockSpec((tr, c), lambda i: (i, 0)), out_shape=jax.ShapeDtypeStruct((r, c), F32),
        compiler_params=_cp("parallel"),
    )(parts)


def _position():
    return lax.axis_index("x"), lax.axis_index("y"), lax.axis_index("c")


def _flat(px, py, pc):
    return 4 * px + 2 * py + pc


def _all_gather(shard, *, name):
    def body(x_ref, out_ref, token, send_sems, recv_sems, local_sem):
        token[...] = jnp.zeros_like(token)
        x, y, c = _position()
        me, sibling = (x, y, c), (x, y, 1 - c)
        chips = [(1 - x, y), (x, 1 - y), (1 - x, 1 - y)]

        def copy(k, block, to, src=None):
            slot = out_ref.at[_flat(*block)]
            return pltpu.make_async_remote_copy(
                src_ref=slot if src is None else src, dst_ref=slot, send_sem=send_sems.at[k],
                recv_sem=recv_sems.at[k], device_id=to, device_id_type=MESH)

        mine = pltpu.make_async_copy(x_ref, out_ref.at[_flat(*me)], local_sem)
        mine.start()
        first = [copy(0, me, sibling, src=x_ref)]
        first += [copy(1 + j, me, (*chip, c), src=x_ref) for j, chip in enumerate(chips)]
        for cp in first:
            cp.start()
        passed = [copy(4 + j, (*chip, c), sibling) for j, chip in enumerate(chips)]
        for j, chip in enumerate(chips):
            copy(1 + j, (*chip, c), me).wait_recv()
            passed[j].start()
        copy(0, sibling, me).wait_recv()
        for j, chip in enumerate(chips):
            copy(4 + j, (*chip, 1 - c), me).wait_recv()
        for cp in first + passed:
            cp.wait_send()
        mine.wait()

    return pl.pallas_call(
        body, name=name,
        out_shape=(jax.ShapeDtypeStruct((N_DEV,) + shard.shape, shard.dtype),
                   jax.ShapeDtypeStruct((SUBLANES, LANES), F32)),
        in_specs=[pl.BlockSpec(memory_space=pl.ANY)],
        out_specs=(pl.BlockSpec(memory_space=pl.ANY), pl.BlockSpec(memory_space=pltpu.VMEM)),
        scratch_shapes=[pltpu.SemaphoreType.DMA((7,)), pltpu.SemaphoreType.DMA((7,)), pltpu.SemaphoreType.DMA(())],
    )(shard)


_HBM = pl.BlockSpec(memory_space=pltpu.HBM)
_SEM = pl.BlockSpec(memory_space=pltpu.SEMAPHORE)
_EFFECT = pltpu.SideEffectType.DATAFLOW_SIDE_EFFECTING


def _copy_ends(src_ref, land_ref, mode, me, to):
    if mode == "gather_slot":
        return src_ref, land_ref.at[me]
    if mode == "gather_cols":
        w = src_ref.shape[1]
        return src_ref, land_ref.at[:, pl.ds(pl.multiple_of(me * w, LANES), w)]
    if mode == "scatter_slot":
        return src_ref.at[to], land_ref.at[me]
    w = land_ref.shape[2]
    return src_ref.at[:, pl.ds(pl.multiple_of(to * w, LANES), w)], land_ref.at[me]


BF16_ROWS = 16


def _land_shape(src, mode):
    if mode == "gather_slot":
        return (N_DEV,) + src.shape
    if mode == "gather_cols":
        return (src.shape[0], N_DEV * src.shape[1])
    if mode == "scatter_slot":
        return src.shape
    return (N_DEV, src.shape[0], src.shape[1] // N_DEV)


OTHER_CHIPS = (2, 4, 6)


def _n_copies(mode):
    return {"gather_chip": 1 + len(OTHER_CHIPS), "forward": len(OTHER_CHIPS)}.get(mode, N_DEV - 1)


def _exchange_copies(src_ref, land_ref, send_sems, recv_sems, mode):
    x, y, c = _position()
    me = _flat(x, y, c)
    peer_of = lambda k: (x ^ ((k >> 2) & 1), y ^ ((k >> 1) & 1), c ^ (k & 1))
    if mode == "forward":
        slots = [land_ref.at[_flat(*peer_of(k))] for k in OTHER_CHIPS]
        plan = [(slot, slot, peer_of(1)) for slot in slots]
    elif mode == "gather_chip":
        plan = [(*_copy_ends(src_ref, land_ref, "gather_slot", me, _flat(*peer_of(k))), peer_of(k))
                for k in (1,) + OTHER_CHIPS]
    else:
        plan = [(*_copy_ends(src_ref, land_ref, mode, me, _flat(*peer_of(k))), peer_of(k)) for k in range(1, N_DEV)]
    return [pltpu.make_async_remote_copy(src_ref=src, dst_ref=dst, send_sem=send_sems.at[i], recv_sem=recv_sems.at[i],
                                         device_id=peer, device_id_type=MESH)
            for i, (src, dst, peer) in enumerate(plan)]


def _place_own(src, mode, dev, *, name):
    rows = src.shape[1] if mode == "scatter_slot" else src.shape[0]
    tr = _tile(rows, 512, BF16_ROWS)
    land = _land_shape(src, mode)
    width = land[-1] if mode.startswith("scatter") else src.shape[1]
    slot = pl.BlockSpec((1, tr, width), lambda i, d: (d[0], i, 0))
    cols = pl.BlockSpec((tr, width), lambda i, d: (i, d[0]))
    whole = pl.BlockSpec((tr, width), lambda i, d: (i, 0))
    in_spec, out_spec = {"gather_slot": (whole, slot), "gather_cols": (whole, cols), "scatter_slot": (slot, slot),
                         "scatter_cols": (cols, slot)}[mode]

    def body(dev_ref, src_ref, land_ref):
        land_ref[...] = src_ref[...].reshape(land_ref.shape)

    return pl.pallas_call(
        body, name=name, out_shape=jax.ShapeDtypeStruct(land, src.dtype),
        grid_spec=pltpu.PrefetchScalarGridSpec(num_scalar_prefetch=1, grid=(rows // tr,), in_specs=[in_spec],
                                               out_specs=out_spec),
        compiler_params=_cp("parallel"),
    )(dev, src)


def _exchange_start(src, mode, dev, *, name, land=None, after=None):
    if land is None:
        land = _place_own(src, "gather_slot" if mode == "gather_chip" else mode, dev, name=name + "_own")
    n_copies = _n_copies(mode)

    def body(*refs):
        src_ref, land_ref = refs[:2]
        send_sems, recv_sems, _, _, token = refs[-5:]
        for cp in _exchange_copies(src_ref, land_ref, send_sems, recv_sems, mode):
            cp.start()
        token[...] = jnp.zeros_like(token)

    hbm = pltpu.with_memory_space_constraint
    *handle, token = pl.pallas_call(
        body, name=name,
        out_shape=(pltpu.SemaphoreType.DMA((n_copies,)), pltpu.SemaphoreType.DMA((n_copies,)),
                   pltpu.HBM(src.shape, src.dtype), pltpu.HBM(land.shape, land.dtype),
                   jax.ShapeDtypeStruct((SUBLANES, LANES), F32)),
        in_specs=(_HBM, _HBM) + ((pl.BlockSpec(memory_space=pl.ANY),) if after is not None else ()),
        out_specs=(_SEM, _SEM, _HBM, _HBM, pl.BlockSpec(memory_space=pltpu.VMEM)),
        input_output_aliases={0: 2, 1: 3}, compiler_params=pltpu.CompilerParams(has_side_effects=_EFFECT),
    )(hbm(src, pltpu.HBM), hbm(land, pltpu.HBM), *(() if after is None else (after,)))
    return (tuple(handle), mode), token


def _exchange_wait(pending, after, *, name):
    (send_sems, recv_sems, src_thru, land_thru), mode = pending

    def body(src_ref, land_ref, send_sems, recv_sems, after_ref, src_dead, got_ref):
        for cp in _exchange_copies(src_ref, land_ref, send_sems, recv_sems, mode):
            cp.wait_send()
            cp.wait_recv()

    return pl.pallas_call(
        body, name=name, out_shape=(pltpu.HBM(src_thru.shape, src_thru.dtype), pltpu.HBM(land_thru.shape, land_thru.dtype)),
        in_specs=(_HBM, _HBM, _SEM, _SEM, pl.BlockSpec(memory_space=pl.ANY)), out_specs=(_HBM, _HBM),
        input_output_aliases={0: 0, 1: 1}, compiler_params=pltpu.CompilerParams(has_side_effects=_EFFECT),
    )(src_thru, land_thru, send_sems, recv_sems, after)[1]


def _after(x, token):
    return x + token[0, 0].astype(x.dtype)


def _touch(*arrays, name):
    def body(*refs):
        refs[-1][...] = jnp.zeros_like(refs[-1])

    return pl.pallas_call(
        body, name=name, out_shape=jax.ShapeDtypeStruct((SUBLANES, LANES), F32),
        in_specs=[pl.BlockSpec(memory_space=pl.ANY)] * len(arrays), out_specs=pl.BlockSpec(memory_space=pltpu.VMEM),
    )(*arrays)


def _pad_cols(a, mult):
    pad = -a.shape[1] % mult
    return jnp.pad(a, ((0, 0), (0, pad))) if pad else a


def _pack(arrs, cols):
    flat = jnp.concatenate([a.reshape(-1).astype(F32) for a in arrs])
    sizes = [int(a.size) for a in arrs]
    flat = jnp.pad(flat, (0, -flat.shape[0] % (SUBLANES * cols)))
    return flat.reshape(-1, cols), sizes


def _unpack(flat2d, sizes, shapes):
    flat = flat2d.reshape(-1)
    out, o = [], 0
    for n, s in zip(sizes, shapes):
        out.append(flat[o:o + n].reshape(s))
        o += n
    return out


PACK_COLS = SUBLANES * LANES


def kernel(x, norm_mix_w, w_in, conv_a_w, conv_a_b, dt_bias, a_log, d_a, norm_a_w, w_proj_a, s5_lam_re, s5_lam_im, s5_log_dt, s5_b_re, s5_b_im, s5_c_re, s5_c_im, s5_d, w_s5_glu, w_out, norm_ffn_w, w_up, conv_ffn_w, conv_ffn_b, w_down, norm_final_w, loss_target, m_norm_mix_w, m_w_in, m_conv_a_w, m_conv_a_b, m_dt_bias, m_a_log, m_d_a, m_norm_a_w, m_w_proj_a, m_s5_lam_re, m_s5_lam_im, m_s5_log_dt, m_s5_b_re, m_s5_b_im, m_s5_c_re, m_s5_c_im, m_s5_d, m_w_s5_glu, m_w_out, m_norm_ffn_w, m_w_up, m_conv_ffn_w, m_conv_ffn_b, m_w_down, m_norm_final_w, v_norm_mix_w, v_w_in, v_conv_a_w, v_conv_a_b, v_dt_bias, v_a_log, v_d_a, v_norm_a_w, v_w_proj_a, v_s5_lam_re, v_s5_lam_im, v_s5_log_dt, v_s5_b_re, v_s5_b_im, v_s5_c_re, v_s5_c_im, v_s5_d, v_w_s5_glu, v_w_out, v_norm_ffn_w, v_w_up, v_conv_ffn_w, v_conv_ffn_b, v_w_down, v_norm_final_w):
    weights = dict(norm_mix_w=norm_mix_w, w_in=w_in, conv_a_w=conv_a_w, conv_a_b=conv_a_b, dt_bias=dt_bias, a_log=a_log, d_a=d_a, norm_a_w=norm_a_w, w_proj_a=w_proj_a, s5_lam_re=s5_lam_re, s5_lam_im=s5_lam_im, s5_log_dt=s5_log_dt, s5_b_re=s5_b_re, s5_b_im=s5_b_im, s5_c_re=s5_c_re, s5_c_im=s5_c_im, s5_d=s5_d, w_s5_glu=w_s5_glu, w_out=w_out, norm_ffn_w=norm_ffn_w, w_up=w_up, conv_ffn_w=conv_ffn_w, conv_ffn_b=conv_ffn_b, w_down=w_down, norm_final_w=norm_final_w)
    moms = dict(norm_mix_w=m_norm_mix_w, w_in=m_w_in, conv_a_w=m_conv_a_w, conv_a_b=m_conv_a_b, dt_bias=m_dt_bias, a_log=m_a_log, d_a=m_d_a, norm_a_w=m_norm_a_w, w_proj_a=m_w_proj_a, s5_lam_re=m_s5_lam_re, s5_lam_im=m_s5_lam_im, s5_log_dt=m_s5_log_dt, s5_b_re=m_s5_b_re, s5_b_im=m_s5_b_im, s5_c_re=m_s5_c_re, s5_c_im=m_s5_c_im, s5_d=m_s5_d, w_s5_glu=m_w_s5_glu, w_out=m_w_out, norm_ffn_w=m_norm_ffn_w, w_up=m_w_up, conv_ffn_w=m_conv_ffn_w, conv_ffn_b=m_conv_ffn_b, w_down=m_w_down, norm_final_w=m_norm_final_w)
    vars_ = dict(norm_mix_w=v_norm_mix_w, w_in=v_w_in, conv_a_w=v_conv_a_w, conv_a_b=v_conv_a_b, dt_bias=v_dt_bias, a_log=v_a_log, d_a=v_d_a, norm_a_w=v_norm_a_w, w_proj_a=v_w_proj_a, s5_lam_re=v_s5_lam_re, s5_lam_im=v_s5_lam_im, s5_log_dt=v_s5_log_dt, s5_b_re=v_s5_b_re, s5_b_im=v_s5_b_im, s5_c_re=v_s5_c_re, s5_c_im=v_s5_c_im, s5_d=v_s5_d, w_s5_glu=v_w_s5_glu, w_out=v_w_out, norm_ffn_w=v_norm_ffn_w, w_up=v_w_up, conv_ffn_w=v_conv_ffn_w, conv_ffn_b=v_conv_ffn_b, w_down=v_w_down, norm_final_w=v_norm_final_w)
    names = list(weights)
    col_sharded = ("w_in", "w_s5_glu", "w_up")
    row_sharded = ("w_proj_a", "w_out", "w_down")
    conv_sharded = ("conv_a_w", "conv_ffn_w")
    replicated = [n for n in names if n not in col_sharded + row_sharded + conv_sharded]

    t, d = x.shape[1:]
    x2, tgt = x.reshape(t, d), loss_target.reshape(t, d)
    nh = dt_bias.shape[-1]
    d_inner = norm_a_w.shape[-1]
    conv_dim = conv_a_b.shape[-1]
    gn = (conv_dim - d_inner) // 2
    ng = gn // D_STATE
    nr = nh // ng
    rp = d_inner // ng
    d_s5 = s5_d.shape[-1]
    gs, ps = s5_lam_re.shape[1:]
    cs = d_s5 // gs
    n_oct = gs // 8
    assert (gs * ps) % SCAN_COLS == 0 and 8 * cs == LANES and gs % 8 == 0
    d_ff = w_down.shape[1] * N_DEV
    dev = _flat(*_position())
    dev1 = dev.reshape(1).astype(jnp.int32)

    ka, kf = conv_a_w.shape[1], conv_ffn_w.shape[1]
    taps = jnp.concatenate([conv_a_w[0].reshape(1, -1), conv_ffn_w[0].reshape(1, -1)], axis=1)
    taps, taps_done = _all_gather(taps, name="ag_conv_taps")
    taps = taps[:, 0]

    def by_cols(n):
        return n in ("w_s5_glu", "w_up") and weights[n].shape[2] % LANES == 0

    chip_stage, t1 = _exchange_start(_after(w_in[0], taps_done).astype(BF16), "gather_chip", dev1, name="ag_w_in")
    w1 = norm_mix_w.reshape(1, 1, d) + t1[0, 0]
    hn1, = _blocked_fwd(_rms_fn, [x2], [w1], [(d, BF16)], tb=256, name="rms1")
    others = ("w_proj_a", "w_s5_glu", "w_out", "w_up", "w_down")
    gather_mode = {n: "gather_cols" if by_cols(n) else "gather_slot" for n in others}
    shards = {n: _after(weights[n][0], t1).astype(BF16) for n in others}
    lands = {n: _place_own(shards[n], gather_mode[n], dev1, name="ag_" + n + "_own") for n in others}
    w_in_2d = [a.reshape(a.shape[-2:]) for a in (w_in, m_w_in, v_w_in)]
    ready = _touch(hn1, *lands.values(), *w_in_2d, name="ready_w_in")
    forward_stage, t2 = _exchange_start(ready, "forward", dev1, land=_exchange_wait(chip_stage, ready, name="agw_w_in"),
                                        name="fw_w_in")
    w_in_blocks = _exchange_wait(forward_stage, t2, name="fww_w_in")
    pending = {}
    for n in others:
        pending[n], _ = _exchange_start(shards[n], gather_mode[n], dev1, land=lands[n], after=w_in_blocks,
                                        name="ag_" + n)

    def gathered(n, after):
        g = _exchange_wait(pending[n], after, name="agw_" + n)
        if by_cols(n):
            return g
        if n in row_sharded:
            return g.reshape(-1, g.shape[2])
        return jnp.transpose(g, (1, 0, 2)).reshape(g.shape[1], -1)

    seg_sizes = dict(z=d_inner, xs=d_inner, bm=gn, cm=gn, dt=nh, u=d_s5, ga=d, gb=d)
    seg_names = tuple(seg_sizes)
    pieces, seg_at = _w_in_pieces(seg_sizes, ("z", "xs", "ga", "gb", "bm", "cm", "u", "dt"), w_in.shape[2])
    na = ka * conv_a_w.shape[2]
    cw_a = jnp.transpose(taps[:, :na].reshape(N_DEV, ka, -1), (1, 0, 2)).reshape(ka, conv_dim)
    cw_f = jnp.transpose(taps[:, na:].reshape(N_DEV, kf, -1), (1, 0, 2)).reshape(kf, 2 * d_ff)
    cb_a, cb_f = conv_a_b, conv_ffn_b
    a_cols = {"xs": slice(0, d_inner), "bm": slice(d_inner, d_inner + gn), "cm": slice(d_inner + gn, conv_dim)}

    w_in_p = _w_in_pack(w_in_blocks, pieces, seg_at, seg_sizes, name="w_in_pack")
    pre = {sn: _mm(hn1, w_in_p, b_win=seg_at[sn], name="in_" + sn) for sn in seg_names}
    act_a = {sn: _conv_fwd(_comb_silu, [pre[sn]], [cw_a[:, a_cols[sn]]], [cb_a[:, a_cols[sn]]], out_dtype=F32,
                           name="conv_a_" + sn) for sn in a_cols}
    dtr3 = jnp.transpose(pre["dt"][:, :nh].reshape(t, ng, nr), (1, 0, 2))
    dtb3, alog3, dsk3 = (p.reshape(ng, 1, nr) for p in (dt_bias, a_log, d_a))
    nw3 = norm_a_w.reshape(ng, 1, rp)
    yn, hsave = _ssd_fwd(act_a["xs"], act_a["bm"], act_a["cm"], pre["z"], dtr3, dtb3, alog3, dsk3, nw3, name="ssd")
    w_proj = gathered("w_proj_a", yn)
    y_a = _mm(yn, w_proj, name="proj_a")

    lam_re3, lam_im3 = s5_lam_re[0][:, None, :], s5_lam_im[0][:, None, :]
    logdt3 = s5_log_dt[0][:, None, None]
    bt_re, bt_im = jnp.transpose(s5_b_re[0], (0, 2, 1)), jnp.transpose(s5_b_im[0], (0, 2, 1))
    lb_re3, lb_im3, bb_re, bb_im = _s5_params(lam_re3, lam_im3, logdt3, bt_re, bt_im, name="s5_params")
    eye = jnp.eye(8, dtype=F32)

    def diag_b(bt):
        return (bt.reshape(n_oct, 8, cs, 1, ps) * eye[None, :, None, :, None]).reshape(n_oct, 8 * cs, 8 * ps)

    def undiag_b(blk):
        return (blk.reshape(n_oct, 8, cs, 8, ps) * eye[None, :, None, :, None]).sum(axis=3).reshape(gs, cs, ps)

    def diag_c(cm):
        ct = jnp.transpose(cm.reshape(n_oct, 8, cs, ps), (0, 1, 3, 2))
        return (ct[:, :, :, None, :] * eye[None, :, None, :, None]).reshape(n_oct, 8 * ps, 8 * cs)

    def undiag_c(blk):
        ct = (blk.reshape(n_oct, 8, ps, 8, cs) * eye[None, :, None, :, None]).sum(axis=3)
        return jnp.transpose(ct, (0, 1, 3, 2)).reshape(gs, cs, ps)

    b_blk_re, b_blk_im = diag_b(bb_re), diag_b(bb_im)
    c_blk_re, c_blk_imn = diag_c(s5_c_re[0]), diag_c(-s5_c_im[0])
    d3 = s5_d.reshape(n_oct, 1, LANES)
    lb_re, lb_im = lb_re3.reshape(1, gs * ps), lb_im3.reshape(1, gs * ps)
    u = pre["u"]
    bu_re, bu_im = _blocked_fwd(_s5_bu_fn, [u], [b_blk_re, b_blk_im], [(gs * ps, F32)] * 2, nj=n_oct, tb=512,
                                name="s5_bu")
    s_re, s_im = _s5_scan_fwd(bu_re, bu_im, lb_re, lb_im, name="s5_scan")
    yb, = _blocked_fwd(_s5_out_fn, [s_re, s_im, u], [c_blk_re, c_blk_imn, d3], [(d_s5, BF16)], nj=n_oct, tb=512,
                       name="s5_out")
    w_glu = gathered("w_s5_glu", yb)
    glu_v = _mm(yb, w_glu, b_win=(0, d), name="glu_v")
    glu_g = _mm(yb, w_glu, b_win=(d, d), name="glu_g")
    merged, = _blocked_fwd(_merge_fn, [glu_v, glu_g, pre["ga"], pre["gb"], y_a], [], [(d, BF16)], tb=256,
                           name="merge")
    w_o = gathered("w_out", merged)
    h1 = _mm(merged, w_o, acc=x2, name="out_proj")
    w2 = norm_ffn_w.reshape(1, 1, d)
    hn2, = _blocked_fwd(_rms_fn, [h1], [w2], [(d, BF16)], tb=256, name="rms2")
    w_u = gathered("w_up", hn2)
    up_g = _mm(hn2, w_u, b_win=(0, d_ff), name="up_g")
    up_v = _mm(hn2, w_u, b_win=(d_ff, d_ff), name="up_v")
    f_w = [cw_f[:, :d_ff], cw_f[:, d_ff:]]
    f_b = [cb_f[:, :d_ff], cb_f[:, d_ff:]]
    act = _conv_fwd(_comb_glu, [up_g, up_v], f_w, f_b, out_dtype=BF16, name="conv_ffn")
    w_dn = gathered("w_down", act)
    h2 = _mm(act, w_dn, acc=h1, name="down")
    loss_tile, dh2, dh2_b, g_final = _loss_head(h2, norm_final_w.reshape(1, d), tgt, name="loss_head")

    grads, scattering = {}, {}

    def scatter_start(n, g):
        if by_cols(n):
            src, mode = g, "scatter_cols"
        elif n in row_sharded:
            src, mode = g.reshape(N_DEV, -1, g.shape[1]), "scatter_slot"
        elif n == "w_in":
            src, mode = g, "scatter_slot"
        else:
            src, mode = jnp.transpose(g.reshape(g.shape[0], N_DEV, -1), (1, 0, 2)), "scatter_slot"
        scattering[n], token = _exchange_start(src, mode, dev1, name="rs_" + n)
        return token

    d_act = _mm(dh2_b, w_dn, tb=True, name="d_act")
    g_down = _mm(act, dh2_b, ta=True, out_dtype=BF16, name="g_w_down")
    tok = scatter_start("w_down", g_down)
    (dup_g, dwf_g, dbf_g), (dup_v, dwf_v, dbf_v) = _conv_bwd(
        _comb_glu, [up_g, up_v], f_w, [_after(f_b[0], tok), f_b[1]], d_act, dx_dtype=BF16, name="conv_ffn_bwd")
    dhn2 = _mm(dup_g, w_u, tb=True, b_win=(0, d_ff), name="d_hn2_g")
    dhn2 = _mm(dup_v, w_u, tb=True, b_win=(d_ff, d_ff), acc=dhn2, name="d_hn2_v")
    g_up = _mm(hn2, dup_g, ta=True, into=(lax.empty((d, 2 * d_ff), BF16), 0), name="g_w_up_g")
    g_up = _mm(hn2, dup_v, ta=True, into=(g_up, d_ff), name="g_w_up_v")
    tok = scatter_start("w_up", g_up)
    dh1, dh1_b, g_w2 = _blocked_bwd(_rms_fn, [h1], [_after(w2, tok)], [dhn2], [(F32, BF16)], adds={0: dh2}, tb=256,
                                    name="rms2_bwd")
    d_merged = _mm(dh1_b, w_o, tb=True, name="d_merged")
    g_out = _mm(merged, dh1_b, ta=True, out_dtype=BF16, name="g_w_out")
    tok = scatter_start("w_out", g_out)
    dglu_v, dglu_g, dga, dgb, dy_a = _blocked_bwd(
        _merge_fn, [glu_v, glu_g, pre["ga"], pre["gb"], y_a], [], [d_merged], [BF16] * 5, tb=128, name="merge_bwd")
    dyb = _mm(dglu_v, w_glu, tb=True, b_win=(0, d), name="d_yb_v")
    dyb = _mm(dglu_g, w_glu, tb=True, b_win=(d, d), acc=dyb, name="d_yb_g")
    g_glu = _mm(yb, dglu_v, ta=True, into=(lax.empty((d_s5, 2 * d), BF16), 0), name="g_w_glu_v")
    g_glu = _mm(yb, dglu_g, ta=True, into=(g_glu, d), name="g_w_glu_g")
    tok = tok + scatter_start("w_s5_glu", g_glu)
    ds_re, ds_im, du_skip, dc_blk_re, dc_blk_imn, dd3 = _blocked_bwd(
        _s5_out_fn, [s_re, s_im, u], [c_blk_re, c_blk_imn, _after(d3, tok)], [dyb], [F32, F32, F32], nj=n_oct, tb=512,
        name="s5_out_bwd")
    dbu_re, dbu_im, dlb_re, dlb_im = _s5_scan_bwd(s_re, s_im, ds_re, ds_im, lb_re, lb_im, name="s5_scan_bwd")
    du, db_blk_re, db_blk_im = _blocked_bwd(
        _s5_bu_fn, [u], [b_blk_re, b_blk_im], [dbu_re, dbu_im], [BF16], adds={0: du_skip}, nj=n_oct, tb=512,
        name="s5_bu_bwd")
    g_lre, g_lim, g_ldt, g_bt_re, g_bt_im = _s5_params(
        lam_re3, lam_im3, logdt3, bt_re, bt_im,
        cts=(dlb_re.reshape(gs, 1, ps), dlb_im.reshape(gs, 1, ps), undiag_b(db_blk_re), undiag_b(db_blk_im)),
        name="s5_params_bwd")
    grads["s5_lam_re"], grads["s5_lam_im"] = g_lre.reshape(s5_lam_re.shape), g_lim.reshape(s5_lam_im.shape)
    grads["s5_log_dt"] = g_ldt.reshape(s5_log_dt.shape)
    grads["s5_b_re"] = jnp.transpose(g_bt_re, (0, 2, 1)).reshape(s5_b_re.shape)
    grads["s5_b_im"] = jnp.transpose(g_bt_im, (0, 2, 1)).reshape(s5_b_im.shape)
    grads["s5_c_re"] = undiag_c(dc_blk_re).reshape(s5_c_re.shape)
    grads["s5_c_im"] = -undiag_c(dc_blk_imn).reshape(s5_c_im.shape)
    grads["s5_d"] = dd3.reshape(s5_d.shape)

    dyn = _mm(dy_a, w_proj, tb=True, name="d_yn")
    g_proj = _mm(yn, dy_a, ta=True, out_dtype=BF16, name="g_w_proj_a")
    tok = scatter_start("w_proj_a", g_proj)
    dxs, dbm, dcm, dz, ddtr3, g_dtb, g_alog, g_dsk, g_nw = _ssd_bwd(
        act_a["xs"], act_a["bm"], act_a["cm"], pre["z"], dtr3, hsave, dtb3, alog3, dsk3, _after(nw3, tok), dyn,
        name="ssd_bwd")
    grads["dt_bias"], grads["a_log"], grads["d_a"] = (g.reshape(1, nh) for g in (g_dtb, g_alog, g_dsk))
    grads["norm_a_w"] = g_nw.reshape(1, d_inner)
    dpre = {"z": dz, "u": du, "ga": dga, "gb": dgb}
    dcw, dcb = {}, {}
    for sn, dact in (("xs", dxs), ("bm", dbm), ("cm", dcm)):
        (dpre[sn], dcw[sn], dcb[sn]), = _conv_bwd(
            _comb_silu, [pre[sn]], [cw_a[:, a_cols[sn]]], [cb_a[:, a_cols[sn]]], dact, dx_dtype=BF16,
            name="conv_a_bwd_" + sn)
    dpre["dt"] = _pad_cols(jnp.transpose(ddtr3, (1, 0, 2)).reshape(t, nh), LANES).astype(BF16)
    g_in = _w_in_unpack({sn: _mm(hn1, dpre[sn], ta=True, name="g_w_in_" + sn) for sn in seg_names}, pieces,
                        w_in.shape[2], name="w_in_unpack")
    tok = scatter_start("w_in", g_in)
    dhn1 = _mm(_after(dpre["dt"], tok), w_in_p, tb=True, b_win=seg_at["dt"], name="d_hn1_dt")
    for sn in seg_names:
        if sn != "dt":
            dhn1 = _mm(dpre[sn], w_in_p, tb=True, b_win=seg_at[sn], acc=dhn1, name="d_hn1_" + sn)
    dx, g_w1 = _blocked_bwd(_rms_fn, [x2], [w1], [dhn1], [F32], adds={0: dh1}, tb=256, name="rms1_bwd")

    grads["norm_mix_w"], grads["norm_ffn_w"] = g_w1.reshape(1, d), g_w2.reshape(1, d)
    grads["norm_final_w"] = g_final.reshape(d)
    grads["conv_a_b"] = jnp.concatenate([dcb["xs"], dcb["bm"], dcb["cm"]], axis=1)
    grads["conv_ffn_b"] = jnp.concatenate([dbf_g, dbf_v], axis=1)
    g_cw_a = jnp.concatenate([dcw["xs"], dcw["bm"], dcw["cm"]], axis=1)
    g_cw_f = jnp.concatenate([dwf_g, dwf_v], axis=1)

    small = [grads[n] for n in replicated] + [g_cw_a, g_cw_f, loss_tile[:1, :1]]
    packed, sizes = _pack(small, PACK_COLS)
    summed = _sum_parts(_all_gather(packed, name="ag_small_grads")[0], name="sum_small_grads")
    *rep_sums, s_cw_a, s_cw_f, loss = _unpack(summed, sizes, [a.shape for a in small])
    for n, g in zip(replicated, rep_sums):
        grads[n] = g
    wa, wf = conv_a_w.shape[2], conv_ffn_w.shape[2]
    grads["conv_a_w"] = lax.dynamic_slice_in_dim(s_cw_a, dev * wa, wa, axis=1)[None]
    grads["conv_ffn_w"] = lax.dynamic_slice_in_dim(s_cw_f, dev * wf, wf, axis=1)[None]

    delta, new_m, new_v = {}, {}, {}
    done = dx
    for n in ("w_down", "w_up", "w_out", "w_s5_glu", "w_proj_a", "w_in"):
        shape = weights[n].shape
        two_d = lambda a: a.reshape(shape[-2], shape[-1])
        w2, m2, v2 = two_d(weights[n]), two_d(moms[n]), two_d(vars_[n])
        land = _exchange_wait(scattering[n], _touch(done, w2, m2, v2, name="ready_" + n), name="rsw_" + n)
        g = _sum_parts(land, name="rs_sum_" + n)
        grads[n] = g.reshape(shape)
        dl, nm, nv = _adamw(w2, g, m2, v2, name="adamw_" + n)
        delta[n], new_m[n], new_v[n] = dl.reshape(shape), nm.reshape(shape), nv.reshape(shape)
        done = dl
    for n in replicated + list(conv_sharded):
        shape = weights[n].shape
        two_d = lambda a: a.reshape(-1, shape[-1])
        dl, nm, nv = _adamw(two_d(weights[n]), two_d(grads[n]), two_d(moms[n]), two_d(vars_[n]), name="adamw_" + n)
        delta[n], new_m[n], new_v[n] = dl.reshape(shape), nm.reshape(shape), nv.reshape(shape)

    return (loss.reshape(()), dx.reshape(x.shape), *[grads[n] for n in names], *[delta[n] for n in names],
            *[new_m[n] for n in names], *[new_v[n] for n in names])
```

```python
import functools

import jax
import jax.numpy as jnp
from jax import lax
from jax.experimental import pallas as pl
from jax.experimental.pallas import tpu as pltpu

F32 = jnp.float32
BF16 = jnp.bfloat16
HIGHEST = lax.Precision.HIGHEST
MESH = pl.DeviceIdType.MESH

EPS = 1e-6
EIG_MAX = -1e-4
D_STATE = 128
CHUNK = 256
ADAM_LR = 0.001
ADAM_B1 = 0.9
ADAM_B2 = 0.999
ADAM_EPS = 1e-08
ADAM_WD = 0.01
ADAM_STEP = 10
N_DEV = 8
LANES = 128
SUBLANES = 8
VMEM_LIMIT = 56 * 1024 * 1024
MM_MAX_K = 4096


def _cp(*sem):
    return pltpu.CompilerParams(dimension_semantics=sem, vmem_limit_bytes=VMEM_LIMIT)


def _tile(dim, pref, unit=LANES):
    if dim <= unit:
        return dim
    t = (min(pref, dim) // unit) * unit
    while dim % t:
        t -= unit
    return t


_DIMS = {"nn": (((1,), (0,)), ((), ())), "nt": (((1,), (1,)), ((), ())), "tn": (((0,), (0,)), ((), ()))}


def _dot(a, b, kind):
    return lax.dot_general(a.astype(BF16), b.astype(BF16), _DIMS[kind], preferred_element_type=F32)


@functools.partial(jax.custom_vjp, nondiff_argnums=(2,))
def _bdot(a, b, kind):
    return _dot(a, b, kind)


def _bdot_fwd(a, b, kind):
    return _dot(a, b, kind), (a, b)


def _bdot_bwd(kind, res, g):
    a, b = res
    if kind == "nn":
        return _dot(g, b, "nt"), _dot(a, g, "tn")
    if kind == "nt":
        return _dot(g, b, "nn"), _dot(g, a, "tn")
    return _dot(b, g, "nt"), _dot(a, g, "nn")


_bdot.defvjp(_bdot_fwd, _bdot_bwd)


def _mm(a, b, *, ta=False, tb=False, acc=None, out_dtype=F32, name, b_win=None, into=None):
    assert not (ta and tb)
    m, k = (a.shape[1], a.shape[0]) if ta else a.shape
    b_off, b_size = b_win or (0, b.shape[1])
    n = b.shape[0] if tb else b_size
    assert (b_size if tb else b.shape[0]) == k, (a.shape, b.shape, ta, tb, b_win)
    o_off = into[1] if into else 0
    nk = -(-k // MM_MAX_K)
    while k % nk or (k // nk) % LANES or (tb and b_off % (k // nk)):
        nk += 1
    tk = k // nk
    tm, tn = _tile(m, 1024), _tile(n, 1024)
    while o_off % tn or (not tb and b_off % tn):
        tn = _tile(n, tn - LANES)
    kind = "tn" if ta else ("nt" if tb else "nn")
    a_spec = pl.BlockSpec((tk, tm), lambda i, j, l: (l, i)) if ta else pl.BlockSpec((tm, tk), lambda i, j, l: (i, l))
    if tb:
        b_spec = pl.BlockSpec((tn, tk), lambda i, j, l: (j, l + b_off // tk))
    else:
        b_spec = pl.BlockSpec((tk, tn), lambda i, j, l: (l, j + b_off // tn))
    c_spec = pl.BlockSpec((tm, tn), lambda i, j, l: (i, j))
    o_spec = pl.BlockSpec((tm, tn), lambda i, j, l: (i, j + o_off // tn))
    has_acc = acc is not None

    def body(*refs):
        a_ref, b_ref = refs[:2]
        c_ref = refs[2] if has_acc else None
        o_ref = refs[2 + has_acc + (into is not None)]
        if nk == 1:
            res = _dot(a_ref[...], b_ref[...], kind)
            if has_acc:
                res = res + c_ref[...].astype(F32)
            o_ref[...] = res.astype(o_ref.dtype)
            return
        acc_ref = refs[-1]
        l = pl.program_id(2)

        @pl.when(l == 0)
        def _():
            if has_acc:
                acc_ref[...] = c_ref[...].astype(F32)
            else:
                acc_ref[...] = jnp.zeros_like(acc_ref)

        acc_ref[...] += _dot(a_ref[...], b_ref[...], kind)

        @pl.when(l == nk - 1)
        def _():
            o_ref[...] = acc_ref[...].astype(o_ref.dtype)

    ins = [a, b] + ([acc] if has_acc else []) + ([into[0]] if into else [])
    in_specs = [a_spec, b_spec] + ([c_spec] if has_acc else []) + ([pl.BlockSpec(memory_space=pl.ANY)] if into else [])
    out_shape = jax.ShapeDtypeStruct(into[0].shape, into[0].dtype) if into else jax.ShapeDtypeStruct((m, n), out_dtype)
    return pl.pallas_call(
        body, name=name, grid=(m // tm, n // tn, nk), in_specs=in_specs, out_specs=o_spec, out_shape=out_shape,
        input_output_aliases={len(ins) - 1: 0} if into else {},
        scratch_shapes=[pltpu.VMEM((tm, tn), F32)] if nk > 1 else [],
        compiler_params=_cp("parallel", "parallel", "arbitrary"),
    )(*ins)


def _w_in_pieces(seg_sizes, seg_order, n_blk):
    layout, o = {}, 0
    for sn in seg_order:
        width = -(-seg_sizes[sn] // LANES) * LANES
        layout[sn] = (o, width)
        o += width
    pieces, start = [], 0
    for sn, sz in seg_sizes.items():
        lo = start
        while lo < start + sz:
            blk = lo // n_blk
            hi = min(start + sz, (blk + 1) * n_blk)
            pieces.append((blk, lo - blk * n_blk, sn, lo - start, layout[sn][0] + lo - start, hi - lo))
            lo = hi
        start += sz
    return pieces, layout


def _w_in_pack(gathered, pieces, layout, seg_sizes, after, *, name, tr=256):
    _, k, n_blk = gathered.shape
    n_pad = sum(w for _, w in layout.values())

    def body(g_ref, after_ref, o_ref):
        for sn, (off, width) in layout.items():
            if width != seg_sizes[sn]:
                o_ref[:, pl.ds(off + seg_sizes[sn], width - seg_sizes[sn])] = jnp.zeros(
                    (tr, width - seg_sizes[sn]), o_ref.dtype)
        for blk, src, _, _, dst, width in pieces:
            o_ref[:, pl.ds(dst, width)] = g_ref[blk, :, pl.ds(src, width)]

    return pl.pallas_call(
        body, name=name, grid=(k // tr,),
        in_specs=[pl.BlockSpec((N_DEV, tr, n_blk), lambda i: (0, i, 0)), pl.BlockSpec(memory_space=pl.ANY)],
        out_specs=pl.BlockSpec((tr, n_pad), lambda i: (i, 0)), out_shape=jax.ShapeDtypeStruct((k, n_pad), gathered.dtype),
        compiler_params=_cp("parallel"),
    )(gathered, after)


def _w_in_unpack(seg_grads, pieces, n_blk, *, name, tr=128):
    names = list(seg_grads)
    k = seg_grads[names[0]].shape[0]

    def body(*refs):
        o_ref = refs[-1]
        seg_ref = dict(zip(names, refs))
        for blk, dst, sn, src, _, width in pieces:
            o_ref[blk, :, pl.ds(dst, width)] = seg_ref[sn][:, pl.ds(src, width)].astype(o_ref.dtype)

    return pl.pallas_call(
        body, name=name, grid=(k // tr,),
        in_specs=[pl.BlockSpec((tr, seg_grads[sn].shape[1]), lambda i: (i, 0)) for sn in names],
        out_specs=pl.BlockSpec((N_DEV, tr, n_blk), lambda i: (0, i, 0)),
        out_shape=jax.ShapeDtypeStruct((N_DEV, k, n_blk), BF16), compiler_params=_cp("parallel"),
    )(*[seg_grads[sn] for sn in names])


def _row_spec(arr, tb, nj):
    return pl.BlockSpec((tb, arr.shape[1] // nj), lambda j, i: (i, j))


def _par_spec(arr):
    return pl.BlockSpec((1,) + arr.shape[1:], lambda j, i: (j, 0, 0))


def _blocked_fwd(fn, rows, params, outs, *, nj=1, tb, name):
    t = rows[0].shape[0]
    nr, npar = len(rows), len(params)

    def body(*refs):
        res = fn(*[r[...] for r in refs[:nr]], *[p[0] for p in refs[nr:nr + npar]])
        for o_ref, val in zip(refs[nr + npar:], res):
            o_ref[...] = val.astype(o_ref.dtype)

    return pl.pallas_call(
        body, name=name, grid=(nj, t // tb),
        in_specs=[_row_spec(a, tb, nj) for a in rows] + [_par_spec(p) for p in params],
        out_specs=[pl.BlockSpec((tb, c // nj), lambda j, i: (i, j)) for c, _ in outs],
        out_shape=[jax.ShapeDtypeStruct((t, c), dt) for c, dt in outs],
        compiler_params=_cp("parallel", "arbitrary"),
    )(*rows, *params)


def _blocked_bwd(fn, rows, params, cts, row_grad_dtypes, *, adds=None, nj=1, tb, name):
    t = rows[0].shape[0]
    nr, npar, nct = len(rows), len(params), len(cts)
    adds = adds or {}
    add_keys = sorted(adds)
    want, want_dtypes = [], []
    for k, dts in enumerate(row_grad_dtypes):
        for dt in (dts if isinstance(dts, tuple) else (dts,)):
            if dt is not None:
                want.append(k)
                want_dtypes.append(dt)

    def body(*refs):
        row_refs = refs[:nr]
        par_refs = refs[nr:nr + npar]
        ct_refs = refs[nr + npar:nr + npar + nct]
        add_refs = dict(zip(add_keys, refs[nr + npar + nct:nr + npar + nct + len(add_keys)]))
        out_refs = refs[nr + npar + nct + len(add_keys):]
        _, vjp = jax.vjp(fn, *[r[...] for r in row_refs], *[p[0] for p in par_refs])
        grads = vjp(tuple(c[...].astype(F32) for c in ct_refs))
        for o_ref, k in zip(out_refs, want):
            g = grads[k]
            if k in add_refs:
                g = g + add_refs[k][...].astype(F32)
            o_ref[...] = g.astype(o_ref.dtype)
        first = pl.program_id(1) == 0
        for o_ref, g in zip(out_refs[len(want):], grads[nr:]):
            @pl.when(first)
            def _(o_ref=o_ref):
                o_ref[...] = jnp.zeros_like(o_ref)
            o_ref[0] += g

    add_arrs = [adds[k] for k in add_keys]
    return pl.pallas_call(
        body, name=name, grid=(nj, t // tb),
        in_specs=[_row_spec(a, tb, nj) for a in rows] + [_par_spec(p) for p in params]
        + [_row_spec(c, tb, nj) for c in cts] + [_row_spec(a, tb, nj) for a in add_arrs],
        out_specs=[_row_spec(rows[k], tb, nj) for k in want] + [_par_spec(p) for p in params],
        out_shape=[jax.ShapeDtypeStruct(rows[k].shape, dt) for k, dt in zip(want, want_dtypes)]
        + [jax.ShapeDtypeStruct(p.shape, F32) for p in params],
        compiler_params=_cp("parallel", "arbitrary"),
    )(*rows, *params, *cts, *add_arrs)


def _rms_fn(x, w):
    return (x * lax.rsqrt(jnp.mean(x * x, axis=-1, keepdims=True) + EPS) * w,)


def _silu(x):
    return x * jax.nn.sigmoid(x)


def _merge_fn(glu_v, glu_g, g_a, g_b, y_a):
    y_b = glu_v * jax.nn.sigmoid(glu_g)
    return (jax.nn.sigmoid(g_a) * y_a + jax.nn.sigmoid(g_b) * y_b,)


def _s5_bu_fn(u, b_re, b_im):
    return _bdot(u, b_re, "nn"), _bdot(u, b_im, "nn")


def _s5_out_fn(s_re, s_im, u, c_re, c_im_neg, d):
    return (jax.nn.gelu(_bdot(s_re, c_re, "nn") + _bdot(s_im, c_im_neg, "nn") + d * u),)


HALO = SUBLANES


STRIP = 64


def _conv_strip(ext_ref, w_ref, b_ref, r0, cols):
    kw = w_ref.shape[0]
    xs = [ext_ref[pl.ds(r0 + HALO - kw + 1 + k, STRIP), cols] for k in range(kw)]
    c = b_ref[:, cols] + w_ref[0:1, cols] * xs[0]
    for k in range(1, kw):
        c = c + w_ref[k:k + 1, cols] * xs[k]
    return c, xs


def _fold(x):
    return x.reshape(STRIP // SUBLANES, SUBLANES, LANES).sum(axis=0)


def _conv_specs(xs, ws, bs, tb, cb, time_of):
    specs = []
    for x, w, b in zip(xs, ws, bs):
        specs += [
            pl.BlockSpec((HALO, cb), lambda j, i: (jnp.maximum(time_of(i) * (tb // HALO) - 1, 0), j)),
            pl.BlockSpec((tb, cb), lambda j, i: (time_of(i), j)),
            pl.BlockSpec((w.shape[0], cb), lambda j, i: (0, j)),
            pl.BlockSpec((1, cb), lambda j, i: (0, j)),
        ]
    return specs


def _conv_fwd(comb, xs, ws, bs, *, out_dtype, name, tb=512):
    t, c = xs[0].shape
    cb = _tile(c, 512)
    ns = len(xs)

    def body(*refs):
        i = pl.program_id(1)
        o_ref = refs[4 * ns]
        exts = refs[4 * ns + 1:]
        for s in range(ns):
            xp_ref, xm_ref = refs[4 * s:4 * s + 2]
            exts[s][pl.ds(0, HALO), :] = jnp.where(i == 0, 0.0, xp_ref[...])
            exts[s][pl.ds(HALO, tb), :] = xm_ref[...]
        for c0 in range(0, cb, LANES):
            cols = pl.ds(c0, LANES)
            for r0 in range(0, tb, STRIP):
                cs = [_conv_strip(exts[s], refs[4 * s + 2], refs[4 * s + 3], r0, cols)[0] for s in range(ns)]
                o_ref[pl.ds(r0, STRIP), cols] = comb(*cs).astype(out_dtype)

    flat = [a for x, w, b in zip(xs, ws, bs) for a in (x, x, w, b)]
    return pl.pallas_call(
        body, name=name, grid=(c // cb, t // tb),
        in_specs=_conv_specs(xs, ws, bs, tb, cb, lambda i: i),
        out_specs=pl.BlockSpec((tb, cb), lambda j, i: (i, j)),
        out_shape=jax.ShapeDtypeStruct((t, c), out_dtype),
        scratch_shapes=[pltpu.VMEM((HALO + tb, cb), F32) for _ in range(ns)],
        compiler_params=_cp("parallel", "arbitrary"),
    )(*flat)


def _conv_bwd(comb, xs, ws, bs, dy, *, dx_dtype, name, tb=512):
    t, c = xs[0].shape
    cb = _tile(c, 512)
    ns = len(xs)
    nt = t // tb
    kw = ws[0].shape[0]

    def body(*refs):
        step = pl.program_id(1)
        dy_ref = refs[4 * ns]
        out_refs = refs[4 * ns + 1:4 * ns + 1 + 3 * ns]
        scratch = refs[4 * ns + 1 + 3 * ns:]
        exts, dcs, carries = scratch[:ns], scratch[ns:2 * ns], scratch[2 * ns:]

        @pl.when(step == 0)
        def _():
            for s in range(ns):
                carries[s][...] = jnp.zeros_like(carries[s])
                out_refs[3 * s + 1][...] = jnp.zeros_like(out_refs[3 * s + 1])
                out_refs[3 * s + 2][...] = jnp.zeros_like(out_refs[3 * s + 2])

        for s in range(ns):
            xp_ref, xm_ref = refs[4 * s:4 * s + 2]
            exts[s][pl.ds(0, HALO), :] = jnp.where(step == nt - 1, 0.0, xp_ref[...])
            exts[s][pl.ds(HALO, tb), :] = xm_ref[...]
            dcs[s][pl.ds(tb, HALO), :] = carries[s][...]
        for c0 in range(0, cb, LANES):
            cols = pl.ds(c0, LANES)
            acc_w = [[jnp.zeros((SUBLANES, LANES), F32) for _ in range(kw)] for _ in range(ns)]
            acc_b = [jnp.zeros((SUBLANES, LANES), F32) for _ in range(ns)]
            for r0 in range(0, tb, STRIP):
                strips = [_conv_strip(exts[s], refs[4 * s + 2], refs[4 * s + 3], r0, cols) for s in range(ns)]
                _, vjp = jax.vjp(comb, *[cs for cs, _ in strips])
                grads = vjp(dy_ref[pl.ds(r0, STRIP), cols].astype(F32))
                for s in range(ns):
                    dcs[s][pl.ds(r0, STRIP), cols] = grads[s]
                    acc_b[s] = acc_b[s] + _fold(grads[s])
                    for k in range(kw):
                        acc_w[s][k] = acc_w[s][k] + _fold(grads[s] * strips[s][1][k])
            for s in range(ns):
                dw_ref, db_ref = out_refs[3 * s + 1], out_refs[3 * s + 2]
                db_ref[:, cols] += jnp.sum(acc_b[s], axis=0, keepdims=True)
                for k in range(kw):
                    dw_ref[k:k + 1, cols] += jnp.sum(acc_w[s][k], axis=0, keepdims=True)
        for s in range(ns):
            w_ref, dx_ref = refs[4 * s + 2], out_refs[3 * s]
            for c0 in range(0, cb, LANES):
                cols = pl.ds(c0, LANES)
                for r0 in range(0, tb, STRIP):
                    dx = w_ref[kw - 1:kw, cols] * dcs[s][pl.ds(r0, STRIP), cols]
                    for k in range(kw - 1):
                        dx = dx + w_ref[k:k + 1, cols] * dcs[s][pl.ds(r0 + kw - 1 - k, STRIP), cols]
                    dx_ref[pl.ds(r0, STRIP), cols] = dx.astype(dx_dtype)
            carries[s][...] = dcs[s][pl.ds(0, HALO), :]

    flat = [a for x, w, b in zip(xs, ws, bs) for a in (x, x, w, b)]
    rev = lambda i: nt - 1 - i
    out_specs, out_shape = [], []
    for x, w, b in zip(xs, ws, bs):
        out_specs += [pl.BlockSpec((tb, cb), lambda j, i: (rev(i), j)),
                      pl.BlockSpec((w.shape[0], cb), lambda j, i: (0, j)),
                      pl.BlockSpec((1, cb), lambda j, i: (0, j))]
        out_shape += [jax.ShapeDtypeStruct((t, c), dx_dtype), jax.ShapeDtypeStruct(w.shape, F32),
                      jax.ShapeDtypeStruct(b.shape, F32)]
    res = pl.pallas_call(
        body, name=name, grid=(c // cb, nt),
        in_specs=_conv_specs(xs, ws, bs, tb, cb, rev) + [pl.BlockSpec((tb, cb), lambda j, i: (rev(i), j))],
        out_specs=out_specs, out_shape=out_shape,
        scratch_shapes=[pltpu.VMEM((HALO + tb, cb), F32) for _ in range(2 * ns)]
        + [pltpu.VMEM((HALO, cb), F32) for _ in range(ns)],
        compiler_params=_cp("parallel", "arbitrary"),
    )(*flat, dy)
    return [tuple(res[3 * s:3 * s + 3]) for s in range(ns)]


def _comb_silu(c):
    return _silu(c)


def _comb_glu(cg, cv):
    return _silu(cg) * cv


def _ssd_fn(nheads, hdim):
    def fn(x, bm, cm, z, dtr, hin, dtb, alog, dsk, nw):
        q = x.shape[0]
        dt = jax.nn.softplus(dtr + dtb)
        da = dt * (-jnp.exp(alog))
        li = lax.broadcasted_iota(jnp.int32, (q, q), 0)
        si = lax.broadcasted_iota(jnp.int32, (q, q), 1)
        causal = li >= si
        tri = causal.astype(F32)
        acs = jnp.dot(tri, da, precision=HIGHEST, preferred_element_type=F32)
        acs_row = lax.dot_general(da, tri, (((0,), (1,)), ((), ())), precision=HIGHEST,
                                  preferred_element_type=F32)
        cb = _bdot(cm, bm, "nt")
        ch = _bdot(cm, hin, "nn")
        ys, hs = [], []
        for r in range(nheads):
            cols = slice(r * hdim, (r + 1) * hdim)
            xr = x[:, cols]
            a_col = acs[:, r:r + 1]
            decay = jnp.exp(jnp.where(causal, a_col - acs_row[r:r + 1, :], -1e30))
            xd = xr * dt[:, r:r + 1]
            y_diag = _bdot(cb * decay, xd, "nn")
            y_off = ch[:, cols] * jnp.exp(a_col)
            last = acs[q - 1:q, r:r + 1]
            st = _bdot(bm * jnp.exp(last - a_col), xd, "tn")
            hs.append(jnp.exp(last) * hin[:, cols] + st)
            ys.append(y_diag + y_off + dsk[:, r:r + 1] * xr)
        y = jnp.concatenate(ys, axis=1) * _silu(z)
        yn = y * lax.rsqrt(jnp.mean(y * y, axis=-1, keepdims=True) + EPS) * nw
        return yn, jnp.concatenate(hs, axis=1)
    return fn


def _ssd_specs(rp, nr, time_of):
    row = lambda w: pl.BlockSpec((CHUNK, w), lambda g, c: (time_of(c), g))
    par = lambda w: pl.BlockSpec((1, 1, w), lambda g, c: (g, 0, 0))
    return dict(
        x=row(rp), bc=row(D_STATE), dtr=pl.BlockSpec((1, CHUNK, nr), lambda g, c: (g, time_of(c), 0)),
        h=pl.BlockSpec((1, 1, D_STATE, rp), lambda g, c: (g, time_of(c), 0, 0)), pr=par(nr), pw=par(rp))


def _ssd_fwd(xs, bm, cm, z, dtr, dtb, alog, dsk, nw, *, name):
    t = xs.shape[0]
    g, _, nr = dtr.shape
    rp = xs.shape[1] // g
    nc = t // CHUNK
    fn = _ssd_fn(nr, rp // nr)
    sp = _ssd_specs(rp, nr, lambda c: c)

    def body(x_ref, b_ref, c_ref, z_ref, dtr_ref, dtb_ref, al_ref, dsk_ref, nw_ref, yn_ref, hs_ref, h_ref):
        @pl.when(pl.program_id(1) == 0)
        def _():
            h_ref[...] = jnp.zeros_like(h_ref)
        hin = h_ref[...]
        hs_ref[0, 0] = hin
        yn, hout = fn(x_ref[...], b_ref[...], c_ref[...], z_ref[...], dtr_ref[0], hin,
                      dtb_ref[0], al_ref[0], dsk_ref[0], nw_ref[0])
        yn_ref[...] = yn.astype(yn_ref.dtype)
        h_ref[...] = hout

    return pl.pallas_call(
        body, name=name, grid=(g, nc),
        in_specs=[sp["x"], sp["bc"], sp["bc"], sp["x"], sp["dtr"], sp["pr"], sp["pr"], sp["pr"], sp["pw"]],
        out_specs=[sp["x"], sp["h"]],
        out_shape=[jax.ShapeDtypeStruct(xs.shape, BF16), jax.ShapeDtypeStruct((g, nc, D_STATE, rp), F32)],
        scratch_shapes=[pltpu.VMEM((D_STATE, rp), F32)],
        compiler_params=_cp("parallel", "arbitrary"),
    )(xs, bm, cm, z, dtr, dtb, alog, dsk, nw)


def _ssd_bwd(xs, bm, cm, z, dtr, hsave, dtb, alog, dsk, nw, dyn, *, name):
    t = xs.shape[0]
    g, _, nr = dtr.shape
    rp = xs.shape[1] // g
    nc = t // CHUNK
    fn = _ssd_fn(nr, rp // nr)
    sp = _ssd_specs(rp, nr, lambda c: nc - 1 - c)

    def body(x_ref, b_ref, c_ref, z_ref, dtr_ref, hs_ref, dtb_ref, al_ref, dsk_ref, nw_ref, dyn_ref,
             dx_ref, db_ref, dc_ref, dz_ref, ddtr_ref, ddtb_ref, dal_ref, ddsk_ref, dnw_ref, dh_ref):
        first = pl.program_id(1) == 0

        @pl.when(first)
        def _():
            dh_ref[...] = jnp.zeros_like(dh_ref)
            for r in (ddtb_ref, dal_ref, ddsk_ref, dnw_ref):
                r[...] = jnp.zeros_like(r)

        _, vjp = jax.vjp(fn, x_ref[...], b_ref[...], c_ref[...], z_ref[...], dtr_ref[0], hs_ref[0, 0],
                         dtb_ref[0], al_ref[0], dsk_ref[0], nw_ref[0])
        dx, db, dc, dz, ddtr, dhin, ddtb, dal, ddsk, dnw = vjp((dyn_ref[...].astype(F32), dh_ref[...]))
        dx_ref[...] = dx
        db_ref[...] = db
        dc_ref[...] = dc
        dz_ref[...] = dz.astype(dz_ref.dtype)
        ddtr_ref[0] = ddtr
        dh_ref[...] = dhin
        ddtb_ref[0] += ddtb
        dal_ref[0] += dal
        ddsk_ref[0] += ddsk
        dnw_ref[0] += dnw

    sd = jax.ShapeDtypeStruct
    return pl.pallas_call(
        body, name=name, grid=(g, nc),
        in_specs=[sp["x"], sp["bc"], sp["bc"], sp["x"], sp["dtr"], sp["h"], sp["pr"], sp["pr"], sp["pr"], sp["pw"],
                  sp["x"]],
        out_specs=[sp["x"], sp["bc"], sp["bc"], sp["x"], sp["dtr"], sp["pr"], sp["pr"], sp["pr"], sp["pw"]],
        out_shape=[sd(xs.shape, F32), sd(bm.shape, F32), sd(cm.shape, F32), sd(z.shape, BF16), sd(dtr.shape, F32),
                   sd(dtb.shape, F32), sd(alog.shape, F32), sd(dsk.shape, F32), sd(nw.shape, F32)],
        scratch_shapes=[pltpu.VMEM((D_STATE, rp), F32)],
        compiler_params=_cp("parallel", "arbitrary"),
    )(xs, bm, cm, z, dtr, hsave, dtb, alog, dsk, nw, dyn)


def _s5_param_fn(lam_re, lam_im, log_dt, bt_re, bt_im):
    lr = jnp.minimum(lam_re, EIG_MAX)
    dt = jnp.exp(log_dt)
    mag = jnp.exp(lr * dt)
    lb_re = mag * jnp.cos(lam_im * dt)
    lb_im = mag * jnp.sin(lam_im * dt)
    n_re = lb_re - 1.0
    den = lr * lr + lam_im * lam_im
    k_re = (n_re * lr + lb_im * lam_im) / den
    k_im = (lb_im * lr - n_re * lam_im) / den
    return lb_re, lb_im, k_re * bt_re - k_im * bt_im, k_re * bt_im + k_im * bt_re


def _s5_params(lam_re, lam_im, log_dt, bt_re, bt_im, cts=None, *, name):
    args = (lam_re, lam_im, log_dt, bt_re, bt_im)
    n = len(args)

    def body(*refs):
        vals = [r[...] for r in refs[:n]]
        if cts is None:
            res = _s5_param_fn(*vals)
        else:
            _, vjp = jax.vjp(_s5_param_fn, *vals)
            res = vjp(tuple(r[...] for r in refs[n:n + 4]))
        for o_ref, v in zip(refs[-len(res):], res):
            o_ref[...] = v

    if cts is None:
        out = [lam_re, lam_im, bt_re, bt_im]
        ins = args
    else:
        out = list(args)
        ins = args + tuple(cts)
    return pl.pallas_call(
        body, name=name, out_shape=[jax.ShapeDtypeStruct(a.shape, F32) for a in out],
        compiler_params=pltpu.CompilerParams(vmem_limit_bytes=VMEM_LIMIT),
    )(*ins)


SCAN_COLS = 512


def _cmul(xr, xi, yr, yi):
    return xr * yr - xi * yi, xr * yi + xi * yr


def _scan_consts(a_re, a_im, cols, reverse):
    shape = (SUBLANES, cols)
    row = lax.broadcasted_iota(jnp.int32, shape, 0)
    dist = (SUBLANES - 1 - row) if reverse else row
    mr, mi = jnp.broadcast_to(a_re, shape), jnp.broadcast_to(a_im, shape)
    pr, pi = mr, mi
    mults = []
    for d in (1, 2, 4):
        mults.append((mr, mi))
        qr, qi = _cmul(pr, pi, mr, mi)
        has_bit = (dist & d) != 0
        pr, pi = jnp.where(has_bit, qr, pr), jnp.where(has_bit, qi, pi)
        mr, mi = _cmul(mr, mi, mr, mi)
    return mults, (pr, pi), dist


def _scan_group(xr, xi, consts, cr, ci, reverse):
    mults, (pr, pi), dist = consts
    for d, (mr, mi) in zip((1, 2, 4), mults):
        shift = (SUBLANES - d) if reverse else d
        sr = jnp.where(dist >= d, pltpu.roll(xr, shift, 0), 0.0)
        si = jnp.where(dist >= d, pltpu.roll(xi, shift, 0), 0.0)
        tr, ti = _cmul(mr, mi, sr, si)
        xr, xi = xr + tr, xi + ti
    last = slice(0, 1) if reverse else slice(SUBLANES - 1, SUBLANES)
    nr, ni = _cmul(pr[last], pi[last], cr, ci)
    tr, ti = _cmul(pr, pi, jnp.broadcast_to(cr, xr.shape), jnp.broadcast_to(ci, xr.shape))
    return xr + tr, xi + ti, xr[last] + nr, xi[last] + ni


def _scan_specs(tb, time_of):
    row = pl.BlockSpec((tb, SCAN_COLS), lambda j, i: (time_of(i), j))
    par = pl.BlockSpec((1, SCAN_COLS), lambda j, i: (0, j))
    return row, par


def _s5_scan_fwd(bu_re, bu_im, lb_re, lb_im, *, name, tb=512):
    t, c = bu_re.shape
    nj = c // SCAN_COLS
    row, par = _scan_specs(tb, lambda i: i)

    def body(bre_ref, bim_ref, lre_ref, lim_ref, sre_ref, sim_ref, cre_ref, cim_ref):
        @pl.when(pl.program_id(1) == 0)
        def _():
            cre_ref[...] = jnp.zeros_like(cre_ref)
            cim_ref[...] = jnp.zeros_like(cim_ref)
        consts = _scan_consts(lre_ref[...], lim_ref[...], SCAN_COLS, False)

        def group(k, carry):
            rows = pl.ds(pl.multiple_of(k * SUBLANES, SUBLANES), SUBLANES)
            sr, si, cr, ci = _scan_group(bre_ref[rows, :], bim_ref[rows, :], consts, *carry, False)
            sre_ref[rows, :] = sr
            sim_ref[rows, :] = si
            return cr, ci

        sr, si = lax.fori_loop(0, tb // SUBLANES, group, (cre_ref[...], cim_ref[...]), unroll=4)
        cre_ref[...] = sr
        cim_ref[...] = si

    return pl.pallas_call(
        body, name=name, grid=(nj, t // tb), in_specs=[row, row, par, par], out_specs=[row, row],
        out_shape=[jax.ShapeDtypeStruct((t, c), F32)] * 2,
        scratch_shapes=[pltpu.VMEM((1, SCAN_COLS), F32)] * 2,
        compiler_params=_cp("parallel", "arbitrary"),
    )(bu_re, bu_im, lb_re, lb_im)


def _s5_scan_bwd(s_re, s_im, ds_re, ds_im, lb_re, lb_im, *, name, tb=512):
    t, c = s_re.shape
    nj = c // SCAN_COLS
    nt = t // tb
    rev = lambda i: nt - 1 - i
    row, par = _scan_specs(tb, rev)
    prev = pl.BlockSpec((HALO, SCAN_COLS), lambda j, i: (jnp.maximum(rev(i) * (tb // HALO) - 1, 0), j))

    def body(sre_ref, sim_ref, pre_ref, pim_ref, dre_ref, dim_ref, lre_ref, lim_ref,
             gre_ref, gim_ref, dlre_ref, dlim_ref, cre_ref, cim_ref, ext_re, ext_im):
        step_id = pl.program_id(1)

        @pl.when(step_id == 0)
        def _():
            cre_ref[...] = jnp.zeros_like(cre_ref)
            cim_ref[...] = jnp.zeros_like(cim_ref)
            dlre_ref[...] = jnp.zeros_like(dlre_ref)
            dlim_ref[...] = jnp.zeros_like(dlim_ref)
        consts = _scan_consts(lre_ref[...], -lim_ref[...], SCAN_COLS, True)
        ngroups = tb // SUBLANES

        def group(k, carry):
            rows = pl.ds(pl.multiple_of((ngroups - 1 - k) * SUBLANES, SUBLANES), SUBLANES)
            gr, gi, cr, ci = _scan_group(dre_ref[rows, :], dim_ref[rows, :], consts, *carry, True)
            gre_ref[rows, :] = gr
            gim_ref[rows, :] = gi
            return cr, ci

        gr, gi = lax.fori_loop(0, ngroups, group, (cre_ref[...], cim_ref[...]), unroll=4)
        cre_ref[...] = gr
        cim_ref[...] = gi
        has_past = step_id != nt - 1
        ext_re[pl.ds(0, HALO), :] = jnp.where(has_past, pre_ref[...], 0.0)
        ext_im[pl.ds(0, HALO), :] = jnp.where(has_past, pim_ref[...], 0.0)
        ext_re[pl.ds(HALO, tb), :] = sre_ref[...]
        ext_im[pl.ds(HALO, tb), :] = sim_ref[...]
        pr, pi = ext_re[pl.ds(HALO - 1, tb), :], ext_im[pl.ds(HALO - 1, tb), :]
        g_re, g_im = gre_ref[...], gim_ref[...]
        dlre_ref[...] += jnp.sum(pr * g_re + pi * g_im, axis=0, keepdims=True)
        dlim_ref[...] += jnp.sum(pr * g_im - pi * g_re, axis=0, keepdims=True)

    return pl.pallas_call(
        body, name=name, grid=(nj, nt),
        in_specs=[row, row, prev, prev, row, row, par, par], out_specs=[row, row, par, par],
        out_shape=[jax.ShapeDtypeStruct((t, c), F32)] * 2 + [jax.ShapeDtypeStruct((1, c), F32)] * 2,
        scratch_shapes=[pltpu.VMEM((1, SCAN_COLS), F32)] * 2 + [pltpu.VMEM((HALO + tb, SCAN_COLS), F32)] * 2,
        compiler_params=_cp("parallel", "arbitrary"),
    )(s_re, s_im, s_re, s_im, ds_re, ds_im, lb_re, lb_im)


def _loss_fn(h, w, tgt):
    err = _rms_fn(h, w)[0] - tgt
    return 0.5 * jnp.sum(jnp.mean(err * err, axis=-1, keepdims=True), axis=0, keepdims=True)


def _loss_head(h, w, tgt, *, name, tb=256):
    t, d = h.shape

    def body(h_ref, w_ref, t_ref, loss_ref, dh_ref, dhb_ref, dw_ref):
        @pl.when(pl.program_id(0) == 0)
        def _():
            loss_ref[...] = jnp.zeros_like(loss_ref)
            dw_ref[...] = jnp.zeros_like(dw_ref)
        part, vjp = jax.vjp(_loss_fn, h_ref[...], w_ref[...], t_ref[...])
        dh, dw, _ = vjp(jnp.ones((1, 1), F32))
        loss_ref[...] += jnp.broadcast_to(part, loss_ref.shape)
        dh_ref[...] = dh
        dhb_ref[...] = dh.astype(BF16)
        dw_ref[...] += dw

    row = pl.BlockSpec((tb, d), lambda i: (i, 0))
    par = pl.BlockSpec((1, d), lambda i: (0, 0))
    return pl.pallas_call(
        body, name=name, grid=(t // tb,), in_specs=[row, par, row],
        out_specs=[pl.BlockSpec((SUBLANES, LANES), lambda i: (0, 0)), row, row, par],
        out_shape=[jax.ShapeDtypeStruct((SUBLANES, LANES), F32), jax.ShapeDtypeStruct((t, d), F32),
                   jax.ShapeDtypeStruct((t, d), BF16), jax.ShapeDtypeStruct((1, d), F32)],
        compiler_params=_cp("arbitrary"),
    )(h, w, tgt)


def _adamw(w, g, m, v, *, name):
    r, c = w.shape
    tr = _tile(r, 256, SUBLANES)

    def body(w_ref, g_ref, m_ref, v_ref, d_ref, nm_ref, nv_ref):
        g = g_ref[...]
        nm = ADAM_B1 * m_ref[...] + (1.0 - ADAM_B1) * g
        nv = ADAM_B2 * v_ref[...] + (1.0 - ADAM_B2) * (g * g)
        m_hat = nm / (1.0 - ADAM_B1 ** ADAM_STEP)
        v_hat = nv / (1.0 - ADAM_B2 ** ADAM_STEP)
        d_ref[...] = -ADAM_LR * (m_hat / (jnp.sqrt(v_hat) + ADAM_EPS) + ADAM_WD * w_ref[...])
        nm_ref[...] = nm
        nv_ref[...] = nv

    spec = pl.BlockSpec((tr, c), lambda i: (i, 0))
    return pl.pallas_call(
        body, name=name, grid=(r // tr,), in_specs=[spec] * 4, out_specs=[spec] * 3,
        out_shape=[jax.ShapeDtypeStruct((r, c), F32)] * 3, compiler_params=_cp("parallel"),
    )(w, g, m, v)


def _sum_parts(parts, *, name):
    _, r, c = parts.shape
    tr = _tile(r, 128, SUBLANES)

    def body(p_ref, o_ref):
        acc = p_ref[0].astype(F32)
        for k in range(1, N_DEV):
            acc = acc + p_ref[k].astype(F32)
        o_ref[...] = acc

    return pl.pallas_call(
        body, name=name, grid=(r // tr,), in_specs=[pl.BlockSpec((N_DEV, tr, c), lambda i: (0, i, 0))],
        out_specs=pl.BlockSpec((tr, c), lambda i: (i, 0)), out_shape=jax.ShapeDtypeStruct((r, c), F32),
        compiler_params=_cp("parallel"),
    )(parts)


def _position():
    return lax.axis_index("x"), lax.axis_index("y"), lax.axis_index("c")


def _flat(px, py, pc):
    return 4 * px + 2 * py + pc


def _all_gather(shard, *, name):
    def body(x_ref, out_ref, token, send_sems, recv_sems, local_sem):
        token[...] = jnp.zeros_like(token)
        x, y, c = _position()
        me, sibling = (x, y, c), (x, y, 1 - c)
        chips = [(1 - x, y), (x, 1 - y), (1 - x, 1 - y)]

        def copy(k, block, to, src=None):
            slot = out_ref.at[_flat(*block)]
            return pltpu.make_async_remote_copy(
                src_ref=slot if src is None else src, dst_ref=slot, send_sem=send_sems.at[k],
                recv_sem=recv_sems.at[k], device_id=to, device_id_type=MESH)

        mine = pltpu.make_async_copy(x_ref, out_ref.at[_flat(*me)], local_sem)
        mine.start()
        first = [copy(0, me, sibling, src=x_ref)]
        first += [copy(1 + j, me, (*chip, c), src=x_ref) for j, chip in enumerate(chips)]
        for cp in first:
            cp.start()
        passed = [copy(4 + j, (*chip, c), sibling) for j, chip in enumerate(chips)]
        for j, chip in enumerate(chips):
            copy(1 + j, (*chip, c), me).wait_recv()
            passed[j].start()
        copy(0, sibling, me).wait_recv()
        for j, chip in enumerate(chips):
            copy(4 + j, (*chip, 1 - c), me).wait_recv()
        for cp in first + passed:
            cp.wait_send()
        mine.wait()

    return pl.pallas_call(
        body, name=name,
        out_shape=(jax.ShapeDtypeStruct((N_DEV,) + shard.shape, shard.dtype),
                   jax.ShapeDtypeStruct((SUBLANES, LANES), F32)),
        in_specs=[pl.BlockSpec(memory_space=pl.ANY)],
        out_specs=(pl.BlockSpec(memory_space=pl.ANY), pl.BlockSpec(memory_space=pltpu.VMEM)),
        scratch_shapes=[pltpu.SemaphoreType.DMA((7,)), pltpu.SemaphoreType.DMA((7,)), pltpu.SemaphoreType.DMA(())],
    )(shard)


_HBM = pl.BlockSpec(memory_space=pltpu.HBM)
_SEM = pl.BlockSpec(memory_space=pltpu.SEMAPHORE)
_EFFECT = pltpu.SideEffectType.DATAFLOW_SIDE_EFFECTING


def _copy_ends(src_ref, land_ref, mode, me, to):
    if mode == "gather_slot":
        return src_ref, land_ref.at[me]
    if mode == "gather_cols":
        w = src_ref.shape[1]
        return src_ref, land_ref.at[:, pl.ds(pl.multiple_of(me * w, LANES), w)]
    if mode == "scatter_slot":
        return src_ref.at[to], land_ref.at[me]
    w = land_ref.shape[2]
    return src_ref.at[:, pl.ds(pl.multiple_of(to * w, LANES), w)], land_ref.at[me]


BF16_ROWS = 16


def _land_shape(src, mode):
    if mode == "gather_slot":
        return (N_DEV,) + src.shape
    if mode == "gather_cols":
        return (src.shape[0], N_DEV * src.shape[1])
    if mode == "scatter_slot":
        return src.shape
    return (N_DEV, src.shape[0], src.shape[1] // N_DEV)


OTHER_CHIPS = (2, 4, 6)


def _n_copies(mode):
    return {"gather_chip": 1 + len(OTHER_CHIPS), "forward": len(OTHER_CHIPS)}.get(mode, N_DEV - 1)


def _exchange_copies(src_ref, land_ref, send_sems, recv_sems, mode):
    x, y, c = _position()
    me = _flat(x, y, c)
    peer_of = lambda k: (x ^ ((k >> 2) & 1), y ^ ((k >> 1) & 1), c ^ (k & 1))
    if mode == "forward":
        slots = [land_ref.at[_flat(*peer_of(k))] for k in OTHER_CHIPS]
        plan = [(slot, slot, peer_of(1)) for slot in slots]
    elif mode == "gather_chip":
        plan = [(*_copy_ends(src_ref, land_ref, "gather_slot", me, _flat(*peer_of(k))), peer_of(k))
                for k in (1,) + OTHER_CHIPS]
    else:
        plan = [(*_copy_ends(src_ref, land_ref, mode, me, _flat(*peer_of(k))), peer_of(k)) for k in range(1, N_DEV)]
    return [pltpu.make_async_remote_copy(src_ref=src, dst_ref=dst, send_sem=send_sems.at[i], recv_sem=recv_sems.at[i],
                                         device_id=peer, device_id_type=MESH)
            for i, (src, dst, peer) in enumerate(plan)]


def _place_own(src, mode, dev, *, name):
    rows = src.shape[1] if mode == "scatter_slot" else src.shape[0]
    tr = _tile(rows, 512, BF16_ROWS)
    land = _land_shape(src, mode)
    width = land[-1] if mode.startswith("scatter") else src.shape[1]
    slot = pl.BlockSpec((1, tr, width), lambda i, d: (d[0], i, 0))
    cols = pl.BlockSpec((tr, width), lambda i, d: (i, d[0]))
    whole = pl.BlockSpec((tr, width), lambda i, d: (i, 0))
    in_spec, out_spec = {"gather_slot": (whole, slot), "gather_cols": (whole, cols), "scatter_slot": (slot, slot),
                         "scatter_cols": (cols, slot)}[mode]

    def body(dev_ref, src_ref, land_ref):
        land_ref[...] = src_ref[...].reshape(land_ref.shape)

    return pl.pallas_call(
        body, name=name, out_shape=jax.ShapeDtypeStruct(land, src.dtype),
        grid_spec=pltpu.PrefetchScalarGridSpec(num_scalar_prefetch=1, grid=(rows // tr,), in_specs=[in_spec],
                                               out_specs=out_spec),
        compiler_params=_cp("parallel"),
    )(dev, src)


def _exchange_start(src, mode, dev, *, name, land=None, after=None):
    if land is None:
        land = _place_own(src, "gather_slot" if mode == "gather_chip" else mode, dev, name=name + "_own")
    n_copies = _n_copies(mode)

    def body(*refs):
        src_ref, land_ref = refs[:2]
        send_sems, recv_sems, _, _, token = refs[-5:]
        for cp in _exchange_copies(src_ref, land_ref, send_sems, recv_sems, mode):
            cp.start()
        token[...] = jnp.zeros_like(token)

    hbm = pltpu.with_memory_space_constraint
    *handle, token = pl.pallas_call(
        body, name=name,
        out_shape=(pltpu.SemaphoreType.DMA((n_copies,)), pltpu.SemaphoreType.DMA((n_copies,)),
                   pltpu.HBM(src.shape, src.dtype), pltpu.HBM(land.shape, land.dtype),
                   jax.ShapeDtypeStruct((SUBLANES, LANES), F32)),
        in_specs=(_HBM, _HBM) + ((pl.BlockSpec(memory_space=pl.ANY),) if after is not None else ()),
        out_specs=(_SEM, _SEM, _HBM, _HBM, pl.BlockSpec(memory_space=pltpu.VMEM)),
        input_output_aliases={0: 2, 1: 3}, compiler_params=pltpu.CompilerParams(has_side_effects=_EFFECT),
    )(hbm(src, pltpu.HBM), hbm(land, pltpu.HBM), *(() if after is None else (after,)))
    return (tuple(handle), mode), token


def _exchange_wait(pending, after, *, name):
    (send_sems, recv_sems, src_thru, land_thru), mode = pending

    def body(src_ref, land_ref, send_sems, recv_sems, after_ref, src_dead, got_ref):
        for cp in _exchange_copies(src_ref, land_ref, send_sems, recv_sems, mode):
            cp.wait_send()
            cp.wait_recv()

    return pl.pallas_call(
        body, name=name, out_shape=(pltpu.HBM(src_thru.shape, src_thru.dtype), pltpu.HBM(land_thru.shape, land_thru.dtype)),
        in_specs=(_HBM, _HBM, _SEM, _SEM, pl.BlockSpec(memory_space=pl.ANY)), out_specs=(_HBM, _HBM),
        input_output_aliases={0: 0, 1: 1}, compiler_params=pltpu.CompilerParams(has_side_effects=_EFFECT),
    )(src_thru, land_thru, send_sems, recv_sems, after)[1]


def _after(x, token):
    return x + token[0, 0].astype(x.dtype)


def _touch(*arrays, name):
    def body(*refs):
        refs[-1][...] = jnp.zeros_like(refs[-1])

    return pl.pallas_call(
        body, name=name, out_shape=jax.ShapeDtypeStruct((SUBLANES, LANES), F32),
        in_specs=[pl.BlockSpec(memory_space=pl.ANY)] * len(arrays), out_specs=pl.BlockSpec(memory_space=pltpu.VMEM),
    )(*arrays)


def _pad_cols(a, mult):
    pad = -a.shape[1] % mult
    return jnp.pad(a, ((0, 0), (0, pad))) if pad else a


def _pack(arrs, cols):
    flat = jnp.concatenate([a.reshape(-1).astype(F32) for a in arrs])
    sizes = [int(a.size) for a in arrs]
    flat = jnp.pad(flat, (0, -flat.shape[0] % (SUBLANES * cols)))
    return flat.reshape(-1, cols), sizes


def _unpack(flat2d, sizes, shapes):
    flat = flat2d.reshape(-1)
    out, o = [], 0
    for n, s in zip(sizes, shapes):
        out.append(flat[o:o + n].reshape(s))
        o += n
    return out


PACK_COLS = SUBLANES * LANES


def kernel(x, norm_mix_w, w_in, conv_a_w, conv_a_b, dt_bias, a_log, d_a, norm_a_w, w_proj_a, s5_lam_re, s5_lam_im, s5_log_dt, s5_b_re, s5_b_im, s5_c_re, s5_c_im, s5_d, w_s5_glu, w_out, norm_ffn_w, w_up, conv_ffn_w, conv_ffn_b, w_down, norm_final_w, loss_target, m_norm_mix_w, m_w_in, m_conv_a_w, m_conv_a_b, m_dt_bias, m_a_log, m_d_a, m_norm_a_w, m_w_proj_a, m_s5_lam_re, m_s5_lam_im, m_s5_log_dt, m_s5_b_re, m_s5_b_im, m_s5_c_re, m_s5_c_im, m_s5_d, m_w_s5_glu, m_w_out, m_norm_ffn_w, m_w_up, m_conv_ffn_w, m_conv_ffn_b, m_w_down, m_norm_final_w, v_norm_mix_w, v_w_in, v_conv_a_w, v_conv_a_b, v_dt_bias, v_a_log, v_d_a, v_norm_a_w, v_w_proj_a, v_s5_lam_re, v_s5_lam_im, v_s5_log_dt, v_s5_b_re, v_s5_b_im, v_s5_c_re, v_s5_c_im, v_s5_d, v_w_s5_glu, v_w_out, v_norm_ffn_w, v_w_up, v_conv_ffn_w, v_conv_ffn_b, v_w_down, v_norm_final_w):
    weights = dict(norm_mix_w=norm_mix_w, w_in=w_in, conv_a_w=conv_a_w, conv_a_b=conv_a_b, dt_bias=dt_bias, a_log=a_log, d_a=d_a, norm_a_w=norm_a_w, w_proj_a=w_proj_a, s5_lam_re=s5_lam_re, s5_lam_im=s5_lam_im, s5_log_dt=s5_log_dt, s5_b_re=s5_b_re, s5_b_im=s5_b_im, s5_c_re=s5_c_re, s5_c_im=s5_c_im, s5_d=s5_d, w_s5_glu=w_s5_glu, w_out=w_out, norm_ffn_w=norm_ffn_w, w_up=w_up, conv_ffn_w=conv_ffn_w, conv_ffn_b=conv_ffn_b, w_down=w_down, norm_final_w=norm_final_w)
    moms = dict(norm_mix_w=m_norm_mix_w, w_in=m_w_in, conv_a_w=m_conv_a_w, conv_a_b=m_conv_a_b, dt_bias=m_dt_bias, a_log=m_a_log, d_a=m_d_a, norm_a_w=m_norm_a_w, w_proj_a=m_w_proj_a, s5_lam_re=m_s5_lam_re, s5_lam_im=m_s5_lam_im, s5_log_dt=m_s5_log_dt, s5_b_re=m_s5_b_re, s5_b_im=m_s5_b_im, s5_c_re=m_s5_c_re, s5_c_im=m_s5_c_im, s5_d=m_s5_d, w_s5_glu=m_w_s5_glu, w_out=m_w_out, norm_ffn_w=m_norm_ffn_w, w_up=m_w_up, conv_ffn_w=m_conv_ffn_w, conv_ffn_b=m_conv_ffn_b, w_down=m_w_down, norm_final_w=m_norm_final_w)
    vars_ = dict(norm_mix_w=v_norm_mix_w, w_in=v_w_in, conv_a_w=v_conv_a_w, conv_a_b=v_conv_a_b, dt_bias=v_dt_bias, a_log=v_a_log, d_a=v_d_a, norm_a_w=v_norm_a_w, w_proj_a=v_w_proj_a, s5_lam_re=v_s5_lam_re, s5_lam_im=v_s5_lam_im, s5_log_dt=v_s5_log_dt, s5_b_re=v_s5_b_re, s5_b_im=v_s5_b_im, s5_c_re=v_s5_c_re, s5_c_im=v_s5_c_im, s5_d=v_s5_d, w_s5_glu=v_w_s5_glu, w_out=v_w_out, norm_ffn_w=v_norm_ffn_w, w_up=v_w_up, conv_ffn_w=v_conv_ffn_w, conv_ffn_b=v_conv_ffn_b, w_down=v_w_down, norm_final_w=v_norm_final_w)
    names = list(weights)
    col_sharded = ("w_in", "w_s5_glu", "w_up")
    row_sharded = ("w_proj_a", "w_out", "w_down")
    conv_sharded = ("conv_a_w", "conv_ffn_w")
    replicated = [n for n in names if n not in col_sharded + row_sharded + conv_sharded]

    t, d = x.shape[1:]
    x2, tgt = x.reshape(t, d), loss_target.reshape(t, d)
    nh = dt_bias.shape[-1]
    d_inner = norm_a_w.shape[-1]
    conv_dim = conv_a_b.shape[-1]
    gn = (conv_dim - d_inner) // 2
    ng = gn // D_STATE
    nr = nh // ng
    rp = d_inner // ng
    d_s5 = s5_d.shape[-1]
    gs, ps = s5_lam_re.shape[1:]
    cs = d_s5 // gs
    n_oct = gs // 8
    assert (gs * ps) % SCAN_COLS == 0 and 8 * cs == LANES and gs % 8 == 0
    d_ff = w_down.shape[1] * N_DEV
    dev = _flat(*_position())
    dev1 = dev.reshape(1).astype(jnp.int32)

    ka, kf = conv_a_w.shape[1], conv_ffn_w.shape[1]
    taps = jnp.concatenate([conv_a_w[0].reshape(1, -1), conv_ffn_w[0].reshape(1, -1)], axis=1)
    taps, taps_done = _all_gather(taps, name="ag_conv_taps")
    taps = taps[:, 0]

    def by_cols(n):
        return n in ("w_s5_glu", "w_up") and weights[n].shape[2] % LANES == 0

    chip_stage, t1 = _exchange_start(_after(w_in[0], taps_done).astype(BF16), "gather_chip", dev1, name="ag_w_in")
    w1 = norm_mix_w.reshape(1, 1, d) + t1[0, 0]
    hn1, = _blocked_fwd(_rms_fn, [x2], [w1], [(d, BF16)], tb=256, name="rms1")
    others = ("w_proj_a", "w_s5_glu", "w_out", "w_up", "w_down")
    gather_mode = {n: "gather_cols" if by_cols(n) else "gather_slot" for n in others}
    shards = {n: _after(weights[n][0], t1).astype(BF16) for n in others}
    lands = {n: _place_own(shards[n], gather_mode[n], dev1, name="ag_" + n + "_own") for n in others}
    w_in_2d = [a.reshape(a.shape[-2:]) for a in (w_in, m_w_in, v_w_in)]
    ready = _touch(hn1, *lands.values(), *w_in_2d, name="ready_w_in")
    forward_stage, t2 = _exchange_start(ready, "forward", dev1, land=_exchange_wait(chip_stage, ready, name="agw_w_in"),
                                        name="fw_w_in")
    w_in_blocks = _exchange_wait(forward_stage, t2, name="fww_w_in")
    pending, started = {}, t2
    for n in others:
        pending[n], token = _exchange_start(shards[n], gather_mode[n], dev1, land=lands[n], after=w_in_blocks,
                                            name="ag_" + n)
        started = started + token

    def gathered(n, after):
        g = _exchange_wait(pending[n], after, name="agw_" + n)
        if by_cols(n):
            return g
        if n in row_sharded:
            return g.reshape(-1, g.shape[2])
        return jnp.transpose(g, (1, 0, 2)).reshape(g.shape[1], -1)

    seg_sizes = dict(z=d_inner, xs=d_inner, bm=gn, cm=gn, dt=nh, u=d_s5, ga=d, gb=d)
    seg_names = tuple(seg_sizes)
    pieces, seg_at = _w_in_pieces(seg_sizes, ("z", "xs", "ga", "gb", "bm", "cm", "u", "dt"), w_in.shape[2])
    na = ka * conv_a_w.shape[2]
    cw_a = jnp.transpose(taps[:, :na].reshape(N_DEV, ka, -1), (1, 0, 2)).reshape(ka, conv_dim)
    cw_f = jnp.transpose(taps[:, na:].reshape(N_DEV, kf, -1), (1, 0, 2)).reshape(kf, 2 * d_ff)
    cb_a, cb_f = conv_a_b, conv_ffn_b
    a_cols = {"xs": slice(0, d_inner), "bm": slice(d_inner, d_inner + gn), "cm": slice(d_inner + gn, conv_dim)}

    w_in_p = _w_in_pack(w_in_blocks, pieces, seg_at, seg_sizes, started, name="w_in_pack")
    pre = {sn: _mm(hn1, w_in_p, b_win=seg_at[sn], name="in_" + sn) for sn in seg_names}
    act_a = {sn: _conv_fwd(_comb_silu, [pre[sn]], [cw_a[:, a_cols[sn]]], [cb_a[:, a_cols[sn]]], out_dtype=F32,
                           name="conv_a_" + sn) for sn in a_cols}
    dtr3 = jnp.transpose(pre["dt"][:, :nh].reshape(t, ng, nr), (1, 0, 2))
    dtb3, alog3, dsk3 = (p.reshape(ng, 1, nr) for p in (dt_bias, a_log, d_a))
    nw3 = norm_a_w.reshape(ng, 1, rp)
    yn, hsave = _ssd_fwd(act_a["xs"], act_a["bm"], act_a["cm"], pre["z"], dtr3, dtb3, alog3, dsk3, nw3, name="ssd")
    w_proj = gathered("w_proj_a", yn)
    y_a = _mm(yn, w_proj, name="proj_a")

    lam_re3, lam_im3 = s5_lam_re[0][:, None, :], s5_lam_im[0][:, None, :]
    logdt3 = s5_log_dt[0][:, None, None]
    bt_re, bt_im = jnp.transpose(s5_b_re[0], (0, 2, 1)), jnp.transpose(s5_b_im[0], (0, 2, 1))
    lb_re3, lb_im3, bb_re, bb_im = _s5_params(lam_re3, lam_im3, logdt3, bt_re, bt_im, name="s5_params")
    eye = jnp.eye(8, dtype=F32)

    def diag_b(bt):
        return (bt.reshape(n_oct, 8, cs, 1, ps) * eye[None, :, None, :, None]).reshape(n_oct, 8 * cs, 8 * ps)

    def undiag_b(blk):
        return (blk.reshape(n_oct, 8, cs, 8, ps) * eye[None, :, None, :, None]).sum(axis=3).reshape(gs, cs, ps)

    def diag_c(cm):
        ct = jnp.transpose(cm.reshape(n_oct, 8, cs, ps), (0, 1, 3, 2))
        return (ct[:, :, :, None, :] * eye[None, :, None, :, None]).reshape(n_oct, 8 * ps, 8 * cs)

    def undiag_c(blk):
        ct = (blk.reshape(n_oct, 8, ps, 8, cs) * eye[None, :, None, :, None]).sum(axis=3)
        return jnp.transpose(ct, (0, 1, 3, 2)).reshape(gs, cs, ps)

    b_blk_re, b_blk_im = diag_b(bb_re), diag_b(bb_im)
    c_blk_re, c_blk_imn = diag_c(s5_c_re[0]), diag_c(-s5_c_im[0])
    d3 = s5_d.reshape(n_oct, 1, LANES)
    lb_re, lb_im = lb_re3.reshape(1, gs * ps), lb_im3.reshape(1, gs * ps)
    u = pre["u"]
    bu_re, bu_im = _blocked_fwd(_s5_bu_fn, [u], [b_blk_re, b_blk_im], [(gs * ps, F32)] * 2, nj=n_oct, tb=512,
                                name="s5_bu")
    s_re, s_im = _s5_scan_fwd(bu_re, bu_im, lb_re, lb_im, name="s5_scan")
    yb, = _blocked_fwd(_s5_out_fn, [s_re, s_im, u], [c_blk_re, c_blk_imn, d3], [(d_s5, BF16)], nj=n_oct, tb=512,
                       name="s5_out")
    w_glu = gathered("w_s5_glu", yb)
    glu_v = _mm(yb, w_glu, b_win=(0, d), name="glu_v")
    glu_g = _mm(yb, w_glu, b_win=(d, d), name="glu_g")
    merged, = _blocked_fwd(_merge_fn, [glu_v, glu_g, pre["ga"], pre["gb"], y_a], [], [(d, BF16)], tb=256,
                           name="merge")
    w_o = gathered("w_out", merged)
    h1 = _mm(merged, w_o, acc=x2, name="out_proj")
    w2 = norm_ffn_w.reshape(1, 1, d)
    hn2, = _blocked_fwd(_rms_fn, [h1], [w2], [(d, BF16)], tb=256, name="rms2")
    w_u = gathered("w_up", hn2)
    up_g = _mm(hn2, w_u, b_win=(0, d_ff), name="up_g")
    up_v = _mm(hn2, w_u, b_win=(d_ff, d_ff), name="up_v")
    f_w = [cw_f[:, :d_ff], cw_f[:, d_ff:]]
    f_b = [cb_f[:, :d_ff], cb_f[:, d_ff:]]
    act = _conv_fwd(_comb_glu, [up_g, up_v], f_w, f_b, out_dtype=BF16, name="conv_ffn")
    w_dn = gathered("w_down", act)
    h2 = _mm(act, w_dn, acc=h1, name="down")
    loss_tile, dh2, dh2_b, g_final = _loss_head(h2, norm_final_w.reshape(1, d), tgt, name="loss_head")

    grads, scattering = {}, {}

    def scatter_start(n, g):
        if by_cols(n):
            src, mode = g, "scatter_cols"
        elif n in row_sharded:
            src, mode = g.reshape(N_DEV, -1, g.shape[1]), "scatter_slot"
        elif n == "w_in":
            src, mode = g, "scatter_slot"
        else:
            src, mode = jnp.transpose(g.reshape(g.shape[0], N_DEV, -1), (1, 0, 2)), "scatter_slot"
        scattering[n], token = _exchange_start(src, mode, dev1, name="rs_" + n)
        return token

    d_act = _mm(dh2_b, w_dn, tb=True, name="d_act")
    g_down = _mm(act, dh2_b, ta=True, out_dtype=BF16, name="g_w_down")
    tok = scatter_start("w_down", g_down)
    (dup_g, dwf_g, dbf_g), (dup_v, dwf_v, dbf_v) = _conv_bwd(
        _comb_glu, [up_g, up_v], f_w, [_after(f_b[0], tok), f_b[1]], d_act, dx_dtype=BF16, name="conv_ffn_bwd")
    dhn2 = _mm(dup_g, w_u, tb=True, b_win=(0, d_ff), name="d_hn2_g")
    dhn2 = _mm(dup_v, w_u, tb=True, b_win=(d_ff, d_ff), acc=dhn2, name="d_hn2_v")
    g_up = _mm(hn2, dup_g, ta=True, into=(lax.empty((d, 2 * d_ff), BF16), 0), name="g_w_up_g")
    g_up = _mm(hn2, dup_v, ta=True, into=(g_up, d_ff), name="g_w_up_v")
    tok = scatter_start("w_up", g_up)
    dh1, dh1_b, g_w2 = _blocked_bwd(_rms_fn, [h1], [_after(w2, tok)], [dhn2], [(F32, BF16)], adds={0: dh2}, tb=256,
                                    name="rms2_bwd")
    d_merged = _mm(dh1_b, w_o, tb=True, name="d_merged")
    g_out = _mm(merged, dh1_b, ta=True, out_dtype=BF16, name="g_w_out")
    tok = scatter_start("w_out", g_out)
    dglu_v, dglu_g, dga, dgb, dy_a = _blocked_bwd(
        _merge_fn, [glu_v, glu_g, pre["ga"], pre["gb"], y_a], [], [d_merged], [BF16] * 5, tb=128, name="merge_bwd")
    dyb = _mm(dglu_v, w_glu, tb=True, b_win=(0, d), name="d_yb_v")
    dyb = _mm(dglu_g, w_glu, tb=True, b_win=(d, d), acc=dyb, name="d_yb_g")
    g_glu = _mm(yb, dglu_v, ta=True, into=(lax.empty((d_s5, 2 * d), BF16), 0), name="g_w_glu_v")
    g_glu = _mm(yb, dglu_g, ta=True, into=(g_glu, d), name="g_w_glu_g")
    tok = tok + scatter_start("w_s5_glu", g_glu)
    ds_re, ds_im, du_skip, dc_blk_re, dc_blk_imn, dd3 = _blocked_bwd(
        _s5_out_fn, [s_re, s_im, u], [c_blk_re, c_blk_imn, _after(d3, tok)], [dyb], [F32, F32, F32], nj=n_oct, tb=512,
        name="s5_out_bwd")
    dbu_re, dbu_im, dlb_re, dlb_im = _s5_scan_bwd(s_re, s_im, ds_re, ds_im, lb_re, lb_im, name="s5_scan_bwd")
    du, db_blk_re, db_blk_im = _blocked_bwd(
        _s5_bu_fn, [u], [b_blk_re, b_blk_im], [dbu_re, dbu_im], [BF16], adds={0: du_skip}, nj=n_oct, tb=512,
        name="s5_bu_bwd")
    g_lre, g_lim, g_ldt, g_bt_re, g_bt_im = _s5_params(
        lam_re3, lam_im3, logdt3, bt_re, bt_im,
        cts=(dlb_re.reshape(gs, 1, ps), dlb_im.reshape(gs, 1, ps), undiag_b(db_blk_re), undiag_b(db_blk_im)),
        name="s5_params_bwd")
    grads["s5_lam_re"], grads["s5_lam_im"] = g_lre.reshape(s5_lam_re.shape), g_lim.reshape(s5_lam_im.shape)
    grads["s5_log_dt"] = g_ldt.reshape(s5_log_dt.shape)
    grads["s5_b_re"] = jnp.transpose(g_bt_re, (0, 2, 1)).reshape(s5_b_re.shape)
    grads["s5_b_im"] = jnp.transpose(g_bt_im, (0, 2, 1)).reshape(s5_b_im.shape)
    grads["s5_c_re"] = undiag_c(dc_blk_re).reshape(s5_c_re.shape)
    grads["s5_c_im"] = -undiag_c(dc_blk_imn).reshape(s5_c_im.shape)
    grads["s5_d"] = dd3.reshape(s5_d.shape)

    dyn = _mm(dy_a, w_proj, tb=True, name="d_yn")
    g_proj = _mm(yn, dy_a, ta=True, out_dtype=BF16, name="g_w_proj_a")
    tok = scatter_start("w_proj_a", g_proj)
    dxs, dbm, dcm, dz, ddtr3, g_dtb, g_alog, g_dsk, g_nw = _ssd_bwd(
        act_a["xs"], act_a["bm"], act_a["cm"], pre["z"], dtr3, hsave, dtb3, alog3, dsk3, _after(nw3, tok), dyn,
        name="ssd_bwd")
    grads["dt_bias"], grads["a_log"], grads["d_a"] = (g.reshape(1, nh) for g in (g_dtb, g_alog, g_dsk))
    grads["norm_a_w"] = g_nw.reshape(1, d_inner)
    dpre = {"z": dz, "u": du, "ga": dga, "gb": dgb}
    dcw, dcb = {}, {}
    for sn, dact in (("xs", dxs), ("bm", dbm), ("cm", dcm)):
        (dpre[sn], dcw[sn], dcb[sn]), = _conv_bwd(
            _comb_silu, [pre[sn]], [cw_a[:, a_cols[sn]]], [cb_a[:, a_cols[sn]]], dact, dx_dtype=BF16,
            name="conv_a_bwd_" + sn)
    dpre["dt"] = _pad_cols(jnp.transpose(ddtr3, (1, 0, 2)).reshape(t, nh), LANES).astype(BF16)
    g_in = _w_in_unpack({sn: _mm(hn1, dpre[sn], ta=True, name="g_w_in_" + sn) for sn in seg_names}, pieces,
                        w_in.shape[2], name="w_in_unpack")
    tok = scatter_start("w_in", g_in)
    dhn1 = _mm(_after(dpre["dt"], tok), w_in_p, tb=True, b_win=seg_at["dt"], name="d_hn1_dt")
    for sn in seg_names:
        if sn != "dt":
            dhn1 = _mm(dpre[sn], w_in_p, tb=True, b_win=seg_at[sn], acc=dhn1, name="d_hn1_" + sn)
    dx, g_w1 = _blocked_bwd(_rms_fn, [x2], [w1], [dhn1], [F32], adds={0: dh1}, tb=256, name="rms1_bwd")

    grads["norm_mix_w"], grads["norm_ffn_w"] = g_w1.reshape(1, d), g_w2.reshape(1, d)
    grads["norm_final_w"] = g_final.reshape(d)
    grads["conv_a_b"] = jnp.concatenate([dcb["xs"], dcb["bm"], dcb["cm"]], axis=1)
    grads["conv_ffn_b"] = jnp.concatenate([dbf_g, dbf_v], axis=1)
    g_cw_a = jnp.concatenate([dcw["xs"], dcw["bm"], dcw["cm"]], axis=1)
    g_cw_f = jnp.concatenate([dwf_g, dwf_v], axis=1)

    small = [grads[n] for n in replicated] + [g_cw_a, g_cw_f, loss_tile[:1, :1]]
    packed, sizes = _pack(small, PACK_COLS)
    summed = _sum_parts(_all_gather(packed, name="ag_small_grads")[0], name="sum_small_grads")
    *rep_sums, s_cw_a, s_cw_f, loss = _unpack(summed, sizes, [a.shape for a in small])
    for n, g in zip(replicated, rep_sums):
        grads[n] = g
    wa, wf = conv_a_w.shape[2], conv_ffn_w.shape[2]
    grads["conv_a_w"] = lax.dynamic_slice_in_dim(s_cw_a, dev * wa, wa, axis=1)[None]
    grads["conv_ffn_w"] = lax.dynamic_slice_in_dim(s_cw_f, dev * wf, wf, axis=1)[None]

    delta, new_m, new_v = {}, {}, {}
    done = dx
    for n in ("w_down", "w_up", "w_out", "w_s5_glu", "w_proj_a", "w_in"):
        shape = weights[n].shape
        two_d = lambda a: a.reshape(shape[-2], shape[-1])
        w2, m2, v2 = two_d(weights[n]), two_d(moms[n]), two_d(vars_[n])
        ready = _touch(done, w2, m2, v2, *((packed,) if n == "w_in" else ()), name="ready_" + n)
        land = _exchange_wait(scattering[n], ready, name="rsw_" + n)
        g = _sum_parts(land, name="rs_sum_" + n)
        grads[n] = g.reshape(shape)
        dl, nm, nv = _adamw(w2, g, m2, v2, name="adamw_" + n)
        delta[n], new_m[n], new_v[n] = dl.reshape(shape), nm.reshape(shape), nv.reshape(shape)
        done = dl
    for n in replicated + list(conv_sharded):
        shape = weights[n].shape
        two_d = lambda a: a.reshape(-1, shape[-1])
        dl, nm, nv = _adamw(two_d(weights[n]), two_d(grads[n]), two_d(moms[n]), two_d(vars_[n]), name="adamw_" + n)
        delta[n], new_m[n], new_v[n] = dl.reshape(shape), nm.reshape(shape), nv.reshape(shape)

    return (loss.reshape(()), dx.reshape(x.shape), *[grads[n] for n in names], *[delta[n] for n in names],
            *[new_m[n] for n in names], *[new_v[n] for n in names])
```

```python
import functools

import jax
import jax.numpy as jnp
from jax import lax
from jax.experimental import pallas as pl
from jax.experimental.pallas import tpu as pltpu

F32 = jnp.float32
BF16 = jnp.bfloat16
HIGHEST = lax.Precision.HIGHEST
MESH = pl.DeviceIdType.MESH

EPS = 1e-6
EIG_MAX = -1e-4
D_STATE = 128
CHUNK = 256
ADAM_LR = 0.001
ADAM_B1 = 0.9
ADAM_B2 = 0.999
ADAM_EPS = 1e-08
ADAM_WD = 0.01
ADAM_STEP = 10
N_DEV = 8
LANES = 128
SUBLANES = 8
VMEM_LIMIT = 56 * 1024 * 1024
MM_MAX_K = 4096


def _cp(*sem):
    return pltpu.CompilerParams(dimension_semantics=sem, vmem_limit_bytes=VMEM_LIMIT)


def _tile(dim, pref, unit=LANES):
    if dim <= unit:
        return dim
    t = (min(pref, dim) // unit) * unit
    while dim % t:
        t -= unit
    return t


_DIMS = {"nn": (((1,), (0,)), ((), ())), "nt": (((1,), (1,)), ((), ())), "tn": (((0,), (0,)), ((), ()))}


def _dot(a, b, kind):
    return lax.dot_general(a.astype(BF16), b.astype(BF16), _DIMS[kind], preferred_element_type=F32)


@functools.partial(jax.custom_vjp, nondiff_argnums=(2,))
def _bdot(a, b, kind):
    return _dot(a, b, kind)


def _bdot_fwd(a, b, kind):
    return _dot(a, b, kind), (a, b)


def _bdot_bwd(kind, res, g):
    a, b = res
    if kind == "nn":
        return _dot(g, b, "nt"), _dot(a, g, "tn")
    if kind == "nt":
        return _dot(g, b, "nn"), _dot(g, a, "tn")
    return _dot(b, g, "nt"), _dot(a, g, "nn")


_bdot.defvjp(_bdot_fwd, _bdot_bwd)


def _mm(a, b, *, ta=False, tb=False, acc=None, out_dtype=F32, name, b_win=None, into=None):
    assert not (ta and tb)
    m, k = (a.shape[1], a.shape[0]) if ta else a.shape
    b_off, b_size = b_win or (0, b.shape[1])
    n = b.shape[0] if tb else b_size
    assert (b_size if tb else b.shape[0]) == k, (a.shape, b.shape, ta, tb, b_win)
    o_off = into[1] if into else 0
    nk = -(-k // MM_MAX_K)
    while k % nk or (k // nk) % LANES or (tb and b_off % (k // nk)):
        nk += 1
    tk = k // nk
    tm, tn = _tile(m, 1024), _tile(n, 1024)
    while o_off % tn or (not tb and b_off % tn):
        tn = _tile(n, tn - LANES)
    kind = "tn" if ta else ("nt" if tb else "nn")
    a_spec = pl.BlockSpec((tk, tm), lambda i, j, l: (l, i)) if ta else pl.BlockSpec((tm, tk), lambda i, j, l: (i, l))
    if tb:
        b_spec = pl.BlockSpec((tn, tk), lambda i, j, l: (j, l + b_off // tk))
    else:
        b_spec = pl.BlockSpec((tk, tn), lambda i, j, l: (l, j + b_off // tn))
    c_spec = pl.BlockSpec((tm, tn), lambda i, j, l: (i, j))
    o_spec = pl.BlockSpec((tm, tn), lambda i, j, l: (i, j + o_off // tn))
    has_acc = acc is not None

    def body(*refs):
        a_ref, b_ref = refs[:2]
        c_ref = refs[2] if has_acc else None
        o_ref = refs[2 + has_acc + (into is not None)]
        if nk == 1:
            res = _dot(a_ref[...], b_ref[...], kind)
            if has_acc:
                res = res + c_ref[...].astype(F32)
            o_ref[...] = res.astype(o_ref.dtype)
            return
        acc_ref = refs[-1]
        l = pl.program_id(2)

        @pl.when(l == 0)
        def _():
            if has_acc:
                acc_ref[...] = c_ref[...].astype(F32)
            else:
                acc_ref[...] = jnp.zeros_like(acc_ref)

        acc_ref[...] += _dot(a_ref[...], b_ref[...], kind)

        @pl.when(l == nk - 1)
        def _():
            o_ref[...] = acc_ref[...].astype(o_ref.dtype)

    ins = [a, b] + ([acc] if has_acc else []) + ([into[0]] if into else [])
    in_specs = [a_spec, b_spec] + ([c_spec] if has_acc else []) + ([pl.BlockSpec(memory_space=pl.ANY)] if into else [])
    out_shape = jax.ShapeDtypeStruct(into[0].shape, into[0].dtype) if into else jax.ShapeDtypeStruct((m, n), out_dtype)
    return pl.pallas_call(
        body, name=name, grid=(m // tm, n // tn, nk), in_specs=in_specs, out_specs=o_spec, out_shape=out_shape,
        input_output_aliases={len(ins) - 1: 0} if into else {},
        scratch_shapes=[pltpu.VMEM((tm, tn), F32)] if nk > 1 else [],
        compiler_params=_cp("parallel", "parallel", "arbitrary"),
    )(*ins)


def _w_in_pieces(seg_sizes, seg_order, n_blk):
    layout, o = {}, 0
    for sn in seg_order:
        width = -(-seg_sizes[sn] // LANES) * LANES
        layout[sn] = (o, width)
        o += width
    pieces, start = [], 0
    for sn, sz in seg_sizes.items():
        lo = start
        while lo < start + sz:
            blk = lo // n_blk
            hi = min(start + sz, (blk + 1) * n_blk)
            pieces.append((blk, lo - blk * n_blk, sn, lo - start, layout[sn][0] + lo - start, hi - lo))
            lo = hi
        start += sz
    return pieces, layout


def _w_in_pack(gathered, pieces, layout, seg_sizes, after, *, name, tr=256):
    _, k, n_blk = gathered.shape
    n_pad = sum(w for _, w in layout.values())

    def body(g_ref, after_ref, o_ref):
        for sn, (off, width) in layout.items():
            if width != seg_sizes[sn]:
                o_ref[:, pl.ds(off + seg_sizes[sn], width - seg_sizes[sn])] = jnp.zeros(
                    (tr, width - seg_sizes[sn]), o_ref.dtype)
        for blk, src, _, _, dst, width in pieces:
            o_ref[:, pl.ds(dst, width)] = g_ref[blk, :, pl.ds(src, width)]

    return pl.pallas_call(
        body, name=name, grid=(k // tr,),
        in_specs=[pl.BlockSpec((N_DEV, tr, n_blk), lambda i: (0, i, 0)), pl.BlockSpec(memory_space=pl.ANY)],
        out_specs=pl.BlockSpec((tr, n_pad), lambda i: (i, 0)), out_shape=jax.ShapeDtypeStruct((k, n_pad), gathered.dtype),
        compiler_params=_cp("parallel"),
    )(gathered, after)


def _w_in_unpack(seg_grads, pieces, n_blk, *, name, tr=128):
    names = list(seg_grads)
    k = seg_grads[names[0]].shape[0]

    def body(*refs):
        o_ref = refs[-1]
        seg_ref = dict(zip(names, refs))
        for blk, dst, sn, src, _, width in pieces:
            o_ref[blk, :, pl.ds(dst, width)] = seg_ref[sn][:, pl.ds(src, width)].astype(o_ref.dtype)

    return pl.pallas_call(
        body, name=name, grid=(k // tr,),
        in_specs=[pl.BlockSpec((tr, seg_grads[sn].shape[1]), lambda i: (i, 0)) for sn in names],
        out_specs=pl.BlockSpec((N_DEV, tr, n_blk), lambda i: (0, i, 0)),
        out_shape=jax.ShapeDtypeStruct((N_DEV, k, n_blk), BF16), compiler_params=_cp("parallel"),
    )(*[seg_grads[sn] for sn in names])


def _row_spec(arr, tb, nj):
    return pl.BlockSpec((tb, arr.shape[1] // nj), lambda j, i: (i, j))


def _par_spec(arr):
    return pl.BlockSpec((1,) + arr.shape[1:], lambda j, i: (j, 0, 0))


def _blocked_fwd(fn, rows, params, outs, *, nj=1, tb, name):
    t = rows[0].shape[0]
    nr, npar = len(rows), len(params)

    def body(*refs):
        res = fn(*[r[...] for r in refs[:nr]], *[p[0] for p in refs[nr:nr + npar]])
        for o_ref, val in zip(refs[nr + npar:], res):
            o_ref[...] = val.astype(o_ref.dtype)

    return pl.pallas_call(
        body, name=name, grid=(nj, t // tb),
        in_specs=[_row_spec(a, tb, nj) for a in rows] + [_par_spec(p) for p in params],
        out_specs=[pl.BlockSpec((tb, c // nj), lambda j, i: (i, j)) for c, _ in outs],
        out_shape=[jax.ShapeDtypeStruct((t, c), dt) for c, dt in outs],
        compiler_params=_cp("parallel", "arbitrary"),
    )(*rows, *params)


def _blocked_bwd(fn, rows, params, cts, row_grad_dtypes, *, adds=None, nj=1, tb, name):
    t = rows[0].shape[0]
    nr, npar, nct = len(rows), len(params), len(cts)
    adds = adds or {}
    add_keys = sorted(adds)
    want, want_dtypes = [], []
    for k, dts in enumerate(row_grad_dtypes):
        for dt in (dts if isinstance(dts, tuple) else (dts,)):
            if dt is not None:
                want.append(k)
                want_dtypes.append(dt)

    def body(*refs):
        row_refs = refs[:nr]
        par_refs = refs[nr:nr + npar]
        ct_refs = refs[nr + npar:nr + npar + nct]
        add_refs = dict(zip(add_keys, refs[nr + npar + nct:nr + npar + nct + len(add_keys)]))
        out_refs = refs[nr + npar + nct + len(add_keys):]
        _, vjp = jax.vjp(fn, *[r[...] for r in row_refs], *[p[0] for p in par_refs])
        grads = vjp(tuple(c[...].astype(F32) for c in ct_refs))
        for o_ref, k in zip(out_refs, want):
            g = grads[k]
            if k in add_refs:
                g = g + add_refs[k][...].astype(F32)
            o_ref[...] = g.astype(o_ref.dtype)
        first = pl.program_id(1) == 0
        for o_ref, g in zip(out_refs[len(want):], grads[nr:]):
            @pl.when(first)
            def _(o_ref=o_ref):
                o_ref[...] = jnp.zeros_like(o_ref)
            o_ref[0] += g

    add_arrs = [adds[k] for k in add_keys]
    return pl.pallas_call(
        body, name=name, grid=(nj, t // tb),
        in_specs=[_row_spec(a, tb, nj) for a in rows] + [_par_spec(p) for p in params]
        + [_row_spec(c, tb, nj) for c in cts] + [_row_spec(a, tb, nj) for a in add_arrs],
        out_specs=[_row_spec(rows[k], tb, nj) for k in want] + [_par_spec(p) for p in params],
        out_shape=[jax.ShapeDtypeStruct(rows[k].shape, dt) for k, dt in zip(want, want_dtypes)]
        + [jax.ShapeDtypeStruct(p.shape, F32) for p in params],
        compiler_params=_cp("parallel", "arbitrary"),
    )(*rows, *params, *cts, *add_arrs)


def _rms_fn(x, w):
    return (x * lax.rsqrt(jnp.mean(x * x, axis=-1, keepdims=True) + EPS) * w,)


def _silu(x):
    return x * jax.nn.sigmoid(x)


def _merge_fn(glu_v, glu_g, g_a, g_b, y_a):
    y_b = glu_v * jax.nn.sigmoid(glu_g)
    return (jax.nn.sigmoid(g_a) * y_a + jax.nn.sigmoid(g_b) * y_b,)


def _s5_bu_fn(u, b_re, b_im):
    return _bdot(u, b_re, "nn"), _bdot(u, b_im, "nn")


def _s5_out_fn(s_re, s_im, u, c_re, c_im_neg, d):
    return (jax.nn.gelu(_bdot(s_re, c_re, "nn") + _bdot(s_im, c_im_neg, "nn") + d * u),)


HALO = SUBLANES


STRIP = 64


def _conv_strip(ext_ref, w_ref, b_ref, r0, cols):
    kw = w_ref.shape[0]
    xs = [ext_ref[pl.ds(r0 + HALO - kw + 1 + k, STRIP), cols] for k in range(kw)]
    c = b_ref[:, cols] + w_ref[0:1, cols] * xs[0]
    for k in range(1, kw):
        c = c + w_ref[k:k + 1, cols] * xs[k]
    return c, xs


def _fold(x):
    return x.reshape(STRIP // SUBLANES, SUBLANES, LANES).sum(axis=0)


def _conv_specs(xs, ws, bs, tb, cb, time_of):
    specs = []
    for x, w, b in zip(xs, ws, bs):
        specs += [
            pl.BlockSpec((HALO, cb), lambda j, i: (jnp.maximum(time_of(i) * (tb // HALO) - 1, 0), j)),
            pl.BlockSpec((tb, cb), lambda j, i: (time_of(i), j)),
            pl.BlockSpec((w.shape[0], cb), lambda j, i: (0, j)),
            pl.BlockSpec((1, cb), lambda j, i: (0, j)),
        ]
    return specs


def _conv_fwd(comb, xs, ws, bs, *, out_dtype, name, tb=512):
    t, c = xs[0].shape
    cb = _tile(c, 512)
    ns = len(xs)

    def body(*refs):
        i = pl.program_id(1)
        o_ref = refs[4 * ns]
        exts = refs[4 * ns + 1:]
        for s in range(ns):
            xp_ref, xm_ref = refs[4 * s:4 * s + 2]
            exts[s][pl.ds(0, HALO), :] = jnp.where(i == 0, 0.0, xp_ref[...])
            exts[s][pl.ds(HALO, tb), :] = xm_ref[...]
        for c0 in range(0, cb, LANES):
            cols = pl.ds(c0, LANES)
            for r0 in range(0, tb, STRIP):
                cs = [_conv_strip(exts[s], refs[4 * s + 2], refs[4 * s + 3], r0, cols)[0] for s in range(ns)]
                o_ref[pl.ds(r0, STRIP), cols] = comb(*cs).astype(out_dtype)

    flat = [a for x, w, b in zip(xs, ws, bs) for a in (x, x, w, b)]
    return pl.pallas_call(
        body, name=name, grid=(c // cb, t // tb),
        in_specs=_conv_specs(xs, ws, bs, tb, cb, lambda i: i),
        out_specs=pl.BlockSpec((tb, cb), lambda j, i: (i, j)),
        out_shape=jax.ShapeDtypeStruct((t, c), out_dtype),
        scratch_shapes=[pltpu.VMEM((HALO + tb, cb), F32) for _ in range(ns)],
        compiler_params=_cp("parallel", "arbitrary"),
    )(*flat)


def _conv_bwd(comb, xs, ws, bs, dy, *, dx_dtype, name, tb=512):
    t, c = xs[0].shape
    cb = _tile(c, 512)
    ns = len(xs)
    nt = t // tb
    kw = ws[0].shape[0]

    def body(*refs):
        step = pl.program_id(1)
        dy_ref = refs[4 * ns]
        out_refs = refs[4 * ns + 1:4 * ns + 1 + 3 * ns]
        scratch = refs[4 * ns + 1 + 3 * ns:]
        exts, dcs, carries = scratch[:ns], scratch[ns:2 * ns], scratch[2 * ns:]

        @pl.when(step == 0)
        def _():
            for s in range(ns):
                carries[s][...] = jnp.zeros_like(carries[s])
                out_refs[3 * s + 1][...] = jnp.zeros_like(out_refs[3 * s + 1])
                out_refs[3 * s + 2][...] = jnp.zeros_like(out_refs[3 * s + 2])

        for s in range(ns):
            xp_ref, xm_ref = refs[4 * s:4 * s + 2]
            exts[s][pl.ds(0, HALO), :] = jnp.where(step == nt - 1, 0.0, xp_ref[...])
            exts[s][pl.ds(HALO, tb), :] = xm_ref[...]
            dcs[s][pl.ds(tb, HALO), :] = carries[s][...]
        for c0 in range(0, cb, LANES):
            cols = pl.ds(c0, LANES)
            acc_w = [[jnp.zeros((SUBLANES, LANES), F32) for _ in range(kw)] for _ in range(ns)]
            acc_b = [jnp.zeros((SUBLANES, LANES), F32) for _ in range(ns)]
            for r0 in range(0, tb, STRIP):
                strips = [_conv_strip(exts[s], refs[4 * s + 2], refs[4 * s + 3], r0, cols) for s in range(ns)]
                _, vjp = jax.vjp(comb, *[cs for cs, _ in strips])
                grads = vjp(dy_ref[pl.ds(r0, STRIP), cols].astype(F32))
                for s in range(ns):
                    dcs[s][pl.ds(r0, STRIP), cols] = grads[s]
                    acc_b[s] = acc_b[s] + _fold(grads[s])
                    for k in range(kw):
                        acc_w[s][k] = acc_w[s][k] + _fold(grads[s] * strips[s][1][k])
            for s in range(ns):
                dw_ref, db_ref = out_refs[3 * s + 1], out_refs[3 * s + 2]
                db_ref[:, cols] += jnp.sum(acc_b[s], axis=0, keepdims=True)
                for k in range(kw):
                    dw_ref[k:k + 1, cols] += jnp.sum(acc_w[s][k], axis=0, keepdims=True)
        for s in range(ns):
            w_ref, dx_ref = refs[4 * s + 2], out_refs[3 * s]
            for c0 in range(0, cb, LANES):
                cols = pl.ds(c0, LANES)
                for r0 in range(0, tb, STRIP):
                    dx = w_ref[kw - 1:kw, cols] * dcs[s][pl.ds(r0, STRIP), cols]
                    for k in range(kw - 1):
                        dx = dx + w_ref[k:k + 1, cols] * dcs[s][pl.ds(r0 + kw - 1 - k, STRIP), cols]
                    dx_ref[pl.ds(r0, STRIP), cols] = dx.astype(dx_dtype)
            carries[s][...] = dcs[s][pl.ds(0, HALO), :]

    flat = [a for x, w, b in zip(xs, ws, bs) for a in (x, x, w, b)]
    rev = lambda i: nt - 1 - i
    out_specs, out_shape = [], []
    for x, w, b in zip(xs, ws, bs):
        out_specs += [pl.BlockSpec((tb, cb), lambda j, i: (rev(i), j)),
                      pl.BlockSpec((w.shape[0], cb), lambda j, i: (0, j)),
                      pl.BlockSpec((1, cb), lambda j, i: (0, j))]
        out_shape += [jax.ShapeDtypeStruct((t, c), dx_dtype), jax.ShapeDtypeStruct(w.shape, F32),
                      jax.ShapeDtypeStruct(b.shape, F32)]
    res = pl.pallas_call(
        body, name=name, grid=(c // cb, nt),
        in_specs=_conv_specs(xs, ws, bs, tb, cb, rev) + [pl.BlockSpec((tb, cb), lambda j, i: (rev(i), j))],
        out_specs=out_specs, out_shape=out_shape,
        scratch_shapes=[pltpu.VMEM((HALO + tb, cb), F32) for _ in range(2 * ns)]
        + [pltpu.VMEM((HALO, cb), F32) for _ in range(ns)],
        compiler_params=_cp("parallel", "arbitrary"),
    )(*flat, dy)
    return [tuple(res[3 * s:3 * s + 3]) for s in range(ns)]


def _comb_silu(c):
    return _silu(c)


def _comb_glu(cg, cv):
    return _silu(cg) * cv


def _ssd_fn(nheads, hdim):
    def fn(x, bm, cm, z, dtr, hin, dtb, alog, dsk, nw):
        q = x.shape[0]
        dt = jax.nn.softplus(dtr + dtb)
        da = dt * (-jnp.exp(alog))
        li = lax.broadcasted_iota(jnp.int32, (q, q), 0)
        si = lax.broadcasted_iota(jnp.int32, (q, q), 1)
        causal = li >= si
        tri = causal.astype(F32)
        acs = jnp.dot(tri, da, precision=HIGHEST, preferred_element_type=F32)
        acs_row = lax.dot_general(da, tri, (((0,), (1,)), ((), ())), precision=HIGHEST,
                                  preferred_element_type=F32)
        cb = _bdot(cm, bm, "nt")
        ch = _bdot(cm, hin, "nn")
        ys, hs = [], []
        for r in range(nheads):
            cols = slice(r * hdim, (r + 1) * hdim)
            xr = x[:, cols]
            a_col = acs[:, r:r + 1]
            decay = jnp.exp(jnp.where(causal, a_col - acs_row[r:r + 1, :], -1e30))
            xd = xr * dt[:, r:r + 1]
            y_diag = _bdot(cb * decay, xd, "nn")
            y_off = ch[:, cols] * jnp.exp(a_col)
            last = acs[q - 1:q, r:r + 1]
            st = _bdot(bm * jnp.exp(last - a_col), xd, "tn")
            hs.append(jnp.exp(last) * hin[:, cols] + st)
            ys.append(y_diag + y_off + dsk[:, r:r + 1] * xr)
        y = jnp.concatenate(ys, axis=1) * _silu(z)
        yn = y * lax.rsqrt(jnp.mean(y * y, axis=-1, keepdims=True) + EPS) * nw
        return yn, jnp.concatenate(hs, axis=1)
    return fn


def _ssd_specs(rp, nr, time_of):
    row = lambda w: pl.BlockSpec((CHUNK, w), lambda g, c: (time_of(c), g))
    par = lambda w: pl.BlockSpec((1, 1, w), lambda g, c: (g, 0, 0))
    return dict(
        x=row(rp), bc=row(D_STATE), dtr=pl.BlockSpec((1, CHUNK, nr), lambda g, c: (g, time_of(c), 0)),
        h=pl.BlockSpec((1, 1, D_STATE, rp), lambda g, c: (g, time_of(c), 0, 0)), pr=par(nr), pw=par(rp))


def _ssd_fwd(xs, bm, cm, z, dtr, dtb, alog, dsk, nw, *, name):
    t = xs.shape[0]
    g, _, nr = dtr.shape
    rp = xs.shape[1] // g
    nc = t // CHUNK
    fn = _ssd_fn(nr, rp // nr)
    sp = _ssd_specs(rp, nr, lambda c: c)

    def body(x_ref, b_ref, c_ref, z_ref, dtr_ref, dtb_ref, al_ref, dsk_ref, nw_ref, yn_ref, hs_ref, h_ref):
        @pl.when(pl.program_id(1) == 0)
        def _():
            h_ref[...] = jnp.zeros_like(h_ref)
        hin = h_ref[...]
        hs_ref[0, 0] = hin
        yn, hout = fn(x_ref[...], b_ref[...], c_ref[...], z_ref[...], dtr_ref[0], hin,
                      dtb_ref[0], al_ref[0], dsk_ref[0], nw_ref[0])
        yn_ref[...] = yn.astype(yn_ref.dtype)
        h_ref[...] = hout

    return pl.pallas_call(
        body, name=name, grid=(g, nc),
        in_specs=[sp["x"], sp["bc"], sp["bc"], sp["x"], sp["dtr"], sp["pr"], sp["pr"], sp["pr"], sp["pw"]],
        out_specs=[sp["x"], sp["h"]],
        out_shape=[jax.ShapeDtypeStruct(xs.shape, BF16), jax.ShapeDtypeStruct((g, nc, D_STATE, rp), F32)],
        scratch_shapes=[pltpu.VMEM((D_STATE, rp), F32)],
        compiler_params=_cp("parallel", "arbitrary"),
    )(xs, bm, cm, z, dtr, dtb, alog, dsk, nw)


def _ssd_bwd(xs, bm, cm, z, dtr, hsave, dtb, alog, dsk, nw, dyn, *, name):
    t = xs.shape[0]
    g, _, nr = dtr.shape
    rp = xs.shape[1] // g
    nc = t // CHUNK
    fn = _ssd_fn(nr, rp // nr)
    sp = _ssd_specs(rp, nr, lambda c: nc - 1 - c)

    def body(x_ref, b_ref, c_ref, z_ref, dtr_ref, hs_ref, dtb_ref, al_ref, dsk_ref, nw_ref, dyn_ref,
             dx_ref, db_ref, dc_ref, dz_ref, ddtr_ref, ddtb_ref, dal_ref, ddsk_ref, dnw_ref, dh_ref):
        first = pl.program_id(1) == 0

        @pl.when(first)
        def _():
            dh_ref[...] = jnp.zeros_like(dh_ref)
            for r in (ddtb_ref, dal_ref, ddsk_ref, dnw_ref):
                r[...] = jnp.zeros_like(r)

        _, vjp = jax.vjp(fn, x_ref[...], b_ref[...], c_ref[...], z_ref[...], dtr_ref[0], hs_ref[0, 0],
                         dtb_ref[0], al_ref[0], dsk_ref[0], nw_ref[0])
        dx, db, dc, dz, ddtr, dhin, ddtb, dal, ddsk, dnw = vjp((dyn_ref[...].astype(F32), dh_ref[...]))
        dx_ref[...] = dx
        db_ref[...] = db
        dc_ref[...] = dc
        dz_ref[...] = dz.astype(dz_ref.dtype)
        ddtr_ref[0] = ddtr
        dh_ref[...] = dhin
        ddtb_ref[0] += ddtb
        dal_ref[0] += dal
        ddsk_ref[0] += ddsk
        dnw_ref[0] += dnw

    sd = jax.ShapeDtypeStruct
    return pl.pallas_call(
        body, name=name, grid=(g, nc),
        in_specs=[sp["x"], sp["bc"], sp["bc"], sp["x"], sp["dtr"], sp["h"], sp["pr"], sp["pr"], sp["pr"], sp["pw"],
                  sp["x"]],
        out_specs=[sp["x"], sp["bc"], sp["bc"], sp["x"], sp["dtr"], sp["pr"], sp["pr"], sp["pr"], sp["pw"]],
        out_shape=[sd(xs.shape, F32), sd(bm.shape, F32), sd(cm.shape, F32), sd(z.shape, BF16), sd(dtr.shape, F32),
                   sd(dtb.shape, F32), sd(alog.shape, F32), sd(dsk.shape, F32), sd(nw.shape, F32)],
        scratch_shapes=[pltpu.VMEM((D_STATE, rp), F32)],
        compiler_params=_cp("parallel", "arbitrary"),
    )(xs, bm, cm, z, dtr, hsave, dtb, alog, dsk, nw, dyn)


def _s5_param_fn(lam_re, lam_im, log_dt, bt_re, bt_im):
    lr = jnp.minimum(lam_re, EIG_MAX)
    dt = jnp.exp(log_dt)
    mag = jnp.exp(lr * dt)
    lb_re = mag * jnp.cos(lam_im * dt)
    lb_im = mag * jnp.sin(lam_im * dt)
    n_re = lb_re - 1.0
    den = lr * lr + lam_im * lam_im
    k_re = (n_re * lr + lb_im * lam_im) / den
    k_im = (lb_im * lr - n_re * lam_im) / den
    return lb_re, lb_im, k_re * bt_re - k_im * bt_im, k_re * bt_im + k_im * bt_re


def _s5_params(lam_re, lam_im, log_dt, bt_re, bt_im, cts=None, *, name):
    args = (lam_re, lam_im, log_dt, bt_re, bt_im)
    n = len(args)

    def body(*refs):
        vals = [r[...] for r in refs[:n]]
        if cts is None:
            res = _s5_param_fn(*vals)
        else:
            _, vjp = jax.vjp(_s5_param_fn, *vals)
            res = vjp(tuple(r[...] for r in refs[n:n + 4]))
        for o_ref, v in zip(refs[-len(res):], res):
            o_ref[...] = v

    if cts is None:
        out = [lam_re, lam_im, bt_re, bt_im]
        ins = args
    else:
        out = list(args)
        ins = args + tuple(cts)
    return pl.pallas_call(
        body, name=name, out_shape=[jax.ShapeDtypeStruct(a.shape, F32) for a in out],
        compiler_params=pltpu.CompilerParams(vmem_limit_bytes=VMEM_LIMIT),
    )(*ins)


SCAN_COLS = 512


def _cmul(xr, xi, yr, yi):
    return xr * yr - xi * yi, xr * yi + xi * yr


def _scan_consts(a_re, a_im, cols, reverse):
    shape = (SUBLANES, cols)
    row = lax.broadcasted_iota(jnp.int32, shape, 0)
    dist = (SUBLANES - 1 - row) if reverse else row
    mr, mi = jnp.broadcast_to(a_re, shape), jnp.broadcast_to(a_im, shape)
    pr, pi = mr, mi
    mults = []
    for d in (1, 2, 4):
        mults.append((mr, mi))
        qr, qi = _cmul(pr, pi, mr, mi)
        has_bit = (dist & d) != 0
        pr, pi = jnp.where(has_bit, qr, pr), jnp.where(has_bit, qi, pi)
        mr, mi = _cmul(mr, mi, mr, mi)
    return mults, (pr, pi), dist


def _scan_group(xr, xi, consts, cr, ci, reverse):
    mults, (pr, pi), dist = consts
    for d, (mr, mi) in zip((1, 2, 4), mults):
        shift = (SUBLANES - d) if reverse else d
        sr = jnp.where(dist >= d, pltpu.roll(xr, shift, 0), 0.0)
        si = jnp.where(dist >= d, pltpu.roll(xi, shift, 0), 0.0)
        tr, ti = _cmul(mr, mi, sr, si)
        xr, xi = xr + tr, xi + ti
    last = slice(0, 1) if reverse else slice(SUBLANES - 1, SUBLANES)
    nr, ni = _cmul(pr[last], pi[last], cr, ci)
    tr, ti = _cmul(pr, pi, jnp.broadcast_to(cr, xr.shape), jnp.broadcast_to(ci, xr.shape))
    return xr + tr, xi + ti, xr[last] + nr, xi[last] + ni


def _scan_specs(tb, time_of):
    row = pl.BlockSpec((tb, SCAN_COLS), lambda j, i: (time_of(i), j))
    par = pl.BlockSpec((1, SCAN_COLS), lambda j, i: (0, j))
    return row, par


def _s5_scan_fwd(bu_re, bu_im, lb_re, lb_im, *, name, tb=512):
    t, c = bu_re.shape
    nj = c // SCAN_COLS
    row, par = _scan_specs(tb, lambda i: i)

    def body(bre_ref, bim_ref, lre_ref, lim_ref, sre_ref, sim_ref, cre_ref, cim_ref):
        @pl.when(pl.program_id(1) == 0)
        def _():
            cre_ref[...] = jnp.zeros_like(cre_ref)
            cim_ref[...] = jnp.zeros_like(cim_ref)
        consts = _scan_consts(lre_ref[...], lim_ref[...], SCAN_COLS, False)

        def group(k, carry):
            rows = pl.ds(pl.multiple_of(k * SUBLANES, SUBLANES), SUBLANES)
            sr, si, cr, ci = _scan_group(bre_ref[rows, :], bim_ref[rows, :], consts, *carry, False)
            sre_ref[rows, :] = sr
            sim_ref[rows, :] = si
            return cr, ci

        sr, si = lax.fori_loop(0, tb // SUBLANES, group, (cre_ref[...], cim_ref[...]), unroll=4)
        cre_ref[...] = sr
        cim_ref[...] = si

    return pl.pallas_call(
        body, name=name, grid=(nj, t // tb), in_specs=[row, row, par, par], out_specs=[row, row],
        out_shape=[jax.ShapeDtypeStruct((t, c), F32)] * 2,
        scratch_shapes=[pltpu.VMEM((1, SCAN_COLS), F32)] * 2,
        compiler_params=_cp("parallel", "arbitrary"),
    )(bu_re, bu_im, lb_re, lb_im)


def _s5_scan_bwd(s_re, s_im, ds_re, ds_im, lb_re, lb_im, *, name, tb=512):
    t, c = s_re.shape
    nj = c // SCAN_COLS
    nt = t // tb
    rev = lambda i: nt - 1 - i
    row, par = _scan_specs(tb, rev)
    prev = pl.BlockSpec((HALO, SCAN_COLS), lambda j, i: (jnp.maximum(rev(i) * (tb // HALO) - 1, 0), j))

    def body(sre_ref, sim_ref, pre_ref, pim_ref, dre_ref, dim_ref, lre_ref, lim_ref,
             gre_ref, gim_ref, dlre_ref, dlim_ref, cre_ref, cim_ref, ext_re, ext_im):
        step_id = pl.program_id(1)

        @pl.when(step_id == 0)
        def _():
            cre_ref[...] = jnp.zeros_like(cre_ref)
            cim_ref[...] = jnp.zeros_like(cim_ref)
            dlre_ref[...] = jnp.zeros_like(dlre_ref)
            dlim_ref[...] = jnp.zeros_like(dlim_ref)
        consts = _scan_consts(lre_ref[...], -lim_ref[...], SCAN_COLS, True)
        ngroups = tb // SUBLANES

        def group(k, carry):
            rows = pl.ds(pl.multiple_of((ngroups - 1 - k) * SUBLANES, SUBLANES), SUBLANES)
            gr, gi, cr, ci = _scan_group(dre_ref[rows, :], dim_ref[rows, :], consts, *carry, True)
            gre_ref[rows, :] = gr
            gim_ref[rows, :] = gi
            return cr, ci

        gr, gi = lax.fori_loop(0, ngroups, group, (cre_ref[...], cim_ref[...]), unroll=4)
        cre_ref[...] = gr
        cim_ref[...] = gi
        has_past = step_id != nt - 1
        ext_re[pl.ds(0, HALO), :] = jnp.where(has_past, pre_ref[...], 0.0)
        ext_im[pl.ds(0, HALO), :] = jnp.where(has_past, pim_ref[...], 0.0)
        ext_re[pl.ds(HALO, tb), :] = sre_ref[...]
        ext_im[pl.ds(HALO, tb), :] = sim_ref[...]
        pr, pi = ext_re[pl.ds(HALO - 1, tb), :], ext_im[pl.ds(HALO - 1, tb), :]
        g_re, g_im = gre_ref[...], gim_ref[...]
        dlre_ref[...] += jnp.sum(pr * g_re + pi * g_im, axis=0, keepdims=True)
        dlim_ref[...] += jnp.sum(pr * g_im - pi * g_re, axis=0, keepdims=True)

    return pl.pallas_call(
        body, name=name, grid=(nj, nt),
        in_specs=[row, row, prev, prev, row, row, par, par], out_specs=[row, row, par, par],
        out_shape=[jax.ShapeDtypeStruct((t, c), F32)] * 2 + [jax.ShapeDtypeStruct((1, c), F32)] * 2,
        scratch_shapes=[pltpu.VMEM((1, SCAN_COLS), F32)] * 2 + [pltpu.VMEM((HALO + tb, SCAN_COLS), F32)] * 2,
        compiler_params=_cp("parallel", "arbitrary"),
    )(s_re, s_im, s_re, s_im, ds_re, ds_im, lb_re, lb_im)


def _loss_fn(h, w, tgt):
    err = _rms_fn(h, w)[0] - tgt
    return 0.5 * jnp.sum(jnp.mean(err * err, axis=-1, keepdims=True), axis=0, keepdims=True)


def _loss_head(h, w, tgt, *, name, tb=256):
    t, d = h.shape

    def body(h_ref, w_ref, t_ref, loss_ref, dh_ref, dhb_ref, dw_ref):
        @pl.when(pl.program_id(0) == 0)
        def _():
            loss_ref[...] = jnp.zeros_like(loss_ref)
            dw_ref[...] = jnp.zeros_like(dw_ref)
        part, vjp = jax.vjp(_loss_fn, h_ref[...], w_ref[...], t_ref[...])
        dh, dw, _ = vjp(jnp.ones((1, 1), F32))
        loss_ref[...] += jnp.broadcast_to(part, loss_ref.shape)
        dh_ref[...] = dh
        dhb_ref[...] = dh.astype(BF16)
        dw_ref[...] += dw

    row = pl.BlockSpec((tb, d), lambda i: (i, 0))
    par = pl.BlockSpec((1, d), lambda i: (0, 0))
    return pl.pallas_call(
        body, name=name, grid=(t // tb,), in_specs=[row, par, row],
        out_specs=[pl.BlockSpec((SUBLANES, LANES), lambda i: (0, 0)), row, row, par],
        out_shape=[jax.ShapeDtypeStruct((SUBLANES, LANES), F32), jax.ShapeDtypeStruct((t, d), F32),
                   jax.ShapeDtypeStruct((t, d), BF16), jax.ShapeDtypeStruct((1, d), F32)],
        compiler_params=_cp("arbitrary"),
    )(h, w, tgt)


def _adamw(w, g, m, v, *, name):
    r, c = w.shape
    tr = _tile(r, 256, SUBLANES)

    def body(w_ref, g_ref, m_ref, v_ref, d_ref, nm_ref, nv_ref):
        g = g_ref[...]
        nm = ADAM_B1 * m_ref[...] + (1.0 - ADAM_B1) * g
        nv = ADAM_B2 * v_ref[...] + (1.0 - ADAM_B2) * (g * g)
        m_hat = nm / (1.0 - ADAM_B1 ** ADAM_STEP)
        v_hat = nv / (1.0 - ADAM_B2 ** ADAM_STEP)
        d_ref[...] = -ADAM_LR * (m_hat / (jnp.sqrt(v_hat) + ADAM_EPS) + ADAM_WD * w_ref[...])
        nm_ref[...] = nm
        nv_ref[...] = nv

    spec = pl.BlockSpec((tr, c), lambda i: (i, 0))
    return pl.pallas_call(
        body, name=name, grid=(r // tr,), in_specs=[spec] * 4, out_specs=[spec] * 3,
        out_shape=[jax.ShapeDtypeStruct((r, c), F32)] * 3, compiler_params=_cp("parallel"),
    )(w, g, m, v)


def _sum_parts(parts, *, name):
    _, r, c = parts.shape
    tr = _tile(r, 128, SUBLANES)

    def body(p_ref, o_ref):
        acc = p_ref[0].astype(F32)
        for k in range(1, N_DEV):
            acc = acc + p_ref[k].astype(F32)
        o_ref[...] = acc

    return pl.pallas_call(
        body, name=name, grid=(r // tr,), in_specs=[pl.BlockSpec((N_DEV, tr, c), lambda i: (0, i, 0))],
        out_specs=pl.BlockSpec((tr, c), lambda i: (i, 0)), out_shape=jax.ShapeDtypeStruct((r, c), F32),
        compiler_params=_cp("parallel"),
    )(parts)


def _position():
    return lax.axis_index("x"), lax.axis_index("y"), lax.axis_index("c")


def _flat(px, py, pc):
    return 4 * px + 2 * py + pc


def _all_gather(shard, *, name):
    def body(x_ref, out_ref, token, send_sems, recv_sems, local_sem):
        token[...] = jnp.zeros_like(token)
        x, y, c = _position()
        me, sibling = (x, y, c), (x, y, 1 - c)
        chips = [(1 - x, y), (x, 1 - y), (1 - x, 1 - y)]

        def copy(k, block, to, src=None):
            slot = out_ref.at[_flat(*block)]
            return pltpu.make_async_remote_copy(
                src_ref=slot if src is None else src, dst_ref=slot, send_sem=send_sems.at[k],
                recv_sem=recv_sems.at[k], device_id=to, device_id_type=MESH)

        mine = pltpu.make_async_copy(x_ref, out_ref.at[_flat(*me)], local_sem)
        mine.start()
        first = [copy(0, me, sibling, src=x_ref)]
        first += [copy(1 + j, me, (*chip, c), src=x_ref) for j, chip in enumerate(chips)]
        for cp in first:
            cp.start()
        passed = [copy(4 + j, (*chip, c), sibling) for j, chip in enumerate(chips)]
        for j, chip in enumerate(chips):
            copy(1 + j, (*chip, c), me).wait_recv()
            passed[j].start()
        copy(0, sibling, me).wait_recv()
        for j, chip in enumerate(chips):
            copy(4 + j, (*chip, 1 - c), me).wait_recv()
        for cp in first + passed:
            cp.wait_send()
        mine.wait()

    return pl.pallas_call(
        body, name=name,
        out_shape=(jax.ShapeDtypeStruct((N_DEV,) + shard.shape, shard.dtype),
                   jax.ShapeDtypeStruct((SUBLANES, LANES), F32)),
        in_specs=[pl.BlockSpec(memory_space=pl.ANY)],
        out_specs=(pl.BlockSpec(memory_space=pl.ANY), pl.BlockSpec(memory_space=pltpu.VMEM)),
        scratch_shapes=[pltpu.SemaphoreType.DMA((7,)), pltpu.SemaphoreType.DMA((7,)), pltpu.SemaphoreType.DMA(())],
    )(shard)


_HBM = pl.BlockSpec(memory_space=pltpu.HBM)
_SEM = pl.BlockSpec(memory_space=pltpu.SEMAPHORE)
_EFFECT = pltpu.SideEffectType.DATAFLOW_SIDE_EFFECTING


def _copy_ends(src_ref, land_ref, mode, me, to):
    if mode == "gather_slot":
        return src_ref, land_ref.at[me]
    if mode == "gather_cols":
        w = src_ref.shape[1]
        return src_ref, land_ref.at[:, pl.ds(pl.multiple_of(me * w, LANES), w)]
    if mode == "scatter_slot":
        return src_ref.at[to], land_ref.at[me]
    w = land_ref.shape[2]
    return src_ref.at[:, pl.ds(pl.multiple_of(to * w, LANES), w)], land_ref.at[me]


BF16_ROWS = 16


def _land_shape(src, mode):
    if mode == "gather_slot":
        return (N_DEV,) + src.shape
    if mode == "gather_cols":
        return (src.shape[0], N_DEV * src.shape[1])
    if mode == "scatter_slot":
        return src.shape
    return (N_DEV, src.shape[0], src.shape[1] // N_DEV)


OTHER_CHIPS = (2, 4, 6)


def _n_copies(mode):
    return {"gather_chip": 1 + len(OTHER_CHIPS), "forward": len(OTHER_CHIPS)}.get(mode, N_DEV - 1)


def _exchange_copies(src_ref, land_ref, send_sems, recv_sems, mode):
    x, y, c = _position()
    me = _flat(x, y, c)
    peer_of = lambda k: (x ^ ((k >> 2) & 1), y ^ ((k >> 1) & 1), c ^ (k & 1))
    if mode == "forward":
        slots = [land_ref.at[_flat(*peer_of(k))] for k in OTHER_CHIPS]
        plan = [(slot, slot, peer_of(1)) for slot in slots]
    elif mode == "gather_chip":
        plan = [(*_copy_ends(src_ref, land_ref, "gather_slot", me, _flat(*peer_of(k))), peer_of(k))
                for k in (1,) + OTHER_CHIPS]
    else:
        plan = [(*_copy_ends(src_ref, land_ref, mode, me, _flat(*peer_of(k))), peer_of(k)) for k in range(1, N_DEV)]
    return [pltpu.make_async_remote_copy(src_ref=src, dst_ref=dst, send_sem=send_sems.at[i], recv_sem=recv_sems.at[i],
                                         device_id=peer, device_id_type=MESH)
            for i, (src, dst, peer) in enumerate(plan)]


def _place_own(src, mode, dev, *, name):
    rows = src.shape[1] if mode == "scatter_slot" else src.shape[0]
    tr = _tile(rows, 512, BF16_ROWS)
    land = _land_shape(src, mode)
    width = land[-1] if mode.startswith("scatter") else src.shape[1]
    slot = pl.BlockSpec((1, tr, width), lambda i, d: (d[0], i, 0))
    cols = pl.BlockSpec((tr, width), lambda i, d: (i, d[0]))
    whole = pl.BlockSpec((tr, width), lambda i, d: (i, 0))
    in_spec, out_spec = {"gather_slot": (whole, slot), "gather_cols": (whole, cols), "scatter_slot": (slot, slot),
                         "scatter_cols": (cols, slot)}[mode]

    def body(dev_ref, src_ref, land_ref):
        land_ref[...] = src_ref[...].reshape(land_ref.shape)

    return pl.pallas_call(
        body, name=name, out_shape=jax.ShapeDtypeStruct(land, src.dtype),
        grid_spec=pltpu.PrefetchScalarGridSpec(num_scalar_prefetch=1, grid=(rows // tr,), in_specs=[in_spec],
                                               out_specs=out_spec),
        compiler_params=_cp("parallel"),
    )(dev, src)


def _exchange_start(src, mode, dev, *, name, land=None, after=None):
    if land is None:
        land = _place_own(src, "gather_slot" if mode == "gather_chip" else mode, dev, name=name + "_own")
    n_copies = _n_copies(mode)

    def body(*refs):
        src_ref, land_ref = refs[:2]
        send_sems, recv_sems, _, _, token = refs[-5:]
        for cp in _exchange_copies(src_ref, land_ref, send_sems, recv_sems, mode):
            cp.start()
        token[...] = jnp.zeros_like(token)

    hbm = pltpu.with_memory_space_constraint
    *handle, token = pl.pallas_call(
        body, name=name,
        out_shape=(pltpu.SemaphoreType.DMA((n_copies,)), pltpu.SemaphoreType.DMA((n_copies,)),
                   pltpu.HBM(src.shape, src.dtype), pltpu.HBM(land.shape, land.dtype),
                   jax.ShapeDtypeStruct((SUBLANES, LANES), F32)),
        in_specs=(_HBM, _HBM) + ((pl.BlockSpec(memory_space=pl.ANY),) if after is not None else ()),
        out_specs=(_SEM, _SEM, _HBM, _HBM, pl.BlockSpec(memory_space=pltpu.VMEM)),
        input_output_aliases={0: 2, 1: 3}, compiler_params=pltpu.CompilerParams(has_side_effects=_EFFECT),
    )(hbm(src, pltpu.HBM), hbm(land, pltpu.HBM), *(() if after is None else (after,)))
    return (tuple(handle), mode), token


def _exchange_wait(pending, after, *, name):
    (send_sems, recv_sems, src_thru, land_thru), mode = pending

    def body(src_ref, land_ref, send_sems, recv_sems, after_ref, src_dead, got_ref):
        for cp in _exchange_copies(src_ref, land_ref, send_sems, recv_sems, mode):
            cp.wait_send()
            cp.wait_recv()

    return pl.pallas_call(
        body, name=name, out_shape=(pltpu.HBM(src_thru.shape, src_thru.dtype), pltpu.HBM(land_thru.shape, land_thru.dtype)),
        in_specs=(_HBM, _HBM, _SEM, _SEM, pl.BlockSpec(memory_space=pl.ANY)), out_specs=(_HBM, _HBM),
        input_output_aliases={0: 0, 1: 1}, compiler_params=pltpu.CompilerParams(has_side_effects=_EFFECT),
    )(src_thru, land_thru, send_sems, recv_sems, after)[1]


def _after(x, token):
    return x + token[0, 0].astype(x.dtype)


def _touch(*arrays, name):
    def body(*refs):
        refs[-1][...] = jnp.zeros_like(refs[-1])

    return pl.pallas_call(
        body, name=name, out_shape=jax.ShapeDtypeStruct((SUBLANES, LANES), F32),
        in_specs=[pl.BlockSpec(memory_space=pl.ANY)] * len(arrays), out_specs=pl.BlockSpec(memory_space=pltpu.VMEM),
    )(*arrays)


def _pad_cols(a, mult):
    pad = -a.shape[1] % mult
    return jnp.pad(a, ((0, 0), (0, pad))) if pad else a


def _pack(arrs, cols):
    flat = jnp.concatenate([a.reshape(-1).astype(F32) for a in arrs])
    sizes = [int(a.size) for a in arrs]
    flat = jnp.pad(flat, (0, -flat.shape[0] % (N_DEV * SUBLANES * cols)))
    return flat.reshape(-1, cols), sizes


def _unpack(flat2d, sizes, shapes):
    flat = flat2d.reshape(-1)
    out, o = [], 0
    for n, s in zip(sizes, shapes):
        out.append(flat[o:o + n].reshape(s))
        o += n
    return out


PACK_COLS = SUBLANES * LANES


def kernel(x, norm_mix_w, w_in, conv_a_w, conv_a_b, dt_bias, a_log, d_a, norm_a_w, w_proj_a, s5_lam_re, s5_lam_im, s5_log_dt, s5_b_re, s5_b_im, s5_c_re, s5_c_im, s5_d, w_s5_glu, w_out, norm_ffn_w, w_up, conv_ffn_w, conv_ffn_b, w_down, norm_final_w, loss_target, m_norm_mix_w, m_w_in, m_conv_a_w, m_conv_a_b, m_dt_bias, m_a_log, m_d_a, m_norm_a_w, m_w_proj_a, m_s5_lam_re, m_s5_lam_im, m_s5_log_dt, m_s5_b_re, m_s5_b_im, m_s5_c_re, m_s5_c_im, m_s5_d, m_w_s5_glu, m_w_out, m_norm_ffn_w, m_w_up, m_conv_ffn_w, m_conv_ffn_b, m_w_down, m_norm_final_w, v_norm_mix_w, v_w_in, v_conv_a_w, v_conv_a_b, v_dt_bias, v_a_log, v_d_a, v_norm_a_w, v_w_proj_a, v_s5_lam_re, v_s5_lam_im, v_s5_log_dt, v_s5_b_re, v_s5_b_im, v_s5_c_re, v_s5_c_im, v_s5_d, v_w_s5_glu, v_w_out, v_norm_ffn_w, v_w_up, v_conv_ffn_w, v_conv_ffn_b, v_w_down, v_norm_final_w):
    weights = dict(norm_mix_w=norm_mix_w, w_in=w_in, conv_a_w=conv_a_w, conv_a_b=conv_a_b, dt_bias=dt_bias, a_log=a_log, d_a=d_a, norm_a_w=norm_a_w, w_proj_a=w_proj_a, s5_lam_re=s5_lam_re, s5_lam_im=s5_lam_im, s5_log_dt=s5_log_dt, s5_b_re=s5_b_re, s5_b_im=s5_b_im, s5_c_re=s5_c_re, s5_c_im=s5_c_im, s5_d=s5_d, w_s5_glu=w_s5_glu, w_out=w_out, norm_ffn_w=norm_ffn_w, w_up=w_up, conv_ffn_w=conv_ffn_w, conv_ffn_b=conv_ffn_b, w_down=w_down, norm_final_w=norm_final_w)
    moms = dict(norm_mix_w=m_norm_mix_w, w_in=m_w_in, conv_a_w=m_conv_a_w, conv_a_b=m_conv_a_b, dt_bias=m_dt_bias, a_log=m_a_log, d_a=m_d_a, norm_a_w=m_norm_a_w, w_proj_a=m_w_proj_a, s5_lam_re=m_s5_lam_re, s5_lam_im=m_s5_lam_im, s5_log_dt=m_s5_log_dt, s5_b_re=m_s5_b_re, s5_b_im=m_s5_b_im, s5_c_re=m_s5_c_re, s5_c_im=m_s5_c_im, s5_d=m_s5_d, w_s5_glu=m_w_s5_glu, w_out=m_w_out, norm_ffn_w=m_norm_ffn_w, w_up=m_w_up, conv_ffn_w=m_conv_ffn_w, conv_ffn_b=m_conv_ffn_b, w_down=m_w_down, norm_final_w=m_norm_final_w)
    vars_ = dict(norm_mix_w=v_norm_mix_w, w_in=v_w_in, conv_a_w=v_conv_a_w, conv_a_b=v_conv_a_b, dt_bias=v_dt_bias, a_log=v_a_log, d_a=v_d_a, norm_a_w=v_norm_a_w, w_proj_a=v_w_proj_a, s5_lam_re=v_s5_lam_re, s5_lam_im=v_s5_lam_im, s5_log_dt=v_s5_log_dt, s5_b_re=v_s5_b_re, s5_b_im=v_s5_b_im, s5_c_re=v_s5_c_re, s5_c_im=v_s5_c_im, s5_d=v_s5_d, w_s5_glu=v_w_s5_glu, w_out=v_w_out, norm_ffn_w=v_norm_ffn_w, w_up=v_w_up, conv_ffn_w=v_conv_ffn_w, conv_ffn_b=v_conv_ffn_b, w_down=v_w_down, norm_final_w=v_norm_final_w)
    names = list(weights)
    col_sharded = ("w_in", "w_s5_glu", "w_up")
    row_sharded = ("w_proj_a", "w_out", "w_down")
    conv_sharded = ("conv_a_w", "conv_ffn_w")
    replicated = [n for n in names if n not in col_sharded + row_sharded + conv_sharded]

    t, d = x.shape[1:]
    x2, tgt = x.reshape(t, d), loss_target.reshape(t, d)
    nh = dt_bias.shape[-1]
    d_inner = norm_a_w.shape[-1]
    conv_dim = conv_a_b.shape[-1]
    gn = (conv_dim - d_inner) // 2
    ng = gn // D_STATE
    nr = nh // ng
    rp = d_inner // ng
    d_s5 = s5_d.shape[-1]
    gs, ps = s5_lam_re.shape[1:]
    cs = d_s5 // gs
    n_oct = gs // 8
    assert (gs * ps) % SCAN_COLS == 0 and 8 * cs == LANES and gs % 8 == 0
    d_ff = w_down.shape[1] * N_DEV
    dev = _flat(*_position())
    dev1 = dev.reshape(1).astype(jnp.int32)

    ka, kf = conv_a_w.shape[1], conv_ffn_w.shape[1]
    taps = jnp.concatenate([conv_a_w[0].reshape(1, -1), conv_ffn_w[0].reshape(1, -1)], axis=1)
    taps, taps_done = _all_gather(taps, name="ag_conv_taps")
    taps = taps[:, 0]

    def by_cols(n):
        return n in ("w_s5_glu", "w_up") and weights[n].shape[2] % LANES == 0

    chip_stage, t1 = _exchange_start(_after(w_in[0], taps_done).astype(BF16), "gather_chip", dev1, name="ag_w_in")
    w1 = norm_mix_w.reshape(1, 1, d) + t1[0, 0]
    hn1, = _blocked_fwd(_rms_fn, [x2], [w1], [(d, BF16)], tb=256, name="rms1")
    others = ("w_proj_a", "w_s5_glu", "w_out", "w_up", "w_down")
    gather_mode = {n: "gather_cols" if by_cols(n) else "gather_slot" for n in others}
    shards = {n: _after(weights[n][0], t1).astype(BF16) for n in others}
    lands = {n: _place_own(shards[n], gather_mode[n], dev1, name="ag_" + n + "_own") for n in others}
    w_in_2d = [a.reshape(a.shape[-2:]) for a in (w_in, m_w_in, v_w_in)]

    lam_re3, lam_im3 = s5_lam_re[0][:, None, :], s5_lam_im[0][:, None, :]
    logdt3 = _after(s5_log_dt[0][:, None, None], t1)
    bt_re, bt_im = jnp.transpose(s5_b_re[0], (0, 2, 1)), jnp.transpose(s5_b_im[0], (0, 2, 1))
    lb_re3, lb_im3, bb_re, bb_im = _s5_params(lam_re3, lam_im3, logdt3, bt_re, bt_im, name="s5_params")
    eye = jnp.eye(8, dtype=F32)

    def diag_b(bt):
        return (bt.reshape(n_oct, 8, cs, 1, ps) * eye[None, :, None, :, None]).reshape(n_oct, 8 * cs, 8 * ps)

    def undiag_b(blk):
        return (blk.reshape(n_oct, 8, cs, 8, ps) * eye[None, :, None, :, None]).sum(axis=3).reshape(gs, cs, ps)

    def diag_c(cm):
        ct = jnp.transpose(cm.reshape(n_oct, 8, cs, ps), (0, 1, 3, 2))
        return (ct[:, :, :, None, :] * eye[None, :, None, :, None]).reshape(n_oct, 8 * ps, 8 * cs)

    def undiag_c(blk):
        ct = (blk.reshape(n_oct, 8, ps, 8, cs) * eye[None, :, None, :, None]).sum(axis=3)
        return jnp.transpose(ct, (0, 1, 3, 2)).reshape(gs, cs, ps)

    b_blk_re, b_blk_im = diag_b(bb_re), diag_b(bb_im)
    c_blk_re, c_blk_imn = diag_c(s5_c_re[0]), diag_c(-s5_c_im[0])
    d3 = s5_d.reshape(n_oct, 1, LANES)
    lb_re, lb_im = lb_re3.reshape(1, gs * ps), lb_im3.reshape(1, gs * ps)

    ready = _touch(hn1, *lands.values(), *w_in_2d, b_blk_re, b_blk_im, c_blk_re, c_blk_imn, lb_re, lb_im,
                   name="ready_startup")
    forward_stage, t2 = _exchange_start(ready, "forward", dev1, land=_exchange_wait(chip_stage, ready, name="agw_w_in"),
                                        name="fw_w_in")
    w_in_blocks = _exchange_wait(forward_stage, t2, name="fww_w_in")
    pending, started = {}, t2
    for n in others:
        pending[n], token = _exchange_start(shards[n], gather_mode[n], dev1, land=lands[n], after=w_in_blocks,
                                            name="ag_" + n)
        started = started + token

    def gathered(n, after):
        g = _exchange_wait(pending[n], after, name="agw_" + n)
        if by_cols(n):
            return g
        if n in row_sharded:
            return g.reshape(-1, g.shape[2])
        return jnp.transpose(g, (1, 0, 2)).reshape(g.shape[1], -1)

    seg_sizes = dict(z=d_inner, xs=d_inner, bm=gn, cm=gn, dt=nh, u=d_s5, ga=d, gb=d)
    seg_names = tuple(seg_sizes)
    pieces, seg_at = _w_in_pieces(seg_sizes, ("z", "xs", "ga", "gb", "bm", "cm", "u", "dt"), w_in.shape[2])
    na = ka * conv_a_w.shape[2]
    cw_a = jnp.transpose(taps[:, :na].reshape(N_DEV, ka, -1), (1, 0, 2)).reshape(ka, conv_dim)
    cw_f = jnp.transpose(taps[:, na:].reshape(N_DEV, kf, -1), (1, 0, 2)).reshape(kf, 2 * d_ff)
    cb_a, cb_f = conv_a_b, conv_ffn_b
    a_cols = {"xs": slice(0, d_inner), "bm": slice(d_inner, d_inner + gn), "cm": slice(d_inner + gn, conv_dim)}

    w_in_p = _w_in_pack(w_in_blocks, pieces, seg_at, seg_sizes, started, name="w_in_pack")
    pre = {sn: _mm(hn1, w_in_p, b_win=seg_at[sn], name="in_" + sn) for sn in seg_names}
    act_a = {sn: _conv_fwd(_comb_silu, [pre[sn]], [cw_a[:, a_cols[sn]]], [cb_a[:, a_cols[sn]]], out_dtype=F32,
                           name="conv_a_" + sn) for sn in a_cols}
    dtr3 = jnp.transpose(pre["dt"][:, :nh].reshape(t, ng, nr), (1, 0, 2))
    dtb3, alog3, dsk3 = (p.reshape(ng, 1, nr) for p in (dt_bias, a_log, d_a))
    nw3 = norm_a_w.reshape(ng, 1, rp)
    yn, hsave = _ssd_fwd(act_a["xs"], act_a["bm"], act_a["cm"], pre["z"], dtr3, dtb3, alog3, dsk3, nw3, name="ssd")
    w_proj = gathered("w_proj_a", yn)
    y_a = _mm(yn, w_proj, name="proj_a")

    u = pre["u"]
    bu_re, bu_im = _blocked_fwd(_s5_bu_fn, [u], [b_blk_re, b_blk_im], [(gs * ps, F32)] * 2, nj=n_oct, tb=512,
                                name="s5_bu")
    s_re, s_im = _s5_scan_fwd(bu_re, bu_im, lb_re, lb_im, name="s5_scan")
    yb, = _blocked_fwd(_s5_out_fn, [s_re, s_im, u], [c_blk_re, c_blk_imn, d3], [(d_s5, BF16)], nj=n_oct, tb=512,
                       name="s5_out")
    w_glu = gathered("w_s5_glu", yb)
    glu_v = _mm(yb, w_glu, b_win=(0, d), name="glu_v")
    glu_g = _mm(yb, w_glu, b_win=(d, d), name="glu_g")
    merged, = _blocked_fwd(_merge_fn, [glu_v, glu_g, pre["ga"], pre["gb"], y_a], [], [(d, BF16)], tb=256,
                           name="merge")
    w_o = gathered("w_out", merged)
    h1 = _mm(merged, w_o, acc=x2, name="out_proj")
    w2 = norm_ffn_w.reshape(1, 1, d)
    hn2, = _blocked_fwd(_rms_fn, [h1], [w2], [(d, BF16)], tb=256, name="rms2")
    w_u = gathered("w_up", hn2)
    up_g = _mm(hn2, w_u, b_win=(0, d_ff), name="up_g")
    up_v = _mm(hn2, w_u, b_win=(d_ff, d_ff), name="up_v")
    f_w = [cw_f[:, :d_ff], cw_f[:, d_ff:]]
    f_b = [cb_f[:, :d_ff], cb_f[:, d_ff:]]
    act = _conv_fwd(_comb_glu, [up_g, up_v], f_w, f_b, out_dtype=BF16, name="conv_ffn")
    w_dn = gathered("w_down", act)
    h2 = _mm(act, w_dn, acc=h1, name="down")
    loss_tile, dh2, dh2_b, g_final = _loss_head(h2, norm_final_w.reshape(1, d), tgt, name="loss_head")

    grads, scattering = {}, {}

    def scatter_start(n, g):
        if by_cols(n):
            src, mode = g, "scatter_cols"
        elif n in row_sharded:
            src, mode = g.reshape(N_DEV, -1, g.shape[1]), "scatter_slot"
        elif n == "w_in":
            src, mode = g, "scatter_slot"
        else:
            src, mode = jnp.transpose(g.reshape(g.shape[0], N_DEV, -1), (1, 0, 2)), "scatter_slot"
        scattering[n], token = _exchange_start(src, mode, dev1, name="rs_" + n)
        return token

    d_act = _mm(dh2_b, w_dn, tb=True, name="d_act")
    g_down = _mm(act, dh2_b, ta=True, out_dtype=BF16, name="g_w_down")
    tok = scatter_start("w_down", g_down)
    (dup_g, dwf_g, dbf_g), (dup_v, dwf_v, dbf_v) = _conv_bwd(
        _comb_glu, [up_g, up_v], f_w, [_after(f_b[0], tok), f_b[1]], d_act, dx_dtype=BF16, name="conv_ffn_bwd")
    dhn2 = _mm(dup_g, w_u, tb=True, b_win=(0, d_ff), name="d_hn2_g")
    dhn2 = _mm(dup_v, w_u, tb=True, b_win=(d_ff, d_ff), acc=dhn2, name="d_hn2_v")
    g_up = _mm(hn2, dup_g, ta=True, into=(lax.empty((d, 2 * d_ff), BF16), 0), name="g_w_up_g")
    g_up = _mm(hn2, dup_v, ta=True, into=(g_up, d_ff), name="g_w_up_v")
    tok = scatter_start("w_up", g_up)
    dh1, dh1_b, g_w2 = _blocked_bwd(_rms_fn, [h1], [_after(w2, tok)], [dhn2], [(F32, BF16)], adds={0: dh2}, tb=256,
                                    name="rms2_bwd")
    d_merged = _mm(dh1_b, w_o, tb=True, name="d_merged")
    g_out = _mm(merged, dh1_b, ta=True, out_dtype=BF16, name="g_w_out")
    tok = scatter_start("w_out", g_out)
    dglu_v, dglu_g, dga, dgb, dy_a = _blocked_bwd(
        _merge_fn, [glu_v, glu_g, pre["ga"], pre["gb"], y_a], [], [d_merged], [BF16] * 5, tb=128, name="merge_bwd")
    dyb = _mm(dglu_v, w_glu, tb=True, b_win=(0, d), name="d_yb_v")
    dyb = _mm(dglu_g, w_glu, tb=True, b_win=(d, d), acc=dyb, name="d_yb_g")
    g_glu = _mm(yb, dglu_v, ta=True, into=(lax.empty((d_s5, 2 * d), BF16), 0), name="g_w_glu_v")
    g_glu = _mm(yb, dglu_g, ta=True, into=(g_glu, d), name="g_w_glu_g")
    tok = tok + scatter_start("w_s5_glu", g_glu)
    ds_re, ds_im, du_skip, dc_blk_re, dc_blk_imn, dd3 = _blocked_bwd(
        _s5_out_fn, [s_re, s_im, u], [c_blk_re, c_blk_imn, _after(d3, tok)], [dyb], [F32, F32, F32], nj=n_oct, tb=512,
        name="s5_out_bwd")
    dbu_re, dbu_im, dlb_re, dlb_im = _s5_scan_bwd(s_re, s_im, ds_re, ds_im, lb_re, lb_im, name="s5_scan_bwd")
    du, db_blk_re, db_blk_im = _blocked_bwd(
        _s5_bu_fn, [u], [b_blk_re, b_blk_im], [dbu_re, dbu_im], [BF16], adds={0: du_skip}, nj=n_oct, tb=512,
        name="s5_bu_bwd")
    g_lre, g_lim, g_ldt, g_bt_re, g_bt_im = _s5_params(
        lam_re3, lam_im3, logdt3, bt_re, bt_im,
        cts=(dlb_re.reshape(gs, 1, ps), dlb_im.reshape(gs, 1, ps), undiag_b(db_blk_re), undiag_b(db_blk_im)),
        name="s5_params_bwd")
    grads["s5_lam_re"], grads["s5_lam_im"] = g_lre.reshape(s5_lam_re.shape), g_lim.reshape(s5_lam_im.shape)
    grads["s5_log_dt"] = g_ldt.reshape(s5_log_dt.shape)
    grads["s5_b_re"] = jnp.transpose(g_bt_re, (0, 2, 1)).reshape(s5_b_re.shape)
    grads["s5_b_im"] = jnp.transpose(g_bt_im, (0, 2, 1)).reshape(s5_b_im.shape)
    grads["s5_c_re"] = undiag_c(dc_blk_re).reshape(s5_c_re.shape)
    grads["s5_c_im"] = -undiag_c(dc_blk_imn).reshape(s5_c_im.shape)
    grads["s5_d"] = dd3.reshape(s5_d.shape)

    dyn = _mm(dy_a, w_proj, tb=True, name="d_yn")
    g_proj = _mm(yn, dy_a, ta=True, out_dtype=BF16, name="g_w_proj_a")
    tok = scatter_start("w_proj_a", g_proj)
    dxs, dbm, dcm, dz, ddtr3, g_dtb, g_alog, g_dsk, g_nw = _ssd_bwd(
        act_a["xs"], act_a["bm"], act_a["cm"], pre["z"], dtr3, hsave, dtb3, alog3, dsk3, _after(nw3, tok), dyn,
        name="ssd_bwd")
    grads["dt_bias"], grads["a_log"], grads["d_a"] = (g.reshape(1, nh) for g in (g_dtb, g_alog, g_dsk))
    grads["norm_a_w"] = g_nw.reshape(1, d_inner)
    dpre = {"z": dz, "u": du, "ga": dga, "gb": dgb}
    dcw, dcb = {}, {}
    for sn, dact in (("xs", dxs), ("bm", dbm), ("cm", dcm)):
        (dpre[sn], dcw[sn], dcb[sn]), = _conv_bwd(
            _comb_silu, [pre[sn]], [cw_a[:, a_cols[sn]]], [cb_a[:, a_cols[sn]]], dact, dx_dtype=BF16,
            name="conv_a_bwd_" + sn)
    dpre["dt"] = _pad_cols(jnp.transpose(ddtr3, (1, 0, 2)).reshape(t, nh), LANES).astype(BF16)
    g_in = _w_in_unpack({sn: _mm(hn1, dpre[sn], ta=True, name="g_w_in_" + sn) for sn in seg_names}, pieces,
                        w_in.shape[2], name="w_in_unpack")
    tok = scatter_start("w_in", g_in)
    dhn1 = _mm(_after(dpre["dt"], tok), w_in_p, tb=True, b_win=seg_at["dt"], name="d_hn1_dt")
    for sn in seg_names:
        if sn != "dt":
            dhn1 = _mm(dpre[sn], w_in_p, tb=True, b_win=seg_at[sn], acc=dhn1, name="d_hn1_" + sn)
    dx, g_w1 = _blocked_bwd(_rms_fn, [x2], [w1], [dhn1], [F32], adds={0: dh1}, tb=256, name="rms1_bwd")

    grads["norm_mix_w"], grads["norm_ffn_w"] = g_w1.reshape(1, d), g_w2.reshape(1, d)
    grads["norm_final_w"] = g_final.reshape(d)
    grads["conv_a_b"] = jnp.concatenate([dcb["xs"], dcb["bm"], dcb["cm"]], axis=1)
    grads["conv_ffn_b"] = jnp.concatenate([dbf_g, dbf_v], axis=1)
    g_cw_a = jnp.concatenate([dcw["xs"], dcw["bm"], dcw["cm"]], axis=1)
    g_cw_f = jnp.concatenate([dwf_g, dwf_v], axis=1)

    small = [grads[n] for n in replicated] + [g_cw_a, g_cw_f, loss_tile[:1, :1]]
    packed, sizes = _pack(small, PACK_COLS)
    small_scatter, _ = _exchange_start(packed.reshape(N_DEV, -1, PACK_COLS), "scatter_slot", dev1, name="rs_small")
    eighth = _sum_parts(_exchange_wait(small_scatter, packed, name="rsw_small"), name="sum_small_grads")
    summed = _all_gather(eighth, name="ag_small_grads")[0].reshape(-1, PACK_COLS)
    *rep_sums, s_cw_a, s_cw_f, loss = _unpack(summed, sizes, [a.shape for a in small])
    for n, g in zip(replicated, rep_sums):
        grads[n] = g
    wa, wf = conv_a_w.shape[2], conv_ffn_w.shape[2]
    grads["conv_a_w"] = lax.dynamic_slice_in_dim(s_cw_a, dev * wa, wa, axis=1)[None]
    grads["conv_ffn_w"] = lax.dynamic_slice_in_dim(s_cw_f, dev * wf, wf, axis=1)[None]

    delta, new_m, new_v = {}, {}, {}
    done = dx
    for n in ("w_down", "w_up", "w_out", "w_s5_glu", "w_proj_a", "w_in"):
        shape = weights[n].shape
        two_d = lambda a: a.reshape(shape[-2], shape[-1])
        w2, m2, v2 = two_d(weights[n]), two_d(moms[n]), two_d(vars_[n])
        ready = _touch(done, w2, m2, v2, *((packed,) if n == "w_in" else ()), name="ready_" + n)
        land = _exchange_wait(scattering[n], ready, name="rsw_" + n)
        g = _sum_parts(land, name="rs_sum_" + n)
        grads[n] = g.reshape(shape)
        dl, nm, nv = _adamw(w2, g, m2, v2, name="adamw_" + n)
        delta[n], new_m[n], new_v[n] = dl.reshape(shape), nm.reshape(shape), nv.reshape(shape)
        done = dl
    for n in replicated + list(conv_sharded):
        shape = weights[n].shape
        two_d = lambda a: a.reshape(-1, shape[-1])
        dl, nm, nv = _adamw(two_d(weights[n]), two_d(grads[n]), two_d(moms[n]), two_d(vars_[n]), name="adamw_" + n)
        delta[n], new_m[n], new_v[n] = dl.reshape(shape), nm.reshape(shape), nv.reshape(shape)

    return (loss.reshape(()), dx.reshape(x.shape), *[grads[n] for n in names], *[delta[n] for n in names],
            *[new_m[n] for n in names], *[new_v[n] for n in names])
```

```python
import functools

import jax
import jax.numpy as jnp
from jax import lax
from jax.experimental import pallas as pl
from jax.experimental.pallas import tpu as pltpu

F32 = jnp.float32
BF16 = jnp.bfloat16
HIGHEST = lax.Precision.HIGHEST
MESH = pl.DeviceIdType.MESH

EPS = 1e-6
EIG_MAX = -1e-4
D_STATE = 128
CHUNK = 256
ADAM_LR = 0.001
ADAM_B1 = 0.9
ADAM_B2 = 0.999
ADAM_EPS = 1e-08
ADAM_WD = 0.01
ADAM_STEP = 10
N_DEV = 8
LANES = 128
SUBLANES = 8
VMEM_LIMIT = 56 * 1024 * 1024
MM_MAX_K = 4096


def _cp(*sem):
    return pltpu.CompilerParams(dimension_semantics=sem, vmem_limit_bytes=VMEM_LIMIT)


def _tile(dim, pref, unit=LANES):
    if dim <= unit:
        return dim
    t = (min(pref, dim) // unit) * unit
    while dim % t:
        t -= unit
    return t


_DIMS = {"nn": (((1,), (0,)), ((), ())), "nt": (((1,), (1,)), ((), ())), "tn": (((0,), (0,)), ((), ()))}


def _dot(a, b, kind):
    return lax.dot_general(a.astype(BF16), b.astype(BF16), _DIMS[kind], preferred_element_type=F32)


@functools.partial(jax.custom_vjp, nondiff_argnums=(2,))
def _bdot(a, b, kind):
    return _dot(a, b, kind)


def _bdot_fwd(a, b, kind):
    return _dot(a, b, kind), (a, b)


def _bdot_bwd(kind, res, g):
    a, b = res
    if kind == "nn":
        return _dot(g, b, "nt"), _dot(a, g, "tn")
    if kind == "nt":
        return _dot(g, b, "nn"), _dot(g, a, "tn")
    return _dot(b, g, "nt"), _dot(a, g, "nn")


_bdot.defvjp(_bdot_fwd, _bdot_bwd)


def _mm(a, b, *, ta=False, tb=False, acc=None, out_dtype=F32, name, b_win=None, into=None):
    assert not (ta and tb)
    m, k = (a.shape[1], a.shape[0]) if ta else a.shape
    b_off, b_size = b_win or (0, b.shape[1])
    n = b.shape[0] if tb else b_size
    assert (b_size if tb else b.shape[0]) == k, (a.shape, b.shape, ta, tb, b_win)
    o_off = into[1] if into else 0
    nk = -(-k // MM_MAX_K)
    while k % nk or (k // nk) % LANES or (tb and b_off % (k // nk)):
        nk += 1
    tk = k // nk
    tm, tn = _tile(m, 1024), _tile(n, 1024)
    while o_off % tn or (not tb and b_off % tn):
        tn = _tile(n, tn - LANES)
    kind = "tn" if ta else ("nt" if tb else "nn")
    a_spec = pl.BlockSpec((tk, tm), lambda i, j, l: (l, i)) if ta else pl.BlockSpec((tm, tk), lambda i, j, l: (i, l))
    if tb:
        b_spec = pl.BlockSpec((tn, tk), lambda i, j, l: (j, l + b_off // tk))
    else:
        b_spec = pl.BlockSpec((tk, tn), lambda i, j, l: (l, j + b_off // tn))
    c_spec = pl.BlockSpec((tm, tn), lambda i, j, l: (i, j))
    o_spec = pl.BlockSpec((tm, tn), lambda i, j, l: (i, j + o_off // tn))
    has_acc = acc is not None

    def body(*refs):
        a_ref, b_ref = refs[:2]
        c_ref = refs[2] if has_acc else None
        o_ref = refs[2 + has_acc + (into is not None)]
        if nk == 1:
            res = _dot(a_ref[...], b_ref[...], kind)
            if has_acc:
                res = res + c_ref[...].astype(F32)
            o_ref[...] = res.astype(o_ref.dtype)
            return
        acc_ref = refs[-1]
        l = pl.program_id(2)

        @pl.when(l == 0)
        def _():
            if has_acc:
                acc_ref[...] = c_ref[...].astype(F32)
            else:
                acc_ref[...] = jnp.zeros_like(acc_ref)

        acc_ref[...] += _dot(a_ref[...], b_ref[...], kind)

        @pl.when(l == nk - 1)
        def _():
            o_ref[...] = acc_ref[...].astype(o_ref.dtype)

    ins = [a, b] + ([acc] if has_acc else []) + ([into[0]] if into else [])
    in_specs = [a_spec, b_spec] + ([c_spec] if has_acc else []) + ([pl.BlockSpec(memory_space=pl.ANY)] if into else [])
    out_shape = jax.ShapeDtypeStruct(into[0].shape, into[0].dtype) if into else jax.ShapeDtypeStruct((m, n), out_dtype)
    return pl.pallas_call(
        body, name=name, grid=(m // tm, n // tn, nk), in_specs=in_specs, out_specs=o_spec, out_shape=out_shape,
        input_output_aliases={len(ins) - 1: 0} if into else {},
        scratch_shapes=[pltpu.VMEM((tm, tn), F32)] if nk > 1 else [],
        compiler_params=_cp("parallel", "parallel", "arbitrary"),
    )(*ins)


def _w_in_pieces(seg_sizes, seg_order, n_blk):
    layout, o = {}, 0
    for sn in seg_order:
        width = -(-seg_sizes[sn] // LANES) * LANES
        layout[sn] = (o, width)
        o += width
    pieces, start = [], 0
    for sn, sz in seg_sizes.items():
        lo = start
        while lo < start + sz:
            blk = lo // n_blk
            hi = min(start + sz, (blk + 1) * n_blk)
            pieces.append((blk, lo - blk * n_blk, sn, lo - start, layout[sn][0] + lo - start, hi - lo))
            lo = hi
        start += sz
    return pieces, layout


def _w_in_pack(gathered, pieces, layout, seg_sizes, after, *, name, tr=256):
    _, k, n_blk = gathered.shape
    n_pad = sum(w for _, w in layout.values())

    def body(g_ref, after_ref, o_ref):
        for sn, (off, width) in layout.items():
            if width != seg_sizes[sn]:
                o_ref[:, pl.ds(off + seg_sizes[sn], width - seg_sizes[sn])] = jnp.zeros(
                    (tr, width - seg_sizes[sn]), o_ref.dtype)
        for blk, src, _, _, dst, width in pieces:
            o_ref[:, pl.ds(dst, width)] = g_ref[blk, :, pl.ds(src, width)]

    return pl.pallas_call(
        body, name=name, grid=(k // tr,),
        in_specs=[pl.BlockSpec((N_DEV, tr, n_blk), lambda i: (0, i, 0)), pl.BlockSpec(memory_space=pl.ANY)],
        out_specs=pl.BlockSpec((tr, n_pad), lambda i: (i, 0)), out_shape=jax.ShapeDtypeStruct((k, n_pad), gathered.dtype),
        compiler_params=_cp("parallel"),
    )(gathered, after)


def _w_in_unpack(seg_grads, pieces, n_blk, *, name, tr=128):
    names = list(seg_grads)
    k = seg_grads[names[0]].shape[0]

    def body(*refs):
        o_ref = refs[-1]
        seg_ref = dict(zip(names, refs))
        for blk, dst, sn, src, _, width in pieces:
            o_ref[blk, :, pl.ds(dst, width)] = seg_ref[sn][:, pl.ds(src, width)].astype(o_ref.dtype)

    return pl.pallas_call(
        body, name=name, grid=(k // tr,),
        in_specs=[pl.BlockSpec((tr, seg_grads[sn].shape[1]), lambda i: (i, 0)) for sn in names],
        out_specs=pl.BlockSpec((N_DEV, tr, n_blk), lambda i: (0, i, 0)),
        out_shape=jax.ShapeDtypeStruct((N_DEV, k, n_blk), BF16), compiler_params=_cp("parallel"),
    )(*[seg_grads[sn] for sn in names])


def _row_spec(arr, tb, nj):
    return pl.BlockSpec((tb, arr.shape[1] // nj), lambda j, i: (i, j))


def _par_spec(arr):
    return pl.BlockSpec((1,) + arr.shape[1:], lambda j, i: (j, 0, 0))


def _blocked_fwd(fn, rows, params, outs, *, nj=1, tb, name):
    t = rows[0].shape[0]
    nr, npar = len(rows), len(params)

    def body(*refs):
        res = fn(*[r[...] for r in refs[:nr]], *[p[0] for p in refs[nr:nr + npar]])
        for o_ref, val in zip(refs[nr + npar:], res):
            o_ref[...] = val.astype(o_ref.dtype)

    return pl.pallas_call(
        body, name=name, grid=(nj, t // tb),
        in_specs=[_row_spec(a, tb, nj) for a in rows] + [_par_spec(p) for p in params],
        out_specs=[pl.BlockSpec((tb, c // nj), lambda j, i: (i, j)) for c, _ in outs],
        out_shape=[jax.ShapeDtypeStruct((t, c), dt) for c, dt in outs],
        compiler_params=_cp("parallel", "arbitrary"),
    )(*rows, *params)


def _blocked_bwd(fn, rows, params, cts, row_grad_dtypes, *, adds=None, nj=1, tb, name):
    t = rows[0].shape[0]
    nr, npar, nct = len(rows), len(params), len(cts)
    adds = adds or {}
    add_keys = sorted(adds)
    want, want_dtypes = [], []
    for k, dts in enumerate(row_grad_dtypes):
        for dt in (dts if isinstance(dts, tuple) else (dts,)):
            if dt is not None:
                want.append(k)
                want_dtypes.append(dt)

    def body(*refs):
        row_refs = refs[:nr]
        par_refs = refs[nr:nr + npar]
        ct_refs = refs[nr + npar:nr + npar + nct]
        add_refs = dict(zip(add_keys, refs[nr + npar + nct:nr + npar + nct + len(add_keys)]))
        out_refs = refs[nr + npar + nct + len(add_keys):]
        _, vjp = jax.vjp(fn, *[r[...] for r in row_refs], *[p[0] for p in par_refs])
        grads = vjp(tuple(c[...].astype(F32) for c in ct_refs))
        for o_ref, k in zip(out_refs, want):
            g = grads[k]
            if k in add_refs:
                g = g + add_refs[k][...].astype(F32)
            o_ref[...] = g.astype(o_ref.dtype)
        first = pl.program_id(1) == 0
        for o_ref, g in zip(out_refs[len(want):], grads[nr:]):
            @pl.when(first)
            def _(o_ref=o_ref):
                o_ref[...] = jnp.zeros_like(o_ref)
            o_ref[0] += g

    add_arrs = [adds[k] for k in add_keys]
    return pl.pallas_call(
        body, name=name, grid=(nj, t // tb),
        in_specs=[_row_spec(a, tb, nj) for a in rows] + [_par_spec(p) for p in params]
        + [_row_spec(c, tb, nj) for c in cts] + [_row_spec(a, tb, nj) for a in add_arrs],
        out_specs=[_row_spec(rows[k], tb, nj) for k in want] + [_par_spec(p) for p in params],
        out_shape=[jax.ShapeDtypeStruct(rows[k].shape, dt) for k, dt in zip(want, want_dtypes)]
        + [jax.ShapeDtypeStruct(p.shape, F32) for p in params],
        compiler_params=_cp("parallel", "arbitrary"),
    )(*rows, *params, *cts, *add_arrs)


def _rms_fn(x, w):
    return (x * lax.rsqrt(jnp.mean(x * x, axis=-1, keepdims=True) + EPS) * w,)


def _silu(x):
    return x * jax.nn.sigmoid(x)


def _merge_fn(glu_v, glu_g, g_a, g_b, y_a):
    y_b = glu_v * jax.nn.sigmoid(glu_g)
    return (jax.nn.sigmoid(g_a) * y_a + jax.nn.sigmoid(g_b) * y_b,)


def _s5_bu_fn(u, b_re, b_im):
    return _bdot(u, b_re, "nn"), _bdot(u, b_im, "nn")


def _s5_out_fn(s_re, s_im, u, c_re, c_im_neg, d):
    return (jax.nn.gelu(_bdot(s_re, c_re, "nn") + _bdot(s_im, c_im_neg, "nn") + d * u),)


HALO = SUBLANES


STRIP = 64


def _conv_strip(ext_ref, w_ref, b_ref, r0, cols):
    kw = w_ref.shape[0]
    xs = [ext_ref[pl.ds(r0 + HALO - kw + 1 + k, STRIP), cols] for k in range(kw)]
    c = b_ref[:, cols] + w_ref[0:1, cols] * xs[0]
    for k in range(1, kw):
        c = c + w_ref[k:k + 1, cols] * xs[k]
    return c, xs


def _fold(x):
    return x.reshape(STRIP // SUBLANES, SUBLANES, LANES).sum(axis=0)


def _conv_specs(xs, ws, bs, tb, cb, time_of):
    specs = []
    for x, w, b in zip(xs, ws, bs):
        specs += [
            pl.BlockSpec((HALO, cb), lambda j, i: (jnp.maximum(time_of(i) * (tb // HALO) - 1, 0), j)),
            pl.BlockSpec((tb, cb), lambda j, i: (time_of(i), j)),
            pl.BlockSpec((w.shape[0], cb), lambda j, i: (0, j)),
            pl.BlockSpec((1, cb), lambda j, i: (0, j)),
        ]
    return specs


def _conv_fwd(comb, xs, ws, bs, *, out_dtype, name, tb=1024):
    t, c = xs[0].shape
    cb = _tile(c, 512)
    ns = len(xs)

    def body(*refs):
        i = pl.program_id(1)
        o_ref = refs[4 * ns]
        exts = refs[4 * ns + 1:]
        for s in range(ns):
            xp_ref, xm_ref = refs[4 * s:4 * s + 2]
            exts[s][pl.ds(0, HALO), :] = jnp.where(i == 0, 0.0, xp_ref[...])
            exts[s][pl.ds(HALO, tb), :] = xm_ref[...]
        for c0 in range(0, cb, LANES):
            cols = pl.ds(c0, LANES)
            for r0 in range(0, tb, STRIP):
                cs = [_conv_strip(exts[s], refs[4 * s + 2], refs[4 * s + 3], r0, cols)[0] for s in range(ns)]
                o_ref[pl.ds(r0, STRIP), cols] = comb(*cs).astype(out_dtype)

    flat = [a for x, w, b in zip(xs, ws, bs) for a in (x, x, w, b)]
    return pl.pallas_call(
        body, name=name, grid=(c // cb, t // tb),
        in_specs=_conv_specs(xs, ws, bs, tb, cb, lambda i: i),
        out_specs=pl.BlockSpec((tb, cb), lambda j, i: (i, j)),
        out_shape=jax.ShapeDtypeStruct((t, c), out_dtype),
        scratch_shapes=[pltpu.VMEM((HALO + tb, cb), F32) for _ in range(ns)],
        compiler_params=_cp("parallel", "arbitrary"),
    )(*flat)


def _conv_bwd(comb, xs, ws, bs, dy, *, dx_dtype, name, tb=1024):
    t, c = xs[0].shape
    cb = _tile(c, 512)
    ns = len(xs)
    nt = t // tb
    kw = ws[0].shape[0]

    def body(*refs):
        step = pl.program_id(1)
        dy_ref = refs[4 * ns]
        out_refs = refs[4 * ns + 1:4 * ns + 1 + 3 * ns]
        scratch = refs[4 * ns + 1 + 3 * ns:]
        exts, dcs, carries = scratch[:ns], scratch[ns:2 * ns], scratch[2 * ns:]

        @pl.when(step == 0)
        def _():
            for s in range(ns):
                carries[s][...] = jnp.zeros_like(carries[s])
                out_refs[3 * s + 1][...] = jnp.zeros_like(out_refs[3 * s + 1])
                out_refs[3 * s + 2][...] = jnp.zeros_like(out_refs[3 * s + 2])

        for s in range(ns):
            xp_ref, xm_ref = refs[4 * s:4 * s + 2]
            exts[s][pl.ds(0, HALO), :] = jnp.where(step == nt - 1, 0.0, xp_ref[...])
            exts[s][pl.ds(HALO, tb), :] = xm_ref[...]
            dcs[s][pl.ds(tb, HALO), :] = carries[s][...]
        for c0 in range(0, cb, LANES):
            cols = pl.ds(c0, LANES)
            acc_w = [[jnp.zeros((SUBLANES, LANES), F32) for _ in range(kw)] for _ in range(ns)]
            acc_b = [jnp.zeros((SUBLANES, LANES), F32) for _ in range(ns)]
            for r0 in range(0, tb, STRIP):
                strips = [_conv_strip(exts[s], refs[4 * s + 2], refs[4 * s + 3], r0, cols) for s in range(ns)]
                _, vjp = jax.vjp(comb, *[cs for cs, _ in strips])
                grads = vjp(dy_ref[pl.ds(r0, STRIP), cols].astype(F32))
                for s in range(ns):
                    dcs[s][pl.ds(r0, STRIP), cols] = grads[s]
                    acc_b[s] = acc_b[s] + _fold(grads[s])
                    for k in range(kw):
                        acc_w[s][k] = acc_w[s][k] + _fold(grads[s] * strips[s][1][k])
            for s in range(ns):
                dw_ref, db_ref = out_refs[3 * s + 1], out_refs[3 * s + 2]
                db_ref[:, cols] += jnp.sum(acc_b[s], axis=0, keepdims=True)
                for k in range(kw):
                    dw_ref[k:k + 1, cols] += jnp.sum(acc_w[s][k], axis=0, keepdims=True)
        for s in range(ns):
            w_ref, dx_ref = refs[4 * s + 2], out_refs[3 * s]
            for c0 in range(0, cb, LANES):
                cols = pl.ds(c0, LANES)
                for r0 in range(0, tb, STRIP):
                    dx = w_ref[kw - 1:kw, cols] * dcs[s][pl.ds(r0, STRIP), cols]
                    for k in range(kw - 1):
                        dx = dx + w_ref[k:k + 1, cols] * dcs[s][pl.ds(r0 + kw - 1 - k, STRIP), cols]
                    dx_ref[pl.ds(r0, STRIP), cols] = dx.astype(dx_dtype)
            carries[s][...] = dcs[s][pl.ds(0, HALO), :]

    flat = [a for x, w, b in zip(xs, ws, bs) for a in (x, x, w, b)]
    rev = lambda i: nt - 1 - i
    out_specs, out_shape = [], []
    for x, w, b in zip(xs, ws, bs):
        out_specs += [pl.BlockSpec((tb, cb), lambda j, i: (rev(i), j)),
                      pl.BlockSpec((w.shape[0], cb), lambda j, i: (0, j)),
                      pl.BlockSpec((1, cb), lambda j, i: (0, j))]
        out_shape += [jax.ShapeDtypeStruct((t, c), dx_dtype), jax.ShapeDtypeStruct(w.shape, F32),
                      jax.ShapeDtypeStruct(b.shape, F32)]
    res = pl.pallas_call(
        body, name=name, grid=(c // cb, nt),
        in_specs=_conv_specs(xs, ws, bs, tb, cb, rev) + [pl.BlockSpec((tb, cb), lambda j, i: (rev(i), j))],
        out_specs=out_specs, out_shape=out_shape,
        scratch_shapes=[pltpu.VMEM((HALO + tb, cb), F32) for _ in range(2 * ns)]
        + [pltpu.VMEM((HALO, cb), F32) for _ in range(ns)],
        compiler_params=_cp("parallel", "arbitrary"),
    )(*flat, dy)
    return [tuple(res[3 * s:3 * s + 3]) for s in range(ns)]


def _comb_silu(c):
    return _silu(c)


def _comb_glu(cg, cv):
    return _silu(cg) * cv


def _ssd_fn(nheads, hdim):
    def fn(x, bm, cm, z, dtr, hin, dtb, alog, dsk, nw):
        q = x.shape[0]
        dt = jax.nn.softplus(dtr + dtb)
        da = dt * (-jnp.exp(alog))
        li = lax.broadcasted_iota(jnp.int32, (q, q), 0)
        si = lax.broadcasted_iota(jnp.int32, (q, q), 1)
        causal = li >= si
        tri = causal.astype(F32)
        acs = jnp.dot(tri, da, precision=HIGHEST, preferred_element_type=F32)
        acs_row = lax.dot_general(da, tri, (((0,), (1,)), ((), ())), precision=HIGHEST,
                                  preferred_element_type=F32)
        cb = _bdot(cm, bm, "nt")
        ch = _bdot(cm, hin, "nn")
        ys, hs = [], []
        for r in range(nheads):
            cols = slice(r * hdim, (r + 1) * hdim)
            xr = x[:, cols]
            a_col = acs[:, r:r + 1]
            decay = jnp.exp(jnp.where(causal, a_col - acs_row[r:r + 1, :], -1e30))
            xd = xr * dt[:, r:r + 1]
            y_diag = _bdot(cb * decay, xd, "nn")
            y_off = ch[:, cols] * jnp.exp(a_col)
            last = acs[q - 1:q, r:r + 1]
            st = _bdot(bm * jnp.exp(last - a_col), xd, "tn")
            hs.append(jnp.exp(last) * hin[:, cols] + st)
            ys.append(y_diag + y_off + dsk[:, r:r + 1] * xr)
        y = jnp.concatenate(ys, axis=1) * _silu(z)
        yn = y * lax.rsqrt(jnp.mean(y * y, axis=-1, keepdims=True) + EPS) * nw
        return yn, jnp.concatenate(hs, axis=1)
    return fn


def _ssd_specs(rp, nr, time_of):
    row = lambda w: pl.BlockSpec((CHUNK, w), lambda g, c: (time_of(c), g))
    par = lambda w: pl.BlockSpec((1, 1, w), lambda g, c: (g, 0, 0))
    return dict(
        x=row(rp), bc=row(D_STATE), dtr=pl.BlockSpec((1, CHUNK, nr), lambda g, c: (g, time_of(c), 0)),
        h=pl.BlockSpec((1, 1, D_STATE, rp), lambda g, c: (g, time_of(c), 0, 0)), pr=par(nr), pw=par(rp))


def _ssd_fwd(xs, bm, cm, z, dtr, dtb, alog, dsk, nw, *, name):
    t = xs.shape[0]
    g, _, nr = dtr.shape
    rp = xs.shape[1] // g
    nc = t // CHUNK
    fn = _ssd_fn(nr, rp // nr)
    sp = _ssd_specs(rp, nr, lambda c: c)

    def body(x_ref, b_ref, c_ref, z_ref, dtr_ref, dtb_ref, al_ref, dsk_ref, nw_ref, yn_ref, hs_ref, h_ref):
        @pl.when(pl.program_id(1) == 0)
        def _():
            h_ref[...] = jnp.zeros_like(h_ref)
        hin = h_ref[...]
        hs_ref[0, 0] = hin
        yn, hout = fn(x_ref[...], b_ref[...], c_ref[...], z_ref[...], dtr_ref[0], hin,
                      dtb_ref[0], al_ref[0], dsk_ref[0], nw_ref[0])
        yn_ref[...] = yn.astype(yn_ref.dtype)
        h_ref[...] = hout

    return pl.pallas_call(
        body, name=name, grid=(g, nc),
        in_specs=[sp["x"], sp["bc"], sp["bc"], sp["x"], sp["dtr"], sp["pr"], sp["pr"], sp["pr"], sp["pw"]],
        out_specs=[sp["x"], sp["h"]],
        out_shape=[jax.ShapeDtypeStruct(xs.shape, BF16), jax.ShapeDtypeStruct((g, nc, D_STATE, rp), F32)],
        scratch_shapes=[pltpu.VMEM((D_STATE, rp), F32)],
        compiler_params=_cp("parallel", "arbitrary"),
    )(xs, bm, cm, z, dtr, dtb, alog, dsk, nw)


def _ssd_bwd(xs, bm, cm, z, dtr, hsave, dtb, alog, dsk, nw, dyn, *, name):
    t = xs.shape[0]
    g, _, nr = dtr.shape
    rp = xs.shape[1] // g
    nc = t // CHUNK
    fn = _ssd_fn(nr, rp // nr)
    sp = _ssd_specs(rp, nr, lambda c: nc - 1 - c)

    def body(x_ref, b_ref, c_ref, z_ref, dtr_ref, hs_ref, dtb_ref, al_ref, dsk_ref, nw_ref, dyn_ref,
             dx_ref, db_ref, dc_ref, dz_ref, ddtr_ref, ddtb_ref, dal_ref, ddsk_ref, dnw_ref, dh_ref):
        first = pl.program_id(1) == 0

        @pl.when(first)
        def _():
            dh_ref[...] = jnp.zeros_like(dh_ref)
            for r in (ddtb_ref, dal_ref, ddsk_ref, dnw_ref):
                r[...] = jnp.zeros_like(r)

        _, vjp = jax.vjp(fn, x_ref[...], b_ref[...], c_ref[...], z_ref[...], dtr_ref[0], hs_ref[0, 0],
                         dtb_ref[0], al_ref[0], dsk_ref[0], nw_ref[0])
        dx, db, dc, dz, ddtr, dhin, ddtb, dal, ddsk, dnw = vjp((dyn_ref[...].astype(F32), dh_ref[...]))
        dx_ref[...] = dx
        db_ref[...] = db
        dc_ref[...] = dc
        dz_ref[...] = dz.astype(dz_ref.dtype)
        ddtr_ref[0] = ddtr
        dh_ref[...] = dhin
        ddtb_ref[0] += ddtb
        dal_ref[0] += dal
        ddsk_ref[0] += ddsk
        dnw_ref[0] += dnw

    sd = jax.ShapeDtypeStruct
    return pl.pallas_call(
        body, name=name, grid=(g, nc),
        in_specs=[sp["x"], sp["bc"], sp["bc"], sp["x"], sp["dtr"], sp["h"], sp["pr"], sp["pr"], sp["pr"], sp["pw"],
                  sp["x"]],
        out_specs=[sp["x"], sp["bc"], sp["bc"], sp["x"], sp["dtr"], sp["pr"], sp["pr"], sp["pr"], sp["pw"]],
        out_shape=[sd(xs.shape, F32), sd(bm.shape, F32), sd(cm.shape, F32), sd(z.shape, BF16), sd(dtr.shape, F32),
                   sd(dtb.shape, F32), sd(alog.shape, F32), sd(dsk.shape, F32), sd(nw.shape, F32)],
        scratch_shapes=[pltpu.VMEM((D_STATE, rp), F32)],
        compiler_params=_cp("parallel", "arbitrary"),
    )(xs, bm, cm, z, dtr, hsave, dtb, alog, dsk, nw, dyn)


def _s5_param_fn(lam_re, lam_im, log_dt, bt_re, bt_im):
    lr = jnp.minimum(lam_re, EIG_MAX)
    dt = jnp.exp(log_dt)
    mag = jnp.exp(lr * dt)
    lb_re = mag * jnp.cos(lam_im * dt)
    lb_im = mag * jnp.sin(lam_im * dt)
    n_re = lb_re - 1.0
    den = lr * lr + lam_im * lam_im
    k_re = (n_re * lr + lb_im * lam_im) / den
    k_im = (lb_im * lr - n_re * lam_im) / den
    return lb_re, lb_im, k_re * bt_re - k_im * bt_im, k_re * bt_im + k_im * bt_re


def _s5_params(lam_re, lam_im, log_dt, bt_re, bt_im, cts=None, *, name):
    args = (lam_re, lam_im, log_dt, bt_re, bt_im)
    n = len(args)

    def body(*refs):
        vals = [r[...] for r in refs[:n]]
        if cts is None:
            res = _s5_param_fn(*vals)
        else:
            _, vjp = jax.vjp(_s5_param_fn, *vals)
            res = vjp(tuple(r[...] for r in refs[n:n + 4]))
        for o_ref, v in zip(refs[-len(res):], res):
            o_ref[...] = v

    if cts is None:
        out = [lam_re, lam_im, bt_re, bt_im]
        ins = args
    else:
        out = list(args)
        ins = args + tuple(cts)
    return pl.pallas_call(
        body, name=name, out_shape=[jax.ShapeDtypeStruct(a.shape, F32) for a in out],
        compiler_params=pltpu.CompilerParams(vmem_limit_bytes=VMEM_LIMIT),
    )(*ins)


SCAN_COLS = 512


def _cmul(xr, xi, yr, yi):
    return xr * yr - xi * yi, xr * yi + xi * yr


def _scan_consts(a_re, a_im, cols, reverse):
    shape = (SUBLANES, cols)
    row = lax.broadcasted_iota(jnp.int32, shape, 0)
    dist = (SUBLANES - 1 - row) if reverse else row
    mr, mi = jnp.broadcast_to(a_re, shape), jnp.broadcast_to(a_im, shape)
    pr, pi = mr, mi
    mults = []
    for d in (1, 2, 4):
        mults.append((mr, mi))
        qr, qi = _cmul(pr, pi, mr, mi)
        has_bit = (dist & d) != 0
        pr, pi = jnp.where(has_bit, qr, pr), jnp.where(has_bit, qi, pi)
        mr, mi = _cmul(mr, mi, mr, mi)
    return mults, (pr, pi), dist


def _scan_group(xr, xi, consts, cr, ci, reverse):
    mults, (pr, pi), dist = consts
    for d, (mr, mi) in zip((1, 2, 4), mults):
        shift = (SUBLANES - d) if reverse else d
        sr = jnp.where(dist >= d, pltpu.roll(xr, shift, 0), 0.0)
        si = jnp.where(dist >= d, pltpu.roll(xi, shift, 0), 0.0)
        tr, ti = _cmul(mr, mi, sr, si)
        xr, xi = xr + tr, xi + ti
    last = slice(0, 1) if reverse else slice(SUBLANES - 1, SUBLANES)
    nr, ni = _cmul(pr[last], pi[last], cr, ci)
    tr, ti = _cmul(pr, pi, jnp.broadcast_to(cr, xr.shape), jnp.broadcast_to(ci, xr.shape))
    return xr + tr, xi + ti, xr[last] + nr, xi[last] + ni


def _scan_specs(tb, time_of):
    row = pl.BlockSpec((tb, SCAN_COLS), lambda j, i: (time_of(i), j))
    par = pl.BlockSpec((1, SCAN_COLS), lambda j, i: (0, j))
    return row, par


def _s5_scan_fwd(bu_re, bu_im, lb_re, lb_im, *, name, tb=1024):
    t, c = bu_re.shape
    nj = c // SCAN_COLS
    row, par = _scan_specs(tb, lambda i: i)

    def body(bre_ref, bim_ref, lre_ref, lim_ref, sre_ref, sim_ref, cre_ref, cim_ref):
        @pl.when(pl.program_id(1) == 0)
        def _():
            cre_ref[...] = jnp.zeros_like(cre_ref)
            cim_ref[...] = jnp.zeros_like(cim_ref)
        consts = _scan_consts(lre_ref[...], lim_ref[...], SCAN_COLS, False)

        def group(k, carry):
            rows = pl.ds(pl.multiple_of(k * SUBLANES, SUBLANES), SUBLANES)
            sr, si, cr, ci = _scan_group(bre_ref[rows, :], bim_ref[rows, :], consts, *carry, False)
            sre_ref[rows, :] = sr
            sim_ref[rows, :] = si
            return cr, ci

        sr, si = lax.fori_loop(0, tb // SUBLANES, group, (cre_ref[...], cim_ref[...]), unroll=4)
        cre_ref[...] = sr
        cim_ref[...] = si

    return pl.pallas_call(
        body, name=name, grid=(nj, t // tb), in_specs=[row, row, par, par], out_specs=[row, row],
        out_shape=[jax.ShapeDtypeStruct((t, c), F32)] * 2,
        scratch_shapes=[pltpu.VMEM((1, SCAN_COLS), F32)] * 2,
        compiler_params=_cp("parallel", "arbitrary"),
    )(bu_re, bu_im, lb_re, lb_im)


def _s5_scan_bwd(s_re, s_im, ds_re, ds_im, lb_re, lb_im, *, name, tb=1024):
    t, c = s_re.shape
    nj = c // SCAN_COLS
    nt = t // tb
    rev = lambda i: nt - 1 - i
    row, par = _scan_specs(tb, rev)
    prev = pl.BlockSpec((HALO, SCAN_COLS), lambda j, i: (jnp.maximum(rev(i) * (tb // HALO) - 1, 0), j))

    def body(sre_ref, sim_ref, pre_ref, pim_ref, dre_ref, dim_ref, lre_ref, lim_ref,
             gre_ref, gim_ref, dlre_ref, dlim_ref, cre_ref, cim_ref, ext_re, ext_im):
        step_id = pl.program_id(1)

        @pl.when(step_id == 0)
        def _():
            cre_ref[...] = jnp.zeros_like(cre_ref)
            cim_ref[...] = jnp.zeros_like(cim_ref)
            dlre_ref[...] = jnp.zeros_like(dlre_ref)
            dlim_ref[...] = jnp.zeros_like(dlim_ref)
        consts = _scan_consts(lre_ref[...], -lim_ref[...], SCAN_COLS, True)
        ngroups = tb // SUBLANES

        def group(k, carry):
            rows = pl.ds(pl.multiple_of((ngroups - 1 - k) * SUBLANES, SUBLANES), SUBLANES)
            gr, gi, cr, ci = _scan_group(dre_ref[rows, :], dim_ref[rows, :], consts, *carry, True)
            gre_ref[rows, :] = gr
            gim_ref[rows, :] = gi
            return cr, ci

        gr, gi = lax.fori_loop(0, ngroups, group, (cre_ref[...], cim_ref[...]), unroll=4)
        cre_ref[...] = gr
        cim_ref[...] = gi
        has_past = step_id != nt - 1
        ext_re[pl.ds(0, HALO), :] = jnp.where(has_past, pre_ref[...], 0.0)
        ext_im[pl.ds(0, HALO), :] = jnp.where(has_past, pim_ref[...], 0.0)
        ext_re[pl.ds(HALO, tb), :] = sre_ref[...]
        ext_im[pl.ds(HALO, tb), :] = sim_ref[...]
        pr, pi = ext_re[pl.ds(HALO - 1, tb), :], ext_im[pl.ds(HALO - 1, tb), :]
        g_re, g_im = gre_ref[...], gim_ref[...]
        dlre_ref[...] += jnp.sum(pr * g_re + pi * g_im, axis=0, keepdims=True)
        dlim_ref[...] += jnp.sum(pr * g_im - pi * g_re, axis=0, keepdims=True)

    return pl.pallas_call(
        body, name=name, grid=(nj, nt),
        in_specs=[row, row, prev, prev, row, row, par, par], out_specs=[row, row, par, par],
        out_shape=[jax.ShapeDtypeStruct((t, c), F32)] * 2 + [jax.ShapeDtypeStruct((1, c), F32)] * 2,
        scratch_shapes=[pltpu.VMEM((1, SCAN_COLS), F32)] * 2 + [pltpu.VMEM((HALO + tb, SCAN_COLS), F32)] * 2,
        compiler_params=_cp("parallel", "arbitrary"),
    )(s_re, s_im, s_re, s_im, ds_re, ds_im, lb_re, lb_im)


def _loss_fn(h, w, tgt):
    err = _rms_fn(h, w)[0] - tgt
    return 0.5 * jnp.sum(jnp.mean(err * err, axis=-1, keepdims=True), axis=0, keepdims=True)


def _loss_head(h, w, tgt, *, name, tb=256):
    t, d = h.shape

    def body(h_ref, w_ref, t_ref, loss_ref, dh_ref, dhb_ref, dw_ref):
        @pl.when(pl.program_id(0) == 0)
        def _():
            loss_ref[...] = jnp.zeros_like(loss_ref)
            dw_ref[...] = jnp.zeros_like(dw_ref)
        part, vjp = jax.vjp(_loss_fn, h_ref[...], w_ref[...], t_ref[...])
        dh, dw, _ = vjp(jnp.ones((1, 1), F32))
        loss_ref[...] += jnp.broadcast_to(part, loss_ref.shape)
        dh_ref[...] = dh
        dhb_ref[...] = dh.astype(BF16)
        dw_ref[...] += dw

    row = pl.BlockSpec((tb, d), lambda i: (i, 0))
    par = pl.BlockSpec((1, d), lambda i: (0, 0))
    return pl.pallas_call(
        body, name=name, grid=(t // tb,), in_specs=[row, par, row],
        out_specs=[pl.BlockSpec((SUBLANES, LANES), lambda i: (0, 0)), row, row, par],
        out_shape=[jax.ShapeDtypeStruct((SUBLANES, LANES), F32), jax.ShapeDtypeStruct((t, d), F32),
                   jax.ShapeDtypeStruct((t, d), BF16), jax.ShapeDtypeStruct((1, d), F32)],
        compiler_params=_cp("arbitrary"),
    )(h, w, tgt)


def _adamw(w, g, m, v, *, name):
    r, c = w.shape
    tr = _tile(r, 256, SUBLANES)

    def body(w_ref, g_ref, m_ref, v_ref, d_ref, nm_ref, nv_ref):
        g = g_ref[...]
        nm = ADAM_B1 * m_ref[...] + (1.0 - ADAM_B1) * g
        nv = ADAM_B2 * v_ref[...] + (1.0 - ADAM_B2) * (g * g)
        m_hat = nm / (1.0 - ADAM_B1 ** ADAM_STEP)
        v_hat = nv / (1.0 - ADAM_B2 ** ADAM_STEP)
        d_ref[...] = -ADAM_LR * (m_hat / (jnp.sqrt(v_hat) + ADAM_EPS) + ADAM_WD * w_ref[...])
        nm_ref[...] = nm
        nv_ref[...] = nv

    spec = pl.BlockSpec((tr, c), lambda i: (i, 0))
    return pl.pallas_call(
        body, name=name, grid=(r // tr,), in_specs=[spec] * 4, out_specs=[spec] * 3,
        out_shape=[jax.ShapeDtypeStruct((r, c), F32)] * 3, compiler_params=_cp("parallel"),
    )(w, g, m, v)


def _sum_parts(parts, *, name):
    _, r, c = parts.shape
    tr = _tile(r, 128, SUBLANES)

    def body(p_ref, o_ref):
        acc = p_ref[0].astype(F32)
        for k in range(1, N_DEV):
            acc = acc + p_ref[k].astype(F32)
        o_ref[...] = acc

    return pl.pallas_call(
        body, name=name, grid=(r // tr,), in_specs=[pl.BlockSpec((N_DEV, tr, c), lambda i: (0, i, 0))],
        out_specs=pl.BlockSpec((tr, c), lambda i: (i, 0)), out_shape=jax.ShapeDtypeStruct((r, c), F32),
        compiler_params=_cp("parallel"),
    )(parts)


def _position():
    return lax.axis_index("x"), lax.axis_index("y"), lax.axis_index("c")


def _flat(px, py, pc):
    return 4 * px + 2 * py + pc


def _all_gather(shard, *, name):
    def body(x_ref, out_ref, token, send_sems, recv_sems, local_sem):
        token[...] = jnp.zeros_like(token)
        x, y, c = _position()
        me, sibling = (x, y, c), (x, y, 1 - c)
        chips = [(1 - x, y), (x, 1 - y), (1 - x, 1 - y)]

        def copy(k, block, to, src=None):
            slot = out_ref.at[_flat(*block)]
            return pltpu.make_async_remote_copy(
                src_ref=slot if src is None else src, dst_ref=slot, send_sem=send_sems.at[k],
                recv_sem=recv_sems.at[k], device_id=to, device_id_type=MESH)

        mine = pltpu.make_async_copy(x_ref, out_ref.at[_flat(*me)], local_sem)
        mine.start()
        first = [copy(0, me, sibling, src=x_ref)]
        first += [copy(1 + j, me, (*chip, c), src=x_ref) for j, chip in enumerate(chips)]
        for cp in first:
            cp.start()
        passed = [copy(4 + j, (*chip, c), sibling) for j, chip in enumerate(chips)]
        for j, chip in enumerate(chips):
            copy(1 + j, (*chip, c), me).wait_recv()
            passed[j].start()
        copy(0, sibling, me).wait_recv()
        for j, chip in enumerate(chips):
            copy(4 + j, (*chip, 1 - c), me).wait_recv()
        for cp in first + passed:
            cp.wait_send()
        mine.wait()

    return pl.pallas_call(
        body, name=name,
        out_shape=(jax.ShapeDtypeStruct((N_DEV,) + shard.shape, shard.dtype),
                   jax.ShapeDtypeStruct((SUBLANES, LANES), F32)),
        in_specs=[pl.BlockSpec(memory_space=pl.ANY)],
        out_specs=(pl.BlockSpec(memory_space=pl.ANY), pl.BlockSpec(memory_space=pltpu.VMEM)),
        scratch_shapes=[pltpu.SemaphoreType.DMA((7,)), pltpu.SemaphoreType.DMA((7,)), pltpu.SemaphoreType.DMA(())],
    )(shard)


_HBM = pl.BlockSpec(memory_space=pltpu.HBM)
_SEM = pl.BlockSpec(memory_space=pltpu.SEMAPHORE)
_EFFECT = pltpu.SideEffectType.DATAFLOW_SIDE_EFFECTING


def _copy_ends(src_ref, land_ref, mode, me, to):
    if mode == "gather_slot":
        return src_ref, land_ref.at[me]
    if mode == "gather_cols":
        w = src_ref.shape[1]
        return src_ref, land_ref.at[:, pl.ds(pl.multiple_of(me * w, LANES), w)]
    if mode == "scatter_slot":
        return src_ref.at[to], land_ref.at[me]
    w = land_ref.shape[2]
    return src_ref.at[:, pl.ds(pl.multiple_of(to * w, LANES), w)], land_ref.at[me]


BF16_ROWS = 16


def _land_shape(src, mode):
    if mode == "gather_slot":
        return (N_DEV,) + src.shape
    if mode == "gather_cols":
        return (src.shape[0], N_DEV * src.shape[1])
    if mode == "scatter_slot":
        return src.shape
    return (N_DEV, src.shape[0], src.shape[1] // N_DEV)


OTHER_CHIPS = (2, 4, 6)


def _n_copies(mode):
    return {"gather_chip": 1 + len(OTHER_CHIPS), "forward": len(OTHER_CHIPS)}.get(mode, N_DEV - 1)


def _exchange_copies(src_ref, land_ref, send_sems, recv_sems, mode):
    x, y, c = _position()
    me = _flat(x, y, c)
    peer_of = lambda k: (x ^ ((k >> 2) & 1), y ^ ((k >> 1) & 1), c ^ (k & 1))
    if mode == "forward":
        slots = [land_ref.at[_flat(*peer_of(k))] for k in OTHER_CHIPS]
        plan = [(slot, slot, peer_of(1)) for slot in slots]
    elif mode == "gather_chip":
        plan = [(*_copy_ends(src_ref, land_ref, "gather_slot", me, _flat(*peer_of(k))), peer_of(k))
                for k in (1,) + OTHER_CHIPS]
    else:
        plan = [(*_copy_ends(src_ref, land_ref, mode, me, _flat(*peer_of(k))), peer_of(k)) for k in range(1, N_DEV)]
    return [pltpu.make_async_remote_copy(src_ref=src, dst_ref=dst, send_sem=send_sems.at[i], recv_sem=recv_sems.at[i],
                                         device_id=peer, device_id_type=MESH)
            for i, (src, dst, peer) in enumerate(plan)]


def _place_own(src, mode, dev, *, name):
    rows = src.shape[1] if mode == "scatter_slot" else src.shape[0]
    tr = _tile(rows, 512, BF16_ROWS)
    land = _land_shape(src, mode)
    width = land[-1] if mode.startswith("scatter") else src.shape[1]
    slot = pl.BlockSpec((1, tr, width), lambda i, d: (d[0], i, 0))
    cols = pl.BlockSpec((tr, width), lambda i, d: (i, d[0]))
    whole = pl.BlockSpec((tr, width), lambda i, d: (i, 0))
    in_spec, out_spec = {"gather_slot": (whole, slot), "gather_cols": (whole, cols), "scatter_slot": (slot, slot),
                         "scatter_cols": (cols, slot)}[mode]

    def body(dev_ref, src_ref, land_ref):
        land_ref[...] = src_ref[...].reshape(land_ref.shape)

    return pl.pallas_call(
        body, name=name, out_shape=jax.ShapeDtypeStruct(land, src.dtype),
        grid_spec=pltpu.PrefetchScalarGridSpec(num_scalar_prefetch=1, grid=(rows // tr,), in_specs=[in_spec],
                                               out_specs=out_spec),
        compiler_params=_cp("parallel"),
    )(dev, src)


def _exchange_start(src, mode, dev, *, name, land=None, after=None):
    if land is None:
        land = _place_own(src, "gather_slot" if mode == "gather_chip" else mode, dev, name=name + "_own")
    n_copies = _n_copies(mode)

    def body(*refs):
        src_ref, land_ref = refs[:2]
        send_sems, recv_sems, _, _, token = refs[-5:]
        for cp in _exchange_copies(src_ref, land_ref, send_sems, recv_sems, mode):
            cp.start()
        token[...] = jnp.zeros_like(token)

    hbm = pltpu.with_memory_space_constraint
    *handle, token = pl.pallas_call(
        body, name=name,
        out_shape=(pltpu.SemaphoreType.DMA((n_copies,)), pltpu.SemaphoreType.DMA((n_copies,)),
                   pltpu.HBM(src.shape, src.dtype), pltpu.HBM(land.shape, land.dtype),
                   jax.ShapeDtypeStruct((SUBLANES, LANES), F32)),
        in_specs=(_HBM, _HBM) + ((pl.BlockSpec(memory_space=pl.ANY),) if after is not None else ()),
        out_specs=(_SEM, _SEM, _HBM, _HBM, pl.BlockSpec(memory_space=pltpu.VMEM)),
        input_output_aliases={0: 2, 1: 3}, compiler_params=pltpu.CompilerParams(has_side_effects=_EFFECT),
    )(hbm(src, pltpu.HBM), hbm(land, pltpu.HBM), *(() if after is None else (after,)))
    return (tuple(handle), mode), token


def _exchange_wait(pending, after, *, name):
    (send_sems, recv_sems, src_thru, land_thru), mode = pending

    def body(src_ref, land_ref, send_sems, recv_sems, after_ref, src_dead, got_ref):
        for cp in _exchange_copies(src_ref, land_ref, send_sems, recv_sems, mode):
            cp.wait_send()
            cp.wait_recv()

    return pl.pallas_call(
        body, name=name, out_shape=(pltpu.HBM(src_thru.shape, src_thru.dtype), pltpu.HBM(land_thru.shape, land_thru.dtype)),
        in_specs=(_HBM, _HBM, _SEM, _SEM, pl.BlockSpec(memory_space=pl.ANY)), out_specs=(_HBM, _HBM),
        input_output_aliases={0: 0, 1: 1}, compiler_params=pltpu.CompilerParams(has_side_effects=_EFFECT),
    )(src_thru, land_thru, send_sems, recv_sems, after)[1]


def _after(x, token):
    return x + token[0, 0].astype(x.dtype)


def _touch(*arrays, name):
    def body(*refs):
        refs[-1][...] = jnp.zeros_like(refs[-1])

    return pl.pallas_call(
        body, name=name, out_shape=jax.ShapeDtypeStruct((SUBLANES, LANES), F32),
        in_specs=[pl.BlockSpec(memory_space=pl.ANY)] * len(arrays), out_specs=pl.BlockSpec(memory_space=pltpu.VMEM),
    )(*arrays)


def _pad_cols(a, mult):
    pad = -a.shape[1] % mult
    return jnp.pad(a, ((0, 0), (0, pad))) if pad else a


def _pack(arrs, cols):
    flat = jnp.concatenate([a.reshape(-1).astype(F32) for a in arrs])
    sizes = [int(a.size) for a in arrs]
    flat = jnp.pad(flat, (0, -flat.shape[0] % (N_DEV * SUBLANES * cols)))
    return flat.reshape(-1, cols), sizes


def _unpack(flat2d, sizes, shapes):
    flat = flat2d.reshape(-1)
    out, o = [], 0
    for n, s in zip(sizes, shapes):
        out.append(flat[o:o + n].reshape(s))
        o += n
    return out


PACK_COLS = SUBLANES * LANES


def kernel(x, norm_mix_w, w_in, conv_a_w, conv_a_b, dt_bias, a_log, d_a, norm_a_w, w_proj_a, s5_lam_re, s5_lam_im, s5_log_dt, s5_b_re, s5_b_im, s5_c_re, s5_c_im, s5_d, w_s5_glu, w_out, norm_ffn_w, w_up, conv_ffn_w, conv_ffn_b, w_down, norm_final_w, loss_target, m_norm_mix_w, m_w_in, m_conv_a_w, m_conv_a_b, m_dt_bias, m_a_log, m_d_a, m_norm_a_w, m_w_proj_a, m_s5_lam_re, m_s5_lam_im, m_s5_log_dt, m_s5_b_re, m_s5_b_im, m_s5_c_re, m_s5_c_im, m_s5_d, m_w_s5_glu, m_w_out, m_norm_ffn_w, m_w_up, m_conv_ffn_w, m_conv_ffn_b, m_w_down, m_norm_final_w, v_norm_mix_w, v_w_in, v_conv_a_w, v_conv_a_b, v_dt_bias, v_a_log, v_d_a, v_norm_a_w, v_w_proj_a, v_s5_lam_re, v_s5_lam_im, v_s5_log_dt, v_s5_b_re, v_s5_b_im, v_s5_c_re, v_s5_c_im, v_s5_d, v_w_s5_glu, v_w_out, v_norm_ffn_w, v_w_up, v_conv_ffn_w, v_conv_ffn_b, v_w_down, v_norm_final_w):
    weights = dict(norm_mix_w=norm_mix_w, w_in=w_in, conv_a_w=conv_a_w, conv_a_b=conv_a_b, dt_bias=dt_bias, a_log=a_log, d_a=d_a, norm_a_w=norm_a_w, w_proj_a=w_proj_a, s5_lam_re=s5_lam_re, s5_lam_im=s5_lam_im, s5_log_dt=s5_log_dt, s5_b_re=s5_b_re, s5_b_im=s5_b_im, s5_c_re=s5_c_re, s5_c_im=s5_c_im, s5_d=s5_d, w_s5_glu=w_s5_glu, w_out=w_out, norm_ffn_w=norm_ffn_w, w_up=w_up, conv_ffn_w=conv_ffn_w, conv_ffn_b=conv_ffn_b, w_down=w_down, norm_final_w=norm_final_w)
    moms = dict(norm_mix_w=m_norm_mix_w, w_in=m_w_in, conv_a_w=m_conv_a_w, conv_a_b=m_conv_a_b, dt_bias=m_dt_bias, a_log=m_a_log, d_a=m_d_a, norm_a_w=m_norm_a_w, w_proj_a=m_w_proj_a, s5_lam_re=m_s5_lam_re, s5_lam_im=m_s5_lam_im, s5_log_dt=m_s5_log_dt, s5_b_re=m_s5_b_re, s5_b_im=m_s5_b_im, s5_c_re=m_s5_c_re, s5_c_im=m_s5_c_im, s5_d=m_s5_d, w_s5_glu=m_w_s5_glu, w_out=m_w_out, norm_ffn_w=m_norm_ffn_w, w_up=m_w_up, conv_ffn_w=m_conv_ffn_w, conv_ffn_b=m_conv_ffn_b, w_down=m_w_down, norm_final_w=m_norm_final_w)
    vars_ = dict(norm_mix_w=v_norm_mix_w, w_in=v_w_in, conv_a_w=v_conv_a_w, conv_a_b=v_conv_a_b, dt_bias=v_dt_bias, a_log=v_a_log, d_a=v_d_a, norm_a_w=v_norm_a_w, w_proj_a=v_w_proj_a, s5_lam_re=v_s5_lam_re, s5_lam_im=v_s5_lam_im, s5_log_dt=v_s5_log_dt, s5_b_re=v_s5_b_re, s5_b_im=v_s5_b_im, s5_c_re=v_s5_c_re, s5_c_im=v_s5_c_im, s5_d=v_s5_d, w_s5_glu=v_w_s5_glu, w_out=v_w_out, norm_ffn_w=v_norm_ffn_w, w_up=v_w_up, conv_ffn_w=v_conv_ffn_w, conv_ffn_b=v_conv_ffn_b, w_down=v_w_down, norm_final_w=v_norm_final_w)
    names = list(weights)
    col_sharded = ("w_in", "w_s5_glu", "w_up")
    row_sharded = ("w_proj_a", "w_out", "w_down")
    conv_sharded = ("conv_a_w", "conv_ffn_w")
    replicated = [n for n in names if n not in col_sharded + row_sharded + conv_sharded]

    t, d = x.shape[1:]
    x2, tgt = x.reshape(t, d), loss_target.reshape(t, d)
    nh = dt_bias.shape[-1]
    d_inner = norm_a_w.shape[-1]
    conv_dim = conv_a_b.shape[-1]
    gn = (conv_dim - d_inner) // 2
    ng = gn // D_STATE
    nr = nh // ng
    rp = d_inner // ng
    d_s5 = s5_d.shape[-1]
    gs, ps = s5_lam_re.shape[1:]
    cs = d_s5 // gs
    n_oct = gs // 8
    assert (gs * ps) % SCAN_COLS == 0 and 8 * cs == LANES and gs % 8 == 0
    d_ff = w_down.shape[1] * N_DEV
    dev = _flat(*_position())
    dev1 = dev.reshape(1).astype(jnp.int32)

    ka, kf = conv_a_w.shape[1], conv_ffn_w.shape[1]
    taps = jnp.concatenate([conv_a_w[0].reshape(1, -1), conv_ffn_w[0].reshape(1, -1)], axis=1)
    taps, taps_done = _all_gather(taps, name="ag_conv_taps")
    taps = taps[:, 0]

    def by_cols(n):
        return n in ("w_s5_glu", "w_up") and weights[n].shape[2] % LANES == 0

    chip_stage, t1 = _exchange_start(_after(w_in[0], taps_done).astype(BF16), "gather_chip", dev1, name="ag_w_in")
    w1 = norm_mix_w.reshape(1, 1, d) + t1[0, 0]
    hn1, = _blocked_fwd(_rms_fn, [x2], [w1], [(d, BF16)], tb=256, name="rms1")
    others = ("w_proj_a", "w_s5_glu", "w_out", "w_up", "w_down")
    gather_mode = {n: "gather_cols" if by_cols(n) else "gather_slot" for n in others}
    shards = {n: _after(weights[n][0], t1).astype(BF16) for n in others}
    lands = {n: _place_own(shards[n], gather_mode[n], dev1, name="ag_" + n + "_own") for n in others}
    w_in_2d = [a.reshape(a.shape[-2:]) for a in (w_in, m_w_in, v_w_in)]

    lam_re3, lam_im3 = s5_lam_re[0][:, None, :], s5_lam_im[0][:, None, :]
    logdt3 = _after(s5_log_dt[0][:, None, None], t1)
    bt_re, bt_im = jnp.transpose(s5_b_re[0], (0, 2, 1)), jnp.transpose(s5_b_im[0], (0, 2, 1))
    lb_re3, lb_im3, bb_re, bb_im = _s5_params(lam_re3, lam_im3, logdt3, bt_re, bt_im, name="s5_params")
    eye = jnp.eye(8, dtype=F32)

    def diag_b(bt):
        return (bt.reshape(n_oct, 8, cs, 1, ps) * eye[None, :, None, :, None]).reshape(n_oct, 8 * cs, 8 * ps)

    def undiag_b(blk):
        return (blk.reshape(n_oct, 8, cs, 8, ps) * eye[None, :, None, :, None]).sum(axis=3).reshape(gs, cs, ps)

    def diag_c(cm):
        ct = jnp.transpose(cm.reshape(n_oct, 8, cs, ps), (0, 1, 3, 2))
        return (ct[:, :, :, None, :] * eye[None, :, None, :, None]).reshape(n_oct, 8 * ps, 8 * cs)

    def undiag_c(blk):
        ct = (blk.reshape(n_oct, 8, ps, 8, cs) * eye[None, :, None, :, None]).sum(axis=3)
        return jnp.transpose(ct, (0, 1, 3, 2)).reshape(gs, cs, ps)

    b_blk_re, b_blk_im = diag_b(bb_re), diag_b(bb_im)
    c_blk_re, c_blk_imn = diag_c(s5_c_re[0]), diag_c(-s5_c_im[0])
    d3 = s5_d.reshape(n_oct, 1, LANES)
    lb_re, lb_im = lb_re3.reshape(1, gs * ps), lb_im3.reshape(1, gs * ps)

    ready = _touch(hn1, *lands.values(), *w_in_2d, b_blk_re, b_blk_im, c_blk_re, c_blk_imn, lb_re, lb_im,
                   name="ready_startup")
    forward_stage, t2 = _exchange_start(ready, "forward", dev1, land=_exchange_wait(chip_stage, ready, name="agw_w_in"),
                                        name="fw_w_in")
    w_in_blocks = _exchange_wait(forward_stage, t2, name="fww_w_in")
    pending, started = {}, t2
    for n in others:
        pending[n], token = _exchange_start(shards[n], gather_mode[n], dev1, land=lands[n], after=w_in_blocks,
                                            name="ag_" + n)
        started = started + token

    def gathered(n, after):
        g = _exchange_wait(pending[n], after, name="agw_" + n)
        if by_cols(n):
            return g
        if n in row_sharded:
            return g.reshape(-1, g.shape[2])
        return jnp.transpose(g, (1, 0, 2)).reshape(g.shape[1], -1)

    seg_sizes = dict(z=d_inner, xs=d_inner, bm=gn, cm=gn, dt=nh, u=d_s5, ga=d, gb=d)
    seg_names = tuple(seg_sizes)
    pieces, seg_at = _w_in_pieces(seg_sizes, ("z", "xs", "ga", "gb", "bm", "cm", "u", "dt"), w_in.shape[2])
    na = ka * conv_a_w.shape[2]
    cw_a = jnp.transpose(taps[:, :na].reshape(N_DEV, ka, -1), (1, 0, 2)).reshape(ka, conv_dim)
    cw_f = jnp.transpose(taps[:, na:].reshape(N_DEV, kf, -1), (1, 0, 2)).reshape(kf, 2 * d_ff)
    cb_a, cb_f = conv_a_b, conv_ffn_b
    a_cols = {"xs": slice(0, d_inner), "bm": slice(d_inner, d_inner + gn), "cm": slice(d_inner + gn, conv_dim)}

    w_in_p = _w_in_pack(w_in_blocks, pieces, seg_at, seg_sizes, started, name="w_in_pack")
    pre = {sn: _mm(hn1, w_in_p, b_win=seg_at[sn], name="in_" + sn) for sn in seg_names}
    act_a = {sn: _conv_fwd(_comb_silu, [pre[sn]], [cw_a[:, a_cols[sn]]], [cb_a[:, a_cols[sn]]], out_dtype=F32,
                           name="conv_a_" + sn) for sn in a_cols}
    dtr3 = jnp.transpose(pre["dt"][:, :nh].reshape(t, ng, nr), (1, 0, 2))
    dtb3, alog3, dsk3 = (p.reshape(ng, 1, nr) for p in (dt_bias, a_log, d_a))
    nw3 = norm_a_w.reshape(ng, 1, rp)
    yn, hsave = _ssd_fwd(act_a["xs"], act_a["bm"], act_a["cm"], pre["z"], dtr3, dtb3, alog3, dsk3, nw3, name="ssd")
    w_proj = gathered("w_proj_a", yn)
    y_a = _mm(yn, w_proj, name="proj_a")

    u = pre["u"]
    bu_re, bu_im = _blocked_fwd(_s5_bu_fn, [u], [b_blk_re, b_blk_im], [(gs * ps, F32)] * 2, nj=n_oct, tb=512,
                                name="s5_bu")
    s_re, s_im = _s5_scan_fwd(bu_re, bu_im, lb_re, lb_im, name="s5_scan")
    yb, = _blocked_fwd(_s5_out_fn, [s_re, s_im, u], [c_blk_re, c_blk_imn, d3], [(d_s5, BF16)], nj=n_oct, tb=512,
                       name="s5_out")
    w_glu = gathered("w_s5_glu", yb)
    glu_v = _mm(yb, w_glu, b_win=(0, d), name="glu_v")
    glu_g = _mm(yb, w_glu, b_win=(d, d), name="glu_g")
    merged, = _blocked_fwd(_merge_fn, [glu_v, glu_g, pre["ga"], pre["gb"], y_a], [], [(d, BF16)], tb=256,
                           name="merge")
    w_o = gathered("w_out", merged)
    h1 = _mm(merged, w_o, acc=x2, name="out_proj")
    w2 = norm_ffn_w.reshape(1, 1, d)
    hn2, = _blocked_fwd(_rms_fn, [h1], [w2], [(d, BF16)], tb=256, name="rms2")
    w_u = gathered("w_up", hn2)
    up_g = _mm(hn2, w_u, b_win=(0, d_ff), name="up_g")
    up_v = _mm(hn2, w_u, b_win=(d_ff, d_ff), name="up_v")
    f_w = [cw_f[:, :d_ff], cw_f[:, d_ff:]]
    f_b = [cb_f[:, :d_ff], cb_f[:, d_ff:]]
    act = _conv_fwd(_comb_glu, [up_g, up_v], f_w, f_b, out_dtype=BF16, name="conv_ffn")
    w_dn = gathered("w_down", act)
    h2 = _mm(act, w_dn, acc=h1, name="down")
    loss_tile, dh2, dh2_b, g_final = _loss_head(h2, norm_final_w.reshape(1, d), tgt, name="loss_head")

    grads, scattering = {}, {}

    def scatter_start(n, g):
        if by_cols(n):
            src, mode = g, "scatter_cols"
        elif n in row_sharded:
            src, mode = g.reshape(N_DEV, -1, g.shape[1]), "scatter_slot"
        elif n == "w_in":
            src, mode = g, "scatter_slot"
        else:
            src, mode = jnp.transpose(g.reshape(g.shape[0], N_DEV, -1), (1, 0, 2)), "scatter_slot"
        scattering[n], token = _exchange_start(src, mode, dev1, name="rs_" + n)
        return token

    d_act = _mm(dh2_b, w_dn, tb=True, name="d_act")
    g_down = _mm(act, dh2_b, ta=True, out_dtype=BF16, name="g_w_down")
    tok = scatter_start("w_down", g_down)
    (dup_g, dwf_g, dbf_g), (dup_v, dwf_v, dbf_v) = _conv_bwd(
        _comb_glu, [up_g, up_v], f_w, [_after(f_b[0], tok), f_b[1]], d_act, dx_dtype=BF16, name="conv_ffn_bwd")
    dhn2 = _mm(dup_g, w_u, tb=True, b_win=(0, d_ff), name="d_hn2_g")
    dhn2 = _mm(dup_v, w_u, tb=True, b_win=(d_ff, d_ff), acc=dhn2, name="d_hn2_v")
    g_up = _mm(hn2, dup_g, ta=True, into=(lax.empty((d, 2 * d_ff), BF16), 0), name="g_w_up_g")
    g_up = _mm(hn2, dup_v, ta=True, into=(g_up, d_ff), name="g_w_up_v")
    tok = scatter_start("w_up", g_up)
    dh1, dh1_b, g_w2 = _blocked_bwd(_rms_fn, [h1], [_after(w2, tok)], [dhn2], [(F32, BF16)], adds={0: dh2}, tb=256,
                                    name="rms2_bwd")
    d_merged = _mm(dh1_b, w_o, tb=True, name="d_merged")
    g_out = _mm(merged, dh1_b, ta=True, out_dtype=BF16, name="g_w_out")
    tok = scatter_start("w_out", g_out)
    dglu_v, dglu_g, dga, dgb, dy_a = _blocked_bwd(
        _merge_fn, [glu_v, glu_g, pre["ga"], pre["gb"], y_a], [], [d_merged], [BF16] * 5, tb=128, name="merge_bwd")
    dyb = _mm(dglu_v, w_glu, tb=True, b_win=(0, d), name="d_yb_v")
    dyb = _mm(dglu_g, w_glu, tb=True, b_win=(d, d), acc=dyb, name="d_yb_g")
    g_glu = _mm(yb, dglu_v, ta=True, into=(lax.empty((d_s5, 2 * d), BF16), 0), name="g_w_glu_v")
    g_glu = _mm(yb, dglu_g, ta=True, into=(g_glu, d), name="g_w_glu_g")
    tok = tok + scatter_start("w_s5_glu", g_glu)
    ds_re, ds_im, du_skip, dc_blk_re, dc_blk_imn, dd3 = _blocked_bwd(
        _s5_out_fn, [s_re, s_im, u], [c_blk_re, c_blk_imn, _after(d3, tok)], [dyb], [F32, F32, F32], nj=n_oct, tb=512,
        name="s5_out_bwd")
    dbu_re, dbu_im, dlb_re, dlb_im = _s5_scan_bwd(s_re, s_im, ds_re, ds_im, lb_re, lb_im, name="s5_scan_bwd")
    du, db_blk_re, db_blk_im = _blocked_bwd(
        _s5_bu_fn, [u], [b_blk_re, b_blk_im], [dbu_re, dbu_im], [BF16], adds={0: du_skip}, nj=n_oct, tb=512,
        name="s5_bu_bwd")
    g_lre, g_lim, g_ldt, g_bt_re, g_bt_im = _s5_params(
        lam_re3, lam_im3, logdt3, bt_re, bt_im,
        cts=(dlb_re.reshape(gs, 1, ps), dlb_im.reshape(gs, 1, ps), undiag_b(db_blk_re), undiag_b(db_blk_im)),
        name="s5_params_bwd")
    grads["s5_lam_re"], grads["s5_lam_im"] = g_lre.reshape(s5_lam_re.shape), g_lim.reshape(s5_lam_im.shape)
    grads["s5_log_dt"] = g_ldt.reshape(s5_log_dt.shape)
    grads["s5_b_re"] = jnp.transpose(g_bt_re, (0, 2, 1)).reshape(s5_b_re.shape)
    grads["s5_b_im"] = jnp.transpose(g_bt_im, (0, 2, 1)).reshape(s5_b_im.shape)
    grads["s5_c_re"] = undiag_c(dc_blk_re).reshape(s5_c_re.shape)
    grads["s5_c_im"] = -undiag_c(dc_blk_imn).reshape(s5_c_im.shape)
    grads["s5_d"] = dd3.reshape(s5_d.shape)

    dyn = _mm(dy_a, w_proj, tb=True, name="d_yn")
    g_proj = _mm(yn, dy_a, ta=True, out_dtype=BF16, name="g_w_proj_a")
    tok = scatter_start("w_proj_a", g_proj)
    dxs, dbm, dcm, dz, ddtr3, g_dtb, g_alog, g_dsk, g_nw = _ssd_bwd(
        act_a["xs"], act_a["bm"], act_a["cm"], pre["z"], dtr3, hsave, dtb3, alog3, dsk3, _after(nw3, tok), dyn,
        name="ssd_bwd")
    grads["dt_bias"], grads["a_log"], grads["d_a"] = (g.reshape(1, nh) for g in (g_dtb, g_alog, g_dsk))
    grads["norm_a_w"] = g_nw.reshape(1, d_inner)
    dpre = {"z": dz, "u": du, "ga": dga, "gb": dgb}
    dcw, dcb = {}, {}
    for sn, dact in (("xs", dxs), ("bm", dbm), ("cm", dcm)):
        (dpre[sn], dcw[sn], dcb[sn]), = _conv_bwd(
            _comb_silu, [pre[sn]], [cw_a[:, a_cols[sn]]], [cb_a[:, a_cols[sn]]], dact, dx_dtype=BF16,
            name="conv_a_bwd_" + sn)
    dpre["dt"] = _pad_cols(jnp.transpose(ddtr3, (1, 0, 2)).reshape(t, nh), LANES).astype(BF16)
    g_in = _w_in_unpack({sn: _mm(hn1, dpre[sn], ta=True, name="g_w_in_" + sn) for sn in seg_names}, pieces,
                        w_in.shape[2], name="w_in_unpack")
    tok = scatter_start("w_in", g_in)
    dhn1 = _mm(_after(dpre["dt"], tok), w_in_p, tb=True, b_win=seg_at["dt"], name="d_hn1_dt")
    for sn in seg_names:
        if sn != "dt":
            dhn1 = _mm(dpre[sn], w_in_p, tb=True, b_win=seg_at[sn], acc=dhn1, name="d_hn1_" + sn)
    dx, g_w1 = _blocked_bwd(_rms_fn, [x2], [w1], [dhn1], [F32], adds={0: dh1}, tb=256, name="rms1_bwd")

    grads["norm_mix_w"], grads["norm_ffn_w"] = g_w1.reshape(1, d), g_w2.reshape(1, d)
    grads["norm_final_w"] = g_final.reshape(d)
    grads["conv_a_b"] = jnp.concatenate([dcb["xs"], dcb["bm"], dcb["cm"]], axis=1)
    grads["conv_ffn_b"] = jnp.concatenate([dbf_g, dbf_v], axis=1)
    g_cw_a = jnp.concatenate([dcw["xs"], dcw["bm"], dcw["cm"]], axis=1)
    g_cw_f = jnp.concatenate([dwf_g, dwf_v], axis=1)

    small = [grads[n] for n in replicated] + [g_cw_a, g_cw_f, loss_tile[:1, :1]]
    packed, sizes = _pack(small, PACK_COLS)
    small_scatter, _ = _exchange_start(packed.reshape(N_DEV, -1, PACK_COLS), "scatter_slot", dev1, name="rs_small")
    eighth = _sum_parts(_exchange_wait(small_scatter, packed, name="rsw_small"), name="sum_small_grads")
    summed = _all_gather(eighth, name="ag_small_grads")[0].reshape(-1, PACK_COLS)
    *rep_sums, s_cw_a, s_cw_f, loss = _unpack(summed, sizes, [a.shape for a in small])
    for n, g in zip(replicated, rep_sums):
        grads[n] = g
    wa, wf = conv_a_w.shape[2], conv_ffn_w.shape[2]
    grads["conv_a_w"] = lax.dynamic_slice_in_dim(s_cw_a, dev * wa, wa, axis=1)[None]
    grads["conv_ffn_w"] = lax.dynamic_slice_in_dim(s_cw_f, dev * wf, wf, axis=1)[None]

    delta, new_m, new_v = {}, {}, {}
    done = dx
    for n in ("w_down", "w_up", "w_out", "w_s5_glu", "w_proj_a", "w_in"):
        shape = weights[n].shape
        two_d = lambda a: a.reshape(shape[-2], shape[-1])
        w2, m2, v2 = two_d(weights[n]), two_d(moms[n]), two_d(vars_[n])
        ready = _touch(done, w2, m2, v2, *((packed,) if n == "w_in" else ()), name="ready_" + n)
        land = _exchange_wait(scattering[n], ready, name="rsw_" + n)
        g = _sum_parts(land, name="rs_sum_" + n)
        grads[n] = g.reshape(shape)
        dl, nm, nv = _adamw(w2, g, m2, v2, name="adamw_" + n)
        delta[n], new_m[n], new_v[n] = dl.reshape(shape), nm.reshape(shape), nv.reshape(shape)
        done = dl
    for n in replicated + list(conv_sharded):
        shape = weights[n].shape
        two_d = lambda a: a.reshape(-1, shape[-1])
        dl, nm, nv = _adamw(two_d(weights[n]), two_d(grads[n]), two_d(moms[n]), two_d(vars_[n]), name="adamw_" + n)
        delta[n], new_m[n], new_v[n] = dl.reshape(shape), nm.reshape(shape), nv.reshape(shape)

    return (loss.reshape(()), dx.reshape(x.shape), *[grads[n] for n in names], *[delta[n] for n in names],
            *[new_m[n] for n in names], *[new_v[n] for n in names])
```

```python
import functools

import jax
import jax.numpy as jnp
from jax import lax
from jax.experimental import pallas as pl
from jax.experimental.pallas import tpu as pltpu

F32 = jnp.float32
BF16 = jnp.bfloat16
HIGHEST = lax.Precision.HIGHEST
MESH = pl.DeviceIdType.MESH

EPS = 1e-6
EIG_MAX = -1e-4
D_STATE = 128
CHUNK = 256
ADAM_LR = 0.001
ADAM_B1 = 0.9
ADAM_B2 = 0.999
ADAM_EPS = 1e-08
ADAM_WD = 0.01
ADAM_STEP = 10
N_DEV = 8
LANES = 128
SUBLANES = 8
VMEM_LIMIT = 56 * 1024 * 1024
MM_MAX_K = 4096


def _cp(*sem):
    return pltpu.CompilerParams(dimension_semantics=sem, vmem_limit_bytes=VMEM_LIMIT)


def _tile(dim, pref, unit=LANES):
    if dim <= unit:
        return dim
    t = (min(pref, dim) // unit) * unit
    while dim % t:
        t -= unit
    return t


_DIMS = {"nn": (((1,), (0,)), ((), ())), "nt": (((1,), (1,)), ((), ())), "tn": (((0,), (0,)), ((), ()))}


def _dot(a, b, kind):
    return lax.dot_general(a.astype(BF16), b.astype(BF16), _DIMS[kind], preferred_element_type=F32)


@functools.partial(jax.custom_vjp, nondiff_argnums=(2,))
def _bdot(a, b, kind):
    return _dot(a, b, kind)


def _bdot_fwd(a, b, kind):
    return _dot(a, b, kind), (a, b)


def _bdot_bwd(kind, res, g):
    a, b = res
    if kind == "nn":
        return _dot(g, b, "nt"), _dot(a, g, "tn")
    if kind == "nt":
        return _dot(g, b, "nn"), _dot(g, a, "tn")
    return _dot(b, g, "nt"), _dot(a, g, "nn")


_bdot.defvjp(_bdot_fwd, _bdot_bwd)


def _mm(a, b, *, ta=False, tb=False, acc=None, out_dtype=F32, name, b_win=None, into=None):
    assert not (ta and tb)
    m, k = (a.shape[1], a.shape[0]) if ta else a.shape
    b_off, b_size = b_win or (0, b.shape[1])
    n = b.shape[0] if tb else b_size
    assert (b_size if tb else b.shape[0]) == k, (a.shape, b.shape, ta, tb, b_win)
    o_off = into[1] if into else 0
    nk = -(-k // MM_MAX_K)
    while k % nk or (k // nk) % LANES or (tb and b_off % (k // nk)):
        nk += 1
    tk = k // nk
    tm, tn = _tile(m, 1024), _tile(n, 1024)
    while o_off % tn or (not tb and b_off % tn):
        tn = _tile(n, tn - LANES)
    kind = "tn" if ta else ("nt" if tb else "nn")
    a_spec = pl.BlockSpec((tk, tm), lambda i, j, l: (l, i)) if ta else pl.BlockSpec((tm, tk), lambda i, j, l: (i, l))
    if tb:
        b_spec = pl.BlockSpec((tn, tk), lambda i, j, l: (j, l + b_off // tk))
    else:
        b_spec = pl.BlockSpec((tk, tn), lambda i, j, l: (l, j + b_off // tn))
    c_spec = pl.BlockSpec((tm, tn), lambda i, j, l: (i, j))
    o_spec = pl.BlockSpec((tm, tn), lambda i, j, l: (i, j + o_off // tn))
    has_acc = acc is not None

    def body(*refs):
        a_ref, b_ref = refs[:2]
        c_ref = refs[2] if has_acc else None
        o_ref = refs[2 + has_acc + (into is not None)]
        if nk == 1:
            res = _dot(a_ref[...], b_ref[...], kind)
            if has_acc:
                res = res + c_ref[...].astype(F32)
            o_ref[...] = res.astype(o_ref.dtype)
            return
        acc_ref = refs[-1]
        l = pl.program_id(2)

        @pl.when(l == 0)
        def _():
            if has_acc:
                acc_ref[...] = c_ref[...].astype(F32)
            else:
                acc_ref[...] = jnp.zeros_like(acc_ref)

        acc_ref[...] += _dot(a_ref[...], b_ref[...], kind)

        @pl.when(l == nk - 1)
        def _():
            o_ref[...] = acc_ref[...].astype(o_ref.dtype)

    ins = [a, b] + ([acc] if has_acc else []) + ([into[0]] if into else [])
    in_specs = [a_spec, b_spec] + ([c_spec] if has_acc else []) + ([pl.BlockSpec(memory_space=pl.ANY)] if into else [])
    out_shape = jax.ShapeDtypeStruct(into[0].shape, into[0].dtype) if into else jax.ShapeDtypeStruct((m, n), out_dtype)
    return pl.pallas_call(
        body, name=name, grid=(m // tm, n // tn, nk), in_specs=in_specs, out_specs=o_spec, out_shape=out_shape,
        input_output_aliases={len(ins) - 1: 0} if into else {},
        scratch_shapes=[pltpu.VMEM((tm, tn), F32)] if nk > 1 else [],
        compiler_params=_cp("parallel", "parallel", "arbitrary"),
    )(*ins)


def _w_in_pieces(seg_sizes, seg_order, n_blk):
    layout, o = {}, 0
    for sn in seg_order:
        width = -(-seg_sizes[sn] // LANES) * LANES
        layout[sn] = (o, width)
        o += width
    pieces, start = [], 0
    for sn, sz in seg_sizes.items():
        lo = start
        while lo < start + sz:
            blk = lo // n_blk
            hi = min(start + sz, (blk + 1) * n_blk)
            pieces.append((blk, lo - blk * n_blk, sn, lo - start, layout[sn][0] + lo - start, hi - lo))
            lo = hi
        start += sz
    return pieces, layout


def _w_in_pack(gathered, pieces, layout, seg_sizes, after, *, name, tr=256):
    _, k, n_blk = gathered.shape
    n_pad = sum(w for _, w in layout.values())

    def body(g_ref, after_ref, o_ref):
        for sn, (off, width) in layout.items():
            if width != seg_sizes[sn]:
                o_ref[:, pl.ds(off + seg_sizes[sn], width - seg_sizes[sn])] = jnp.zeros(
                    (tr, width - seg_sizes[sn]), o_ref.dtype)
        for blk, src, _, _, dst, width in pieces:
            o_ref[:, pl.ds(dst, width)] = g_ref[blk, :, pl.ds(src, width)]

    return pl.pallas_call(
        body, name=name, grid=(k // tr,),
        in_specs=[pl.BlockSpec((N_DEV, tr, n_blk), lambda i: (0, i, 0)), pl.BlockSpec(memory_space=pl.ANY)],
        out_specs=pl.BlockSpec((tr, n_pad), lambda i: (i, 0)), out_shape=jax.ShapeDtypeStruct((k, n_pad), gathered.dtype),
        compiler_params=_cp("parallel"),
    )(gathered, after)


def _w_in_unpack(seg_grads, pieces, n_blk, *, name, tr=128):
    names = list(seg_grads)
    k = seg_grads[names[0]].shape[0]

    def body(*refs):
        o_ref = refs[-1]
        seg_ref = dict(zip(names, refs))
        for blk, dst, sn, src, _, width in pieces:
            o_ref[blk, :, pl.ds(dst, width)] = seg_ref[sn][:, pl.ds(src, width)].astype(o_ref.dtype)

    return pl.pallas_call(
        body, name=name, grid=(k // tr,),
        in_specs=[pl.BlockSpec((tr, seg_grads[sn].shape[1]), lambda i: (i, 0)) for sn in names],
        out_specs=pl.BlockSpec((N_DEV, tr, n_blk), lambda i: (0, i, 0)),
        out_shape=jax.ShapeDtypeStruct((N_DEV, k, n_blk), BF16), compiler_params=_cp("parallel"),
    )(*[seg_grads[sn] for sn in names])


def _row_spec(arr, tb, nj):
    return pl.BlockSpec((tb, arr.shape[1] // nj), lambda j, i: (i, j))


def _par_spec(arr):
    return pl.BlockSpec((1,) + arr.shape[1:], lambda j, i: (j, 0, 0))


def _blocked_fwd(fn, rows, params, outs, *, nj=1, tb, name):
    t = rows[0].shape[0]
    nr, npar = len(rows), len(params)

    def body(*refs):
        res = fn(*[r[...] for r in refs[:nr]], *[p[0] for p in refs[nr:nr + npar]])
        for o_ref, val in zip(refs[nr + npar:], res):
            o_ref[...] = val.astype(o_ref.dtype)

    return pl.pallas_call(
        body, name=name, grid=(nj, t // tb),
        in_specs=[_row_spec(a, tb, nj) for a in rows] + [_par_spec(p) for p in params],
        out_specs=[pl.BlockSpec((tb, c // nj), lambda j, i: (i, j)) for c, _ in outs],
        out_shape=[jax.ShapeDtypeStruct((t, c), dt) for c, dt in outs],
        compiler_params=_cp("parallel", "arbitrary"),
    )(*rows, *params)


def _blocked_bwd(fn, rows, params, cts, row_grad_dtypes, *, adds=None, nj=1, tb, name):
    t = rows[0].shape[0]
    nr, npar, nct = len(rows), len(params), len(cts)
    adds = adds or {}
    add_keys = sorted(adds)
    want, want_dtypes = [], []
    for k, dts in enumerate(row_grad_dtypes):
        for dt in (dts if isinstance(dts, tuple) else (dts,)):
            if dt is not None:
                want.append(k)
                want_dtypes.append(dt)

    def body(*refs):
        row_refs = refs[:nr]
        par_refs = refs[nr:nr + npar]
        ct_refs = refs[nr + npar:nr + npar + nct]
        add_refs = dict(zip(add_keys, refs[nr + npar + nct:nr + npar + nct + len(add_keys)]))
        out_refs = refs[nr + npar + nct + len(add_keys):]
        _, vjp = jax.vjp(fn, *[r[...] for r in row_refs], *[p[0] for p in par_refs])
        grads = vjp(tuple(c[...].astype(F32) for c in ct_refs))
        for o_ref, k in zip(out_refs, want):
            g = grads[k]
            if k in add_refs:
                g = g + add_refs[k][...].astype(F32)
            o_ref[...] = g.astype(o_ref.dtype)
        first = pl.program_id(1) == 0
        for o_ref, g in zip(out_refs[len(want):], grads[nr:]):
            @pl.when(first)
            def _(o_ref=o_ref):
                o_ref[...] = jnp.zeros_like(o_ref)
            o_ref[0] += g

    add_arrs = [adds[k] for k in add_keys]
    return pl.pallas_call(
        body, name=name, grid=(nj, t // tb),
        in_specs=[_row_spec(a, tb, nj) for a in rows] + [_par_spec(p) for p in params]
        + [_row_spec(c, tb, nj) for c in cts] + [_row_spec(a, tb, nj) for a in add_arrs],
        out_specs=[_row_spec(rows[k], tb, nj) for k in want] + [_par_spec(p) for p in params],
        out_shape=[jax.ShapeDtypeStruct(rows[k].shape, dt) for k, dt in zip(want, want_dtypes)]
        + [jax.ShapeDtypeStruct(p.shape, F32) for p in params],
        compiler_params=_cp("parallel", "arbitrary"),
    )(*rows, *params, *cts, *add_arrs)


def _rms_fn(x, w):
    return (x * lax.rsqrt(jnp.mean(x * x, axis=-1, keepdims=True) + EPS) * w,)


def _silu(x):
    return x * jax.nn.sigmoid(x)


def _merge_fn(glu_v, glu_g, g_a, g_b, y_a):
    y_b = glu_v * jax.nn.sigmoid(glu_g)
    return (jax.nn.sigmoid(g_a) * y_a + jax.nn.sigmoid(g_b) * y_b,)


def _s5_bu_fn(u, b_re, b_im):
    return _bdot(u, b_re, "nn"), _bdot(u, b_im, "nn")


def _s5_out_fn(s_re, s_im, u, c_re, c_im_neg, d):
    return (jax.nn.gelu(_bdot(s_re, c_re, "nn") + _bdot(s_im, c_im_neg, "nn") + d * u),)


HALO = SUBLANES


STRIP = 64


def _conv_strip(ext_ref, w_ref, b_ref, r0, cols):
    kw = w_ref.shape[0]
    xs = [ext_ref[pl.ds(r0 + HALO - kw + 1 + k, STRIP), cols] for k in range(kw)]
    c = b_ref[:, cols] + w_ref[0:1, cols] * xs[0]
    for k in range(1, kw):
        c = c + w_ref[k:k + 1, cols] * xs[k]
    return c, xs


def _fold(x):
    return x.reshape(STRIP // SUBLANES, SUBLANES, LANES).sum(axis=0)


def _conv_specs(xs, ws, bs, tb, cb, time_of):
    specs = []
    for x, w, b in zip(xs, ws, bs):
        specs += [
            pl.BlockSpec((HALO, cb), lambda j, i: (jnp.maximum(time_of(i) * (tb // HALO) - 1, 0), j)),
            pl.BlockSpec((tb, cb), lambda j, i: (time_of(i), j)),
            pl.BlockSpec((w.shape[0], cb), lambda j, i: (0, j)),
            pl.BlockSpec((1, cb), lambda j, i: (0, j)),
        ]
    return specs


def _conv_fwd(comb, xs, ws, bs, *, out_dtype, name, tb=1024):
    t, c = xs[0].shape
    cb = _tile(c, 512)
    ns = len(xs)

    def body(*refs):
        i = pl.program_id(1)
        o_ref = refs[4 * ns]
        exts = refs[4 * ns + 1:]
        for s in range(ns):
            xp_ref, xm_ref = refs[4 * s:4 * s + 2]
            exts[s][pl.ds(0, HALO), :] = jnp.where(i == 0, 0.0, xp_ref[...])
            exts[s][pl.ds(HALO, tb), :] = xm_ref[...]
        for c0 in range(0, cb, LANES):
            cols = pl.ds(c0, LANES)
            for r0 in range(0, tb, STRIP):
                cs = [_conv_strip(exts[s], refs[4 * s + 2], refs[4 * s + 3], r0, cols)[0] for s in range(ns)]
                o_ref[pl.ds(r0, STRIP), cols] = comb(*cs).astype(out_dtype)

    flat = [a for x, w, b in zip(xs, ws, bs) for a in (x, x, w, b)]
    return pl.pallas_call(
        body, name=name, grid=(c // cb, t // tb),
        in_specs=_conv_specs(xs, ws, bs, tb, cb, lambda i: i),
        out_specs=pl.BlockSpec((tb, cb), lambda j, i: (i, j)),
        out_shape=jax.ShapeDtypeStruct((t, c), out_dtype),
        scratch_shapes=[pltpu.VMEM((HALO + tb, cb), F32) for _ in range(ns)],
        compiler_params=_cp("parallel", "arbitrary"),
    )(*flat)


def _conv_bwd(comb, xs, ws, bs, dy, *, dx_dtype, name, tb=1024):
    t, c = xs[0].shape
    cb = _tile(c, 512)
    ns = len(xs)
    nt = t // tb
    kw = ws[0].shape[0]

    def body(*refs):
        step = pl.program_id(1)
        dy_ref = refs[4 * ns]
        out_refs = refs[4 * ns + 1:4 * ns + 1 + 3 * ns]
        scratch = refs[4 * ns + 1 + 3 * ns:]
        exts, dcs, carries = scratch[:ns], scratch[ns:2 * ns], scratch[2 * ns:]

        @pl.when(step == 0)
        def _():
            for s in range(ns):
                carries[s][...] = jnp.zeros_like(carries[s])
                out_refs[3 * s + 1][...] = jnp.zeros_like(out_refs[3 * s + 1])
                out_refs[3 * s + 2][...] = jnp.zeros_like(out_refs[3 * s + 2])

        for s in range(ns):
            xp_ref, xm_ref = refs[4 * s:4 * s + 2]
            exts[s][pl.ds(0, HALO), :] = jnp.where(step == nt - 1, 0.0, xp_ref[...])
            exts[s][pl.ds(HALO, tb), :] = xm_ref[...]
            dcs[s][pl.ds(tb, HALO), :] = carries[s][...]
        for c0 in range(0, cb, LANES):
            cols = pl.ds(c0, LANES)
            acc_w = [[jnp.zeros((SUBLANES, LANES), F32) for _ in range(kw)] for _ in range(ns)]
            acc_b = [jnp.zeros((SUBLANES, LANES), F32) for _ in range(ns)]
            for r0 in range(0, tb, STRIP):
                strips = [_conv_strip(exts[s], refs[4 * s + 2], refs[4 * s + 3], r0, cols) for s in range(ns)]
                _, vjp = jax.vjp(comb, *[cs for cs, _ in strips])
                grads = vjp(dy_ref[pl.ds(r0, STRIP), cols].astype(F32))
                for s in range(ns):
                    dcs[s][pl.ds(r0, STRIP), cols] = grads[s]
                    acc_b[s] = acc_b[s] + _fold(grads[s])
                    for k in range(kw):
                        acc_w[s][k] = acc_w[s][k] + _fold(grads[s] * strips[s][1][k])
            for s in range(ns):
                dw_ref, db_ref = out_refs[3 * s + 1], out_refs[3 * s + 2]
                db_ref[:, cols] += jnp.sum(acc_b[s], axis=0, keepdims=True)
                for k in range(kw):
                    dw_ref[k:k + 1, cols] += jnp.sum(acc_w[s][k], axis=0, keepdims=True)
        for s in range(ns):
            w_ref, dx_ref = refs[4 * s + 2], out_refs[3 * s]
            for c0 in range(0, cb, LANES):
                cols = pl.ds(c0, LANES)
                for r0 in range(0, tb, STRIP):
                    dx = w_ref[kw - 1:kw, cols] * dcs[s][pl.ds(r0, STRIP), cols]
                    for k in range(kw - 1):
                        dx = dx + w_ref[k:k + 1, cols] * dcs[s][pl.ds(r0 + kw - 1 - k, STRIP), cols]
                    dx_ref[pl.ds(r0, STRIP), cols] = dx.astype(dx_dtype)
            carries[s][...] = dcs[s][pl.ds(0, HALO), :]

    flat = [a for x, w, b in zip(xs, ws, bs) for a in (x, x, w, b)]
    rev = lambda i: nt - 1 - i
    out_specs, out_shape = [], []
    for x, w, b in zip(xs, ws, bs):
        out_specs += [pl.BlockSpec((tb, cb), lambda j, i: (rev(i), j)),
                      pl.BlockSpec((w.shape[0], cb), lambda j, i: (0, j)),
                      pl.BlockSpec((1, cb), lambda j, i: (0, j))]
        out_shape += [jax.ShapeDtypeStruct((t, c), dx_dtype), jax.ShapeDtypeStruct(w.shape, F32),
                      jax.ShapeDtypeStruct(b.shape, F32)]
    res = pl.pallas_call(
        body, name=name, grid=(c // cb, nt),
        in_specs=_conv_specs(xs, ws, bs, tb, cb, rev) + [pl.BlockSpec((tb, cb), lambda j, i: (rev(i), j))],
        out_specs=out_specs, out_shape=out_shape,
        scratch_shapes=[pltpu.VMEM((HALO + tb, cb), F32) for _ in range(2 * ns)]
        + [pltpu.VMEM((HALO, cb), F32) for _ in range(ns)],
        compiler_params=_cp("parallel", "arbitrary"),
    )(*flat, dy)
    return [tuple(res[3 * s:3 * s + 3]) for s in range(ns)]


def _comb_silu(c):
    return _silu(c)


def _comb_glu(cg, cv):
    return _silu(cg) * cv


def _ssd_fn(nheads, hdim):
    def fn(x, bm, cm, z, dtr, hin, dtb, alog, dsk, nw):
        q = x.shape[0]
        dt = jax.nn.softplus(dtr + dtb)
        da = dt * (-jnp.exp(alog))
        li = lax.broadcasted_iota(jnp.int32, (q, q), 0)
        si = lax.broadcasted_iota(jnp.int32, (q, q), 1)
        causal = li >= si
        tri = causal.astype(F32)
        acs = jnp.dot(tri, da, precision=HIGHEST, preferred_element_type=F32)
        acs_row = lax.dot_general(da, tri, (((0,), (1,)), ((), ())), precision=HIGHEST,
                                  preferred_element_type=F32)
        cb = _bdot(cm, bm, "nt")
        ch = _bdot(cm, hin, "nn")
        ys, hs = [], []
        for r in range(nheads):
            cols = slice(r * hdim, (r + 1) * hdim)
            xr = x[:, cols]
            a_col = acs[:, r:r + 1]
            decay = jnp.exp(jnp.where(causal, a_col - acs_row[r:r + 1, :], -1e30))
            xd = xr * dt[:, r:r + 1]
            y_diag = _bdot(cb * decay, xd, "nn")
            y_off = ch[:, cols] * jnp.exp(a_col)
            last = acs[q - 1:q, r:r + 1]
            st = _bdot(bm * jnp.exp(last - a_col), xd, "tn")
            hs.append(jnp.exp(last) * hin[:, cols] + st)
            ys.append(y_diag + y_off + dsk[:, r:r + 1] * xr)
        y = jnp.concatenate(ys, axis=1) * _silu(z)
        yn = y * lax.rsqrt(jnp.mean(y * y, axis=-1, keepdims=True) + EPS) * nw
        return yn, jnp.concatenate(hs, axis=1)
    return fn


def _ssd_specs(rp, nr, time_of):
    row = lambda w: pl.BlockSpec((CHUNK, w), lambda g, c: (time_of(c), g))
    par = lambda w: pl.BlockSpec((1, 1, w), lambda g, c: (g, 0, 0))
    return dict(
        x=row(rp), bc=row(D_STATE), dtr=pl.BlockSpec((1, CHUNK, nr), lambda g, c: (g, time_of(c), 0)),
        h=pl.BlockSpec((1, 1, D_STATE, rp), lambda g, c: (g, time_of(c), 0, 0)), pr=par(nr), pw=par(rp))


def _ssd_fwd(xs, bm, cm, z, dtr, dtb, alog, dsk, nw, *, name):
    t = xs.shape[0]
    g, _, nr = dtr.shape
    rp = xs.shape[1] // g
    nc = t // CHUNK
    fn = _ssd_fn(nr, rp // nr)
    sp = _ssd_specs(rp, nr, lambda c: c)

    def body(x_ref, b_ref, c_ref, z_ref, dtr_ref, dtb_ref, al_ref, dsk_ref, nw_ref, yn_ref, hs_ref, h_ref):
        @pl.when(pl.program_id(1) == 0)
        def _():
            h_ref[...] = jnp.zeros_like(h_ref)
        hin = h_ref[...]
        hs_ref[0, 0] = hin
        yn, hout = fn(x_ref[...], b_ref[...], c_ref[...], z_ref[...], dtr_ref[0], hin,
                      dtb_ref[0], al_ref[0], dsk_ref[0], nw_ref[0])
        yn_ref[...] = yn.astype(yn_ref.dtype)
        h_ref[...] = hout

    return pl.pallas_call(
        body, name=name, grid=(g, nc),
        in_specs=[sp["x"], sp["bc"], sp["bc"], sp["x"], sp["dtr"], sp["pr"], sp["pr"], sp["pr"], sp["pw"]],
        out_specs=[sp["x"], sp["h"]],
        out_shape=[jax.ShapeDtypeStruct(xs.shape, BF16), jax.ShapeDtypeStruct((g, nc, D_STATE, rp), F32)],
        scratch_shapes=[pltpu.VMEM((D_STATE, rp), F32)],
        compiler_params=_cp("parallel", "arbitrary"),
    )(xs, bm, cm, z, dtr, dtb, alog, dsk, nw)


def _ssd_bwd(xs, bm, cm, z, dtr, hsave, dtb, alog, dsk, nw, dyn, *, name):
    t = xs.shape[0]
    g, _, nr = dtr.shape
    rp = xs.shape[1] // g
    nc = t // CHUNK
    fn = _ssd_fn(nr, rp // nr)
    sp = _ssd_specs(rp, nr, lambda c: nc - 1 - c)

    def body(x_ref, b_ref, c_ref, z_ref, dtr_ref, hs_ref, dtb_ref, al_ref, dsk_ref, nw_ref, dyn_ref,
             dx_ref, db_ref, dc_ref, dz_ref, ddtr_ref, ddtb_ref, dal_ref, ddsk_ref, dnw_ref, dh_ref):
        first = pl.program_id(1) == 0

        @pl.when(first)
        def _():
            dh_ref[...] = jnp.zeros_like(dh_ref)
            for r in (ddtb_ref, dal_ref, ddsk_ref, dnw_ref):
                r[...] = jnp.zeros_like(r)

        _, vjp = jax.vjp(fn, x_ref[...], b_ref[...], c_ref[...], z_ref[...], dtr_ref[0], hs_ref[0, 0],
                         dtb_ref[0], al_ref[0], dsk_ref[0], nw_ref[0])
        dx, db, dc, dz, ddtr, dhin, ddtb, dal, ddsk, dnw = vjp((dyn_ref[...].astype(F32), dh_ref[...]))
        dx_ref[...] = dx
        db_ref[...] = db
        dc_ref[...] = dc
        dz_ref[...] = dz.astype(dz_ref.dtype)
        ddtr_ref[0] = ddtr
        dh_ref[...] = dhin
        ddtb_ref[0] += ddtb
        dal_ref[0] += dal
        ddsk_ref[0] += ddsk
        dnw_ref[0] += dnw

    sd = jax.ShapeDtypeStruct
    return pl.pallas_call(
        body, name=name, grid=(g, nc),
        in_specs=[sp["x"], sp["bc"], sp["bc"], sp["x"], sp["dtr"], sp["h"], sp["pr"], sp["pr"], sp["pr"], sp["pw"],
                  sp["x"]],
        out_specs=[sp["x"], sp["bc"], sp["bc"], sp["x"], sp["dtr"], sp["pr"], sp["pr"], sp["pr"], sp["pw"]],
        out_shape=[sd(xs.shape, F32), sd(bm.shape, F32), sd(cm.shape, F32), sd(z.shape, BF16), sd(dtr.shape, F32),
                   sd(dtb.shape, F32), sd(alog.shape, F32), sd(dsk.shape, F32), sd(nw.shape, F32)],
        scratch_shapes=[pltpu.VMEM((D_STATE, rp), F32)],
        compiler_params=_cp("parallel", "arbitrary"),
    )(xs, bm, cm, z, dtr, hsave, dtb, alog, dsk, nw, dyn)


def _s5_param_fn(lam_re, lam_im, log_dt, bt_re, bt_im):
    lr = jnp.minimum(lam_re, EIG_MAX)
    dt = jnp.exp(log_dt)
    mag = jnp.exp(lr * dt)
    lb_re = mag * jnp.cos(lam_im * dt)
    lb_im = mag * jnp.sin(lam_im * dt)
    n_re = lb_re - 1.0
    den = lr * lr + lam_im * lam_im
    k_re = (n_re * lr + lb_im * lam_im) / den
    k_im = (lb_im * lr - n_re * lam_im) / den
    return lb_re, lb_im, k_re * bt_re - k_im * bt_im, k_re * bt_im + k_im * bt_re


def _s5_params(lam_re, lam_im, log_dt, bt_re, bt_im, cts=None, *, name):
    args = (lam_re, lam_im, log_dt, bt_re, bt_im)
    n = len(args)

    def body(*refs):
        vals = [r[...] for r in refs[:n]]
        if cts is None:
            res = _s5_param_fn(*vals)
        else:
            _, vjp = jax.vjp(_s5_param_fn, *vals)
            res = vjp(tuple(r[...] for r in refs[n:n + 4]))
        for o_ref, v in zip(refs[-len(res):], res):
            o_ref[...] = v

    if cts is None:
        out = [lam_re, lam_im, bt_re, bt_im]
        ins = args
    else:
        out = list(args)
        ins = args + tuple(cts)
    return pl.pallas_call(
        body, name=name, out_shape=[jax.ShapeDtypeStruct(a.shape, F32) for a in out],
        compiler_params=pltpu.CompilerParams(vmem_limit_bytes=VMEM_LIMIT),
    )(*ins)


SCAN_COLS = 512


def _cmul(xr, xi, yr, yi):
    return xr * yr - xi * yi, xr * yi + xi * yr


def _scan_consts(a_re, a_im, cols, reverse):
    shape = (SUBLANES, cols)
    row = lax.broadcasted_iota(jnp.int32, shape, 0)
    dist = (SUBLANES - 1 - row) if reverse else row
    mr, mi = jnp.broadcast_to(a_re, shape), jnp.broadcast_to(a_im, shape)
    pr, pi = mr, mi
    mults = []
    for d in (1, 2, 4):
        mults.append((mr, mi))
        qr, qi = _cmul(pr, pi, mr, mi)
        has_bit = (dist & d) != 0
        pr, pi = jnp.where(has_bit, qr, pr), jnp.where(has_bit, qi, pi)
        mr, mi = _cmul(mr, mi, mr, mi)
    return mults, (pr, pi), dist


def _scan_group(xr, xi, consts, cr, ci, reverse):
    mults, (pr, pi), dist = consts
    for d, (mr, mi) in zip((1, 2, 4), mults):
        shift = (SUBLANES - d) if reverse else d
        sr = jnp.where(dist >= d, pltpu.roll(xr, shift, 0), 0.0)
        si = jnp.where(dist >= d, pltpu.roll(xi, shift, 0), 0.0)
        tr, ti = _cmul(mr, mi, sr, si)
        xr, xi = xr + tr, xi + ti
    last = slice(0, 1) if reverse else slice(SUBLANES - 1, SUBLANES)
    nr, ni = _cmul(pr[last], pi[last], cr, ci)
    tr, ti = _cmul(pr, pi, jnp.broadcast_to(cr, xr.shape), jnp.broadcast_to(ci, xr.shape))
    return xr + tr, xi + ti, xr[last] + nr, xi[last] + ni


def _scan_specs(tb, time_of):
    row = pl.BlockSpec((tb, SCAN_COLS), lambda j, i: (time_of(i), j))
    par = pl.BlockSpec((1, SCAN_COLS), lambda j, i: (0, j))
    return row, par


def _s5_scan_fwd(bu_re, bu_im, lb_re, lb_im, *, name, tb=1024):
    t, c = bu_re.shape
    nj = c // SCAN_COLS
    row, par = _scan_specs(tb, lambda i: i)

    def body(bre_ref, bim_ref, lre_ref, lim_ref, sre_ref, sim_ref, cre_ref, cim_ref):
        @pl.when(pl.program_id(1) == 0)
        def _():
            cre_ref[...] = jnp.zeros_like(cre_ref)
            cim_ref[...] = jnp.zeros_like(cim_ref)
        consts = _scan_consts(lre_ref[...], lim_ref[...], SCAN_COLS, False)

        def group(k, carry):
            rows = pl.ds(pl.multiple_of(k * SUBLANES, SUBLANES), SUBLANES)
            sr, si, cr, ci = _scan_group(bre_ref[rows, :], bim_ref[rows, :], consts, *carry, False)
            sre_ref[rows, :] = sr
            sim_ref[rows, :] = si
            return cr, ci

        sr, si = lax.fori_loop(0, tb // SUBLANES, group, (cre_ref[...], cim_ref[...]), unroll=4)
        cre_ref[...] = sr
        cim_ref[...] = si

    return pl.pallas_call(
        body, name=name, grid=(nj, t // tb), in_specs=[row, row, par, par], out_specs=[row, row],
        out_shape=[jax.ShapeDtypeStruct((t, c), F32)] * 2,
        scratch_shapes=[pltpu.VMEM((1, SCAN_COLS), F32)] * 2,
        compiler_params=_cp("parallel", "arbitrary"),
    )(bu_re, bu_im, lb_re, lb_im)


def _s5_scan_bwd(s_re, s_im, ds_re, ds_im, lb_re, lb_im, *, name, tb=1024):
    t, c = s_re.shape
    nj = c // SCAN_COLS
    nt = t // tb
    rev = lambda i: nt - 1 - i
    row, par = _scan_specs(tb, rev)
    prev = pl.BlockSpec((HALO, SCAN_COLS), lambda j, i: (jnp.maximum(rev(i) * (tb // HALO) - 1, 0), j))

    def body(sre_ref, sim_ref, pre_ref, pim_ref, dre_ref, dim_ref, lre_ref, lim_ref,
             gre_ref, gim_ref, dlre_ref, dlim_ref, cre_ref, cim_ref, ext_re, ext_im):
        step_id = pl.program_id(1)

        @pl.when(step_id == 0)
        def _():
            cre_ref[...] = jnp.zeros_like(cre_ref)
            cim_ref[...] = jnp.zeros_like(cim_ref)
            dlre_ref[...] = jnp.zeros_like(dlre_ref)
            dlim_ref[...] = jnp.zeros_like(dlim_ref)
        consts = _scan_consts(lre_ref[...], -lim_ref[...], SCAN_COLS, True)
        ngroups = tb // SUBLANES

        def group(k, carry):
            rows = pl.ds(pl.multiple_of((ngroups - 1 - k) * SUBLANES, SUBLANES), SUBLANES)
            gr, gi, cr, ci = _scan_group(dre_ref[rows, :], dim_ref[rows, :], consts, *carry, True)
            gre_ref[rows, :] = gr
            gim_ref[rows, :] = gi
            return cr, ci

        gr, gi = lax.fori_loop(0, ngroups, group, (cre_ref[...], cim_ref[...]), unroll=4)
        cre_ref[...] = gr
        cim_ref[...] = gi
        has_past = step_id != nt - 1
        ext_re[pl.ds(0, HALO), :] = jnp.where(has_past, pre_ref[...], 0.0)
        ext_im[pl.ds(0, HALO), :] = jnp.where(has_past, pim_ref[...], 0.0)
        ext_re[pl.ds(HALO, tb), :] = sre_ref[...]
        ext_im[pl.ds(HALO, tb), :] = sim_ref[...]
        pr, pi = ext_re[pl.ds(HALO - 1, tb), :], ext_im[pl.ds(HALO - 1, tb), :]
        g_re, g_im = gre_ref[...], gim_ref[...]
        dlre_ref[...] += jnp.sum(pr * g_re + pi * g_im, axis=0, keepdims=True)
        dlim_ref[...] += jnp.sum(pr * g_im - pi * g_re, axis=0, keepdims=True)

    return pl.pallas_call(
        body, name=name, grid=(nj, nt),
        in_specs=[row, row, prev, prev, row, row, par, par], out_specs=[row, row, par, par],
        out_shape=[jax.ShapeDtypeStruct((t, c), F32)] * 2 + [jax.ShapeDtypeStruct((1, c), F32)] * 2,
        scratch_shapes=[pltpu.VMEM((1, SCAN_COLS), F32)] * 2 + [pltpu.VMEM((HALO + tb, SCAN_COLS), F32)] * 2,
        compiler_params=_cp("parallel", "arbitrary"),
    )(s_re, s_im, s_re, s_im, ds_re, ds_im, lb_re, lb_im)


def _loss_fn(h, w, tgt):
    err = _rms_fn(h, w)[0] - tgt
    return 0.5 * jnp.sum(jnp.mean(err * err, axis=-1, keepdims=True), axis=0, keepdims=True)


def _loss_head(h, w, tgt, *, name, tb=256):
    t, d = h.shape

    def body(h_ref, w_ref, t_ref, loss_ref, dh_ref, dhb_ref, dw_ref):
        @pl.when(pl.program_id(0) == 0)
        def _():
            loss_ref[...] = jnp.zeros_like(loss_ref)
            dw_ref[...] = jnp.zeros_like(dw_ref)
        part, vjp = jax.vjp(_loss_fn, h_ref[...], w_ref[...], t_ref[...])
        dh, dw, _ = vjp(jnp.ones((1, 1), F32))
        loss_ref[...] += jnp.broadcast_to(part, loss_ref.shape)
        dh_ref[...] = dh
        dhb_ref[...] = dh.astype(BF16)
        dw_ref[...] += dw

    row = pl.BlockSpec((tb, d), lambda i: (i, 0))
    par = pl.BlockSpec((1, d), lambda i: (0, 0))
    return pl.pallas_call(
        body, name=name, grid=(t // tb,), in_specs=[row, par, row],
        out_specs=[pl.BlockSpec((SUBLANES, LANES), lambda i: (0, 0)), row, row, par],
        out_shape=[jax.ShapeDtypeStruct((SUBLANES, LANES), F32), jax.ShapeDtypeStruct((t, d), F32),
                   jax.ShapeDtypeStruct((t, d), BF16), jax.ShapeDtypeStruct((1, d), F32)],
        compiler_params=_cp("arbitrary"),
    )(h, w, tgt)


def _adamw(w, g, m, v, *, name):
    r, c = w.shape
    tr = _tile(r, 256, SUBLANES)

    def body(w_ref, g_ref, m_ref, v_ref, d_ref, nm_ref, nv_ref):
        g = g_ref[...]
        nm = ADAM_B1 * m_ref[...] + (1.0 - ADAM_B1) * g
        nv = ADAM_B2 * v_ref[...] + (1.0 - ADAM_B2) * (g * g)
        m_hat = nm / (1.0 - ADAM_B1 ** ADAM_STEP)
        v_hat = nv / (1.0 - ADAM_B2 ** ADAM_STEP)
        d_ref[...] = -ADAM_LR * (m_hat / (jnp.sqrt(v_hat) + ADAM_EPS) + ADAM_WD * w_ref[...])
        nm_ref[...] = nm
        nv_ref[...] = nv

    spec = pl.BlockSpec((tr, c), lambda i: (i, 0))
    return pl.pallas_call(
        body, name=name, grid=(r // tr,), in_specs=[spec] * 4, out_specs=[spec] * 3,
        out_shape=[jax.ShapeDtypeStruct((r, c), F32)] * 3, compiler_params=_cp("parallel"),
    )(w, g, m, v)


def _sum_parts(parts, *, name):
    _, r, c = parts.shape
    tr = _tile(r, 128, SUBLANES)

    def body(p_ref, o_ref):
        acc = p_ref[0].astype(F32)
        for k in range(1, N_DEV):
            acc = acc + p_ref[k].astype(F32)
        o_ref[...] = acc

    return pl.pallas_call(
        body, name=name, grid=(r // tr,), in_specs=[pl.BlockSpec((N_DEV, tr, c), lambda i: (0, i, 0))],
        out_specs=pl.BlockSpec((tr, c), lambda i: (i, 0)), out_shape=jax.ShapeDtypeStruct((r, c), F32),
        compiler_params=_cp("parallel"),
    )(parts)


def _position():
    return lax.axis_index("x"), lax.axis_index("y"), lax.axis_index("c")


def _flat(px, py, pc):
    return 4 * px + 2 * py + pc


def _all_gather(shard, *, name):
    def body(x_ref, out_ref, token, send_sems, recv_sems, local_sem):
        token[...] = jnp.zeros_like(token)
        x, y, c = _position()
        me, sibling = (x, y, c), (x, y, 1 - c)
        chips = [(1 - x, y), (x, 1 - y), (1 - x, 1 - y)]

        def copy(k, block, to, src=None):
            slot = out_ref.at[_flat(*block)]
            return pltpu.make_async_remote_copy(
                src_ref=slot if src is None else src, dst_ref=slot, send_sem=send_sems.at[k],
                recv_sem=recv_sems.at[k], device_id=to, device_id_type=MESH)

        mine = pltpu.make_async_copy(x_ref, out_ref.at[_flat(*me)], local_sem)
        mine.start()
        first = [copy(0, me, sibling, src=x_ref)]
        first += [copy(1 + j, me, (*chip, c), src=x_ref) for j, chip in enumerate(chips)]
        for cp in first:
            cp.start()
        passed = [copy(4 + j, (*chip, c), sibling) for j, chip in enumerate(chips)]
        for j, chip in enumerate(chips):
            copy(1 + j, (*chip, c), me).wait_recv()
            passed[j].start()
        copy(0, sibling, me).wait_recv()
        for j, chip in enumerate(chips):
            copy(4 + j, (*chip, 1 - c), me).wait_recv()
        for cp in first + passed:
            cp.wait_send()
        mine.wait()

    return pl.pallas_call(
        body, name=name,
        out_shape=(jax.ShapeDtypeStruct((N_DEV,) + shard.shape, shard.dtype),
                   jax.ShapeDtypeStruct((SUBLANES, LANES), F32)),
        in_specs=[pl.BlockSpec(memory_space=pl.ANY)],
        out_specs=(pl.BlockSpec(memory_space=pl.ANY), pl.BlockSpec(memory_space=pltpu.VMEM)),
        scratch_shapes=[pltpu.SemaphoreType.DMA((7,)), pltpu.SemaphoreType.DMA((7,)), pltpu.SemaphoreType.DMA(())],
    )(shard)


_HBM = pl.BlockSpec(memory_space=pltpu.HBM)
_SEM = pl.BlockSpec(memory_space=pltpu.SEMAPHORE)
_EFFECT = pltpu.SideEffectType.DATAFLOW_SIDE_EFFECTING


def _copy_ends(src_ref, land_ref, mode, me, to):
    if mode == "gather_slot":
        return src_ref, land_ref.at[me]
    if mode == "gather_cols":
        w = src_ref.shape[1]
        return src_ref, land_ref.at[:, pl.ds(pl.multiple_of(me * w, LANES), w)]
    if mode == "scatter_slot":
        return src_ref.at[to], land_ref.at[me]
    w = land_ref.shape[2]
    return src_ref.at[:, pl.ds(pl.multiple_of(to * w, LANES), w)], land_ref.at[me]


BF16_ROWS = 16


def _land_shape(src, mode):
    if mode == "gather_slot":
        return (N_DEV,) + src.shape
    if mode == "gather_cols":
        return (src.shape[0], N_DEV * src.shape[1])
    if mode == "scatter_slot":
        return src.shape
    return (N_DEV, src.shape[0], src.shape[1] // N_DEV)


OTHER_CHIPS = (2, 4, 6)


def _n_copies(mode):
    return {"gather_chip": 1 + len(OTHER_CHIPS), "forward": len(OTHER_CHIPS)}.get(mode, N_DEV - 1)


def _exchange_copies(src_ref, land_ref, send_sems, recv_sems, mode):
    x, y, c = _position()
    me = _flat(x, y, c)
    peer_of = lambda k: (x ^ ((k >> 2) & 1), y ^ ((k >> 1) & 1), c ^ (k & 1))
    if mode == "forward":
        slots = [land_ref.at[_flat(*peer_of(k))] for k in OTHER_CHIPS]
        plan = [(slot, slot, peer_of(1)) for slot in slots]
    elif mode == "gather_chip":
        plan = [(*_copy_ends(src_ref, land_ref, "gather_slot", me, _flat(*peer_of(k))), peer_of(k))
                for k in (1,) + OTHER_CHIPS]
    else:
        plan = [(*_copy_ends(src_ref, land_ref, mode, me, _flat(*peer_of(k))), peer_of(k)) for k in range(1, N_DEV)]
    return [pltpu.make_async_remote_copy(src_ref=src, dst_ref=dst, send_sem=send_sems.at[i], recv_sem=recv_sems.at[i],
                                         device_id=peer, device_id_type=MESH)
            for i, (src, dst, peer) in enumerate(plan)]


def _place_own(src, mode, dev, *, name):
    rows = src.shape[1] if mode == "scatter_slot" else src.shape[0]
    tr = _tile(rows, 512, BF16_ROWS)
    land = _land_shape(src, mode)
    width = land[-1] if mode.startswith("scatter") else src.shape[1]
    slot = pl.BlockSpec((1, tr, width), lambda i, d: (d[0], i, 0))
    cols = pl.BlockSpec((tr, width), lambda i, d: (i, d[0]))
    whole = pl.BlockSpec((tr, width), lambda i, d: (i, 0))
    in_spec, out_spec = {"gather_slot": (whole, slot), "gather_cols": (whole, cols), "scatter_slot": (slot, slot),
                         "scatter_cols": (cols, slot)}[mode]

    def body(dev_ref, src_ref, land_ref):
        land_ref[...] = src_ref[...].reshape(land_ref.shape)

    return pl.pallas_call(
        body, name=name, out_shape=jax.ShapeDtypeStruct(land, src.dtype),
        grid_spec=pltpu.PrefetchScalarGridSpec(num_scalar_prefetch=1, grid=(rows // tr,), in_specs=[in_spec],
                                               out_specs=out_spec),
        compiler_params=_cp("parallel"),
    )(dev, src)


def _exchange_start(src, mode, dev, *, name, land=None, after=None):
    if land is None:
        land = _place_own(src, "gather_slot" if mode == "gather_chip" else mode, dev, name=name + "_own")
    n_copies = _n_copies(mode)

    def body(*refs):
        src_ref, land_ref = refs[:2]
        send_sems, recv_sems, _, _, token = refs[-5:]
        for cp in _exchange_copies(src_ref, land_ref, send_sems, recv_sems, mode):
            cp.start()
        token[...] = jnp.zeros_like(token)

    hbm = pltpu.with_memory_space_constraint
    *handle, token = pl.pallas_call(
        body, name=name,
        out_shape=(pltpu.SemaphoreType.DMA((n_copies,)), pltpu.SemaphoreType.DMA((n_copies,)),
                   pltpu.HBM(src.shape, src.dtype), pltpu.HBM(land.shape, land.dtype),
                   jax.ShapeDtypeStruct((SUBLANES, LANES), F32)),
        in_specs=(_HBM, _HBM) + ((pl.BlockSpec(memory_space=pl.ANY),) if after is not None else ()),
        out_specs=(_SEM, _SEM, _HBM, _HBM, pl.BlockSpec(memory_space=pltpu.VMEM)),
        input_output_aliases={0: 2, 1: 3}, compiler_params=pltpu.CompilerParams(has_side_effects=_EFFECT),
    )(hbm(src, pltpu.HBM), hbm(land, pltpu.HBM), *(() if after is None else (after,)))
    return (tuple(handle), mode), token


def _exchange_wait(pending, after, *, name):
    (send_sems, recv_sems, src_thru, land_thru), mode = pending

    def body(src_ref, land_ref, send_sems, recv_sems, after_ref, src_dead, got_ref):
        for cp in _exchange_copies(src_ref, land_ref, send_sems, recv_sems, mode):
            cp.wait_send()
            cp.wait_recv()

    return pl.pallas_call(
        body, name=name, out_shape=(pltpu.HBM(src_thru.shape, src_thru.dtype), pltpu.HBM(land_thru.shape, land_thru.dtype)),
        in_specs=(_HBM, _HBM, _SEM, _SEM, pl.BlockSpec(memory_space=pl.ANY)), out_specs=(_HBM, _HBM),
        input_output_aliases={0: 0, 1: 1}, compiler_params=pltpu.CompilerParams(has_side_effects=_EFFECT),
    )(src_thru, land_thru, send_sems, recv_sems, after)[1]


def _after(x, token):
    return x + token[0, 0].astype(x.dtype)


def _touch(*arrays, name):
    def body(*refs):
        refs[-1][...] = jnp.zeros_like(refs[-1])

    return pl.pallas_call(
        body, name=name, out_shape=jax.ShapeDtypeStruct((SUBLANES, LANES), F32),
        in_specs=[pl.BlockSpec(memory_space=pl.ANY)] * len(arrays), out_specs=pl.BlockSpec(memory_space=pltpu.VMEM),
    )(*arrays)


def _pad_cols(a, mult):
    pad = -a.shape[1] % mult
    return jnp.pad(a, ((0, 0), (0, pad))) if pad else a


def _pack(arrs, cols):
    flat = jnp.concatenate([a.reshape(-1).astype(F32) for a in arrs])
    sizes = [int(a.size) for a in arrs]
    flat = jnp.pad(flat, (0, -flat.shape[0] % (N_DEV * SUBLANES * cols)))
    return flat.reshape(-1, cols), sizes


def _unpack(flat2d, sizes, shapes):
    flat = flat2d.reshape(-1)
    out, o = [], 0
    for n, s in zip(sizes, shapes):
        out.append(flat[o:o + n].reshape(s))
        o += n
    return out


PACK_COLS = SUBLANES * LANES


def kernel(x, norm_mix_w, w_in, conv_a_w, conv_a_b, dt_bias, a_log, d_a, norm_a_w, w_proj_a, s5_lam_re, s5_lam_im, s5_log_dt, s5_b_re, s5_b_im, s5_c_re, s5_c_im, s5_d, w_s5_glu, w_out, norm_ffn_w, w_up, conv_ffn_w, conv_ffn_b, w_down, norm_final_w, loss_target, m_norm_mix_w, m_w_in, m_conv_a_w, m_conv_a_b, m_dt_bias, m_a_log, m_d_a, m_norm_a_w, m_w_proj_a, m_s5_lam_re, m_s5_lam_im, m_s5_log_dt, m_s5_b_re, m_s5_b_im, m_s5_c_re, m_s5_c_im, m_s5_d, m_w_s5_glu, m_w_out, m_norm_ffn_w, m_w_up, m_conv_ffn_w, m_conv_ffn_b, m_w_down, m_norm_final_w, v_norm_mix_w, v_w_in, v_conv_a_w, v_conv_a_b, v_dt_bias, v_a_log, v_d_a, v_norm_a_w, v_w_proj_a, v_s5_lam_re, v_s5_lam_im, v_s5_log_dt, v_s5_b_re, v_s5_b_im, v_s5_c_re, v_s5_c_im, v_s5_d, v_w_s5_glu, v_w_out, v_norm_ffn_w, v_w_up, v_conv_ffn_w, v_conv_ffn_b, v_w_down, v_norm_final_w):
    weights = dict(norm_mix_w=norm_mix_w, w_in=w_in, conv_a_w=conv_a_w, conv_a_b=conv_a_b, dt_bias=dt_bias, a_log=a_log, d_a=d_a, norm_a_w=norm_a_w, w_proj_a=w_proj_a, s5_lam_re=s5_lam_re, s5_lam_im=s5_lam_im, s5_log_dt=s5_log_dt, s5_b_re=s5_b_re, s5_b_im=s5_b_im, s5_c_re=s5_c_re, s5_c_im=s5_c_im, s5_d=s5_d, w_s5_glu=w_s5_glu, w_out=w_out, norm_ffn_w=norm_ffn_w, w_up=w_up, conv_ffn_w=conv_ffn_w, conv_ffn_b=conv_ffn_b, w_down=w_down, norm_final_w=norm_final_w)
    moms = dict(norm_mix_w=m_norm_mix_w, w_in=m_w_in, conv_a_w=m_conv_a_w, conv_a_b=m_conv_a_b, dt_bias=m_dt_bias, a_log=m_a_log, d_a=m_d_a, norm_a_w=m_norm_a_w, w_proj_a=m_w_proj_a, s5_lam_re=m_s5_lam_re, s5_lam_im=m_s5_lam_im, s5_log_dt=m_s5_log_dt, s5_b_re=m_s5_b_re, s5_b_im=m_s5_b_im, s5_c_re=m_s5_c_re, s5_c_im=m_s5_c_im, s5_d=m_s5_d, w_s5_glu=m_w_s5_glu, w_out=m_w_out, norm_ffn_w=m_norm_ffn_w, w_up=m_w_up, conv_ffn_w=m_conv_ffn_w, conv_ffn_b=m_conv_ffn_b, w_down=m_w_down, norm_final_w=m_norm_final_w)
    vars_ = dict(norm_mix_w=v_norm_mix_w, w_in=v_w_in, conv_a_w=v_conv_a_w, conv_a_b=v_conv_a_b, dt_bias=v_dt_bias, a_log=v_a_log, d_a=v_d_a, norm_a_w=v_norm_a_w, w_proj_a=v_w_proj_a, s5_lam_re=v_s5_lam_re, s5_lam_im=v_s5_lam_im, s5_log_dt=v_s5_log_dt, s5_b_re=v_s5_b_re, s5_b_im=v_s5_b_im, s5_c_re=v_s5_c_re, s5_c_im=v_s5_c_im, s5_d=v_s5_d, w_s5_glu=v_w_s5_glu, w_out=v_w_out, norm_ffn_w=v_norm_ffn_w, w_up=v_w_up, conv_ffn_w=v_conv_ffn_w, conv_ffn_b=v_conv_ffn_b, w_down=v_w_down, norm_final_w=v_norm_final_w)
    names = list(weights)
    col_sharded = ("w_in", "w_s5_glu", "w_up")
    row_sharded = ("w_proj_a", "w_out", "w_down")
    conv_sharded = ("conv_a_w", "conv_ffn_w")
    replicated = [n for n in names if n not in col_sharded + row_sharded + conv_sharded]

    t, d = x.shape[1:]
    x2, tgt = x.reshape(t, d), loss_target.reshape(t, d)
    nh = dt_bias.shape[-1]
    d_inner = norm_a_w.shape[-1]
    conv_dim = conv_a_b.shape[-1]
    gn = (conv_dim - d_inner) // 2
    ng = gn // D_STATE
    nr = nh // ng
    rp = d_inner // ng
    d_s5 = s5_d.shape[-1]
    gs, ps = s5_lam_re.shape[1:]
    cs = d_s5 // gs
    n_oct = gs // 8
    assert (gs * ps) % SCAN_COLS == 0 and 8 * cs == LANES and gs % 8 == 0
    d_ff = w_down.shape[1] * N_DEV
    dev = _flat(*_position())
    dev1 = dev.reshape(1).astype(jnp.int32)

    ka, kf = conv_a_w.shape[1], conv_ffn_w.shape[1]
    taps = jnp.concatenate([conv_a_w[0].reshape(1, -1), conv_ffn_w[0].reshape(1, -1)], axis=1)
    taps, taps_done = _all_gather(taps, name="ag_conv_taps")
    taps = taps[:, 0]

    def by_cols(n):
        return n in ("w_s5_glu", "w_up") and weights[n].shape[2] % LANES == 0

    chip_stage, t1 = _exchange_start(_after(w_in[0], taps_done).astype(BF16), "gather_chip", dev1, name="ag_w_in")
    w1 = norm_mix_w.reshape(1, 1, d) + t1[0, 0]
    hn1, = _blocked_fwd(_rms_fn, [x2], [w1], [(d, BF16)], tb=256, name="rms1")
    others = ("w_proj_a", "w_s5_glu", "w_out", "w_up", "w_down")
    gather_mode = {n: "gather_cols" if by_cols(n) else "gather_slot" for n in others}
    shards = {n: _after(weights[n][0], t1).astype(BF16) for n in others}
    lands = {n: _place_own(shards[n], gather_mode[n], dev1, name="ag_" + n + "_own") for n in others}
    w_in_2d = [a.reshape(a.shape[-2:]) for a in (w_in, m_w_in, v_w_in)]

    lam_re3, lam_im3 = s5_lam_re[0][:, None, :], s5_lam_im[0][:, None, :]
    logdt3 = _after(s5_log_dt[0][:, None, None], t1)
    bt_re, bt_im = jnp.transpose(s5_b_re[0], (0, 2, 1)), jnp.transpose(s5_b_im[0], (0, 2, 1))
    lb_re3, lb_im3, bb_re, bb_im = _s5_params(lam_re3, lam_im3, logdt3, bt_re, bt_im, name="s5_params")
    eye = jnp.eye(8, dtype=F32)

    def diag_b(bt):
        return (bt.reshape(n_oct, 8, cs, 1, ps) * eye[None, :, None, :, None]).reshape(n_oct, 8 * cs, 8 * ps)

    def undiag_b(blk):
        return (blk.reshape(n_oct, 8, cs, 8, ps) * eye[None, :, None, :, None]).sum(axis=3).reshape(gs, cs, ps)

    def diag_c(cm):
        ct = jnp.transpose(cm.reshape(n_oct, 8, cs, ps), (0, 1, 3, 2))
        return (ct[:, :, :, None, :] * eye[None, :, None, :, None]).reshape(n_oct, 8 * ps, 8 * cs)

    def undiag_c(blk):
        ct = (blk.reshape(n_oct, 8, ps, 8, cs) * eye[None, :, None, :, None]).sum(axis=3)
        return jnp.transpose(ct, (0, 1, 3, 2)).reshape(gs, cs, ps)

    b_blk_re, b_blk_im = diag_b(bb_re), diag_b(bb_im)
    c_blk_re, c_blk_imn = diag_c(s5_c_re[0]), diag_c(-s5_c_im[0])
    d3 = s5_d.reshape(n_oct, 1, LANES)
    lb_re, lb_im = lb_re3.reshape(1, gs * ps), lb_im3.reshape(1, gs * ps)

    ready = _touch(hn1, *lands.values(), *w_in_2d, b_blk_re, b_blk_im, c_blk_re, c_blk_imn, lb_re, lb_im,
                   name="ready_startup")
    forward_stage, t2 = _exchange_start(ready, "forward", dev1, land=_exchange_wait(chip_stage, ready, name="agw_w_in"),
                                        name="fw_w_in")
    w_in_blocks = _exchange_wait(forward_stage, t2, name="fww_w_in")
    pending, started = {}, t2
    for n in others:
        pending[n], token = _exchange_start(shards[n], gather_mode[n], dev1, land=lands[n], after=w_in_blocks,
                                            name="ag_" + n)
        started = started + token

    def gathered(n, after):
        g = _exchange_wait(pending[n], after, name="agw_" + n)
        if by_cols(n):
            return g
        if n in row_sharded:
            return g.reshape(-1, g.shape[2])
        return jnp.transpose(g, (1, 0, 2)).reshape(g.shape[1], -1)

    seg_sizes = dict(z=d_inner, xs=d_inner, bm=gn, cm=gn, dt=nh, u=d_s5, ga=d, gb=d)
    seg_names = tuple(seg_sizes)
    pieces, seg_at = _w_in_pieces(seg_sizes, ("z", "xs", "ga", "gb", "bm", "cm", "u", "dt"), w_in.shape[2])
    na = ka * conv_a_w.shape[2]
    cw_a = jnp.transpose(taps[:, :na].reshape(N_DEV, ka, -1), (1, 0, 2)).reshape(ka, conv_dim)
    cw_f = jnp.transpose(taps[:, na:].reshape(N_DEV, kf, -1), (1, 0, 2)).reshape(kf, 2 * d_ff)
    cb_a, cb_f = conv_a_b, conv_ffn_b
    a_cols = {"xs": slice(0, d_inner), "bm": slice(d_inner, d_inner + gn), "cm": slice(d_inner + gn, conv_dim)}

    w_in_p = _w_in_pack(w_in_blocks, pieces, seg_at, seg_sizes, started, name="w_in_pack")
    pre = {sn: _mm(hn1, w_in_p, b_win=seg_at[sn], name="in_" + sn) for sn in seg_names}
    act_a = {sn: _conv_fwd(_comb_silu, [pre[sn]], [cw_a[:, a_cols[sn]]], [cb_a[:, a_cols[sn]]], out_dtype=F32,
                           name="conv_a_" + sn) for sn in a_cols}
    dtr3 = jnp.transpose(pre["dt"][:, :nh].reshape(t, ng, nr), (1, 0, 2))
    dtb3, alog3, dsk3 = (p.reshape(ng, 1, nr) for p in (dt_bias, a_log, d_a))
    nw3 = norm_a_w.reshape(ng, 1, rp)
    yn, hsave = _ssd_fwd(act_a["xs"], act_a["bm"], act_a["cm"], pre["z"], dtr3, dtb3, alog3, dsk3, nw3, name="ssd")
    w_proj = gathered("w_proj_a", yn)
    y_a = _mm(yn, w_proj, name="proj_a")

    u = pre["u"]
    bu_re, bu_im = _blocked_fwd(_s5_bu_fn, [u], [b_blk_re, b_blk_im], [(gs * ps, F32)] * 2, nj=n_oct, tb=1024,
                                name="s5_bu")
    s_re, s_im = _s5_scan_fwd(bu_re, bu_im, lb_re, lb_im, name="s5_scan")
    yb, = _blocked_fwd(_s5_out_fn, [s_re, s_im, u], [c_blk_re, c_blk_imn, d3], [(d_s5, BF16)], nj=n_oct, tb=1024,
                       name="s5_out")
    w_glu = gathered("w_s5_glu", yb)
    glu_v = _mm(yb, w_glu, b_win=(0, d), name="glu_v")
    glu_g = _mm(yb, w_glu, b_win=(d, d), name="glu_g")
    merged, = _blocked_fwd(_merge_fn, [glu_v, glu_g, pre["ga"], pre["gb"], y_a], [], [(d, BF16)], tb=256,
                           name="merge")
    w_o = gathered("w_out", merged)
    h1 = _mm(merged, w_o, acc=x2, name="out_proj")
    w2 = norm_ffn_w.reshape(1, 1, d)
    hn2, = _blocked_fwd(_rms_fn, [h1], [w2], [(d, BF16)], tb=256, name="rms2")
    w_u = gathered("w_up", hn2)
    up_g = _mm(hn2, w_u, b_win=(0, d_ff), name="up_g")
    up_v = _mm(hn2, w_u, b_win=(d_ff, d_ff), name="up_v")
    f_w = [cw_f[:, :d_ff], cw_f[:, d_ff:]]
    f_b = [cb_f[:, :d_ff], cb_f[:, d_ff:]]
    act = _conv_fwd(_comb_glu, [up_g, up_v], f_w, f_b, out_dtype=BF16, name="conv_ffn")
    w_dn = gathered("w_down", act)
    h2 = _mm(act, w_dn, acc=h1, name="down")
    loss_tile, dh2, dh2_b, g_final = _loss_head(h2, norm_final_w.reshape(1, d), tgt, name="loss_head")

    grads, scattering = {}, {}

    def scatter_start(n, g):
        if by_cols(n):
            src, mode = g, "scatter_cols"
        elif n in row_sharded:
            src, mode = g.reshape(N_DEV, -1, g.shape[1]), "scatter_slot"
        elif n == "w_in":
            src, mode = g, "scatter_slot"
        else:
            src, mode = jnp.transpose(g.reshape(g.shape[0], N_DEV, -1), (1, 0, 2)), "scatter_slot"
        scattering[n], token = _exchange_start(src, mode, dev1, name="rs_" + n)
        return token

    d_act = _mm(dh2_b, w_dn, tb=True, name="d_act")
    g_down = _mm(act, dh2_b, ta=True, out_dtype=BF16, name="g_w_down")
    tok = scatter_start("w_down", g_down)
    (dup_g, dwf_g, dbf_g), (dup_v, dwf_v, dbf_v) = _conv_bwd(
        _comb_glu, [up_g, up_v], f_w, [_after(f_b[0], tok), f_b[1]], d_act, dx_dtype=BF16, name="conv_ffn_bwd")
    dhn2 = _mm(dup_g, w_u, tb=True, b_win=(0, d_ff), name="d_hn2_g")
    dhn2 = _mm(dup_v, w_u, tb=True, b_win=(d_ff, d_ff), acc=dhn2, name="d_hn2_v")
    g_up = _mm(hn2, dup_g, ta=True, into=(lax.empty((d, 2 * d_ff), BF16), 0), name="g_w_up_g")
    g_up = _mm(hn2, dup_v, ta=True, into=(g_up, d_ff), name="g_w_up_v")
    tok = scatter_start("w_up", g_up)
    dh1, dh1_b, g_w2 = _blocked_bwd(_rms_fn, [h1], [_after(w2, tok)], [dhn2], [(F32, BF16)], adds={0: dh2}, tb=256,
                                    name="rms2_bwd")
    d_merged = _mm(dh1_b, w_o, tb=True, name="d_merged")
    g_out = _mm(merged, dh1_b, ta=True, out_dtype=BF16, name="g_w_out")
    tok = scatter_start("w_out", g_out)
    dglu_v, dglu_g, dga, dgb, dy_a = _blocked_bwd(
        _merge_fn, [glu_v, glu_g, pre["ga"], pre["gb"], y_a], [], [d_merged], [BF16] * 5, tb=128, name="merge_bwd")
    dyb = _mm(dglu_v, w_glu, tb=True, b_win=(0, d), name="d_yb_v")
    dyb = _mm(dglu_g, w_glu, tb=True, b_win=(d, d), acc=dyb, name="d_yb_g")
    g_glu = _mm(yb, dglu_v, ta=True, into=(lax.empty((d_s5, 2 * d), BF16), 0), name="g_w_glu_v")
    g_glu = _mm(yb, dglu_g, ta=True, into=(g_glu, d), name="g_w_glu_g")
    tok = tok + scatter_start("w_s5_glu", g_glu)
    ds_re, ds_im, du_skip, dc_blk_re, dc_blk_imn, dd3 = _blocked_bwd(
        _s5_out_fn, [s_re, s_im, u], [c_blk_re, c_blk_imn, _after(d3, tok)], [dyb], [F32, F32, F32], nj=n_oct, tb=1024,
        name="s5_out_bwd")
    dbu_re, dbu_im, dlb_re, dlb_im = _s5_scan_bwd(s_re, s_im, ds_re, ds_im, lb_re, lb_im, name="s5_scan_bwd")
    du, db_blk_re, db_blk_im = _blocked_bwd(
        _s5_bu_fn, [u], [b_blk_re, b_blk_im], [dbu_re, dbu_im], [BF16], adds={0: du_skip}, nj=n_oct, tb=1024,
        name="s5_bu_bwd")
    g_lre, g_lim, g_ldt, g_bt_re, g_bt_im = _s5_params(
        lam_re3, lam_im3, logdt3, bt_re, bt_im,
        cts=(dlb_re.reshape(gs, 1, ps), dlb_im.reshape(gs, 1, ps), undiag_b(db_blk_re), undiag_b(db_blk_im)),
        name="s5_params_bwd")
    grads["s5_lam_re"], grads["s5_lam_im"] = g_lre.reshape(s5_lam_re.shape), g_lim.reshape(s5_lam_im.shape)
    grads["s5_log_dt"] = g_ldt.reshape(s5_log_dt.shape)
    grads["s5_b_re"] = jnp.transpose(g_bt_re, (0, 2, 1)).reshape(s5_b_re.shape)
    grads["s5_b_im"] = jnp.transpose(g_bt_im, (0, 2, 1)).reshape(s5_b_im.shape)
    grads["s5_c_re"] = undiag_c(dc_blk_re).reshape(s5_c_re.shape)
    grads["s5_c_im"] = -undiag_c(dc_blk_imn).reshape(s5_c_im.shape)
    grads["s5_d"] = dd3.reshape(s5_d.shape)

    dyn = _mm(dy_a, w_proj, tb=True, name="d_yn")
    g_proj = _mm(yn, dy_a, ta=True, out_dtype=BF16, name="g_w_proj_a")
    tok = scatter_start("w_proj_a", g_proj)
    dxs, dbm, dcm, dz, ddtr3, g_dtb, g_alog, g_dsk, g_nw = _ssd_bwd(
        act_a["xs"], act_a["bm"], act_a["cm"], pre["z"], dtr3, hsave, dtb3, alog3, dsk3, _after(nw3, tok), dyn,
        name="ssd_bwd")
    grads["dt_bias"], grads["a_log"], grads["d_a"] = (g.reshape(1, nh) for g in (g_dtb, g_alog, g_dsk))
    grads["norm_a_w"] = g_nw.reshape(1, d_inner)
    dpre = {"z": dz, "u": du, "ga": dga, "gb": dgb}
    dcw, dcb = {}, {}
    for sn, dact in (("xs", dxs), ("bm", dbm), ("cm", dcm)):
        (dpre[sn], dcw[sn], dcb[sn]), = _conv_bwd(
            _comb_silu, [pre[sn]], [cw_a[:, a_cols[sn]]], [cb_a[:, a_cols[sn]]], dact, dx_dtype=BF16,
            name="conv_a_bwd_" + sn)
    dpre["dt"] = _pad_cols(jnp.transpose(ddtr3, (1, 0, 2)).reshape(t, nh), LANES).astype(BF16)
    g_in = _w_in_unpack({sn: _mm(hn1, dpre[sn], ta=True, name="g_w_in_" + sn) for sn in seg_names}, pieces,
                        w_in.shape[2], name="w_in_unpack")
    tok = scatter_start("w_in", g_in)
    dhn1 = _mm(_after(dpre["dt"], tok), w_in_p, tb=True, b_win=seg_at["dt"], name="d_hn1_dt")
    for sn in seg_names:
        if sn != "dt":
            dhn1 = _mm(dpre[sn], w_in_p, tb=True, b_win=seg_at[sn], acc=dhn1, name="d_hn1_" + sn)
    dx, g_w1 = _blocked_bwd(_rms_fn, [x2], [w1], [dhn1], [F32], adds={0: dh1}, tb=256, name="rms1_bwd")

    grads["norm_mix_w"], grads["norm_ffn_w"] = g_w1.reshape(1, d), g_w2.reshape(1, d)
    grads["norm_final_w"] = g_final.reshape(d)
    grads["conv_a_b"] = jnp.concatenate([dcb["xs"], dcb["bm"], dcb["cm"]], axis=1)
    grads["conv_ffn_b"] = jnp.concatenate([dbf_g, dbf_v], axis=1)
    g_cw_a = jnp.concatenate([dcw["xs"], dcw["bm"], dcw["cm"]], axis=1)
    g_cw_f = jnp.concatenate([dwf_g, dwf_v], axis=1)

    small = [grads[n] for n in replicated] + [g_cw_a, g_cw_f, loss_tile[:1, :1]]
    packed, sizes = _pack(small, PACK_COLS)
    small_scatter, _ = _exchange_start(packed.reshape(N_DEV, -1, PACK_COLS), "scatter_slot", dev1, name="rs_small")
    eighth = _sum_parts(_exchange_wait(small_scatter, packed, name="rsw_small"), name="sum_small_grads")
    summed = _all_gather(eighth, name="ag_small_grads")[0].reshape(-1, PACK_COLS)
    *rep_sums, s_cw_a, s_cw_f, loss = _unpack(summed, sizes, [a.shape for a in small])
    for n, g in zip(replicated, rep_sums):
        grads[n] = g
    wa, wf = conv_a_w.shape[2], conv_ffn_w.shape[2]
    grads["conv_a_w"] = lax.dynamic_slice_in_dim(s_cw_a, dev * wa, wa, axis=1)[None]
    grads["conv_ffn_w"] = lax.dynamic_slice_in_dim(s_cw_f, dev * wf, wf, axis=1)[None]

    delta, new_m, new_v = {}, {}, {}
    done = dx
    for n in ("w_down", "w_up", "w_out", "w_s5_glu", "w_proj_a", "w_in"):
        shape = weights[n].shape
        two_d = lambda a: a.reshape(shape[-2], shape[-1])
        w2, m2, v2 = two_d(weights[n]), two_d(moms[n]), two_d(vars_[n])
        ready = _touch(done, w2, m2, v2, *((packed,) if n == "w_in" else ()), name="ready_" + n)
        land = _exchange_wait(scattering[n], ready, name="rsw_" + n)
        g = _sum_parts(land, name="rs_sum_" + n)
        grads[n] = g.reshape(shape)
        dl, nm, nv = _adamw(w2, g, m2, v2, name="adamw_" + n)
        delta[n], new_m[n], new_v[n] = dl.reshape(shape), nm.reshape(shape), nv.reshape(shape)
        done = dl
    for n in replicated + list(conv_sharded):
        shape = weights[n].shape
        two_d = lambda a: a.reshape(-1, shape[-1])
        dl, nm, nv = _adamw(two_d(weights[n]), two_d(grads[n]), two_d(moms[n]), two_d(vars_[n]), name="adamw_" + n)
        delta[n], new_m[n], new_v[n] = dl.reshape(shape), nm.reshape(shape), nv.reshape(shape)

    return (loss.reshape(()), dx.reshape(x.shape), *[grads[n] for n in names], *[delta[n] for n in names],
            *[new_m[n] for n in names], *[new_v[n] for n in names])
```
